```python
import math
import jax, jax.numpy as jnp
from jax import lax
import numpy as np

D_MODEL = 2048
BATCH = 8
SEQ = 8192
DEPTH = 1

D_CONV = 1024
CONV_GROUPS = 16
CONV_WIDTH = 3
N_HEADS = 8
QK_NOPE = 128
QK_ROPE = 64
QK_HEAD = QK_NOPE + QK_ROPE
V_HEAD = 128
D_ATTN = N_HEADS * V_HEAD
Q_LORA = 512
KV_LORA = 256
ROPE_BASE = 10000.0
Q_BLOCK = 128
D_MIX = D_CONV + D_ATTN
IN_COLS = 4 * D_CONV + Q_LORA + KV_LORA + QK_ROPE + D_ATTN
EPS = 1e-6

kernel_name = "hymba_conv_mla_adaln_layer"


def _rmsnorm(x, g):
    x32 = x.astype(jnp.float32)
    y = x32 * lax.rsqrt(jnp.mean(x32 * x32, axis=-1, keepdims=True) + EPS)
    return (y * g.astype(jnp.float32)).astype(x.dtype)


def _rope_tables(positions):
    inv_freq = ROPE_BASE ** (-jnp.arange(0, QK_ROPE, 2, dtype=jnp.float32) / QK_ROPE)
    ang = positions.astype(jnp.float32)[..., None] * inv_freq
    return jnp.cos(ang), jnp.sin(ang)


def _apply_rope(x, cos, sin):
    half = QK_ROPE // 2
    x32 = x.astype(jnp.float32)
    x1, x2 = x32[..., :half], x32[..., half:]
    out = jnp.concatenate([x1 * cos - x2 * sin, x1 * sin + x2 * cos], axis=-1)
    return out.astype(x.dtype)


def _short_conv_branch(x_c, b_c, c_c, z_c, conv_w):
    u = c_c * x_c
    seq = u.shape[1]
    u_pad = jnp.pad(u, ((0, 0), (CONV_WIDTH - 1, 0), (0, 0)))
    conv = sum(conv_w[k] * u_pad[:, k:k + seq, :] for k in range(CONV_WIDTH))
    y = b_c * conv
    return y * jax.nn.silu(z_c)


def _causal_blocked_attention(q, k, v):
    bsz, seq = q.shape[0], q.shape[1]
    n_blk = seq // Q_BLOCK
    scale = 1.0 / math.sqrt(QK_HEAD)
    q_blocks = q.reshape(bsz, n_blk, Q_BLOCK, N_HEADS, QK_HEAD).transpose(1, 0, 2, 3, 4)
    key_idx = jnp.arange(seq, dtype=jnp.int32)

    def one_block(args):
        qb, blk = args
        q_idx = blk * Q_BLOCK + jnp.arange(Q_BLOCK, dtype=jnp.int32)
        s = jnp.einsum('bqhd,bkhd->bhqk', qb, k).astype(jnp.float32) * scale
        mask = key_idx[None, :] <= q_idx[:, None]
        s = jnp.where(mask[None, None], s, -jnp.inf)
        p = jax.nn.softmax(s, axis=-1).astype(v.dtype)
        return jnp.einsum('bhqk,bkhd->bqhd', p, v)

    out = lax.map(one_block, (q_blocks, jnp.arange(n_blk, dtype=jnp.int32)))
    return out.transpose(1, 0, 2, 3, 4).reshape(bsz, seq, N_HEADS, V_HEAD)


def _mla_branch(c_q, c_kv, k_rope, z_a, cos, sin, q_a_g, w_q_b, kv_a_g, w_kv_b, q_g, k_g):
    bsz, seq = c_q.shape[0], c_q.shape[1]
    q = (_rmsnorm(c_q, q_a_g) @ w_q_b).reshape(bsz, seq, N_HEADS, QK_HEAD)
    kv = (_rmsnorm(c_kv, kv_a_g) @ w_kv_b).reshape(bsz, seq, N_HEADS, QK_NOPE + V_HEAD)
    k_nope, v = kv[..., :QK_NOPE], kv[..., QK_NOPE:]
    k = jnp.concatenate([k_nope, jnp.broadcast_to(k_rope[:, :, None, :], (bsz, seq, N_HEADS, QK_ROPE))], axis=-1)
    q = _rmsnorm(q, q_g)
    k = _rmsnorm(k, k_g)
    cos_h, sin_h = cos[:, :, None, :], sin[:, :, None, :]
    q = jnp.concatenate([q[..., :QK_NOPE], _apply_rope(q[..., QK_NOPE:], cos_h, sin_h)], axis=-1)
    k = jnp.concatenate([k[..., :QK_NOPE], _apply_rope(k[..., QK_NOPE:], cos_h, sin_h)], axis=-1)
    o = _causal_blocked_attention(q, k, v).reshape(bsz, seq, D_ATTN)
    return o * jax.nn.silu(z_a)


def _layer(x, c, cos, sin, ada_w, ada_b, norm_g, w_in, conv_w, q_a_g, w_q_b, kv_a_g, w_kv_b, q_g, k_g, w_out):
    mod = jax.nn.silu(c) @ ada_w + ada_b
    shift, scale, gate = jnp.split(mod, 3, axis=-1)
    h = _rmsnorm(x, norm_g) * (1.0 + scale[:, None, :]) + shift[:, None, :]
    u = h @ w_in
    splits = np.cumsum([D_CONV, D_CONV, D_CONV, D_CONV, Q_LORA, KV_LORA, QK_ROPE])
    x_c, b_c, c_c, z_c, c_q, c_kv, k_rope, z_a = jnp.split(u, splits.tolist(), axis=-1)
    y_conv = _short_conv_branch(x_c, b_c, c_c, z_c, conv_w)
    y_attn = _mla_branch(c_q, c_kv, k_rope, z_a, cos, sin, q_a_g, w_q_b, kv_a_g, w_kv_b, q_g, k_g)
    y = jnp.concatenate([y_conv, y_attn], axis=-1) @ w_out
    return x + gate[:, None, :] * y


def _fwd_setup_inputs(seed: int = 0) -> dict:
    key = jax.random.key(seed)
    ks = jax.random.split(key, 20)
    f32 = jnp.float32

    def nrm(k, shape, fan_in, mult=1.0):
        return jax.random.normal(k, shape, f32) * (mult * fan_in ** -0.5)

    def gain(k, shape):
        return 1.0 + 0.02 * jax.random.normal(k, shape, f32)

    x = jax.random.normal(ks[0], (BATCH, SEQ, D_MODEL), f32)
    c = jax.random.normal(ks[1], (BATCH, D_MODEL), f32)
    positions = jnp.broadcast_to(jnp.arange(SEQ, dtype=jnp.int32), (BATCH, SEQ))
    return {
        "x": x,
        "c": c,
        "positions": positions,
        "ada_w": nrm(ks[2], (DEPTH, D_MODEL, 3 * D_MODEL), D_MODEL, 0.5),
        "ada_b": 0.01 * jax.random.normal(ks[3], (DEPTH, 3 * D_MODEL), f32),
        "norm_g": gain(ks[4], (DEPTH, D_MODEL)),
        "w_in": nrm(ks[5], (DEPTH, D_MODEL, IN_COLS), D_MODEL),
        "conv_w": nrm(ks[6], (DEPTH, CONV_WIDTH, D_CONV), CONV_WIDTH),
        "q_a_g": gain(ks[7], (DEPTH, Q_LORA)),
        "w_q_b": nrm(ks[8], (DEPTH, Q_LORA, N_HEADS * QK_HEAD), Q_LORA),
        "kv_a_g": gain(ks[9], (DEPTH, KV_LORA)),
        "w_kv_b": nrm(ks[10], (DEPTH, KV_LORA, N_HEADS * (QK_NOPE + V_HEAD)), KV_LORA),
        "q_g": gain(ks[11], (DEPTH, QK_HEAD)),
        "k_g": gain(ks[12], (DEPTH, QK_HEAD)),
        "w_out": nrm(ks[13], (DEPTH, D_MIX, D_MODEL), D_MIX),
    }


def _fwd_reference(x, c, positions, ada_w, ada_b, norm_g, w_in, conv_w, q_a_g, w_q_b, kv_a_g, w_kv_b, q_g, k_g, w_out):
    cos, sin = _rope_tables(positions)
    for l in range(DEPTH):
        x = _layer(x, c, cos, sin, ada_w[l], ada_b[l], norm_g[l], w_in[l], conv_w[l],
                   q_a_g[l], w_q_b[l], kv_a_g[l], w_kv_b[l], q_g[l], k_g[l], w_out[l])
    return x


import jax as _jax
import jax.numpy as _jnp

TWIN_FORMAT = 'train_step'
FWD_PARAMS = ['x', 'c', 'positions', 'ada_w', 'ada_b', 'norm_g', 'w_in', 'conv_w', 'q_a_g', 'w_q_b', 'kv_a_g', 'w_kv_b', 'q_g', 'k_g', 'w_out']
TWIN_WEIGHTS = ['ada_w', 'ada_b', 'norm_g', 'w_in', 'conv_w', 'q_a_g', 'w_q_b', 'kv_a_g', 'w_kv_b', 'q_g', 'k_g', 'w_out']
TWIN_DIFF_INPUT = 'x'
TWIN_INPUTS = ['x', 'c', 'positions', 'ada_w', 'ada_b', 'norm_g', 'w_in', 'conv_w', 'q_a_g', 'w_q_b', 'kv_a_g', 'w_kv_b', 'q_g', 'k_g', 'w_out', 'loss_target', 'm_ada_w', 'm_ada_b', 'm_norm_g', 'm_w_in', 'm_conv_w', 'm_q_a_g', 'm_w_q_b', 'm_kv_a_g', 'm_w_kv_b', 'm_q_g', 'm_k_g', 'm_w_out', 'v_ada_w', 'v_ada_b', 'v_norm_g', 'v_w_in', 'v_conv_w', 'v_q_a_g', 'v_w_q_b', 'v_kv_a_g', 'v_w_kv_b', 'v_q_g', 'v_k_g', 'v_w_out']
TWIN_OUTPUTS = ['loss', 'grad_x', 'grad_ada_w', 'grad_ada_b', 'grad_norm_g', 'grad_w_in', 'grad_conv_w', 'grad_q_a_g', 'grad_w_q_b', 'grad_kv_a_g', 'grad_w_kv_b', 'grad_q_g', 'grad_k_g', 'grad_w_out', 'delta_ada_w', 'delta_ada_b', 'delta_norm_g', 'delta_w_in', 'delta_conv_w', 'delta_q_a_g', 'delta_w_q_b', 'delta_kv_a_g', 'delta_w_kv_b', 'delta_q_g', 'delta_k_g', 'delta_w_out', 'new_m_ada_w', 'new_m_ada_b', 'new_m_norm_g', 'new_m_w_in', 'new_m_conv_w', 'new_m_q_a_g', 'new_m_w_q_b', 'new_m_kv_a_g', 'new_m_w_kv_b', 'new_m_q_g', 'new_m_k_g', 'new_m_w_out', 'new_v_ada_w', 'new_v_ada_b', 'new_v_norm_g', 'new_v_w_in', 'new_v_conv_w', 'new_v_q_a_g', 'new_v_w_q_b', 'new_v_kv_a_g', 'new_v_w_kv_b', 'new_v_q_g', 'new_v_k_g', 'new_v_w_out']
TWIN_LEAF_KINDS = {'loss': 'loss', 'grad_x': 'grad_x', 'grad_ada_w': 'grad_w', 'grad_ada_b': 'grad_w', 'grad_norm_g': 'grad_w', 'grad_w_in': 'grad_w', 'grad_conv_w': 'grad_w', 'grad_q_a_g': 'grad_w', 'grad_w_q_b': 'grad_w', 'grad_kv_a_g': 'grad_w', 'grad_w_kv_b': 'grad_w', 'grad_q_g': 'grad_w', 'grad_k_g': 'grad_w', 'grad_w_out': 'grad_w', 'delta_ada_w': 'delta_w', 'delta_ada_b': 'delta_w', 'delta_norm_g': 'delta_w', 'delta_w_in': 'delta_w', 'delta_conv_w': 'delta_w', 'delta_q_a_g': 'delta_w', 'delta_w_q_b': 'delta_w', 'delta_kv_a_g': 'delta_w', 'delta_w_kv_b': 'delta_w', 'delta_q_g': 'delta_w', 'delta_k_g': 'delta_w', 'delta_w_out': 'delta_w', 'new_m_ada_w': 'new_m', 'new_m_ada_b': 'new_m', 'new_m_norm_g': 'new_m', 'new_m_w_in': 'new_m', 'new_m_conv_w': 'new_m', 'new_m_q_a_g': 'new_m', 'new_m_w_q_b': 'new_m', 'new_m_kv_a_g': 'new_m', 'new_m_w_kv_b': 'new_m', 'new_m_q_g': 'new_m', 'new_m_k_g': 'new_m', 'new_m_w_out': 'new_m', 'new_v_ada_w': 'new_v', 'new_v_ada_b': 'new_v', 'new_v_norm_g': 'new_v', 'new_v_w_in': 'new_v', 'new_v_conv_w': 'new_v', 'new_v_q_a_g': 'new_v', 'new_v_w_q_b': 'new_v', 'new_v_kv_a_g': 'new_v', 'new_v_w_kv_b': 'new_v', 'new_v_q_g': 'new_v', 'new_v_k_g': 'new_v', 'new_v_w_out': 'new_v'}


def _forward(args):
    return _fwd_reference(*[args[k] for k in FWD_PARAMS])


def _output_shape():
    def fwd():
        inp = _fwd_setup_inputs(0)
        return _fwd_reference(*[inp[k] for k in FWD_PARAMS])
    out = _jax.eval_shape(fwd)
    return out.shape, out.dtype

N_MICROBATCH = 1
ADAM_LR = 0.001
ADAM_B1 = 0.9
ADAM_B2 = 0.999
ADAM_EPS = 1e-08
ADAM_WD = 0.01
ADAM_STEP = 10
PER_EXAMPLE_BATCH_AXIS = {'x': 0, 'c': 0, 'positions': 0, 'loss_target': 0}
SHARED_INPUTS = []
_WEIGHT_DTYPES = {'ada_w': _jnp.float32, 'ada_b': _jnp.float32, 'norm_g': _jnp.float32, 'w_in': _jnp.float32, 'conv_w': _jnp.float32, 'q_a_g': _jnp.float32, 'w_q_b': _jnp.float32, 'kv_a_g': _jnp.float32, 'w_kv_b': _jnp.float32, 'q_g': _jnp.float32, 'k_g': _jnp.float32, 'w_out': _jnp.float32}
MOMENT_SCALE = {'ada_w': 7.994507e-01, 'ada_b': 2.548657e+00, 'norm_g': 4.243050e+00, 'w_in': 1.192328e-01, 'conv_w': 1.209553e+00, 'q_a_g': 8.073990e-03, 'w_q_b': 5.024050e-03, 'kv_a_g': 3.372728e-01, 'w_kv_b': 1.662654e-02, 'q_g': 3.005692e-02, 'k_g': 3.000680e-02, 'w_out': 4.434421e-02}


def _to_microbatches(a, axis):
    t = _jnp.moveaxis(a, axis, 0)
    t = t.reshape((N_MICROBATCH, t.shape[0] // N_MICROBATCH) + t.shape[1:])
    return _jnp.moveaxis(t, 1, axis + 1)


def setup_inputs(seed: int = 0) -> dict:
    inp = _fwd_setup_inputs(seed)
    key = _jax.random.fold_in(_jax.random.key(seed), 7919)
    shape, _ = _output_shape()
    out = dict(inp)
    out["loss_target"] = _jax.random.normal(_jax.random.fold_in(key, 0), shape, _jnp.float32)
    for i, name in enumerate(TWIN_WEIGHTS):
        w = inp[name].astype(_jnp.float32)
        if MOMENT_SCALE is None:
            s = _jnp.sqrt(_jnp.mean(_jnp.square(w)) + 1e-30)
        else:
            s = MOMENT_SCALE[name]
        km, kv = _jax.random.split(_jax.random.fold_in(key, i + 1))
        out[name] = w
        out["m_" + name] = s * _jax.random.normal(km, w.shape, _jnp.float32)
        out["v_" + name] = (s * s) * _jax.random.uniform(kv, w.shape, _jnp.float32, 0.5, 1.5)
    if N_MICROBATCH > 1:
        for name, axis in PER_EXAMPLE_BATCH_AXIS.items():
            out[name] = _to_microbatches(out[name], axis)
    return {'x': out['x'], 'c': out['c'], 'positions': out['positions'], 'ada_w': out['ada_w'], 'ada_b': out['ada_b'], 'norm_g': out['norm_g'], 'w_in': out['w_in'], 'conv_w': out['conv_w'], 'q_a_g': out['q_a_g'], 'w_q_b': out['w_q_b'], 'kv_a_g': out['kv_a_g'], 'w_kv_b': out['w_kv_b'], 'q_g': out['q_g'], 'k_g': out['k_g'], 'w_out': out['w_out'], 'loss_target': out['loss_target'], 'm_ada_w': out['m_ada_w'], 'm_ada_b': out['m_ada_b'], 'm_norm_g': out['m_norm_g'], 'm_w_in': out['m_w_in'], 'm_conv_w': out['m_conv_w'], 'm_q_a_g': out['m_q_a_g'], 'm_w_q_b': out['m_w_q_b'], 'm_kv_a_g': out['m_kv_a_g'], 'm_w_kv_b': out['m_w_kv_b'], 'm_q_g': out['m_q_g'], 'm_k_g': out['m_k_g'], 'm_w_out': out['m_w_out'], 'v_ada_w': out['v_ada_w'], 'v_ada_b': out['v_ada_b'], 'v_norm_g': out['v_norm_g'], 'v_w_in': out['v_w_in'], 'v_conv_w': out['v_conv_w'], 'v_q_a_g': out['v_q_a_g'], 'v_w_q_b': out['v_w_q_b'], 'v_kv_a_g': out['v_kv_a_g'], 'v_w_kv_b': out['v_w_kv_b'], 'v_q_g': out['v_q_g'], 'v_k_g': out['v_k_g'], 'v_w_out': out['v_w_out']}


def _loss(weights, diff, rest, loss_target):
    with _jax.named_scope("forward"):
        args = {**rest, TWIN_DIFF_INPUT: diff, **{k: w.astype(_WEIGHT_DTYPES[k]) for k, w in weights.items()}}
        y = _forward(args)
    with _jax.named_scope("loss_head"):
        err = _jnp.square(y.astype(_jnp.float32) - loss_target)
        return 0.5 * _jnp.sum(_jnp.mean(err, axis=-1)) if err.ndim else 0.5 * err


def _adamw(w, g, m, v):
    m = ADAM_B1 * m + (1.0 - ADAM_B1) * g
    v = ADAM_B2 * v + (1.0 - ADAM_B2) * _jnp.square(g)
    m_hat = m / (1.0 - ADAM_B1 ** ADAM_STEP)
    v_hat = v / (1.0 - ADAM_B2 ** ADAM_STEP)
    delta = -ADAM_LR * (m_hat / (_jnp.sqrt(v_hat) + ADAM_EPS) + ADAM_WD * w)
    return delta, m, v


def reference(x, c, positions, ada_w, ada_b, norm_g, w_in, conv_w, q_a_g, w_q_b, kv_a_g, w_kv_b, q_g, k_g, w_out, loss_target, m_ada_w, m_ada_b, m_norm_g, m_w_in, m_conv_w, m_q_a_g, m_w_q_b, m_kv_a_g, m_w_kv_b, m_q_g, m_k_g, m_w_out, v_ada_w, v_ada_b, v_norm_g, v_w_in, v_conv_w, v_q_a_g, v_w_q_b, v_kv_a_g, v_w_kv_b, v_q_g, v_k_g, v_w_out):
    given = dict(x=x, c=c, positions=positions, ada_w=ada_w, ada_b=ada_b, norm_g=norm_g, w_in=w_in, conv_w=conv_w, q_a_g=q_a_g, w_q_b=w_q_b, kv_a_g=kv_a_g, w_kv_b=w_kv_b, q_g=q_g, k_g=k_g, w_out=w_out, loss_target=loss_target, m_ada_w=m_ada_w, m_ada_b=m_ada_b, m_norm_g=m_norm_g, m_w_in=m_w_in, m_conv_w=m_conv_w, m_q_a_g=m_q_a_g, m_w_q_b=m_w_q_b, m_kv_a_g=m_kv_a_g, m_w_kv_b=m_w_kv_b, m_q_g=m_q_g, m_k_g=m_k_g, m_w_out=m_w_out, v_ada_w=v_ada_w, v_ada_b=v_ada_b, v_norm_g=v_norm_g, v_w_in=v_w_in, v_conv_w=v_conv_w, v_q_a_g=v_q_a_g, v_w_q_b=v_w_q_b, v_kv_a_g=v_kv_a_g, v_w_kv_b=v_w_kv_b, v_q_g=v_q_g, v_k_g=v_k_g, v_w_out=v_w_out)
    weights = {n: given[n] for n in TWIN_WEIGHTS}
    shared = {n: given[n] for n in SHARED_INPUTS}
    per_example = {n: given[n] for n in ['x', 'c', 'positions']}
    grad_fn = _jax.value_and_grad(_loss, argnums=(0, 1))

    def one_microbatch(ex, loss_target):
        ex = dict(ex)
        diff = ex.pop(TWIN_DIFF_INPUT)
        return grad_fn(weights, diff, {**shared, **ex}, loss_target)

    if N_MICROBATCH == 1:
        loss, (grad_w, grad_x) = one_microbatch(per_example, given["loss_target"])
    else:
        def body(carry, xs):
            loss_sum, grad_sum = carry
            l_k, (gw_k, gx_k) = one_microbatch(xs[0], xs[1])
            with _jax.named_scope("update"):
                return (loss_sum + l_k, _jax.tree.map(_jnp.add, grad_sum, gw_k)), gx_k

        init = (_jnp.zeros((), _jnp.float32), _jax.tree.map(_jnp.zeros_like, weights))
        (loss, grad_w), grad_x = _jax.lax.scan(body, init, (per_example, given["loss_target"]))
    with _jax.named_scope("update"):
        delta_w, new_m, new_v = {}, {}, {}
        for n in TWIN_WEIGHTS:
            delta_w[n], new_m[n], new_v[n] = _adamw(weights[n], grad_w[n], given["m_" + n], given["v_" + n])
    return (loss, grad_x, *[grad_w[n] for n in TWIN_WEIGHTS], *[delta_w[n] for n in TWIN_WEIGHTS],
            *[new_m[n] for n in TWIN_WEIGHTS], *[new_v[n] for n in TWIN_WEIGHTS])
```

```python
import functools
import math

import jax
import jax.numpy as jnp
from jax import lax
from jax.experimental import pallas as pl
from jax.experimental.pallas import tpu as pltpu

F32 = jnp.float32
BF16 = jnp.bfloat16
MESH = pl.DeviceIdType.MESH
SDS = jax.ShapeDtypeStruct
ANY = pl.BlockSpec(memory_space=pl.ANY)

D_MODEL = 2048
D_CONV = 1024
N_HEADS = 8
QK_NOPE = 128
QK_ROPE = 64
QK_HEAD = QK_NOPE + QK_ROPE
V_HEAD = 128
D_ATTN = N_HEADS * V_HEAD
Q_LORA = 512
KV_LORA = 256
ROPE_BASE = 10000.0
EPS = 1e-6
ADAM_LR, ADAM_B1, ADAM_B2, ADAM_EPS, ADAM_WD, ADAM_STEP = 0.001, 0.9, 0.999, 1e-08, 0.01, 10
N_CHIPS = 4
N_DEV = 8

LANES = 128
V7X_VMEM_BYTES = 64 * 1024 * 1024
MIB = 1024 * 1024

HEAD_PAD = 256
Q_PAD = N_HEADS * HEAD_PAD
U_ZA = 4 * D_CONV
U_CQ = U_ZA + D_ATTN
U_CKV = U_CQ + Q_LORA
U_KR = U_CKV + KV_LORA
KR_PAD = 256
U_COLS = U_KR + KR_PAD
MLA_COLS = Q_LORA + KV_LORA + KR_PAD

ATT_T = 512
INPROJ_TM, INPROJ_TN = 1024, 512
ROW_T = 512
OUT_T = 256
DH_TM, DH_TN = 512, 512
TN_TM, TN_TN, TN_TK = 1024, 1024, 512


def _cp(sem=None, vmem_mib=None, **kw):
    if sem is not None:
        kw["dimension_semantics"] = sem
    if vmem_mib is not None:
        kw["vmem_limit_bytes"] = min(vmem_mib * MIB, V7X_VMEM_BYTES - 4 * MIB)
    return pltpu.CompilerParams(**kw)


def _sigmoid(z):
    return 1.0 / (1.0 + jnp.exp(-z))


def _silu_grad(z, sg):
    return sg * (1.0 + z * (1.0 - sg))


def _nt(a, b):
    return lax.dot_general(a, b, (((1,), (1,)), ((), ())), preferred_element_type=F32)


def _tn(a, b):
    return lax.dot_general(a, b, (((0,), (0,)), ((), ())), preferred_element_type=F32)


def _nn(a, b):
    return jnp.dot(a, b, preferred_element_type=F32)


def _place():
    return lax.axis_index("x"), lax.axis_index("y"), lax.axis_index("c")


def _gather8(v, name, with_sum):
    rows, cols = v.shape

    def body(v_ref, out_ref, *rest):
        if with_sum:
            sum_ref, send_sems, recv_sems = rest
        else:
            send_sems, recv_sems = rest
        mx, my, mc = _place()
        me = 4 * mx + 2 * my + mc
        out_ref[me] = v_ref[...]
        peers = []
        for d in range(1, N_DEV):
            px = 1 - mx if d & 4 else mx
            py = 1 - my if d & 2 else my
            pc = 1 - mc if d & 1 else mc
            peers.append((px, py, pc))

        def copy(d, slot, to):
            return pltpu.make_async_remote_copy(
                src_ref=v_ref, dst_ref=out_ref.at[slot], send_sem=send_sems.at[d], recv_sem=recv_sems.at[d],
                device_id=to, device_id_type=MESH)

        sends = [copy(d, me, p) for d, p in enumerate(peers)]
        for cp in sends:
            cp.start()
        for d, (px, py, pc) in enumerate(peers):
            copy(d, 4 * px + 2 * py + pc, (px, py, pc)).wait_recv()
        for cp in sends:
            cp.wait_send()
        if with_sum:
            acc = out_ref[0]
            for b in range(1, N_DEV):
                acc = acc + out_ref[b]
            sum_ref[...] = acc

    out_shape = [SDS((N_DEV, rows, cols), F32)]
    if with_sum:
        out_shape.append(SDS((rows, cols), F32))
    vm = pl.BlockSpec(memory_space=pltpu.VMEM)
    return pl.pallas_call(
        body, name=name, out_shape=out_shape, in_specs=[vm], out_specs=[vm] * len(out_shape),
        scratch_shapes=[pltpu.SemaphoreType.DMA((N_DEV - 1,)), pltpu.SemaphoreType.DMA((N_DEV - 1,))],
    )(v)


def _chips_of(mx, my):
    chips = [(mx, 1 - my), (1 - mx, my), (1 - mx, 1 - my)]
    return chips, [2 * px + py for px, py in chips]


def _allgather_shards(shards):
    na = len(shards)
    halves = [s.shape[0] // 2 for s in shards]

    def body(*refs):
        ins, outs = refs[:na], refs[na:2 * na]
        s1, r1, s2, r2, lsem = refs[2 * na:]
        mx, my, mc = _place()
        k = 2 * mx + my
        sib = (mx, my, 1 - mc)
        chips, kks = _chips_of(mx, my)

        def half(a, slot, c):
            return outs[a].at[slot, pl.ds(c * halves[a], halves[a])]

        def mine(a):
            return ins[a].at[pl.ds(mc * halves[a], halves[a])]

        local = [pltpu.make_async_copy(ins[a], outs[a].at[k], lsem.at[a]) for a in range(na)]
        for cp in local:
            cp.start()
        sends = []
        for a in range(na):
            for d, (px, py) in enumerate(chips):
                cp = pltpu.make_async_remote_copy(
                    src_ref=mine(a), dst_ref=half(a, k, mc), send_sem=s1.at[3 * a + d], recv_sem=r1.at[3 * a + d],
                    device_id=(px, py, mc), device_id_type=MESH)
                cp.start()
                sends.append(cp)
        for a in range(na):
            for d, (px, py) in enumerate(chips):
                pltpu.make_async_remote_copy(
                    src_ref=mine(a), dst_ref=half(a, kks[d], mc), send_sem=s1.at[3 * a + d], recv_sem=r1.at[3 * a + d],
                    device_id=(px, py, mc), device_id_type=MESH).wait_recv()
                cp = pltpu.make_async_remote_copy(
                    src_ref=half(a, kks[d], mc), dst_ref=half(a, kks[d], mc), send_sem=s2.at[3 * a + d],
                    recv_sem=r2.at[3 * a + d], device_id=sib, device_id_type=MESH)
                cp.start()
                sends.append(cp)
        for a in range(na):
            for d in range(3):
                pltpu.make_async_remote_copy(
                    src_ref=half(a, kks[d], 1 - mc), dst_ref=half(a, kks[d], 1 - mc), send_sem=s2.at[3 * a + d],
                    recv_sem=r2.at[3 * a + d], device_id=sib, device_id_type=MESH).wait_recv()
        for cp in sends:
            cp.wait_send()
        for cp in local:
            cp.wait()

    return pl.pallas_call(
        body, name="allgather_weights",
        out_shape=[SDS((N_CHIPS,) + s.shape, s.dtype) for s in shards],
        in_specs=[ANY] * na, out_specs=[ANY] * na,
        scratch_shapes=[pltpu.SemaphoreType.DMA((3 * na,))] * 4 + [pltpu.SemaphoreType.DMA((na,))],
    )(*shards)


def _rs_core_swap(grads):
    na = len(grads)
    halves = [g.shape[1] // 2 for g in grads]

    def body(*refs):
        ins, mine_out, theirs_out = refs[:na], refs[na:2 * na], refs[2 * na:3 * na]
        ssem, rsem, lsem = refs[3 * na:]
        mx, my, mc = _place()
        sib = (mx, my, 1 - mc)
        local, sends = [], []
        for a in range(na):
            cp = pltpu.make_async_copy(ins[a].at[:, pl.ds(mc * halves[a], halves[a])], mine_out[a], lsem.at[a])
            cp.start()
            local.append(cp)
            cp = pltpu.make_async_remote_copy(
                src_ref=ins[a].at[:, pl.ds((1 - mc) * halves[a], halves[a])], dst_ref=theirs_out[a],
                send_sem=ssem.at[a], recv_sem=rsem.at[a], device_id=sib, device_id_type=MESH)
            cp.start()
            sends.append(cp)
        for cp in sends:
            cp.wait_recv()
        for cp in sends:
            cp.wait_send()
        for cp in local:
            cp.wait()

    half_shapes = [SDS((N_CHIPS, h) + g.shape[2:], g.dtype) for g, h in zip(grads, halves)]
    outs = pl.pallas_call(
        body, name="rs_core_swap", out_shape=half_shapes + half_shapes,
        in_specs=[ANY] * na, out_specs=[ANY] * (2 * na),
        scratch_shapes=[pltpu.SemaphoreType.DMA((na,))] * 3,
    )(*grads)
    return outs[:na], outs[na:]


def _rs_chip_exchange(parts):
    na = len(parts)

    def body(*refs):
        ins, outs = refs[:na], refs[na:2 * na]
        ssem, rsem, lsem = refs[2 * na:]
        mx, my, mc = _place()
        k = 2 * mx + my
        chips, kks = _chips_of(mx, my)
        local, sends = [], []
        for a in range(na):
            cp = pltpu.make_async_copy(ins[a].at[k], outs[a].at[k], lsem.at[a])
            cp.start()
            local.append(cp)
            for d, (px, py) in enumerate(chips):
                cp = pltpu.make_async_remote_copy(
                    src_ref=ins[a].at[kks[d]], dst_ref=outs[a].at[k], send_sem=ssem.at[3 * a + d],
                    recv_sem=rsem.at[3 * a + d], device_id=(px, py, mc), device_id_type=MESH)
                cp.start()
                sends.append(cp)
        for a in range(na):
            for d, (px, py) in enumerate(chips):
                pltpu.make_async_remote_copy(
                    src_ref=ins[a].at[kks[d]], dst_ref=outs[a].at[kks[d]], send_sem=ssem.at[3 * a + d],
                    recv_sem=rsem.at[3 * a + d], device_id=(px, py, mc), device_id_type=MESH).wait_recv()
        for cp in sends:
            cp.wait_send()
        for cp in local:
            cp.wait()

    return pl.pallas_call(
        body, name="rs_chip_exchange", out_shape=[SDS(p.shape, p.dtype) for p in parts],
        in_specs=[ANY] * na, out_specs=[ANY] * na,
        scratch_shapes=[pltpu.SemaphoreType.DMA((3 * na,))] * 2 + [pltpu.SemaphoreType.DMA((na,))],
    )(*parts)


def _rs_core_join(halves):
    na = len(halves)

    def body(*refs):
        ins, outs = refs[:na], refs[na:2 * na]
        ssem, rsem, lsem = refs[2 * na:]
        mx, my, mc = _place()
        sib = (mx, my, 1 - mc)
        local, sends = [], []
        for a in range(na):
            cp = pltpu.make_async_copy(ins[a], outs[a].at[mc], lsem.at[a])
            cp.start()
            local.append(cp)
            cp = pltpu.make_async_remote_copy(
                src_ref=ins[a], dst_ref=outs[a].at[mc], send_sem=ssem.at[a], recv_sem=rsem.at[a],
                device_id=sib, device_id_type=MESH)
            cp.start()
            sends.append(cp)
        for a in range(na):
            pltpu.make_async_remote_copy(
                src_ref=ins[a], dst_ref=outs[a].at[1 - mc], send_sem=ssem.at[a], recv_sem=rsem.at[a],
                device_id=sib, device_id_type=MESH).wait_recv()
        for cp in sends:
            cp.wait_send()
        for cp in local:
            cp.wait()

    return pl.pallas_call(
        body, name="rs_core_join", out_shape=[SDS((2,) + h.shape, h.dtype) for h in halves],
        in_specs=[ANY] * na, out_specs=[ANY] * na,
        scratch_shapes=[pltpu.SemaphoreType.DMA((na,))] * 3,
    )(*halves)


def _row_tile(rows, limit):
    if rows <= limit:
        return rows
    best = None
    for t in range(16, limit + 1, 16):
        if rows % t == 0:
            best = t
    assert best is not None, rows
    return best


def _add_bf16(a, b, name):
    shape = a.shape
    a2, b2 = a.reshape(-1, shape[-1]), b.reshape(-1, shape[-1])
    rows, cols = a2.shape
    tb = _row_tile(rows, 512)

    def body(a_ref, b_ref, o_ref):
        o_ref[...] = (a_ref[...].astype(F32) + b_ref[...].astype(F32)).astype(BF16)

    spec = pl.BlockSpec((tb, cols), lambda i: (i, 0))
    out = pl.pallas_call(
        body, name=name, grid=(rows // tb,), in_specs=[spec, spec], out_specs=spec, out_shape=SDS((rows, cols), BF16),
        compiler_params=_cp(("arbitrary",)),
    )(a2, b2)
    return out.reshape(shape)


def _sum_chips(p, name):
    _, rows, cols = p.shape
    tb = _row_tile(rows, 256)

    def body(p_ref, o_ref):
        acc = p_ref[0].astype(F32)
        for j in range(1, N_CHIPS):
            acc = acc + p_ref[j].astype(F32)
        o_ref[...] = acc

    return pl.pallas_call(
        body, name=name, grid=(rows // tb,),
        in_specs=[pl.BlockSpec((N_CHIPS, tb, cols), lambda i: (0, i, 0))],
        out_specs=pl.BlockSpec((tb, cols), lambda i: (i, 0)), out_shape=SDS((rows, cols), F32),
        compiler_params=_cp(("arbitrary",)),
    )(p)


def _adamw_math(w, g, m, v):
    m2 = ADAM_B1 * m + (1.0 - ADAM_B1) * g
    v2 = ADAM_B2 * v + (1.0 - ADAM_B2) * (g * g)
    m_hat = m2 / (1.0 - ADAM_B1 ** ADAM_STEP)
    v_hat = v2 / (1.0 - ADAM_B2 ** ADAM_STEP)
    delta = -ADAM_LR * (m_hat / (jnp.sqrt(v_hat) + ADAM_EPS) + ADAM_WD * w)
    return delta, m2, v2


def _adamw(w, g, m, v, name):
    rows, cols = w.shape
    tb = _row_tile(rows, 256)

    def body(w_ref, g_ref, m_ref, v_ref, d_ref, m2_ref, v2_ref):
        d, m2, v2 = _adamw_math(w_ref[...], g_ref[...], m_ref[...], v_ref[...])
        d_ref[...] = d
        m2_ref[...] = m2
        v2_ref[...] = v2

    spec = pl.BlockSpec((tb, cols), lambda i: (i, 0))
    return pl.pallas_call(
        body, name=name, grid=(rows // tb,), in_specs=[spec] * 4, out_specs=[spec] * 3,
        out_shape=[SDS((rows, cols), F32)] * 3, compiler_params=_cp(("arbitrary",), 40),
    )(w, g, m, v)


def _ada_w_update(sc_all, dmod_k, w, m, v):
    rows, cols = w.shape
    tb = 256

    def body(s_ref, dm_ref, w_ref, m_ref, v_ref, g_ref, d_ref, m2_ref, v2_ref):
        g = _tn(s_ref[...].astype(BF16), dm_ref[...].astype(BF16))
        d, m2, v2 = _adamw_math(w_ref[...], g, m_ref[...], v_ref[...])
        g_ref[...] = g
        d_ref[...] = d
        m2_ref[...] = m2
        v2_ref[...] = v2

    spec = pl.BlockSpec((tb, cols), lambda i: (i, 0))
    return pl.pallas_call(
        body, name="ada_w_update", grid=(rows // tb,),
        in_specs=[pl.BlockSpec((N_DEV, tb), lambda i: (0, i)), pl.BlockSpec((N_DEV, cols), lambda i: (0, 0)), spec, spec, spec],
        out_specs=[spec] * 4, out_shape=[SDS((rows, cols), F32)] * 4, compiler_params=_cp(("arbitrary",), 40),
    )(sc_all, dmod_k, w, m, v)


def _ada_mod(c_all, w, b_k):
    rows, cols = w.shape
    tn = 512

    def body(c_ref, w_ref, b_ref, o_ref, s_ref):
        cv = c_ref[...]
        s = cv * _sigmoid(cv)
        s_ref[...] = s
        o_ref[...] = _nn(s.astype(BF16), w_ref[...].astype(BF16)) + b_ref[...]

    return pl.pallas_call(
        body, name="ada_mod", grid=(cols // tn,),
        in_specs=[pl.BlockSpec((N_DEV, rows), lambda j: (0, 0)), pl.BlockSpec((rows, tn), lambda j: (0, j)),
                  pl.BlockSpec((1, tn), lambda j: (0, j))],
        out_specs=[pl.BlockSpec((N_DEV, tn), lambda j: (0, j)), pl.BlockSpec((N_DEV, rows), lambda j: (0, 0))],
        out_shape=[SDS((N_DEV, cols), F32), SDS((N_DEV, rows), F32)], compiler_params=_cp(("arbitrary",)),
    )(c_all, w, b_k)


def _inproj(x, norm_g, scale, shift, w_my):
    seq, dm = x.shape
    ncols = w_my.shape[1]
    tm, tn = min(INPROJ_TM, seq), INPROJ_TN

    def body(x_ref, g_ref, sc_ref, sh_ref, w_ref, h_ref, u_ref):
        @pl.when(pl.program_id(1) == 0)
        def _():
            xv = x_ref[...]
            r = lax.rsqrt(jnp.mean(xv * xv, axis=-1, keepdims=True) + EPS)
            hv = (xv * r * g_ref[...]) * (1.0 + sc_ref[...]) + sh_ref[...]
            h_ref[...] = hv.astype(BF16)

        u_ref[...] = _nn(h_ref[...], w_ref[...]).astype(BF16)

    vec = pl.BlockSpec((1, dm), lambda i, j: (0, 0))
    return pl.pallas_call(
        body, name="inproj", grid=(seq // tm, ncols // tn),
        in_specs=[pl.BlockSpec((tm, dm), lambda i, j: (i, 0)), vec, vec, vec, pl.BlockSpec((dm, tn), lambda i, j: (0, j))],
        out_specs=[pl.BlockSpec((tm, dm), lambda i, j: (i, 0)), pl.BlockSpec((tm, tn), lambda i, j: (i, j))],
        out_shape=[SDS((seq, dm), BF16), SDS((seq, ncols), BF16)],
        compiler_params=_cp(("arbitrary", "arbitrary"), 48),
    )(x, norm_g, scale, shift, w_my)


HALO = 16


def _conv_taps(uc, halo, ext_ref, ts, causal):
    if causal:
        ext_ref[0:HALO, :] = halo
        ext_ref[HALO:HALO + ts, :] = uc
        return ext_ref[pl.ds(HALO - 1, ts), :], ext_ref[pl.ds(HALO - 2, ts), :]
    ext_ref[0:ts, :] = uc
    ext_ref[ts:ts + HALO, :] = halo
    return ext_ref[pl.ds(1, ts), :], ext_ref[pl.ds(2, ts), :]


def _conv_fwd(u, conv_w):
    seq = u.shape[0]
    ts = min(ROW_T, seq)
    hb = ts // HALO

    def body(xc_ref, bc_ref, cc_ref, zc_ref, xp_ref, cp_ref, w_ref, y_ref, ext_ref):
        i = pl.program_id(0)
        uc = cc_ref[...].astype(F32) * xc_ref[...].astype(F32)
        up = cp_ref[...].astype(F32) * xp_ref[...].astype(F32)
        up = jnp.where(i > 0, up, 0.0)
        u1, u2 = _conv_taps(uc, up, ext_ref, ts, True)
        conv = w_ref[0:1, :] * u2 + w_ref[1:2, :] * u1 + w_ref[2:3, :] * uc
        z = zc_ref[...].astype(F32)
        y_ref[...] = ((bc_ref[...].astype(F32) * conv) * (z * _sigmoid(z))).astype(BF16)

    def col(cb):
        return pl.BlockSpec((ts, D_CONV), lambda i: (i, cb))

    def prev(cb):
        return pl.BlockSpec((HALO, D_CONV), lambda i: (jnp.maximum(i * hb - 1, 0), cb))

    return pl.pallas_call(
        body, name="conv_fwd", grid=(seq // ts,),
        in_specs=[col(0), col(1), col(2), col(3), prev(0), prev(2), pl.BlockSpec((3, D_CONV), lambda i: (0, 0))],
        out_specs=pl.BlockSpec((ts, D_CONV), lambda i: (i, 0)), out_shape=SDS((seq, D_CONV), BF16),
        scratch_shapes=[pltpu.VMEM((ts + HALO, D_CONV), F32)],
        compiler_params=_cp(("arbitrary",), 40),
    )(u, u, u, u, u, u, conv_w)


def _rope_tables(pos_ref, freq_ref):
    ang = pos_ref[...].astype(F32) * freq_ref[...]
    lane = lax.broadcasted_iota(jnp.int32, ang.shape, 1)
    cs, sn = jnp.cos(ang), jnp.sin(ang)
    half = QK_ROPE // 2
    cos_t = jnp.where(lane < QK_ROPE, cs, 0.0)
    sin_lo = jnp.where(lane < half, sn, 0.0)
    sin_hi = jnp.where((lane >= half) & (lane < QK_ROPE), sn, 0.0)
    return cos_t, sin_lo, sin_hi


def _rope(blk, tables):
    cos_t, sin_lo, sin_hi = tables
    half = QK_ROPE // 2
    return blk * cos_t - pltpu.roll(blk, LANES - half, 1) * sin_lo + pltpu.roll(blk, half, 1) * sin_hi


def _rope_bwd(g, tables):
    cos_t, sin_lo, sin_hi = tables
    half = QK_ROPE // 2
    return g * cos_t + pltpu.roll(g, LANES - half, 1) * sin_lo - pltpu.roll(g, half, 1) * sin_hi


def _rms(v, n):
    r = lax.rsqrt(jnp.sum(v * v, axis=-1, keepdims=True) * (1.0 / n) + EPS)
    return v * r, r


def _mla_prep(u, pos, freq, q_a_g, wq, kv_a_g, wkn, wv, q_g, k_g):
    seq = u.shape[0]
    ts = min(ROW_T, seq)
    qscale = 1.0 / math.sqrt(QK_HEAD)

    def body(cq_ref, ckv_ref, kr_ref, pos_ref, freq_ref, qag_ref, wq_ref, kvag_ref, wkn_ref, wv_ref, qg_ref, kg_ref,
             q_ref, k_ref, v_ref):
        tables = _rope_tables(pos_ref, freq_ref)
        cqn, _ = _rms(cq_ref[...].astype(F32), Q_LORA)
        qp = _nn((cqn * qag_ref[...]).astype(BF16), wq_ref[...])
        qg = qg_ref[...]
        for h in range(N_HEADS):
            lo = h * HEAD_PAD
            qn, _ = _rms(qp[:, lo:lo + HEAD_PAD], QK_HEAD)
            qn = qn * qg
            q_ref[:, lo:lo + LANES] = (qn[:, :LANES] * qscale).astype(BF16)
            q_ref[:, lo + LANES:lo + HEAD_PAD] = (_rope(qn[:, LANES:], tables) * qscale).astype(BF16)
        ckvn, _ = _rms(ckv_ref[...].astype(F32), KV_LORA)
        ckvb = (ckvn * kvag_ref[...]).astype(BF16)
        kn = _nn(ckvb, wkn_ref[...])
        v_ref[...] = _nn(ckvb, wv_ref[...]).astype(BF16)
        kr = kr_ref[:, 0:LANES].astype(F32)
        ssr = jnp.sum(kr * kr, axis=-1, keepdims=True)
        kg = kg_ref[...]
        for h in range(N_HEADS):
            knh = kn[:, h * QK_NOPE:(h + 1) * QK_NOPE]
            r = lax.rsqrt((jnp.sum(knh * knh, axis=-1, keepdims=True) + ssr) * (1.0 / QK_HEAD) + EPS)
            lo = h * HEAD_PAD
            k_ref[:, lo:lo + LANES] = (knh * r * kg[:, :LANES]).astype(BF16)
            k_ref[:, lo + LANES:lo + HEAD_PAD] = _rope(kr * r * kg[:, LANES:], tables).astype(BF16)

    def full(a):
        return pl.BlockSpec(a.shape, lambda i: (0,) * a.ndim)

    return pl.pallas_call(
        body, name="mla_prep", grid=(seq // ts,),
        in_specs=[pl.BlockSpec((ts, Q_LORA), lambda i: (i, U_CQ // Q_LORA)),
                  pl.BlockSpec((ts, KV_LORA), lambda i: (i, U_CKV // KV_LORA)),
                  pl.BlockSpec((ts, KR_PAD), lambda i: (i, U_KR // KR_PAD)),
                  pl.BlockSpec((ts, 1), lambda i: (i, 0)), full(freq), full(q_a_g), full(wq), full(kv_a_g), full(wkn),
                  full(wv), full(q_g), full(k_g)],
        out_specs=[pl.BlockSpec((ts, Q_PAD), lambda i: (i, 0)), pl.BlockSpec((ts, Q_PAD), lambda i: (i, 0)),
                   pl.BlockSpec((ts, D_ATTN), lambda i: (i, 0))],
        out_shape=[SDS((seq, Q_PAD), BF16), SDS((seq, Q_PAD), BF16), SDS((seq, D_ATTN), BF16)],
        compiler_params=_cp(("arbitrary",), 48),
    )(u, u, u, pos, freq, q_a_g, wq, kv_a_g, wkn, wv, q_g, k_g)


def _causal_mask(t):
    return lax.broadcasted_iota(jnp.int32, (t, t), 0) <= lax.broadcasted_iota(jnp.int32, (t, t), 1)


def _flash_fwd(q, k, v, u):
    seq = q.shape[0]
    t = min(ATT_T, seq)
    za_blk = U_ZA // V_HEAD

    def body(q_ref, k_ref, v_ref, z_ref, o_ref, y_ref, lse_ref):
        i = pl.program_id(1)
        qv = q_ref[...]

        def step(j, carry, masked):
            m, l, acc = carry
            off = pl.multiple_of(j * t, t)
            s = _nt(k_ref[pl.ds(off, t), :], qv)
            if masked:
                s = jnp.where(_causal_mask(t), s, -jnp.inf)
            m_new = jnp.maximum(m, jnp.max(s, axis=0, keepdims=True))
            alpha = jnp.exp(m - m_new)
            p = jnp.exp(s - m_new)
            l = alpha * l + jnp.sum(p, axis=0, keepdims=True)
            acc = alpha * acc + _tn(v_ref[pl.ds(off, t), :], p.astype(BF16))
            return m_new, l, acc

        init = (jnp.full((1, t), -jnp.inf, F32), jnp.zeros((1, t), F32), jnp.zeros((V_HEAD, t), F32))
        carry = lax.fori_loop(0, i, lambda j, c: step(j, c, False), init)
        m, l, acc = step(i, carry, True)
        o = (acc * (1.0 / l)).T
        lse_ref[...] = m + jnp.log(l)
        o_ref[...] = o.astype(BF16)
        z = z_ref[...].astype(F32)
        y_ref[...] = (o * (z * _sigmoid(z))).astype(BF16)

    tile = pl.BlockSpec((t, V_HEAD), lambda h, i: (i, h))
    return pl.pallas_call(
        body, name="flash_fwd", grid=(N_HEADS, seq // t),
        in_specs=[pl.BlockSpec((t, HEAD_PAD), lambda h, i: (i, h)), pl.BlockSpec((seq, HEAD_PAD), lambda h, i: (0, h)),
                  pl.BlockSpec((seq, V_HEAD), lambda h, i: (0, h)), pl.BlockSpec((t, V_HEAD), lambda h, i: (i, za_blk + h))],
        out_specs=[tile, tile, pl.BlockSpec((None, 1, t), lambda h, i: (h, 0, i))],
        out_shape=[SDS((seq, D_ATTN), BF16), SDS((seq, D_ATTN), BF16), SDS((N_HEADS, 1, seq), F32)],
        compiler_params=_cp(("arbitrary", "arbitrary"), 40),
    )(q, k, v, u)


def _outproj_loss(y_conv, y_attn, x, target, gate, w_out):
    seq, dm = x.shape
    ts = min(OUT_T, seq)
    n = seq // ts
    dmix = w_out.shape[0]

    def body(yc_ref, ya_ref, x_ref, t_ref, gate_ref, wo_hbm, dout_ref, dy_ref, dyc_ref, stats_ref, wo_ref, sem, acc_ref):
        i = pl.program_id(0)

        @pl.when(i == 0)
        def _():
            cp = pltpu.make_async_copy(wo_hbm, wo_ref, sem)
            cp.start()
            cp.wait()
            acc_ref[...] = jnp.zeros_like(acc_ref)

        y = _nn(yc_ref[...], wo_ref[0:D_CONV, :]) + _nn(ya_ref[...], wo_ref[D_CONV:dmix, :])
        gate_v = gate_ref[...]
        diff = (x_ref[...] + gate_v * y) - t_ref[...]
        dout = diff * (1.0 / dm)
        dout_ref[...] = dout
        acc_ref[0:8, :] += jnp.sum((dout * y).reshape(ts // 8, 8, dm), axis=0)
        acc_ref[8:16, :] += jnp.sum((diff * diff).reshape(ts // 8, 8, dm), axis=0)
        dy = (dout * gate_v).astype(BF16)
        dy_ref[...] = dy
        dyc_ref[...] = _nt(dy, wo_ref[...]).astype(BF16)

        @pl.when(i == n - 1)
        def _():
            stats_ref[...] = jnp.zeros_like(stats_ref)
            stats_ref[0:1, :] = jnp.sum(acc_ref[0:8, :], axis=0, keepdims=True)
            loss = jnp.sum(acc_ref[8:16, :]) * (0.5 / dm)
            stats_ref[1:2, :] = jnp.full((1, dm), loss, F32)

    row = pl.BlockSpec((ts, dm), lambda i: (i, 0))
    half = pl.BlockSpec((ts, D_CONV), lambda i: (i, 0))
    return pl.pallas_call(
        body, name="outproj_loss", grid=(n,),
        in_specs=[half, half, row, row, pl.BlockSpec((1, dm), lambda i: (0, 0)), ANY],
        out_specs=[row, row, pl.BlockSpec((ts, dmix), lambda i: (i, 0)), pl.BlockSpec((8, dm), lambda i: (0, 0))],
        out_shape=[SDS((seq, dm), F32), SDS((seq, dm), BF16), SDS((seq, dmix), BF16), SDS((8, dm), F32)],
        scratch_shapes=[pltpu.VMEM(w_out.shape, BF16), pltpu.SemaphoreType.DMA(()), pltpu.VMEM((16, dm), F32)],
        compiler_params=_cp(("arbitrary",), 52),
    )(y_conv, y_attn, x, target, gate, w_out)


def _matmul_tn(a, b, name):
    seq, m = a.shape
    n = b.shape[1]
    tm, tn, tk = min(TN_TM, m), min(TN_TN, n), min(TN_TK, seq)
    nk = seq // tk

    def body(a_ref, b_ref, o_ref, acc_ref):
        kk = pl.program_id(2)

        @pl.when(kk == 0)
        def _():
            acc_ref[...] = jnp.zeros_like(acc_ref)

        acc_ref[...] += _tn(a_ref[...], b_ref[...])

        @pl.when(kk == nk - 1)
        def _():
            o_ref[...] = acc_ref[...].astype(BF16)

    return pl.pallas_call(
        body, name=name, grid=(m // tm, n // tn, nk),
        in_specs=[pl.BlockSpec((tk, tm), lambda i, j, kk: (kk, i)), pl.BlockSpec((tk, tn), lambda i, j, kk: (kk, j))],
        out_specs=pl.BlockSpec((tm, tn), lambda i, j, kk: (i, j)), out_shape=SDS((m, n), BF16),
        scratch_shapes=[pltpu.VMEM((tm, tn), F32)],
        compiler_params=_cp(("arbitrary", "arbitrary", "arbitrary"), 40),
    )(a, b)


def _attn_gate_bwd(dycat, o, u):
    seq = o.shape[0]
    ts = min(ROW_T, seq)

    def body(dy_ref, o_ref, z_ref, do_ref, dz_ref, dl_ref):
        dy = dy_ref[...].astype(F32)
        ov = o_ref[...].astype(F32)
        z = z_ref[...].astype(F32)
        sg = _sigmoid(z)
        do = dy * (z * sg)
        do_ref[...] = do.astype(BF16)
        dz_ref[...] = (dy * ov * _silu_grad(z, sg)).astype(BF16)
        prod = do * ov
        ones = jnp.ones((8, V_HEAD), F32)
        for h in range(N_HEADS):
            ph = prod[:, h * V_HEAD:(h + 1) * V_HEAD]
            rows = lax.dot_general(ones, ph, (((1,), (1,)), ((), ())), precision=lax.Precision.HIGHEST,
                                   preferred_element_type=F32)
            dl_ref[h] = rows[0:1, :]

    blk = pl.BlockSpec((ts, D_ATTN), lambda i: (i, 0))
    return pl.pallas_call(
        body, name="attn_gate_bwd", grid=(seq // ts,),
        in_specs=[pl.BlockSpec((ts, D_ATTN), lambda i: (i, 1)), blk, pl.BlockSpec((ts, D_ATTN), lambda i: (i, U_ZA // D_ATTN))],
        out_specs=[blk, blk, pl.BlockSpec((N_HEADS, 1, ts), lambda i: (0, 0, i))],
        out_shape=[SDS((seq, D_ATTN), BF16), SDS((seq, D_ATTN), BF16), SDS((N_HEADS, 1, seq), F32)],
        compiler_params=_cp(("arbitrary",), 40),
    )(dycat, o, u)


def _flash_bwd(q, k, v, do, lse, delta):
    seq = q.shape[0]
    t = min(ATT_T, seq)
    n = seq // t

    def body(k_ref, v_ref, q_ref, do_ref, lse_ref, dl_ref, dq_ref, dk_ref, dv_ref):
        j = pl.program_id(1)

        @pl.when(j == 0)
        def _():
            dq_ref[...] = jnp.zeros_like(dq_ref)

        kv = k_ref[...]
        vv = v_ref[...]

        def step(i, carry, masked):
            dk, dv = carry
            off = pl.multiple_of(i * t, t)
            qv = q_ref[pl.ds(off, t), :]
            dov = do_ref[pl.ds(off, t), :]
            p = jnp.exp(_nt(kv, qv) - lse_ref[:, pl.ds(off, t)])
            if masked:
                p = jnp.where(_causal_mask(t), p, 0.0)
            dv = dv + _nn(p.astype(BF16), dov)
            ds = (p * (_nt(vv, dov) - dl_ref[:, pl.ds(off, t)])).astype(BF16)
            dk = dk + _nn(ds, qv)
            dq_ref[pl.ds(off, t), :] += _tn(ds, kv)
            return dk, dv

        carry = step(j, (jnp.zeros((t, HEAD_PAD), F32), jnp.zeros((t, V_HEAD), F32)), True)
        dk, dv = lax.fori_loop(j + 1, n, lambda i, c: step(i, c, False), carry)
        dk_ref[...] = dk
        dv_ref[...] = dv.astype(BF16)

    row = pl.BlockSpec((None, 1, seq), lambda h, j: (h, 0, 0))
    return pl.pallas_call(
        body, name="flash_bwd", grid=(N_HEADS, n),
        in_specs=[pl.BlockSpec((t, HEAD_PAD), lambda h, j: (j, h)), pl.BlockSpec((t, V_HEAD), lambda h, j: (j, h)),
                  pl.BlockSpec((seq, HEAD_PAD), lambda h, j: (0, h)), pl.BlockSpec((seq, V_HEAD), lambda h, j: (0, h)), row, row],
        out_specs=[pl.BlockSpec((seq, HEAD_PAD), lambda h, j: (0, h)), pl.BlockSpec((t, HEAD_PAD), lambda h, j: (j, h)),
                   pl.BlockSpec((t, V_HEAD), lambda h, j: (j, h))],
        out_shape=[SDS((seq, Q_PAD), F32), SDS((seq, Q_PAD), F32), SDS((seq, D_ATTN), BF16)],
        compiler_params=_cp(("arbitrary", "arbitrary"), 52),
    )(k, v, q, do, lse, delta)


SG_QAG, SG_KVAG, SG_QG, SG_KG, SG_COLS = 0, Q_LORA, Q_LORA + KV_LORA, Q_LORA + KV_LORA + HEAD_PAD, D_MODEL


def _mla_bwd(dq, dk, dv, u, pos, freq, q_a_g, wq, kv_a_g, wkn, wv, q_g, k_g):
    seq = u.shape[0]
    ts = min(ROW_T, seq)
    n = seq // ts
    qscale = 1.0 / math.sqrt(QK_HEAD)

    def body(dq_ref, dk_ref, dv_ref, cq_ref, ckv_ref, kr_ref, pos_ref, freq_ref, qag_ref, wq_ref, kvag_ref, wkn_ref,
             wv_ref, qg_ref, kg_ref, du_ref, dwq_ref, dwkn_ref, dwv_ref, sg_ref, dqp_ref, dkn_ref):
        i = pl.program_id(0)

        @pl.when(i == 0)
        def _():
            dwq_ref[...] = jnp.zeros_like(dwq_ref)
            dwkn_ref[...] = jnp.zeros_like(dwkn_ref)
            dwv_ref[...] = jnp.zeros_like(dwv_ref)
            sg_ref[...] = jnp.zeros_like(sg_ref)

        tables = _rope_tables(pos_ref, freq_ref)

        cq = cq_ref[...].astype(F32)
        cqn, rq = _rms(cq, Q_LORA)
        qag = qag_ref[...]
        cqb = (cqn * qag).astype(BF16)
        qp = _nn(cqb, wq_ref[...])
        qg = qg_ref[...]
        dqg = jnp.zeros((1, HEAD_PAD), F32)
        for h in range(N_HEADS):
            lo = h * HEAD_PAD
            xn, r = _rms(qp[:, lo:lo + HEAD_PAD], QK_HEAD)
            g = jnp.concatenate([dq_ref[:, lo:lo + LANES], _rope_bwd(dq_ref[:, lo + LANES:lo + HEAD_PAD], tables)],
                                axis=-1) * qscale
            dqg = dqg + jnp.sum(g * xn, axis=0, keepdims=True)
            gy = g * qg
            mean = jnp.sum(gy * xn, axis=-1, keepdims=True) * (1.0 / QK_HEAD)
            dqp_ref[:, lo:lo + HEAD_PAD] = (r * (gy - xn * mean)).astype(BF16)
        dqp = dqp_ref[...]
        dwq_ref[...] += _tn(cqb, dqp)
        dcqn = _nt(dqp, wq_ref[...])
        sg_ref[0:1, SG_QAG:SG_QAG + Q_LORA] += jnp.sum(dcqn * cqn, axis=0, keepdims=True)
        sg_ref[0:1, SG_QG:SG_QG + HEAD_PAD] += dqg
        gy = dcqn * qag
        mean = jnp.sum(gy * cqn, axis=-1, keepdims=True) * (1.0 / Q_LORA)
        du_ref[:, 0:Q_LORA] = (rq * (gy - cqn * mean)).astype(BF16)

        ckv = ckv_ref[...].astype(F32)
        ckvn, rkv = _rms(ckv, KV_LORA)
        kvag = kvag_ref[...]
        ckvb = (ckvn * kvag).astype(BF16)
        kn = _nn(ckvb, wkn_ref[...])
        kr = kr_ref[:, 0:LANES].astype(F32)
        ssr = jnp.sum(kr * kr, axis=-1, keepdims=True)
        kg = kg_ref[...]
        kg_n, kg_r = kg[:, :LANES], kg[:, LANES:]
        dkg_n = jnp.zeros((1, LANES), F32)
        dkg_r = jnp.zeros((1, LANES), F32)
        dkr = jnp.zeros((ts, LANES), F32)
        for h in range(N_HEADS):
            knh = kn[:, h * QK_NOPE:(h + 1) * QK_NOPE]
            r = lax.rsqrt((jnp.sum(knh * knh, axis=-1, keepdims=True) + ssr) * (1.0 / QK_HEAD) + EPS)
            xn_n, xn_r = knh * r, kr * r
            lo = h * HEAD_PAD
            g_n = dk_ref[:, lo:lo + LANES]
            g_r = _rope_bwd(dk_ref[:, lo + LANES:lo + HEAD_PAD], tables)
            dkg_n = dkg_n + jnp.sum(g_n * xn_n, axis=0, keepdims=True)
            dkg_r = dkg_r + jnp.sum(g_r * xn_r, axis=0, keepdims=True)
            gy_n, gy_r = g_n * kg_n, g_r * kg_r
            mean = (jnp.sum(gy_n * xn_n, axis=-1, keepdims=True) + jnp.sum(gy_r * xn_r, axis=-1, keepdims=True)) * (1.0 / QK_HEAD)
            dkn_ref[:, h * QK_NOPE:(h + 1) * QK_NOPE] = (r * (gy_n - xn_n * mean)).astype(BF16)
            dkr = dkr + r * (gy_r - xn_r * mean)
        dkn = dkn_ref[...]
        dvv = dv_ref[...]
        dwkn_ref[...] += _tn(ckvb, dkn)
        dwv_ref[...] += _tn(ckvb, dvv)
        dckvn = _nt(dkn, wkn_ref[...]) + _nt(dvv, wv_ref[...])
        sg_ref[0:1, SG_KVAG:SG_KVAG + KV_LORA] += jnp.sum(dckvn * ckvn, axis=0, keepdims=True)
        sg_ref[0:1, SG_KG:SG_KG + LANES] += dkg_n
        sg_ref[0:1, SG_KG + LANES:SG_KG + HEAD_PAD] += dkg_r
        gy = dckvn * kvag
        mean = jnp.sum(gy * ckvn, axis=-1, keepdims=True) * (1.0 / KV_LORA)
        du_ref[:, Q_LORA:Q_LORA + KV_LORA] = (rkv * (gy - ckvn * mean)).astype(BF16)
        du_ref[:, Q_LORA + KV_LORA:Q_LORA + KV_LORA + LANES] = dkr.astype(BF16)
        du_ref[:, Q_LORA + KV_LORA + LANES:MLA_COLS] = jnp.zeros((ts, LANES), BF16)

    def full(a):
        return pl.BlockSpec(a.shape, lambda i: (0,) * a.ndim)

    wide = pl.BlockSpec((ts, Q_PAD), lambda i: (i, 0))
    return pl.pallas_call(
        body, name="mla_bwd", grid=(n,),
        in_specs=[wide, wide, pl.BlockSpec((ts, D_ATTN), lambda i: (i, 0)),
                  pl.BlockSpec((ts, Q_LORA), lambda i: (i, U_CQ // Q_LORA)),
                  pl.BlockSpec((ts, KV_LORA), lambda i: (i, U_CKV // KV_LORA)),
                  pl.BlockSpec((ts, KR_PAD), lambda i: (i, U_KR // KR_PAD)),
                  pl.BlockSpec((ts, 1), lambda i: (i, 0)), full(freq), full(q_a_g), full(wq), full(kv_a_g), full(wkn),
                  full(wv), full(q_g), full(k_g)],
        out_specs=[pl.BlockSpec((ts, MLA_COLS), lambda i: (i, 0)), pl.BlockSpec((Q_LORA, Q_PAD), lambda i: (0, 0)),
                   pl.BlockSpec((KV_LORA, D_ATTN), lambda i: (0, 0)), pl.BlockSpec((KV_LORA, D_ATTN), lambda i: (0, 0)),
                   pl.BlockSpec((8, SG_COLS), lambda i: (0, 0))],
        out_shape=[SDS((seq, MLA_COLS), BF16), SDS((Q_LORA, Q_PAD), F32), SDS((KV_LORA, D_ATTN), F32),
                   SDS((KV_LORA, D_ATTN), F32), SDS((8, SG_COLS), F32)],
        scratch_shapes=[pltpu.VMEM((ts, Q_PAD), BF16), pltpu.VMEM((ts, D_ATTN), BF16)],
        compiler_params=_cp(("arbitrary",), 56),
    )(dq, dk, dv, u, u, u, pos, freq, q_a_g, wq, kv_a_g, wkn, wv, q_g, k_g)


def _conv_bwd(dycat, u, conv_w):
    seq = u.shape[0]
    ts = min(ROW_T, seq)
    n = seq // ts
    hb = ts // HALO

    def body(dy_ref, xc_ref, bc_ref, cc_ref, zc_ref, xp_ref, cp_ref, dyn_ref, bn_ref, zn_ref, w_ref,
             du_ref, dw_ref, ext_ref):
        i = pl.program_id(0)

        @pl.when(i == 0)
        def _():
            dw_ref[...] = jnp.zeros_like(dw_ref)

        xc = xc_ref[...].astype(F32)
        cc = cc_ref[...].astype(F32)
        uc = cc * xc
        up = jnp.where(i > 0, cp_ref[...].astype(F32) * xp_ref[...].astype(F32), 0.0)
        u1, u2 = _conv_taps(uc, up, ext_ref, ts, True)
        w0, w1, w2 = w_ref[0:1, :], w_ref[1:2, :], w_ref[2:3, :]
        conv = w0 * u2 + w1 * u1 + w2 * uc
        z = zc_ref[...].astype(F32)
        sg = _sigmoid(z)
        sz = z * sg
        b = bc_ref[...].astype(F32)
        dy = dy_ref[...].astype(F32)
        du_ref[:, 3 * D_CONV:4 * D_CONV] = (dy * (b * conv) * _silu_grad(z, sg)).astype(BF16)
        du_ref[:, D_CONV:2 * D_CONV] = (dy * sz * conv).astype(BF16)
        dconv = dy * sz * b
        dw_ref[0:1, :] += jnp.sum(dconv * u2, axis=0, keepdims=True)
        dw_ref[1:2, :] += jnp.sum(dconv * u1, axis=0, keepdims=True)
        dw_ref[2:3, :] += jnp.sum(dconv * uc, axis=0, keepdims=True)
        zn = zn_ref[...].astype(F32)
        dnext = dyn_ref[...].astype(F32) * (zn * _sigmoid(zn)) * bn_ref[...].astype(F32)
        dnext = jnp.where(i < n - 1, dnext, 0.0)
        d1, d2 = _conv_taps(dconv, dnext, ext_ref, ts, False)
        du = w2 * dconv + w1 * d1 + w0 * d2
        du_ref[:, 2 * D_CONV:3 * D_CONV] = (du * xc).astype(BF16)
        du_ref[:, 0:D_CONV] = (du * cc).astype(BF16)

    def col(cb):
        return pl.BlockSpec((ts, D_CONV), lambda i: (i, cb))

    def prev(cb):
        return pl.BlockSpec((HALO, D_CONV), lambda i: (jnp.maximum(i * hb - 1, 0), cb))

    def nxt(cb):
        return pl.BlockSpec((HALO, D_CONV), lambda i: (jnp.minimum((i + 1) * hb, n * hb - 1), cb))

    return pl.pallas_call(
        body, name="conv_bwd", grid=(n,),
        in_specs=[col(0), col(0), col(1), col(2), col(3), prev(0), prev(2), nxt(0), nxt(1), nxt(3),
                  pl.BlockSpec((3, D_CONV), lambda i: (0, 0))],
        out_specs=[pl.BlockSpec((ts, 4 * D_CONV), lambda i: (i, 0)), pl.BlockSpec((8, D_CONV), lambda i: (0, 0))],
        out_shape=[SDS((seq, 4 * D_CONV), BF16), SDS((8, D_CONV), F32)],
        scratch_shapes=[pltpu.VMEM((ts + HALO, D_CONV), F32)],
        compiler_params=_cp(("arbitrary",), 48),
    )(dycat, u, u, u, u, u, u, dycat, u, u, conv_w)


def _inproj_bwd(du_conv, du_za, du_mla, w_my):
    seq = du_conv.shape[0]
    dm = w_my.shape[0]
    tm, tn = min(DH_TM, seq), DH_TN

    def body(dc_ref, dz_ref, dm_ref, w_ref, o_ref):
        acc = _nt(dc_ref[...], w_ref[:, 0:U_ZA])
        acc = acc + _nt(dz_ref[...], w_ref[:, U_ZA:U_CQ])
        acc = acc + _nt(dm_ref[...], w_ref[:, U_CQ:U_COLS])
        o_ref[...] = acc

    return pl.pallas_call(
        body, name="inproj_bwd", grid=(seq // tm, dm // tn),
        in_specs=[pl.BlockSpec((tm, U_ZA), lambda i, j: (i, 0)), pl.BlockSpec((tm, D_ATTN), lambda i, j: (i, 0)),
                  pl.BlockSpec((tm, MLA_COLS), lambda i, j: (i, 0)), pl.BlockSpec((tn, U_COLS), lambda i, j: (j, 0))],
        out_specs=pl.BlockSpec((tm, tn), lambda i, j: (i, j)), out_shape=SDS((seq, dm), F32),
        compiler_params=_cp(("arbitrary", "arbitrary"), 48),
    )(du_conv, du_za, du_mla, w_my)


def _prenorm_bwd(x, dh, dout, norm_g, scale):
    seq, dm = x.shape
    ts = min(ROW_T, seq)
    n = seq // ts

    def body(x_ref, dh_ref, dout_ref, g_ref, sc_ref, gx_ref, st_ref, acc_ref):
        i = pl.program_id(0)

        @pl.when(i == 0)
        def _():
            acc_ref[...] = jnp.zeros_like(acc_ref)

        xv = x_ref[...]
        xn, r = _rms(xv, dm)
        dh_v = dh_ref[...]
        gv = g_ref[...]
        one_sc = 1.0 + sc_ref[...]

        def fold(a):
            return jnp.sum(a.reshape(ts // 8, 8, dm), axis=0)

        acc_ref[0:8, :] += fold(dh_v)
        acc_ref[8:16, :] += fold(dh_v * (xn * gv))
        dxg = dh_v * one_sc
        acc_ref[16:24, :] += fold(dxg * xn)
        dxn = dxg * gv
        mean = jnp.sum(dxn * xn, axis=-1, keepdims=True) * (1.0 / dm)
        gx_ref[...] = dout_ref[...] + r * (dxn - xn * mean)

        @pl.when(i == n - 1)
        def _():
            st_ref[...] = jnp.zeros_like(st_ref)
            for k in range(3):
                st_ref[k:k + 1, :] = jnp.sum(acc_ref[8 * k:8 * k + 8, :], axis=0, keepdims=True)

    row = pl.BlockSpec((ts, dm), lambda i: (i, 0))
    vec = pl.BlockSpec((1, dm), lambda i: (0, 0))
    return pl.pallas_call(
        body, name="prenorm_bwd", grid=(n,), in_specs=[row, row, row, vec, vec],
        out_specs=[row, pl.BlockSpec((8, dm), lambda i: (0, 0))],
        out_shape=[SDS((seq, dm), F32), SDS((8, dm), F32)],
        scratch_shapes=[pltpu.VMEM((24, dm), F32)], input_output_aliases={2: 0},
        compiler_params=_cp(("arbitrary",), 52),
    )(x, dh, dout, norm_g, scale)


def _unshard_cols(g):
    return jnp.transpose(g, (1, 0, 2)).reshape(g.shape[1], -1)


def _shard_cols(w):
    r = w.shape[0]
    return jnp.transpose(w.reshape(r, N_CHIPS, -1), (1, 0, 2))


def _w_in_to_my(w):
    c4 = 4 * D_CONV
    cq, ckv, kr, za = c4, c4 + Q_LORA, c4 + Q_LORA + KV_LORA, c4 + Q_LORA + KV_LORA + QK_ROPE
    pad = jnp.zeros((w.shape[0], KR_PAD - QK_ROPE), w.dtype)
    return jnp.concatenate([w[:, :c4], w[:, za:], w[:, cq:ckv], w[:, ckv:kr], w[:, kr:za], pad], axis=1)


def _w_in_from_my(g_conv, g_za, g_mla):
    return jnp.concatenate([g_conv, g_mla[:, :Q_LORA + KV_LORA + QK_ROPE], g_za], axis=1)


def _heads_pad(w):
    r = w.shape[0]
    w3 = w.reshape(r, N_HEADS, QK_HEAD)
    return jnp.pad(w3, ((0, 0), (0, 0), (0, HEAD_PAD - QK_HEAD))).reshape(r, Q_PAD)


def _heads_unpad(w):
    r = w.shape[0]
    return w.reshape(r, N_HEADS, HEAD_PAD)[:, :, :QK_HEAD].reshape(r, N_HEADS * QK_HEAD)


def kernel(x, c, positions, ada_w, ada_b, norm_g, w_in, conv_w, q_a_g, w_q_b, kv_a_g, w_kv_b, q_g, k_g, w_out, loss_target, m_ada_w, m_ada_b, m_norm_g, m_w_in, m_conv_w, m_q_a_g, m_w_q_b, m_kv_a_g, m_w_kv_b, m_q_g, m_k_g, m_w_out, v_ada_w, v_ada_b, v_norm_g, v_w_in, v_conv_w, v_q_a_g, v_w_q_b, v_kv_a_g, v_w_kv_b, v_q_g, v_k_g, v_w_out):
    mx, my, mc = _place()
    chip = 2 * mx + my
    me = 2 * chip + mc
    seq = x.shape[1]
    x2, t2 = x[0], loss_target[0]
    cw_cols = conv_w.shape[2]

    small = jnp.zeros((8, D_MODEL), F32)
    small = small.at[0].set(c[0])
    small = small.at[1:4, :cw_cols].set(conv_w[0])
    small_all = _gather8(small, "gather_c_conv", False)[0]
    c_all = small_all[:, 0, :]
    conv_full = jnp.transpose(small_all.reshape(N_CHIPS, 2, 8, D_MODEL)[:, 0, 1:4, :cw_cols], (1, 0, 2)).reshape(3, D_CONV)

    ada_cols = ada_w.shape[2]
    b_k = lax.dynamic_slice(ada_b, (0, chip * ada_cols), (1, ada_cols))
    mod_k, sc_all = _ada_mod(c_all, ada_w[0], b_k)
    mod_all = _gather8(mod_k, "gather_mod", False)[0]
    mod_row = lax.dynamic_slice(mod_all.reshape(N_CHIPS, 2, N_DEV, ada_cols), (0, mc, me, 0), (N_CHIPS, 1, 1, ada_cols))
    mod_row = mod_row.reshape(3, D_MODEL)
    shift, scale, gate = mod_row[0:1], mod_row[1:2], mod_row[2:3]

    shards = [w_in[0].astype(BF16), w_q_b[0].astype(BF16), w_kv_b[0].astype(BF16), w_out[0].astype(BF16)]
    g_in, g_q, g_kv, g_out = _allgather_shards(shards)
    w_my = _w_in_to_my(_unshard_cols(g_in))
    wq = _heads_pad(_unshard_cols(g_q))
    wkv = _unshard_cols(g_kv).reshape(KV_LORA, N_HEADS, QK_NOPE + V_HEAD)
    wkn = wkv[:, :, :QK_NOPE].reshape(KV_LORA, N_HEADS * QK_NOPE)
    wv = wkv[:, :, QK_NOPE:].reshape(KV_LORA, D_ATTN)
    wo = g_out.reshape(N_CHIPS * g_out.shape[1], D_MODEL)

    h, u = _inproj(x2, norm_g, scale, shift, w_my)
    y_conv = _conv_fwd(u, conv_full)
    pos = positions.reshape(seq, 1)
    inv_freq = ROPE_BASE ** (-jnp.arange(0, QK_ROPE, 2, dtype=F32) / QK_ROPE)
    freq = jnp.concatenate([inv_freq, inv_freq, jnp.zeros((LANES - QK_ROPE,), F32)]).reshape(1, LANES)
    q_g_pad = jnp.pad(q_g, ((0, 0), (0, HEAD_PAD - QK_HEAD)))
    k_g_pad = jnp.pad(k_g, ((0, 0), (0, HEAD_PAD - QK_HEAD)))
    q, k, v = _mla_prep(u, pos, freq, q_a_g, wq, kv_a_g, wkn, wv, q_g_pad, k_g_pad)
    o, y_attn, lse = _flash_fwd(q, k, v, u)
    dout, dy, dycat, st_out = _outproj_loss(y_conv, y_attn, x2, t2, gate, wo)

    ycat = jnp.concatenate([y_conv, y_attn], axis=1)
    dw_out = _matmul_tn(ycat, dy, "dw_out")
    do, du_za, delta = _attn_gate_bwd(dycat, o, u)
    dq, dk, dv = _flash_bwd(q, k, v, do, lse, delta)
    du_mla, dwq, dwkn, dwv, sg_mla = _mla_bwd(dq, dk, dv, u, pos, freq, q_a_g, wq, kv_a_g, wkn, wv, q_g_pad, k_g_pad)
    du_conv, dconv_w = _conv_bwd(dycat, u, conv_full)
    dh = _inproj_bwd(du_conv, du_za, du_mla, w_my)
    grad_x, st_in = _prenorm_bwd(x2, dh, dout, norm_g, scale)
    dw_conv = _matmul_tn(h, du_conv, "dw_in_conv")
    dw_za = _matmul_tn(h, du_za, "dw_in_za")
    dw_mla = _matmul_tn(h, du_mla, "dw_in_mla")

    sgrad = jnp.zeros((8, D_MODEL), F32)
    sgrad = sgrad.at[0:2].set(st_in[0:2])
    sgrad = sgrad.at[2].set(st_out[0])
    sgrad = sgrad.at[3].set(st_in[2])
    sgrad = sgrad.at[4, :D_CONV].set(dconv_w[0]).at[4, D_CONV:].set(dconv_w[1])
    sgrad = sgrad.at[5, :D_CONV].set(dconv_w[2]).at[5, D_CONV:].set(sg_mla[0, :D_CONV])
    sgrad = sgrad.at[6, :HEAD_PAD].set(sg_mla[0, SG_KG:SG_KG + HEAD_PAD])
    sgrad = sgrad.at[7].set(st_out[1])
    sg_all, sg_sum = _gather8(sgrad, "gather_small_grads", True)
    loss = sg_sum[7, 0]
    g_ada_b = sg_sum[0:3].reshape(1, 3 * D_MODEL)
    g_norm_g = sg_sum[3:4]
    conv_sum = jnp.stack([sg_sum[4, :D_CONV], sg_sum[4, D_CONV:], sg_sum[5, :D_CONV]])
    g_conv_w = lax.dynamic_slice(conv_sum, (0, chip * cw_cols), (3, cw_cols))
    g_q_a_g = sg_sum[5:6, D_CONV + SG_QAG:D_CONV + SG_QAG + Q_LORA]
    g_kv_a_g = sg_sum[5:6, D_CONV + SG_KVAG:D_CONV + SG_KVAG + KV_LORA]
    g_q_g = sg_sum[5:6, D_CONV + SG_QG:D_CONV + SG_QG + QK_HEAD]
    g_k_g = sg_sum[6:7, :QK_HEAD]
    dmod_k = lax.dynamic_slice(sg_all[:, 0:3, :].reshape(N_DEV, 3 * D_MODEL), (0, chip * ada_cols), (N_DEV, ada_cols))

    dw_in_nat = _w_in_from_my(dw_conv, dw_za, dw_mla)
    dw_q_nat = _heads_unpad(dwq).astype(BF16)
    dw_kv_nat = jnp.concatenate([dwkn.reshape(KV_LORA, N_HEADS, QK_NOPE), dwv.reshape(KV_LORA, N_HEADS, V_HEAD)],
                                axis=2).reshape(KV_LORA, N_HEADS * (QK_NOPE + V_HEAD)).astype(BF16)
    grads = [_shard_cols(dw_in_nat), _shard_cols(dw_q_nat), _shard_cols(dw_kv_nat),
             dw_out.reshape(N_CHIPS, dw_out.shape[0] // N_CHIPS, D_MODEL)]
    mine, theirs = _rs_core_swap(grads)
    names = ["w_in", "w_q_b", "w_kv_b", "w_out"]
    parts = [_add_bf16(a, b, "rs_add_" + nm) for a, b, nm in zip(mine, theirs, names)]
    recv = _rs_chip_exchange(parts)
    halves = [_sum_chips(p, "rs_sum_" + nm) for p, nm in zip(recv, names)]
    joined = _rs_core_join(halves)
    g_big = [j.reshape(2 * j.shape[1], j.shape[2]) for j in joined]

    g_ada_w, d_ada_w, nm_ada_w, nv_ada_w = _ada_w_update(sc_all, dmod_k, ada_w[0], m_ada_w[0], v_ada_w[0])
    upd = {}
    big = {"w_in": (w_in, m_w_in, v_w_in), "w_q_b": (w_q_b, m_w_q_b, v_w_q_b), "w_kv_b": (w_kv_b, m_w_kv_b, v_w_kv_b),
           "w_out": (w_out, m_w_out, v_w_out)}
    for nm, g in zip(names, g_big):
        w_, m_, v_ = big[nm]
        upd[nm] = (g,) + tuple(_adamw(w_[0], g, m_[0], v_[0], "adamw_" + nm))
    small_w = {"ada_b": (ada_b, m_ada_b, v_ada_b, g_ada_b), "norm_g": (norm_g, m_norm_g, v_norm_g, g_norm_g),
               "conv_w": (conv_w[0], m_conv_w[0], v_conv_w[0], g_conv_w), "q_a_g": (q_a_g, m_q_a_g, v_q_a_g, g_q_a_g),
               "kv_a_g": (kv_a_g, m_kv_a_g, v_kv_a_g, g_kv_a_g), "q_g": (q_g, m_q_g, v_q_g, g_q_g),
               "k_g": (k_g, m_k_g, v_k_g, g_k_g)}
    for nm, (w_, m_, v_, g) in small_w.items():
        upd[nm] = (g,) + tuple(_adamw(w_, g, m_, v_, "adamw_" + nm))
    upd["ada_w"] = (g_ada_w, d_ada_w, nm_ada_w, nv_ada_w)

    order = ["ada_w", "ada_b", "norm_g", "w_in", "conv_w", "q_a_g", "w_q_b", "kv_a_g", "w_kv_b", "q_g", "k_g", "w_out"]
    lead1 = {"ada_w", "w_in", "conv_w", "w_q_b", "w_kv_b", "w_out"}

    def shaped(nm, a):
        return a[None] if nm in lead1 else a

    outs = [loss, grad_x[None]]
    for idx in range(4):
        outs += [shaped(nm, upd[nm][idx]) for nm in order]
    return tuple(outs)
```

```python
import functools
import math

import jax
import jax.numpy as jnp
from jax import lax
from jax.experimental import pallas as pl
from jax.experimental.pallas import tpu as pltpu

F32 = jnp.float32
BF16 = jnp.bfloat16
MESH = pl.DeviceIdType.MESH
SDS = jax.ShapeDtypeStruct
ANY = pl.BlockSpec(memory_space=pl.ANY)

D_MODEL = 2048
D_CONV = 1024
N_HEADS = 8
QK_NOPE = 128
QK_ROPE = 64
QK_HEAD = QK_NOPE + QK_ROPE
V_HEAD = 128
D_ATTN = N_HEADS * V_HEAD
Q_LORA = 512
KV_LORA = 256
ROPE_BASE = 10000.0
EPS = 1e-6
LOG2E = math.log2(math.e)
LN2 = math.log(2.0)
ADAM_LR, ADAM_B1, ADAM_B2, ADAM_EPS, ADAM_WD, ADAM_STEP = 0.001, 0.9, 0.999, 1e-08, 0.01, 10
N_CHIPS = 4
N_DEV = 8

LANES = 128
V7X_VMEM_BYTES = 64 * 1024 * 1024
MIB = 1024 * 1024

HEAD_PAD = 256
Q_PAD = N_HEADS * HEAD_PAD
U_ZA = 4 * D_CONV
U_CQ = U_ZA + D_ATTN
U_CKV = U_CQ + Q_LORA
U_KR = U_CKV + KV_LORA
KR_PAD = 256
U_COLS = U_KR + KR_PAD
MLA_COLS = Q_LORA + KV_LORA + KR_PAD

ATT_T = 512
INPROJ_TM, INPROJ_TN = 1024, 512
ROW_T = 512
OUT_T = 256
DH_TM, DH_TN = 512, 512
TN_TM, TN_TN, TN_TK = 1024, 1024, 512


def _cp(sem=None, vmem_mib=None, **kw):
    if sem is not None:
        kw["dimension_semantics"] = sem
    if vmem_mib is not None:
        kw["vmem_limit_bytes"] = min(vmem_mib * MIB, V7X_VMEM_BYTES - 4 * MIB)
    return pltpu.CompilerParams(**kw)


def _sigmoid(z):
    return 1.0 / (1.0 + jnp.exp(-z))


def _silu_grad(z, sg):
    return sg * (1.0 + z * (1.0 - sg))


def _nt(a, b):
    return lax.dot_general(a, b, (((1,), (1,)), ((), ())), preferred_element_type=F32)


def _tn(a, b):
    return lax.dot_general(a, b, (((0,), (0,)), ((), ())), preferred_element_type=F32)


def _nn(a, b):
    return jnp.dot(a, b, preferred_element_type=F32)


def _place():
    return lax.axis_index("x"), lax.axis_index("y"), lax.axis_index("c")


def _gather8(v, name, with_sum):
    rows, cols = v.shape

    def body(v_ref, out_ref, *rest):
        if with_sum:
            sum_ref, send_sems, recv_sems = rest
        else:
            send_sems, recv_sems = rest
        mx, my, mc = _place()
        me = 4 * mx + 2 * my + mc
        out_ref[me] = v_ref[...]
        peers = []
        for d in range(1, N_DEV):
            px = 1 - mx if d & 4 else mx
            py = 1 - my if d & 2 else my
            pc = 1 - mc if d & 1 else mc
            peers.append((px, py, pc))

        def copy(d, slot, to):
            return pltpu.make_async_remote_copy(
                src_ref=v_ref, dst_ref=out_ref.at[slot], send_sem=send_sems.at[d], recv_sem=recv_sems.at[d],
                device_id=to, device_id_type=MESH)

        sends = [copy(d, me, p) for d, p in enumerate(peers)]
        for cp in sends:
            cp.start()
        for d, (px, py, pc) in enumerate(peers):
            copy(d, 4 * px + 2 * py + pc, (px, py, pc)).wait_recv()
        for cp in sends:
            cp.wait_send()
        if with_sum:
            acc = out_ref[0]
            for b in range(1, N_DEV):
                acc = acc + out_ref[b]
            sum_ref[...] = acc

    out_shape = [SDS((N_DEV, rows, cols), F32)]
    if with_sum:
        out_shape.append(SDS((rows, cols), F32))
    vm = pl.BlockSpec(memory_space=pltpu.VMEM)
    return pl.pallas_call(
        body, name=name, out_shape=out_shape, in_specs=[vm], out_specs=[vm] * len(out_shape),
        scratch_shapes=[pltpu.SemaphoreType.DMA((N_DEV - 1,)), pltpu.SemaphoreType.DMA((N_DEV - 1,))],
    )(v)


def _chips_of(mx, my):
    chips = [(mx, 1 - my), (1 - mx, my), (1 - mx, 1 - my)]
    return chips, [2 * px + py for px, py in chips]


def _allgather_shards(shards):
    na = len(shards)
    halves = [s.shape[0] // 2 for s in shards]

    def body(*refs):
        ins, outs = refs[:na], refs[na:2 * na]
        s1, r1, s2, r2 = refs[2 * na:]
        mx, my, mc = _place()
        k = 2 * mx + my
        sib = (mx, my, 1 - mc)
        chips, kks = _chips_of(mx, my)

        def half(a, slot, c):
            return outs[a].at[slot, pl.ds(c * halves[a], halves[a])]

        def mine(a):
            return ins[a].at[pl.ds(mc * halves[a], halves[a])]

        sends = []
        for a in range(na):
            for d, (px, py) in enumerate(chips):
                cp = pltpu.make_async_remote_copy(
                    src_ref=mine(a), dst_ref=half(a, k, mc), send_sem=s1.at[3 * a + d], recv_sem=r1.at[3 * a + d],
                    device_id=(px, py, mc), device_id_type=MESH)
                cp.start()
                sends.append(cp)
        for a in range(na):
            for d, (px, py) in enumerate(chips):
                pltpu.make_async_remote_copy(
                    src_ref=mine(a), dst_ref=half(a, kks[d], mc), send_sem=s1.at[3 * a + d], recv_sem=r1.at[3 * a + d],
                    device_id=(px, py, mc), device_id_type=MESH).wait_recv()
                cp = pltpu.make_async_remote_copy(
                    src_ref=half(a, kks[d], mc), dst_ref=half(a, kks[d], mc), send_sem=s2.at[3 * a + d],
                    recv_sem=r2.at[3 * a + d], device_id=sib, device_id_type=MESH)
                cp.start()
                sends.append(cp)
        for a in range(na):
            for d in range(3):
                pltpu.make_async_remote_copy(
                    src_ref=half(a, kks[d], 1 - mc), dst_ref=half(a, kks[d], 1 - mc), send_sem=s2.at[3 * a + d],
                    recv_sem=r2.at[3 * a + d], device_id=sib, device_id_type=MESH).wait_recv()
        for cp in sends:
            cp.wait_send()

    return pl.pallas_call(
        body, name="allgather_weights",
        out_shape=[SDS((N_CHIPS,) + s.shape, s.dtype) for s in shards],
        in_specs=[ANY] * na, out_specs=[ANY] * na,
        scratch_shapes=[pltpu.SemaphoreType.DMA((3 * na,))] * 4,
    )(*shards)


def _rs_core_swap(grads):
    na = len(grads)
    halves = [g.shape[1] // 2 for g in grads]

    def body(*refs):
        ins, outs = refs[:na], refs[na:2 * na]
        ssem, rsem = refs[2 * na:]
        mx, my, mc = _place()
        sib = (mx, my, 1 - mc)
        sends = []
        for a in range(na):
            cp = pltpu.make_async_remote_copy(
                src_ref=ins[a].at[:, pl.ds((1 - mc) * halves[a], halves[a])], dst_ref=outs[a],
                send_sem=ssem.at[a], recv_sem=rsem.at[a], device_id=sib, device_id_type=MESH)
            cp.start()
            sends.append(cp)
        for cp in sends:
            cp.wait_recv()
        for cp in sends:
            cp.wait_send()

    return pl.pallas_call(
        body, name="rs_core_swap", out_shape=[SDS((N_CHIPS, h) + g.shape[2:], g.dtype) for g, h in zip(grads, halves)],
        in_specs=[ANY] * na, out_specs=[ANY] * na,
        scratch_shapes=[pltpu.SemaphoreType.DMA((na,))] * 2,
    )(*grads)


def _rs_chip_exchange(parts):
    na = len(parts)

    def body(*refs):
        ins, outs = refs[:na], refs[na:2 * na]
        ssem, rsem = refs[2 * na:]
        mx, my, mc = _place()
        k = 2 * mx + my
        chips, kks = _chips_of(mx, my)
        sends = []
        for a in range(na):
            for d, (px, py) in enumerate(chips):
                cp = pltpu.make_async_remote_copy(
                    src_ref=ins[a].at[kks[d]], dst_ref=outs[a].at[k], send_sem=ssem.at[3 * a + d],
                    recv_sem=rsem.at[3 * a + d], device_id=(px, py, mc), device_id_type=MESH)
                cp.start()
                sends.append(cp)
        for a in range(na):
            for d, (px, py) in enumerate(chips):
                pltpu.make_async_remote_copy(
                    src_ref=ins[a].at[kks[d]], dst_ref=outs[a].at[kks[d]], send_sem=ssem.at[3 * a + d],
                    recv_sem=rsem.at[3 * a + d], device_id=(px, py, mc), device_id_type=MESH).wait_recv()
        for cp in sends:
            cp.wait_send()

    return pl.pallas_call(
        body, name="rs_chip_exchange", out_shape=[SDS(p.shape, p.dtype) for p in parts],
        in_specs=[ANY] * na, out_specs=[ANY] * na,
        scratch_shapes=[pltpu.SemaphoreType.DMA((3 * na,))] * 2,
    )(*parts)


def _rs_core_join(halves):
    na = len(halves)

    def body(*refs):
        ins, outs = refs[:na], refs[na:2 * na]
        ssem, rsem = refs[2 * na:]
        mx, my, mc = _place()
        sib = (mx, my, 1 - mc)
        sends = []
        for a in range(na):
            cp = pltpu.make_async_remote_copy(
                src_ref=ins[a], dst_ref=outs[a].at[mc], send_sem=ssem.at[a], recv_sem=rsem.at[a],
                device_id=sib, device_id_type=MESH)
            cp.start()
            sends.append(cp)
        for a in range(na):
            pltpu.make_async_remote_copy(
                src_ref=ins[a], dst_ref=outs[a].at[1 - mc], send_sem=ssem.at[a], recv_sem=rsem.at[a],
                device_id=sib, device_id_type=MESH).wait_recv()
        for cp in sends:
            cp.wait_send()

    return pl.pallas_call(
        body, name="rs_core_join", out_shape=[SDS((2,) + h.shape, h.dtype) for h in halves],
        in_specs=[ANY] * na, out_specs=[ANY] * na,
        scratch_shapes=[pltpu.SemaphoreType.DMA((na,))] * 2,
    )(*halves)


def _row_tile(rows, limit):
    if rows <= limit:
        return rows
    best = None
    for t in range(16, limit + 1, 16):
        if rows % t == 0:
            best = t
    assert best is not None, rows
    return best


def _add_bf16(a, b, name):
    shape = a.shape
    a2, b2 = a.reshape(-1, shape[-1]), b.reshape(-1, shape[-1])
    rows, cols = a2.shape
    tb = _row_tile(rows, 512)

    def body(a_ref, b_ref, o_ref):
        o_ref[...] = (a_ref[...].astype(F32) + b_ref[...].astype(F32)).astype(BF16)

    spec = pl.BlockSpec((tb, cols), lambda i: (i, 0))
    out = pl.pallas_call(
        body, name=name, grid=(rows // tb,), in_specs=[spec, spec], out_specs=spec, out_shape=SDS((rows, cols), BF16),
        compiler_params=_cp(("arbitrary",)),
    )(a2, b2)
    return out.reshape(shape)


def _sum_chips(p, name):
    _, rows, cols = p.shape
    tb = _row_tile(rows, 256)

    def body(p_ref, o_ref):
        acc = p_ref[0].astype(F32)
        for j in range(1, N_CHIPS):
            acc = acc + p_ref[j].astype(F32)
        o_ref[...] = acc

    return pl.pallas_call(
        body, name=name, grid=(rows // tb,),
        in_specs=[pl.BlockSpec((N_CHIPS, tb, cols), lambda i: (0, i, 0))],
        out_specs=pl.BlockSpec((tb, cols), lambda i: (i, 0)), out_shape=SDS((rows, cols), F32),
        compiler_params=_cp(("arbitrary",)),
    )(p)


def _adamw_math(w, g, m, v):
    m2 = ADAM_B1 * m + (1.0 - ADAM_B1) * g
    v2 = ADAM_B2 * v + (1.0 - ADAM_B2) * (g * g)
    m_hat = m2 / (1.0 - ADAM_B1 ** ADAM_STEP)
    v_hat = v2 / (1.0 - ADAM_B2 ** ADAM_STEP)
    delta = -ADAM_LR * (m_hat / (jnp.sqrt(v_hat) + ADAM_EPS) + ADAM_WD * w)
    return delta, m2, v2


def _adamw(w, g, m, v, name):
    rows, cols = w.shape
    tb = _row_tile(rows, 256)

    def body(w_ref, g_ref, m_ref, v_ref, d_ref, m2_ref, v2_ref):
        d, m2, v2 = _adamw_math(w_ref[...], g_ref[...], m_ref[...], v_ref[...])
        d_ref[...] = d
        m2_ref[...] = m2
        v2_ref[...] = v2

    spec = pl.BlockSpec((tb, cols), lambda i: (i, 0))
    return pl.pallas_call(
        body, name=name, grid=(rows // tb,), in_specs=[spec] * 4, out_specs=[spec] * 3,
        out_shape=[SDS((rows, cols), F32)] * 3, compiler_params=_cp(("arbitrary",), 40),
    )(w, g, m, v)


def _ada_w_update(sc_all, dmod_k, w, m, v):
    rows, cols = w.shape
    tb = 256

    def body(s_ref, dm_ref, w_ref, m_ref, v_ref, g_ref, d_ref, m2_ref, v2_ref):
        g = _tn(s_ref[...].astype(BF16), dm_ref[...].astype(BF16))
        d, m2, v2 = _adamw_math(w_ref[...], g, m_ref[...], v_ref[...])
        g_ref[...] = g
        d_ref[...] = d
        m2_ref[...] = m2
        v2_ref[...] = v2

    spec = pl.BlockSpec((tb, cols), lambda i: (i, 0))
    return pl.pallas_call(
        body, name="ada_w_update", grid=(rows // tb,),
        in_specs=[pl.BlockSpec((N_DEV, tb), lambda i: (0, i)), pl.BlockSpec((N_DEV, cols), lambda i: (0, 0)), spec, spec, spec],
        out_specs=[spec] * 4, out_shape=[SDS((rows, cols), F32)] * 4, compiler_params=_cp(("arbitrary",), 40),
    )(sc_all, dmod_k, w, m, v)


def _ada_mod(c_all, w, b_k):
    rows, cols = w.shape
    tn = 512

    def body(c_ref, w_ref, b_ref, o_ref, s_ref):
        cv = c_ref[...]
        s = cv * _sigmoid(cv)
        s_ref[...] = s
        o_ref[...] = _nn(s.astype(BF16), w_ref[...].astype(BF16)) + b_ref[...]

    return pl.pallas_call(
        body, name="ada_mod", grid=(cols // tn,),
        in_specs=[pl.BlockSpec((N_DEV, rows), lambda j: (0, 0)), pl.BlockSpec((rows, tn), lambda j: (0, j)),
                  pl.BlockSpec((1, tn), lambda j: (0, j))],
        out_specs=[pl.BlockSpec((N_DEV, tn), lambda j: (0, j)), pl.BlockSpec((N_DEV, rows), lambda j: (0, 0))],
        out_shape=[SDS((N_DEV, cols), F32), SDS((N_DEV, rows), F32)], compiler_params=_cp(("arbitrary",)),
    )(c_all, w, b_k)


def _inproj(x, norm_g, scale, shift, w_my):
    seq, dm = x.shape
    ncols = w_my.shape[1]
    tm, tn = min(INPROJ_TM, seq), INPROJ_TN

    def body(x_ref, g_ref, sc_ref, sh_ref, w_ref, h_ref, u_ref):
        @pl.when(pl.program_id(1) == 0)
        def _():
            xv = x_ref[...]
            r = lax.rsqrt(jnp.mean(xv * xv, axis=-1, keepdims=True) + EPS)
            hv = (xv * r * g_ref[...]) * (1.0 + sc_ref[...]) + sh_ref[...]
            h_ref[...] = hv.astype(BF16)

        u_ref[...] = _nn(h_ref[...], w_ref[...]).astype(BF16)

    vec = pl.BlockSpec((1, dm), lambda i, j: (0, 0))
    return pl.pallas_call(
        body, name="inproj", grid=(seq // tm, ncols // tn),
        in_specs=[pl.BlockSpec((tm, dm), lambda i, j: (i, 0)), vec, vec, vec, pl.BlockSpec((dm, tn), lambda i, j: (0, j))],
        out_specs=[pl.BlockSpec((tm, dm), lambda i, j: (i, 0)), pl.BlockSpec((tm, tn), lambda i, j: (i, j))],
        out_shape=[SDS((seq, dm), BF16), SDS((seq, ncols), BF16)],
        compiler_params=_cp(("arbitrary", "arbitrary"), 48),
    )(x, norm_g, scale, shift, w_my)


HALO = 16


def _conv_taps(uc, halo, ext_ref, ts, causal):
    if causal:
        ext_ref[0:HALO, :] = halo
        ext_ref[HALO:HALO + ts, :] = uc
        return ext_ref[pl.ds(HALO - 1, ts), :], ext_ref[pl.ds(HALO - 2, ts), :]
    ext_ref[0:ts, :] = uc
    ext_ref[ts:ts + HALO, :] = halo
    return ext_ref[pl.ds(1, ts), :], ext_ref[pl.ds(2, ts), :]


def _conv_fwd(u, conv_w):
    seq = u.shape[0]
    ts = min(ROW_T, seq)
    hb = ts // HALO

    def body(xc_ref, bc_ref, cc_ref, zc_ref, xp_ref, cp_ref, w_ref, y_ref, ext_ref):
        i = pl.program_id(0)
        uc = cc_ref[...].astype(F32) * xc_ref[...].astype(F32)
        up = cp_ref[...].astype(F32) * xp_ref[...].astype(F32)
        up = jnp.where(i > 0, up, 0.0)
        u1, u2 = _conv_taps(uc, up, ext_ref, ts, True)
        conv = w_ref[0:1, :] * u2 + w_ref[1:2, :] * u1 + w_ref[2:3, :] * uc
        z = zc_ref[...].astype(F32)
        y_ref[...] = ((bc_ref[...].astype(F32) * conv) * (z * _sigmoid(z))).astype(BF16)

    def col(cb):
        return pl.BlockSpec((ts, D_CONV), lambda i: (i, cb))

    def prev(cb):
        return pl.BlockSpec((HALO, D_CONV), lambda i: (jnp.maximum(i * hb - 1, 0), cb))

    return pl.pallas_call(
        body, name="conv_fwd", grid=(seq // ts,),
        in_specs=[col(0), col(1), col(2), col(3), prev(0), prev(2), pl.BlockSpec((3, D_CONV), lambda i: (0, 0))],
        out_specs=pl.BlockSpec((ts, D_CONV), lambda i: (i, 0)), out_shape=SDS((seq, D_CONV), BF16),
        scratch_shapes=[pltpu.VMEM((ts + HALO, D_CONV), F32)],
        compiler_params=_cp(("arbitrary",), 40),
    )(u, u, u, u, u, u, conv_w)


def _rope_tables(pos_ref, freq_ref):
    ang = pos_ref[...].astype(F32) * freq_ref[...]
    lane = lax.broadcasted_iota(jnp.int32, ang.shape, 1)
    cs, sn = jnp.cos(ang), jnp.sin(ang)
    half = QK_ROPE // 2
    cos_t = jnp.where(lane < QK_ROPE, cs, 0.0)
    sin_lo = jnp.where(lane < half, sn, 0.0)
    sin_hi = jnp.where((lane >= half) & (lane < QK_ROPE), sn, 0.0)
    return cos_t, sin_lo, sin_hi


def _rope(blk, tables):
    cos_t, sin_lo, sin_hi = tables
    half = QK_ROPE // 2
    return blk * cos_t - pltpu.roll(blk, LANES - half, 1) * sin_lo + pltpu.roll(blk, half, 1) * sin_hi


def _rope_bwd(g, tables):
    cos_t, sin_lo, sin_hi = tables
    half = QK_ROPE // 2
    return g * cos_t + pltpu.roll(g, LANES - half, 1) * sin_lo - pltpu.roll(g, half, 1) * sin_hi


def _rms(v, n):
    r = lax.rsqrt(jnp.sum(v * v, axis=-1, keepdims=True) * (1.0 / n) + EPS)
    return v * r, r


def _mla_prep(u, pos, freq, q_a_g, wq, kv_a_g, wkn, wv, q_g, k_g):
    seq = u.shape[0]
    ts = min(ROW_T, seq)
    qscale = LOG2E / math.sqrt(QK_HEAD)

    def body(cq_ref, ckv_ref, kr_ref, pos_ref, freq_ref, qag_ref, wq_ref, kvag_ref, wkn_ref, wv_ref, qg_ref, kg_ref,
             q_ref, k_ref, v_ref):
        tables = _rope_tables(pos_ref, freq_ref)
        cqn, _ = _rms(cq_ref[...].astype(F32), Q_LORA)
        qp = _nn((cqn * qag_ref[...]).astype(BF16), wq_ref[...])
        qg = qg_ref[...]
        for h in range(N_HEADS):
            lo = h * HEAD_PAD
            qn, _ = _rms(qp[:, lo:lo + HEAD_PAD], QK_HEAD)
            qn = qn * qg
            q_ref[:, lo:lo + LANES] = (qn[:, :LANES] * qscale).astype(BF16)
            q_ref[:, lo + LANES:lo + HEAD_PAD] = (_rope(qn[:, LANES:], tables) * qscale).astype(BF16)
        ckvn, _ = _rms(ckv_ref[...].astype(F32), KV_LORA)
        ckvb = (ckvn * kvag_ref[...]).astype(BF16)
        kn = _nn(ckvb, wkn_ref[...])
        v_ref[...] = _nn(ckvb, wv_ref[...]).astype(BF16)
        kr = kr_ref[:, 0:LANES].astype(F32)
        ssr = jnp.sum(kr * kr, axis=-1, keepdims=True)
        kg = kg_ref[...]
        for h in range(N_HEADS):
            knh = kn[:, h * QK_NOPE:(h + 1) * QK_NOPE]
            r = lax.rsqrt((jnp.sum(knh * knh, axis=-1, keepdims=True) + ssr) * (1.0 / QK_HEAD) + EPS)
            lo = h * HEAD_PAD
            k_ref[:, lo:lo + LANES] = (knh * r * kg[:, :LANES]).astype(BF16)
            k_ref[:, lo + LANES:lo + HEAD_PAD] = _rope(kr * r * kg[:, LANES:], tables).astype(BF16)

    def full(a):
        return pl.BlockSpec(a.shape, lambda i: (0,) * a.ndim)

    return pl.pallas_call(
        body, name="mla_prep", grid=(seq // ts,),
        in_specs=[pl.BlockSpec((ts, Q_LORA), lambda i: (i, U_CQ // Q_LORA)),
                  pl.BlockSpec((ts, KV_LORA), lambda i: (i, U_CKV // KV_LORA)),
                  pl.BlockSpec((ts, KR_PAD), lambda i: (i, U_KR // KR_PAD)),
                  pl.BlockSpec((ts, 1), lambda i: (i, 0)), full(freq), full(q_a_g), full(wq), full(kv_a_g), full(wkn),
                  full(wv), full(q_g), full(k_g)],
        out_specs=[pl.BlockSpec((ts, Q_PAD), lambda i: (i, 0)), pl.BlockSpec((ts, Q_PAD), lambda i: (i, 0)),
                   pl.BlockSpec((ts, D_ATTN), lambda i: (i, 0))],
        out_shape=[SDS((seq, Q_PAD), BF16), SDS((seq, Q_PAD), BF16), SDS((seq, D_ATTN), BF16)],
        compiler_params=_cp(("arbitrary",), 48),
    )(u, u, u, pos, freq, q_a_g, wq, kv_a_g, wkn, wv, q_g, k_g)


def _causal_mask(t, tq, q0):
    return lax.broadcasted_iota(jnp.int32, (t, tq), 0) <= lax.broadcasted_iota(jnp.int32, (t, tq), 1) + q0


def _flash_fwd(q, k, v, u):
    seq = q.shape[0]
    t = min(ATT_T, seq)
    za_blk = U_ZA // V_HEAD

    def body(q_ref, k_ref, v_ref, z_ref, o_ref, y_ref, lse_ref, s_a, s_b, m_ref, l_ref, acc_ref):
        i = pl.program_id(1)
        ones = jnp.ones((16, t), BF16)

        def keys(j):
            return pl.ds(pl.multiple_of(j * t, t), t)

        def scores(j, s_ref):
            s_ref[...] = _nt(k_ref[keys(j), :], q_ref[...])

        def absorb(j, s_ref, masked):
            s = s_ref[...]
            if masked:
                s = jnp.where(_causal_mask(t, t, 0), s, -jnp.inf)
            m = m_ref[...]
            m_new = jnp.maximum(m, jnp.max(s, axis=0, keepdims=True))
            alpha = jnp.exp2(m - m_new)
            p = jnp.exp2((s - m_new).astype(BF16))
            m_ref[...] = m_new
            l_ref[...] = alpha * l_ref[...] + _nn(ones, p)[0:1, :]
            acc_ref[...] = alpha * acc_ref[...] + _tn(v_ref[keys(j), :], p)

        scores(0, s_a)
        m_ref[...] = jnp.full_like(m_ref, -jnp.inf)
        l_ref[...] = jnp.zeros_like(l_ref)
        acc_ref[...] = jnp.zeros_like(acc_ref)

        def step(j, carry):
            @pl.when(j % 2 == 0)
            def _():
                scores(j + 1, s_b)
                absorb(j, s_a, False)

            @pl.when(j % 2 == 1)
            def _():
                scores(j + 1, s_a)
                absorb(j, s_b, False)

            return carry

        lax.fori_loop(0, i, step, 0)

        @pl.when(i % 2 == 0)
        def _():
            absorb(i, s_a, True)

        @pl.when(i % 2 == 1)
        def _():
            absorb(i, s_b, True)

        l = l_ref[...]
        o = (acc_ref[...] * (1.0 / l)).T
        lse_ref[...] = m_ref[...] + jnp.log2(l)
        o_ref[...] = o.astype(BF16)
        z = z_ref[...].astype(F32)
        y_ref[...] = (o * (z * _sigmoid(z))).astype(BF16)

    tile = pl.BlockSpec((t, V_HEAD), lambda h, i: (i, h))
    return pl.pallas_call(
        body, name="flash_fwd", grid=(N_HEADS, seq // t),
        in_specs=[pl.BlockSpec((t, HEAD_PAD), lambda h, i: (i, h)), pl.BlockSpec((seq, HEAD_PAD), lambda h, i: (0, h)),
                  pl.BlockSpec((seq, V_HEAD), lambda h, i: (0, h)), pl.BlockSpec((t, V_HEAD), lambda h, i: (i, za_blk + h))],
        out_specs=[tile, tile, pl.BlockSpec((None, 1, t), lambda h, i: (h, 0, i))],
        out_shape=[SDS((seq, D_ATTN), BF16), SDS((seq, D_ATTN), BF16), SDS((N_HEADS, 1, seq), F32)],
        scratch_shapes=[pltpu.VMEM((t, t), F32), pltpu.VMEM((t, t), F32), pltpu.VMEM((1, t), F32), pltpu.VMEM((1, t), F32),
                        pltpu.VMEM((V_HEAD, t), F32)],
        compiler_params=_cp(("arbitrary", "arbitrary"), 40),
    )(q, k, v, u)


def _outproj_loss(y_conv, y_attn, x, target, gate, w_out):
    seq, dm = x.shape
    ts = min(OUT_T, seq)
    n = seq // ts
    dmix = w_out.shape[0]

    def body(yc_ref, ya_ref, x_ref, t_ref, gate_ref, wo_hbm, dout_ref, dy_ref, dyc_ref, stats_ref, wo_ref, sem, acc_ref):
        i = pl.program_id(0)

        @pl.when(i == 0)
        def _():
            cp = pltpu.make_async_copy(wo_hbm, wo_ref, sem)
            cp.start()
            cp.wait()
            acc_ref[...] = jnp.zeros_like(acc_ref)

        y = _nn(yc_ref[...], wo_ref[0:D_CONV, :]) + _nn(ya_ref[...], wo_ref[D_CONV:dmix, :])
        gate_v = gate_ref[...]
        diff = (x_ref[...] + gate_v * y) - t_ref[...]
        dout = diff * (1.0 / dm)
        dout_ref[...] = dout
        acc_ref[0:8, :] += jnp.sum((dout * y).reshape(ts // 8, 8, dm), axis=0)
        acc_ref[8:16, :] += jnp.sum((diff * diff).reshape(ts // 8, 8, dm), axis=0)
        dy = (dout * gate_v).astype(BF16)
        dy_ref[...] = dy
        dyc_ref[...] = _nt(dy, wo_ref[...]).astype(BF16)

        @pl.when(i == n - 1)
        def _():
            stats_ref[...] = jnp.zeros_like(stats_ref)
            stats_ref[0:1, :] = jnp.sum(acc_ref[0:8, :], axis=0, keepdims=True)
            loss = jnp.sum(acc_ref[8:16, :]) * (0.5 / dm)
            stats_ref[1:2, :] = jnp.full((1, dm), loss, F32)

    row = pl.BlockSpec((ts, dm), lambda i: (i, 0))
    half = pl.BlockSpec((ts, D_CONV), lambda i: (i, 0))
    return pl.pallas_call(
        body, name="outproj_loss", grid=(n,),
        in_specs=[half, half, row, row, pl.BlockSpec((1, dm), lambda i: (0, 0)), ANY],
        out_specs=[row, row, pl.BlockSpec((ts, dmix), lambda i: (i, 0)), pl.BlockSpec((8, dm), lambda i: (0, 0))],
        out_shape=[SDS((seq, dm), F32), SDS((seq, dm), BF16), SDS((seq, dmix), BF16), SDS((8, dm), F32)],
        scratch_shapes=[pltpu.VMEM(w_out.shape, BF16), pltpu.SemaphoreType.DMA(()), pltpu.VMEM((16, dm), F32)],
        compiler_params=_cp(("arbitrary",), 52),
    )(y_conv, y_attn, x, target, gate, w_out)


def _matmul_tn(a, b, name):
    seq, m = a.shape
    n = b.shape[1]
    tm, tn, tk = min(TN_TM, m), min(TN_TN, n), min(TN_TK, seq)
    nk = seq // tk

    def body(a_ref, b_ref, o_ref, acc_ref):
        kk = pl.program_id(2)

        @pl.when(kk == 0)
        def _():
            acc_ref[...] = jnp.zeros_like(acc_ref)

        acc_ref[...] += _tn(a_ref[...], b_ref[...])

        @pl.when(kk == nk - 1)
        def _():
            o_ref[...] = acc_ref[...].astype(BF16)

    return pl.pallas_call(
        body, name=name, grid=(m // tm, n // tn, nk),
        in_specs=[pl.BlockSpec((tk, tm), lambda i, j, kk: (kk, i)), pl.BlockSpec((tk, tn), lambda i, j, kk: (kk, j))],
        out_specs=pl.BlockSpec((tm, tn), lambda i, j, kk: (i, j)), out_shape=SDS((m, n), BF16),
        scratch_shapes=[pltpu.VMEM((tm, tn), F32)],
        compiler_params=_cp(("arbitrary", "arbitrary", "arbitrary"), 40),
    )(a, b)


def _attn_gate_bwd(dycat, o, u):
    seq = o.shape[0]
    ts = min(ROW_T, seq)

    def body(dy_ref, o_ref, z_ref, dot_ref, dz_ref, dl_ref):
        dy = dy_ref[...].astype(F32)
        ov = o_ref[...].astype(F32)
        z = z_ref[...].astype(F32)
        sg = _sigmoid(z)
        do = dy * (z * sg)
        dz_ref[...] = (dy * ov * _silu_grad(z, sg)).astype(BF16)
        prod = do * ov
        ones = jnp.ones((8, V_HEAD), F32)
        for h in range(N_HEADS):
            cols = slice(h * V_HEAD, (h + 1) * V_HEAD)
            dot_ref[h] = do[:, cols].T.astype(BF16)
            rows = lax.dot_general(ones, prod[:, cols], (((1,), (1,)), ((), ())), precision=lax.Precision.HIGHEST,
                                   preferred_element_type=F32)
            dl_ref[h] = rows[0:1, :]

    blk = pl.BlockSpec((ts, D_ATTN), lambda i: (i, 0))
    return pl.pallas_call(
        body, name="attn_gate_bwd", grid=(seq // ts,),
        in_specs=[pl.BlockSpec((ts, D_ATTN), lambda i: (i, 1)), blk, pl.BlockSpec((ts, D_ATTN), lambda i: (i, U_ZA // D_ATTN))],
        out_specs=[pl.BlockSpec((N_HEADS, V_HEAD, ts), lambda i: (0, 0, i)), blk,
                   pl.BlockSpec((N_HEADS, 1, ts), lambda i: (0, 0, i))],
        out_shape=[SDS((N_HEADS, V_HEAD, seq), BF16), SDS((seq, D_ATTN), BF16), SDS((N_HEADS, 1, seq), F32)],
        compiler_params=_cp(("arbitrary",), 40),
    )(dycat, o, u)


def _flash_bwd(q, k, v, do_t, lse, delta):
    seq = q.shape[0]
    t = min(ATT_T, seq)
    n = seq // t

    def body(k_ref, v_ref, q_ref, dot_ref, lse_ref, dl_ref, dq_ref, dk_ref, dv_ref, s_a, s_b, dp_a, dp_b, dvt_ref):
        j = pl.program_id(1)

        @pl.when(j == 0)
        def _():
            dq_ref[...] = jnp.zeros_like(dq_ref)

        def rows(i):
            return pl.ds(pl.multiple_of(i * t, t), t)

        def products(i, s_ref, dp_ref):
            s_ref[...] = _nt(k_ref[...], q_ref[rows(i), :])
            dp_ref[...] = _nn(v_ref[...], dot_ref[:, rows(i)])

        def absorb(i, s_ref, dp_ref, masked):
            p = jnp.exp2((s_ref[...] - lse_ref[:, rows(i)]).astype(BF16))
            if masked:
                p = jnp.where(_causal_mask(t, t, 0), p, jnp.zeros_like(p))
            dvt_ref[...] += _nt(dot_ref[:, rows(i)], p)
            ds = p * (dp_ref[...] - dl_ref[:, rows(i)]).astype(BF16)
            dk_ref[...] += _nn(ds, q_ref[rows(i), :])
            dq_ref[rows(i), :] += _tn(ds, k_ref[...])

        dk_ref[...] = jnp.zeros_like(dk_ref)
        dvt_ref[...] = jnp.zeros_like(dvt_ref)
        products(j, s_a, dp_a)
        left = n - 1 - j

        @pl.when(left > 0)
        def _():
            products(j + 1, s_b, dp_b)
            absorb(j, s_a, dp_a, True)

        @pl.when(left == 0)
        def _():
            absorb(j, s_a, dp_a, True)

        def step(r, carry):
            i = j + 1 + r

            @pl.when(r % 2 == 0)
            def _():
                products(i + 1, s_a, dp_a)
                absorb(i, s_b, dp_b, False)

            @pl.when(r % 2 == 1)
            def _():
                products(i + 1, s_b, dp_b)
                absorb(i, s_a, dp_a, False)

            return carry

        lax.fori_loop(0, left - 1, step, 0)

        @pl.when((left > 0) & (left % 2 == 1))
        def _():
            absorb(n - 1, s_b, dp_b, False)

        @pl.when((left > 0) & (left % 2 == 0))
        def _():
            absorb(n - 1, s_a, dp_a, False)

        dv_ref[...] = dvt_ref[...].T.astype(BF16)

    row = pl.BlockSpec((None, 1, seq), lambda h, j: (h, 0, 0))
    return pl.pallas_call(
        body, name="flash_bwd", grid=(N_HEADS, n),
        in_specs=[pl.BlockSpec((t, HEAD_PAD), lambda h, j: (j, h)), pl.BlockSpec((t, V_HEAD), lambda h, j: (j, h)),
                  pl.BlockSpec((seq, HEAD_PAD), lambda h, j: (0, h)), pl.BlockSpec((None, V_HEAD, seq), lambda h, j: (h, 0, 0)),
                  row, row],
        out_specs=[pl.BlockSpec((seq, HEAD_PAD), lambda h, j: (0, h)), pl.BlockSpec((t, HEAD_PAD), lambda h, j: (j, h)),
                   pl.BlockSpec((t, V_HEAD), lambda h, j: (j, h))],
        out_shape=[SDS((seq, Q_PAD), F32), SDS((seq, Q_PAD), F32), SDS((seq, D_ATTN), BF16)],
        scratch_shapes=[pltpu.VMEM((t, t), F32)] * 4 + [pltpu.VMEM((V_HEAD, t), F32)],
        compiler_params=_cp(("arbitrary", "arbitrary"), 56),
    )(k, v, q, do_t, lse, delta)


SG_QAG, SG_KVAG, SG_QG, SG_KG, SG_COLS = 0, Q_LORA, Q_LORA + KV_LORA, Q_LORA + KV_LORA + HEAD_PAD, D_MODEL


def _mla_bwd(dq, dk, dv, u, pos, freq, q_a_g, wq, kv_a_g, wkn, wv, q_g, k_g):
    seq = u.shape[0]
    ts = min(ROW_T, seq)
    n = seq // ts
    qscale = 1.0 / math.sqrt(QK_HEAD)

    def body(dq_ref, dk_ref, dv_ref, cq_ref, ckv_ref, kr_ref, pos_ref, freq_ref, qag_ref, wq_ref, kvag_ref, wkn_ref,
             wv_ref, qg_ref, kg_ref, du_ref, dwq_ref, dwkn_ref, dwv_ref, sg_ref, dqp_ref, dkn_ref):
        i = pl.program_id(0)

        @pl.when(i == 0)
        def _():
            dwq_ref[...] = jnp.zeros_like(dwq_ref)
            dwkn_ref[...] = jnp.zeros_like(dwkn_ref)
            dwv_ref[...] = jnp.zeros_like(dwv_ref)
            sg_ref[...] = jnp.zeros_like(sg_ref)

        tables = _rope_tables(pos_ref, freq_ref)

        cq = cq_ref[...].astype(F32)
        cqn, rq = _rms(cq, Q_LORA)
        qag = qag_ref[...]
        cqb = (cqn * qag).astype(BF16)
        qp = _nn(cqb, wq_ref[...])
        qg = qg_ref[...]
        dqg = jnp.zeros((1, HEAD_PAD), F32)
        for h in range(N_HEADS):
            lo = h * HEAD_PAD
            xn, r = _rms(qp[:, lo:lo + HEAD_PAD], QK_HEAD)
            g = jnp.concatenate([dq_ref[:, lo:lo + LANES], _rope_bwd(dq_ref[:, lo + LANES:lo + HEAD_PAD], tables)],
                                axis=-1) * qscale
            dqg = dqg + jnp.sum(g * xn, axis=0, keepdims=True)
            gy = g * qg
            mean = jnp.sum(gy * xn, axis=-1, keepdims=True) * (1.0 / QK_HEAD)
            dqp_ref[:, lo:lo + HEAD_PAD] = (r * (gy - xn * mean)).astype(BF16)
        dqp = dqp_ref[...]
        dwq_ref[...] += _tn(cqb, dqp)
        dcqn = _nt(dqp, wq_ref[...])
        sg_ref[0:1, SG_QAG:SG_QAG + Q_LORA] += jnp.sum(dcqn * cqn, axis=0, keepdims=True)
        sg_ref[0:1, SG_QG:SG_QG + HEAD_PAD] += dqg
        gy = dcqn * qag
        mean = jnp.sum(gy * cqn, axis=-1, keepdims=True) * (1.0 / Q_LORA)
        du_ref[:, 0:Q_LORA] = (rq * (gy - cqn * mean)).astype(BF16)

        ckv = ckv_ref[...].astype(F32)
        ckvn, rkv = _rms(ckv, KV_LORA)
        kvag = kvag_ref[...]
        ckvb = (ckvn * kvag).astype(BF16)
        kn = _nn(ckvb, wkn_ref[...])
        kr = kr_ref[:, 0:LANES].astype(F32)
        ssr = jnp.sum(kr * kr, axis=-1, keepdims=True)
        kg = kg_ref[...]
        kg_n, kg_r = kg[:, :LANES] * LN2, kg[:, LANES:] * LN2
        dkg_n = jnp.zeros((1, LANES), F32)
        dkg_r = jnp.zeros((1, LANES), F32)
        dkr = jnp.zeros((ts, LANES), F32)
        for h in range(N_HEADS):
            knh = kn[:, h * QK_NOPE:(h + 1) * QK_NOPE]
            r = lax.rsqrt((jnp.sum(knh * knh, axis=-1, keepdims=True) + ssr) * (1.0 / QK_HEAD) + EPS)
            xn_n, xn_r = knh * r, kr * r
            lo = h * HEAD_PAD
            g_n = dk_ref[:, lo:lo + LANES]
            g_r = _rope_bwd(dk_ref[:, lo + LANES:lo + HEAD_PAD], tables)
            dkg_n = dkg_n + jnp.sum(g_n * xn_n, axis=0, keepdims=True)
            dkg_r = dkg_r + jnp.sum(g_r * xn_r, axis=0, keepdims=True)
            gy_n, gy_r = g_n * kg_n, g_r * kg_r
            mean = (jnp.sum(gy_n * xn_n, axis=-1, keepdims=True) + jnp.sum(gy_r * xn_r, axis=-1, keepdims=True)) * (1.0 / QK_HEAD)
            dkn_ref[:, h * QK_NOPE:(h + 1) * QK_NOPE] = (r * (gy_n - xn_n * mean)).astype(BF16)
            dkr = dkr + r * (gy_r - xn_r * mean)
        dkn = dkn_ref[...]
        dvv = dv_ref[...]
        dwkn_ref[...] += _tn(ckvb, dkn)
        dwv_ref[...] += _tn(ckvb, dvv)
        dckvn = _nt(dkn, wkn_ref[...]) + _nt(dvv, wv_ref[...])
        sg_ref[0:1, SG_KVAG:SG_KVAG + KV_LORA] += jnp.sum(dckvn * ckvn, axis=0, keepdims=True)
        sg_ref[0:1, SG_KG:SG_KG + LANES] += dkg_n * LN2
        sg_ref[0:1, SG_KG + LANES:SG_KG + HEAD_PAD] += dkg_r * LN2
        gy = dckvn * kvag
        mean = jnp.sum(gy * ckvn, axis=-1, keepdims=True) * (1.0 / KV_LORA)
        du_ref[:, Q_LORA:Q_LORA + KV_LORA] = (rkv * (gy - ckvn * mean)).astype(BF16)
        du_ref[:, Q_LORA + KV_LORA:Q_LORA + KV_LORA + LANES] = dkr.astype(BF16)
        du_ref[:, Q_LORA + KV_LORA + LANES:MLA_COLS] = jnp.zeros((ts, LANES), BF16)

    def full(a):
        return pl.BlockSpec(a.shape, lambda i: (0,) * a.ndim)

    wide = pl.BlockSpec((ts, Q_PAD), lambda i: (i, 0))
    return pl.pallas_call(
        body, name="mla_bwd", grid=(n,),
        in_specs=[wide, wide, pl.BlockSpec((ts, D_ATTN), lambda i: (i, 0)),
                  pl.BlockSpec((ts, Q_LORA), lambda i: (i, U_CQ // Q_LORA)),
                  pl.BlockSpec((ts, KV_LORA), lambda i: (i, U_CKV // KV_LORA)),
                  pl.BlockSpec((ts, KR_PAD), lambda i: (i, U_KR // KR_PAD)),
                  pl.BlockSpec((ts, 1), lambda i: (i, 0)), full(freq), full(q_a_g), full(wq), full(kv_a_g), full(wkn),
                  full(wv), full(q_g), full(k_g)],
        out_specs=[pl.BlockSpec((ts, MLA_COLS), lambda i: (i, 0)), pl.BlockSpec((Q_LORA, Q_PAD), lambda i: (0, 0)),
                   pl.BlockSpec((KV_LORA, D_ATTN), lambda i: (0, 0)), pl.BlockSpec((KV_LORA, D_ATTN), lambda i: (0, 0)),
                   pl.BlockSpec((8, SG_COLS), lambda i: (0, 0))],
        out_shape=[SDS((seq, MLA_COLS), BF16), SDS((Q_LORA, Q_PAD), F32), SDS((KV_LORA, D_ATTN), F32),
                   SDS((KV_LORA, D_ATTN), F32), SDS((8, SG_COLS), F32)],
        scratch_shapes=[pltpu.VMEM((ts, Q_PAD), BF16), pltpu.VMEM((ts, D_ATTN), BF16)],
        compiler_params=_cp(("arbitrary",), 56),
    )(dq, dk, dv, u, u, u, pos, freq, q_a_g, wq, kv_a_g, wkn, wv, q_g, k_g)


def _conv_bwd(dycat, u, conv_w):
    seq = u.shape[0]
    ts = min(ROW_T, seq)
    n = seq // ts
    hb = ts // HALO

    def body(dy_ref, xc_ref, bc_ref, cc_ref, zc_ref, xp_ref, cp_ref, dyn_ref, bn_ref, zn_ref, w_ref,
             du_ref, dw_ref, ext_ref):
        i = pl.program_id(0)

        @pl.when(i == 0)
        def _():
            dw_ref[...] = jnp.zeros_like(dw_ref)

        xc = xc_ref[...].astype(F32)
        cc = cc_ref[...].astype(F32)
        uc = cc * xc
        up = jnp.where(i > 0, cp_ref[...].astype(F32) * xp_ref[...].astype(F32), 0.0)
        u1, u2 = _conv_taps(uc, up, ext_ref, ts, True)
        w0, w1, w2 = w_ref[0:1, :], w_ref[1:2, :], w_ref[2:3, :]
        conv = w0 * u2 + w1 * u1 + w2 * uc
        z = zc_ref[...].astype(F32)
        sg = _sigmoid(z)
        sz = z * sg
        b = bc_ref[...].astype(F32)
        dy = dy_ref[...].astype(F32)
        du_ref[:, 3 * D_CONV:4 * D_CONV] = (dy * (b * conv) * _silu_grad(z, sg)).astype(BF16)
        du_ref[:, D_CONV:2 * D_CONV] = (dy * sz * conv).astype(BF16)
        dconv = dy * sz * b
        dw_ref[0:1, :] += jnp.sum(dconv * u2, axis=0, keepdims=True)
        dw_ref[1:2, :] += jnp.sum(dconv * u1, axis=0, keepdims=True)
        dw_ref[2:3, :] += jnp.sum(dconv * uc, axis=0, keepdims=True)
        zn = zn_ref[...].astype(F32)
        dnext = dyn_ref[...].astype(F32) * (zn * _sigmoid(zn)) * bn_ref[...].astype(F32)
        dnext = jnp.where(i < n - 1, dnext, 0.0)
        d1, d2 = _conv_taps(dconv, dnext, ext_ref, ts, False)
        du = w2 * dconv + w1 * d1 + w0 * d2
        du_ref[:, 2 * D_CONV:3 * D_CONV] = (du * xc).astype(BF16)
        du_ref[:, 0:D_CONV] = (du * cc).astype(BF16)

    def col(cb):
        return pl.BlockSpec((ts, D_CONV), lambda i: (i, cb))

    def prev(cb):
        return pl.BlockSpec((HALO, D_CONV), lambda i: (jnp.maximum(i * hb - 1, 0), cb))

    def nxt(cb):
        return pl.BlockSpec((HALO, D_CONV), lambda i: (jnp.minimum((i + 1) * hb, n * hb - 1), cb))

    return pl.pallas_call(
        body, name="conv_bwd", grid=(n,),
        in_specs=[col(0), col(0), col(1), col(2), col(3), prev(0), prev(2), nxt(0), nxt(1), nxt(3),
                  pl.BlockSpec((3, D_CONV), lambda i: (0, 0))],
        out_specs=[pl.BlockSpec((ts, 4 * D_CONV), lambda i: (i, 0)), pl.BlockSpec((8, D_CONV), lambda i: (0, 0))],
        out_shape=[SDS((seq, 4 * D_CONV), BF16), SDS((8, D_CONV), F32)],
        scratch_shapes=[pltpu.VMEM((ts + HALO, D_CONV), F32)],
        compiler_params=_cp(("arbitrary",), 48),
    )(dycat, u, u, u, u, u, u, dycat, u, u, conv_w)


def _inproj_bwd(du_conv, du_za, du_mla, w_my):
    seq = du_conv.shape[0]
    dm = w_my.shape[0]
    tm, tn = min(DH_TM, seq), DH_TN

    def body(dc_ref, dz_ref, dm_ref, w_ref, o_ref):
        acc = _nt(dc_ref[...], w_ref[:, 0:U_ZA])
        acc = acc + _nt(dz_ref[...], w_ref[:, U_ZA:U_CQ])
        acc = acc + _nt(dm_ref[...], w_ref[:, U_CQ:U_COLS])
        o_ref[...] = acc

    return pl.pallas_call(
        body, name="inproj_bwd", grid=(seq // tm, dm // tn),
        in_specs=[pl.BlockSpec((tm, U_ZA), lambda i, j: (i, 0)), pl.BlockSpec((tm, D_ATTN), lambda i, j: (i, 0)),
                  pl.BlockSpec((tm, MLA_COLS), lambda i, j: (i, 0)), pl.BlockSpec((tn, U_COLS), lambda i, j: (j, 0))],
        out_specs=pl.BlockSpec((tm, tn), lambda i, j: (i, j)), out_shape=SDS((seq, dm), F32),
        compiler_params=_cp(("arbitrary", "arbitrary"), 48),
    )(du_conv, du_za, du_mla, w_my)


def _prenorm_bwd(x, dh, dout, norm_g, scale):
    seq, dm = x.shape
    ts = min(ROW_T, seq)
    n = seq // ts

    def body(x_ref, dh_ref, dout_ref, g_ref, sc_ref, gx_ref, st_ref, acc_ref):
        i = pl.program_id(0)

        @pl.when(i == 0)
        def _():
            acc_ref[...] = jnp.zeros_like(acc_ref)

        xv = x_ref[...]
        xn, r = _rms(xv, dm)
        dh_v = dh_ref[...]
        gv = g_ref[...]
        one_sc = 1.0 + sc_ref[...]

        def fold(a):
            return jnp.sum(a.reshape(ts // 8, 8, dm), axis=0)

        acc_ref[0:8, :] += fold(dh_v)
        acc_ref[8:16, :] += fold(dh_v * (xn * gv))
        dxg = dh_v * one_sc
        acc_ref[16:24, :] += fold(dxg * xn)
        dxn = dxg * gv
        mean = jnp.sum(dxn * xn, axis=-1, keepdims=True) * (1.0 / dm)
        gx_ref[...] = dout_ref[...] + r * (dxn - xn * mean)

        @pl.when(i == n - 1)
        def _():
            st_ref[...] = jnp.zeros_like(st_ref)
            for k in range(3):
                st_ref[k:k + 1, :] = jnp.sum(acc_ref[8 * k:8 * k + 8, :], axis=0, keepdims=True)

    row = pl.BlockSpec((ts, dm), lambda i: (i, 0))
    vec = pl.BlockSpec((1, dm), lambda i: (0, 0))
    return pl.pallas_call(
        body, name="prenorm_bwd", grid=(n,), in_specs=[row, row, row, vec, vec],
        out_specs=[row, pl.BlockSpec((8, dm), lambda i: (0, 0))],
        out_shape=[SDS((seq, dm), F32), SDS((8, dm), F32)],
        scratch_shapes=[pltpu.VMEM((24, dm), F32)], input_output_aliases={2: 0},
        compiler_params=_cp(("arbitrary",), 52),
    )(x, dh, dout, norm_g, scale)


def _unshard_cols(g):
    return jnp.transpose(g, (1, 0, 2)).reshape(g.shape[1], -1)


def _shard_cols(w):
    r = w.shape[0]
    return jnp.transpose(w.reshape(r, N_CHIPS, -1), (1, 0, 2))


def _w_in_to_my(w):
    c4 = 4 * D_CONV
    cq, ckv, kr, za = c4, c4 + Q_LORA, c4 + Q_LORA + KV_LORA, c4 + Q_LORA + KV_LORA + QK_ROPE
    pad = jnp.zeros((w.shape[0], KR_PAD - QK_ROPE), w.dtype)
    return jnp.concatenate([w[:, :c4], w[:, za:], w[:, cq:ckv], w[:, ckv:kr], w[:, kr:za], pad], axis=1)


def _w_in_from_my(g_conv, g_za, g_mla):
    return jnp.concatenate([g_conv, g_mla[:, :Q_LORA + KV_LORA + QK_ROPE], g_za], axis=1)


def _heads_pad(w):
    r = w.shape[0]
    w3 = w.reshape(r, N_HEADS, QK_HEAD)
    return jnp.pad(w3, ((0, 0), (0, 0), (0, HEAD_PAD - QK_HEAD))).reshape(r, Q_PAD)


def _heads_unpad(w):
    r = w.shape[0]
    return w.reshape(r, N_HEADS, HEAD_PAD)[:, :, :QK_HEAD].reshape(r, N_HEADS * QK_HEAD)


def kernel(x, c, positions, ada_w, ada_b, norm_g, w_in, conv_w, q_a_g, w_q_b, kv_a_g, w_kv_b, q_g, k_g, w_out, loss_target, m_ada_w, m_ada_b, m_norm_g, m_w_in, m_conv_w, m_q_a_g, m_w_q_b, m_kv_a_g, m_w_kv_b, m_q_g, m_k_g, m_w_out, v_ada_w, v_ada_b, v_norm_g, v_w_in, v_conv_w, v_q_a_g, v_w_q_b, v_kv_a_g, v_w_kv_b, v_q_g, v_k_g, v_w_out):
    mx, my, mc = _place()
    chip = 2 * mx + my
    me = 2 * chip + mc
    seq = x.shape[1]
    x2, t2 = x[0], loss_target[0]
    cw_cols = conv_w.shape[2]

    small = jnp.zeros((8, D_MODEL), F32)
    small = small.at[0].set(c[0])
    small = small.at[1:4, :cw_cols].set(conv_w[0])
    small_all = _gather8(small, "gather_c_conv", False)[0]
    c_all = small_all[:, 0, :]
    conv_full = jnp.transpose(small_all.reshape(N_CHIPS, 2, 8, D_MODEL)[:, 0, 1:4, :cw_cols], (1, 0, 2)).reshape(3, D_CONV)

    ada_cols = ada_w.shape[2]
    b_k = lax.dynamic_slice(ada_b, (0, chip * ada_cols), (1, ada_cols))
    mod_k, sc_all = _ada_mod(c_all, ada_w[0], b_k)
    mod_all = _gather8(mod_k, "gather_mod", False)[0]
    mod_row = lax.dynamic_slice(mod_all.reshape(N_CHIPS, 2, N_DEV, ada_cols), (0, mc, me, 0), (N_CHIPS, 1, 1, ada_cols))
    mod_row = mod_row.reshape(3, D_MODEL)
    shift, scale, gate = mod_row[0:1], mod_row[1:2], mod_row[2:3]

    shards = [w_in[0].astype(BF16), w_q_b[0].astype(BF16), w_kv_b[0].astype(BF16), w_out[0].astype(BF16)]
    gathered = _allgather_shards(shards)
    g_in, g_q, g_kv, g_out = [lax.dynamic_update_slice(g, s[None], (chip, 0, 0)) for g, s in zip(gathered, shards)]
    w_my = _w_in_to_my(_unshard_cols(g_in))
    wq = _heads_pad(_unshard_cols(g_q))
    wkv = _unshard_cols(g_kv).reshape(KV_LORA, N_HEADS, QK_NOPE + V_HEAD)
    wkn = wkv[:, :, :QK_NOPE].reshape(KV_LORA, N_HEADS * QK_NOPE)
    wv = wkv[:, :, QK_NOPE:].reshape(KV_LORA, D_ATTN)
    wo = g_out.reshape(N_CHIPS * g_out.shape[1], D_MODEL)

    h, u = _inproj(x2, norm_g, scale, shift, w_my)
    y_conv = _conv_fwd(u, conv_full)
    pos = positions.reshape(seq, 1)
    inv_freq = ROPE_BASE ** (-jnp.arange(0, QK_ROPE, 2, dtype=F32) / QK_ROPE)
    freq = jnp.concatenate([inv_freq, inv_freq, jnp.zeros((LANES - QK_ROPE,), F32)]).reshape(1, LANES)
    q_g_pad = jnp.pad(q_g, ((0, 0), (0, HEAD_PAD - QK_HEAD)))
    k_g_pad = jnp.pad(k_g, ((0, 0), (0, HEAD_PAD - QK_HEAD)))
    q, k, v = _mla_prep(u, pos, freq, q_a_g, wq, kv_a_g, wkn, wv, q_g_pad, k_g_pad)
    o, y_attn, lse = _flash_fwd(q, k, v, u)
    dout, dy, dycat, st_out = _outproj_loss(y_conv, y_attn, x2, t2, gate, wo)

    ycat = jnp.concatenate([y_conv, y_attn], axis=1)
    dw_out = _matmul_tn(ycat, dy, "dw_out")
    do_t, du_za, delta = _attn_gate_bwd(dycat, o, u)
    dq, dk, dv = _flash_bwd(q, k, v, do_t, lse, delta)
    du_mla, dwq, dwkn, dwv, sg_mla = _mla_bwd(dq, dk, dv, u, pos, freq, q_a_g, wq, kv_a_g, wkn, wv, q_g_pad, k_g_pad)
    du_conv, dconv_w = _conv_bwd(dycat, u, conv_full)
    dh = _inproj_bwd(du_conv, du_za, du_mla, w_my)
    grad_x, st_in = _prenorm_bwd(x2, dh, dout, norm_g, scale)
    dw_conv = _matmul_tn(h, du_conv, "dw_in_conv")
    dw_za = _matmul_tn(h, du_za, "dw_in_za")
    dw_mla = _matmul_tn(h, du_mla, "dw_in_mla")

    sgrad = jnp.zeros((8, D_MODEL), F32)
    sgrad = sgrad.at[0:2].set(st_in[0:2])
    sgrad = sgrad.at[2].set(st_out[0])
    sgrad = sgrad.at[3].set(st_in[2])
    sgrad = sgrad.at[4, :D_CONV].set(dconv_w[0]).at[4, D_CONV:].set(dconv_w[1])
    sgrad = sgrad.at[5, :D_CONV].set(dconv_w[2]).at[5, D_CONV:].set(sg_mla[0, :D_CONV])
    sgrad = sgrad.at[6, :HEAD_PAD].set(sg_mla[0, SG_KG:SG_KG + HEAD_PAD])
    sgrad = sgrad.at[7].set(st_out[1])
    sg_all, sg_sum = _gather8(sgrad, "gather_small_grads", True)
    loss = sg_sum[7, 0]
    g_ada_b = sg_sum[0:3].reshape(1, 3 * D_MODEL)
    g_norm_g = sg_sum[3:4]
    conv_sum = jnp.stack([sg_sum[4, :D_CONV], sg_sum[4, D_CONV:], sg_sum[5, :D_CONV]])
    g_conv_w = lax.dynamic_slice(conv_sum, (0, chip * cw_cols), (3, cw_cols))
    g_q_a_g = sg_sum[5:6, D_CONV + SG_QAG:D_CONV + SG_QAG + Q_LORA]
    g_kv_a_g = sg_sum[5:6, D_CONV + SG_KVAG:D_CONV + SG_KVAG + KV_LORA]
    g_q_g = sg_sum[5:6, D_CONV + SG_QG:D_CONV + SG_QG + QK_HEAD]
    g_k_g = sg_sum[6:7, :QK_HEAD]
    dmod_k = lax.dynamic_slice(sg_all[:, 0:3, :].reshape(N_DEV, 3 * D_MODEL), (0, chip * ada_cols), (N_DEV, ada_cols))

    dw_in_nat = _w_in_from_my(dw_conv, dw_za, dw_mla)
    dw_q_nat = _heads_unpad(dwq).astype(BF16)
    dw_kv_nat = jnp.concatenate([dwkn.reshape(KV_LORA, N_HEADS, QK_NOPE), dwv.reshape(KV_LORA, N_HEADS, V_HEAD)],
                                axis=2).reshape(KV_LORA, N_HEADS * (QK_NOPE + V_HEAD)).astype(BF16)
    grads = [_shard_cols(dw_in_nat), _shard_cols(dw_q_nat), _shard_cols(dw_kv_nat),
             dw_out.reshape(N_CHIPS, dw_out.shape[0] // N_CHIPS, D_MODEL)]
    theirs = _rs_core_swap(grads)
    mine = [lax.dynamic_slice(g, (0, mc * (g.shape[1] // 2), 0), (N_CHIPS, g.shape[1] // 2, g.shape[2])) for g in grads]
    names = ["w_in", "w_q_b", "w_kv_b", "w_out"]
    parts = [_add_bf16(a, b, "rs_add_" + nm) for a, b, nm in zip(mine, theirs, names)]
    recv = _rs_chip_exchange(parts)
    recv = [lax.dynamic_update_slice(r, lax.dynamic_slice(p, (chip, 0, 0), (1,) + p.shape[1:]), (chip, 0, 0))
            for r, p in zip(recv, parts)]
    halves = [_sum_chips(p, "rs_sum_" + nm) for p, nm in zip(recv, names)]
    joined = _rs_core_join(halves)
    joined = [lax.dynamic_update_slice(j, hf[None], (mc, 0, 0)) for j, hf in zip(joined, halves)]
    g_big = [j.reshape(2 * j.shape[1], j.shape[2]) for j in joined]

    g_ada_w, d_ada_w, nm_ada_w, nv_ada_w = _ada_w_update(sc_all, dmod_k, ada_w[0], m_ada_w[0], v_ada_w[0])
    upd = {}
    big = {"w_in": (w_in, m_w_in, v_w_in), "w_q_b": (w_q_b, m_w_q_b, v_w_q_b), "w_kv_b": (w_kv_b, m_w_kv_b, v_w_kv_b),
           "w_out": (w_out, m_w_out, v_w_out)}
    for nm, g in zip(names, g_big):
        w_, m_, v_ = big[nm]
        upd[nm] = (g,) + tuple(_adamw(w_[0], g, m_[0], v_[0], "adamw_" + nm))
    small_w = {"ada_b": (ada_b, m_ada_b, v_ada_b, g_ada_b), "norm_g": (norm_g, m_norm_g, v_norm_g, g_norm_g),
               "conv_w": (conv_w[0], m_conv_w[0], v_conv_w[0], g_conv_w), "q_a_g": (q_a_g, m_q_a_g, v_q_a_g, g_q_a_g),
               "kv_a_g": (kv_a_g, m_kv_a_g, v_kv_a_g, g_kv_a_g), "q_g": (q_g, m_q_g, v_q_g, g_q_g),
               "k_g": (k_g, m_k_g, v_k_g, g_k_g)}
    for nm, (w_, m_, v_, g) in small_w.items():
        upd[nm] = (g,) + tuple(_adamw(w_, g, m_, v_, "adamw_" + nm))
    upd["ada_w"] = (g_ada_w, d_ada_w, nm_ada_w, nv_ada_w)

    order = ["ada_w", "ada_b", "norm_g", "w_in", "conv_w", "q_a_g", "w_q_b", "kv_a_g", "w_kv_b", "q_g", "k_g", "w_out"]
    lead1 = {"ada_w", "w_in", "conv_w", "w_q_b", "w_kv_b", "w_out"}

    def shaped(nm, a):
        return a[None] if nm in lead1 else a

    outs = [loss, grad_x[None]]
    for idx in range(4):
        outs += [shaped(nm, upd[nm][idx]) for nm in order]
    return tuple(outs)
```

```python
import functools
import math

import jax
import jax.numpy as jnp
from jax import lax
from jax.experimental import pallas as pl
from jax.experimental.pallas import tpu as pltpu

F32 = jnp.float32
BF16 = jnp.bfloat16
MESH = pl.DeviceIdType.MESH
SDS = jax.ShapeDtypeStruct
ANY = pl.BlockSpec(memory_space=pl.ANY)

D_MODEL = 2048
D_CONV = 1024
N_HEADS = 8
QK_NOPE = 128
QK_ROPE = 64
QK_HEAD = QK_NOPE + QK_ROPE
V_HEAD = 128
D_ATTN = N_HEADS * V_HEAD
Q_LORA = 512
KV_LORA = 256
ROPE_BASE = 10000.0
EPS = 1e-6
LOG2E = math.log2(math.e)
LN2 = math.log(2.0)
ADAM_LR, ADAM_B1, ADAM_B2, ADAM_EPS, ADAM_WD, ADAM_STEP = 0.001, 0.9, 0.999, 1e-08, 0.01, 10
N_CHIPS = 4
N_DEV = 8

LANES = 128
V7X_VMEM_BYTES = 64 * 1024 * 1024
MIB = 1024 * 1024

HEAD_PAD = 256
Q_PAD = N_HEADS * HEAD_PAD
U_ZA = 4 * D_CONV
U_CQ = U_ZA + D_ATTN
U_CKV = U_CQ + Q_LORA
U_KR = U_CKV + KV_LORA
KR_PAD = 256
U_COLS = U_KR + KR_PAD
MLA_COLS = Q_LORA + KV_LORA + KR_PAD

ATT_T = 512
INPROJ_TM, INPROJ_TN = 1024, 512
ROW_T = 512
OUT_T = 256
DH_TM, DH_TN = 512, 512
TN_TM, TN_TN, TN_TK = 1024, 1024, 512


def _cp(sem=None, vmem_mib=None, **kw):
    if sem is not None:
        kw["dimension_semantics"] = sem
    if vmem_mib is not None:
        kw["vmem_limit_bytes"] = min(vmem_mib * MIB, V7X_VMEM_BYTES - 4 * MIB)
    return pltpu.CompilerParams(**kw)


def _sigmoid(z):
    return 1.0 / (1.0 + jnp.exp(-z))


def _silu_grad(z, sg):
    return sg * (1.0 + z * (1.0 - sg))


def _nt(a, b):
    return lax.dot_general(a, b, (((1,), (1,)), ((), ())), preferred_element_type=F32)


def _tn(a, b):
    return lax.dot_general(a, b, (((0,), (0,)), ((), ())), preferred_element_type=F32)


def _nn(a, b):
    return jnp.dot(a, b, preferred_element_type=F32)


def _place():
    return lax.axis_index("x"), lax.axis_index("y"), lax.axis_index("c")


def _gather8(v, name, with_sum):
    rows, cols = v.shape

    def body(v_ref, out_ref, *rest):
        if with_sum:
            sum_ref, send_sems, recv_sems = rest
        else:
            send_sems, recv_sems = rest
        mx, my, mc = _place()
        me = 4 * mx + 2 * my + mc
        out_ref[me] = v_ref[...]
        peers = []
        for d in range(1, N_DEV):
            px = 1 - mx if d & 4 else mx
            py = 1 - my if d & 2 else my
            pc = 1 - mc if d & 1 else mc
            peers.append((px, py, pc))

        def copy(d, slot, to):
            return pltpu.make_async_remote_copy(
                src_ref=v_ref, dst_ref=out_ref.at[slot], send_sem=send_sems.at[d], recv_sem=recv_sems.at[d],
                device_id=to, device_id_type=MESH)

        sends = [copy(d, me, p) for d, p in enumerate(peers)]
        for cp in sends:
            cp.start()
        for d, (px, py, pc) in enumerate(peers):
            copy(d, 4 * px + 2 * py + pc, (px, py, pc)).wait_recv()
        for cp in sends:
            cp.wait_send()
        if with_sum:
            acc = out_ref[0]
            for b in range(1, N_DEV):
                acc = acc + out_ref[b]
            sum_ref[...] = acc

    out_shape = [SDS((N_DEV, rows, cols), F32)]
    if with_sum:
        out_shape.append(SDS((rows, cols), F32))
    vm = pl.BlockSpec(memory_space=pltpu.VMEM)
    return pl.pallas_call(
        body, name=name, out_shape=out_shape, in_specs=[vm], out_specs=[vm] * len(out_shape),
        scratch_shapes=[pltpu.SemaphoreType.DMA((N_DEV - 1,)), pltpu.SemaphoreType.DMA((N_DEV - 1,))],
    )(v)


def _chips_of(mx, my):
    chips = [(mx, 1 - my), (1 - mx, my), (1 - mx, 1 - my)]
    return chips, [2 * px + py for px, py in chips]


def _allgather_shards(shards):
    na = len(shards)
    halves = [s.shape[0] // 2 for s in shards]

    def body(*refs):
        ins, outs = refs[:na], refs[na:2 * na]
        s1, r1, s2, r2 = refs[2 * na:]
        mx, my, mc = _place()
        k = 2 * mx + my
        sib = (mx, my, 1 - mc)
        chips, kks = _chips_of(mx, my)

        def half(a, slot, c):
            return outs[a].at[slot, pl.ds(c * halves[a], halves[a])]

        def mine(a):
            return ins[a].at[pl.ds(mc * halves[a], halves[a])]

        sends = []
        for a in range(na):
            for d, (px, py) in enumerate(chips):
                cp = pltpu.make_async_remote_copy(
                    src_ref=mine(a), dst_ref=half(a, k, mc), send_sem=s1.at[3 * a + d], recv_sem=r1.at[3 * a + d],
                    device_id=(px, py, mc), device_id_type=MESH)
                cp.start()
                sends.append(cp)
        for a in range(na):
            for d, (px, py) in enumerate(chips):
                pltpu.make_async_remote_copy(
                    src_ref=mine(a), dst_ref=half(a, kks[d], mc), send_sem=s1.at[3 * a + d], recv_sem=r1.at[3 * a + d],
                    device_id=(px, py, mc), device_id_type=MESH).wait_recv()
                cp = pltpu.make_async_remote_copy(
                    src_ref=half(a, kks[d], mc), dst_ref=half(a, kks[d], mc), send_sem=s2.at[3 * a + d],
                    recv_sem=r2.at[3 * a + d], device_id=sib, device_id_type=MESH)
                cp.start()
                sends.append(cp)
        for a in range(na):
            for d in range(3):
                pltpu.make_async_remote_copy(
                    src_ref=half(a, kks[d], 1 - mc), dst_ref=half(a, kks[d], 1 - mc), send_sem=s2.at[3 * a + d],
                    recv_sem=r2.at[3 * a + d], device_id=sib, device_id_type=MESH).wait_recv()
        for cp in sends:
            cp.wait_send()

    return pl.pallas_call(
        body, name="allgather_weights",
        out_shape=[SDS((N_CHIPS,) + s.shape, s.dtype) for s in shards],
        in_specs=[ANY] * na, out_specs=[ANY] * na,
        scratch_shapes=[pltpu.SemaphoreType.DMA((3 * na,))] * 4,
    )(*shards)


def _rs_core_swap(grads):
    na = len(grads)
    halves = [g.shape[1] // 2 for g in grads]

    def body(*refs):
        ins, outs = refs[:na], refs[na:2 * na]
        ssem, rsem = refs[2 * na:]
        mx, my, mc = _place()
        sib = (mx, my, 1 - mc)
        sends = []
        for a in range(na):
            cp = pltpu.make_async_remote_copy(
                src_ref=ins[a].at[:, pl.ds((1 - mc) * halves[a], halves[a])], dst_ref=outs[a],
                send_sem=ssem.at[a], recv_sem=rsem.at[a], device_id=sib, device_id_type=MESH)
            cp.start()
            sends.append(cp)
        for cp in sends:
            cp.wait_recv()
        for cp in sends:
            cp.wait_send()

    return pl.pallas_call(
        body, name="rs_core_swap", out_shape=[SDS((N_CHIPS, h) + g.shape[2:], g.dtype) for g, h in zip(grads, halves)],
        in_specs=[ANY] * na, out_specs=[ANY] * na,
        scratch_shapes=[pltpu.SemaphoreType.DMA((na,))] * 2,
    )(*grads)


def _rs_chip_exchange(parts):
    na = len(parts)

    def body(*refs):
        ins, outs = refs[:na], refs[na:2 * na]
        ssem, rsem = refs[2 * na:]
        mx, my, mc = _place()
        k = 2 * mx + my
        chips, kks = _chips_of(mx, my)
        sends = []
        for a in range(na):
            for d, (px, py) in enumerate(chips):
                cp = pltpu.make_async_remote_copy(
                    src_ref=ins[a].at[kks[d]], dst_ref=outs[a].at[k], send_sem=ssem.at[3 * a + d],
                    recv_sem=rsem.at[3 * a + d], device_id=(px, py, mc), device_id_type=MESH)
                cp.start()
                sends.append(cp)
        for a in range(na):
            for d, (px, py) in enumerate(chips):
                pltpu.make_async_remote_copy(
                    src_ref=ins[a].at[kks[d]], dst_ref=outs[a].at[kks[d]], send_sem=ssem.at[3 * a + d],
                    recv_sem=rsem.at[3 * a + d], device_id=(px, py, mc), device_id_type=MESH).wait_recv()
        for cp in sends:
            cp.wait_send()

    return pl.pallas_call(
        body, name="rs_chip_exchange", out_shape=[SDS(p.shape, p.dtype) for p in parts],
        in_specs=[ANY] * na, out_specs=[ANY] * na,
        scratch_shapes=[pltpu.SemaphoreType.DMA((3 * na,))] * 2,
    )(*parts)


def _rs_core_join(halves):
    na = len(halves)

    def body(*refs):
        ins, outs = refs[:na], refs[na:2 * na]
        ssem, rsem = refs[2 * na:]
        mx, my, mc = _place()
        sib = (mx, my, 1 - mc)
        sends = []
        for a in range(na):
            cp = pltpu.make_async_remote_copy(
                src_ref=ins[a], dst_ref=outs[a].at[mc], send_sem=ssem.at[a], recv_sem=rsem.at[a],
                device_id=sib, device_id_type=MESH)
            cp.start()
            sends.append(cp)
        for a in range(na):
            pltpu.make_async_remote_copy(
                src_ref=ins[a], dst_ref=outs[a].at[1 - mc], send_sem=ssem.at[a], recv_sem=rsem.at[a],
                device_id=sib, device_id_type=MESH).wait_recv()
        for cp in sends:
            cp.wait_send()

    return pl.pallas_call(
        body, name="rs_core_join", out_shape=[SDS((2,) + h.shape, h.dtype) for h in halves],
        in_specs=[ANY] * na, out_specs=[ANY] * na,
        scratch_shapes=[pltpu.SemaphoreType.DMA((na,))] * 2,
    )(*halves)


def _row_tile(rows, limit):
    if rows <= limit:
        return rows
    best = None
    for t in range(16, limit + 1, 16):
        if rows % t == 0:
            best = t
    assert best is not None, rows
    return best


def _add_half_bf16(g, b, core, name):
    _, h, cols = b.shape
    tb = _row_tile(h, 512)
    nb = h // tb

    def body(core_ref, g_ref, b_ref, o_ref):
        o_ref[...] = (g_ref[...].astype(F32) + b_ref[...].astype(F32)).astype(BF16)

    spec = pl.BlockSpec((None, tb, cols), lambda kk, i, core_ref: (kk, i, 0))
    return pl.pallas_call(
        body, name=name,
        grid_spec=pltpu.PrefetchScalarGridSpec(
            num_scalar_prefetch=1, grid=(N_CHIPS, nb),
            in_specs=[pl.BlockSpec((None, tb, cols), lambda kk, i, core_ref: (kk, core_ref[0] * nb + i, 0)), spec],
            out_specs=spec),
        out_shape=SDS(b.shape, BF16), compiler_params=_cp(("arbitrary", "arbitrary")),
    )(core, g, b)


def _sum_chips(p, name):
    _, rows, cols = p.shape
    tb = _row_tile(rows, 256)

    def body(p_ref, o_ref):
        acc = p_ref[0].astype(F32)
        for j in range(1, N_CHIPS):
            acc = acc + p_ref[j].astype(F32)
        o_ref[...] = acc

    return pl.pallas_call(
        body, name=name, grid=(rows // tb,),
        in_specs=[pl.BlockSpec((N_CHIPS, tb, cols), lambda i: (0, i, 0))],
        out_specs=pl.BlockSpec((tb, cols), lambda i: (i, 0)), out_shape=SDS((rows, cols), F32),
        compiler_params=_cp(("arbitrary",)),
    )(p)


def _adamw_math(w, g, m, v):
    m2 = ADAM_B1 * m + (1.0 - ADAM_B1) * g
    v2 = ADAM_B2 * v + (1.0 - ADAM_B2) * (g * g)
    m_hat = m2 / (1.0 - ADAM_B1 ** ADAM_STEP)
    v_hat = v2 / (1.0 - ADAM_B2 ** ADAM_STEP)
    delta = -ADAM_LR * (m_hat / (jnp.sqrt(v_hat) + ADAM_EPS) + ADAM_WD * w)
    return delta, m2, v2


def _adamw(w, g, m, v, name):
    rows, cols = w.shape
    tb = _row_tile(rows, 256)

    def body(w_ref, g_ref, m_ref, v_ref, d_ref, m2_ref, v2_ref):
        d, m2, v2 = _adamw_math(w_ref[...], g_ref[...], m_ref[...], v_ref[...])
        d_ref[...] = d
        m2_ref[...] = m2
        v2_ref[...] = v2

    spec = pl.BlockSpec((tb, cols), lambda i: (i, 0))
    return pl.pallas_call(
        body, name=name, grid=(rows // tb,), in_specs=[spec] * 4, out_specs=[spec] * 3,
        out_shape=[SDS((rows, cols), F32)] * 3, compiler_params=_cp(("arbitrary",), 40),
    )(w, g, m, v)


def _ada_w_update(sc_all, dmod_k, w, m, v):
    rows, cols = w.shape
    tb = 256

    def body(s_ref, dm_ref, w_ref, m_ref, v_ref, g_ref, d_ref, m2_ref, v2_ref):
        g = _tn(s_ref[...].astype(BF16), dm_ref[...].astype(BF16))
        d, m2, v2 = _adamw_math(w_ref[...], g, m_ref[...], v_ref[...])
        g_ref[...] = g
        d_ref[...] = d
        m2_ref[...] = m2
        v2_ref[...] = v2

    spec = pl.BlockSpec((tb, cols), lambda i: (i, 0))
    return pl.pallas_call(
        body, name="ada_w_update", grid=(rows // tb,),
        in_specs=[pl.BlockSpec((N_DEV, tb), lambda i: (0, i)), pl.BlockSpec((N_DEV, cols), lambda i: (0, 0)), spec, spec, spec],
        out_specs=[spec] * 4, out_shape=[SDS((rows, cols), F32)] * 4, compiler_params=_cp(("arbitrary",), 40),
    )(sc_all, dmod_k, w, m, v)


def _ada_mod(c_all, w, b_k):
    rows, cols = w.shape
    tn = 512

    def body(c_ref, w_ref, b_ref, o_ref, s_ref):
        cv = c_ref[...]
        s = cv * _sigmoid(cv)
        s_ref[...] = s
        o_ref[...] = _nn(s.astype(BF16), w_ref[...].astype(BF16)) + b_ref[...]

    return pl.pallas_call(
        body, name="ada_mod", grid=(cols // tn,),
        in_specs=[pl.BlockSpec((N_DEV, rows), lambda j: (0, 0)), pl.BlockSpec((rows, tn), lambda j: (0, j)),
                  pl.BlockSpec((1, tn), lambda j: (0, j))],
        out_specs=[pl.BlockSpec((N_DEV, tn), lambda j: (0, j)), pl.BlockSpec((N_DEV, rows), lambda j: (0, 0))],
        out_shape=[SDS((N_DEV, cols), F32), SDS((N_DEV, rows), F32)], compiler_params=_cp(("arbitrary",)),
    )(c_all, w, b_k)


def _inproj(x, norm_g, scale, shift, w_my):
    seq, dm = x.shape
    ncols = w_my.shape[1]
    tm, tn = min(INPROJ_TM, seq), INPROJ_TN

    def body(x_ref, g_ref, sc_ref, sh_ref, w_ref, h_ref, u_ref):
        @pl.when(pl.program_id(1) == 0)
        def _():
            xv = x_ref[...]
            r = lax.rsqrt(jnp.mean(xv * xv, axis=-1, keepdims=True) + EPS)
            hv = (xv * r * g_ref[...]) * (1.0 + sc_ref[...]) + sh_ref[...]
            h_ref[...] = hv.astype(BF16)

        u_ref[...] = _nn(h_ref[...], w_ref[...]).astype(BF16)

    vec = pl.BlockSpec((1, dm), lambda i, j: (0, 0))
    return pl.pallas_call(
        body, name="inproj", grid=(seq // tm, ncols // tn),
        in_specs=[pl.BlockSpec((tm, dm), lambda i, j: (i, 0)), vec, vec, vec, pl.BlockSpec((dm, tn), lambda i, j: (0, j))],
        out_specs=[pl.BlockSpec((tm, dm), lambda i, j: (i, 0)), pl.BlockSpec((tm, tn), lambda i, j: (i, j))],
        out_shape=[SDS((seq, dm), BF16), SDS((seq, ncols), BF16)],
        compiler_params=_cp(("arbitrary", "arbitrary"), 48),
    )(x, norm_g, scale, shift, w_my)


HALO = 16


def _conv_taps(uc, halo, ext_ref, ts, causal):
    if causal:
        ext_ref[0:HALO, :] = halo
        ext_ref[HALO:HALO + ts, :] = uc
        return ext_ref[pl.ds(HALO - 1, ts), :], ext_ref[pl.ds(HALO - 2, ts), :]
    ext_ref[0:ts, :] = uc
    ext_ref[ts:ts + HALO, :] = halo
    return ext_ref[pl.ds(1, ts), :], ext_ref[pl.ds(2, ts), :]


def _conv_fwd(u, conv_w):
    seq = u.shape[0]
    ts = min(ROW_T, seq)
    hb = ts // HALO

    def body(xc_ref, bc_ref, cc_ref, zc_ref, xp_ref, cp_ref, w_ref, y_ref, ext_ref):
        i = pl.program_id(0)
        uc = cc_ref[...].astype(F32) * xc_ref[...].astype(F32)
        up = cp_ref[...].astype(F32) * xp_ref[...].astype(F32)
        up = jnp.where(i > 0, up, 0.0)
        u1, u2 = _conv_taps(uc, up, ext_ref, ts, True)
        conv = w_ref[0:1, :] * u2 + w_ref[1:2, :] * u1 + w_ref[2:3, :] * uc
        z = zc_ref[...].astype(F32)
        y_ref[...] = ((bc_ref[...].astype(F32) * conv) * (z * _sigmoid(z))).astype(BF16)

    def col(cb):
        return pl.BlockSpec((ts, D_CONV), lambda i: (i, cb))

    def prev(cb):
        return pl.BlockSpec((HALO, D_CONV), lambda i: (jnp.maximum(i * hb - 1, 0), cb))

    return pl.pallas_call(
        body, name="conv_fwd", grid=(seq // ts,),
        in_specs=[col(0), col(1), col(2), col(3), prev(0), prev(2), pl.BlockSpec((3, D_CONV), lambda i: (0, 0))],
        out_specs=pl.BlockSpec((ts, D_CONV), lambda i: (i, 0)), out_shape=SDS((seq, D_CONV), BF16),
        scratch_shapes=[pltpu.VMEM((ts + HALO, D_CONV), F32)],
        compiler_params=_cp(("arbitrary",), 40),
    )(u, u, u, u, u, u, conv_w)


def _rope_tables(pos_ref, freq_ref):
    ang = pos_ref[...].astype(F32) * freq_ref[...]
    lane = lax.broadcasted_iota(jnp.int32, ang.shape, 1)
    cs, sn = jnp.cos(ang), jnp.sin(ang)
    half = QK_ROPE // 2
    cos_t = jnp.where(lane < QK_ROPE, cs, 0.0)
    sin_lo = jnp.where(lane < half, sn, 0.0)
    sin_hi = jnp.where((lane >= half) & (lane < QK_ROPE), sn, 0.0)
    return cos_t, sin_lo, sin_hi


def _rope(blk, tables):
    cos_t, sin_lo, sin_hi = tables
    half = QK_ROPE // 2
    return blk * cos_t - pltpu.roll(blk, LANES - half, 1) * sin_lo + pltpu.roll(blk, half, 1) * sin_hi


def _rope_bwd(g, tables):
    cos_t, sin_lo, sin_hi = tables
    half = QK_ROPE // 2
    return g * cos_t + pltpu.roll(g, LANES - half, 1) * sin_lo - pltpu.roll(g, half, 1) * sin_hi


def _rms(v, n):
    r = lax.rsqrt(jnp.sum(v * v, axis=-1, keepdims=True) * (1.0 / n) + EPS)
    return v * r, r


def _mla_prep(u, pos, freq, q_a_g, wq, kv_a_g, wkn, wv, q_g, k_g):
    seq = u.shape[0]
    ts = min(ROW_T, seq)
    qscale = LOG2E / math.sqrt(QK_HEAD)

    def body(cq_ref, ckv_ref, kr_ref, pos_ref, freq_ref, qag_ref, wq_ref, kvag_ref, wkn_ref, wv_ref, qg_ref, kg_ref,
             q_ref, k_ref, v_ref):
        tables = _rope_tables(pos_ref, freq_ref)
        cqn, _ = _rms(cq_ref[...].astype(F32), Q_LORA)
        qp = _nn((cqn * qag_ref[...]).astype(BF16), wq_ref[...])
        qg = qg_ref[...]
        for h in range(N_HEADS):
            lo = h * HEAD_PAD
            qn, _ = _rms(qp[:, lo:lo + HEAD_PAD], QK_HEAD)
            qn = qn * qg
            q_ref[:, lo:lo + LANES] = (qn[:, :LANES] * qscale).astype(BF16)
            q_ref[:, lo + LANES:lo + HEAD_PAD] = (_rope(qn[:, LANES:], tables) * qscale).astype(BF16)
        ckvn, _ = _rms(ckv_ref[...].astype(F32), KV_LORA)
        ckvb = (ckvn * kvag_ref[...]).astype(BF16)
        kn = _nn(ckvb, wkn_ref[...])
        v_ref[...] = _nn(ckvb, wv_ref[...]).astype(BF16)
        kr = kr_ref[:, 0:LANES].astype(F32)
        ssr = jnp.sum(kr * kr, axis=-1, keepdims=True)
        kg = kg_ref[...]
        for h in range(N_HEADS):
            knh = kn[:, h * QK_NOPE:(h + 1) * QK_NOPE]
            r = lax.rsqrt((jnp.sum(knh * knh, axis=-1, keepdims=True) + ssr) * (1.0 / QK_HEAD) + EPS)
            lo = h * HEAD_PAD
            k_ref[:, lo:lo + LANES] = (knh * r * kg[:, :LANES]).astype(BF16)
            k_ref[:, lo + LANES:lo + HEAD_PAD] = _rope(kr * r * kg[:, LANES:], tables).astype(BF16)

    def full(a):
        return pl.BlockSpec(a.shape, lambda i: (0,) * a.ndim)

    return pl.pallas_call(
        body, name="mla_prep", grid=(seq // ts,),
        in_specs=[pl.BlockSpec((ts, Q_LORA), lambda i: (i, U_CQ // Q_LORA)),
                  pl.BlockSpec((ts, KV_LORA), lambda i: (i, U_CKV // KV_LORA)),
                  pl.BlockSpec((ts, KR_PAD), lambda i: (i, U_KR // KR_PAD)),
                  pl.BlockSpec((ts, 1), lambda i: (i, 0)), full(freq), full(q_a_g), full(wq), full(kv_a_g), full(wkn),
                  full(wv), full(q_g), full(k_g)],
        out_specs=[pl.BlockSpec((ts, Q_PAD), lambda i: (i, 0)), pl.BlockSpec((ts, Q_PAD), lambda i: (i, 0)),
                   pl.BlockSpec((ts, D_ATTN), lambda i: (i, 0))],
        out_shape=[SDS((seq, Q_PAD), BF16), SDS((seq, Q_PAD), BF16), SDS((seq, D_ATTN), BF16)],
        compiler_params=_cp(("arbitrary",), 48),
    )(u, u, u, pos, freq, q_a_g, wq, kv_a_g, wkn, wv, q_g, k_g)


def _causal_mask(t, tq, q0):
    return lax.broadcasted_iota(jnp.int32, (t, tq), 0) <= lax.broadcasted_iota(jnp.int32, (t, tq), 1) + q0


def _flash_fwd(q, k, v, u):
    seq = q.shape[0]
    t = min(ATT_T, seq)
    za_blk = U_ZA // V_HEAD

    def body(q_ref, k_ref, v_ref, z_ref, o_ref, y_ref, lse_ref, s_a, s_b, m_ref, l_ref, acc_ref):
        i = pl.program_id(1)
        ones = jnp.ones((16, t), BF16)

        def keys(j):
            return pl.ds(pl.multiple_of(j * t, t), t)

        def scores(j, s_ref):
            s_ref[...] = _nt(k_ref[keys(j), :], q_ref[...])

        def absorb(j, s_ref, masked):
            s = s_ref[...]
            if masked:
                s = jnp.where(_causal_mask(t, t, 0), s, -jnp.inf)
            m = m_ref[...]
            m_new = jnp.maximum(m, jnp.max(s, axis=0, keepdims=True))
            alpha = jnp.exp2(m - m_new)
            p = jnp.exp2((s - m_new).astype(BF16))
            m_ref[...] = m_new
            l_ref[...] = alpha * l_ref[...] + _nn(ones, p)[0:1, :]
            acc_ref[...] = alpha * acc_ref[...] + _tn(v_ref[keys(j), :], p)

        scores(0, s_a)
        m_ref[...] = jnp.full_like(m_ref, -jnp.inf)
        l_ref[...] = jnp.zeros_like(l_ref)
        acc_ref[...] = jnp.zeros_like(acc_ref)

        def pair(jj, carry):
            j = 2 * jj
            scores(j + 1, s_b)
            absorb(j, s_a, False)
            scores(j + 2, s_a)
            absorb(j + 1, s_b, False)
            return carry

        lax.fori_loop(0, i // 2, pair, 0)

        @pl.when(i % 2 == 1)
        def _():
            scores(i, s_b)
            absorb(i - 1, s_a, False)
            absorb(i, s_b, True)

        @pl.when(i % 2 == 0)
        def _():
            absorb(i, s_a, True)

        l = l_ref[...]
        o = (acc_ref[...] * (1.0 / l)).T
        lse_ref[...] = m_ref[...] + jnp.log2(l)
        o_ref[...] = o.astype(BF16)
        z = z_ref[...].astype(F32)
        y_ref[...] = (o * (z * _sigmoid(z))).astype(BF16)

    tile = pl.BlockSpec((t, V_HEAD), lambda h, i: (i, h))
    return pl.pallas_call(
        body, name="flash_fwd", grid=(N_HEADS, seq // t),
        in_specs=[pl.BlockSpec((t, HEAD_PAD), lambda h, i: (i, h)), pl.BlockSpec((seq, HEAD_PAD), lambda h, i: (0, h)),
                  pl.BlockSpec((seq, V_HEAD), lambda h, i: (0, h)), pl.BlockSpec((t, V_HEAD), lambda h, i: (i, za_blk + h))],
        out_specs=[tile, tile, pl.BlockSpec((None, 1, t), lambda h, i: (h, 0, i))],
        out_shape=[SDS((seq, D_ATTN), BF16), SDS((seq, D_ATTN), BF16), SDS((N_HEADS, 1, seq), F32)],
        scratch_shapes=[pltpu.VMEM((t, t), F32), pltpu.VMEM((t, t), F32), pltpu.VMEM((1, t), F32), pltpu.VMEM((1, t), F32),
                        pltpu.VMEM((V_HEAD, t), F32)],
        compiler_params=_cp(("arbitrary", "arbitrary"), 40),
    )(q, k, v, u)


def _outproj_loss(y_conv, y_attn, x, target, gate, w_out):
    seq, dm = x.shape
    ts = min(OUT_T, seq)
    n = seq // ts
    dmix = w_out.shape[0]

    def body(yc_ref, ya_ref, x_ref, t_ref, gate_ref, wo_hbm, dout_ref, dy_ref, dyc_ref, stats_ref, wo_ref, sem, acc_ref):
        i = pl.program_id(0)

        @pl.when(i == 0)
        def _():
            cp = pltpu.make_async_copy(wo_hbm, wo_ref, sem)
            cp.start()
            cp.wait()
            acc_ref[...] = jnp.zeros_like(acc_ref)

        y = _nn(yc_ref[...], wo_ref[0:D_CONV, :]) + _nn(ya_ref[...], wo_ref[D_CONV:dmix, :])
        gate_v = gate_ref[...]
        diff = (x_ref[...] + gate_v * y) - t_ref[...]
        dout = diff * (1.0 / dm)
        dout_ref[...] = dout
        acc_ref[0:8, :] += jnp.sum((dout * y).reshape(ts // 8, 8, dm), axis=0)
        acc_ref[8:16, :] += jnp.sum((diff * diff).reshape(ts // 8, 8, dm), axis=0)
        dy = (dout * gate_v).astype(BF16)
        dy_ref[...] = dy
        dyc_ref[...] = _nt(dy, wo_ref[...]).astype(BF16)

        @pl.when(i == n - 1)
        def _():
            stats_ref[...] = jnp.zeros_like(stats_ref)
            stats_ref[0:1, :] = jnp.sum(acc_ref[0:8, :], axis=0, keepdims=True)
            loss = jnp.sum(acc_ref[8:16, :]) * (0.5 / dm)
            stats_ref[1:2, :] = jnp.full((1, dm), loss, F32)

    row = pl.BlockSpec((ts, dm), lambda i: (i, 0))
    half = pl.BlockSpec((ts, D_CONV), lambda i: (i, 0))
    return pl.pallas_call(
        body, name="outproj_loss", grid=(n,),
        in_specs=[half, half, row, row, pl.BlockSpec((1, dm), lambda i: (0, 0)), ANY],
        out_specs=[row, row, pl.BlockSpec((ts, dmix), lambda i: (i, 0)), pl.BlockSpec((8, dm), lambda i: (0, 0))],
        out_shape=[SDS((seq, dm), F32), SDS((seq, dm), BF16), SDS((seq, dmix), BF16), SDS((8, dm), F32)],
        scratch_shapes=[pltpu.VMEM(w_out.shape, BF16), pltpu.SemaphoreType.DMA(()), pltpu.VMEM((16, dm), F32)],
        compiler_params=_cp(("arbitrary",), 52),
    )(y_conv, y_attn, x, target, gate, w_out)


def _matmul_tn(a, b, name):
    seq, m = a.shape
    n = b.shape[1]
    tm, tn, tk = min(TN_TM, m), min(TN_TN, n), min(TN_TK, seq)
    nk = seq // tk

    def body(a_ref, b_ref, o_ref, acc_ref):
        kk = pl.program_id(2)

        @pl.when(kk == 0)
        def _():
            acc_ref[...] = jnp.zeros_like(acc_ref)

        acc_ref[...] += _tn(a_ref[...], b_ref[...])

        @pl.when(kk == nk - 1)
        def _():
            o_ref[...] = acc_ref[...].astype(BF16)

    return pl.pallas_call(
        body, name=name, grid=(m // tm, n // tn, nk),
        in_specs=[pl.BlockSpec((tk, tm), lambda i, j, kk: (kk, i)), pl.BlockSpec((tk, tn), lambda i, j, kk: (kk, j))],
        out_specs=pl.BlockSpec((tm, tn), lambda i, j, kk: (i, j)), out_shape=SDS((m, n), BF16),
        scratch_shapes=[pltpu.VMEM((tm, tn), F32)],
        compiler_params=_cp(("arbitrary", "arbitrary", "arbitrary"), 40),
    )(a, b)


def _attn_gate_bwd(dycat, o, u):
    seq = o.shape[0]
    ts = min(ROW_T, seq)

    def body(dy_ref, o_ref, z_ref, dot_ref, dz_ref, dl_ref):
        dy = dy_ref[...].astype(F32)
        ov = o_ref[...].astype(F32)
        z = z_ref[...].astype(F32)
        sg = _sigmoid(z)
        do = dy * (z * sg)
        dz_ref[...] = (dy * ov * _silu_grad(z, sg)).astype(BF16)
        prod = do * ov
        ones = jnp.ones((8, V_HEAD), F32)
        for h in range(N_HEADS):
            cols = slice(h * V_HEAD, (h + 1) * V_HEAD)
            dot_ref[h] = do[:, cols].T.astype(BF16)
            rows = lax.dot_general(ones, prod[:, cols], (((1,), (1,)), ((), ())), precision=lax.Precision.HIGHEST,
                                   preferred_element_type=F32)
            dl_ref[h] = rows[0:1, :]

    blk = pl.BlockSpec((ts, D_ATTN), lambda i: (i, 0))
    return pl.pallas_call(
        body, name="attn_gate_bwd", grid=(seq // ts,),
        in_specs=[pl.BlockSpec((ts, D_ATTN), lambda i: (i, 1)), blk, pl.BlockSpec((ts, D_ATTN), lambda i: (i, U_ZA // D_ATTN))],
        out_specs=[pl.BlockSpec((N_HEADS, V_HEAD, ts), lambda i: (0, 0, i)), blk,
                   pl.BlockSpec((N_HEADS, 1, ts), lambda i: (0, 0, i))],
        out_shape=[SDS((N_HEADS, V_HEAD, seq), BF16), SDS((seq, D_ATTN), BF16), SDS((N_HEADS, 1, seq), F32)],
        compiler_params=_cp(("arbitrary",), 40),
    )(dycat, o, u)


def _flash_bwd(q, k, v, do_t, lse, delta):
    seq = q.shape[0]
    t = min(ATT_T, seq)
    n = seq // t

    def body(k_ref, v_ref, q_ref, dot_ref, lse_ref, dl_ref, dq_ref, dk_ref, dv_ref, s_a, s_b, dp_a, dp_b, dvt_ref):
        j = pl.program_id(1)

        @pl.when(j == 0)
        def _():
            dq_ref[...] = jnp.zeros_like(dq_ref)

        def rows(r):
            return pl.ds(pl.multiple_of((j + r) * t, t), t)

        def products(r, s_ref, dp_ref):
            s_ref[...] = _nt(k_ref[...], q_ref[rows(r), :])
            dp_ref[...] = _nn(v_ref[...], dot_ref[:, rows(r)])

        def absorb(r, s_ref, dp_ref, masked):
            p = jnp.exp2((s_ref[...] - lse_ref[:, rows(r)]).astype(BF16))
            if masked:
                p = jnp.where(_causal_mask(t, t, 0), p, jnp.zeros_like(p))
            dvt_ref[...] += _nt(dot_ref[:, rows(r)], p)
            ds = p * (dp_ref[...] - dl_ref[:, rows(r)]).astype(BF16)
            dk_ref[...] += _nn(ds, q_ref[rows(r), :])
            dq_ref[rows(r), :] += _tn(ds, k_ref[...])

        dk_ref[...] = jnp.zeros_like(dk_ref)
        dvt_ref[...] = jnp.zeros_like(dvt_ref)
        products(0, s_a, dp_a)
        last = n - 1 - j

        @pl.when(last == 0)
        def _():
            absorb(0, s_a, dp_a, True)

        @pl.when(last > 0)
        def _():
            products(1, s_b, dp_b)
            absorb(0, s_a, dp_a, True)

        inner = jnp.maximum(last - 1, 0)

        def pair(pp, carry):
            r = 1 + 2 * pp
            products(r + 1, s_a, dp_a)
            absorb(r, s_b, dp_b, False)
            products(r + 2, s_b, dp_b)
            absorb(r + 1, s_a, dp_a, False)
            return carry

        lax.fori_loop(0, inner // 2, pair, 0)

        @pl.when((last > 0) & (inner % 2 == 1))
        def _():
            products(last, s_a, dp_a)
            absorb(last - 1, s_b, dp_b, False)
            absorb(last, s_a, dp_a, False)

        @pl.when((last > 0) & (inner % 2 == 0))
        def _():
            absorb(last, s_b, dp_b, False)

        dv_ref[...] = dvt_ref[...].T.astype(BF16)

    row = pl.BlockSpec((None, 1, seq), lambda h, j: (h, 0, 0))
    return pl.pallas_call(
        body, name="flash_bwd", grid=(N_HEADS, n),
        in_specs=[pl.BlockSpec((t, HEAD_PAD), lambda h, j: (j, h)), pl.BlockSpec((t, V_HEAD), lambda h, j: (j, h)),
                  pl.BlockSpec((seq, HEAD_PAD), lambda h, j: (0, h)), pl.BlockSpec((None, V_HEAD, seq), lambda h, j: (h, 0, 0)),
                  row, row],
        out_specs=[pl.BlockSpec((seq, HEAD_PAD), lambda h, j: (0, h)), pl.BlockSpec((t, HEAD_PAD), lambda h, j: (j, h)),
                   pl.BlockSpec((t, V_HEAD), lambda h, j: (j, h))],
        out_shape=[SDS((seq, Q_PAD), F32), SDS((seq, Q_PAD), F32), SDS((seq, D_ATTN), BF16)],
        scratch_shapes=[pltpu.VMEM((t, t), F32)] * 4 + [pltpu.VMEM((V_HEAD, t), F32)],
        compiler_params=_cp(("arbitrary", "arbitrary"), 56),
    )(k, v, q, do_t, lse, delta)


SG_QAG, SG_KVAG, SG_QG, SG_KG, SG_COLS = 0, Q_LORA, Q_LORA + KV_LORA, Q_LORA + KV_LORA + HEAD_PAD, D_MODEL


def _mla_bwd(dq, dk, dv, u, pos, freq, q_a_g, wq, kv_a_g, wkn, wv, q_g, k_g):
    seq = u.shape[0]
    ts = min(ROW_T, seq)
    n = seq // ts
    qscale = 1.0 / math.sqrt(QK_HEAD)

    def body(dq_ref, dk_ref, dv_ref, cq_ref, ckv_ref, kr_ref, pos_ref, freq_ref, qag_ref, wq_ref, kvag_ref, wkn_ref,
             wv_ref, qg_ref, kg_ref, du_ref, dwq_ref, dwkn_ref, dwv_ref, sg_ref, dqp_ref, dkn_ref):
        i = pl.program_id(0)

        @pl.when(i == 0)
        def _():
            dwq_ref[...] = jnp.zeros_like(dwq_ref)
            dwkn_ref[...] = jnp.zeros_like(dwkn_ref)
            dwv_ref[...] = jnp.zeros_like(dwv_ref)
            sg_ref[...] = jnp.zeros_like(sg_ref)

        tables = _rope_tables(pos_ref, freq_ref)

        cq = cq_ref[...].astype(F32)
        cqn, rq = _rms(cq, Q_LORA)
        qag = qag_ref[...]
        cqb = (cqn * qag).astype(BF16)
        qp = _nn(cqb, wq_ref[...])
        qg = qg_ref[...]
        dqg = jnp.zeros((1, HEAD_PAD), F32)
        for h in range(N_HEADS):
            lo = h * HEAD_PAD
            xn, r = _rms(qp[:, lo:lo + HEAD_PAD], QK_HEAD)
            g = jnp.concatenate([dq_ref[:, lo:lo + LANES], _rope_bwd(dq_ref[:, lo + LANES:lo + HEAD_PAD], tables)],
                                axis=-1) * qscale
            dqg = dqg + jnp.sum(g * xn, axis=0, keepdims=True)
            gy = g * qg
            mean = jnp.sum(gy * xn, axis=-1, keepdims=True) * (1.0 / QK_HEAD)
            dqp_ref[:, lo:lo + HEAD_PAD] = (r * (gy - xn * mean)).astype(BF16)
        dqp = dqp_ref[...]
        dwq_ref[...] += _tn(cqb, dqp)
        dcqn = _nt(dqp, wq_ref[...])
        sg_ref[0:1, SG_QAG:SG_QAG + Q_LORA] += jnp.sum(dcqn * cqn, axis=0, keepdims=True)
        sg_ref[0:1, SG_QG:SG_QG + HEAD_PAD] += dqg
        gy = dcqn * qag
        mean = jnp.sum(gy * cqn, axis=-1, keepdims=True) * (1.0 / Q_LORA)
        du_ref[:, 0:Q_LORA] = (rq * (gy - cqn * mean)).astype(BF16)

        ckv = ckv_ref[...].astype(F32)
        ckvn, rkv = _rms(ckv, KV_LORA)
        kvag = kvag_ref[...]
        ckvb = (ckvn * kvag).astype(BF16)
        kn = _nn(ckvb, wkn_ref[...])
        kr = kr_ref[:, 0:LANES].astype(F32)
        ssr = jnp.sum(kr * kr, axis=-1, keepdims=True)
        kg = kg_ref[...]
        kg_n, kg_r = kg[:, :LANES] * LN2, kg[:, LANES:] * LN2
        dkg_n = jnp.zeros((1, LANES), F32)
        dkg_r = jnp.zeros((1, LANES), F32)
        dkr = jnp.zeros((ts, LANES), F32)
        for h in range(N_HEADS):
            knh = kn[:, h * QK_NOPE:(h + 1) * QK_NOPE]
            r = lax.rsqrt((jnp.sum(knh * knh, axis=-1, keepdims=True) + ssr) * (1.0 / QK_HEAD) + EPS)
            xn_n, xn_r = knh * r, kr * r
            lo = h * HEAD_PAD
            g_n = dk_ref[:, lo:lo + LANES]
            g_r = _rope_bwd(dk_ref[:, lo + LANES:lo + HEAD_PAD], tables)
            dkg_n = dkg_n + jnp.sum(g_n * xn_n, axis=0, keepdims=True)
            dkg_r = dkg_r + jnp.sum(g_r * xn_r, axis=0, keepdims=True)
            gy_n, gy_r = g_n * kg_n, g_r * kg_r
            mean = (jnp.sum(gy_n * xn_n, axis=-1, keepdims=True) + jnp.sum(gy_r * xn_r, axis=-1, keepdims=True)) * (1.0 / QK_HEAD)
            dkn_ref[:, h * QK_NOPE:(h + 1) * QK_NOPE] = (r * (gy_n - xn_n * mean)).astype(BF16)
            dkr = dkr + r * (gy_r - xn_r * mean)
        dkn = dkn_ref[...]
        dvv = dv_ref[...]
        dwkn_ref[...] += _tn(ckvb, dkn)
        dwv_ref[...] += _tn(ckvb, dvv)
        dckvn = _nt(dkn, wkn_ref[...]) + _nt(dvv, wv_ref[...])
        sg_ref[0:1, SG_KVAG:SG_KVAG + KV_LORA] += jnp.sum(dckvn * ckvn, axis=0, keepdims=True)
        sg_ref[0:1, SG_KG:SG_KG + LANES] += dkg_n * LN2
        sg_ref[0:1, SG_KG + LANES:SG_KG + HEAD_PAD] += dkg_r * LN2
        gy = dckvn * kvag
        mean = jnp.sum(gy * ckvn, axis=-1, keepdims=True) * (1.0 / KV_LORA)
        du_ref[:, Q_LORA:Q_LORA + KV_LORA] = (rkv * (gy - ckvn * mean)).astype(BF16)
        du_ref[:, Q_LORA + KV_LORA:Q_LORA + KV_LORA + LANES] = dkr.astype(BF16)
        du_ref[:, Q_LORA + KV_LORA + LANES:MLA_COLS] = jnp.zeros((ts, LANES), BF16)

    def full(a):
        return pl.BlockSpec(a.shape, lambda i: (0,) * a.ndim)

    wide = pl.BlockSpec((ts, Q_PAD), lambda i: (i, 0))
    return pl.pallas_call(
        body, name="mla_bwd", grid=(n,),
        in_specs=[wide, wide, pl.BlockSpec((ts, D_ATTN), lambda i: (i, 0)),
                  pl.BlockSpec((ts, Q_LORA), lambda i: (i, U_CQ // Q_LORA)),
                  pl.BlockSpec((ts, KV_LORA), lambda i: (i, U_CKV // KV_LORA)),
                  pl.BlockSpec((ts, KR_PAD), lambda i: (i, U_KR // KR_PAD)),
                  pl.BlockSpec((ts, 1), lambda i: (i, 0)), full(freq), full(q_a_g), full(wq), full(kv_a_g), full(wkn),
                  full(wv), full(q_g), full(k_g)],
        out_specs=[pl.BlockSpec((ts, MLA_COLS), lambda i: (i, 0)), pl.BlockSpec((Q_LORA, Q_PAD), lambda i: (0, 0)),
                   pl.BlockSpec((KV_LORA, D_ATTN), lambda i: (0, 0)), pl.BlockSpec((KV_LORA, D_ATTN), lambda i: (0, 0)),
                   pl.BlockSpec((8, SG_COLS), lambda i: (0, 0))],
        out_shape=[SDS((seq, MLA_COLS), BF16), SDS((Q_LORA, Q_PAD), F32), SDS((KV_LORA, D_ATTN), F32),
                   SDS((KV_LORA, D_ATTN), F32), SDS((8, SG_COLS), F32)],
        scratch_shapes=[pltpu.VMEM((ts, Q_PAD), BF16), pltpu.VMEM((ts, D_ATTN), BF16)],
        compiler_params=_cp(("arbitrary",), 56),
    )(dq, dk, dv, u, u, u, pos, freq, q_a_g, wq, kv_a_g, wkn, wv, q_g, k_g)


def _conv_bwd(dycat, u, conv_w):
    seq = u.shape[0]
    ts = min(ROW_T, seq)
    n = seq // ts
    hb = ts // HALO

    def body(dy_ref, xc_ref, bc_ref, cc_ref, zc_ref, xp_ref, cp_ref, dyn_ref, bn_ref, zn_ref, w_ref,
             du_ref, dw_ref, ext_ref):
        i = pl.program_id(0)

        @pl.when(i == 0)
        def _():
            dw_ref[...] = jnp.zeros_like(dw_ref)

        xc = xc_ref[...].astype(F32)
        cc = cc_ref[...].astype(F32)
        uc = cc * xc
        up = jnp.where(i > 0, cp_ref[...].astype(F32) * xp_ref[...].astype(F32), 0.0)
        u1, u2 = _conv_taps(uc, up, ext_ref, ts, True)
        w0, w1, w2 = w_ref[0:1, :], w_ref[1:2, :], w_ref[2:3, :]
        conv = w0 * u2 + w1 * u1 + w2 * uc
        z = zc_ref[...].astype(F32)
        sg = _sigmoid(z)
        sz = z * sg
        b = bc_ref[...].astype(F32)
        dy = dy_ref[...].astype(F32)
        du_ref[:, 3 * D_CONV:4 * D_CONV] = (dy * (b * conv) * _silu_grad(z, sg)).astype(BF16)
        du_ref[:, D_CONV:2 * D_CONV] = (dy * sz * conv).astype(BF16)
        dconv = dy * sz * b
        dw_ref[0:1, :] += jnp.sum(dconv * u2, axis=0, keepdims=True)
        dw_ref[1:2, :] += jnp.sum(dconv * u1, axis=0, keepdims=True)
        dw_ref[2:3, :] += jnp.sum(dconv * uc, axis=0, keepdims=True)
        zn = zn_ref[...].astype(F32)
        dnext = dyn_ref[...].astype(F32) * (zn * _sigmoid(zn)) * bn_ref[...].astype(F32)
        dnext = jnp.where(i < n - 1, dnext, 0.0)
        d1, d2 = _conv_taps(dconv, dnext, ext_ref, ts, False)
        du = w2 * dconv + w1 * d1 + w0 * d2
        du_ref[:, 2 * D_CONV:3 * D_CONV] = (du * xc).astype(BF16)
        du_ref[:, 0:D_CONV] = (du * cc).astype(BF16)

    def col(cb):
        return pl.BlockSpec((ts, D_CONV), lambda i: (i, cb))

    def prev(cb):
        return pl.BlockSpec((HALO, D_CONV), lambda i: (jnp.maximum(i * hb - 1, 0), cb))

    def nxt(cb):
        return pl.BlockSpec((HALO, D_CONV), lambda i: (jnp.minimum((i + 1) * hb, n * hb - 1), cb))

    return pl.pallas_call(
        body, name="conv_bwd", grid=(n,),
        in_specs=[col(0), col(0), col(1), col(2), col(3), prev(0), prev(2), nxt(0), nxt(1), nxt(3),
                  pl.BlockSpec((3, D_CONV), lambda i: (0, 0))],
        out_specs=[pl.BlockSpec((ts, 4 * D_CONV), lambda i: (i, 0)), pl.BlockSpec((8, D_CONV), lambda i: (0, 0))],
        out_shape=[SDS((seq, 4 * D_CONV), BF16), SDS((8, D_CONV), F32)],
        scratch_shapes=[pltpu.VMEM((ts + HALO, D_CONV), F32)],
        compiler_params=_cp(("arbitrary",), 48),
    )(dycat, u, u, u, u, u, u, dycat, u, u, conv_w)


def _inproj_bwd(du_conv, du_za, du_mla, w_my):
    seq = du_conv.shape[0]
    dm = w_my.shape[0]
    tm, tn = min(DH_TM, seq), DH_TN

    def body(dc_ref, dz_ref, dm_ref, w_ref, o_ref):
        acc = _nt(dc_ref[...], w_ref[:, 0:U_ZA])
        acc = acc + _nt(dz_ref[...], w_ref[:, U_ZA:U_CQ])
        acc = acc + _nt(dm_ref[...], w_ref[:, U_CQ:U_COLS])
        o_ref[...] = acc

    return pl.pallas_call(
        body, name="inproj_bwd", grid=(seq // tm, dm // tn),
        in_specs=[pl.BlockSpec((tm, U_ZA), lambda i, j: (i, 0)), pl.BlockSpec((tm, D_ATTN), lambda i, j: (i, 0)),
                  pl.BlockSpec((tm, MLA_COLS), lambda i, j: (i, 0)), pl.BlockSpec((tn, U_COLS), lambda i, j: (j, 0))],
        out_specs=pl.BlockSpec((tm, tn), lambda i, j: (i, j)), out_shape=SDS((seq, dm), F32),
        compiler_params=_cp(("arbitrary", "arbitrary"), 48),
    )(du_conv, du_za, du_mla, w_my)


def _prenorm_bwd(x, dh, dout, norm_g, scale):
    seq, dm = x.shape
    ts = min(ROW_T, seq)
    n = seq // ts

    def body(x_ref, dh_ref, dout_ref, g_ref, sc_ref, gx_ref, st_ref, acc_ref):
        i = pl.program_id(0)

        @pl.when(i == 0)
        def _():
            acc_ref[...] = jnp.zeros_like(acc_ref)

        xv = x_ref[...]
        xn, r = _rms(xv, dm)
        dh_v = dh_ref[...]
        gv = g_ref[...]
        one_sc = 1.0 + sc_ref[...]

        def fold(a):
            return jnp.sum(a.reshape(ts // 8, 8, dm), axis=0)

        acc_ref[0:8, :] += fold(dh_v)
        acc_ref[8:16, :] += fold(dh_v * (xn * gv))
        dxg = dh_v * one_sc
        acc_ref[16:24, :] += fold(dxg * xn)
        dxn = dxg * gv
        mean = jnp.sum(dxn * xn, axis=-1, keepdims=True) * (1.0 / dm)
        gx_ref[...] = dout_ref[...] + r * (dxn - xn * mean)

        @pl.when(i == n - 1)
        def _():
            st_ref[...] = jnp.zeros_like(st_ref)
            for k in range(3):
                st_ref[k:k + 1, :] = jnp.sum(acc_ref[8 * k:8 * k + 8, :], axis=0, keepdims=True)

    row = pl.BlockSpec((ts, dm), lambda i: (i, 0))
    vec = pl.BlockSpec((1, dm), lambda i: (0, 0))
    return pl.pallas_call(
        body, name="prenorm_bwd", grid=(n,), in_specs=[row, row, row, vec, vec],
        out_specs=[row, pl.BlockSpec((8, dm), lambda i: (0, 0))],
        out_shape=[SDS((seq, dm), F32), SDS((8, dm), F32)],
        scratch_shapes=[pltpu.VMEM((24, dm), F32)], input_output_aliases={2: 0},
        compiler_params=_cp(("arbitrary",), 52),
    )(x, dh, dout, norm_g, scale)


def _unshard_cols(g):
    return jnp.transpose(g, (1, 0, 2)).reshape(g.shape[1], -1)


def _shard_cols(w):
    r = w.shape[0]
    return jnp.transpose(w.reshape(r, N_CHIPS, -1), (1, 0, 2))


def _w_in_to_my(w):
    c4 = 4 * D_CONV
    cq, ckv, kr, za = c4, c4 + Q_LORA, c4 + Q_LORA + KV_LORA, c4 + Q_LORA + KV_LORA + QK_ROPE
    pad = jnp.zeros((w.shape[0], KR_PAD - QK_ROPE), w.dtype)
    return jnp.concatenate([w[:, :c4], w[:, za:], w[:, cq:ckv], w[:, ckv:kr], w[:, kr:za], pad], axis=1)


def _w_in_from_my(g_conv, g_za, g_mla):
    return jnp.concatenate([g_conv, g_mla[:, :Q_LORA + KV_LORA + QK_ROPE], g_za], axis=1)


def _heads_pad(w):
    r = w.shape[0]
    w3 = w.reshape(r, N_HEADS, QK_HEAD)
    return jnp.pad(w3, ((0, 0), (0, 0), (0, HEAD_PAD - QK_HEAD))).reshape(r, Q_PAD)


def _heads_unpad(w):
    r = w.shape[0]
    return w.reshape(r, N_HEADS, HEAD_PAD)[:, :, :QK_HEAD].reshape(r, N_HEADS * QK_HEAD)


def kernel(x, c, positions, ada_w, ada_b, norm_g, w_in, conv_w, q_a_g, w_q_b, kv_a_g, w_kv_b, q_g, k_g, w_out, loss_target, m_ada_w, m_ada_b, m_norm_g, m_w_in, m_conv_w, m_q_a_g, m_w_q_b, m_kv_a_g, m_w_kv_b, m_q_g, m_k_g, m_w_out, v_ada_w, v_ada_b, v_norm_g, v_w_in, v_conv_w, v_q_a_g, v_w_q_b, v_kv_a_g, v_w_kv_b, v_q_g, v_k_g, v_w_out):
    mx, my, mc = _place()
    chip = 2 * mx + my
    me = 2 * chip + mc
    seq = x.shape[1]
    x2, t2 = x[0], loss_target[0]
    cw_cols = conv_w.shape[2]

    small = jnp.zeros((8, D_MODEL), F32)
    small = small.at[0].set(c[0])
    small = small.at[1:4, :cw_cols].set(conv_w[0])
    small_all = _gather8(small, "gather_c_conv", False)[0]
    c_all = small_all[:, 0, :]
    conv_full = jnp.transpose(small_all.reshape(N_CHIPS, 2, 8, D_MODEL)[:, 0, 1:4, :cw_cols], (1, 0, 2)).reshape(3, D_CONV)

    ada_cols = ada_w.shape[2]
    b_k = lax.dynamic_slice(ada_b, (0, chip * ada_cols), (1, ada_cols))
    mod_k, sc_all = _ada_mod(c_all, ada_w[0], b_k)
    mod_all = _gather8(mod_k, "gather_mod", False)[0]
    mod_row = lax.dynamic_slice(mod_all.reshape(N_CHIPS, 2, N_DEV, ada_cols), (0, mc, me, 0), (N_CHIPS, 1, 1, ada_cols))
    mod_row = mod_row.reshape(3, D_MODEL)
    shift, scale, gate = mod_row[0:1], mod_row[1:2], mod_row[2:3]

    shards = [w_in[0].astype(BF16), w_q_b[0].astype(BF16), w_kv_b[0].astype(BF16), w_out[0].astype(BF16)]
    gathered = _allgather_shards(shards)
    g_in, g_q, g_kv, g_out = [lax.dynamic_update_slice(g, s[None], (chip, 0, 0)) for g, s in zip(gathered, shards)]
    w_my = _w_in_to_my(_unshard_cols(g_in))
    wq = _heads_pad(_unshard_cols(g_q))
    wkv = _unshard_cols(g_kv).reshape(KV_LORA, N_HEADS, QK_NOPE + V_HEAD)
    wkn = wkv[:, :, :QK_NOPE].reshape(KV_LORA, N_HEADS * QK_NOPE)
    wv = wkv[:, :, QK_NOPE:].reshape(KV_LORA, D_ATTN)
    wo = g_out.reshape(N_CHIPS * g_out.shape[1], D_MODEL)

    h, u = _inproj(x2, norm_g, scale, shift, w_my)
    y_conv = _conv_fwd(u, conv_full)
    pos = positions.reshape(seq, 1)
    inv_freq = ROPE_BASE ** (-jnp.arange(0, QK_ROPE, 2, dtype=F32) / QK_ROPE)
    freq = jnp.concatenate([inv_freq, inv_freq, jnp.zeros((LANES - QK_ROPE,), F32)]).reshape(1, LANES)
    q_g_pad = jnp.pad(q_g, ((0, 0), (0, HEAD_PAD - QK_HEAD)))
    k_g_pad = jnp.pad(k_g, ((0, 0), (0, HEAD_PAD - QK_HEAD)))
    q, k, v = _mla_prep(u, pos, freq, q_a_g, wq, kv_a_g, wkn, wv, q_g_pad, k_g_pad)
    o, y_attn, lse = _flash_fwd(q, k, v, u)
    dout, dy, dycat, st_out = _outproj_loss(y_conv, y_attn, x2, t2, gate, wo)

    dw_out = jnp.concatenate([_matmul_tn(y_conv, dy, "dw_out_conv"), _matmul_tn(y_attn, dy, "dw_out_attn")], axis=0)
    do_t, du_za, delta = _attn_gate_bwd(dycat, o, u)
    dq, dk, dv = _flash_bwd(q, k, v, do_t, lse, delta)
    du_mla, dwq, dwkn, dwv, sg_mla = _mla_bwd(dq, dk, dv, u, pos, freq, q_a_g, wq, kv_a_g, wkn, wv, q_g_pad, k_g_pad)
    du_conv, dconv_w = _conv_bwd(dycat, u, conv_full)
    dh = _inproj_bwd(du_conv, du_za, du_mla, w_my)
    grad_x, st_in = _prenorm_bwd(x2, dh, dout, norm_g, scale)
    dw_conv = _matmul_tn(h, du_conv, "dw_in_conv")
    dw_za = _matmul_tn(h, du_za, "dw_in_za")
    dw_mla = _matmul_tn(h, du_mla, "dw_in_mla")

    sgrad = jnp.zeros((8, D_MODEL), F32)
    sgrad = sgrad.at[0:2].set(st_in[0:2])
    sgrad = sgrad.at[2].set(st_out[0])
    sgrad = sgrad.at[3].set(st_in[2])
    sgrad = sgrad.at[4, :D_CONV].set(dconv_w[0]).at[4, D_CONV:].set(dconv_w[1])
    sgrad = sgrad.at[5, :D_CONV].set(dconv_w[2]).at[5, D_CONV:].set(sg_mla[0, :D_CONV])
    sgrad = sgrad.at[6, :HEAD_PAD].set(sg_mla[0, SG_KG:SG_KG + HEAD_PAD])
    sgrad = sgrad.at[7].set(st_out[1])
    sg_all, sg_sum = _gather8(sgrad, "gather_small_grads", True)
    loss = sg_sum[7, 0]
    g_ada_b = sg_sum[0:3].reshape(1, 3 * D_MODEL)
    g_norm_g = sg_sum[3:4]
    conv_sum = jnp.stack([sg_sum[4, :D_CONV], sg_sum[4, D_CONV:], sg_sum[5, :D_CONV]])
    g_conv_w = lax.dynamic_slice(conv_sum, (0, chip * cw_cols), (3, cw_cols))
    g_q_a_g = sg_sum[5:6, D_CONV + SG_QAG:D_CONV + SG_QAG + Q_LORA]
    g_kv_a_g = sg_sum[5:6, D_CONV + SG_KVAG:D_CONV + SG_KVAG + KV_LORA]
    g_q_g = sg_sum[5:6, D_CONV + SG_QG:D_CONV + SG_QG + QK_HEAD]
    g_k_g = sg_sum[6:7, :QK_HEAD]
    dmod_k = lax.dynamic_slice(sg_all[:, 0:3, :].reshape(N_DEV, 3 * D_MODEL), (0, chip * ada_cols), (N_DEV, ada_cols))

    dw_in_nat = _w_in_from_my(dw_conv, dw_za, dw_mla)
    dw_q_nat = _heads_unpad(dwq).astype(BF16)
    dw_kv_nat = jnp.concatenate([dwkn.reshape(KV_LORA, N_HEADS, QK_NOPE), dwv.reshape(KV_LORA, N_HEADS, V_HEAD)],
                                axis=2).reshape(KV_LORA, N_HEADS * (QK_NOPE + V_HEAD)).astype(BF16)
    grads = [_shard_cols(dw_in_nat), _shard_cols(dw_q_nat), _shard_cols(dw_kv_nat),
             dw_out.reshape(N_CHIPS, dw_out.shape[0] // N_CHIPS, D_MODEL)]
    theirs = _rs_core_swap(grads)
    names = ["w_in", "w_q_b", "w_kv_b", "w_out"]
    core = jnp.reshape(mc, (1,)).astype(jnp.int32)
    parts = [_add_half_bf16(g, b, core, "rs_add_" + nm) for g, b, nm in zip(grads, theirs, names)]
    recv = _rs_chip_exchange(parts)
    recv = [lax.dynamic_update_slice(r, lax.dynamic_slice(p, (chip, 0, 0), (1,) + p.shape[1:]), (chip, 0, 0))
            for r, p in zip(recv, parts)]
    halves = [_sum_chips(p, "rs_sum_" + nm) for p, nm in zip(recv, names)]
    joined = _rs_core_join(halves)
    joined = [lax.dynamic_update_slice(j, hf[None], (mc, 0, 0)) for j, hf in zip(joined, halves)]
    g_big = [j.reshape(2 * j.shape[1], j.shape[2]) for j in joined]

    g_ada_w, d_ada_w, nm_ada_w, nv_ada_w = _ada_w_update(sc_all, dmod_k, ada_w[0], m_ada_w[0], v_ada_w[0])
    upd = {}
    big = {"w_in": (w_in, m_w_in, v_w_in), "w_q_b": (w_q_b, m_w_q_b, v_w_q_b), "w_kv_b": (w_kv_b, m_w_kv_b, v_w_kv_b),
           "w_out": (w_out, m_w_out, v_w_out)}
    for nm, g in zip(names, g_big):
        w_, m_, v_ = big[nm]
        upd[nm] = (g,) + tuple(_adamw(w_[0], g, m_[0], v_[0], "adamw_" + nm))
    small_w = {"ada_b": (ada_b, m_ada_b, v_ada_b, g_ada_b), "norm_g": (norm_g, m_norm_g, v_norm_g, g_norm_g),
               "conv_w": (conv_w[0], m_conv_w[0], v_conv_w[0], g_conv_w), "q_a_g": (q_a_g, m_q_a_g, v_q_a_g, g_q_a_g),
               "kv_a_g": (kv_a_g, m_kv_a_g, v_kv_a_g, g_kv_a_g), "q_g": (q_g, m_q_g, v_q_g, g_q_g),
               "k_g": (k_g, m_k_g, v_k_g, g_k_g)}
    for nm, (w_, m_, v_, g) in small_w.items():
        upd[nm] = (g,) + tuple(_adamw(w_, g, m_, v_, "adamw_" + nm))
    upd["ada_w"] = (g_ada_w, d_ada_w, nm_ada_w, nv_ada_w)

    order = ["ada_w", "ada_b", "norm_g", "w_in", "conv_w", "q_a_g", "w_q_b", "kv_a_g", "w_kv_b", "q_g", "k_g", "w_out"]
    lead1 = {"ada_w", "w_in", "conv_w", "w_q_b", "w_kv_b", "w_out"}

    def shaped(nm, a):
        return a[None] if nm in lead1 else a

    outs = [loss, grad_x[None]]
    for idx in range(4):
        outs += [shaped(nm, upd[nm][idx]) for nm in order]
    return tuple(outs)
```

```python
import functools
import math

import jax
import jax.numpy as jnp
from jax import lax
from jax.experimental import pallas as pl
from jax.experimental.pallas import tpu as pltpu

F32 = jnp.float32
BF16 = jnp.bfloat16
MESH = pl.DeviceIdType.MESH
SDS = jax.ShapeDtypeStruct
ANY = pl.BlockSpec(memory_space=pl.ANY)

D_MODEL = 2048
D_CONV = 1024
N_HEADS = 8
QK_NOPE = 128
QK_ROPE = 64
QK_HEAD = QK_NOPE + QK_ROPE
V_HEAD = 128
D_ATTN = N_HEADS * V_HEAD
Q_LORA = 512
KV_LORA = 256
ROPE_BASE = 10000.0
EPS = 1e-6
LOG2E = math.log2(math.e)
LN2 = math.log(2.0)
ADAM_LR, ADAM_B1, ADAM_B2, ADAM_EPS, ADAM_WD, ADAM_STEP = 0.001, 0.9, 0.999, 1e-08, 0.01, 10
N_CHIPS = 4
N_DEV = 8

LANES = 128
V7X_VMEM_BYTES = 64 * 1024 * 1024
MIB = 1024 * 1024

HEAD_PAD = 256
Q_PAD = N_HEADS * HEAD_PAD
U_ZA = 4 * D_CONV
U_CQ = U_ZA + D_ATTN
U_CKV = U_CQ + Q_LORA
U_KR = U_CKV + KV_LORA
KR_PAD = 256
U_COLS = U_KR + KR_PAD
MLA_COLS = Q_LORA + KV_LORA + KR_PAD

ATT_T = 512
INPROJ_TM, INPROJ_TN = 1024, 512
ROW_T = 512
OUT_T = 256
DH_TM, DH_TN = 512, 512
TN_TM, TN_TN, TN_TK = 1024, 1024, 512


def _cp(sem=None, vmem_mib=None, **kw):
    if sem is not None:
        kw["dimension_semantics"] = sem
    if vmem_mib is not None:
        kw["vmem_limit_bytes"] = min(vmem_mib * MIB, V7X_VMEM_BYTES - 4 * MIB)
    return pltpu.CompilerParams(**kw)


def _sigmoid(z):
    return 1.0 / (1.0 + jnp.exp(-z))


def _silu_grad(z, sg):
    return sg * (1.0 + z * (1.0 - sg))


def _nt(a, b):
    return lax.dot_general(a, b, (((1,), (1,)), ((), ())), preferred_element_type=F32)


def _tn(a, b):
    return lax.dot_general(a, b, (((0,), (0,)), ((), ())), preferred_element_type=F32)


def _nn(a, b):
    return jnp.dot(a, b, preferred_element_type=F32)


def _place():
    return lax.axis_index("x"), lax.axis_index("y"), lax.axis_index("c")


def _gather8(v, name, with_sum):
    rows, cols = v.shape

    def body(v_ref, out_ref, *rest):
        if with_sum:
            sum_ref, send_sems, recv_sems = rest
        else:
            send_sems, recv_sems = rest
        mx, my, mc = _place()
        me = 4 * mx + 2 * my + mc
        out_ref[me] = v_ref[...]
        peers = []
        for d in range(1, N_DEV):
            px = 1 - mx if d & 4 else mx
            py = 1 - my if d & 2 else my
            pc = 1 - mc if d & 1 else mc
            peers.append((px, py, pc))

        def copy(d, slot, to):
            return pltpu.make_async_remote_copy(
                src_ref=v_ref, dst_ref=out_ref.at[slot], send_sem=send_sems.at[d], recv_sem=recv_sems.at[d],
                device_id=to, device_id_type=MESH)

        sends = [copy(d, me, p) for d, p in enumerate(peers)]
        for cp in sends:
            cp.start()
        for d, (px, py, pc) in enumerate(peers):
            copy(d, 4 * px + 2 * py + pc, (px, py, pc)).wait_recv()
        for cp in sends:
            cp.wait_send()
        if with_sum:
            acc = out_ref[0]
            for b in range(1, N_DEV):
                acc = acc + out_ref[b]
            sum_ref[...] = acc

    out_shape = [SDS((N_DEV, rows, cols), F32)]
    if with_sum:
        out_shape.append(SDS((rows, cols), F32))
    vm = pl.BlockSpec(memory_space=pltpu.VMEM)
    return pl.pallas_call(
        body, name=name, out_shape=out_shape, in_specs=[vm], out_specs=[vm] * len(out_shape),
        scratch_shapes=[pltpu.SemaphoreType.DMA((N_DEV - 1,)), pltpu.SemaphoreType.DMA((N_DEV - 1,))],
    )(v)


def _chips_of(mx, my):
    chips = [(mx, 1 - my), (1 - mx, my), (1 - mx, 1 - my)]
    return chips, [2 * px + py for px, py in chips]


def _allgather_shards(shards):
    na = len(shards)
    halves = [s.shape[0] // 2 for s in shards]

    def body(*refs):
        ins, outs = refs[:na], refs[na:2 * na]
        s1, r1, s2, r2 = refs[2 * na:]
        mx, my, mc = _place()
        k = 2 * mx + my
        sib = (mx, my, 1 - mc)
        chips, kks = _chips_of(mx, my)

        def half(a, slot, c):
            return outs[a].at[slot, pl.ds(c * halves[a], halves[a])]

        def mine(a):
            return ins[a].at[pl.ds(mc * halves[a], halves[a])]

        sends = []
        for a in range(na):
            for d, (px, py) in enumerate(chips):
                cp = pltpu.make_async_remote_copy(
                    src_ref=mine(a), dst_ref=half(a, k, mc), send_sem=s1.at[3 * a + d], recv_sem=r1.at[3 * a + d],
                    device_id=(px, py, mc), device_id_type=MESH)
                cp.start()
                sends.append(cp)
        for a in range(na):
            for d, (px, py) in enumerate(chips):
                pltpu.make_async_remote_copy(
                    src_ref=mine(a), dst_ref=half(a, kks[d], mc), send_sem=s1.at[3 * a + d], recv_sem=r1.at[3 * a + d],
                    device_id=(px, py, mc), device_id_type=MESH).wait_recv()
                cp = pltpu.make_async_remote_copy(
                    src_ref=half(a, kks[d], mc), dst_ref=half(a, kks[d], mc), send_sem=s2.at[3 * a + d],
                    recv_sem=r2.at[3 * a + d], device_id=sib, device_id_type=MESH)
                cp.start()
                sends.append(cp)
        for a in range(na):
            for d in range(3):
                pltpu.make_async_remote_copy(
                    src_ref=half(a, kks[d], 1 - mc), dst_ref=half(a, kks[d], 1 - mc), send_sem=s2.at[3 * a + d],
                    recv_sem=r2.at[3 * a + d], device_id=sib, device_id_type=MESH).wait_recv()
        for cp in sends:
            cp.wait_send()

    return pl.pallas_call(
        body, name="allgather_weights",
        out_shape=[SDS((N_CHIPS,) + s.shape, s.dtype) for s in shards],
        in_specs=[ANY] * na, out_specs=[ANY] * na,
        scratch_shapes=[pltpu.SemaphoreType.DMA((3 * na,))] * 4,
    )(*shards)


def _rs_core_swap(grads):
    na = len(grads)
    halves = [g.shape[1] // 2 for g in grads]

    def body(*refs):
        ins, outs = refs[:na], refs[na:2 * na]
        ssem, rsem = refs[2 * na:]
        mx, my, mc = _place()
        sib = (mx, my, 1 - mc)
        sends = []
        for a in range(na):
            cp = pltpu.make_async_remote_copy(
                src_ref=ins[a].at[:, pl.ds((1 - mc) * halves[a], halves[a])], dst_ref=outs[a],
                send_sem=ssem.at[a], recv_sem=rsem.at[a], device_id=sib, device_id_type=MESH)
            cp.start()
            sends.append(cp)
        for cp in sends:
            cp.wait_recv()
        for cp in sends:
            cp.wait_send()

    return pl.pallas_call(
        body, name="rs_core_swap", out_shape=[SDS((N_CHIPS, h) + g.shape[2:], g.dtype) for g, h in zip(grads, halves)],
        in_specs=[ANY] * na, out_specs=[ANY] * na,
        scratch_shapes=[pltpu.SemaphoreType.DMA((na,))] * 2,
    )(*grads)


def _chip_exchange_copies(ins, outs, ssem, rsem):
    mx, my, mc = _place()
    k = 2 * mx + my
    chips, kks = _chips_of(mx, my)
    sends, recvs = [], []
    for a in range(len(ins)):
        for d, (px, py) in enumerate(chips):
            def copy(dst_slot):
                return pltpu.make_async_remote_copy(
                    src_ref=ins[a].at[kks[d]], dst_ref=outs[a].at[dst_slot], send_sem=ssem.at[3 * a + d],
                    recv_sem=rsem.at[3 * a + d], device_id=(px, py, mc), device_id_type=MESH)
            sends.append(copy(k))
            recvs.append(copy(kks[d]))
    return sends, recvs


def _rs_core_join(halves):
    na = len(halves)

    def body(*refs):
        ins, outs = refs[:na], refs[na:2 * na]
        ssem, rsem = refs[2 * na:]
        mx, my, mc = _place()
        sib = (mx, my, 1 - mc)
        sends = []
        for a in range(na):
            cp = pltpu.make_async_remote_copy(
                src_ref=ins[a], dst_ref=outs[a].at[mc], send_sem=ssem.at[a], recv_sem=rsem.at[a],
                device_id=sib, device_id_type=MESH)
            cp.start()
            sends.append(cp)
        for a in range(na):
            pltpu.make_async_remote_copy(
                src_ref=ins[a], dst_ref=outs[a].at[1 - mc], send_sem=ssem.at[a], recv_sem=rsem.at[a],
                device_id=sib, device_id_type=MESH).wait_recv()
        for cp in sends:
            cp.wait_send()

    return pl.pallas_call(
        body, name="rs_core_join", out_shape=[SDS((2,) + h.shape, h.dtype) for h in halves],
        in_specs=[ANY] * na, out_specs=[ANY] * na,
        scratch_shapes=[pltpu.SemaphoreType.DMA((na,))] * 2,
    )(*halves)


def _row_tile(rows, limit):
    if rows <= limit:
        return rows
    best = None
    for t in range(16, limit + 1, 16):
        if rows % t == 0:
            best = t
    assert best is not None, rows
    return best


def _add_half_bf16(g, b, core, name):
    _, h, cols = b.shape
    tb = _row_tile(h, 512)
    nb = h // tb

    def body(core_ref, g_ref, b_ref, o_ref):
        o_ref[...] = (g_ref[...].astype(F32) + b_ref[...].astype(F32)).astype(BF16)

    spec = pl.BlockSpec((None, tb, cols), lambda kk, i, core_ref: (kk, i, 0))
    return pl.pallas_call(
        body, name=name,
        grid_spec=pltpu.PrefetchScalarGridSpec(
            num_scalar_prefetch=1, grid=(N_CHIPS, nb),
            in_specs=[pl.BlockSpec((None, tb, cols), lambda kk, i, core_ref: (kk, core_ref[0] * nb + i, 0)), spec],
            out_specs=spec),
        out_shape=SDS(b.shape, BF16), compiler_params=_cp(("arbitrary", "arbitrary")),
    )(core, g, b)


def _sum_chips(p, name):
    _, rows, cols = p.shape
    tb = _row_tile(rows, 256)

    def body(p_ref, o_ref):
        acc = p_ref[0].astype(F32)
        for j in range(1, N_CHIPS):
            acc = acc + p_ref[j].astype(F32)
        o_ref[...] = acc

    return pl.pallas_call(
        body, name=name, grid=(rows // tb,),
        in_specs=[pl.BlockSpec((N_CHIPS, tb, cols), lambda i: (0, i, 0))],
        out_specs=pl.BlockSpec((tb, cols), lambda i: (i, 0)), out_shape=SDS((rows, cols), F32),
        compiler_params=_cp(("arbitrary",)),
    )(p)


def _adamw_math(w, g, m, v):
    m2 = ADAM_B1 * m + (1.0 - ADAM_B1) * g
    v2 = ADAM_B2 * v + (1.0 - ADAM_B2) * (g * g)
    m_hat = m2 / (1.0 - ADAM_B1 ** ADAM_STEP)
    v_hat = v2 / (1.0 - ADAM_B2 ** ADAM_STEP)
    delta = -ADAM_LR * (m_hat / (jnp.sqrt(v_hat) + ADAM_EPS) + ADAM_WD * w)
    return delta, m2, v2


def _adamw(w, g, m, v, name):
    rows, cols = w.shape
    tb = _row_tile(rows, 256)

    def body(w_ref, g_ref, m_ref, v_ref, d_ref, m2_ref, v2_ref):
        d, m2, v2 = _adamw_math(w_ref[...], g_ref[...], m_ref[...], v_ref[...])
        d_ref[...] = d
        m2_ref[...] = m2
        v2_ref[...] = v2

    spec = pl.BlockSpec((tb, cols), lambda i: (i, 0))
    return pl.pallas_call(
        body, name=name, grid=(rows // tb,), in_specs=[spec] * 4, out_specs=[spec] * 3,
        out_shape=[SDS((rows, cols), F32)] * 3, compiler_params=_cp(("arbitrary",), 40),
    )(w, g, m, v)


def _ada_w_update(sc_all, dmod_k, w, m, v):
    rows, cols = w.shape
    tb = 256

    def body(s_ref, dm_ref, w_ref, m_ref, v_ref, g_ref, d_ref, m2_ref, v2_ref):
        g = _tn(s_ref[...].astype(BF16), dm_ref[...].astype(BF16))
        d, m2, v2 = _adamw_math(w_ref[...], g, m_ref[...], v_ref[...])
        g_ref[...] = g
        d_ref[...] = d
        m2_ref[...] = m2
        v2_ref[...] = v2

    spec = pl.BlockSpec((tb, cols), lambda i: (i, 0))
    return pl.pallas_call(
        body, name="ada_w_update", grid=(rows // tb,),
        in_specs=[pl.BlockSpec((N_DEV, tb), lambda i: (0, i)), pl.BlockSpec((N_DEV, cols), lambda i: (0, 0)), spec, spec, spec],
        out_specs=[spec] * 4, out_shape=[SDS((rows, cols), F32)] * 4, compiler_params=_cp(("arbitrary",), 40),
    )(sc_all, dmod_k, w, m, v)


def _ada_mod(c_all, w, b_k):
    rows, cols = w.shape
    tn = 512

    def body(c_ref, w_ref, b_ref, o_ref, s_ref):
        cv = c_ref[...]
        s = cv * _sigmoid(cv)
        s_ref[...] = s
        o_ref[...] = _nn(s.astype(BF16), w_ref[...].astype(BF16)) + b_ref[...]

    return pl.pallas_call(
        body, name="ada_mod", grid=(cols // tn,),
        in_specs=[pl.BlockSpec((N_DEV, rows), lambda j: (0, 0)), pl.BlockSpec((rows, tn), lambda j: (0, j)),
                  pl.BlockSpec((1, tn), lambda j: (0, j))],
        out_specs=[pl.BlockSpec((N_DEV, tn), lambda j: (0, j)), pl.BlockSpec((N_DEV, rows), lambda j: (0, 0))],
        out_shape=[SDS((N_DEV, cols), F32), SDS((N_DEV, rows), F32)], compiler_params=_cp(("arbitrary",)),
    )(c_all, w, b_k)


def _inproj(x, norm_g, scale, shift, w_my, shards):
    seq, dm = x.shape
    ncols = w_my.shape[1]
    tm, tn = min(INPROJ_TM, seq), INPROJ_TN
    ni, nj = seq // tm, ncols // tn
    na = len(shards)

    def body(x_ref, g_ref, sc_ref, sh_ref, w_ref, *rest):
        shard_refs, h_ref, u_ref, got_refs = rest[:na], rest[na], rest[na + 1], rest[na + 2:2 * na + 2]
        ssem, rsem = rest[2 * na + 2:]
        i, j = pl.program_id(0), pl.program_id(1)
        mx, my, mc = _place()
        k = 2 * mx + my
        chips, kks = _chips_of(mx, my)

        def copy(a, d, slot):
            return pltpu.make_async_remote_copy(
                src_ref=shard_refs[a], dst_ref=got_refs[a].at[slot], send_sem=ssem.at[3 * a + d],
                recv_sem=rsem.at[3 * a + d], device_id=(chips[d][0], chips[d][1], mc), device_id_type=MESH)

        @pl.when((i == 0) & (j == 0))
        def _():
            for a in range(na):
                for d in range(3):
                    copy(a, d, k).start()

        @pl.when(j == 0)
        def _():
            xv = x_ref[...]
            r = lax.rsqrt(jnp.mean(xv * xv, axis=-1, keepdims=True) + EPS)
            hv = (xv * r * g_ref[...]) * (1.0 + sc_ref[...]) + sh_ref[...]
            h_ref[...] = hv.astype(BF16)

        u_ref[...] = _nn(h_ref[...], w_ref[...]).astype(BF16)

        @pl.when((i == ni - 1) & (j == nj - 1))
        def _():
            for a in range(na):
                for d in range(3):
                    copy(a, d, kks[d]).wait_recv()
            for a in range(na):
                for d in range(3):
                    copy(a, d, k).wait_send()

    vec = pl.BlockSpec((1, dm), lambda i, j: (0, 0))
    outs = pl.pallas_call(
        body, name="inproj", grid=(ni, nj),
        in_specs=[pl.BlockSpec((tm, dm), lambda i, j: (i, 0)), vec, vec, vec, pl.BlockSpec((dm, tn), lambda i, j: (0, j))]
                 + [ANY] * na,
        out_specs=[pl.BlockSpec((tm, dm), lambda i, j: (i, 0)), pl.BlockSpec((tm, tn), lambda i, j: (i, j))] + [ANY] * na,
        out_shape=[SDS((seq, dm), BF16), SDS((seq, ncols), BF16)] + [SDS((N_CHIPS,) + s.shape, s.dtype) for s in shards],
        scratch_shapes=[pltpu.SemaphoreType.DMA((3 * na,))] * 2,
        compiler_params=_cp(("arbitrary", "arbitrary"), 48),
    )(x, norm_g, scale, shift, w_my, *shards)
    return outs[0], outs[1], outs[2:]


HALO = 16


def _conv_taps(uc, halo, ext_ref, ts, causal):
    if causal:
        ext_ref[0:HALO, :] = halo
        ext_ref[HALO:HALO + ts, :] = uc
        return ext_ref[pl.ds(HALO - 1, ts), :], ext_ref[pl.ds(HALO - 2, ts), :]
    ext_ref[0:ts, :] = uc
    ext_ref[ts:ts + HALO, :] = halo
    return ext_ref[pl.ds(1, ts), :], ext_ref[pl.ds(2, ts), :]


def _conv_fwd(u, conv_w):
    seq = u.shape[0]
    ts = min(ROW_T, seq)
    hb = ts // HALO

    def body(xc_ref, bc_ref, cc_ref, zc_ref, xp_ref, cp_ref, w_ref, y_ref, ext_ref):
        i = pl.program_id(0)
        uc = cc_ref[...].astype(F32) * xc_ref[...].astype(F32)
        up = cp_ref[...].astype(F32) * xp_ref[...].astype(F32)
        up = jnp.where(i > 0, up, 0.0)
        u1, u2 = _conv_taps(uc, up, ext_ref, ts, True)
        conv = w_ref[0:1, :] * u2 + w_ref[1:2, :] * u1 + w_ref[2:3, :] * uc
        z = zc_ref[...].astype(F32)
        y_ref[...] = ((bc_ref[...].astype(F32) * conv) * (z * _sigmoid(z))).astype(BF16)

    def col(cb):
        return pl.BlockSpec((ts, D_CONV), lambda i: (i, cb))

    def prev(cb):
        return pl.BlockSpec((HALO, D_CONV), lambda i: (jnp.maximum(i * hb - 1, 0), cb))

    return pl.pallas_call(
        body, name="conv_fwd", grid=(seq // ts,),
        in_specs=[col(0), col(1), col(2), col(3), prev(0), prev(2), pl.BlockSpec((3, D_CONV), lambda i: (0, 0))],
        out_specs=pl.BlockSpec((ts, D_CONV), lambda i: (i, 0)), out_shape=SDS((seq, D_CONV), BF16),
        scratch_shapes=[pltpu.VMEM((ts + HALO, D_CONV), F32)],
        compiler_params=_cp(("arbitrary",), 40),
    )(u, u, u, u, u, u, conv_w)


def _rope_tables(pos_ref, freq_ref):
    ang = pos_ref[...].astype(F32) * freq_ref[...]
    lane = lax.broadcasted_iota(jnp.int32, ang.shape, 1)
    cs, sn = jnp.cos(ang), jnp.sin(ang)
    half = QK_ROPE // 2
    cos_t = jnp.where(lane < QK_ROPE, cs, 0.0)
    sin_lo = jnp.where(lane < half, sn, 0.0)
    sin_hi = jnp.where((lane >= half) & (lane < QK_ROPE), sn, 0.0)
    return cos_t, sin_lo, sin_hi


def _rope(blk, tables):
    cos_t, sin_lo, sin_hi = tables
    half = QK_ROPE // 2
    return blk * cos_t - pltpu.roll(blk, LANES - half, 1) * sin_lo + pltpu.roll(blk, half, 1) * sin_hi


def _rope_bwd(g, tables):
    cos_t, sin_lo, sin_hi = tables
    half = QK_ROPE // 2
    return g * cos_t + pltpu.roll(g, LANES - half, 1) * sin_lo - pltpu.roll(g, half, 1) * sin_hi


def _rms(v, n):
    r = lax.rsqrt(jnp.sum(v * v, axis=-1, keepdims=True) * (1.0 / n) + EPS)
    return v * r, r


def _mla_prep(u, pos, freq, q_a_g, wq, kv_a_g, wkn, wv, q_g, k_g):
    seq = u.shape[0]
    ts = min(ROW_T, seq)
    qscale = LOG2E / math.sqrt(QK_HEAD)

    def body(cq_ref, ckv_ref, kr_ref, pos_ref, freq_ref, qag_ref, wq_ref, kvag_ref, wkn_ref, wv_ref, qg_ref, kg_ref,
             q_ref, k_ref, v_ref):
        tables = _rope_tables(pos_ref, freq_ref)
        cqn, _ = _rms(cq_ref[...].astype(F32), Q_LORA)
        qp = _nn((cqn * qag_ref[...]).astype(BF16), wq_ref[...])
        qg = qg_ref[...]
        for h in range(N_HEADS):
            lo = h * HEAD_PAD
            qn, _ = _rms(qp[:, lo:lo + HEAD_PAD], QK_HEAD)
            qn = qn * qg
            q_ref[:, lo:lo + LANES] = (qn[:, :LANES] * qscale).astype(BF16)
            q_ref[:, lo + LANES:lo + HEAD_PAD] = (_rope(qn[:, LANES:], tables) * qscale).astype(BF16)
        ckvn, _ = _rms(ckv_ref[...].astype(F32), KV_LORA)
        ckvb = (ckvn * kvag_ref[...]).astype(BF16)
        kn = _nn(ckvb, wkn_ref[...])
        v_ref[...] = _nn(ckvb, wv_ref[...]).astype(BF16)
        kr = kr_ref[:, 0:LANES].astype(F32)
        ssr = jnp.sum(kr * kr, axis=-1, keepdims=True)
        kg = kg_ref[...]
        for h in range(N_HEADS):
            knh = kn[:, h * QK_NOPE:(h + 1) * QK_NOPE]
            r = lax.rsqrt((jnp.sum(knh * knh, axis=-1, keepdims=True) + ssr) * (1.0 / QK_HEAD) + EPS)
            lo = h * HEAD_PAD
            k_ref[:, lo:lo + LANES] = (knh * r * kg[:, :LANES]).astype(BF16)
            k_ref[:, lo + LANES:lo + HEAD_PAD] = _rope(kr * r * kg[:, LANES:], tables).astype(BF16)

    def full(a):
        return pl.BlockSpec(a.shape, lambda i: (0,) * a.ndim)

    return pl.pallas_call(
        body, name="mla_prep", grid=(seq // ts,),
        in_specs=[pl.BlockSpec((ts, Q_LORA), lambda i: (i, U_CQ // Q_LORA)),
                  pl.BlockSpec((ts, KV_LORA), lambda i: (i, U_CKV // KV_LORA)),
                  pl.BlockSpec((ts, KR_PAD), lambda i: (i, U_KR // KR_PAD)),
                  pl.BlockSpec((ts, 1), lambda i: (i, 0)), full(freq), full(q_a_g), full(wq), full(kv_a_g), full(wkn),
                  full(wv), full(q_g), full(k_g)],
        out_specs=[pl.BlockSpec((ts, Q_PAD), lambda i: (i, 0)), pl.BlockSpec((ts, Q_PAD), lambda i: (i, 0)),
                   pl.BlockSpec((ts, D_ATTN), lambda i: (i, 0))],
        out_shape=[SDS((seq, Q_PAD), BF16), SDS((seq, Q_PAD), BF16), SDS((seq, D_ATTN), BF16)],
        compiler_params=_cp(("arbitrary",), 48),
    )(u, u, u, pos, freq, q_a_g, wq, kv_a_g, wkn, wv, q_g, k_g)


def _causal_mask(t, tq, q0):
    return lax.broadcasted_iota(jnp.int32, (t, tq), 0) <= lax.broadcasted_iota(jnp.int32, (t, tq), 1) + q0


def _flash_fwd(q, k, v, u):
    seq = q.shape[0]
    t = min(ATT_T, seq)
    za_blk = U_ZA // V_HEAD

    def body(q_ref, k_ref, v_ref, z_ref, o_ref, y_ref, lse_ref, s_a, s_b, m_ref, l_ref, acc_ref):
        i = pl.program_id(1)
        ones = jnp.ones((16, t), BF16)

        def keys(j):
            return pl.ds(pl.multiple_of(j * t, t), t)

        def scores(j, s_ref):
            s_ref[...] = _nt(k_ref[keys(j), :], q_ref[...])

        def absorb(j, s_ref, masked):
            s = s_ref[...]
            if masked:
                s = jnp.where(_causal_mask(t, t, 0), s, -jnp.inf)
            m = m_ref[...]
            m_new = jnp.maximum(m, jnp.max(s, axis=0, keepdims=True))
            alpha = jnp.exp2(m - m_new)
            p = jnp.exp2((s - m_new).astype(BF16))
            m_ref[...] = m_new
            l_ref[...] = alpha * l_ref[...] + _nn(ones, p)[0:1, :]
            acc_ref[...] = alpha * acc_ref[...] + _tn(v_ref[keys(j), :], p)

        scores(0, s_a)
        m_ref[...] = jnp.full_like(m_ref, -jnp.inf)
        l_ref[...] = jnp.zeros_like(l_ref)
        acc_ref[...] = jnp.zeros_like(acc_ref)

        def pair(jj, carry):
            j = 2 * jj
            scores(j + 1, s_b)
            absorb(j, s_a, False)
            scores(j + 2, s_a)
            absorb(j + 1, s_b, False)
            return carry

        lax.fori_loop(0, i // 2, pair, 0)

        @pl.when(i % 2 == 1)
        def _():
            scores(i, s_b)
            absorb(i - 1, s_a, False)
            absorb(i, s_b, True)

        @pl.when(i % 2 == 0)
        def _():
            absorb(i, s_a, True)

        l = l_ref[...]
        o = (acc_ref[...] * (1.0 / l)).T
        lse_ref[...] = m_ref[...] + jnp.log2(l)
        o_ref[...] = o.astype(BF16)
        z = z_ref[...].astype(F32)
        y_ref[...] = (o * (z * _sigmoid(z))).astype(BF16)

    tile = pl.BlockSpec((t, V_HEAD), lambda h, i: (i, h))
    return pl.pallas_call(
        body, name="flash_fwd", grid=(N_HEADS, seq // t),
        in_specs=[pl.BlockSpec((t, HEAD_PAD), lambda h, i: (i, h)), pl.BlockSpec((seq, HEAD_PAD), lambda h, i: (0, h)),
                  pl.BlockSpec((seq, V_HEAD), lambda h, i: (0, h)), pl.BlockSpec((t, V_HEAD), lambda h, i: (i, za_blk + h))],
        out_specs=[tile, tile, pl.BlockSpec((None, 1, t), lambda h, i: (h, 0, i))],
        out_shape=[SDS((seq, D_ATTN), BF16), SDS((seq, D_ATTN), BF16), SDS((N_HEADS, 1, seq), F32)],
        scratch_shapes=[pltpu.VMEM((t, t), F32), pltpu.VMEM((t, t), F32), pltpu.VMEM((1, t), F32), pltpu.VMEM((1, t), F32),
                        pltpu.VMEM((V_HEAD, t), F32)],
        compiler_params=_cp(("arbitrary", "arbitrary"), 40),
    )(q, k, v, u)


def _outproj_loss(y_conv, y_attn, x, target, gate, w_out):
    seq, dm = x.shape
    ts = min(OUT_T, seq)
    n = seq // ts
    dmix = w_out.shape[0]

    def body(yc_ref, ya_ref, x_ref, t_ref, gate_ref, wo_hbm, dout_ref, dy_ref, dyc_ref, stats_ref, wo_ref, sem, acc_ref):
        i = pl.program_id(0)

        @pl.when(i == 0)
        def _():
            cp = pltpu.make_async_copy(wo_hbm, wo_ref, sem)
            cp.start()
            cp.wait()
            acc_ref[...] = jnp.zeros_like(acc_ref)

        y = _nn(yc_ref[...], wo_ref[0:D_CONV, :]) + _nn(ya_ref[...], wo_ref[D_CONV:dmix, :])
        gate_v = gate_ref[...]
        diff = (x_ref[...] + gate_v * y) - t_ref[...]
        dout = diff * (1.0 / dm)
        dout_ref[...] = dout
        acc_ref[0:8, :] += jnp.sum((dout * y).reshape(ts // 8, 8, dm), axis=0)
        acc_ref[8:16, :] += jnp.sum((diff * diff).reshape(ts // 8, 8, dm), axis=0)
        dy = (dout * gate_v).astype(BF16)
        dy_ref[...] = dy
        dyc_ref[...] = _nt(dy, wo_ref[...]).astype(BF16)

        @pl.when(i == n - 1)
        def _():
            stats_ref[...] = jnp.zeros_like(stats_ref)
            stats_ref[0:1, :] = jnp.sum(acc_ref[0:8, :], axis=0, keepdims=True)
            loss = jnp.sum(acc_ref[8:16, :]) * (0.5 / dm)
            stats_ref[1:2, :] = jnp.full((1, dm), loss, F32)

    row = pl.BlockSpec((ts, dm), lambda i: (i, 0))
    half = pl.BlockSpec((ts, D_CONV), lambda i: (i, 0))
    return pl.pallas_call(
        body, name="outproj_loss", grid=(n,),
        in_specs=[half, half, row, row, pl.BlockSpec((1, dm), lambda i: (0, 0)), ANY],
        out_specs=[row, row, pl.BlockSpec((ts, dmix), lambda i: (i, 0)), pl.BlockSpec((8, dm), lambda i: (0, 0))],
        out_shape=[SDS((seq, dm), F32), SDS((seq, dm), BF16), SDS((seq, dmix), BF16), SDS((8, dm), F32)],
        scratch_shapes=[pltpu.VMEM(w_out.shape, BF16), pltpu.SemaphoreType.DMA(()), pltpu.VMEM((16, dm), F32)],
        compiler_params=_cp(("arbitrary",), 52),
    )(y_conv, y_attn, x, target, gate, w_out)


def _matmul_tn(a, b, name):
    seq, m = a.shape
    n = b.shape[1]
    tm, tn, tk = min(TN_TM, m), min(TN_TN, n), min(TN_TK, seq)
    nk = seq // tk

    def body(a_ref, b_ref, o_ref, acc_ref):
        kk = pl.program_id(2)

        @pl.when(kk == 0)
        def _():
            acc_ref[...] = jnp.zeros_like(acc_ref)

        acc_ref[...] += _tn(a_ref[...], b_ref[...])

        @pl.when(kk == nk - 1)
        def _():
            o_ref[...] = acc_ref[...].astype(BF16)

    return pl.pallas_call(
        body, name=name, grid=(m // tm, n // tn, nk),
        in_specs=[pl.BlockSpec((tk, tm), lambda i, j, kk: (kk, i)), pl.BlockSpec((tk, tn), lambda i, j, kk: (kk, j))],
        out_specs=pl.BlockSpec((tm, tn), lambda i, j, kk: (i, j)), out_shape=SDS((m, n), BF16),
        scratch_shapes=[pltpu.VMEM((tm, tn), F32)],
        compiler_params=_cp(("arbitrary", "arbitrary", "arbitrary"), 40),
    )(a, b)


def _attn_gate_bwd(dycat, o, u):
    seq = o.shape[0]
    ts = min(ROW_T, seq)

    def body(dy_ref, o_ref, z_ref, dot_ref, dz_ref, dl_ref):
        dy = dy_ref[...].astype(F32)
        ov = o_ref[...].astype(F32)
        z = z_ref[...].astype(F32)
        sg = _sigmoid(z)
        do = dy * (z * sg)
        dz_ref[...] = (dy * ov * _silu_grad(z, sg)).astype(BF16)
        prod = do * ov
        ones = jnp.ones((8, V_HEAD), F32)
        for h in range(N_HEADS):
            cols = slice(h * V_HEAD, (h + 1) * V_HEAD)
            dot_ref[h] = do[:, cols].T.astype(BF16)
            rows = lax.dot_general(ones, prod[:, cols], (((1,), (1,)), ((), ())), precision=lax.Precision.HIGHEST,
                                   preferred_element_type=F32)
            dl_ref[h] = rows[0:1, :]

    blk = pl.BlockSpec((ts, D_ATTN), lambda i: (i, 0))
    return pl.pallas_call(
        body, name="attn_gate_bwd", grid=(seq // ts,),
        in_specs=[pl.BlockSpec((ts, D_ATTN), lambda i: (i, 1)), blk, pl.BlockSpec((ts, D_ATTN), lambda i: (i, U_ZA // D_ATTN))],
        out_specs=[pl.BlockSpec((N_HEADS, V_HEAD, ts), lambda i: (0, 0, i)), blk,
                   pl.BlockSpec((N_HEADS, 1, ts), lambda i: (0, 0, i))],
        out_shape=[SDS((N_HEADS, V_HEAD, seq), BF16), SDS((seq, D_ATTN), BF16), SDS((N_HEADS, 1, seq), F32)],
        compiler_params=_cp(("arbitrary",), 40),
    )(dycat, o, u)


def _flash_bwd(q, k, v, do_t, lse, delta):
    seq = q.shape[0]
    t = min(ATT_T, seq)
    n = seq // t

    def body(k_ref, v_ref, q_ref, dot_ref, lse_ref, dl_ref, dq_ref, dk_ref, dv_ref, s_a, s_b, dp_a, dp_b, dvt_ref):
        j = pl.program_id(1)

        @pl.when(j == 0)
        def _():
            dq_ref[...] = jnp.zeros_like(dq_ref)

        def rows(r):
            return pl.ds(pl.multiple_of((j + r) * t, t), t)

        def products(r, s_ref, dp_ref):
            s_ref[...] = _nt(k_ref[...], q_ref[rows(r), :])
            dp_ref[...] = _nn(v_ref[...], dot_ref[:, rows(r)])

        def absorb(r, s_ref, dp_ref, masked):
            p = jnp.exp2((s_ref[...] - lse_ref[:, rows(r)]).astype(BF16))
            if masked:
                p = jnp.where(_causal_mask(t, t, 0), p, jnp.zeros_like(p))
            dvt_ref[...] += _nt(dot_ref[:, rows(r)], p)
            ds = p * (dp_ref[...] - dl_ref[:, rows(r)]).astype(BF16)
            dk_ref[...] += _nn(ds, q_ref[rows(r), :])
            dq_ref[rows(r), :] += _tn(ds, k_ref[...])

        dk_ref[...] = jnp.zeros_like(dk_ref)
        dvt_ref[...] = jnp.zeros_like(dvt_ref)
        products(0, s_a, dp_a)
        last = n - 1 - j

        @pl.when(last == 0)
        def _():
            absorb(0, s_a, dp_a, True)

        @pl.when(last > 0)
        def _():
            products(1, s_b, dp_b)
            absorb(0, s_a, dp_a, True)

        inner = jnp.maximum(last - 1, 0)

        def pair(pp, carry):
            r = 1 + 2 * pp
            products(r + 1, s_a, dp_a)
            absorb(r, s_b, dp_b, False)
            products(r + 2, s_b, dp_b)
            absorb(r + 1, s_a, dp_a, False)
            return carry

        lax.fori_loop(0, inner // 2, pair, 0)

        @pl.when((last > 0) & (inner % 2 == 1))
        def _():
            products(last, s_a, dp_a)
            absorb(last - 1, s_b, dp_b, False)
            absorb(last, s_a, dp_a, False)

        @pl.when((last > 0) & (inner % 2 == 0))
        def _():
            absorb(last, s_b, dp_b, False)

        dv_ref[...] = dvt_ref[...].T.astype(BF16)

    row = pl.BlockSpec((None, 1, seq), lambda h, j: (h, 0, 0))
    return pl.pallas_call(
        body, name="flash_bwd", grid=(N_HEADS, n),
        in_specs=[pl.BlockSpec((t, HEAD_PAD), lambda h, j: (j, h)), pl.BlockSpec((t, V_HEAD), lambda h, j: (j, h)),
                  pl.BlockSpec((seq, HEAD_PAD), lambda h, j: (0, h)), pl.BlockSpec((None, V_HEAD, seq), lambda h, j: (h, 0, 0)),
                  row, row],
        out_specs=[pl.BlockSpec((seq, HEAD_PAD), lambda h, j: (0, h)), pl.BlockSpec((t, HEAD_PAD), lambda h, j: (j, h)),
                   pl.BlockSpec((t, V_HEAD), lambda h, j: (j, h))],
        out_shape=[SDS((seq, Q_PAD), F32), SDS((seq, Q_PAD), F32), SDS((seq, D_ATTN), BF16)],
        scratch_shapes=[pltpu.VMEM((t, t), F32)] * 4 + [pltpu.VMEM((V_HEAD, t), F32)],
        compiler_params=_cp(("arbitrary", "arbitrary"), 56),
    )(k, v, q, do_t, lse, delta)


SG_QAG, SG_KVAG, SG_QG, SG_KG, SG_COLS = 0, Q_LORA, Q_LORA + KV_LORA, Q_LORA + KV_LORA + HEAD_PAD, D_MODEL


def _mla_bwd(dq, dk, dv, u, pos, freq, q_a_g, wq, kv_a_g, wkn, wv, q_g, k_g):
    seq = u.shape[0]
    ts = min(ROW_T, seq)
    n = seq // ts
    qscale = 1.0 / math.sqrt(QK_HEAD)

    def body(dq_ref, dk_ref, dv_ref, cq_ref, ckv_ref, kr_ref, pos_ref, freq_ref, qag_ref, wq_ref, kvag_ref, wkn_ref,
             wv_ref, qg_ref, kg_ref, du_ref, dwq_ref, dwkn_ref, dwv_ref, sg_ref, dqp_ref, dkn_ref):
        i = pl.program_id(0)

        @pl.when(i == 0)
        def _():
            dwq_ref[...] = jnp.zeros_like(dwq_ref)
            dwkn_ref[...] = jnp.zeros_like(dwkn_ref)
            dwv_ref[...] = jnp.zeros_like(dwv_ref)
            sg_ref[...] = jnp.zeros_like(sg_ref)

        tables = _rope_tables(pos_ref, freq_ref)

        cq = cq_ref[...].astype(F32)
        cqn, rq = _rms(cq, Q_LORA)
        qag = qag_ref[...]
        cqb = (cqn * qag).astype(BF16)
        qp = _nn(cqb, wq_ref[...])
        qg = qg_ref[...]
        dqg = jnp.zeros((1, HEAD_PAD), F32)
        for h in range(N_HEADS):
            lo = h * HEAD_PAD
            xn, r = _rms(qp[:, lo:lo + HEAD_PAD], QK_HEAD)
            g = jnp.concatenate([dq_ref[:, lo:lo + LANES], _rope_bwd(dq_ref[:, lo + LANES:lo + HEAD_PAD], tables)],
                                axis=-1) * qscale
            dqg = dqg + jnp.sum(g * xn, axis=0, keepdims=True)
            gy = g * qg
            mean = jnp.sum(gy * xn, axis=-1, keepdims=True) * (1.0 / QK_HEAD)
            dqp_ref[:, lo:lo + HEAD_PAD] = (r * (gy - xn * mean)).astype(BF16)
        dqp = dqp_ref[...]
        dwq_ref[...] += _tn(cqb, dqp)
        dcqn = _nt(dqp, wq_ref[...])
        sg_ref[0:1, SG_QAG:SG_QAG + Q_LORA] += jnp.sum(dcqn * cqn, axis=0, keepdims=True)
        sg_ref[0:1, SG_QG:SG_QG + HEAD_PAD] += dqg
        gy = dcqn * qag
        mean = jnp.sum(gy * cqn, axis=-1, keepdims=True) * (1.0 / Q_LORA)
        du_ref[:, 0:Q_LORA] = (rq * (gy - cqn * mean)).astype(BF16)

        ckv = ckv_ref[...].astype(F32)
        ckvn, rkv = _rms(ckv, KV_LORA)
        kvag = kvag_ref[...]
        ckvb = (ckvn * kvag).astype(BF16)
        kn = _nn(ckvb, wkn_ref[...])
        kr = kr_ref[:, 0:LANES].astype(F32)
        ssr = jnp.sum(kr * kr, axis=-1, keepdims=True)
        kg = kg_ref[...]
        kg_n, kg_r = kg[:, :LANES] * LN2, kg[:, LANES:] * LN2
        dkg_n = jnp.zeros((1, LANES), F32)
        dkg_r = jnp.zeros((1, LANES), F32)
        dkr = jnp.zeros((ts, LANES), F32)
        for h in range(N_HEADS):
            knh = kn[:, h * QK_NOPE:(h + 1) * QK_NOPE]
            r = lax.rsqrt((jnp.sum(knh * knh, axis=-1, keepdims=True) + ssr) * (1.0 / QK_HEAD) + EPS)
            xn_n, xn_r = knh * r, kr * r
            lo = h * HEAD_PAD
            g_n = dk_ref[:, lo:lo + LANES]
            g_r = _rope_bwd(dk_ref[:, lo + LANES:lo + HEAD_PAD], tables)
            dkg_n = dkg_n + jnp.sum(g_n * xn_n, axis=0, keepdims=True)
            dkg_r = dkg_r + jnp.sum(g_r * xn_r, axis=0, keepdims=True)
            gy_n, gy_r = g_n * kg_n, g_r * kg_r
            mean = (jnp.sum(gy_n * xn_n, axis=-1, keepdims=True) + jnp.sum(gy_r * xn_r, axis=-1, keepdims=True)) * (1.0 / QK_HEAD)
            dkn_ref[:, h * QK_NOPE:(h + 1) * QK_NOPE] = (r * (gy_n - xn_n * mean)).astype(BF16)
            dkr = dkr + r * (gy_r - xn_r * mean)
        dkn = dkn_ref[...]
        dvv = dv_ref[...]
        dwkn_ref[...] += _tn(ckvb, dkn)
        dwv_ref[...] += _tn(ckvb, dvv)
        dckvn = _nt(dkn, wkn_ref[...]) + _nt(dvv, wv_ref[...])
        sg_ref[0:1, SG_KVAG:SG_KVAG + KV_LORA] += jnp.sum(dckvn * ckvn, axis=0, keepdims=True)
        sg_ref[0:1, SG_KG:SG_KG + LANES] += dkg_n * LN2
        sg_ref[0:1, SG_KG + LANES:SG_KG + HEAD_PAD] += dkg_r * LN2
        gy = dckvn * kvag
        mean = jnp.sum(gy * ckvn, axis=-1, keepdims=True) * (1.0 / KV_LORA)
        du_ref[:, Q_LORA:Q_LORA + KV_LORA] = (rkv * (gy - ckvn * mean)).astype(BF16)
        du_ref[:, Q_LORA + KV_LORA:Q_LORA + KV_LORA + LANES] = dkr.astype(BF16)
        du_ref[:, Q_LORA + KV_LORA + LANES:MLA_COLS] = jnp.zeros((ts, LANES), BF16)

    def full(a):
        return pl.BlockSpec(a.shape, lambda i: (0,) * a.ndim)

    wide = pl.BlockSpec((ts, Q_PAD), lambda i: (i, 0))
    return pl.pallas_call(
        body, name="mla_bwd", grid=(n,),
        in_specs=[wide, wide, pl.BlockSpec((ts, D_ATTN), lambda i: (i, 0)),
                  pl.BlockSpec((ts, Q_LORA), lambda i: (i, U_CQ // Q_LORA)),
                  pl.BlockSpec((ts, KV_LORA), lambda i: (i, U_CKV // KV_LORA)),
                  pl.BlockSpec((ts, KR_PAD), lambda i: (i, U_KR // KR_PAD)),
                  pl.BlockSpec((ts, 1), lambda i: (i, 0)), full(freq), full(q_a_g), full(wq), full(kv_a_g), full(wkn),
                  full(wv), full(q_g), full(k_g)],
        out_specs=[pl.BlockSpec((ts, MLA_COLS), lambda i: (i, 0)), pl.BlockSpec((Q_LORA, Q_PAD), lambda i: (0, 0)),
                   pl.BlockSpec((KV_LORA, D_ATTN), lambda i: (0, 0)), pl.BlockSpec((KV_LORA, D_ATTN), lambda i: (0, 0)),
                   pl.BlockSpec((8, SG_COLS), lambda i: (0, 0))],
        out_shape=[SDS((seq, MLA_COLS), BF16), SDS((Q_LORA, Q_PAD), F32), SDS((KV_LORA, D_ATTN), F32),
                   SDS((KV_LORA, D_ATTN), F32), SDS((8, SG_COLS), F32)],
        scratch_shapes=[pltpu.VMEM((ts, Q_PAD), BF16), pltpu.VMEM((ts, D_ATTN), BF16)],
        compiler_params=_cp(("arbitrary",), 56),
    )(dq, dk, dv, u, u, u, pos, freq, q_a_g, wq, kv_a_g, wkn, wv, q_g, k_g)


def _conv_bwd(dycat, u, conv_w):
    seq = u.shape[0]
    ts = min(ROW_T, seq)
    n = seq // ts
    hb = ts // HALO

    def body(dy_ref, xc_ref, bc_ref, cc_ref, zc_ref, xp_ref, cp_ref, dyn_ref, bn_ref, zn_ref, w_ref,
             du_ref, dw_ref, ext_ref):
        i = pl.program_id(0)

        @pl.when(i == 0)
        def _():
            dw_ref[...] = jnp.zeros_like(dw_ref)

        xc = xc_ref[...].astype(F32)
        cc = cc_ref[...].astype(F32)
        uc = cc * xc
        up = jnp.where(i > 0, cp_ref[...].astype(F32) * xp_ref[...].astype(F32), 0.0)
        u1, u2 = _conv_taps(uc, up, ext_ref, ts, True)
        w0, w1, w2 = w_ref[0:1, :], w_ref[1:2, :], w_ref[2:3, :]
        conv = w0 * u2 + w1 * u1 + w2 * uc
        z = zc_ref[...].astype(F32)
        sg = _sigmoid(z)
        sz = z * sg
        b = bc_ref[...].astype(F32)
        dy = dy_ref[...].astype(F32)
        du_ref[:, 3 * D_CONV:4 * D_CONV] = (dy * (b * conv) * _silu_grad(z, sg)).astype(BF16)
        du_ref[:, D_CONV:2 * D_CONV] = (dy * sz * conv).astype(BF16)
        dconv = dy * sz * b
        dw_ref[0:1, :] += jnp.sum(dconv * u2, axis=0, keepdims=True)
        dw_ref[1:2, :] += jnp.sum(dconv * u1, axis=0, keepdims=True)
        dw_ref[2:3, :] += jnp.sum(dconv * uc, axis=0, keepdims=True)
        zn = zn_ref[...].astype(F32)
        dnext = dyn_ref[...].astype(F32) * (zn * _sigmoid(zn)) * bn_ref[...].astype(F32)
        dnext = jnp.where(i < n - 1, dnext, 0.0)
        d1, d2 = _conv_taps(dconv, dnext, ext_ref, ts, False)
        du = w2 * dconv + w1 * d1 + w0 * d2
        du_ref[:, 2 * D_CONV:3 * D_CONV] = (du * xc).astype(BF16)
        du_ref[:, 0:D_CONV] = (du * cc).astype(BF16)

    def col(cb):
        return pl.BlockSpec((ts, D_CONV), lambda i: (i, cb))

    def prev(cb):
        return pl.BlockSpec((HALO, D_CONV), lambda i: (jnp.maximum(i * hb - 1, 0), cb))

    def nxt(cb):
        return pl.BlockSpec((HALO, D_CONV), lambda i: (jnp.minimum((i + 1) * hb, n * hb - 1), cb))

    return pl.pallas_call(
        body, name="conv_bwd", grid=(n,),
        in_specs=[col(0), col(0), col(1), col(2), col(3), prev(0), prev(2), nxt(0), nxt(1), nxt(3),
                  pl.BlockSpec((3, D_CONV), lambda i: (0, 0))],
        out_specs=[pl.BlockSpec((ts, 4 * D_CONV), lambda i: (i, 0)), pl.BlockSpec((8, D_CONV), lambda i: (0, 0))],
        out_shape=[SDS((seq, 4 * D_CONV), BF16), SDS((8, D_CONV), F32)],
        scratch_shapes=[pltpu.VMEM((ts + HALO, D_CONV), F32)],
        compiler_params=_cp(("arbitrary",), 48),
    )(dycat, u, u, u, u, u, u, dycat, u, u, conv_w)


def _inproj_bwd(du_conv, du_za, du_mla, w_my, parts):
    seq = du_conv.shape[0]
    dm = w_my.shape[0]
    tm, tn = min(DH_TM, seq), DH_TN
    ni, nj = seq // tm, dm // tn
    na = len(parts)

    def body(dc_ref, dz_ref, dm_ref, w_ref, *rest):
        part_refs, o_ref, recv_refs = rest[:na], rest[na], rest[na + 1:2 * na + 1]
        ssem, rsem = rest[2 * na + 1:]
        i, j = pl.program_id(0), pl.program_id(1)
        sends, recvs = _chip_exchange_copies(part_refs, recv_refs, ssem, rsem)

        @pl.when((i == 0) & (j == 0))
        def _():
            for cp in sends:
                cp.start()

        acc = _nt(dc_ref[...], w_ref[:, 0:U_ZA])
        acc = acc + _nt(dz_ref[...], w_ref[:, U_ZA:U_CQ])
        acc = acc + _nt(dm_ref[...], w_ref[:, U_CQ:U_COLS])
        o_ref[...] = acc

        @pl.when((i == ni - 1) & (j == nj - 1))
        def _():
            for cp in recvs:
                cp.wait_recv()
            for cp in sends:
                cp.wait_send()

    outs = pl.pallas_call(
        body, name="inproj_bwd", grid=(ni, nj),
        in_specs=[pl.BlockSpec((tm, U_ZA), lambda i, j: (i, 0)), pl.BlockSpec((tm, D_ATTN), lambda i, j: (i, 0)),
                  pl.BlockSpec((tm, MLA_COLS), lambda i, j: (i, 0)), pl.BlockSpec((tn, U_COLS), lambda i, j: (j, 0))]
                 + [ANY] * na,
        out_specs=[pl.BlockSpec((tm, tn), lambda i, j: (i, j))] + [ANY] * na,
        out_shape=[SDS((seq, dm), F32)] + [SDS(p.shape, p.dtype) for p in parts],
        scratch_shapes=[pltpu.SemaphoreType.DMA((3 * na,))] * 2,
        compiler_params=_cp(("arbitrary", "arbitrary"), 48),
    )(du_conv, du_za, du_mla, w_my, *parts)
    return outs[0], outs[1:]


def _prenorm_bwd(x, dh, dout, norm_g, scale):
    seq, dm = x.shape
    ts = min(ROW_T, seq)
    n = seq // ts

    def body(x_ref, dh_ref, dout_ref, g_ref, sc_ref, gx_ref, st_ref, acc_ref):
        i = pl.program_id(0)

        @pl.when(i == 0)
        def _():
            acc_ref[...] = jnp.zeros_like(acc_ref)

        xv = x_ref[...]
        xn, r = _rms(xv, dm)
        dh_v = dh_ref[...]
        gv = g_ref[...]
        one_sc = 1.0 + sc_ref[...]

        def fold(a):
            return jnp.sum(a.reshape(ts // 8, 8, dm), axis=0)

        acc_ref[0:8, :] += fold(dh_v)
        acc_ref[8:16, :] += fold(dh_v * (xn * gv))
        dxg = dh_v * one_sc
        acc_ref[16:24, :] += fold(dxg * xn)
        dxn = dxg * gv
        mean = jnp.sum(dxn * xn, axis=-1, keepdims=True) * (1.0 / dm)
        gx_ref[...] = dout_ref[...] + r * (dxn - xn * mean)

        @pl.when(i == n - 1)
        def _():
            st_ref[...] = jnp.zeros_like(st_ref)
            for k in range(3):
                st_ref[k:k + 1, :] = jnp.sum(acc_ref[8 * k:8 * k + 8, :], axis=0, keepdims=True)

    row = pl.BlockSpec((ts, dm), lambda i: (i, 0))
    vec = pl.BlockSpec((1, dm), lambda i: (0, 0))
    return pl.pallas_call(
        body, name="prenorm_bwd", grid=(n,), in_specs=[row, row, row, vec, vec],
        out_specs=[row, pl.BlockSpec((8, dm), lambda i: (0, 0))],
        out_shape=[SDS((seq, dm), F32), SDS((8, dm), F32)],
        scratch_shapes=[pltpu.VMEM((24, dm), F32)], input_output_aliases={2: 0},
        compiler_params=_cp(("arbitrary",), 52),
    )(x, dh, dout, norm_g, scale)


def _unshard_cols(g):
    return jnp.transpose(g, (1, 0, 2)).reshape(g.shape[1], -1)


def _shard_cols(w):
    r = w.shape[0]
    return jnp.transpose(w.reshape(r, N_CHIPS, -1), (1, 0, 2))


def _w_in_to_my(w):
    c4 = 4 * D_CONV
    cq, ckv, kr, za = c4, c4 + Q_LORA, c4 + Q_LORA + KV_LORA, c4 + Q_LORA + KV_LORA + QK_ROPE
    pad = jnp.zeros((w.shape[0], KR_PAD - QK_ROPE), w.dtype)
    return jnp.concatenate([w[:, :c4], w[:, za:], w[:, cq:ckv], w[:, ckv:kr], w[:, kr:za], pad], axis=1)


def _w_in_from_my(g_conv, g_za, g_mla):
    return jnp.concatenate([g_conv, g_mla[:, :Q_LORA + KV_LORA + QK_ROPE], g_za], axis=1)


def _heads_pad(w):
    r = w.shape[0]
    w3 = w.reshape(r, N_HEADS, QK_HEAD)
    return jnp.pad(w3, ((0, 0), (0, 0), (0, HEAD_PAD - QK_HEAD))).reshape(r, Q_PAD)


def _heads_unpad(w):
    r = w.shape[0]
    return w.reshape(r, N_HEADS, HEAD_PAD)[:, :, :QK_HEAD].reshape(r, N_HEADS * QK_HEAD)


def kernel(x, c, positions, ada_w, ada_b, norm_g, w_in, conv_w, q_a_g, w_q_b, kv_a_g, w_kv_b, q_g, k_g, w_out, loss_target, m_ada_w, m_ada_b, m_norm_g, m_w_in, m_conv_w, m_q_a_g, m_w_q_b, m_kv_a_g, m_w_kv_b, m_q_g, m_k_g, m_w_out, v_ada_w, v_ada_b, v_norm_g, v_w_in, v_conv_w, v_q_a_g, v_w_q_b, v_kv_a_g, v_w_kv_b, v_q_g, v_k_g, v_w_out):
    mx, my, mc = _place()
    chip = 2 * mx + my
    me = 2 * chip + mc
    seq = x.shape[1]
    x2, t2 = x[0], loss_target[0]
    cw_cols = conv_w.shape[2]

    small = jnp.zeros((8, D_MODEL), F32)
    small = small.at[0].set(c[0])
    small = small.at[1:4, :cw_cols].set(conv_w[0])
    small_all = _gather8(small, "gather_c_conv", False)[0]
    c_all = small_all[:, 0, :]
    conv_full = jnp.transpose(small_all.reshape(N_CHIPS, 2, 8, D_MODEL)[:, 0, 1:4, :cw_cols], (1, 0, 2)).reshape(3, D_CONV)

    ada_cols = ada_w.shape[2]
    b_k = lax.dynamic_slice(ada_b, (0, chip * ada_cols), (1, ada_cols))
    mod_k, sc_all = _ada_mod(c_all, ada_w[0], b_k)
    mod_all = _gather8(mod_k, "gather_mod", False)[0]
    mod_row = lax.dynamic_slice(mod_all.reshape(N_CHIPS, 2, N_DEV, ada_cols), (0, mc, me, 0), (N_CHIPS, 1, 1, ada_cols))
    mod_row = mod_row.reshape(3, D_MODEL)
    shift, scale, gate = mod_row[0:1], mod_row[1:2], mod_row[2:3]

    def own_slot(g, s):
        return lax.dynamic_update_slice(g, s[None], (chip, 0, 0))

    shard_in = w_in[0].astype(BF16)
    g_in = own_slot(_allgather_shards([shard_in])[0], shard_in)
    w_my = _w_in_to_my(_unshard_cols(g_in))

    later = [w_q_b[0].astype(BF16), w_kv_b[0].astype(BF16), w_out[0].astype(BF16)]
    h, u, got = _inproj(x2, norm_g, scale, shift, w_my, later)
    g_q, g_kv, g_out = [own_slot(g, s) for g, s in zip(got, later)]
    wq = _heads_pad(_unshard_cols(g_q))
    wkv = _unshard_cols(g_kv).reshape(KV_LORA, N_HEADS, QK_NOPE + V_HEAD)
    wkn = wkv[:, :, :QK_NOPE].reshape(KV_LORA, N_HEADS * QK_NOPE)
    wv = wkv[:, :, QK_NOPE:].reshape(KV_LORA, D_ATTN)
    wo = g_out.reshape(N_CHIPS * g_out.shape[1], D_MODEL)
    y_conv = _conv_fwd(u, conv_full)
    pos = positions.reshape(seq, 1)
    inv_freq = ROPE_BASE ** (-jnp.arange(0, QK_ROPE, 2, dtype=F32) / QK_ROPE)
    freq = jnp.concatenate([inv_freq, inv_freq, jnp.zeros((LANES - QK_ROPE,), F32)]).reshape(1, LANES)
    q_g_pad = jnp.pad(q_g, ((0, 0), (0, HEAD_PAD - QK_HEAD)))
    k_g_pad = jnp.pad(k_g, ((0, 0), (0, HEAD_PAD - QK_HEAD)))
    q, k, v = _mla_prep(u, pos, freq, q_a_g, wq, kv_a_g, wkn, wv, q_g_pad, k_g_pad)
    o, y_attn, lse = _flash_fwd(q, k, v, u)
    dout, dy, dycat, st_out = _outproj_loss(y_conv, y_attn, x2, t2, gate, wo)

    dw_out = jnp.concatenate([_matmul_tn(y_conv, dy, "dw_out_conv"), _matmul_tn(y_attn, dy, "dw_out_attn")], axis=0)
    do_t, du_za, delta = _attn_gate_bwd(dycat, o, u)
    dq, dk, dv = _flash_bwd(q, k, v, do_t, lse, delta)
    du_mla, dwq, dwkn, dwv, sg_mla = _mla_bwd(dq, dk, dv, u, pos, freq, q_a_g, wq, kv_a_g, wkn, wv, q_g_pad, k_g_pad)
    du_conv, dconv_w = _conv_bwd(dycat, u, conv_full)
    dw_conv = _matmul_tn(h, du_conv, "dw_in_conv")
    dw_za = _matmul_tn(h, du_za, "dw_in_za")
    dw_mla = _matmul_tn(h, du_mla, "dw_in_mla")

    dw_in_nat = _w_in_from_my(dw_conv, dw_za, dw_mla)
    dw_q_nat = _heads_unpad(dwq).astype(BF16)
    dw_kv_nat = jnp.concatenate([dwkn.reshape(KV_LORA, N_HEADS, QK_NOPE), dwv.reshape(KV_LORA, N_HEADS, V_HEAD)],
                                axis=2).reshape(KV_LORA, N_HEADS * (QK_NOPE + V_HEAD)).astype(BF16)
    grads = [_shard_cols(dw_in_nat), _shard_cols(dw_q_nat), _shard_cols(dw_kv_nat),
             dw_out.reshape(N_CHIPS, dw_out.shape[0] // N_CHIPS, D_MODEL)]
    theirs = _rs_core_swap(grads)
    names = ["w_in", "w_q_b", "w_kv_b", "w_out"]
    core = jnp.reshape(mc, (1,)).astype(jnp.int32)
    parts = [_add_half_bf16(g, b, core, "rs_add_" + nm) for g, b, nm in zip(grads, theirs, names)]
    dh, recv = _inproj_bwd(du_conv, du_za, du_mla, w_my, parts)
    recv = [lax.dynamic_update_slice(r, lax.dynamic_slice(p, (chip, 0, 0), (1,) + p.shape[1:]), (chip, 0, 0))
            for r, p in zip(recv, parts)]
    halves = [_sum_chips(p, "rs_sum_" + nm) for p, nm in zip(recv, names)]
    joined = _rs_core_join(halves)
    joined = [lax.dynamic_update_slice(j, hf[None], (mc, 0, 0)) for j, hf in zip(joined, halves)]
    g_big = [j.reshape(2 * j.shape[1], j.shape[2]) for j in joined]
    grad_x, st_in = _prenorm_bwd(x2, dh, dout, norm_g, scale)

    sgrad = jnp.zeros((8, D_MODEL), F32)
    sgrad = sgrad.at[0:2].set(st_in[0:2])
    sgrad = sgrad.at[2].set(st_out[0])
    sgrad = sgrad.at[3].set(st_in[2])
    sgrad = sgrad.at[4, :D_CONV].set(dconv_w[0]).at[4, D_CONV:].set(dconv_w[1])
    sgrad = sgrad.at[5, :D_CONV].set(dconv_w[2]).at[5, D_CONV:].set(sg_mla[0, :D_CONV])
    sgrad = sgrad.at[6, :HEAD_PAD].set(sg_mla[0, SG_KG:SG_KG + HEAD_PAD])
    sgrad = sgrad.at[7].set(st_out[1])
    sg_all, sg_sum = _gather8(sgrad, "gather_small_grads", True)
    loss = sg_sum[7, 0]
    g_ada_b = sg_sum[0:3].reshape(1, 3 * D_MODEL)
    g_norm_g = sg_sum[3:4]
    conv_sum = jnp.stack([sg_sum[4, :D_CONV], sg_sum[4, D_CONV:], sg_sum[5, :D_CONV]])
    g_conv_w = lax.dynamic_slice(conv_sum, (0, chip * cw_cols), (3, cw_cols))
    g_q_a_g = sg_sum[5:6, D_CONV + SG_QAG:D_CONV + SG_QAG + Q_LORA]
    g_kv_a_g = sg_sum[5:6, D_CONV + SG_KVAG:D_CONV + SG_KVAG + KV_LORA]
    g_q_g = sg_sum[5:6, D_CONV + SG_QG:D_CONV + SG_QG + QK_HEAD]
    g_k_g = sg_sum[6:7, :QK_HEAD]
    dmod_k = lax.dynamic_slice(sg_all[:, 0:3, :].reshape(N_DEV, 3 * D_MODEL), (0, chip * ada_cols), (N_DEV, ada_cols))

    g_ada_w, d_ada_w, nm_ada_w, nv_ada_w = _ada_w_update(sc_all, dmod_k, ada_w[0], m_ada_w[0], v_ada_w[0])
    upd = {}
    big = {"w_in": (w_in, m_w_in, v_w_in), "w_q_b": (w_q_b, m_w_q_b, v_w_q_b), "w_kv_b": (w_kv_b, m_w_kv_b, v_w_kv_b),
           "w_out": (w_out, m_w_out, v_w_out)}
    for nm, g in zip(names, g_big):
        w_, m_, v_ = big[nm]
        upd[nm] = (g,) + tuple(_adamw(w_[0], g, m_[0], v_[0], "adamw_" + nm))
    small_w = {"ada_b": (ada_b, m_ada_b, v_ada_b, g_ada_b), "norm_g": (norm_g, m_norm_g, v_norm_g, g_norm_g),
               "conv_w": (conv_w[0], m_conv_w[0], v_conv_w[0], g_conv_w), "q_a_g": (q_a_g, m_q_a_g, v_q_a_g, g_q_a_g),
               "kv_a_g": (kv_a_g, m_kv_a_g, v_kv_a_g, g_kv_a_g), "q_g": (q_g, m_q_g, v_q_g, g_q_g),
               "k_g": (k_g, m_k_g, v_k_g, g_k_g)}
    for nm, (w_, m_, v_, g) in small_w.items():
        upd[nm] = (g,) + tuple(_adamw(w_, g, m_, v_, "adamw_" + nm))
    upd["ada_w"] = (g_ada_w, d_ada_w, nm_ada_w, nv_ada_w)

    order = ["ada_w", "ada_b", "norm_g", "w_in", "conv_w", "q_a_g", "w_q_b", "kv_a_g", "w_kv_b", "q_g", "k_g", "w_out"]
    lead1 = {"ada_w", "w_in", "conv_w", "w_q_b", "w_kv_b", "w_out"}

    def shaped(nm, a):
        return a[None] if nm in lead1 else a

    outs = [loss, grad_x[None]]
    for idx in range(4):
        outs += [shaped(nm, upd[nm][idx]) for nm in order]
    return tuple(outs)
```

```python
import functools
import math

import jax
import jax.numpy as jnp
from jax import lax
from jax.experimental import pallas as pl
from jax.experimental.pallas import tpu as pltpu

F32 = jnp.float32
BF16 = jnp.bfloat16
MESH = pl.DeviceIdType.MESH
SDS = jax.ShapeDtypeStruct
ANY = pl.BlockSpec(memory_space=pl.ANY)

D_MODEL = 2048
D_CONV = 1024
N_HEADS = 8
QK_NOPE = 128
QK_ROPE = 64
QK_HEAD = QK_NOPE + QK_ROPE
V_HEAD = 128
D_ATTN = N_HEADS * V_HEAD
Q_LORA = 512
KV_LORA = 256
ROPE_BASE = 10000.0
EPS = 1e-6
LOG2E = math.log2(math.e)
LN2 = math.log(2.0)
ADAM_LR, ADAM_B1, ADAM_B2, ADAM_EPS, ADAM_WD, ADAM_STEP = 0.001, 0.9, 0.999, 1e-08, 0.01, 10
N_CHIPS = 4
N_DEV = 8

LANES = 128
V7X_VMEM_BYTES = 64 * 1024 * 1024
MIB = 1024 * 1024

HEAD_PAD = 256
Q_PAD = N_HEADS * HEAD_PAD
U_ZA = 4 * D_CONV
U_CQ = U_ZA + D_ATTN
U_CKV = U_CQ + Q_LORA
U_KR = U_CKV + KV_LORA
KR_PAD = 256
U_COLS = U_KR + KR_PAD
MLA_COLS = Q_LORA + KV_LORA + KR_PAD

ATT_T = 512
INPROJ_TM, INPROJ_TN = 1024, 512
ROW_T = 512
OUT_T = 256
DH_TM, DH_TN = 512, 512
TN_TM, TN_TN, TN_TK = 1024, 1024, 2048
ATT_UNROLL = 4


def _cp(sem=None, vmem_mib=None, **kw):
    if sem is not None:
        kw["dimension_semantics"] = sem
    if vmem_mib is not None:
        kw["vmem_limit_bytes"] = min(vmem_mib * MIB, V7X_VMEM_BYTES - 4 * MIB)
    return pltpu.CompilerParams(**kw)


def _sigmoid(z):
    return 1.0 / (1.0 + jnp.exp(-z))


def _silu_grad(z, sg):
    return sg * (1.0 + z * (1.0 - sg))


def _nt(a, b):
    return lax.dot_general(a, b, (((1,), (1,)), ((), ())), preferred_element_type=F32)


def _tn(a, b):
    return lax.dot_general(a, b, (((0,), (0,)), ((), ())), preferred_element_type=F32)


def _nn(a, b):
    return jnp.dot(a, b, preferred_element_type=F32)


def _place():
    return lax.axis_index("x"), lax.axis_index("y"), lax.axis_index("c")


def _gather8(v, name, with_sum):
    rows, cols = v.shape

    def body(v_ref, out_ref, *rest):
        if with_sum:
            sum_ref, send_sems, recv_sems = rest
        else:
            send_sems, recv_sems = rest
        mx, my, mc = _place()
        me = 4 * mx + 2 * my + mc
        out_ref[me] = v_ref[...]
        peers = []
        for d in range(1, N_DEV):
            px = 1 - mx if d & 4 else mx
            py = 1 - my if d & 2 else my
            pc = 1 - mc if d & 1 else mc
            peers.append((px, py, pc))

        def copy(d, slot, to):
            return pltpu.make_async_remote_copy(
                src_ref=v_ref, dst_ref=out_ref.at[slot], send_sem=send_sems.at[d], recv_sem=recv_sems.at[d],
                device_id=to, device_id_type=MESH)

        sends = [copy(d, me, p) for d, p in enumerate(peers)]
        for cp in sends:
            cp.start()
        for d, (px, py, pc) in enumerate(peers):
            copy(d, 4 * px + 2 * py + pc, (px, py, pc)).wait_recv()
        for cp in sends:
            cp.wait_send()
        if with_sum:
            acc = out_ref[0]
            for b in range(1, N_DEV):
                acc = acc + out_ref[b]
            sum_ref[...] = acc

    out_shape = [SDS((N_DEV, rows, cols), F32)]
    if with_sum:
        out_shape.append(SDS((rows, cols), F32))
    vm = pl.BlockSpec(memory_space=pltpu.VMEM)
    return pl.pallas_call(
        body, name=name, out_shape=out_shape, in_specs=[vm], out_specs=[vm] * len(out_shape),
        scratch_shapes=[pltpu.SemaphoreType.DMA((N_DEV - 1,)), pltpu.SemaphoreType.DMA((N_DEV - 1,))],
    )(v)


def _chips_of(mx, my):
    chips = [(mx, 1 - my), (1 - mx, my), (1 - mx, 1 - my)]
    return chips, [2 * px + py for px, py in chips]


def _allgather_shards(shards):
    na = len(shards)
    halves = [s.shape[0] // 2 for s in shards]

    def body(*refs):
        ins, outs = refs[:na], refs[na:2 * na]
        s1, r1, s2, r2 = refs[2 * na:]
        mx, my, mc = _place()
        k = 2 * mx + my
        sib = (mx, my, 1 - mc)
        chips, kks = _chips_of(mx, my)

        def half(a, slot, c):
            return outs[a].at[slot, pl.ds(c * halves[a], halves[a])]

        def mine(a):
            return ins[a].at[pl.ds(mc * halves[a], halves[a])]

        sends = []
        for a in range(na):
            for d, (px, py) in enumerate(chips):
                cp = pltpu.make_async_remote_copy(
                    src_ref=mine(a), dst_ref=half(a, k, mc), send_sem=s1.at[3 * a + d], recv_sem=r1.at[3 * a + d],
                    device_id=(px, py, mc), device_id_type=MESH)
                cp.start()
                sends.append(cp)
        for a in range(na):
            for d, (px, py) in enumerate(chips):
                pltpu.make_async_remote_copy(
                    src_ref=mine(a), dst_ref=half(a, kks[d], mc), send_sem=s1.at[3 * a + d], recv_sem=r1.at[3 * a + d],
                    device_id=(px, py, mc), device_id_type=MESH).wait_recv()
                cp = pltpu.make_async_remote_copy(
                    src_ref=half(a, kks[d], mc), dst_ref=half(a, kks[d], mc), send_sem=s2.at[3 * a + d],
                    recv_sem=r2.at[3 * a + d], device_id=sib, device_id_type=MESH)
                cp.start()
                sends.append(cp)
        for a in range(na):
            for d in range(3):
                pltpu.make_async_remote_copy(
                    src_ref=half(a, kks[d], 1 - mc), dst_ref=half(a, kks[d], 1 - mc), send_sem=s2.at[3 * a + d],
                    recv_sem=r2.at[3 * a + d], device_id=sib, device_id_type=MESH).wait_recv()
        for cp in sends:
            cp.wait_send()

    return pl.pallas_call(
        body, name="allgather_weights",
        out_shape=[SDS((N_CHIPS,) + s.shape, s.dtype) for s in shards],
        in_specs=[ANY] * na, out_specs=[ANY] * na,
        scratch_shapes=[pltpu.SemaphoreType.DMA((3 * na,))] * 4,
    )(*shards)


def _rs_core_swap(grads):
    na = len(grads)
    halves = [g.shape[1] // 2 for g in grads]

    def body(*refs):
        ins, outs = refs[:na], refs[na:2 * na]
        ssem, rsem = refs[2 * na:]
        mx, my, mc = _place()
        sib = (mx, my, 1 - mc)
        sends = []
        for a in range(na):
            cp = pltpu.make_async_remote_copy(
                src_ref=ins[a].at[:, pl.ds((1 - mc) * halves[a], halves[a])], dst_ref=outs[a],
                send_sem=ssem.at[a], recv_sem=rsem.at[a], device_id=sib, device_id_type=MESH)
            cp.start()
            sends.append(cp)
        for cp in sends:
            cp.wait_recv()
        for cp in sends:
            cp.wait_send()

    return pl.pallas_call(
        body, name="rs_core_swap", out_shape=[SDS((N_CHIPS, h) + g.shape[2:], g.dtype) for g, h in zip(grads, halves)],
        in_specs=[ANY] * na, out_specs=[ANY] * na,
        scratch_shapes=[pltpu.SemaphoreType.DMA((na,))] * 2,
    )(*grads)


def _chip_exchange_copies(ins, outs, ssem, rsem):
    mx, my, mc = _place()
    k = 2 * mx + my
    chips, kks = _chips_of(mx, my)
    sends, recvs = [], []
    for a in range(len(ins)):
        for d, (px, py) in enumerate(chips):
            def copy(dst_slot):
                return pltpu.make_async_remote_copy(
                    src_ref=ins[a].at[kks[d]], dst_ref=outs[a].at[dst_slot], send_sem=ssem.at[3 * a + d],
                    recv_sem=rsem.at[3 * a + d], device_id=(px, py, mc), device_id_type=MESH)
            sends.append(copy(k))
            recvs.append(copy(kks[d]))
    return sends, recvs


def _rs_core_join(halves):
    na = len(halves)

    def body(*refs):
        ins, outs = refs[:na], refs[na:2 * na]
        ssem, rsem = refs[2 * na:]
        mx, my, mc = _place()
        sib = (mx, my, 1 - mc)
        sends = []
        for a in range(na):
            cp = pltpu.make_async_remote_copy(
                src_ref=ins[a], dst_ref=outs[a].at[mc], send_sem=ssem.at[a], recv_sem=rsem.at[a],
                device_id=sib, device_id_type=MESH)
            cp.start()
            sends.append(cp)
        for a in range(na):
            pltpu.make_async_remote_copy(
                src_ref=ins[a], dst_ref=outs[a].at[1 - mc], send_sem=ssem.at[a], recv_sem=rsem.at[a],
                device_id=sib, device_id_type=MESH).wait_recv()
        for cp in sends:
            cp.wait_send()

    return pl.pallas_call(
        body, name="rs_core_join", out_shape=[SDS((2,) + h.shape, h.dtype) for h in halves],
        in_specs=[ANY] * na, out_specs=[ANY] * na,
        scratch_shapes=[pltpu.SemaphoreType.DMA((na,))] * 2,
    )(*halves)


def _row_tile(rows, limit):
    if rows <= limit:
        return rows
    best = None
    for t in range(16, limit + 1, 16):
        if rows % t == 0:
            best = t
    assert best is not None, rows
    return best


def _add_half_bf16(g, b, core, name):
    _, h, cols = b.shape
    tb = _row_tile(h, 512)
    nb = h // tb

    def body(core_ref, g_ref, b_ref, o_ref):
        o_ref[...] = (g_ref[...].astype(F32) + b_ref[...].astype(F32)).astype(BF16)

    spec = pl.BlockSpec((None, tb, cols), lambda kk, i, core_ref: (kk, i, 0))
    return pl.pallas_call(
        body, name=name,
        grid_spec=pltpu.PrefetchScalarGridSpec(
            num_scalar_prefetch=1, grid=(N_CHIPS, nb),
            in_specs=[pl.BlockSpec((None, tb, cols), lambda kk, i, core_ref: (kk, core_ref[0] * nb + i, 0)), spec],
            out_specs=spec),
        out_shape=SDS(b.shape, BF16), compiler_params=_cp(("arbitrary", "arbitrary")),
    )(core, g, b)


def _sum_chips(p, name):
    _, rows, cols = p.shape
    tb = _row_tile(rows, 256)

    def body(p_ref, o_ref):
        acc = p_ref[0].astype(F32)
        for j in range(1, N_CHIPS):
            acc = acc + p_ref[j].astype(F32)
        o_ref[...] = acc

    return pl.pallas_call(
        body, name=name, grid=(rows // tb,),
        in_specs=[pl.BlockSpec((N_CHIPS, tb, cols), lambda i: (0, i, 0))],
        out_specs=pl.BlockSpec((tb, cols), lambda i: (i, 0)), out_shape=SDS((rows, cols), F32),
        compiler_params=_cp(("arbitrary",)),
    )(p)


def _adamw_math(w, g, m, v):
    m2 = ADAM_B1 * m + (1.0 - ADAM_B1) * g
    v2 = ADAM_B2 * v + (1.0 - ADAM_B2) * (g * g)
    m_hat = m2 / (1.0 - ADAM_B1 ** ADAM_STEP)
    v_hat = v2 / (1.0 - ADAM_B2 ** ADAM_STEP)
    delta = -ADAM_LR * (m_hat / (jnp.sqrt(v_hat) + ADAM_EPS) + ADAM_WD * w)
    return delta, m2, v2


def _adamw(w, g, m, v, name):
    rows, cols = w.shape
    tb = _row_tile(rows, 256)

    def body(w_ref, g_ref, m_ref, v_ref, d_ref, m2_ref, v2_ref):
        d, m2, v2 = _adamw_math(w_ref[...], g_ref[...], m_ref[...], v_ref[...])
        d_ref[...] = d
        m2_ref[...] = m2
        v2_ref[...] = v2

    spec = pl.BlockSpec((tb, cols), lambda i: (i, 0))
    return pl.pallas_call(
        body, name=name, grid=(rows // tb,), in_specs=[spec] * 4, out_specs=[spec] * 3,
        out_shape=[SDS((rows, cols), F32)] * 3, compiler_params=_cp(("arbitrary",), 40),
    )(w, g, m, v)


def _ada_w_update(sc_all, dmod_k, w, m, v):
    rows, cols = w.shape
    tb = 256

    def body(s_ref, dm_ref, w_ref, m_ref, v_ref, g_ref, d_ref, m2_ref, v2_ref):
        g = _tn(s_ref[...].astype(BF16), dm_ref[...].astype(BF16))
        d, m2, v2 = _adamw_math(w_ref[...], g, m_ref[...], v_ref[...])
        g_ref[...] = g
        d_ref[...] = d
        m2_ref[...] = m2
        v2_ref[...] = v2

    spec = pl.BlockSpec((tb, cols), lambda i: (i, 0))
    return pl.pallas_call(
        body, name="ada_w_update", grid=(rows // tb,),
        in_specs=[pl.BlockSpec((N_DEV, tb), lambda i: (0, i)), pl.BlockSpec((N_DEV, cols), lambda i: (0, 0)), spec, spec, spec],
        out_specs=[spec] * 4, out_shape=[SDS((rows, cols), F32)] * 4, compiler_params=_cp(("arbitrary",), 40),
    )(sc_all, dmod_k, w, m, v)


def _ada_mod(c_all, w, b_k):
    rows, cols = w.shape
    tn = 512

    def body(c_ref, w_ref, b_ref, o_ref, s_ref):
        cv = c_ref[...]
        s = cv * _sigmoid(cv)
        s_ref[...] = s
        o_ref[...] = _nn(s.astype(BF16), w_ref[...].astype(BF16)) + b_ref[...]

    return pl.pallas_call(
        body, name="ada_mod", grid=(cols // tn,),
        in_specs=[pl.BlockSpec((N_DEV, rows), lambda j: (0, 0)), pl.BlockSpec((rows, tn), lambda j: (0, j)),
                  pl.BlockSpec((1, tn), lambda j: (0, j))],
        out_specs=[pl.BlockSpec((N_DEV, tn), lambda j: (0, j)), pl.BlockSpec((N_DEV, rows), lambda j: (0, 0))],
        out_shape=[SDS((N_DEV, cols), F32), SDS((N_DEV, rows), F32)], compiler_params=_cp(("arbitrary",)),
    )(c_all, w, b_k)


def _inproj(x, norm_g, scale, shift, w_my, shards):
    seq, dm = x.shape
    ncols = w_my.shape[1]
    tm, tn = min(INPROJ_TM, seq), INPROJ_TN
    ni, nj = seq // tm, ncols // tn
    na = len(shards)

    def body(x_ref, g_ref, sc_ref, sh_ref, w_ref, *rest):
        shard_refs, h_ref, u_ref, got_refs = rest[:na], rest[na], rest[na + 1], rest[na + 2:2 * na + 2]
        ssem, rsem = rest[2 * na + 2:]
        i, j = pl.program_id(0), pl.program_id(1)
        mx, my, mc = _place()
        k = 2 * mx + my
        chips, kks = _chips_of(mx, my)

        def copy(a, d, slot):
            return pltpu.make_async_remote_copy(
                src_ref=shard_refs[a], dst_ref=got_refs[a].at[slot], send_sem=ssem.at[3 * a + d],
                recv_sem=rsem.at[3 * a + d], device_id=(chips[d][0], chips[d][1], mc), device_id_type=MESH)

        @pl.when((i == 0) & (j == 0))
        def _():
            for a in range(na):
                for d in range(3):
                    copy(a, d, k).start()

        @pl.when(j == 0)
        def _():
            xv = x_ref[...]
            r = lax.rsqrt(jnp.mean(xv * xv, axis=-1, keepdims=True) + EPS)
            hv = (xv * r * g_ref[...]) * (1.0 + sc_ref[...]) + sh_ref[...]
            h_ref[...] = hv.astype(BF16)

        u_ref[...] = _nn(h_ref[...], w_ref[...]).astype(BF16)

        @pl.when((i == ni - 1) & (j == nj - 1))
        def _():
            for a in range(na):
                for d in range(3):
                    copy(a, d, kks[d]).wait_recv()
            for a in range(na):
                for d in range(3):
                    copy(a, d, k).wait_send()

    vec = pl.BlockSpec((1, dm), lambda i, j: (0, 0))
    outs = pl.pallas_call(
        body, name="inproj", grid=(ni, nj),
        in_specs=[pl.BlockSpec((tm, dm), lambda i, j: (i, 0)), vec, vec, vec, pl.BlockSpec((dm, tn), lambda i, j: (0, j))]
                 + [ANY] * na,
        out_specs=[pl.BlockSpec((tm, dm), lambda i, j: (i, 0)), pl.BlockSpec((tm, tn), lambda i, j: (i, j))] + [ANY] * na,
        out_shape=[SDS((seq, dm), BF16), SDS((seq, ncols), BF16)] + [SDS((N_CHIPS,) + s.shape, s.dtype) for s in shards],
        scratch_shapes=[pltpu.SemaphoreType.DMA((3 * na,))] * 2,
        compiler_params=_cp(("arbitrary", "arbitrary"), 48),
    )(x, norm_g, scale, shift, w_my, *shards)
    return outs[0], outs[1], outs[2:]


HALO = 16


def _conv_taps(uc, halo, ext_ref, ts, causal):
    if causal:
        ext_ref[0:HALO, :] = halo
        ext_ref[HALO:HALO + ts, :] = uc
        return ext_ref[pl.ds(HALO - 1, ts), :], ext_ref[pl.ds(HALO - 2, ts), :]
    ext_ref[0:ts, :] = uc
    ext_ref[ts:ts + HALO, :] = halo
    return ext_ref[pl.ds(1, ts), :], ext_ref[pl.ds(2, ts), :]


def _conv_fwd(u, conv_w):
    seq = u.shape[0]
    ts = min(ROW_T, seq)
    hb = ts // HALO

    def body(xc_ref, bc_ref, cc_ref, zc_ref, xp_ref, cp_ref, w_ref, y_ref, ext_ref):
        i = pl.program_id(0)
        uc = cc_ref[...].astype(F32) * xc_ref[...].astype(F32)
        up = cp_ref[...].astype(F32) * xp_ref[...].astype(F32)
        up = jnp.where(i > 0, up, 0.0)
        u1, u2 = _conv_taps(uc, up, ext_ref, ts, True)
        conv = w_ref[0:1, :] * u2 + w_ref[1:2, :] * u1 + w_ref[2:3, :] * uc
        z = zc_ref[...].astype(F32)
        y_ref[...] = ((bc_ref[...].astype(F32) * conv) * (z * _sigmoid(z))).astype(BF16)

    def col(cb):
        return pl.BlockSpec((ts, D_CONV), lambda i: (i, cb))

    def prev(cb):
        return pl.BlockSpec((HALO, D_CONV), lambda i: (jnp.maximum(i * hb - 1, 0), cb))

    return pl.pallas_call(
        body, name="conv_fwd", grid=(seq // ts,),
        in_specs=[col(0), col(1), col(2), col(3), prev(0), prev(2), pl.BlockSpec((3, D_CONV), lambda i: (0, 0))],
        out_specs=pl.BlockSpec((ts, D_CONV), lambda i: (i, 0)), out_shape=SDS((seq, D_CONV), BF16),
        scratch_shapes=[pltpu.VMEM((ts + HALO, D_CONV), F32)],
        compiler_params=_cp(("arbitrary",), 40),
    )(u, u, u, u, u, u, conv_w)


def _rope_tables(pos_ref, freq_ref):
    ang = pos_ref[...].astype(F32) * freq_ref[...]
    lane = lax.broadcasted_iota(jnp.int32, ang.shape, 1)
    cs, sn = jnp.cos(ang), jnp.sin(ang)
    half = QK_ROPE // 2
    cos_t = jnp.where(lane < QK_ROPE, cs, 0.0)
    sin_lo = jnp.where(lane < half, sn, 0.0)
    sin_hi = jnp.where((lane >= half) & (lane < QK_ROPE), sn, 0.0)
    return cos_t, sin_lo, sin_hi


def _rope(blk, tables):
    cos_t, sin_lo, sin_hi = tables
    half = QK_ROPE // 2
    return blk * cos_t - pltpu.roll(blk, LANES - half, 1) * sin_lo + pltpu.roll(blk, half, 1) * sin_hi


def _rope_bwd(g, tables):
    cos_t, sin_lo, sin_hi = tables
    half = QK_ROPE // 2
    return g * cos_t + pltpu.roll(g, LANES - half, 1) * sin_lo - pltpu.roll(g, half, 1) * sin_hi


def _rms(v, n):
    r = lax.rsqrt(jnp.sum(v * v, axis=-1, keepdims=True) * (1.0 / n) + EPS)
    return v * r, r


def _mla_prep(u, pos, freq, q_a_g, wq, kv_a_g, wkn, wv, q_g, k_g):
    seq = u.shape[0]
    ts = min(ROW_T, seq)
    qscale = LOG2E / math.sqrt(QK_HEAD)

    def body(cq_ref, ckv_ref, kr_ref, pos_ref, freq_ref, qag_ref, wq_ref, kvag_ref, wkn_ref, wv_ref, qg_ref, kg_ref,
             q_ref, k_ref, v_ref):
        tables = _rope_tables(pos_ref, freq_ref)
        cqn, _ = _rms(cq_ref[...].astype(F32), Q_LORA)
        qp = _nn((cqn * qag_ref[...]).astype(BF16), wq_ref[...])
        qg = qg_ref[...]
        for h in range(N_HEADS):
            lo = h * HEAD_PAD
            qn, _ = _rms(qp[:, lo:lo + HEAD_PAD], QK_HEAD)
            qn = qn * qg
            q_ref[:, lo:lo + LANES] = (qn[:, :LANES] * qscale).astype(BF16)
            q_ref[:, lo + LANES:lo + HEAD_PAD] = (_rope(qn[:, LANES:], tables) * qscale).astype(BF16)
        ckvn, _ = _rms(ckv_ref[...].astype(F32), KV_LORA)
        ckvb = (ckvn * kvag_ref[...]).astype(BF16)
        kn = _nn(ckvb, wkn_ref[...])
        v_ref[...] = _nn(ckvb, wv_ref[...]).astype(BF16)
        kr = kr_ref[:, 0:LANES].astype(F32)
        ssr = jnp.sum(kr * kr, axis=-1, keepdims=True)
        kg = kg_ref[...]
        for h in range(N_HEADS):
            knh = kn[:, h * QK_NOPE:(h + 1) * QK_NOPE]
            r = lax.rsqrt((jnp.sum(knh * knh, axis=-1, keepdims=True) + ssr) * (1.0 / QK_HEAD) + EPS)
            lo = h * HEAD_PAD
            k_ref[:, lo:lo + LANES] = (knh * r * kg[:, :LANES]).astype(BF16)
            k_ref[:, lo + LANES:lo + HEAD_PAD] = _rope(kr * r * kg[:, LANES:], tables).astype(BF16)

    def full(a):
        return pl.BlockSpec(a.shape, lambda i: (0,) * a.ndim)

    return pl.pallas_call(
        body, name="mla_prep", grid=(seq // ts,),
        in_specs=[pl.BlockSpec((ts, Q_LORA), lambda i: (i, U_CQ // Q_LORA)),
                  pl.BlockSpec((ts, KV_LORA), lambda i: (i, U_CKV // KV_LORA)),
                  pl.BlockSpec((ts, KR_PAD), lambda i: (i, U_KR // KR_PAD)),
                  pl.BlockSpec((ts, 1), lambda i: (i, 0)), full(freq), full(q_a_g), full(wq), full(kv_a_g), full(wkn),
                  full(wv), full(q_g), full(k_g)],
        out_specs=[pl.BlockSpec((ts, Q_PAD), lambda i: (i, 0)), pl.BlockSpec((ts, Q_PAD), lambda i: (i, 0)),
                   pl.BlockSpec((ts, D_ATTN), lambda i: (i, 0))],
        out_shape=[SDS((seq, Q_PAD), BF16), SDS((seq, Q_PAD), BF16), SDS((seq, D_ATTN), BF16)],
        compiler_params=_cp(("arbitrary",), 48),
    )(u, u, u, pos, freq, q_a_g, wq, kv_a_g, wkn, wv, q_g, k_g)


def _causal_mask(t, tq, q0):
    return lax.broadcasted_iota(jnp.int32, (t, tq), 0) <= lax.broadcasted_iota(jnp.int32, (t, tq), 1) + q0


def _flash_fwd(q, k, v, u):
    seq = q.shape[0]
    t = min(ATT_T, seq)
    za_blk = U_ZA // V_HEAD

    def body(q_ref, k_ref, v_ref, z_ref, o_ref, y_ref, lse_ref, s_a, s_b, top_a, top_b, m_ref, l_ref, acc_ref):
        i = pl.program_id(1)
        ones = jnp.ones((16, t), BF16)
        bufs = ((s_a, top_a), (s_b, top_b))

        def keys(j):
            return pl.ds(pl.multiple_of(j * t, t), t)

        def scores(j, buf):
            s_ref, top_ref = buf
            s = _nt(k_ref[keys(j), :], q_ref[...])
            s_ref[...] = s
            top_ref[...] = jnp.max(s, axis=0, keepdims=True)

        def absorb(j, buf, masked):
            s_ref, top_ref = buf
            s = s_ref[...]
            if masked:
                s = jnp.where(_causal_mask(t, t, 0), s, -jnp.inf)
                top = jnp.max(s, axis=0, keepdims=True)
            else:
                top = top_ref[...]
            m = m_ref[...]
            m_new = jnp.maximum(m, top)
            alpha = jnp.exp2(m - m_new)
            p = jnp.exp2((s - m_new).astype(BF16))
            m_ref[...] = m_new
            l_ref[...] = alpha * l_ref[...] + _nn(ones, p)[0:1, :]
            acc_ref[...] = alpha * acc_ref[...] + _tn(v_ref[keys(j), :], p)

        scores(0, bufs[0])
        m_ref[...] = jnp.full_like(m_ref, -jnp.inf)
        l_ref[...] = jnp.zeros_like(l_ref)
        acc_ref[...] = jnp.zeros_like(acc_ref)

        def trip(width):
            def body(jj, carry):
                for w in range(width):
                    j = width * jj + w
                    scores(j + 1, bufs[(w + 1) % 2])
                    absorb(j, bufs[w % 2], False)
                return carry
            return body

        quads = i // ATT_UNROLL
        lax.fori_loop(0, quads, trip(ATT_UNROLL), 0)
        lax.fori_loop(quads * (ATT_UNROLL // 2), i // 2, trip(2), 0)

        @pl.when(i % 2 == 1)
        def _():
            scores(i, bufs[1])
            absorb(i - 1, bufs[0], False)
            absorb(i, bufs[1], True)

        @pl.when(i % 2 == 0)
        def _():
            absorb(i, bufs[0], True)

        l = l_ref[...]
        o = (acc_ref[...] * (1.0 / l)).T
        lse_ref[...] = m_ref[...] + jnp.log2(l)
        o_ref[...] = o.astype(BF16)
        z = z_ref[...].astype(F32)
        y_ref[...] = (o * (z * _sigmoid(z))).astype(BF16)

    tile = pl.BlockSpec((t, V_HEAD), lambda h, i: (i, h))
    return pl.pallas_call(
        body, name="flash_fwd", grid=(N_HEADS, seq // t),
        in_specs=[pl.BlockSpec((t, HEAD_PAD), lambda h, i: (i, h)), pl.BlockSpec((seq, HEAD_PAD), lambda h, i: (0, h)),
                  pl.BlockSpec((seq, V_HEAD), lambda h, i: (0, h)), pl.BlockSpec((t, V_HEAD), lambda h, i: (i, za_blk + h))],
        out_specs=[tile, tile, pl.BlockSpec((None, 1, t), lambda h, i: (h, 0, i))],
        out_shape=[SDS((seq, D_ATTN), BF16), SDS((seq, D_ATTN), BF16), SDS((N_HEADS, 1, seq), F32)],
        scratch_shapes=[pltpu.VMEM((t, t), F32)] * 2 + [pltpu.VMEM((1, t), F32)] * 4 + [pltpu.VMEM((V_HEAD, t), F32)],
        compiler_params=_cp(("arbitrary", "arbitrary"), 40),
    )(q, k, v, u)


def _outproj_loss(y_conv, y_attn, x, target, gate, w_out):
    seq, dm = x.shape
    ts = min(OUT_T, seq)
    n = seq // ts
    dmix = w_out.shape[0]

    def body(yc_ref, ya_ref, x_ref, t_ref, gate_ref, wo_hbm, dout_ref, dy_ref, dyc_ref, stats_ref, wo_ref, sem, acc_ref):
        i = pl.program_id(0)

        @pl.when(i == 0)
        def _():
            cp = pltpu.make_async_copy(wo_hbm, wo_ref, sem)
            cp.start()
            cp.wait()
            acc_ref[...] = jnp.zeros_like(acc_ref)

        y = _nn(yc_ref[...], wo_ref[0:D_CONV, :]) + _nn(ya_ref[...], wo_ref[D_CONV:dmix, :])
        gate_v = gate_ref[...]
        diff = (x_ref[...] + gate_v * y) - t_ref[...]
        dout = diff * (1.0 / dm)
        dout_ref[...] = dout
        acc_ref[0:8, :] += jnp.sum((dout * y).reshape(ts // 8, 8, dm), axis=0)
        acc_ref[8:16, :] += jnp.sum((diff * diff).reshape(ts // 8, 8, dm), axis=0)
        dy = (dout * gate_v).astype(BF16)
        dy_ref[...] = dy
        dyc_ref[...] = _nt(dy, wo_ref[...]).astype(BF16)

        @pl.when(i == n - 1)
        def _():
            stats_ref[...] = jnp.zeros_like(stats_ref)
            stats_ref[0:1, :] = jnp.sum(acc_ref[0:8, :], axis=0, keepdims=True)
            loss = jnp.sum(acc_ref[8:16, :]) * (0.5 / dm)
            stats_ref[1:2, :] = jnp.full((1, dm), loss, F32)

    row = pl.BlockSpec((ts, dm), lambda i: (i, 0))
    half = pl.BlockSpec((ts, D_CONV), lambda i: (i, 0))
    return pl.pallas_call(
        body, name="outproj_loss", grid=(n,),
        in_specs=[half, half, row, row, pl.BlockSpec((1, dm), lambda i: (0, 0)), ANY],
        out_specs=[row, row, pl.BlockSpec((ts, dmix), lambda i: (i, 0)), pl.BlockSpec((8, dm), lambda i: (0, 0))],
        out_shape=[SDS((seq, dm), F32), SDS((seq, dm), BF16), SDS((seq, dmix), BF16), SDS((8, dm), F32)],
        scratch_shapes=[pltpu.VMEM(w_out.shape, BF16), pltpu.SemaphoreType.DMA(()), pltpu.VMEM((16, dm), F32)],
        compiler_params=_cp(("arbitrary",), 52),
    )(y_conv, y_attn, x, target, gate, w_out)


def _matmul_tn(a, b, name):
    seq, m = a.shape
    n = b.shape[1]
    tm, tn, tk = min(TN_TM, m), min(TN_TN, n), min(TN_TK, seq)
    nk = seq // tk

    def body(a_ref, b_ref, o_ref, acc_ref):
        kk = pl.program_id(2)

        @pl.when(kk == 0)
        def _():
            acc_ref[...] = jnp.zeros_like(acc_ref)

        acc_ref[...] += _tn(a_ref[...], b_ref[...])

        @pl.when(kk == nk - 1)
        def _():
            o_ref[...] = acc_ref[...].astype(BF16)

    return pl.pallas_call(
        body, name=name, grid=(m // tm, n // tn, nk),
        in_specs=[pl.BlockSpec((tk, tm), lambda i, j, kk: (kk, i)), pl.BlockSpec((tk, tn), lambda i, j, kk: (kk, j))],
        out_specs=pl.BlockSpec((tm, tn), lambda i, j, kk: (i, j)), out_shape=SDS((m, n), BF16),
        scratch_shapes=[pltpu.VMEM((tm, tn), F32)],
        compiler_params=_cp(("arbitrary", "arbitrary", "arbitrary"), 40),
    )(a, b)


def _attn_gate_bwd(dycat, o, u):
    seq = o.shape[0]
    ts = min(ROW_T, seq)

    def body(dy_ref, o_ref, z_ref, dot_ref, dz_ref, dl_ref):
        dy = dy_ref[...].astype(F32)
        ov = o_ref[...].astype(F32)
        z = z_ref[...].astype(F32)
        sg = _sigmoid(z)
        do = dy * (z * sg)
        dz_ref[...] = (dy * ov * _silu_grad(z, sg)).astype(BF16)
        prod = do * ov
        ones = jnp.ones((8, V_HEAD), F32)
        for h in range(N_HEADS):
            cols = slice(h * V_HEAD, (h + 1) * V_HEAD)
            dot_ref[h] = do[:, cols].T.astype(BF16)
            rows = lax.dot_general(ones, prod[:, cols], (((1,), (1,)), ((), ())), precision=lax.Precision.HIGHEST,
                                   preferred_element_type=F32)
            dl_ref[h] = rows[0:1, :]

    blk = pl.BlockSpec((ts, D_ATTN), lambda i: (i, 0))
    return pl.pallas_call(
        body, name="attn_gate_bwd", grid=(seq // ts,),
        in_specs=[pl.BlockSpec((ts, D_ATTN), lambda i: (i, 1)), blk, pl.BlockSpec((ts, D_ATTN), lambda i: (i, U_ZA // D_ATTN))],
        out_specs=[pl.BlockSpec((N_HEADS, V_HEAD, ts), lambda i: (0, 0, i)), blk,
                   pl.BlockSpec((N_HEADS, 1, ts), lambda i: (0, 0, i))],
        out_shape=[SDS((N_HEADS, V_HEAD, seq), BF16), SDS((seq, D_ATTN), BF16), SDS((N_HEADS, 1, seq), F32)],
        compiler_params=_cp(("arbitrary",), 40),
    )(dycat, o, u)


def _flash_bwd(q, k, v, do_t, lse, delta):
    seq = q.shape[0]
    t = min(ATT_T, seq)
    n = seq // t

    def body(k_ref, v_ref, q_ref, dot_ref, lse_ref, dl_ref, dq_ref, dk_ref, dv_ref, s_a, s_b, dp_a, dp_b, dvt_ref):
        j = pl.program_id(1)

        @pl.when(j == 0)
        def _():
            dq_ref[...] = jnp.zeros_like(dq_ref)

        def rows(r):
            return pl.ds(pl.multiple_of((j + r) * t, t), t)

        def products(r, s_ref, dp_ref):
            s_ref[...] = _nt(k_ref[...], q_ref[rows(r), :])
            dp_ref[...] = _nn(v_ref[...], dot_ref[:, rows(r)])

        def absorb(r, s_ref, dp_ref, masked):
            p = jnp.exp2((s_ref[...] - lse_ref[:, rows(r)]).astype(BF16))
            if masked:
                p = jnp.where(_causal_mask(t, t, 0), p, jnp.zeros_like(p))
            dvt_ref[...] += _nt(dot_ref[:, rows(r)], p)
            ds = p * (dp_ref[...] - dl_ref[:, rows(r)]).astype(BF16)
            dk_ref[...] += _nn(ds, q_ref[rows(r), :])
            dq_ref[rows(r), :] += _tn(ds, k_ref[...])

        dk_ref[...] = jnp.zeros_like(dk_ref)
        dvt_ref[...] = jnp.zeros_like(dvt_ref)
        products(0, s_a, dp_a)
        last = n - 1 - j

        @pl.when(last == 0)
        def _():
            absorb(0, s_a, dp_a, True)

        @pl.when(last > 0)
        def _():
            products(1, s_b, dp_b)
            absorb(0, s_a, dp_a, True)

        inner = jnp.maximum(last - 1, 0)

        bufs = ((s_b, dp_b), (s_a, dp_a))

        def trip(width):
            def body(pp, carry):
                for w in range(width):
                    r = 1 + width * pp + w
                    products(r + 1, *bufs[(w + 1) % 2])
                    absorb(r, *bufs[w % 2], False)
                return carry
            return body

        quads = inner // ATT_UNROLL
        lax.fori_loop(0, quads, trip(ATT_UNROLL), 0)
        lax.fori_loop(quads * (ATT_UNROLL // 2), inner // 2, trip(2), 0)

        @pl.when((last > 0) & (inner % 2 == 1))
        def _():
            products(last, s_a, dp_a)
            absorb(last - 1, s_b, dp_b, False)
            absorb(last, s_a, dp_a, False)

        @pl.when((last > 0) & (inner % 2 == 0))
        def _():
            absorb(last, s_b, dp_b, False)

        dv_ref[...] = dvt_ref[...].T.astype(BF16)

    row = pl.BlockSpec((None, 1, seq), lambda h, j: (h, 0, 0))
    return pl.pallas_call(
        body, name="flash_bwd", grid=(N_HEADS, n),
        in_specs=[pl.BlockSpec((t, HEAD_PAD), lambda h, j: (j, h)), pl.BlockSpec((t, V_HEAD), lambda h, j: (j, h)),
                  pl.BlockSpec((seq, HEAD_PAD), lambda h, j: (0, h)), pl.BlockSpec((None, V_HEAD, seq), lambda h, j: (h, 0, 0)),
                  row, row],
        out_specs=[pl.BlockSpec((seq, HEAD_PAD), lambda h, j: (0, h)), pl.BlockSpec((t, HEAD_PAD), lambda h, j: (j, h)),
                   pl.BlockSpec((t, V_HEAD), lambda h, j: (j, h))],
        out_shape=[SDS((seq, Q_PAD), F32), SDS((seq, Q_PAD), F32), SDS((seq, D_ATTN), BF16)],
        scratch_shapes=[pltpu.VMEM((t, t), F32)] * 4 + [pltpu.VMEM((V_HEAD, t), F32)],
        compiler_params=_cp(("arbitrary", "arbitrary"), 56),
    )(k, v, q, do_t, lse, delta)


SG_QAG, SG_KVAG, SG_QG, SG_KG, SG_COLS = 0, Q_LORA, Q_LORA + KV_LORA, Q_LORA + KV_LORA + HEAD_PAD, D_MODEL


def _mla_bwd(dq, dk, dv, u, pos, freq, q_a_g, wq, kv_a_g, wkn, wv, q_g, k_g):
    seq = u.shape[0]
    ts = min(ROW_T, seq)
    n = seq // ts
    qscale = 1.0 / math.sqrt(QK_HEAD)

    def body(dq_ref, dk_ref, dv_ref, cq_ref, ckv_ref, kr_ref, pos_ref, freq_ref, qag_ref, wq_ref, kvag_ref, wkn_ref,
             wv_ref, qg_ref, kg_ref, du_ref, dwq_ref, dwkn_ref, dwv_ref, sg_ref, dqp_ref, dkn_ref):
        i = pl.program_id(0)

        @pl.when(i == 0)
        def _():
            dwq_ref[...] = jnp.zeros_like(dwq_ref)
            dwkn_ref[...] = jnp.zeros_like(dwkn_ref)
            dwv_ref[...] = jnp.zeros_like(dwv_ref)
            sg_ref[...] = jnp.zeros_like(sg_ref)

        tables = _rope_tables(pos_ref, freq_ref)

        cq = cq_ref[...].astype(F32)
        cqn, rq = _rms(cq, Q_LORA)
        qag = qag_ref[...]
        cqb = (cqn * qag).astype(BF16)
        qp = _nn(cqb, wq_ref[...])
        qg = qg_ref[...]
        dqg = jnp.zeros((1, HEAD_PAD), F32)
        for h in range(N_HEADS):
            lo = h * HEAD_PAD
            xn, r = _rms(qp[:, lo:lo + HEAD_PAD], QK_HEAD)
            g = jnp.concatenate([dq_ref[:, lo:lo + LANES], _rope_bwd(dq_ref[:, lo + LANES:lo + HEAD_PAD], tables)],
                                axis=-1) * qscale
            dqg = dqg + jnp.sum(g * xn, axis=0, keepdims=True)
            gy = g * qg
            mean = jnp.sum(gy * xn, axis=-1, keepdims=True) * (1.0 / QK_HEAD)
            dqp_ref[:, lo:lo + HEAD_PAD] = (r * (gy - xn * mean)).astype(BF16)
        dqp = dqp_ref[...]
        dwq_ref[...] += _tn(cqb, dqp)
        dcqn = _nt(dqp, wq_ref[...])
        sg_ref[0:1, SG_QAG:SG_QAG + Q_LORA] += jnp.sum(dcqn * cqn, axis=0, keepdims=True)
        sg_ref[0:1, SG_QG:SG_QG + HEAD_PAD] += dqg
        gy = dcqn * qag
        mean = jnp.sum(gy * cqn, axis=-1, keepdims=True) * (1.0 / Q_LORA)
        du_ref[:, 0:Q_LORA] = (rq * (gy - cqn * mean)).astype(BF16)

        ckv = ckv_ref[...].astype(F32)
        ckvn, rkv = _rms(ckv, KV_LORA)
        kvag = kvag_ref[...]
        ckvb = (ckvn * kvag).astype(BF16)
        kn = _nn(ckvb, wkn_ref[...])
        kr = kr_ref[:, 0:LANES].astype(F32)
        ssr = jnp.sum(kr * kr, axis=-1, keepdims=True)
        kg = kg_ref[...]
        kg_n, kg_r = kg[:, :LANES] * LN2, kg[:, LANES:] * LN2
        dkg_n = jnp.zeros((1, LANES), F32)
        dkg_r = jnp.zeros((1, LANES), F32)
        dkr = jnp.zeros((ts, LANES), F32)
        for h in range(N_HEADS):
            knh = kn[:, h * QK_NOPE:(h + 1) * QK_NOPE]
            r = lax.rsqrt((jnp.sum(knh * knh, axis=-1, keepdims=True) + ssr) * (1.0 / QK_HEAD) + EPS)
            xn_n, xn_r = knh * r, kr * r
            lo = h * HEAD_PAD
            g_n = dk_ref[:, lo:lo + LANES]
            g_r = _rope_bwd(dk_ref[:, lo + LANES:lo + HEAD_PAD], tables)
            dkg_n = dkg_n + jnp.sum(g_n * xn_n, axis=0, keepdims=True)
            dkg_r = dkg_r + jnp.sum(g_r * xn_r, axis=0, keepdims=True)
            gy_n, gy_r = g_n * kg_n, g_r * kg_r
            mean = (jnp.sum(gy_n * xn_n, axis=-1, keepdims=True) + jnp.sum(gy_r * xn_r, axis=-1, keepdims=True)) * (1.0 / QK_HEAD)
            dkn_ref[:, h * QK_NOPE:(h + 1) * QK_NOPE] = (r * (gy_n - xn_n * mean)).astype(BF16)
            dkr = dkr + r * (gy_r - xn_r * mean)
        dkn = dkn_ref[...]
        dvv = dv_ref[...]
        dwkn_ref[...] += _tn(ckvb, dkn)
        dwv_ref[...] += _tn(ckvb, dvv)
        dckvn = _nt(dkn, wkn_ref[...]) + _nt(dvv, wv_ref[...])
        sg_ref[0:1, SG_KVAG:SG_KVAG + KV_LORA] += jnp.sum(dckvn * ckvn, axis=0, keepdims=True)
        sg_ref[0:1, SG_KG:SG_KG + LANES] += dkg_n * LN2
        sg_ref[0:1, SG_KG + LANES:SG_KG + HEAD_PAD] += dkg_r * LN2
        gy = dckvn * kvag
        mean = jnp.sum(gy * ckvn, axis=-1, keepdims=True) * (1.0 / KV_LORA)
        du_ref[:, Q_LORA:Q_LORA + KV_LORA] = (rkv * (gy - ckvn * mean)).astype(BF16)
        du_ref[:, Q_LORA + KV_LORA:Q_LORA + KV_LORA + LANES] = dkr.astype(BF16)
        du_ref[:, Q_LORA + KV_LORA + LANES:MLA_COLS] = jnp.zeros((ts, LANES), BF16)

    def full(a):
        return pl.BlockSpec(a.shape, lambda i: (0,) * a.ndim)

    wide = pl.BlockSpec((ts, Q_PAD), lambda i: (i, 0))
    return pl.pallas_call(
        body, name="mla_bwd", grid=(n,),
        in_specs=[wide, wide, pl.BlockSpec((ts, D_ATTN), lambda i: (i, 0)),
                  pl.BlockSpec((ts, Q_LORA), lambda i: (i, U_CQ // Q_LORA)),
                  pl.BlockSpec((ts, KV_LORA), lambda i: (i, U_CKV // KV_LORA)),
                  pl.BlockSpec((ts, KR_PAD), lambda i: (i, U_KR // KR_PAD)),
                  pl.BlockSpec((ts, 1), lambda i: (i, 0)), full(freq), full(q_a_g), full(wq), full(kv_a_g), full(wkn),
                  full(wv), full(q_g), full(k_g)],
        out_specs=[pl.BlockSpec((ts, MLA_COLS), lambda i: (i, 0)), pl.BlockSpec((Q_LORA, Q_PAD), lambda i: (0, 0)),
                   pl.BlockSpec((KV_LORA, D_ATTN), lambda i: (0, 0)), pl.BlockSpec((KV_LORA, D_ATTN), lambda i: (0, 0)),
                   pl.BlockSpec((8, SG_COLS), lambda i: (0, 0))],
        out_shape=[SDS((seq, MLA_COLS), BF16), SDS((Q_LORA, Q_PAD), F32), SDS((KV_LORA, D_ATTN), F32),
                   SDS((KV_LORA, D_ATTN), F32), SDS((8, SG_COLS), F32)],
        scratch_shapes=[pltpu.VMEM((ts, Q_PAD), BF16), pltpu.VMEM((ts, D_ATTN), BF16)],
        compiler_params=_cp(("arbitrary",), 56),
    )(dq, dk, dv, u, u, u, pos, freq, q_a_g, wq, kv_a_g, wkn, wv, q_g, k_g)


def _conv_bwd(dycat, u, conv_w):
    seq = u.shape[0]
    ts = min(ROW_T, seq)
    n = seq // ts
    hb = ts // HALO

    def body(dy_ref, xc_ref, bc_ref, cc_ref, zc_ref, xp_ref, cp_ref, dyn_ref, bn_ref, zn_ref, w_ref,
             du_ref, dw_ref, ext_ref):
        i = pl.program_id(0)

        @pl.when(i == 0)
        def _():
            dw_ref[...] = jnp.zeros_like(dw_ref)

        xc = xc_ref[...].astype(F32)
        cc = cc_ref[...].astype(F32)
        uc = cc * xc
        up = jnp.where(i > 0, cp_ref[...].astype(F32) * xp_ref[...].astype(F32), 0.0)
        u1, u2 = _conv_taps(uc, up, ext_ref, ts, True)
        w0, w1, w2 = w_ref[0:1, :], w_ref[1:2, :], w_ref[2:3, :]
        conv = w0 * u2 + w1 * u1 + w2 * uc
        z = zc_ref[...].astype(F32)
        sg = _sigmoid(z)
        sz = z * sg
        b = bc_ref[...].astype(F32)
        dy = dy_ref[...].astype(F32)
        du_ref[:, 3 * D_CONV:4 * D_CONV] = (dy * (b * conv) * _silu_grad(z, sg)).astype(BF16)
        du_ref[:, D_CONV:2 * D_CONV] = (dy * sz * conv).astype(BF16)
        dconv = dy * sz * b
        dw_ref[0:1, :] += jnp.sum(dconv * u2, axis=0, keepdims=True)
        dw_ref[1:2, :] += jnp.sum(dconv * u1, axis=0, keepdims=True)
        dw_ref[2:3, :] += jnp.sum(dconv * uc, axis=0, keepdims=True)
        zn = zn_ref[...].astype(F32)
        dnext = dyn_ref[...].astype(F32) * (zn * _sigmoid(zn)) * bn_ref[...].astype(F32)
        dnext = jnp.where(i < n - 1, dnext, 0.0)
        d1, d2 = _conv_taps(dconv, dnext, ext_ref, ts, False)
        du = w2 * dconv + w1 * d1 + w0 * d2
        du_ref[:, 2 * D_CONV:3 * D_CONV] = (du * xc).astype(BF16)
        du_ref[:, 0:D_CONV] = (du * cc).astype(BF16)

    def col(cb):
        return pl.BlockSpec((ts, D_CONV), lambda i: (i, cb))

    def prev(cb):
        return pl.BlockSpec((HALO, D_CONV), lambda i: (jnp.maximum(i * hb - 1, 0), cb))

    def nxt(cb):
        return pl.BlockSpec((HALO, D_CONV), lambda i: (jnp.minimum((i + 1) * hb, n * hb - 1), cb))

    return pl.pallas_call(
        body, name="conv_bwd", grid=(n,),
        in_specs=[col(0), col(0), col(1), col(2), col(3), prev(0), prev(2), nxt(0), nxt(1), nxt(3),
                  pl.BlockSpec((3, D_CONV), lambda i: (0, 0))],
        out_specs=[pl.BlockSpec((ts, 4 * D_CONV), lambda i: (i, 0)), pl.BlockSpec((8, D_CONV), lambda i: (0, 0))],
        out_shape=[SDS((seq, 4 * D_CONV), BF16), SDS((8, D_CONV), F32)],
        scratch_shapes=[pltpu.VMEM((ts + HALO, D_CONV), F32)],
        compiler_params=_cp(("arbitrary",), 48),
    )(dycat, u, u, u, u, u, u, dycat, u, u, conv_w)


def _inproj_bwd(du_conv, du_za, du_mla, w_my, parts):
    seq = du_conv.shape[0]
    dm = w_my.shape[0]
    tm, tn = min(DH_TM, seq), DH_TN
    ni, nj = seq // tm, dm // tn
    na = len(parts)

    def body(dc_ref, dz_ref, dm_ref, w_ref, *rest):
        part_refs, o_ref, recv_refs = rest[:na], rest[na], rest[na + 1:2 * na + 1]
        ssem, rsem = rest[2 * na + 1:]
        i, j = pl.program_id(0), pl.program_id(1)
        sends, recvs = _chip_exchange_copies(part_refs, recv_refs, ssem, rsem)

        @pl.when((i == 0) & (j == 0))
        def _():
            for cp in sends:
                cp.start()

        acc = _nt(dc_ref[...], w_ref[:, 0:U_ZA])
        acc = acc + _nt(dz_ref[...], w_ref[:, U_ZA:U_CQ])
        acc = acc + _nt(dm_ref[...], w_ref[:, U_CQ:U_COLS])
        o_ref[...] = acc

        @pl.when((i == ni - 1) & (j == nj - 1))
        def _():
            for cp in recvs:
                cp.wait_recv()
            for cp in sends:
                cp.wait_send()

    outs = pl.pallas_call(
        body, name="inproj_bwd", grid=(ni, nj),
        in_specs=[pl.BlockSpec((tm, U_ZA), lambda i, j: (i, 0)), pl.BlockSpec((tm, D_ATTN), lambda i, j: (i, 0)),
                  pl.BlockSpec((tm, MLA_COLS), lambda i, j: (i, 0)), pl.BlockSpec((tn, U_COLS), lambda i, j: (j, 0))]
                 + [ANY] * na,
        out_specs=[pl.BlockSpec((tm, tn), lambda i, j: (i, j))] + [ANY] * na,
        out_shape=[SDS((seq, dm), F32)] + [SDS(p.shape, p.dtype) for p in parts],
        scratch_shapes=[pltpu.SemaphoreType.DMA((3 * na,))] * 2,
        compiler_params=_cp(("arbitrary", "arbitrary"), 48),
    )(du_conv, du_za, du_mla, w_my, *parts)
    return outs[0], outs[1:]


def _prenorm_bwd(x, dh, dout, norm_g, scale):
    seq, dm = x.shape
    ts = min(ROW_T, seq)
    n = seq // ts

    def body(x_ref, dh_ref, dout_ref, g_ref, sc_ref, gx_ref, st_ref, acc_ref):
        i = pl.program_id(0)

        @pl.when(i == 0)
        def _():
            acc_ref[...] = jnp.zeros_like(acc_ref)

        xv = x_ref[...]
        xn, r = _rms(xv, dm)
        dh_v = dh_ref[...]
        gv = g_ref[...]
        one_sc = 1.0 + sc_ref[...]

        def fold(a):
            return jnp.sum(a.reshape(ts // 8, 8, dm), axis=0)

        acc_ref[0:8, :] += fold(dh_v)
        acc_ref[8:16, :] += fold(dh_v * (xn * gv))
        dxg = dh_v * one_sc
        acc_ref[16:24, :] += fold(dxg * xn)
        dxn = dxg * gv
        mean = jnp.sum(dxn * xn, axis=-1, keepdims=True) * (1.0 / dm)
        gx_ref[...] = dout_ref[...] + r * (dxn - xn * mean)

        @pl.when(i == n - 1)
        def _():
            st_ref[...] = jnp.zeros_like(st_ref)
            for k in range(3):
                st_ref[k:k + 1, :] = jnp.sum(acc_ref[8 * k:8 * k + 8, :], axis=0, keepdims=True)

    row = pl.BlockSpec((ts, dm), lambda i: (i, 0))
    vec = pl.BlockSpec((1, dm), lambda i: (0, 0))
    return pl.pallas_call(
        body, name="prenorm_bwd", grid=(n,), in_specs=[row, row, row, vec, vec],
        out_specs=[row, pl.BlockSpec((8, dm), lambda i: (0, 0))],
        out_shape=[SDS((seq, dm), F32), SDS((8, dm), F32)],
        scratch_shapes=[pltpu.VMEM((24, dm), F32)], input_output_aliases={2: 0},
        compiler_params=_cp(("arbitrary",), 52),
    )(x, dh, dout, norm_g, scale)


def _unshard_cols(g):
    return jnp.transpose(g, (1, 0, 2)).reshape(g.shape[1], -1)


def _shard_cols(w):
    r = w.shape[0]
    return jnp.transpose(w.reshape(r, N_CHIPS, -1), (1, 0, 2))


def _w_in_to_my(w):
    c4 = 4 * D_CONV
    cq, ckv, kr, za = c4, c4 + Q_LORA, c4 + Q_LORA + KV_LORA, c4 + Q_LORA + KV_LORA + QK_ROPE
    pad = jnp.zeros((w.shape[0], KR_PAD - QK_ROPE), w.dtype)
    return jnp.concatenate([w[:, :c4], w[:, za:], w[:, cq:ckv], w[:, ckv:kr], w[:, kr:za], pad], axis=1)


def _w_in_from_my(g_conv, g_za, g_mla):
    return jnp.concatenate([g_conv, g_mla[:, :Q_LORA + KV_LORA + QK_ROPE], g_za], axis=1)


def _heads_pad(w):
    r = w.shape[0]
    w3 = w.reshape(r, N_HEADS, QK_HEAD)
    return jnp.pad(w3, ((0, 0), (0, 0), (0, HEAD_PAD - QK_HEAD))).reshape(r, Q_PAD)


def _heads_unpad(w):
    r = w.shape[0]
    return w.reshape(r, N_HEADS, HEAD_PAD)[:, :, :QK_HEAD].reshape(r, N_HEADS * QK_HEAD)


def kernel(x, c, positions, ada_w, ada_b, norm_g, w_in, conv_w, q_a_g, w_q_b, kv_a_g, w_kv_b, q_g, k_g, w_out, loss_target, m_ada_w, m_ada_b, m_norm_g, m_w_in, m_conv_w, m_q_a_g, m_w_q_b, m_kv_a_g, m_w_kv_b, m_q_g, m_k_g, m_w_out, v_ada_w, v_ada_b, v_norm_g, v_w_in, v_conv_w, v_q_a_g, v_w_q_b, v_kv_a_g, v_w_kv_b, v_q_g, v_k_g, v_w_out):
    mx, my, mc = _place()
    chip = 2 * mx + my
    me = 2 * chip + mc
    seq = x.shape[1]
    x2, t2 = x[0], loss_target[0]
    cw_cols = conv_w.shape[2]

    small = jnp.zeros((8, D_MODEL), F32)
    small = small.at[0].set(c[0])
    small = small.at[1:4, :cw_cols].set(conv_w[0])
    small_all = _gather8(small, "gather_c_conv", False)[0]
    c_all = small_all[:, 0, :]
    conv_full = jnp.transpose(small_all.reshape(N_CHIPS, 2, 8, D_MODEL)[:, 0, 1:4, :cw_cols], (1, 0, 2)).reshape(3, D_CONV)

    ada_cols = ada_w.shape[2]
    b_k = lax.dynamic_slice(ada_b, (0, chip * ada_cols), (1, ada_cols))
    mod_k, sc_all = _ada_mod(c_all, ada_w[0], b_k)
    mod_all = _gather8(mod_k, "gather_mod", False)[0]
    mod_row = lax.dynamic_slice(mod_all.reshape(N_CHIPS, 2, N_DEV, ada_cols), (0, mc, me, 0), (N_CHIPS, 1, 1, ada_cols))
    mod_row = mod_row.reshape(3, D_MODEL)
    shift, scale, gate = mod_row[0:1], mod_row[1:2], mod_row[2:3]

    def own_slot(g, s):
        return lax.dynamic_update_slice(g, s[None], (chip, 0, 0))

    shard_in = w_in[0].astype(BF16)
    g_in = own_slot(_allgather_shards([shard_in])[0], shard_in)
    w_my = _w_in_to_my(_unshard_cols(g_in))

    later = [w_q_b[0].astype(BF16), w_kv_b[0].astype(BF16), w_out[0].astype(BF16)]
    h, u, got = _inproj(x2, norm_g, scale, shift, w_my, later)
    g_q, g_kv, g_out = [own_slot(g, s) for g, s in zip(got, later)]
    wq = _heads_pad(_unshard_cols(g_q))
    wkv = _unshard_cols(g_kv).reshape(KV_LORA, N_HEADS, QK_NOPE + V_HEAD)
    wkn = wkv[:, :, :QK_NOPE].reshape(KV_LORA, N_HEADS * QK_NOPE)
    wv = wkv[:, :, QK_NOPE:].reshape(KV_LORA, D_ATTN)
    wo = g_out.reshape(N_CHIPS * g_out.shape[1], D_MODEL)
    y_conv = _conv_fwd(u, conv_full)
    pos = positions.reshape(seq, 1)
    inv_freq = ROPE_BASE ** (-jnp.arange(0, QK_ROPE, 2, dtype=F32) / QK_ROPE)
    freq = jnp.concatenate([inv_freq, inv_freq, jnp.zeros((LANES - QK_ROPE,), F32)]).reshape(1, LANES)
    q_g_pad = jnp.pad(q_g, ((0, 0), (0, HEAD_PAD - QK_HEAD)))
    k_g_pad = jnp.pad(k_g, ((0, 0), (0, HEAD_PAD - QK_HEAD)))
    q, k, v = _mla_prep(u, pos, freq, q_a_g, wq, kv_a_g, wkn, wv, q_g_pad, k_g_pad)
    o, y_attn, lse = _flash_fwd(q, k, v, u)
    dout, dy, dycat, st_out = _outproj_loss(y_conv, y_attn, x2, t2, gate, wo)

    dw_out = jnp.concatenate([_matmul_tn(y_conv, dy, "dw_out_conv"), _matmul_tn(y_attn, dy, "dw_out_attn")], axis=0)
    do_t, du_za, delta = _attn_gate_bwd(dycat, o, u)
    dq, dk, dv = _flash_bwd(q, k, v, do_t, lse, delta)
    du_mla, dwq, dwkn, dwv, sg_mla = _mla_bwd(dq, dk, dv, u, pos, freq, q_a_g, wq, kv_a_g, wkn, wv, q_g_pad, k_g_pad)
    du_conv, dconv_w = _conv_bwd(dycat, u, conv_full)
    dw_conv = _matmul_tn(h, du_conv, "dw_in_conv")
    dw_za = _matmul_tn(h, du_za, "dw_in_za")
    dw_mla = _matmul_tn(h, du_mla, "dw_in_mla")

    dw_in_nat = _w_in_from_my(dw_conv, dw_za, dw_mla)
    dw_q_nat = _heads_unpad(dwq).astype(BF16)
    dw_kv_nat = jnp.concatenate([dwkn.reshape(KV_LORA, N_HEADS, QK_NOPE), dwv.reshape(KV_LORA, N_HEADS, V_HEAD)],
                                axis=2).reshape(KV_LORA, N_HEADS * (QK_NOPE + V_HEAD)).astype(BF16)
    grads = [_shard_cols(dw_in_nat), _shard_cols(dw_q_nat), _shard_cols(dw_kv_nat),
             dw_out.reshape(N_CHIPS, dw_out.shape[0] // N_CHIPS, D_MODEL)]
    theirs = _rs_core_swap(grads)
    names = ["w_in", "w_q_b", "w_kv_b", "w_out"]
    core = jnp.reshape(mc, (1,)).astype(jnp.int32)
    parts = [_add_half_bf16(g, b, core, "rs_add_" + nm) for g, b, nm in zip(grads, theirs, names)]
    dh, recv = _inproj_bwd(du_conv, du_za, du_mla, w_my, parts)
    recv = [lax.dynamic_update_slice(r, lax.dynamic_slice(p, (chip, 0, 0), (1,) + p.shape[1:]), (chip, 0, 0))
            for r, p in zip(recv, parts)]
    halves = [_sum_chips(p, "rs_sum_" + nm) for p, nm in zip(recv, names)]
    joined = _rs_core_join(halves)
    joined = [lax.dynamic_update_slice(j, hf[None], (mc, 0, 0)) for j, hf in zip(joined, halves)]
    g_big = [j.reshape(2 * j.shape[1], j.shape[2]) for j in joined]
    grad_x, st_in = _prenorm_bwd(x2, dh, dout, norm_g, scale)

    sgrad = jnp.zeros((8, D_MODEL), F32)
    sgrad = sgrad.at[0:2].set(st_in[0:2])
    sgrad = sgrad.at[2].set(st_out[0])
    sgrad = sgrad.at[3].set(st_in[2])
    sgrad = sgrad.at[4, :D_CONV].set(dconv_w[0]).at[4, D_CONV:].set(dconv_w[1])
    sgrad = sgrad.at[5, :D_CONV].set(dconv_w[2]).at[5, D_CONV:].set(sg_mla[0, :D_CONV])
    sgrad = sgrad.at[6, :HEAD_PAD].set(sg_mla[0, SG_KG:SG_KG + HEAD_PAD])
    sgrad = sgrad.at[7].set(st_out[1])
    sg_all, sg_sum = _gather8(sgrad, "gather_small_grads", True)
    loss = sg_sum[7, 0]
    g_ada_b = sg_sum[0:3].reshape(1, 3 * D_MODEL)
    g_norm_g = sg_sum[3:4]
    conv_sum = jnp.stack([sg_sum[4, :D_CONV], sg_sum[4, D_CONV:], sg_sum[5, :D_CONV]])
    g_conv_w = lax.dynamic_slice(conv_sum, (0, chip * cw_cols), (3, cw_cols))
    g_q_a_g = sg_sum[5:6, D_CONV + SG_QAG:D_CONV + SG_QAG + Q_LORA]
    g_kv_a_g = sg_sum[5:6, D_CONV + SG_KVAG:D_CONV + SG_KVAG + KV_LORA]
    g_q_g = sg_sum[5:6, D_CONV + SG_QG:D_CONV + SG_QG + QK_HEAD]
    g_k_g = sg_sum[6:7, :QK_HEAD]
    dmod_k = lax.dynamic_slice(sg_all[:, 0:3, :].reshape(N_DEV, 3 * D_MODEL), (0, chip * ada_cols), (N_DEV, ada_cols))

    g_ada_w, d_ada_w, nm_ada_w, nv_ada_w = _ada_w_update(sc_all, dmod_k, ada_w[0], m_ada_w[0], v_ada_w[0])
    upd = {}
    big = {"w_in": (w_in, m_w_in, v_w_in), "w_q_b": (w_q_b, m_w_q_b, v_w_q_b), "w_kv_b": (w_kv_b, m_w_kv_b, v_w_kv_b),
           "w_out": (w_out, m_w_out, v_w_out)}
    for nm, g in zip(names, g_big):
        w_, m_, v_ = big[nm]
        upd[nm] = (g,) + tuple(_adamw(w_[0], g, m_[0], v_[0], "adamw_" + nm))
    small_w = {"ada_b": (ada_b, m_ada_b, v_ada_b, g_ada_b), "norm_g": (norm_g, m_norm_g, v_norm_g, g_norm_g),
               "conv_w": (conv_w[0], m_conv_w[0], v_conv_w[0], g_conv_w), "q_a_g": (q_a_g, m_q_a_g, v_q_a_g, g_q_a_g),
               "kv_a_g": (kv_a_g, m_kv_a_g, v_kv_a_g, g_kv_a_g), "q_g": (q_g, m_q_g, v_q_g, g_q_g),
               "k_g": (k_g, m_k_g, v_k_g, g_k_g)}
    for nm, (w_, m_, v_, g) in small_w.items():
        upd[nm] = (g,) + tuple(_adamw(w_, g, m_, v_, "adamw_" + nm))
    upd["ada_w"] = (g_ada_w, d_ada_w, nm_ada_w, nv_ada_w)

    order = ["ada_w", "ada_b", "norm_g", "w_in", "conv_w", "q_a_g", "w_q_b", "kv_a_g", "w_kv_b", "q_g", "k_g", "w_out"]
    lead1 = {"ada_w", "w_in", "conv_w", "w_q_b", "w_kv_b", "w_out"}

    def shaped(nm, a):
        return a[None] if nm in lead1 else a

    outs = [loss, grad_x[None]]
    for idx in range(4):
        outs += [shaped(nm, upd[nm][idx]) for nm in order]
    return tuple(outs)
```

```python
import functools
import math

import jax
import jax.numpy as jnp
from jax import lax
from jax.experimental import pallas as pl
from jax.experimental.pallas import tpu as pltpu

F32 = jnp.float32
BF16 = jnp.bfloat16
MESH = pl.DeviceIdType.MESH
SDS = jax.ShapeDtypeStruct
ANY = pl.BlockSpec(memory_space=pl.ANY)

D_MODEL = 2048
D_CONV = 1024
N_HEADS = 8
QK_NOPE = 128
QK_ROPE = 64
QK_HEAD = QK_NOPE + QK_ROPE
V_HEAD = 128
D_ATTN = N_HEADS * V_HEAD
Q_LORA = 512
KV_LORA = 256
ROPE_BASE = 10000.0
EPS = 1e-6
LOG2E = math.log2(math.e)
LN2 = math.log(2.0)
ADAM_LR, ADAM_B1, ADAM_B2, ADAM_EPS, ADAM_WD, ADAM_STEP = 0.001, 0.9, 0.999, 1e-08, 0.01, 10
N_CHIPS = 4
N_DEV = 8

LANES = 128
V7X_VMEM_BYTES = 64 * 1024 * 1024
MIB = 1024 * 1024

HEAD_PAD = 256
Q_PAD = N_HEADS * HEAD_PAD
U_ZA = 4 * D_CONV
U_CQ = U_ZA + D_ATTN
U_CKV = U_CQ + Q_LORA
U_KR = U_CKV + KV_LORA
KR_PAD = 256
U_COLS = U_KR + KR_PAD
MLA_COLS = Q_LORA + KV_LORA + KR_PAD

ATT_T = 512
INPROJ_TM, INPROJ_TN = 1024, 512
ROW_T = 512
OUT_T = 256
DH_TM, DH_TN = 512, 512
TN_TM, TN_TN, TN_TK = 1024, 1024, 2048
ATT_UNROLL = 4


def _cp(sem=None, vmem_mib=None, **kw):
    if sem is not None:
        kw["dimension_semantics"] = sem
    if vmem_mib is not None:
        kw["vmem_limit_bytes"] = min(vmem_mib * MIB, V7X_VMEM_BYTES - 4 * MIB)
    return pltpu.CompilerParams(**kw)


def _sigmoid(z):
    return 1.0 / (1.0 + jnp.exp(-z))


def _silu_grad(z, sg):
    return sg * (1.0 + z * (1.0 - sg))


def _nt(a, b):
    return lax.dot_general(a, b, (((1,), (1,)), ((), ())), preferred_element_type=F32)


def _tn(a, b):
    return lax.dot_general(a, b, (((0,), (0,)), ((), ())), preferred_element_type=F32)


def _nn(a, b):
    return jnp.dot(a, b, preferred_element_type=F32)


def _place():
    return lax.axis_index("x"), lax.axis_index("y"), lax.axis_index("c")


def _gather8(v, name, with_sum):
    rows, cols = v.shape

    def body(v_ref, out_ref, *rest):
        if with_sum:
            sum_ref, send_sems, recv_sems = rest
        else:
            send_sems, recv_sems = rest
        mx, my, mc = _place()
        me = 4 * mx + 2 * my + mc
        out_ref[me] = v_ref[...]
        peers = []
        for d in range(1, N_DEV):
            px = 1 - mx if d & 4 else mx
            py = 1 - my if d & 2 else my
            pc = 1 - mc if d & 1 else mc
            peers.append((px, py, pc))

        def copy(d, slot, to):
            return pltpu.make_async_remote_copy(
                src_ref=v_ref, dst_ref=out_ref.at[slot], send_sem=send_sems.at[d], recv_sem=recv_sems.at[d],
                device_id=to, device_id_type=MESH)

        sends = [copy(d, me, p) for d, p in enumerate(peers)]
        for cp in sends:
            cp.start()
        for d, (px, py, pc) in enumerate(peers):
            copy(d, 4 * px + 2 * py + pc, (px, py, pc)).wait_recv()
        for cp in sends:
            cp.wait_send()
        if with_sum:
            acc = out_ref[0]
            for b in range(1, N_DEV):
                acc = acc + out_ref[b]
            sum_ref[...] = acc

    out_shape = [SDS((N_DEV, rows, cols), F32)]
    if with_sum:
        out_shape.append(SDS((rows, cols), F32))
    vm = pl.BlockSpec(memory_space=pltpu.VMEM)
    return pl.pallas_call(
        body, name=name, out_shape=out_shape, in_specs=[vm], out_specs=[vm] * len(out_shape),
        scratch_shapes=[pltpu.SemaphoreType.DMA((N_DEV - 1,)), pltpu.SemaphoreType.DMA((N_DEV - 1,))],
    )(v)


def _chips_of(mx, my):
    chips = [(mx, 1 - my), (1 - mx, my), (1 - mx, 1 - my)]
    return chips, [2 * px + py for px, py in chips]


def _allgather_shards(shards):
    na = len(shards)
    halves = [s.shape[0] // 2 for s in shards]

    def body(*refs):
        ins, outs = refs[:na], refs[na:2 * na]
        s1, r1, s2, r2 = refs[2 * na:]
        mx, my, mc = _place()
        k = 2 * mx + my
        sib = (mx, my, 1 - mc)
        chips, kks = _chips_of(mx, my)

        def half(a, slot, c):
            return outs[a].at[slot, pl.ds(c * halves[a], halves[a])]

        def mine(a):
            return ins[a].at[pl.ds(mc * halves[a], halves[a])]

        sends = []
        for a in range(na):
            for d, (px, py) in enumerate(chips):
                cp = pltpu.make_async_remote_copy(
                    src_ref=mine(a), dst_ref=half(a, k, mc), send_sem=s1.at[3 * a + d], recv_sem=r1.at[3 * a + d],
                    device_id=(px, py, mc), device_id_type=MESH)
                cp.start()
                sends.append(cp)
        for a in range(na):
            for d, (px, py) in enumerate(chips):
                pltpu.make_async_remote_copy(
                    src_ref=mine(a), dst_ref=half(a, kks[d], mc), send_sem=s1.at[3 * a + d], recv_sem=r1.at[3 * a + d],
                    device_id=(px, py, mc), device_id_type=MESH).wait_recv()
                cp = pltpu.make_async_remote_copy(
                    src_ref=half(a, kks[d], mc), dst_ref=half(a, kks[d], mc), send_sem=s2.at[3 * a + d],
                    recv_sem=r2.at[3 * a + d], device_id=sib, device_id_type=MESH)
                cp.start()
                sends.append(cp)
        for a in range(na):
            for d in range(3):
                pltpu.make_async_remote_copy(
                    src_ref=half(a, kks[d], 1 - mc), dst_ref=half(a, kks[d], 1 - mc), send_sem=s2.at[3 * a + d],
                    recv_sem=r2.at[3 * a + d], device_id=sib, device_id_type=MESH).wait_recv()
        for cp in sends:
            cp.wait_send()

    return pl.pallas_call(
        body, name="allgather_weights",
        out_shape=[SDS((N_CHIPS,) + s.shape, s.dtype) for s in shards],
        in_specs=[ANY] * na, out_specs=[ANY] * na,
        scratch_shapes=[pltpu.SemaphoreType.DMA((3 * na,))] * 4,
    )(*shards)


def _rs_core_swap(grads):
    na = len(grads)
    halves = [g.shape[1] // 2 for g in grads]

    def body(*refs):
        ins, outs = refs[:na], refs[na:2 * na]
        ssem, rsem = refs[2 * na:]
        mx, my, mc = _place()
        sib = (mx, my, 1 - mc)
        sends = []
        for a in range(na):
            cp = pltpu.make_async_remote_copy(
                src_ref=ins[a].at[:, pl.ds((1 - mc) * halves[a], halves[a])], dst_ref=outs[a],
                send_sem=ssem.at[a], recv_sem=rsem.at[a], device_id=sib, device_id_type=MESH)
            cp.start()
            sends.append(cp)
        for cp in sends:
            cp.wait_recv()
        for cp in sends:
            cp.wait_send()

    return pl.pallas_call(
        body, name="rs_core_swap", out_shape=[SDS((N_CHIPS, h) + g.shape[2:], g.dtype) for g, h in zip(grads, halves)],
        in_specs=[ANY] * na, out_specs=[ANY] * na,
        scratch_shapes=[pltpu.SemaphoreType.DMA((na,))] * 2,
    )(*grads)


def _chip_exchange_copies(ins, outs, ssem, rsem):
    mx, my, mc = _place()
    k = 2 * mx + my
    chips, kks = _chips_of(mx, my)
    sends, recvs = [], []
    for a in range(len(ins)):
        for d, (px, py) in enumerate(chips):
            def copy(dst_slot):
                return pltpu.make_async_remote_copy(
                    src_ref=ins[a].at[kks[d]], dst_ref=outs[a].at[dst_slot], send_sem=ssem.at[3 * a + d],
                    recv_sem=rsem.at[3 * a + d], device_id=(px, py, mc), device_id_type=MESH)
            sends.append(copy(k))
            recvs.append(copy(kks[d]))
    return sends, recvs


def _rs_core_join(halves):
    na = len(halves)

    def body(*refs):
        ins, outs = refs[:na], refs[na:2 * na]
        ssem, rsem = refs[2 * na:]
        mx, my, mc = _place()
        sib = (mx, my, 1 - mc)
        sends = []
        for a in range(na):
            cp = pltpu.make_async_remote_copy(
                src_ref=ins[a], dst_ref=outs[a].at[mc], send_sem=ssem.at[a], recv_sem=rsem.at[a],
                device_id=sib, device_id_type=MESH)
            cp.start()
            sends.append(cp)
        for a in range(na):
            pltpu.make_async_remote_copy(
                src_ref=ins[a], dst_ref=outs[a].at[1 - mc], send_sem=ssem.at[a], recv_sem=rsem.at[a],
                device_id=sib, device_id_type=MESH).wait_recv()
        for cp in sends:
            cp.wait_send()

    return pl.pallas_call(
        body, name="rs_core_join", out_shape=[SDS((2,) + h.shape, h.dtype) for h in halves],
        in_specs=[ANY] * na, out_specs=[ANY] * na,
        scratch_shapes=[pltpu.SemaphoreType.DMA((na,))] * 2,
    )(*halves)


def _row_tile(rows, limit):
    if rows <= limit:
        return rows
    best = None
    for t in range(16, limit + 1, 16):
        if rows % t == 0:
            best = t
    assert best is not None, rows
    return best


def _add_half_bf16(g, b, core, name):
    _, h, cols = b.shape
    tb = _row_tile(h, 512)
    nb = h // tb

    def body(core_ref, g_ref, b_ref, o_ref):
        o_ref[...] = (g_ref[...].astype(F32) + b_ref[...].astype(F32)).astype(BF16)

    spec = pl.BlockSpec((None, tb, cols), lambda kk, i, core_ref: (kk, i, 0))
    return pl.pallas_call(
        body, name=name,
        grid_spec=pltpu.PrefetchScalarGridSpec(
            num_scalar_prefetch=1, grid=(N_CHIPS, nb),
            in_specs=[pl.BlockSpec((None, tb, cols), lambda kk, i, core_ref: (kk, core_ref[0] * nb + i, 0)), spec],
            out_specs=spec),
        out_shape=SDS(b.shape, BF16), compiler_params=_cp(("arbitrary", "arbitrary")),
    )(core, g, b)


def _sum_chips(p, name):
    _, rows, cols = p.shape
    tb = _row_tile(rows, 256)

    def body(p_ref, o_ref):
        acc = p_ref[0].astype(F32)
        for j in range(1, N_CHIPS):
            acc = acc + p_ref[j].astype(F32)
        o_ref[...] = acc

    return pl.pallas_call(
        body, name=name, grid=(rows // tb,),
        in_specs=[pl.BlockSpec((N_CHIPS, tb, cols), lambda i: (0, i, 0))],
        out_specs=pl.BlockSpec((tb, cols), lambda i: (i, 0)), out_shape=SDS((rows, cols), F32),
        compiler_params=_cp(("arbitrary",)),
    )(p)


def _adamw_math(w, g, m, v):
    m2 = ADAM_B1 * m + (1.0 - ADAM_B1) * g
    v2 = ADAM_B2 * v + (1.0 - ADAM_B2) * (g * g)
    m_hat = m2 / (1.0 - ADAM_B1 ** ADAM_STEP)
    v_hat = v2 / (1.0 - ADAM_B2 ** ADAM_STEP)
    delta = -ADAM_LR * (m_hat / (jnp.sqrt(v_hat) + ADAM_EPS) + ADAM_WD * w)
    return delta, m2, v2


def _adamw(w, g, m, v, name, echo_g=False):
    rows, cols = w.shape
    tb = _row_tile(rows, 256)
    nout = 4 if echo_g else 3

    def body(w_ref, g_ref, m_ref, v_ref, d_ref, m2_ref, v2_ref, *echo):
        gv = g_ref[...]
        d, m2, v2 = _adamw_math(w_ref[...], gv, m_ref[...], v_ref[...])
        d_ref[...] = d
        m2_ref[...] = m2
        v2_ref[...] = v2
        if echo_g:
            echo[0][...] = gv

    spec = pl.BlockSpec((tb, cols), lambda i: (i, 0))
    return pl.pallas_call(
        body, name=name, grid=(rows // tb,), in_specs=[spec] * 4, out_specs=[spec] * nout,
        out_shape=[SDS((rows, cols), F32)] * nout, compiler_params=_cp(("arbitrary",), 40),
    )(w, g, m, v)


def _ada_w_update(sc_all, dmod_k, w, m, v):
    rows, cols = w.shape
    tb = 256

    def body(s_ref, dm_ref, w_ref, m_ref, v_ref, g_ref, d_ref, m2_ref, v2_ref):
        g = _tn(s_ref[...].astype(BF16), dm_ref[...].astype(BF16))
        d, m2, v2 = _adamw_math(w_ref[...], g, m_ref[...], v_ref[...])
        g_ref[...] = g
        d_ref[...] = d
        m2_ref[...] = m2
        v2_ref[...] = v2

    spec = pl.BlockSpec((tb, cols), lambda i: (i, 0))
    return pl.pallas_call(
        body, name="ada_w_update", grid=(rows // tb,),
        in_specs=[pl.BlockSpec((N_DEV, tb), lambda i: (0, i)), pl.BlockSpec((N_DEV, cols), lambda i: (0, 0)), spec, spec, spec],
        out_specs=[spec] * 4, out_shape=[SDS((rows, cols), F32)] * 4, compiler_params=_cp(("arbitrary",), 40),
    )(sc_all, dmod_k, w, m, v)


def _ada_mod(c_all, w, b_k):
    rows, cols = w.shape
    tn = 512

    def body(c_ref, w_ref, b_ref, o_ref, s_ref):
        cv = c_ref[...]
        s = cv * _sigmoid(cv)
        s_ref[...] = s
        o_ref[...] = _nn(s.astype(BF16), w_ref[...].astype(BF16)) + b_ref[...]

    return pl.pallas_call(
        body, name="ada_mod", grid=(cols // tn,),
        in_specs=[pl.BlockSpec((N_DEV, rows), lambda j: (0, 0)), pl.BlockSpec((rows, tn), lambda j: (0, j)),
                  pl.BlockSpec((1, tn), lambda j: (0, j))],
        out_specs=[pl.BlockSpec((N_DEV, tn), lambda j: (0, j)), pl.BlockSpec((N_DEV, rows), lambda j: (0, 0))],
        out_shape=[SDS((N_DEV, cols), F32), SDS((N_DEV, rows), F32)], compiler_params=_cp(("arbitrary",)),
    )(c_all, w, b_k)


def _inproj(x, norm_g, scale, shift, w_t, shards):
    seq, dm = x.shape
    ncols = w_t.shape[0]
    tm, tn = min(INPROJ_TM, seq), INPROJ_TN
    ni, nj = seq // tm, ncols // tn
    na = len(shards)

    def body(x_ref, g_ref, sc_ref, sh_ref, w_ref, *rest):
        shard_refs, h_ref, u_ref, got_refs = rest[:na], rest[na], rest[na + 1], rest[na + 2:2 * na + 2]
        ssem, rsem = rest[2 * na + 2:]
        i, j = pl.program_id(0), pl.program_id(1)
        mx, my, mc = _place()
        k = 2 * mx + my
        chips, kks = _chips_of(mx, my)

        def copy(a, d, slot):
            return pltpu.make_async_remote_copy(
                src_ref=shard_refs[a], dst_ref=got_refs[a].at[slot], send_sem=ssem.at[3 * a + d],
                recv_sem=rsem.at[3 * a + d], device_id=(chips[d][0], chips[d][1], mc), device_id_type=MESH)

        @pl.when((i == 0) & (j == 0))
        def _():
            for a in range(na):
                for d in range(3):
                    copy(a, d, k).start()

        @pl.when(j == 0)
        def _():
            xv = x_ref[...]
            r = lax.rsqrt(jnp.mean(xv * xv, axis=-1, keepdims=True) + EPS)
            hv = (xv * r * g_ref[...]) * (1.0 + sc_ref[...]) + sh_ref[...]
            h_ref[...] = hv.astype(BF16)

        u_ref[...] = _nt(h_ref[...], w_ref[...]).astype(BF16)

        @pl.when((i == ni - 1) & (j == nj - 1))
        def _():
            for a in range(na):
                for d in range(3):
                    copy(a, d, kks[d]).wait_recv()
            for a in range(na):
                for d in range(3):
                    copy(a, d, k).wait_send()

    vec = pl.BlockSpec((1, dm), lambda i, j: (0, 0))
    outs = pl.pallas_call(
        body, name="inproj", grid=(ni, nj),
        in_specs=[pl.BlockSpec((tm, dm), lambda i, j: (i, 0)), vec, vec, vec, pl.BlockSpec((tn, dm), lambda i, j: (j, 0))]
                 + [ANY] * na,
        out_specs=[pl.BlockSpec((tm, dm), lambda i, j: (i, 0)), pl.BlockSpec((tm, tn), lambda i, j: (i, j))] + [ANY] * na,
        out_shape=[SDS((seq, dm), BF16), SDS((seq, ncols), BF16)] + [SDS((N_CHIPS,) + s.shape, s.dtype) for s in shards],
        scratch_shapes=[pltpu.SemaphoreType.DMA((3 * na,))] * 2,
        compiler_params=_cp(("arbitrary", "arbitrary"), 48),
    )(x, norm_g, scale, shift, w_t, *shards)
    return outs[0], outs[1], outs[2:]


HALO = 16


def _conv_taps(uc, halo, ext_ref, ts, causal):
    if causal:
        ext_ref[0:HALO, :] = halo
        ext_ref[HALO:HALO + ts, :] = uc
        return ext_ref[pl.ds(HALO - 1, ts), :], ext_ref[pl.ds(HALO - 2, ts), :]
    ext_ref[0:ts, :] = uc
    ext_ref[ts:ts + HALO, :] = halo
    return ext_ref[pl.ds(1, ts), :], ext_ref[pl.ds(2, ts), :]


def _conv_fwd(u, conv_w):
    seq = u.shape[0]
    ts = min(ROW_T, seq)
    hb = ts // HALO

    def body(xc_ref, bc_ref, cc_ref, zc_ref, xp_ref, cp_ref, w_ref, y_ref, ext_ref):
        i = pl.program_id(0)
        uc = cc_ref[...].astype(F32) * xc_ref[...].astype(F32)
        up = cp_ref[...].astype(F32) * xp_ref[...].astype(F32)
        up = jnp.where(i > 0, up, 0.0)
        u1, u2 = _conv_taps(uc, up, ext_ref, ts, True)
        conv = w_ref[0:1, :] * u2 + w_ref[1:2, :] * u1 + w_ref[2:3, :] * uc
        z = zc_ref[...].astype(F32)
        y_ref[...] = ((bc_ref[...].astype(F32) * conv) * (z * _sigmoid(z))).astype(BF16)

    def col(cb):
        return pl.BlockSpec((ts, D_CONV), lambda i: (i, cb))

    def prev(cb):
        return pl.BlockSpec((HALO, D_CONV), lambda i: (jnp.maximum(i * hb - 1, 0), cb))

    return pl.pallas_call(
        body, name="conv_fwd", grid=(seq // ts,),
        in_specs=[col(0), col(1), col(2), col(3), prev(0), prev(2), pl.BlockSpec((3, D_CONV), lambda i: (0, 0))],
        out_specs=pl.BlockSpec((ts, D_CONV), lambda i: (i, 0)), out_shape=SDS((seq, D_CONV), BF16),
        scratch_shapes=[pltpu.VMEM((ts + HALO, D_CONV), F32)],
        compiler_params=_cp(("arbitrary",), 40),
    )(u, u, u, u, u, u, conv_w)


def _rope_tables(pos_ref, freq_ref):
    ang = pos_ref[...].astype(F32) * freq_ref[...]
    lane = lax.broadcasted_iota(jnp.int32, ang.shape, 1)
    cs, sn = jnp.cos(ang), jnp.sin(ang)
    half = QK_ROPE // 2
    cos_t = jnp.where(lane < QK_ROPE, cs, 0.0)
    sin_lo = jnp.where(lane < half, sn, 0.0)
    sin_hi = jnp.where((lane >= half) & (lane < QK_ROPE), sn, 0.0)
    return cos_t, sin_lo, sin_hi


def _rope(blk, tables):
    cos_t, sin_lo, sin_hi = tables
    half = QK_ROPE // 2
    return blk * cos_t - pltpu.roll(blk, LANES - half, 1) * sin_lo + pltpu.roll(blk, half, 1) * sin_hi


def _rope_bwd(g, tables):
    cos_t, sin_lo, sin_hi = tables
    half = QK_ROPE // 2
    return g * cos_t + pltpu.roll(g, LANES - half, 1) * sin_lo - pltpu.roll(g, half, 1) * sin_hi


def _rms(v, n):
    r = lax.rsqrt(jnp.sum(v * v, axis=-1, keepdims=True) * (1.0 / n) + EPS)
    return v * r, r


def _mla_prep(u, pos, freq, q_a_g, wq, kv_a_g, wkn, wv, q_g, k_g):
    seq = u.shape[0]
    ts = min(ROW_T, seq)
    qscale = LOG2E / math.sqrt(QK_HEAD)

    def body(cq_ref, ckv_ref, kr_ref, pos_ref, freq_ref, qag_ref, wq_ref, kvag_ref, wkn_ref, wv_ref, qg_ref, kg_ref,
             q_ref, k_ref, v_ref):
        tables = _rope_tables(pos_ref, freq_ref)
        cqn, _ = _rms(cq_ref[...].astype(F32), Q_LORA)
        qp = _nn((cqn * qag_ref[...]).astype(BF16), wq_ref[...])
        qg = qg_ref[...]
        for h in range(N_HEADS):
            lo = h * HEAD_PAD
            qn, _ = _rms(qp[:, lo:lo + HEAD_PAD], QK_HEAD)
            qn = qn * qg
            q_ref[:, lo:lo + LANES] = (qn[:, :LANES] * qscale).astype(BF16)
            q_ref[:, lo + LANES:lo + HEAD_PAD] = (_rope(qn[:, LANES:], tables) * qscale).astype(BF16)
        ckvn, _ = _rms(ckv_ref[...].astype(F32), KV_LORA)
        ckvb = (ckvn * kvag_ref[...]).astype(BF16)
        kn = _nn(ckvb, wkn_ref[...])
        v_ref[...] = _nn(ckvb, wv_ref[...]).astype(BF16)
        kr = kr_ref[:, 0:LANES].astype(F32)
        ssr = jnp.sum(kr * kr, axis=-1, keepdims=True)
        kg = kg_ref[...]
        for h in range(N_HEADS):
            knh = kn[:, h * QK_NOPE:(h + 1) * QK_NOPE]
            r = lax.rsqrt((jnp.sum(knh * knh, axis=-1, keepdims=True) + ssr) * (1.0 / QK_HEAD) + EPS)
            lo = h * HEAD_PAD
            k_ref[:, lo:lo + LANES] = (knh * r * kg[:, :LANES]).astype(BF16)
            k_ref[:, lo + LANES:lo + HEAD_PAD] = _rope(kr * r * kg[:, LANES:], tables).astype(BF16)

    def full(a):
        return pl.BlockSpec(a.shape, lambda i: (0,) * a.ndim)

    return pl.pallas_call(
        body, name="mla_prep", grid=(seq // ts,),
        in_specs=[pl.BlockSpec((ts, Q_LORA), lambda i: (i, U_CQ // Q_LORA)),
                  pl.BlockSpec((ts, KV_LORA), lambda i: (i, U_CKV // KV_LORA)),
                  pl.BlockSpec((ts, KR_PAD), lambda i: (i, U_KR // KR_PAD)),
                  pl.BlockSpec((ts, 1), lambda i: (i, 0)), full(freq), full(q_a_g), full(wq), full(kv_a_g), full(wkn),
                  full(wv), full(q_g), full(k_g)],
        out_specs=[pl.BlockSpec((ts, Q_PAD), lambda i: (i, 0)), pl.BlockSpec((ts, Q_PAD), lambda i: (i, 0)),
                   pl.BlockSpec((ts, D_ATTN), lambda i: (i, 0))],
        out_shape=[SDS((seq, Q_PAD), BF16), SDS((seq, Q_PAD), BF16), SDS((seq, D_ATTN), BF16)],
        compiler_params=_cp(("arbitrary",), 48),
    )(u, u, u, pos, freq, q_a_g, wq, kv_a_g, wkn, wv, q_g, k_g)


def _causal_mask(t, tq, q0):
    return lax.broadcasted_iota(jnp.int32, (t, tq), 0) <= lax.broadcasted_iota(jnp.int32, (t, tq), 1) + q0


def _flash_fwd(q, k, v, u):
    seq = q.shape[0]
    t = min(ATT_T, seq)
    za_blk = U_ZA // V_HEAD

    def body(q_ref, k_ref, v_ref, z_ref, o_ref, y_ref, lse_ref, s_a, s_b, top_a, top_b, m_ref, l_ref, acc_ref):
        i = pl.program_id(1)
        ones = jnp.ones((16, t), BF16)
        bufs = ((s_a, top_a), (s_b, top_b))

        def keys(j):
            return pl.ds(pl.multiple_of(j * t, t), t)

        def scores(j, buf):
            s_ref, top_ref = buf
            s = _nt(k_ref[keys(j), :], q_ref[...])
            s_ref[...] = s
            top_ref[...] = jnp.max(s, axis=0, keepdims=True)

        def absorb(j, buf, masked):
            s_ref, top_ref = buf
            s = s_ref[...]
            if masked:
                s = jnp.where(_causal_mask(t, t, 0), s, -jnp.inf)
                top = jnp.max(s, axis=0, keepdims=True)
            else:
                top = top_ref[...]
            m = m_ref[...]
            m_new = jnp.maximum(m, top)
            alpha = jnp.exp2(m - m_new)
            p = jnp.exp2((s - m_new).astype(BF16))
            m_ref[...] = m_new
            l_ref[...] = alpha * l_ref[...] + _nn(ones, p)[0:1, :]
            acc_ref[...] = alpha * acc_ref[...] + _tn(v_ref[keys(j), :], p)

        scores(0, bufs[0])
        m_ref[...] = jnp.full_like(m_ref, -jnp.inf)
        l_ref[...] = jnp.zeros_like(l_ref)
        acc_ref[...] = jnp.zeros_like(acc_ref)

        def trip(width):
            def body(jj, carry):
                for w in range(width):
                    j = width * jj + w
                    scores(j + 1, bufs[(w + 1) % 2])
                    absorb(j, bufs[w % 2], False)
                return carry
            return body

        quads = i // ATT_UNROLL
        lax.fori_loop(0, quads, trip(ATT_UNROLL), 0)
        lax.fori_loop(quads * (ATT_UNROLL // 2), i // 2, trip(2), 0)

        @pl.when(i % 2 == 1)
        def _():
            scores(i, bufs[1])
            absorb(i - 1, bufs[0], False)
            absorb(i, bufs[1], True)

        @pl.when(i % 2 == 0)
        def _():
            absorb(i, bufs[0], True)

        l = l_ref[...]
        o = (acc_ref[...] * (1.0 / l)).T
        lse_ref[...] = m_ref[...] + jnp.log2(l)
        o_ref[...] = o.astype(BF16)
        z = z_ref[...].astype(F32)
        y_ref[...] = (o * (z * _sigmoid(z))).astype(BF16)

    tile = pl.BlockSpec((t, V_HEAD), lambda h, i: (i, h))
    return pl.pallas_call(
        body, name="flash_fwd", grid=(N_HEADS, seq // t),
        in_specs=[pl.BlockSpec((t, HEAD_PAD), lambda h, i: (i, h)), pl.BlockSpec((seq, HEAD_PAD), lambda h, i: (0, h)),
                  pl.BlockSpec((seq, V_HEAD), lambda h, i: (0, h)), pl.BlockSpec((t, V_HEAD), lambda h, i: (i, za_blk + h))],
        out_specs=[tile, tile, pl.BlockSpec((None, 1, t), lambda h, i: (h, 0, i))],
        out_shape=[SDS((seq, D_ATTN), BF16), SDS((seq, D_ATTN), BF16), SDS((N_HEADS, 1, seq), F32)],
        scratch_shapes=[pltpu.VMEM((t, t), F32)] * 2 + [pltpu.VMEM((1, t), F32)] * 4 + [pltpu.VMEM((V_HEAD, t), F32)],
        compiler_params=_cp(("arbitrary", "arbitrary"), 40),
    )(q, k, v, u)


def _outproj_loss(y_conv, y_attn, x, target, gate, w_out):
    seq, dm = x.shape
    ts = min(OUT_T, seq)
    n = seq // ts
    dmix = w_out.shape[0]

    def body(yc_ref, ya_ref, x_ref, t_ref, gate_ref, wo_hbm, dout_ref, dy_ref, dyc_ref, stats_ref, wo_ref, sem, acc_ref):
        i = pl.program_id(0)

        @pl.when(i == 0)
        def _():
            cp = pltpu.make_async_copy(wo_hbm, wo_ref, sem)
            cp.start()
            cp.wait()
            acc_ref[...] = jnp.zeros_like(acc_ref)

        y = _nn(yc_ref[...], wo_ref[0:D_CONV, :]) + _nn(ya_ref[...], wo_ref[D_CONV:dmix, :])
        gate_v = gate_ref[...]
        diff = (x_ref[...] + gate_v * y) - t_ref[...]
        dout = diff * (1.0 / dm)
        dout_ref[...] = dout
        acc_ref[0:8, :] += jnp.sum((dout * y).reshape(ts // 8, 8, dm), axis=0)
        acc_ref[8:16, :] += jnp.sum((diff * diff).reshape(ts // 8, 8, dm), axis=0)
        dy = (dout * gate_v).astype(BF16)
        dy_ref[...] = dy
        dyc_ref[...] = _nt(dy, wo_ref[...]).astype(BF16)

        @pl.when(i == n - 1)
        def _():
            stats_ref[...] = jnp.zeros_like(stats_ref)
            stats_ref[0:1, :] = jnp.sum(acc_ref[0:8, :], axis=0, keepdims=True)
            loss = jnp.sum(acc_ref[8:16, :]) * (0.5 / dm)
            stats_ref[1:2, :] = jnp.full((1, dm), loss, F32)

    row = pl.BlockSpec((ts, dm), lambda i: (i, 0))
    half = pl.BlockSpec((ts, D_CONV), lambda i: (i, 0))
    return pl.pallas_call(
        body, name="outproj_loss", grid=(n,),
        in_specs=[half, half, row, row, pl.BlockSpec((1, dm), lambda i: (0, 0)), ANY],
        out_specs=[row, row, pl.BlockSpec((ts, dmix), lambda i: (i, 0)), pl.BlockSpec((8, dm), lambda i: (0, 0))],
        out_shape=[SDS((seq, dm), F32), SDS((seq, dm), BF16), SDS((seq, dmix), BF16), SDS((8, dm), F32)],
        scratch_shapes=[pltpu.VMEM(w_out.shape, BF16), pltpu.SemaphoreType.DMA(()), pltpu.VMEM((16, dm), F32)],
        compiler_params=_cp(("arbitrary",), 52),
    )(y_conv, y_attn, x, target, gate, w_out)


def _matmul_tn(a, b, name):
    seq, m = a.shape
    n = b.shape[1]
    tm, tn, tk = min(TN_TM, m), min(TN_TN, n), min(TN_TK, seq)
    nk = seq // tk

    def body(a_ref, b_ref, o_ref, acc_ref):
        kk = pl.program_id(2)

        @pl.when(kk == 0)
        def _():
            acc_ref[...] = jnp.zeros_like(acc_ref)

        acc_ref[...] += _tn(a_ref[...], b_ref[...])

        @pl.when(kk == nk - 1)
        def _():
            o_ref[...] = acc_ref[...].astype(BF16)

    return pl.pallas_call(
        body, name=name, grid=(m // tm, n // tn, nk),
        in_specs=[pl.BlockSpec((tk, tm), lambda i, j, kk: (kk, i)), pl.BlockSpec((tk, tn), lambda i, j, kk: (kk, j))],
        out_specs=pl.BlockSpec((tm, tn), lambda i, j, kk: (i, j)), out_shape=SDS((m, n), BF16),
        scratch_shapes=[pltpu.VMEM((tm, tn), F32)],
        compiler_params=_cp(("arbitrary", "arbitrary", "arbitrary"), 40),
    )(a, b)


def _attn_gate_bwd(dycat, o, u):
    seq = o.shape[0]
    ts = min(ROW_T, seq)

    def body(dy_ref, o_ref, z_ref, dot_ref, dz_ref, dl_ref):
        dy = dy_ref[...].astype(F32)
        ov = o_ref[...].astype(F32)
        z = z_ref[...].astype(F32)
        sg = _sigmoid(z)
        do = dy * (z * sg)
        dz_ref[...] = (dy * ov * _silu_grad(z, sg)).astype(BF16)
        prod = do * ov
        ones = jnp.ones((8, V_HEAD), F32)
        for h in range(N_HEADS):
            cols = slice(h * V_HEAD, (h + 1) * V_HEAD)
            dot_ref[h] = do[:, cols].T.astype(BF16)
            rows = lax.dot_general(ones, prod[:, cols], (((1,), (1,)), ((), ())), precision=lax.Precision.HIGHEST,
                                   preferred_element_type=F32)
            dl_ref[h] = rows[0:1, :]

    blk = pl.BlockSpec((ts, D_ATTN), lambda i: (i, 0))
    return pl.pallas_call(
        body, name="attn_gate_bwd", grid=(seq // ts,),
        in_specs=[pl.BlockSpec((ts, D_ATTN), lambda i: (i, 1)), blk, pl.BlockSpec((ts, D_ATTN), lambda i: (i, U_ZA // D_ATTN))],
        out_specs=[pl.BlockSpec((N_HEADS, V_HEAD, ts), lambda i: (0, 0, i)), blk,
                   pl.BlockSpec((N_HEADS, 1, ts), lambda i: (0, 0, i))],
        out_shape=[SDS((N_HEADS, V_HEAD, seq), BF16), SDS((seq, D_ATTN), BF16), SDS((N_HEADS, 1, seq), F32)],
        compiler_params=_cp(("arbitrary",), 40),
    )(dycat, o, u)


def _flash_bwd(q, k, v, do_t, lse, delta):
    seq = q.shape[0]
    t = min(ATT_T, seq)
    n = seq // t

    def body(k_ref, v_ref, q_ref, dot_ref, lse_ref, dl_ref, dq_ref, dk_ref, dv_ref, s_a, s_b, dp_a, dp_b, dvt_ref):
        j = pl.program_id(1)

        @pl.when(j == 0)
        def _():
            dq_ref[...] = jnp.zeros_like(dq_ref)

        def rows(r):
            return pl.ds(pl.multiple_of((j + r) * t, t), t)

        def products(r, s_ref, dp_ref):
            s_ref[...] = _nt(k_ref[...], q_ref[rows(r), :])
            dp_ref[...] = _nn(v_ref[...], dot_ref[:, rows(r)])

        def absorb(r, s_ref, dp_ref, masked):
            p = jnp.exp2((s_ref[...] - lse_ref[:, rows(r)]).astype(BF16))
            if masked:
                p = jnp.where(_causal_mask(t, t, 0), p, jnp.zeros_like(p))
            dvt_ref[...] += _nt(dot_ref[:, rows(r)], p)
            ds = p * (dp_ref[...] - dl_ref[:, rows(r)]).astype(BF16)
            dk_ref[...] += _nn(ds, q_ref[rows(r), :])
            dq_ref[rows(r), :] += _tn(ds, k_ref[...])

        dk_ref[...] = jnp.zeros_like(dk_ref)
        dvt_ref[...] = jnp.zeros_like(dvt_ref)
        products(0, s_a, dp_a)
        last = n - 1 - j

        @pl.when(last == 0)
        def _():
            absorb(0, s_a, dp_a, True)

        @pl.when(last > 0)
        def _():
            products(1, s_b, dp_b)
            absorb(0, s_a, dp_a, True)

        inner = jnp.maximum(last - 1, 0)

        bufs = ((s_b, dp_b), (s_a, dp_a))

        def trip(width):
            def body(pp, carry):
                for w in range(width):
                    r = 1 + width * pp + w
                    products(r + 1, *bufs[(w + 1) % 2])
                    absorb(r, *bufs[w % 2], False)
                return carry
            return body

        quads = inner // ATT_UNROLL
        lax.fori_loop(0, quads, trip(ATT_UNROLL), 0)
        lax.fori_loop(quads * (ATT_UNROLL // 2), inner // 2, trip(2), 0)

        @pl.when((last > 0) & (inner % 2 == 1))
        def _():
            products(last, s_a, dp_a)
            absorb(last - 1, s_b, dp_b, False)
            absorb(last, s_a, dp_a, False)

        @pl.when((last > 0) & (inner % 2 == 0))
        def _():
            absorb(last, s_b, dp_b, False)

        dv_ref[...] = dvt_ref[...].T.astype(BF16)

    row = pl.BlockSpec((None, 1, seq), lambda h, j: (h, 0, 0))
    return pl.pallas_call(
        body, name="flash_bwd", grid=(N_HEADS, n),
        in_specs=[pl.BlockSpec((t, HEAD_PAD), lambda h, j: (j, h)), pl.BlockSpec((t, V_HEAD), lambda h, j: (j, h)),
                  pl.BlockSpec((seq, HEAD_PAD), lambda h, j: (0, h)), pl.BlockSpec((None, V_HEAD, seq), lambda h, j: (h, 0, 0)),
                  row, row],
        out_specs=[pl.BlockSpec((seq, HEAD_PAD), lambda h, j: (0, h)), pl.BlockSpec((t, HEAD_PAD), lambda h, j: (j, h)),
                   pl.BlockSpec((t, V_HEAD), lambda h, j: (j, h))],
        out_shape=[SDS((seq, Q_PAD), F32), SDS((seq, Q_PAD), F32), SDS((seq, D_ATTN), BF16)],
        scratch_shapes=[pltpu.VMEM((t, t), F32)] * 4 + [pltpu.VMEM((V_HEAD, t), F32)],
        compiler_params=_cp(("arbitrary", "arbitrary"), 56),
    )(k, v, q, do_t, lse, delta)


SG_QAG, SG_KVAG, SG_QG, SG_KG, SG_COLS = 0, Q_LORA, Q_LORA + KV_LORA, Q_LORA + KV_LORA + HEAD_PAD, D_MODEL


def _mla_bwd(dq, dk, dv, u, pos, freq, q_a_g, wq, kv_a_g, wkn, wv, q_g, k_g):
    seq = u.shape[0]
    ts = min(ROW_T, seq)
    n = seq // ts
    qscale = 1.0 / math.sqrt(QK_HEAD)

    def body(dq_ref, dk_ref, dv_ref, cq_ref, ckv_ref, kr_ref, pos_ref, freq_ref, qag_ref, wq_ref, kvag_ref, wkn_ref,
             wv_ref, qg_ref, kg_ref, du_ref, dwq_ref, dwkn_ref, dwv_ref, sg_ref, dqp_ref, dkn_ref):
        i = pl.program_id(0)

        @pl.when(i == 0)
        def _():
            dwq_ref[...] = jnp.zeros_like(dwq_ref)
            dwkn_ref[...] = jnp.zeros_like(dwkn_ref)
            dwv_ref[...] = jnp.zeros_like(dwv_ref)
            sg_ref[...] = jnp.zeros_like(sg_ref)

        tables = _rope_tables(pos_ref, freq_ref)

        cq = cq_ref[...].astype(F32)
        cqn, rq = _rms(cq, Q_LORA)
        qag = qag_ref[...]
        cqb = (cqn * qag).astype(BF16)
        qp = _nn(cqb, wq_ref[...])
        qg = qg_ref[...]
        dqg = jnp.zeros((1, HEAD_PAD), F32)
        for h in range(N_HEADS):
            lo = h * HEAD_PAD
            xn, r = _rms(qp[:, lo:lo + HEAD_PAD], QK_HEAD)
            g = jnp.concatenate([dq_ref[:, lo:lo + LANES], _rope_bwd(dq_ref[:, lo + LANES:lo + HEAD_PAD], tables)],
                                axis=-1) * qscale
            dqg = dqg + jnp.sum(g * xn, axis=0, keepdims=True)
            gy = g * qg
            mean = jnp.sum(gy * xn, axis=-1, keepdims=True) * (1.0 / QK_HEAD)
            dqp_ref[:, lo:lo + HEAD_PAD] = (r * (gy - xn * mean)).astype(BF16)
        dqp = dqp_ref[...]
        dwq_ref[...] += _tn(cqb, dqp)
        dcqn = _nt(dqp, wq_ref[...])
        sg_ref[0:1, SG_QAG:SG_QAG + Q_LORA] += jnp.sum(dcqn * cqn, axis=0, keepdims=True)
        sg_ref[0:1, SG_QG:SG_QG + HEAD_PAD] += dqg
        gy = dcqn * qag
        mean = jnp.sum(gy * cqn, axis=-1, keepdims=True) * (1.0 / Q_LORA)
        du_ref[:, 0:Q_LORA] = (rq * (gy - cqn * mean)).astype(BF16)

        ckv = ckv_ref[...].astype(F32)
        ckvn, rkv = _rms(ckv, KV_LORA)
        kvag = kvag_ref[...]
        ckvb = (ckvn * kvag).astype(BF16)
        kn = _nn(ckvb, wkn_ref[...])
        kr = kr_ref[:, 0:LANES].astype(F32)
        ssr = jnp.sum(kr * kr, axis=-1, keepdims=True)
        kg = kg_ref[...]
        kg_n, kg_r = kg[:, :LANES] * LN2, kg[:, LANES:] * LN2
        dkg_n = jnp.zeros((1, LANES), F32)
        dkg_r = jnp.zeros((1, LANES), F32)
        dkr = jnp.zeros((ts, LANES), F32)
        for h in range(N_HEADS):
            knh = kn[:, h * QK_NOPE:(h + 1) * QK_NOPE]
            r = lax.rsqrt((jnp.sum(knh * knh, axis=-1, keepdims=True) + ssr) * (1.0 / QK_HEAD) + EPS)
            xn_n, xn_r = knh * r, kr * r
            lo = h * HEAD_PAD
            g_n = dk_ref[:, lo:lo + LANES]
            g_r = _rope_bwd(dk_ref[:, lo + LANES:lo + HEAD_PAD], tables)
            dkg_n = dkg_n + jnp.sum(g_n * xn_n, axis=0, keepdims=True)
            dkg_r = dkg_r + jnp.sum(g_r * xn_r, axis=0, keepdims=True)
            gy_n, gy_r = g_n * kg_n, g_r * kg_r
            mean = (jnp.sum(gy_n * xn_n, axis=-1, keepdims=True) + jnp.sum(gy_r * xn_r, axis=-1, keepdims=True)) * (1.0 / QK_HEAD)
            dkn_ref[:, h * QK_NOPE:(h + 1) * QK_NOPE] = (r * (gy_n - xn_n * mean)).astype(BF16)
            dkr = dkr + r * (gy_r - xn_r * mean)
        dkn = dkn_ref[...]
        dvv = dv_ref[...]
        dwkn_ref[...] += _tn(ckvb, dkn)
        dwv_ref[...] += _tn(ckvb, dvv)
        dckvn = _nt(dkn, wkn_ref[...]) + _nt(dvv, wv_ref[...])
        sg_ref[0:1, SG_KVAG:SG_KVAG + KV_LORA] += jnp.sum(dckvn * ckvn, axis=0, keepdims=True)
        sg_ref[0:1, SG_KG:SG_KG + LANES] += dkg_n * LN2
        sg_ref[0:1, SG_KG + LANES:SG_KG + HEAD_PAD] += dkg_r * LN2
        gy = dckvn * kvag
        mean = jnp.sum(gy * ckvn, axis=-1, keepdims=True) * (1.0 / KV_LORA)
        du_ref[:, Q_LORA:Q_LORA + KV_LORA] = (rkv * (gy - ckvn * mean)).astype(BF16)
        du_ref[:, Q_LORA + KV_LORA:Q_LORA + KV_LORA + LANES] = dkr.astype(BF16)
        du_ref[:, Q_LORA + KV_LORA + LANES:MLA_COLS] = jnp.zeros((ts, LANES), BF16)

    def full(a):
        return pl.BlockSpec(a.shape, lambda i: (0,) * a.ndim)

    wide = pl.BlockSpec((ts, Q_PAD), lambda i: (i, 0))
    return pl.pallas_call(
        body, name="mla_bwd", grid=(n,),
        in_specs=[wide, wide, pl.BlockSpec((ts, D_ATTN), lambda i: (i, 0)),
                  pl.BlockSpec((ts, Q_LORA), lambda i: (i, U_CQ // Q_LORA)),
                  pl.BlockSpec((ts, KV_LORA), lambda i: (i, U_CKV // KV_LORA)),
                  pl.BlockSpec((ts, KR_PAD), lambda i: (i, U_KR // KR_PAD)),
                  pl.BlockSpec((ts, 1), lambda i: (i, 0)), full(freq), full(q_a_g), full(wq), full(kv_a_g), full(wkn),
                  full(wv), full(q_g), full(k_g)],
        out_specs=[pl.BlockSpec((ts, MLA_COLS), lambda i: (i, 0)), pl.BlockSpec((Q_LORA, Q_PAD), lambda i: (0, 0)),
                   pl.BlockSpec((KV_LORA, D_ATTN), lambda i: (0, 0)), pl.BlockSpec((KV_LORA, D_ATTN), lambda i: (0, 0)),
                   pl.BlockSpec((8, SG_COLS), lambda i: (0, 0))],
        out_shape=[SDS((seq, MLA_COLS), BF16), SDS((Q_LORA, Q_PAD), F32), SDS((KV_LORA, D_ATTN), F32),
                   SDS((KV_LORA, D_ATTN), F32), SDS((8, SG_COLS), F32)],
        scratch_shapes=[pltpu.VMEM((ts, Q_PAD), BF16), pltpu.VMEM((ts, D_ATTN), BF16)],
        compiler_params=_cp(("arbitrary",), 56),
    )(dq, dk, dv, u, u, u, pos, freq, q_a_g, wq, kv_a_g, wkn, wv, q_g, k_g)


def _conv_bwd(dycat, u, conv_w):
    seq = u.shape[0]
    ts = min(ROW_T, seq)
    n = seq // ts
    hb = ts // HALO

    def body(dy_ref, xc_ref, bc_ref, cc_ref, zc_ref, xp_ref, cp_ref, dyn_ref, bn_ref, zn_ref, w_ref,
             du_ref, dw_ref, ext_ref):
        i = pl.program_id(0)

        @pl.when(i == 0)
        def _():
            dw_ref[...] = jnp.zeros_like(dw_ref)

        xc = xc_ref[...].astype(F32)
        cc = cc_ref[...].astype(F32)
        uc = cc * xc
        up = jnp.where(i > 0, cp_ref[...].astype(F32) * xp_ref[...].astype(F32), 0.0)
        u1, u2 = _conv_taps(uc, up, ext_ref, ts, True)
        w0, w1, w2 = w_ref[0:1, :], w_ref[1:2, :], w_ref[2:3, :]
        conv = w0 * u2 + w1 * u1 + w2 * uc
        z = zc_ref[...].astype(F32)
        sg = _sigmoid(z)
        sz = z * sg
        b = bc_ref[...].astype(F32)
        dy = dy_ref[...].astype(F32)
        du_ref[:, 3 * D_CONV:4 * D_CONV] = (dy * (b * conv) * _silu_grad(z, sg)).astype(BF16)
        du_ref[:, D_CONV:2 * D_CONV] = (dy * sz * conv).astype(BF16)
        dconv = dy * sz * b
        dw_ref[0:1, :] += jnp.sum(dconv * u2, axis=0, keepdims=True)
        dw_ref[1:2, :] += jnp.sum(dconv * u1, axis=0, keepdims=True)
        dw_ref[2:3, :] += jnp.sum(dconv * uc, axis=0, keepdims=True)
        zn = zn_ref[...].astype(F32)
        dnext = dyn_ref[...].astype(F32) * (zn * _sigmoid(zn)) * bn_ref[...].astype(F32)
        dnext = jnp.where(i < n - 1, dnext, 0.0)
        d1, d2 = _conv_taps(dconv, dnext, ext_ref, ts, False)
        du = w2 * dconv + w1 * d1 + w0 * d2
        du_ref[:, 2 * D_CONV:3 * D_CONV] = (du * xc).astype(BF16)
        du_ref[:, 0:D_CONV] = (du * cc).astype(BF16)

    def col(cb):
        return pl.BlockSpec((ts, D_CONV), lambda i: (i, cb))

    def prev(cb):
        return pl.BlockSpec((HALO, D_CONV), lambda i: (jnp.maximum(i * hb - 1, 0), cb))

    def nxt(cb):
        return pl.BlockSpec((HALO, D_CONV), lambda i: (jnp.minimum((i + 1) * hb, n * hb - 1), cb))

    return pl.pallas_call(
        body, name="conv_bwd", grid=(n,),
        in_specs=[col(0), col(0), col(1), col(2), col(3), prev(0), prev(2), nxt(0), nxt(1), nxt(3),
                  pl.BlockSpec((3, D_CONV), lambda i: (0, 0))],
        out_specs=[pl.BlockSpec((ts, 4 * D_CONV), lambda i: (i, 0)), pl.BlockSpec((8, D_CONV), lambda i: (0, 0))],
        out_shape=[SDS((seq, 4 * D_CONV), BF16), SDS((8, D_CONV), F32)],
        scratch_shapes=[pltpu.VMEM((ts + HALO, D_CONV), F32)],
        compiler_params=_cp(("arbitrary",), 48),
    )(dycat, u, u, u, u, u, u, dycat, u, u, conv_w)


def _inproj_bwd(du_conv, du_za, du_mla, w_t, parts):
    seq = du_conv.shape[0]
    dm = w_t.shape[1]
    tm, tn = min(DH_TM, seq), DH_TN
    ni, nj = seq // tm, dm // tn
    na = len(parts)

    def body(dc_ref, dz_ref, dm_ref, w_ref, *rest):
        part_refs, o_ref, recv_refs = rest[:na], rest[na], rest[na + 1:2 * na + 1]
        ssem, rsem = rest[2 * na + 1:]
        i, j = pl.program_id(0), pl.program_id(1)
        sends, recvs = _chip_exchange_copies(part_refs, recv_refs, ssem, rsem)

        @pl.when((i == 0) & (j == 0))
        def _():
            for cp in sends:
                cp.start()

        acc = _nn(dc_ref[...], w_ref[0:U_ZA, :])
        acc = acc + _nn(dz_ref[...], w_ref[U_ZA:U_CQ, :])
        acc = acc + _nn(dm_ref[...], w_ref[U_CQ:U_COLS, :])
        o_ref[...] = acc

        @pl.when((i == ni - 1) & (j == nj - 1))
        def _():
            for cp in recvs:
                cp.wait_recv()
            for cp in sends:
                cp.wait_send()

    outs = pl.pallas_call(
        body, name="inproj_bwd", grid=(ni, nj),
        in_specs=[pl.BlockSpec((tm, U_ZA), lambda i, j: (i, 0)), pl.BlockSpec((tm, D_ATTN), lambda i, j: (i, 0)),
                  pl.BlockSpec((tm, MLA_COLS), lambda i, j: (i, 0)), pl.BlockSpec((U_COLS, tn), lambda i, j: (0, j))]
                 + [ANY] * na,
        out_specs=[pl.BlockSpec((tm, tn), lambda i, j: (i, j))] + [ANY] * na,
        out_shape=[SDS((seq, dm), F32)] + [SDS(p.shape, p.dtype) for p in parts],
        scratch_shapes=[pltpu.SemaphoreType.DMA((3 * na,))] * 2,
        compiler_params=_cp(("arbitrary", "arbitrary"), 48),
    )(du_conv, du_za, du_mla, w_t, *parts)
    return outs[0], outs[1:]


def _prenorm_bwd(x, dh, dout, norm_g, scale):
    seq, dm = x.shape
    ts = min(ROW_T, seq)
    n = seq // ts

    def body(x_ref, dh_ref, dout_ref, g_ref, sc_ref, gx_ref, st_ref, acc_ref):
        i = pl.program_id(0)

        @pl.when(i == 0)
        def _():
            acc_ref[...] = jnp.zeros_like(acc_ref)

        xv = x_ref[...]
        xn, r = _rms(xv, dm)
        dh_v = dh_ref[...]
        gv = g_ref[...]
        one_sc = 1.0 + sc_ref[...]

        def fold(a):
            return jnp.sum(a.reshape(ts // 8, 8, dm), axis=0)

        acc_ref[0:8, :] += fold(dh_v)
        acc_ref[8:16, :] += fold(dh_v * (xn * gv))
        dxg = dh_v * one_sc
        acc_ref[16:24, :] += fold(dxg * xn)
        dxn = dxg * gv
        mean = jnp.sum(dxn * xn, axis=-1, keepdims=True) * (1.0 / dm)
        gx_ref[...] = dout_ref[...] + r * (dxn - xn * mean)

        @pl.when(i == n - 1)
        def _():
            st_ref[...] = jnp.zeros_like(st_ref)
            for k in range(3):
                st_ref[k:k + 1, :] = jnp.sum(acc_ref[8 * k:8 * k + 8, :], axis=0, keepdims=True)

    row = pl.BlockSpec((ts, dm), lambda i: (i, 0))
    vec = pl.BlockSpec((1, dm), lambda i: (0, 0))
    return pl.pallas_call(
        body, name="prenorm_bwd", grid=(n,), in_specs=[row, row, row, vec, vec],
        out_specs=[row, pl.BlockSpec((8, dm), lambda i: (0, 0))],
        out_shape=[SDS((seq, dm), F32), SDS((8, dm), F32)],
        scratch_shapes=[pltpu.VMEM((24, dm), F32)], input_output_aliases={2: 0},
        compiler_params=_cp(("arbitrary",), 52),
    )(x, dh, dout, norm_g, scale)


def _unshard_cols(g):
    return jnp.transpose(g, (1, 0, 2)).reshape(g.shape[1], -1)


def _shard_cols(w):
    r = w.shape[0]
    return jnp.transpose(w.reshape(r, N_CHIPS, -1), (1, 0, 2))


W_IN_COLS = 4 * D_CONV + Q_LORA + KV_LORA + QK_ROPE + D_ATTN
SHARD_ROWS = W_IN_COLS // N_CHIPS
SHARD_PAD = 1536


def _w_t_to_my(g):
    w = g[:, :SHARD_ROWS, :].reshape(W_IN_COLS, g.shape[2])
    c4 = 4 * D_CONV
    cq, ckv, kr, za = c4, c4 + Q_LORA, c4 + Q_LORA + KV_LORA, c4 + Q_LORA + KV_LORA + QK_ROPE
    pad = jnp.zeros((KR_PAD - QK_ROPE, w.shape[1]), w.dtype)
    return jnp.concatenate([w[:c4], w[za:], w[cq:ckv], w[ckv:kr], w[kr:za], pad], axis=0)


def _w_t_from_my(g_conv, g_za, g_mla):
    w = jnp.concatenate([g_conv, g_mla[:Q_LORA + KV_LORA + QK_ROPE], g_za], axis=0)
    w = w.reshape(N_CHIPS, SHARD_ROWS, w.shape[1])
    return jnp.pad(w, ((0, 0), (0, SHARD_PAD - SHARD_ROWS), (0, 0)))


def _heads_pad(w):
    r = w.shape[0]
    w3 = w.reshape(r, N_HEADS, QK_HEAD)
    return jnp.pad(w3, ((0, 0), (0, 0), (0, HEAD_PAD - QK_HEAD))).reshape(r, Q_PAD)


def _heads_unpad(w):
    r = w.shape[0]
    return w.reshape(r, N_HEADS, HEAD_PAD)[:, :, :QK_HEAD].reshape(r, N_HEADS * QK_HEAD)


def kernel(x, c, positions, ada_w, ada_b, norm_g, w_in, conv_w, q_a_g, w_q_b, kv_a_g, w_kv_b, q_g, k_g, w_out, loss_target, m_ada_w, m_ada_b, m_norm_g, m_w_in, m_conv_w, m_q_a_g, m_w_q_b, m_kv_a_g, m_w_kv_b, m_q_g, m_k_g, m_w_out, v_ada_w, v_ada_b, v_norm_g, v_w_in, v_conv_w, v_q_a_g, v_w_q_b, v_kv_a_g, v_w_kv_b, v_q_g, v_k_g, v_w_out):
    mx, my, mc = _place()
    chip = 2 * mx + my
    me = 2 * chip + mc
    seq = x.shape[1]
    x2, t2 = x[0], loss_target[0]
    cw_cols = conv_w.shape[2]

    small = jnp.zeros((8, D_MODEL), F32)
    small = small.at[0].set(c[0])
    small = small.at[1:4, :cw_cols].set(conv_w[0])
    small_all = _gather8(small, "gather_c_conv", False)[0]
    c_all = small_all[:, 0, :]
    conv_full = jnp.transpose(small_all.reshape(N_CHIPS, 2, 8, D_MODEL)[:, 0, 1:4, :cw_cols], (1, 0, 2)).reshape(3, D_CONV)

    ada_cols = ada_w.shape[2]
    b_k = lax.dynamic_slice(ada_b, (0, chip * ada_cols), (1, ada_cols))
    mod_k, sc_all = _ada_mod(c_all, ada_w[0], b_k)
    mod_all = _gather8(mod_k, "gather_mod", False)[0]
    mod_row = lax.dynamic_slice(mod_all.reshape(N_CHIPS, 2, N_DEV, ada_cols), (0, mc, me, 0), (N_CHIPS, 1, 1, ada_cols))
    mod_row = mod_row.reshape(3, D_MODEL)
    shift, scale, gate = mod_row[0:1], mod_row[1:2], mod_row[2:3]

    def own_slot(g, s):
        return lax.dynamic_update_slice(g, s[None], (chip, 0, 0))

    w_in_t, m_w_in_t, v_w_in_t = [jnp.transpose(a[0]) for a in (w_in, m_w_in, v_w_in)]
    shard_in = jnp.pad(w_in_t.astype(BF16), ((0, SHARD_PAD - SHARD_ROWS), (0, 0)))
    g_in = own_slot(_allgather_shards([shard_in])[0], shard_in)
    w_t = _w_t_to_my(g_in)

    later = [w_q_b[0].astype(BF16), w_kv_b[0].astype(BF16), w_out[0].astype(BF16)]
    h, u, got = _inproj(x2, norm_g, scale, shift, w_t, later)
    g_q, g_kv, g_out = [own_slot(g, s) for g, s in zip(got, later)]
    wq = _heads_pad(_unshard_cols(g_q))
    wkv = _unshard_cols(g_kv).reshape(KV_LORA, N_HEADS, QK_NOPE + V_HEAD)
    wkn = wkv[:, :, :QK_NOPE].reshape(KV_LORA, N_HEADS * QK_NOPE)
    wv = wkv[:, :, QK_NOPE:].reshape(KV_LORA, D_ATTN)
    wo = g_out.reshape(N_CHIPS * g_out.shape[1], D_MODEL)
    y_conv = _conv_fwd(u, conv_full)
    pos = positions.reshape(seq, 1)
    inv_freq = ROPE_BASE ** (-jnp.arange(0, QK_ROPE, 2, dtype=F32) / QK_ROPE)
    freq = jnp.concatenate([inv_freq, inv_freq, jnp.zeros((LANES - QK_ROPE,), F32)]).reshape(1, LANES)
    q_g_pad = jnp.pad(q_g, ((0, 0), (0, HEAD_PAD - QK_HEAD)))
    k_g_pad = jnp.pad(k_g, ((0, 0), (0, HEAD_PAD - QK_HEAD)))
    q, k, v = _mla_prep(u, pos, freq, q_a_g, wq, kv_a_g, wkn, wv, q_g_pad, k_g_pad)
    o, y_attn, lse = _flash_fwd(q, k, v, u)
    dout, dy, dycat, st_out = _outproj_loss(y_conv, y_attn, x2, t2, gate, wo)

    dw_out = jnp.concatenate([_matmul_tn(y_conv, dy, "dw_out_conv"), _matmul_tn(y_attn, dy, "dw_out_attn")], axis=0)
    do_t, du_za, delta = _attn_gate_bwd(dycat, o, u)
    dq, dk, dv = _flash_bwd(q, k, v, do_t, lse, delta)
    du_mla, dwq, dwkn, dwv, sg_mla = _mla_bwd(dq, dk, dv, u, pos, freq, q_a_g, wq, kv_a_g, wkn, wv, q_g_pad, k_g_pad)
    du_conv, dconv_w = _conv_bwd(dycat, u, conv_full)
    dw_conv = _matmul_tn(du_conv, h, "dw_in_conv")
    dw_za = _matmul_tn(du_za, h, "dw_in_za")
    dw_mla = _matmul_tn(du_mla, h, "dw_in_mla")

    dw_q_nat = _heads_unpad(dwq).astype(BF16)
    dw_kv_nat = jnp.concatenate([dwkn.reshape(KV_LORA, N_HEADS, QK_NOPE), dwv.reshape(KV_LORA, N_HEADS, V_HEAD)],
                                axis=2).reshape(KV_LORA, N_HEADS * (QK_NOPE + V_HEAD)).astype(BF16)
    grads = [_w_t_from_my(dw_conv, dw_za, dw_mla), _shard_cols(dw_q_nat), _shard_cols(dw_kv_nat),
             dw_out.reshape(N_CHIPS, dw_out.shape[0] // N_CHIPS, D_MODEL)]
    theirs = _rs_core_swap(grads)
    names = ["w_in", "w_q_b", "w_kv_b", "w_out"]
    core = jnp.reshape(mc, (1,)).astype(jnp.int32)
    parts = [_add_half_bf16(g, b, core, "rs_add_" + nm) for g, b, nm in zip(grads, theirs, names)]
    dh, recv = _inproj_bwd(du_conv, du_za, du_mla, w_t, parts)
    recv = [lax.dynamic_update_slice(r, lax.dynamic_slice(p, (chip, 0, 0), (1,) + p.shape[1:]), (chip, 0, 0))
            for r, p in zip(recv, parts)]
    halves = [_sum_chips(p, "rs_sum_" + nm) for p, nm in zip(recv, names)]
    joined = _rs_core_join(halves)
    joined = [lax.dynamic_update_slice(j, hf[None], (mc, 0, 0)) for j, hf in zip(joined, halves)]
    g_big = [j.reshape(2 * j.shape[1], j.shape[2]) for j in joined]
    grad_x, st_in = _prenorm_bwd(x2, dh, dout, norm_g, scale)

    sgrad = jnp.zeros((8, D_MODEL), F32)
    sgrad = sgrad.at[0:2].set(st_in[0:2])
    sgrad = sgrad.at[2].set(st_out[0])
    sgrad = sgrad.at[3].set(st_in[2])
    sgrad = sgrad.at[4, :D_CONV].set(dconv_w[0]).at[4, D_CONV:].set(dconv_w[1])
    sgrad = sgrad.at[5, :D_CONV].set(dconv_w[2]).at[5, D_CONV:].set(sg_mla[0, :D_CONV])
    sgrad = sgrad.at[6, :HEAD_PAD].set(sg_mla[0, SG_KG:SG_KG + HEAD_PAD])
    sgrad = sgrad.at[7].set(st_out[1])
    sg_all, sg_sum = _gather8(sgrad, "gather_small_grads", True)
    loss = sg_sum[7, 0]
    g_ada_b = sg_sum[0:3].reshape(1, 3 * D_MODEL)
    g_norm_g = sg_sum[3:4]
    conv_sum = jnp.stack([sg_sum[4, :D_CONV], sg_sum[4, D_CONV:], sg_sum[5, :D_CONV]])
    g_conv_w = lax.dynamic_slice(conv_sum, (0, chip * cw_cols), (3, cw_cols))
    g_q_a_g = sg_sum[5:6, D_CONV + SG_QAG:D_CONV + SG_QAG + Q_LORA]
    g_kv_a_g = sg_sum[5:6, D_CONV + SG_KVAG:D_CONV + SG_KVAG + KV_LORA]
    g_q_g = sg_sum[5:6, D_CONV + SG_QG:D_CONV + SG_QG + QK_HEAD]
    g_k_g = sg_sum[6:7, :QK_HEAD]
    dmod_k = lax.dynamic_slice(sg_all[:, 0:3, :].reshape(N_DEV, 3 * D_MODEL), (0, chip * ada_cols), (N_DEV, ada_cols))

    g_ada_w, d_ada_w, nm_ada_w, nv_ada_w = _ada_w_update(sc_all, dmod_k, ada_w[0], m_ada_w[0], v_ada_w[0])
    upd = {}
    big = {"w_q_b": (w_q_b, m_w_q_b, v_w_q_b), "w_kv_b": (w_kv_b, m_w_kv_b, v_w_kv_b), "w_out": (w_out, m_w_out, v_w_out)}
    for nm, g in zip(names[1:], g_big[1:]):
        w_, m_, v_ = big[nm]
        upd[nm] = (g,) + tuple(_adamw(w_[0], g, m_[0], v_[0], "adamw_" + nm))
    d_t, nm_t, nv_t, g_t = _adamw(w_in_t, g_big[0], m_w_in_t, v_w_in_t, "adamw_w_in", echo_g=True)
    upd["w_in"] = tuple(jnp.transpose(a) for a in (g_t, d_t, nm_t, nv_t))
    small_w = {"ada_b": (ada_b, m_ada_b, v_ada_b, g_ada_b), "norm_g": (norm_g, m_norm_g, v_norm_g, g_norm_g),
               "conv_w": (conv_w[0], m_conv_w[0], v_conv_w[0], g_conv_w), "q_a_g": (q_a_g, m_q_a_g, v_q_a_g, g_q_a_g),
               "kv_a_g": (kv_a_g, m_kv_a_g, v_kv_a_g, g_kv_a_g), "q_g": (q_g, m_q_g, v_q_g, g_q_g),
               "k_g": (k_g, m_k_g, v_k_g, g_k_g)}
    for nm, (w_, m_, v_, g) in small_w.items():
        upd[nm] = (g,) + tuple(_adamw(w_, g, m_, v_, "adamw_" + nm))
    upd["ada_w"] = (g_ada_w, d_ada_w, nm_ada_w, nv_ada_w)

    order = ["ada_w", "ada_b", "norm_g", "w_in", "conv_w", "q_a_g", "w_q_b", "kv_a_g", "w_kv_b", "q_g", "k_g", "w_out"]
    lead1 = {"ada_w", "w_in", "conv_w", "w_q_b", "w_kv_b", "w_out"}

    def shaped(nm, a):
        return a[None] if nm in lead1 else a

    outs = [loss, grad_x[None]]
    for idx in range(4):
        outs += [shaped(nm, upd[nm][idx]) for nm in order]
    return tuple(outs)
```

```python
import functools
import math

import jax
import jax.numpy as jnp
from jax import lax
from jax.experimental import pallas as pl
from jax.experimental.pallas import tpu as pltpu

F32 = jnp.float32
BF16 = jnp.bfloat16
MESH = pl.DeviceIdType.MESH
SDS = jax.ShapeDtypeStruct
ANY = pl.BlockSpec(memory_space=pl.ANY)

D_MODEL = 2048
D_CONV = 1024
N_HEADS = 8
QK_NOPE = 128
QK_ROPE = 64
QK_HEAD = QK_NOPE + QK_ROPE
V_HEAD = 128
D_ATTN = N_HEADS * V_HEAD
Q_LORA = 512
KV_LORA = 256
ROPE_BASE = 10000.0
EPS = 1e-6
LOG2E = math.log2(math.e)
LN2 = math.log(2.0)
ADAM_LR, ADAM_B1, ADAM_B2, ADAM_EPS, ADAM_WD, ADAM_STEP = 0.001, 0.9, 0.999, 1e-08, 0.01, 10
N_CHIPS = 4
N_DEV = 8

LANES = 128
V7X_VMEM_BYTES = 64 * 1024 * 1024
MIB = 1024 * 1024

HEAD_PAD = 256
Q_PAD = N_HEADS * HEAD_PAD
U_ZA = 4 * D_CONV
U_CQ = U_ZA + D_ATTN
U_CKV = U_CQ + Q_LORA
U_KR = U_CKV + KV_LORA
KR_PAD = 256
U_COLS = U_KR + KR_PAD
MLA_COLS = Q_LORA + KV_LORA + KR_PAD

ATT_T = 512
INPROJ_TM, INPROJ_TN = 1024, 512
ROW_T = 512
OUT_T = 256
DH_TM, DH_TN = 512, 512
TN_TM, TN_TN, TN_TK = 1024, 1024, 2048
ATT_UNROLL = 4


def _cp(sem=None, vmem_mib=None, **kw):
    if sem is not None:
        kw["dimension_semantics"] = sem
    if vmem_mib is not None:
        kw["vmem_limit_bytes"] = min(vmem_mib * MIB, V7X_VMEM_BYTES - 4 * MIB)
    return pltpu.CompilerParams(**kw)


def _sigmoid(z):
    return 1.0 / (1.0 + jnp.exp(-z))


def _silu_grad(z, sg):
    return sg * (1.0 + z * (1.0 - sg))


def _nt(a, b):
    return lax.dot_general(a, b, (((1,), (1,)), ((), ())), preferred_element_type=F32)


def _tn(a, b):
    return lax.dot_general(a, b, (((0,), (0,)), ((), ())), preferred_element_type=F32)


def _nn(a, b):
    return jnp.dot(a, b, preferred_element_type=F32)


def _place():
    return lax.axis_index("x"), lax.axis_index("y"), lax.axis_index("c")


def _gather8(v, name, with_sum):
    rows, cols = v.shape

    def body(v_ref, out_ref, *rest):
        if with_sum:
            sum_ref, send_sems, recv_sems = rest
        else:
            send_sems, recv_sems = rest
        mx, my, mc = _place()
        me = 4 * mx + 2 * my + mc
        out_ref[me] = v_ref[...]
        peers = []
        for d in range(1, N_DEV):
            px = 1 - mx if d & 4 else mx
            py = 1 - my if d & 2 else my
            pc = 1 - mc if d & 1 else mc
            peers.append((px, py, pc))

        def copy(d, slot, to):
            return pltpu.make_async_remote_copy(
                src_ref=v_ref, dst_ref=out_ref.at[slot], send_sem=send_sems.at[d], recv_sem=recv_sems.at[d],
                device_id=to, device_id_type=MESH)

        sends = [copy(d, me, p) for d, p in enumerate(peers)]
        for cp in sends:
            cp.start()
        for d, (px, py, pc) in enumerate(peers):
            copy(d, 4 * px + 2 * py + pc, (px, py, pc)).wait_recv()
        for cp in sends:
            cp.wait_send()
        if with_sum:
            acc = out_ref[0]
            for b in range(1, N_DEV):
                acc = acc + out_ref[b]
            sum_ref[...] = acc

    out_shape = [SDS((N_DEV, rows, cols), F32)]
    if with_sum:
        out_shape.append(SDS((rows, cols), F32))
    vm = pl.BlockSpec(memory_space=pltpu.VMEM)
    return pl.pallas_call(
        body, name=name, out_shape=out_shape, in_specs=[vm], out_specs=[vm] * len(out_shape),
        scratch_shapes=[pltpu.SemaphoreType.DMA((N_DEV - 1,)), pltpu.SemaphoreType.DMA((N_DEV - 1,))],
    )(v)


def _chips_of(mx, my):
    chips = [(mx, 1 - my), (1 - mx, my), (1 - mx, 1 - my)]
    return chips, [2 * px + py for px, py in chips]


def _allgather_shards(shards):
    na = len(shards)
    halves = [s.shape[0] // 2 for s in shards]

    def body(*refs):
        ins, outs = refs[:na], refs[na:2 * na]
        s1, r1, s2, r2 = refs[2 * na:]
        mx, my, mc = _place()
        k = 2 * mx + my
        sib = (mx, my, 1 - mc)
        chips, kks = _chips_of(mx, my)

        def half(a, slot, c):
            return outs[a].at[slot, pl.ds(c * halves[a], halves[a])]

        def mine(a):
            return ins[a].at[pl.ds(mc * halves[a], halves[a])]

        sends = []
        for a in range(na):
            for d, (px, py) in enumerate(chips):
                cp = pltpu.make_async_remote_copy(
                    src_ref=mine(a), dst_ref=half(a, k, mc), send_sem=s1.at[3 * a + d], recv_sem=r1.at[3 * a + d],
                    device_id=(px, py, mc), device_id_type=MESH)
                cp.start()
                sends.append(cp)
        for a in range(na):
            for d, (px, py) in enumerate(chips):
                pltpu.make_async_remote_copy(
                    src_ref=mine(a), dst_ref=half(a, kks[d], mc), send_sem=s1.at[3 * a + d], recv_sem=r1.at[3 * a + d],
                    device_id=(px, py, mc), device_id_type=MESH).wait_recv()
                cp = pltpu.make_async_remote_copy(
                    src_ref=half(a, kks[d], mc), dst_ref=half(a, kks[d], mc), send_sem=s2.at[3 * a + d],
                    recv_sem=r2.at[3 * a + d], device_id=sib, device_id_type=MESH)
                cp.start()
                sends.append(cp)
        for a in range(na):
            for d in range(3):
                pltpu.make_async_remote_copy(
                    src_ref=half(a, kks[d], 1 - mc), dst_ref=half(a, kks[d], 1 - mc), send_sem=s2.at[3 * a + d],
                    recv_sem=r2.at[3 * a + d], device_id=sib, device_id_type=MESH).wait_recv()
        for cp in sends:
            cp.wait_send()

    return pl.pallas_call(
        body, name="allgather_weights",
        out_shape=[SDS((N_CHIPS,) + s.shape, s.dtype) for s in shards],
        in_specs=[ANY] * na, out_specs=[ANY] * na,
        scratch_shapes=[pltpu.SemaphoreType.DMA((3 * na,))] * 4,
    )(*shards)


def _rs_core_swap(grads):
    na = len(grads)
    halves = [g.shape[1] // 2 for g in grads]

    def body(*refs):
        ins, outs = refs[:na], refs[na:2 * na]
        ssem, rsem = refs[2 * na:]
        mx, my, mc = _place()
        sib = (mx, my, 1 - mc)
        sends = []
        for a in range(na):
            cp = pltpu.make_async_remote_copy(
                src_ref=ins[a].at[:, pl.ds((1 - mc) * halves[a], halves[a])], dst_ref=outs[a],
                send_sem=ssem.at[a], recv_sem=rsem.at[a], device_id=sib, device_id_type=MESH)
            cp.start()
            sends.append(cp)
        for cp in sends:
            cp.wait_recv()
        for cp in sends:
            cp.wait_send()

    return pl.pallas_call(
        body, name="rs_core_swap", out_shape=[SDS((N_CHIPS, h) + g.shape[2:], g.dtype) for g, h in zip(grads, halves)],
        in_specs=[ANY] * na, out_specs=[ANY] * na,
        scratch_shapes=[pltpu.SemaphoreType.DMA((na,))] * 2,
    )(*grads)


def _chip_exchange_copies(ins, outs, ssem, rsem):
    mx, my, mc = _place()
    k = 2 * mx + my
    chips, kks = _chips_of(mx, my)
    sends, recvs = [], []
    for a in range(len(ins)):
        for d, (px, py) in enumerate(chips):
            def copy(dst_slot):
                return pltpu.make_async_remote_copy(
                    src_ref=ins[a].at[kks[d]], dst_ref=outs[a].at[dst_slot], send_sem=ssem.at[3 * a + d],
                    recv_sem=rsem.at[3 * a + d], device_id=(px, py, mc), device_id_type=MESH)
            sends.append(copy(k))
            recvs.append(copy(kks[d]))
    return sends, recvs


def _rs_core_join(halves):
    na = len(halves)

    def body(*refs):
        ins, outs = refs[:na], refs[na:2 * na]
        ssem, rsem = refs[2 * na:]
        mx, my, mc = _place()
        sib = (mx, my, 1 - mc)
        sends = []
        for a in range(na):
            cp = pltpu.make_async_remote_copy(
                src_ref=ins[a], dst_ref=outs[a].at[mc], send_sem=ssem.at[a], recv_sem=rsem.at[a],
                device_id=sib, device_id_type=MESH)
            cp.start()
            sends.append(cp)
        for a in range(na):
            pltpu.make_async_remote_copy(
                src_ref=ins[a], dst_ref=outs[a].at[1 - mc], send_sem=ssem.at[a], recv_sem=rsem.at[a],
                device_id=sib, device_id_type=MESH).wait_recv()
        for cp in sends:
            cp.wait_send()

    return pl.pallas_call(
        body, name="rs_core_join", out_shape=[SDS((2,) + h.shape, h.dtype) for h in halves],
        in_specs=[ANY] * na, out_specs=[ANY] * na,
        scratch_shapes=[pltpu.SemaphoreType.DMA((na,))] * 2,
    )(*halves)


def _row_tile(rows, limit, mult=16):
    if rows <= limit:
        return rows
    best = None
    for t in range(mult, limit + 1, mult):
        if rows % t == 0:
            best = t
    assert best is not None, rows
    return best


def _add_half_bf16(g, b, core, name):
    _, h, cols = b.shape
    tb = _row_tile(h, 512)
    nb = h // tb

    def body(core_ref, g_ref, b_ref, o_ref):
        o_ref[...] = (g_ref[...].astype(F32) + b_ref[...].astype(F32)).astype(BF16)

    spec = pl.BlockSpec((None, tb, cols), lambda kk, i, core_ref: (kk, i, 0))
    return pl.pallas_call(
        body, name=name,
        grid_spec=pltpu.PrefetchScalarGridSpec(
            num_scalar_prefetch=1, grid=(N_CHIPS, nb),
            in_specs=[pl.BlockSpec((None, tb, cols), lambda kk, i, core_ref: (kk, core_ref[0] * nb + i, 0)), spec],
            out_specs=spec),
        out_shape=SDS(b.shape, BF16), compiler_params=_cp(("arbitrary", "arbitrary")),
    )(core, g, b)


def _sum_chips(p, name):
    _, rows, cols = p.shape
    tb = _row_tile(rows, 256)

    def body(p_ref, o_ref):
        acc = p_ref[0].astype(F32)
        for j in range(1, N_CHIPS):
            acc = acc + p_ref[j].astype(F32)
        o_ref[...] = acc

    return pl.pallas_call(
        body, name=name, grid=(rows // tb,),
        in_specs=[pl.BlockSpec((N_CHIPS, tb, cols), lambda i: (0, i, 0))],
        out_specs=pl.BlockSpec((tb, cols), lambda i: (i, 0)), out_shape=SDS((rows, cols), F32),
        compiler_params=_cp(("arbitrary",)),
    )(p)


def _adamw_math(w, g, m, v):
    m2 = ADAM_B1 * m + (1.0 - ADAM_B1) * g
    v2 = ADAM_B2 * v + (1.0 - ADAM_B2) * (g * g)
    m_hat = m2 / (1.0 - ADAM_B1 ** ADAM_STEP)
    v_hat = v2 / (1.0 - ADAM_B2 ** ADAM_STEP)
    delta = -ADAM_LR * (m_hat / (jnp.sqrt(v_hat) + ADAM_EPS) + ADAM_WD * w)
    return delta, m2, v2


def _adamw(w, g, m, v, name, echo_g=False):
    rows, cols = w.shape
    tb = _row_tile(rows, 256, mult=8)
    nout = 4 if echo_g else 3

    def body(w_ref, g_ref, m_ref, v_ref, d_ref, m2_ref, v2_ref, *echo):
        gv = g_ref[...]
        d, m2, v2 = _adamw_math(w_ref[...], gv, m_ref[...], v_ref[...])
        d_ref[...] = d
        m2_ref[...] = m2
        v2_ref[...] = v2
        if echo_g:
            echo[0][...] = gv

    spec = pl.BlockSpec((tb, cols), lambda i: (i, 0))
    return pl.pallas_call(
        body, name=name, grid=(rows // tb,), in_specs=[spec] * 4, out_specs=[spec] * nout,
        out_shape=[SDS((rows, cols), F32)] * nout, compiler_params=_cp(("arbitrary",), 40),
    )(w, g, m, v)


def _ada_w_update(sc_all, dmod_k, w, m, v):
    rows, cols = w.shape
    tb = 256

    def body(s_ref, dm_ref, w_ref, m_ref, v_ref, g_ref, d_ref, m2_ref, v2_ref):
        g = _tn(s_ref[...].astype(BF16), dm_ref[...].astype(BF16))
        d, m2, v2 = _adamw_math(w_ref[...], g, m_ref[...], v_ref[...])
        g_ref[...] = g
        d_ref[...] = d
        m2_ref[...] = m2
        v2_ref[...] = v2

    spec = pl.BlockSpec((tb, cols), lambda i: (i, 0))
    return pl.pallas_call(
        body, name="ada_w_update", grid=(rows // tb,),
        in_specs=[pl.BlockSpec((N_DEV, tb), lambda i: (0, i)), pl.BlockSpec((N_DEV, cols), lambda i: (0, 0)), spec, spec, spec],
        out_specs=[spec] * 4, out_shape=[SDS((rows, cols), F32)] * 4, compiler_params=_cp(("arbitrary",), 40),
    )(sc_all, dmod_k, w, m, v)


def _ada_mod(c_all, w, b_k):
    rows, cols = w.shape
    tn = 512

    def body(c_ref, w_ref, b_ref, o_ref, s_ref):
        cv = c_ref[...]
        s = cv * _sigmoid(cv)
        s_ref[...] = s
        o_ref[...] = _nn(s.astype(BF16), w_ref[...].astype(BF16)) + b_ref[...]

    return pl.pallas_call(
        body, name="ada_mod", grid=(cols // tn,),
        in_specs=[pl.BlockSpec((N_DEV, rows), lambda j: (0, 0)), pl.BlockSpec((rows, tn), lambda j: (0, j)),
                  pl.BlockSpec((1, tn), lambda j: (0, j))],
        out_specs=[pl.BlockSpec((N_DEV, tn), lambda j: (0, j)), pl.BlockSpec((N_DEV, rows), lambda j: (0, 0))],
        out_shape=[SDS((N_DEV, cols), F32), SDS((N_DEV, rows), F32)], compiler_params=_cp(("arbitrary",)),
    )(c_all, w, b_k)


def _inproj(x, norm_g, scale, shift, w_t, shards):
    seq, dm = x.shape
    ncols = w_t.shape[0]
    tm, tn = min(INPROJ_TM, seq), INPROJ_TN
    ni, nj = seq // tm, ncols // tn
    na = len(shards)

    def body(x_ref, g_ref, sc_ref, sh_ref, w_ref, *rest):
        shard_refs, h_ref, u_ref, got_refs = rest[:na], rest[na], rest[na + 1], rest[na + 2:2 * na + 2]
        ssem, rsem = rest[2 * na + 2:]
        i, j = pl.program_id(0), pl.program_id(1)
        mx, my, mc = _place()
        k = 2 * mx + my
        chips, kks = _chips_of(mx, my)

        def copy(a, d, slot):
            return pltpu.make_async_remote_copy(
                src_ref=shard_refs[a], dst_ref=got_refs[a].at[slot], send_sem=ssem.at[3 * a + d],
                recv_sem=rsem.at[3 * a + d], device_id=(chips[d][0], chips[d][1], mc), device_id_type=MESH)

        @pl.when((i == 0) & (j == 0))
        def _():
            for a in range(na):
                for d in range(3):
                    copy(a, d, k).start()

        @pl.when(j == 0)
        def _():
            xv = x_ref[...]
            r = lax.rsqrt(jnp.mean(xv * xv, axis=-1, keepdims=True) + EPS)
            hv = (xv * r * g_ref[...]) * (1.0 + sc_ref[...]) + sh_ref[...]
            h_ref[...] = hv.astype(BF16)

        u_ref[...] = _nt(h_ref[...], w_ref[...]).astype(BF16)

        @pl.when((i == ni - 1) & (j == nj - 1))
        def _():
            for a in range(na):
                for d in range(3):
                    copy(a, d, kks[d]).wait_recv()
            for a in range(na):
                for d in range(3):
                    copy(a, d, k).wait_send()

    vec = pl.BlockSpec((1, dm), lambda i, j: (0, 0))
    outs = pl.pallas_call(
        body, name="inproj", grid=(ni, nj),
        in_specs=[pl.BlockSpec((tm, dm), lambda i, j: (i, 0)), vec, vec, vec, pl.BlockSpec((tn, dm), lambda i, j: (j, 0))]
                 + [ANY] * na,
        out_specs=[pl.BlockSpec((tm, dm), lambda i, j: (i, 0)), pl.BlockSpec((tm, tn), lambda i, j: (i, j))] + [ANY] * na,
        out_shape=[SDS((seq, dm), BF16), SDS((seq, ncols), BF16)] + [SDS((N_CHIPS,) + s.shape, s.dtype) for s in shards],
        scratch_shapes=[pltpu.SemaphoreType.DMA((3 * na,))] * 2,
        compiler_params=_cp(("arbitrary", "arbitrary"), 48),
    )(x, norm_g, scale, shift, w_t, *shards)
    return outs[0], outs[1], outs[2:]


HALO = 16


def _conv_taps(uc, halo, ext_ref, ts, causal):
    if causal:
        ext_ref[0:HALO, :] = halo
        ext_ref[HALO:HALO + ts, :] = uc
        return ext_ref[pl.ds(HALO - 1, ts), :], ext_ref[pl.ds(HALO - 2, ts), :]
    ext_ref[0:ts, :] = uc
    ext_ref[ts:ts + HALO, :] = halo
    return ext_ref[pl.ds(1, ts), :], ext_ref[pl.ds(2, ts), :]


def _conv_fwd(u, conv_w):
    seq = u.shape[0]
    ts = min(ROW_T, seq)
    hb = ts // HALO

    def body(xc_ref, bc_ref, cc_ref, zc_ref, xp_ref, cp_ref, w_ref, y_ref, ext_ref):
        i = pl.program_id(0)
        uc = cc_ref[...].astype(F32) * xc_ref[...].astype(F32)
        up = cp_ref[...].astype(F32) * xp_ref[...].astype(F32)
        up = jnp.where(i > 0, up, 0.0)
        u1, u2 = _conv_taps(uc, up, ext_ref, ts, True)
        conv = w_ref[0:1, :] * u2 + w_ref[1:2, :] * u1 + w_ref[2:3, :] * uc
        z = zc_ref[...].astype(F32)
        y_ref[...] = ((bc_ref[...].astype(F32) * conv) * (z * _sigmoid(z))).astype(BF16)

    def col(cb):
        return pl.BlockSpec((ts, D_CONV), lambda i: (i, cb))

    def prev(cb):
        return pl.BlockSpec((HALO, D_CONV), lambda i: (jnp.maximum(i * hb - 1, 0), cb))

    return pl.pallas_call(
        body, name="conv_fwd", grid=(seq // ts,),
        in_specs=[col(0), col(1), col(2), col(3), prev(0), prev(2), pl.BlockSpec((3, D_CONV), lambda i: (0, 0))],
        out_specs=pl.BlockSpec((ts, D_CONV), lambda i: (i, 0)), out_shape=SDS((seq, D_CONV), BF16),
        scratch_shapes=[pltpu.VMEM((ts + HALO, D_CONV), F32)],
        compiler_params=_cp(("arbitrary",), 40),
    )(u, u, u, u, u, u, conv_w)


def _rope_tables(pos_ref, freq_ref):
    ang = pos_ref[...].astype(F32) * freq_ref[...]
    lane = lax.broadcasted_iota(jnp.int32, ang.shape, 1)
    cs, sn = jnp.cos(ang), jnp.sin(ang)
    half = QK_ROPE // 2
    cos_t = jnp.where(lane < QK_ROPE, cs, 0.0)
    sin_lo = jnp.where(lane < half, sn, 0.0)
    sin_hi = jnp.where((lane >= half) & (lane < QK_ROPE), sn, 0.0)
    return cos_t, sin_lo, sin_hi


def _rope(blk, tables):
    cos_t, sin_lo, sin_hi = tables
    half = QK_ROPE // 2
    return blk * cos_t - pltpu.roll(blk, LANES - half, 1) * sin_lo + pltpu.roll(blk, half, 1) * sin_hi


def _rope_bwd(g, tables):
    cos_t, sin_lo, sin_hi = tables
    half = QK_ROPE // 2
    return g * cos_t + pltpu.roll(g, LANES - half, 1) * sin_lo - pltpu.roll(g, half, 1) * sin_hi


def _rms(v, n):
    r = lax.rsqrt(jnp.sum(v * v, axis=-1, keepdims=True) * (1.0 / n) + EPS)
    return v * r, r


def _mla_prep(u, pos, freq, q_a_g, wq, kv_a_g, wkn, wv, q_g, k_g):
    seq = u.shape[0]
    ts = min(ROW_T, seq)
    qscale = LOG2E / math.sqrt(QK_HEAD)

    def body(cq_ref, ckv_ref, kr_ref, pos_ref, freq_ref, qag_ref, wq_ref, kvag_ref, wkn_ref, wv_ref, qg_ref, kg_ref,
             q_ref, k_ref, v_ref):
        tables = _rope_tables(pos_ref, freq_ref)
        cqn, _ = _rms(cq_ref[...].astype(F32), Q_LORA)
        qp = _nn((cqn * qag_ref[...]).astype(BF16), wq_ref[...])
        qg = qg_ref[...]
        for h in range(N_HEADS):
            lo = h * HEAD_PAD
            qn, _ = _rms(qp[:, lo:lo + HEAD_PAD], QK_HEAD)
            qn = qn * qg
            q_ref[:, lo:lo + LANES] = (qn[:, :LANES] * qscale).astype(BF16)
            q_ref[:, lo + LANES:lo + HEAD_PAD] = (_rope(qn[:, LANES:], tables) * qscale).astype(BF16)
        ckvn, _ = _rms(ckv_ref[...].astype(F32), KV_LORA)
        ckvb = (ckvn * kvag_ref[...]).astype(BF16)
        kn = _nn(ckvb, wkn_ref[...])
        v_ref[...] = _nn(ckvb, wv_ref[...]).astype(BF16)
        kr = kr_ref[:, 0:LANES].astype(F32)
        ssr = jnp.sum(kr * kr, axis=-1, keepdims=True)
        kg = kg_ref[...]
        for h in range(N_HEADS):
            knh = kn[:, h * QK_NOPE:(h + 1) * QK_NOPE]
            r = lax.rsqrt((jnp.sum(knh * knh, axis=-1, keepdims=True) + ssr) * (1.0 / QK_HEAD) + EPS)
            lo = h * HEAD_PAD
            k_ref[:, lo:lo + LANES] = (knh * r * kg[:, :LANES]).astype(BF16)
            k_ref[:, lo + LANES:lo + HEAD_PAD] = _rope(kr * r * kg[:, LANES:], tables).astype(BF16)

    def full(a):
        return pl.BlockSpec(a.shape, lambda i: (0,) * a.ndim)

    return pl.pallas_call(
        body, name="mla_prep", grid=(seq // ts,),
        in_specs=[pl.BlockSpec((ts, Q_LORA), lambda i: (i, U_CQ // Q_LORA)),
                  pl.BlockSpec((ts, KV_LORA), lambda i: (i, U_CKV // KV_LORA)),
                  pl.BlockSpec((ts, KR_PAD), lambda i: (i, U_KR // KR_PAD)),
                  pl.BlockSpec((ts, 1), lambda i: (i, 0)), full(freq), full(q_a_g), full(wq), full(kv_a_g), full(wkn),
                  full(wv), full(q_g), full(k_g)],
        out_specs=[pl.BlockSpec((ts, Q_PAD), lambda i: (i, 0)), pl.BlockSpec((ts, Q_PAD), lambda i: (i, 0)),
                   pl.BlockSpec((ts, D_ATTN), lambda i: (i, 0))],
        out_shape=[SDS((seq, Q_PAD), BF16), SDS((seq, Q_PAD), BF16), SDS((seq, D_ATTN), BF16)],
        compiler_params=_cp(("arbitrary",), 48),
    )(u, u, u, pos, freq, q_a_g, wq, kv_a_g, wkn, wv, q_g, k_g)


def _causal_mask(t, tq, q0):
    return lax.broadcasted_iota(jnp.int32, (t, tq), 0) <= lax.broadcasted_iota(jnp.int32, (t, tq), 1) + q0


def _flash_fwd(q, k, v, u):
    seq = q.shape[0]
    t = min(ATT_T, seq)
    za_blk = U_ZA // V_HEAD

    def body(q_ref, k_ref, v_ref, z_ref, o_ref, y_ref, lse_ref, s_a, s_b, top_a, top_b, m_ref, l_ref, acc_ref):
        i = pl.program_id(1)
        ones = jnp.ones((16, t), BF16)
        bufs = ((s_a, top_a), (s_b, top_b))

        def keys(j):
            return pl.ds(pl.multiple_of(j * t, t), t)

        def scores(j, buf):
            s_ref, top_ref = buf
            s = _nt(k_ref[keys(j), :], q_ref[...])
            s_ref[...] = s
            top_ref[...] = jnp.max(s, axis=0, keepdims=True)

        def absorb(j, buf, masked):
            s_ref, top_ref = buf
            s = s_ref[...]
            if masked:
                s = jnp.where(_causal_mask(t, t, 0), s, -jnp.inf)
                top = jnp.max(s, axis=0, keepdims=True)
            else:
                top = top_ref[...]
            m = m_ref[...]
            m_new = jnp.maximum(m, top)
            alpha = jnp.exp2(m - m_new)
            p = jnp.exp2((s - m_new).astype(BF16))
            m_ref[...] = m_new
            l_ref[...] = alpha * l_ref[...] + _nn(ones, p)[0:1, :]
            acc_ref[...] = alpha * acc_ref[...] + _tn(v_ref[keys(j), :], p)

        scores(0, bufs[0])
        m_ref[...] = jnp.full_like(m_ref, -jnp.inf)
        l_ref[...] = jnp.zeros_like(l_ref)
        acc_ref[...] = jnp.zeros_like(acc_ref)

        def trip(width):
            def body(jj, carry):
                for w in range(width):
                    j = width * jj + w
                    scores(j + 1, bufs[(w + 1) % 2])
                    absorb(j, bufs[w % 2], False)
                return carry
            return body

        quads = i // ATT_UNROLL
        lax.fori_loop(0, quads, trip(ATT_UNROLL), 0)
        lax.fori_loop(quads * (ATT_UNROLL // 2), i // 2, trip(2), 0)

        @pl.when(i % 2 == 1)
        def _():
            scores(i, bufs[1])
            absorb(i - 1, bufs[0], False)
            absorb(i, bufs[1], True)

        @pl.when(i % 2 == 0)
        def _():
            absorb(i, bufs[0], True)

        l = l_ref[...]
        o = (acc_ref[...] * (1.0 / l)).T
        lse_ref[...] = m_ref[...] + jnp.log2(l)
        o_ref[...] = o.astype(BF16)
        z = z_ref[...].astype(F32)
        y_ref[...] = (o * (z * _sigmoid(z))).astype(BF16)

    tile = pl.BlockSpec((t, V_HEAD), lambda h, i: (i, h))
    return pl.pallas_call(
        body, name="flash_fwd", grid=(N_HEADS, seq // t),
        in_specs=[pl.BlockSpec((t, HEAD_PAD), lambda h, i: (i, h)), pl.BlockSpec((seq, HEAD_PAD), lambda h, i: (0, h)),
                  pl.BlockSpec((seq, V_HEAD), lambda h, i: (0, h)), pl.BlockSpec((t, V_HEAD), lambda h, i: (i, za_blk + h))],
        out_specs=[tile, tile, pl.BlockSpec((None, 1, t), lambda h, i: (h, 0, i))],
        out_shape=[SDS((seq, D_ATTN), BF16), SDS((seq, D_ATTN), BF16), SDS((N_HEADS, 1, seq), F32)],
        scratch_shapes=[pltpu.VMEM((t, t), F32)] * 2 + [pltpu.VMEM((1, t), F32)] * 4 + [pltpu.VMEM((V_HEAD, t), F32)],
        compiler_params=_cp(("arbitrary", "arbitrary"), 40),
    )(q, k, v, u)


def _outproj_loss(y_conv, y_attn, x, target, gate, w_out):
    seq, dm = x.shape
    ts = min(OUT_T, seq)
    n = seq // ts
    dmix = w_out.shape[0]

    def body(yc_ref, ya_ref, x_ref, t_ref, gate_ref, wo_hbm, dout_ref, dy_ref, dyc_ref, stats_ref, wo_ref, sem, acc_ref):
        i = pl.program_id(0)

        @pl.when(i == 0)
        def _():
            cp = pltpu.make_async_copy(wo_hbm, wo_ref, sem)
            cp.start()
            cp.wait()
            acc_ref[...] = jnp.zeros_like(acc_ref)

        y = _nn(yc_ref[...], wo_ref[0:D_CONV, :]) + _nn(ya_ref[...], wo_ref[D_CONV:dmix, :])
        gate_v = gate_ref[...]
        diff = (x_ref[...] + gate_v * y) - t_ref[...]
        dout = diff * (1.0 / dm)
        dout_ref[...] = dout
        acc_ref[0:8, :] += jnp.sum((dout * y).reshape(ts // 8, 8, dm), axis=0)
        acc_ref[8:16, :] += jnp.sum((diff * diff).reshape(ts // 8, 8, dm), axis=0)
        dy = (dout * gate_v).astype(BF16)
        dy_ref[...] = dy
        dyc_ref[...] = _nt(dy, wo_ref[...]).astype(BF16)

        @pl.when(i == n - 1)
        def _():
            stats_ref[...] = jnp.zeros_like(stats_ref)
            stats_ref[0:1, :] = jnp.sum(acc_ref[0:8, :], axis=0, keepdims=True)
            loss = jnp.sum(acc_ref[8:16, :]) * (0.5 / dm)
            stats_ref[1:2, :] = jnp.full((1, dm), loss, F32)

    row = pl.BlockSpec((ts, dm), lambda i: (i, 0))
    half = pl.BlockSpec((ts, D_CONV), lambda i: (i, 0))
    return pl.pallas_call(
        body, name="outproj_loss", grid=(n,),
        in_specs=[half, half, row, row, pl.BlockSpec((1, dm), lambda i: (0, 0)), ANY],
        out_specs=[row, row, pl.BlockSpec((ts, dmix), lambda i: (i, 0)), pl.BlockSpec((8, dm), lambda i: (0, 0))],
        out_shape=[SDS((seq, dm), F32), SDS((seq, dm), BF16), SDS((seq, dmix), BF16), SDS((8, dm), F32)],
        scratch_shapes=[pltpu.VMEM(w_out.shape, BF16), pltpu.SemaphoreType.DMA(()), pltpu.VMEM((16, dm), F32)],
        compiler_params=_cp(("arbitrary",), 52),
    )(y_conv, y_attn, x, target, gate, w_out)


def _matmul_tn(a, b, name):
    seq, m = a.shape
    n = b.shape[1]
    tm, tn, tk = min(TN_TM, m), min(TN_TN, n), min(TN_TK, seq)
    nk = seq // tk

    def body(a_ref, b_ref, o_ref, acc_ref):
        kk = pl.program_id(2)

        @pl.when(kk == 0)
        def _():
            acc_ref[...] = jnp.zeros_like(acc_ref)

        acc_ref[...] += _tn(a_ref[...], b_ref[...])

        @pl.when(kk == nk - 1)
        def _():
            o_ref[...] = acc_ref[...].astype(BF16)

    return pl.pallas_call(
        body, name=name, grid=(m // tm, n // tn, nk),
        in_specs=[pl.BlockSpec((tk, tm), lambda i, j, kk: (kk, i)), pl.BlockSpec((tk, tn), lambda i, j, kk: (kk, j))],
        out_specs=pl.BlockSpec((tm, tn), lambda i, j, kk: (i, j)), out_shape=SDS((m, n), BF16),
        scratch_shapes=[pltpu.VMEM((tm, tn), F32)],
        compiler_params=_cp(("arbitrary", "arbitrary", "arbitrary"), 40),
    )(a, b)


def _attn_gate_bwd(dycat, o, u):
    seq = o.shape[0]
    ts = min(ROW_T, seq)

    def body(dy_ref, o_ref, z_ref, dot_ref, dz_ref, dl_ref):
        dy = dy_ref[...].astype(F32)
        ov = o_ref[...].astype(F32)
        z = z_ref[...].astype(F32)
        sg = _sigmoid(z)
        do = dy * (z * sg)
        dz_ref[...] = (dy * ov * _silu_grad(z, sg)).astype(BF16)
        prod = do * ov
        ones = jnp.ones((8, V_HEAD), F32)
        for h in range(N_HEADS):
            cols = slice(h * V_HEAD, (h + 1) * V_HEAD)
            dot_ref[h] = do[:, cols].T.astype(BF16)
            rows = lax.dot_general(ones, prod[:, cols], (((1,), (1,)), ((), ())), precision=lax.Precision.HIGHEST,
                                   preferred_element_type=F32)
            dl_ref[h] = rows[0:1, :]

    blk = pl.BlockSpec((ts, D_ATTN), lambda i: (i, 0))
    return pl.pallas_call(
        body, name="attn_gate_bwd", grid=(seq // ts,),
        in_specs=[pl.BlockSpec((ts, D_ATTN), lambda i: (i, 1)), blk, pl.BlockSpec((ts, D_ATTN), lambda i: (i, U_ZA // D_ATTN))],
        out_specs=[pl.BlockSpec((N_HEADS, V_HEAD, ts), lambda i: (0, 0, i)), blk,
                   pl.BlockSpec((N_HEADS, 1, ts), lambda i: (0, 0, i))],
        out_shape=[SDS((N_HEADS, V_HEAD, seq), BF16), SDS((seq, D_ATTN), BF16), SDS((N_HEADS, 1, seq), F32)],
        compiler_params=_cp(("arbitrary",), 40),
    )(dycat, o, u)


def _flash_bwd(q, k, v, do_t, lse, delta):
    seq = q.shape[0]
    t = min(ATT_T, seq)
    n = seq // t

    def body(k_ref, v_ref, q_ref, dot_ref, lse_ref, dl_ref, dq_ref, dk_ref, dv_ref, s_a, s_b, dp_a, dp_b, dvt_ref):
        j = pl.program_id(1)

        @pl.when(j == 0)
        def _():
            dq_ref[...] = jnp.zeros_like(dq_ref)

        def rows(r):
            return pl.ds(pl.multiple_of((j + r) * t, t), t)

        def products(r, s_ref, dp_ref):
            s_ref[...] = _nt(k_ref[...], q_ref[rows(r), :])
            dp_ref[...] = _nn(v_ref[...], dot_ref[:, rows(r)])

        def absorb(r, s_ref, dp_ref, masked):
            p = jnp.exp2((s_ref[...] - lse_ref[:, rows(r)]).astype(BF16))
            if masked:
                p = jnp.where(_causal_mask(t, t, 0), p, jnp.zeros_like(p))
            dvt_ref[...] += _nt(dot_ref[:, rows(r)], p)
            ds = p * (dp_ref[...] - dl_ref[:, rows(r)]).astype(BF16)
            dk_ref[...] += _nn(ds, q_ref[rows(r), :])
            dq_ref[rows(r), :] += _tn(ds, k_ref[...])

        dk_ref[...] = jnp.zeros_like(dk_ref)
        dvt_ref[...] = jnp.zeros_like(dvt_ref)
        products(0, s_a, dp_a)
        last = n - 1 - j

        @pl.when(last == 0)
        def _():
            absorb(0, s_a, dp_a, True)

        @pl.when(last > 0)
        def _():
            products(1, s_b, dp_b)
            absorb(0, s_a, dp_a, True)

        inner = jnp.maximum(last - 1, 0)

        bufs = ((s_b, dp_b), (s_a, dp_a))

        def trip(width):
            def body(pp, carry):
                for w in range(width):
                    r = 1 + width * pp + w
                    products(r + 1, *bufs[(w + 1) % 2])
                    absorb(r, *bufs[w % 2], False)
                return carry
            return body

        quads = inner // ATT_UNROLL
        lax.fori_loop(0, quads, trip(ATT_UNROLL), 0)
        lax.fori_loop(quads * (ATT_UNROLL // 2), inner // 2, trip(2), 0)

        @pl.when((last > 0) & (inner % 2 == 1))
        def _():
            products(last, s_a, dp_a)
            absorb(last - 1, s_b, dp_b, False)
            absorb(last, s_a, dp_a, False)

        @pl.when((last > 0) & (inner % 2 == 0))
        def _():
            absorb(last, s_b, dp_b, False)

        dv_ref[...] = dvt_ref[...].T.astype(BF16)

    row = pl.BlockSpec((None, 1, seq), lambda h, j: (h, 0, 0))
    return pl.pallas_call(
        body, name="flash_bwd", grid=(N_HEADS, n),
        in_specs=[pl.BlockSpec((t, HEAD_PAD), lambda h, j: (j, h)), pl.BlockSpec((t, V_HEAD), lambda h, j: (j, h)),
                  pl.BlockSpec((seq, HEAD_PAD), lambda h, j: (0, h)), pl.BlockSpec((None, V_HEAD, seq), lambda h, j: (h, 0, 0)),
                  row, row],
        out_specs=[pl.BlockSpec((seq, HEAD_PAD), lambda h, j: (0, h)), pl.BlockSpec((t, HEAD_PAD), lambda h, j: (j, h)),
                   pl.BlockSpec((t, V_HEAD), lambda h, j: (j, h))],
        out_shape=[SDS((seq, Q_PAD), F32), SDS((seq, Q_PAD), F32), SDS((seq, D_ATTN), BF16)],
        scratch_shapes=[pltpu.VMEM((t, t), F32)] * 4 + [pltpu.VMEM((V_HEAD, t), F32)],
        compiler_params=_cp(("arbitrary", "arbitrary"), 56),
    )(k, v, q, do_t, lse, delta)


SG_QAG, SG_KVAG, SG_QG, SG_KG, SG_COLS = 0, Q_LORA, Q_LORA + KV_LORA, Q_LORA + KV_LORA + HEAD_PAD, D_MODEL


def _mla_bwd(dq, dk, dv, u, pos, freq, q_a_g, wq, kv_a_g, wkn, wv, q_g, k_g):
    seq = u.shape[0]
    ts = min(ROW_T, seq)
    n = seq // ts
    qscale = 1.0 / math.sqrt(QK_HEAD)

    def body(dq_ref, dk_ref, dv_ref, cq_ref, ckv_ref, kr_ref, pos_ref, freq_ref, qag_ref, wq_ref, kvag_ref, wkn_ref,
             wv_ref, qg_ref, kg_ref, du_ref, dwq_ref, dwkn_ref, dwv_ref, sg_ref, dqp_ref, dkn_ref):
        i = pl.program_id(0)

        @pl.when(i == 0)
        def _():
            dwq_ref[...] = jnp.zeros_like(dwq_ref)
            dwkn_ref[...] = jnp.zeros_like(dwkn_ref)
            dwv_ref[...] = jnp.zeros_like(dwv_ref)
            sg_ref[...] = jnp.zeros_like(sg_ref)

        tables = _rope_tables(pos_ref, freq_ref)

        cq = cq_ref[...].astype(F32)
        cqn, rq = _rms(cq, Q_LORA)
        qag = qag_ref[...]
        cqb = (cqn * qag).astype(BF16)
        qp = _nn(cqb, wq_ref[...])
        qg = qg_ref[...]
        dqg = jnp.zeros((1, HEAD_PAD), F32)
        for h in range(N_HEADS):
            lo = h * HEAD_PAD
            xn, r = _rms(qp[:, lo:lo + HEAD_PAD], QK_HEAD)
            g = jnp.concatenate([dq_ref[:, lo:lo + LANES], _rope_bwd(dq_ref[:, lo + LANES:lo + HEAD_PAD], tables)],
                                axis=-1) * qscale
            dqg = dqg + jnp.sum(g * xn, axis=0, keepdims=True)
            gy = g * qg
            mean = jnp.sum(gy * xn, axis=-1, keepdims=True) * (1.0 / QK_HEAD)
            dqp_ref[:, lo:lo + HEAD_PAD] = (r * (gy - xn * mean)).astype(BF16)
        dqp = dqp_ref[...]
        dwq_ref[...] += _tn(cqb, dqp)
        dcqn = _nt(dqp, wq_ref[...])
        sg_ref[0:1, SG_QAG:SG_QAG + Q_LORA] += jnp.sum(dcqn * cqn, axis=0, keepdims=True)
        sg_ref[0:1, SG_QG:SG_QG + HEAD_PAD] += dqg
        gy = dcqn * qag
        mean = jnp.sum(gy * cqn, axis=-1, keepdims=True) * (1.0 / Q_LORA)
        du_ref[:, 0:Q_LORA] = (rq * (gy - cqn * mean)).astype(BF16)

        ckv = ckv_ref[...].astype(F32)
        ckvn, rkv = _rms(ckv, KV_LORA)
        kvag = kvag_ref[...]
        ckvb = (ckvn * kvag).astype(BF16)
        kn = _nn(ckvb, wkn_ref[...])
        kr = kr_ref[:, 0:LANES].astype(F32)
        ssr = jnp.sum(kr * kr, axis=-1, keepdims=True)
        kg = kg_ref[...]
        kg_n, kg_r = kg[:, :LANES] * LN2, kg[:, LANES:] * LN2
        dkg_n = jnp.zeros((1, LANES), F32)
        dkg_r = jnp.zeros((1, LANES), F32)
        dkr = jnp.zeros((ts, LANES), F32)
        for h in range(N_HEADS):
            knh = kn[:, h * QK_NOPE:(h + 1) * QK_NOPE]
            r = lax.rsqrt((jnp.sum(knh * knh, axis=-1, keepdims=True) + ssr) * (1.0 / QK_HEAD) + EPS)
            xn_n, xn_r = knh * r, kr * r
            lo = h * HEAD_PAD
            g_n = dk_ref[:, lo:lo + LANES]
            g_r = _rope_bwd(dk_ref[:, lo + LANES:lo + HEAD_PAD], tables)
            dkg_n = dkg_n + jnp.sum(g_n * xn_n, axis=0, keepdims=True)
            dkg_r = dkg_r + jnp.sum(g_r * xn_r, axis=0, keepdims=True)
            gy_n, gy_r = g_n * kg_n, g_r * kg_r
            mean = (jnp.sum(gy_n * xn_n, axis=-1, keepdims=True) + jnp.sum(gy_r * xn_r, axis=-1, keepdims=True)) * (1.0 / QK_HEAD)
            dkn_ref[:, h * QK_NOPE:(h + 1) * QK_NOPE] = (r * (gy_n - xn_n * mean)).astype(BF16)
            dkr = dkr + r * (gy_r - xn_r * mean)
        dkn = dkn_ref[...]
        dvv = dv_ref[...]
        dwkn_ref[...] += _tn(ckvb, dkn)
        dwv_ref[...] += _tn(ckvb, dvv)
        dckvn = _nt(dkn, wkn_ref[...]) + _nt(dvv, wv_ref[...])
        sg_ref[0:1, SG_KVAG:SG_KVAG + KV_LORA] += jnp.sum(dckvn * ckvn, axis=0, keepdims=True)
        sg_ref[0:1, SG_KG:SG_KG + LANES] += dkg_n * LN2
        sg_ref[0:1, SG_KG + LANES:SG_KG + HEAD_PAD] += dkg_r * LN2
        gy = dckvn * kvag
        mean = jnp.sum(gy * ckvn, axis=-1, keepdims=True) * (1.0 / KV_LORA)
        du_ref[:, Q_LORA:Q_LORA + KV_LORA] = (rkv * (gy - ckvn * mean)).astype(BF16)
        du_ref[:, Q_LORA + KV_LORA:Q_LORA + KV_LORA + LANES] = dkr.astype(BF16)
        du_ref[:, Q_LORA + KV_LORA + LANES:MLA_COLS] = jnp.zeros((ts, LANES), BF16)

    def full(a):
        return pl.BlockSpec(a.shape, lambda i: (0,) * a.ndim)

    wide = pl.BlockSpec((ts, Q_PAD), lambda i: (i, 0))
    return pl.pallas_call(
        body, name="mla_bwd", grid=(n,),
        in_specs=[wide, wide, pl.BlockSpec((ts, D_ATTN), lambda i: (i, 0)),
                  pl.BlockSpec((ts, Q_LORA), lambda i: (i, U_CQ // Q_LORA)),
                  pl.BlockSpec((ts, KV_LORA), lambda i: (i, U_CKV // KV_LORA)),
                  pl.BlockSpec((ts, KR_PAD), lambda i: (i, U_KR // KR_PAD)),
                  pl.BlockSpec((ts, 1), lambda i: (i, 0)), full(freq), full(q_a_g), full(wq), full(kv_a_g), full(wkn),
                  full(wv), full(q_g), full(k_g)],
        out_specs=[pl.BlockSpec((ts, MLA_COLS), lambda i: (i, 0)), pl.BlockSpec((Q_LORA, Q_PAD), lambda i: (0, 0)),
                   pl.BlockSpec((KV_LORA, D_ATTN), lambda i: (0, 0)), pl.BlockSpec((KV_LORA, D_ATTN), lambda i: (0, 0)),
                   pl.BlockSpec((8, SG_COLS), lambda i: (0, 0))],
        out_shape=[SDS((seq, MLA_COLS), BF16), SDS((Q_LORA, Q_PAD), F32), SDS((KV_LORA, D_ATTN), F32),
                   SDS((KV_LORA, D_ATTN), F32), SDS((8, SG_COLS), F32)],
        scratch_shapes=[pltpu.VMEM((ts, Q_PAD), BF16), pltpu.VMEM((ts, D_ATTN), BF16)],
        compiler_params=_cp(("arbitrary",), 56),
    )(dq, dk, dv, u, u, u, pos, freq, q_a_g, wq, kv_a_g, wkn, wv, q_g, k_g)


def _conv_bwd(dycat, u, conv_w):
    seq = u.shape[0]
    ts = min(ROW_T, seq)
    n = seq // ts
    hb = ts // HALO

    def body(dy_ref, xc_ref, bc_ref, cc_ref, zc_ref, xp_ref, cp_ref, dyn_ref, bn_ref, zn_ref, w_ref,
             du_ref, dw_ref, ext_ref):
        i = pl.program_id(0)

        @pl.when(i == 0)
        def _():
            dw_ref[...] = jnp.zeros_like(dw_ref)

        xc = xc_ref[...].astype(F32)
        cc = cc_ref[...].astype(F32)
        uc = cc * xc
        up = jnp.where(i > 0, cp_ref[...].astype(F32) * xp_ref[...].astype(F32), 0.0)
        u1, u2 = _conv_taps(uc, up, ext_ref, ts, True)
        w0, w1, w2 = w_ref[0:1, :], w_ref[1:2, :], w_ref[2:3, :]
        conv = w0 * u2 + w1 * u1 + w2 * uc
        z = zc_ref[...].astype(F32)
        sg = _sigmoid(z)
        sz = z * sg
        b = bc_ref[...].astype(F32)
        dy = dy_ref[...].astype(F32)
        du_ref[:, 3 * D_CONV:4 * D_CONV] = (dy * (b * conv) * _silu_grad(z, sg)).astype(BF16)
        du_ref[:, D_CONV:2 * D_CONV] = (dy * sz * conv).astype(BF16)
        dconv = dy * sz * b
        dw_ref[0:1, :] += jnp.sum(dconv * u2, axis=0, keepdims=True)
        dw_ref[1:2, :] += jnp.sum(dconv * u1, axis=0, keepdims=True)
        dw_ref[2:3, :] += jnp.sum(dconv * uc, axis=0, keepdims=True)
        zn = zn_ref[...].astype(F32)
        dnext = dyn_ref[...].astype(F32) * (zn * _sigmoid(zn)) * bn_ref[...].astype(F32)
        dnext = jnp.where(i < n - 1, dnext, 0.0)
        d1, d2 = _conv_taps(dconv, dnext, ext_ref, ts, False)
        du = w2 * dconv + w1 * d1 + w0 * d2
        du_ref[:, 2 * D_CONV:3 * D_CONV] = (du * xc).astype(BF16)
        du_ref[:, 0:D_CONV] = (du * cc).astype(BF16)

    def col(cb):
        return pl.BlockSpec((ts, D_CONV), lambda i: (i, cb))

    def prev(cb):
        return pl.BlockSpec((HALO, D_CONV), lambda i: (jnp.maximum(i * hb - 1, 0), cb))

    def nxt(cb):
        return pl.BlockSpec((HALO, D_CONV), lambda i: (jnp.minimum((i + 1) * hb, n * hb - 1), cb))

    return pl.pallas_call(
        body, name="conv_bwd", grid=(n,),
        in_specs=[col(0), col(0), col(1), col(2), col(3), prev(0), prev(2), nxt(0), nxt(1), nxt(3),
                  pl.BlockSpec((3, D_CONV), lambda i: (0, 0))],
        out_specs=[pl.BlockSpec((ts, 4 * D_CONV), lambda i: (i, 0)), pl.BlockSpec((8, D_CONV), lambda i: (0, 0))],
        out_shape=[SDS((seq, 4 * D_CONV), BF16), SDS((8, D_CONV), F32)],
        scratch_shapes=[pltpu.VMEM((ts + HALO, D_CONV), F32)],
        compiler_params=_cp(("arbitrary",), 48),
    )(dycat, u, u, u, u, u, u, dycat, u, u, conv_w)


def _inproj_bwd(du_conv, du_za, du_mla, w_t, parts):
    seq = du_conv.shape[0]
    dm = w_t.shape[1]
    tm, tn = min(DH_TM, seq), DH_TN
    ni, nj = seq // tm, dm // tn
    na = len(parts)

    def body(dc_ref, dz_ref, dm_ref, w_ref, *rest):
        part_refs, o_ref, recv_refs = rest[:na], rest[na], rest[na + 1:2 * na + 1]
        ssem, rsem = rest[2 * na + 1:]
        i, j = pl.program_id(0), pl.program_id(1)
        sends, recvs = _chip_exchange_copies(part_refs, recv_refs, ssem, rsem)

        @pl.when((i == 0) & (j == 0))
        def _():
            for cp in sends:
                cp.start()

        acc = _nn(dc_ref[...], w_ref[0:U_ZA, :])
        acc = acc + _nn(dz_ref[...], w_ref[U_ZA:U_CQ, :])
        acc = acc + _nn(dm_ref[...], w_ref[U_CQ:U_COLS, :])
        o_ref[...] = acc

        @pl.when((i == ni - 1) & (j == nj - 1))
        def _():
            for cp in recvs:
                cp.wait_recv()
            for cp in sends:
                cp.wait_send()

    outs = pl.pallas_call(
        body, name="inproj_bwd", grid=(ni, nj),
        in_specs=[pl.BlockSpec((tm, U_ZA), lambda i, j: (i, 0)), pl.BlockSpec((tm, D_ATTN), lambda i, j: (i, 0)),
                  pl.BlockSpec((tm, MLA_COLS), lambda i, j: (i, 0)), pl.BlockSpec((U_COLS, tn), lambda i, j: (0, j))]
                 + [ANY] * na,
        out_specs=[pl.BlockSpec((tm, tn), lambda i, j: (i, j))] + [ANY] * na,
        out_shape=[SDS((seq, dm), F32)] + [SDS(p.shape, p.dtype) for p in parts],
        scratch_shapes=[pltpu.SemaphoreType.DMA((3 * na,))] * 2,
        compiler_params=_cp(("arbitrary", "arbitrary"), 48),
    )(du_conv, du_za, du_mla, w_t, *parts)
    return outs[0], outs[1:]


def _prenorm_bwd(x, dh, dout, norm_g, scale):
    seq, dm = x.shape
    ts = min(ROW_T, seq)
    n = seq // ts

    def body(x_ref, dh_ref, dout_ref, g_ref, sc_ref, gx_ref, st_ref, acc_ref):
        i = pl.program_id(0)

        @pl.when(i == 0)
        def _():
            acc_ref[...] = jnp.zeros_like(acc_ref)

        xv = x_ref[...]
        xn, r = _rms(xv, dm)
        dh_v = dh_ref[...]
        gv = g_ref[...]
        one_sc = 1.0 + sc_ref[...]

        def fold(a):
            return jnp.sum(a.reshape(ts // 8, 8, dm), axis=0)

        acc_ref[0:8, :] += fold(dh_v)
        acc_ref[8:16, :] += fold(dh_v * (xn * gv))
        dxg = dh_v * one_sc
        acc_ref[16:24, :] += fold(dxg * xn)
        dxn = dxg * gv
        mean = jnp.sum(dxn * xn, axis=-1, keepdims=True) * (1.0 / dm)
        gx_ref[...] = dout_ref[...] + r * (dxn - xn * mean)

        @pl.when(i == n - 1)
        def _():
            st_ref[...] = jnp.zeros_like(st_ref)
            for k in range(3):
                st_ref[k:k + 1, :] = jnp.sum(acc_ref[8 * k:8 * k + 8, :], axis=0, keepdims=True)

    row = pl.BlockSpec((ts, dm), lambda i: (i, 0))
    vec = pl.BlockSpec((1, dm), lambda i: (0, 0))
    return pl.pallas_call(
        body, name="prenorm_bwd", grid=(n,), in_specs=[row, row, row, vec, vec],
        out_specs=[row, pl.BlockSpec((8, dm), lambda i: (0, 0))],
        out_shape=[SDS((seq, dm), F32), SDS((8, dm), F32)],
        scratch_shapes=[pltpu.VMEM((24, dm), F32)], input_output_aliases={2: 0},
        compiler_params=_cp(("arbitrary",), 52),
    )(x, dh, dout, norm_g, scale)


def _unshard_cols(g):
    return jnp.transpose(g, (1, 0, 2)).reshape(g.shape[1], -1)


def _shard_cols(w):
    r = w.shape[0]
    return jnp.transpose(w.reshape(r, N_CHIPS, -1), (1, 0, 2))


W_IN_COLS = 4 * D_CONV + Q_LORA + KV_LORA + QK_ROPE + D_ATTN
SHARD_ROWS = W_IN_COLS // N_CHIPS
SHARD_PAD = 1536


def _w_in_pieces():
    c4 = 4 * D_CONV
    groups = [(0, c4, 0), (c4, c4 + Q_LORA, U_CQ), (c4 + Q_LORA, c4 + Q_LORA + KV_LORA, U_CKV),
              (c4 + Q_LORA + KV_LORA, W_IN_COLS - D_ATTN, U_KR), (W_IN_COLS - D_ATTN, W_IN_COLS, U_ZA)]
    pieces = []
    for lo, hi, my in groups:
        for chip in range(N_CHIPS):
            a, b = max(lo, chip * SHARD_ROWS), min(hi, (chip + 1) * SHARD_ROWS)
            if a < b:
                pieces.append((chip, a - chip * SHARD_ROWS, b - a, my + a - lo))
    return pieces


def _w_t_to_my(g):
    w = jnp.zeros((U_COLS, g.shape[2]), g.dtype)
    for chip, row, n, my in _w_in_pieces():
        w = lax.dynamic_update_slice(w, g[chip, row:row + n], (my, 0))
    return w


def _w_t_from_my(g_conv, g_za, g_mla):
    w = jnp.zeros((N_CHIPS, SHARD_PAD, g_conv.shape[1]), g_conv.dtype)
    for chip, row, n, my in _w_in_pieces():
        src, base = (g_conv, 0) if my < U_ZA else (g_za, U_ZA) if my < U_CQ else (g_mla, U_CQ)
        w = lax.dynamic_update_slice(w, src[my - base:my - base + n][None], (chip, row, 0))
    return w


def _heads_pad(w):
    r = w.shape[0]
    w3 = w.reshape(r, N_HEADS, QK_HEAD)
    return jnp.pad(w3, ((0, 0), (0, 0), (0, HEAD_PAD - QK_HEAD))).reshape(r, Q_PAD)


def _heads_unpad(w):
    r = w.shape[0]
    return w.reshape(r, N_HEADS, HEAD_PAD)[:, :, :QK_HEAD].reshape(r, N_HEADS * QK_HEAD)


def kernel(x, c, positions, ada_w, ada_b, norm_g, w_in, conv_w, q_a_g, w_q_b, kv_a_g, w_kv_b, q_g, k_g, w_out, loss_target, m_ada_w, m_ada_b, m_norm_g, m_w_in, m_conv_w, m_q_a_g, m_w_q_b, m_kv_a_g, m_w_kv_b, m_q_g, m_k_g, m_w_out, v_ada_w, v_ada_b, v_norm_g, v_w_in, v_conv_w, v_q_a_g, v_w_q_b, v_kv_a_g, v_w_kv_b, v_q_g, v_k_g, v_w_out):
    mx, my, mc = _place()
    chip = 2 * mx + my
    me = 2 * chip + mc
    seq = x.shape[1]
    x2, t2 = x[0], loss_target[0]
    cw_cols = conv_w.shape[2]

    small = jnp.zeros((8, D_MODEL), F32)
    small = small.at[0].set(c[0])
    small = small.at[1:4, :cw_cols].set(conv_w[0])
    small_all = _gather8(small, "gather_c_conv", False)[0]
    c_all = small_all[:, 0, :]
    conv_full = jnp.transpose(small_all.reshape(N_CHIPS, 2, 8, D_MODEL)[:, 0, 1:4, :cw_cols], (1, 0, 2)).reshape(3, D_CONV)

    ada_cols = ada_w.shape[2]
    b_k = lax.dynamic_slice(ada_b, (0, chip * ada_cols), (1, ada_cols))
    mod_k, sc_all = _ada_mod(c_all, ada_w[0], b_k)
    mod_all = _gather8(mod_k, "gather_mod", False)[0]
    mod_row = lax.dynamic_slice(mod_all.reshape(N_CHIPS, 2, N_DEV, ada_cols), (0, mc, me, 0), (N_CHIPS, 1, 1, ada_cols))
    mod_row = mod_row.reshape(3, D_MODEL)
    shift, scale, gate = mod_row[0:1], mod_row[1:2], mod_row[2:3]

    def own_slot(g, s):
        return lax.dynamic_update_slice(g, s[None], (chip, 0, 0))

    w_in_t, m_w_in_t, v_w_in_t = [jnp.transpose(a[0]) for a in (w_in, m_w_in, v_w_in)]
    shard_in = jnp.pad(w_in_t.astype(BF16), ((0, SHARD_PAD - SHARD_ROWS), (0, 0)))
    g_in = own_slot(_allgather_shards([shard_in])[0], shard_in)
    w_t = _w_t_to_my(g_in)

    later = [w_q_b[0].astype(BF16), w_kv_b[0].astype(BF16), w_out[0].astype(BF16)]
    h, u, got = _inproj(x2, norm_g, scale, shift, w_t, later)
    g_q, g_kv, g_out = [own_slot(g, s) for g, s in zip(got, later)]
    wq = _heads_pad(_unshard_cols(g_q))
    wkv = _unshard_cols(g_kv).reshape(KV_LORA, N_HEADS, QK_NOPE + V_HEAD)
    wkn = wkv[:, :, :QK_NOPE].reshape(KV_LORA, N_HEADS * QK_NOPE)
    wv = wkv[:, :, QK_NOPE:].reshape(KV_LORA, D_ATTN)
    wo = g_out.reshape(N_CHIPS * g_out.shape[1], D_MODEL)
    y_conv = _conv_fwd(u, conv_full)
    pos = positions.reshape(seq, 1)
    inv_freq = ROPE_BASE ** (-jnp.arange(0, QK_ROPE, 2, dtype=F32) / QK_ROPE)
    freq = jnp.concatenate([inv_freq, inv_freq, jnp.zeros((LANES - QK_ROPE,), F32)]).reshape(1, LANES)
    q_g_pad = jnp.pad(q_g, ((0, 0), (0, HEAD_PAD - QK_HEAD)))
    k_g_pad = jnp.pad(k_g, ((0, 0), (0, HEAD_PAD - QK_HEAD)))
    q, k, v = _mla_prep(u, pos, freq, q_a_g, wq, kv_a_g, wkn, wv, q_g_pad, k_g_pad)
    o, y_attn, lse = _flash_fwd(q, k, v, u)
    dout, dy, dycat, st_out = _outproj_loss(y_conv, y_attn, x2, t2, gate, wo)

    dw_out = jnp.concatenate([_matmul_tn(y_conv, dy, "dw_out_conv"), _matmul_tn(y_attn, dy, "dw_out_attn")], axis=0)
    do_t, du_za, delta = _attn_gate_bwd(dycat, o, u)
    dq, dk, dv = _flash_bwd(q, k, v, do_t, lse, delta)
    du_mla, dwq, dwkn, dwv, sg_mla = _mla_bwd(dq, dk, dv, u, pos, freq, q_a_g, wq, kv_a_g, wkn, wv, q_g_pad, k_g_pad)
    du_conv, dconv_w = _conv_bwd(dycat, u, conv_full)
    dw_conv = _matmul_tn(du_conv, h, "dw_in_conv")
    dw_za = _matmul_tn(du_za, h, "dw_in_za")
    dw_mla = _matmul_tn(du_mla, h, "dw_in_mla")

    dw_q_nat = _heads_unpad(dwq).astype(BF16)
    dw_kv_nat = jnp.concatenate([dwkn.reshape(KV_LORA, N_HEADS, QK_NOPE), dwv.reshape(KV_LORA, N_HEADS, V_HEAD)],
                                axis=2).reshape(KV_LORA, N_HEADS * (QK_NOPE + V_HEAD)).astype(BF16)
    grads = [_w_t_from_my(dw_conv, dw_za, dw_mla), _shard_cols(dw_q_nat), _shard_cols(dw_kv_nat),
             dw_out.reshape(N_CHIPS, dw_out.shape[0] // N_CHIPS, D_MODEL)]
    theirs = _rs_core_swap(grads)
    names = ["w_in", "w_q_b", "w_kv_b", "w_out"]
    core = jnp.reshape(mc, (1,)).astype(jnp.int32)
    parts = [_add_half_bf16(g, b, core, "rs_add_" + nm) for g, b, nm in zip(grads, theirs, names)]
    dh, recv = _inproj_bwd(du_conv, du_za, du_mla, w_t, parts)
    recv = [lax.dynamic_update_slice(r, lax.dynamic_slice(p, (chip, 0, 0), (1,) + p.shape[1:]), (chip, 0, 0))
            for r, p in zip(recv, parts)]
    halves = [_sum_chips(p, "rs_sum_" + nm) for p, nm in zip(recv, names)]
    joined = _rs_core_join(halves)
    joined = [lax.dynamic_update_slice(j, hf[None], (mc, 0, 0)) for j, hf in zip(joined, halves)]
    g_big = [j.reshape(2 * j.shape[1], j.shape[2]) for j in joined]
    grad_x, st_in = _prenorm_bwd(x2, dh, dout, norm_g, scale)

    sgrad = jnp.zeros((8, D_MODEL), F32)
    sgrad = sgrad.at[0:2].set(st_in[0:2])
    sgrad = sgrad.at[2].set(st_out[0])
    sgrad = sgrad.at[3].set(st_in[2])
    sgrad = sgrad.at[4, :D_CONV].set(dconv_w[0]).at[4, D_CONV:].set(dconv_w[1])
    sgrad = sgrad.at[5, :D_CONV].set(dconv_w[2]).at[5, D_CONV:].set(sg_mla[0, :D_CONV])
    sgrad = sgrad.at[6, :HEAD_PAD].set(sg_mla[0, SG_KG:SG_KG + HEAD_PAD])
    sgrad = sgrad.at[7].set(st_out[1])
    sg_all, sg_sum = _gather8(sgrad, "gather_small_grads", True)
    loss = sg_sum[7, 0]
    g_ada_b = sg_sum[0:3].reshape(1, 3 * D_MODEL)
    g_norm_g = sg_sum[3:4]
    conv_sum = jnp.stack([sg_sum[4, :D_CONV], sg_sum[4, D_CONV:], sg_sum[5, :D_CONV]])
    g_conv_w = lax.dynamic_slice(conv_sum, (0, chip * cw_cols), (3, cw_cols))
    g_q_a_g = sg_sum[5:6, D_CONV + SG_QAG:D_CONV + SG_QAG + Q_LORA]
    g_kv_a_g = sg_sum[5:6, D_CONV + SG_KVAG:D_CONV + SG_KVAG + KV_LORA]
    g_q_g = sg_sum[5:6, D_CONV + SG_QG:D_CONV + SG_QG + QK_HEAD]
    g_k_g = sg_sum[6:7, :QK_HEAD]
    dmod_k = lax.dynamic_slice(sg_all[:, 0:3, :].reshape(N_DEV, 3 * D_MODEL), (0, chip * ada_cols), (N_DEV, ada_cols))

    g_ada_w, d_ada_w, nm_ada_w, nv_ada_w = _ada_w_update(sc_all, dmod_k, ada_w[0], m_ada_w[0], v_ada_w[0])
    upd = {}
    big = {"w_q_b": (w_q_b, m_w_q_b, v_w_q_b), "w_kv_b": (w_kv_b, m_w_kv_b, v_w_kv_b), "w_out": (w_out, m_w_out, v_w_out)}
    for nm, g in zip(names[1:], g_big[1:]):
        w_, m_, v_ = big[nm]
        upd[nm] = (g,) + tuple(_adamw(w_[0], g, m_[0], v_[0], "adamw_" + nm))
    d_t, nm_t, nv_t, g_t = _adamw(w_in_t, g_big[0], m_w_in_t, v_w_in_t, "adamw_w_in", echo_g=True)
    upd["w_in"] = tuple(jnp.transpose(a) for a in (g_t, d_t, nm_t, nv_t))
    small_w = {"ada_b": (ada_b, m_ada_b, v_ada_b, g_ada_b), "norm_g": (norm_g, m_norm_g, v_norm_g, g_norm_g),
               "conv_w": (conv_w[0], m_conv_w[0], v_conv_w[0], g_conv_w), "q_a_g": (q_a_g, m_q_a_g, v_q_a_g, g_q_a_g),
               "kv_a_g": (kv_a_g, m_kv_a_g, v_kv_a_g, g_kv_a_g), "q_g": (q_g, m_q_g, v_q_g, g_q_g),
               "k_g": (k_g, m_k_g, v_k_g, g_k_g)}
    for nm, (w_, m_, v_, g) in small_w.items():
        upd[nm] = (g,) + tuple(_adamw(w_, g, m_, v_, "adamw_" + nm))
    upd["ada_w"] = (g_ada_w, d_ada_w, nm_ada_w, nv_ada_w)

    order = ["ada_w", "ada_b", "norm_g", "w_in", "conv_w", "q_a_g", "w_q_b", "kv_a_g", "w_kv_b", "q_g", "k_g", "w_out"]
    lead1 = {"ada_w", "w_in", "conv_w", "w_q_b", "w_kv_b", "w_out"}

    def shaped(nm, a):
        return a[None] if nm in lead1 else a

    outs = [loss, grad_x[None]]
    for idx in range(4):
        outs += [shaped(nm, upd[nm][idx]) for nm in order]
    return tuple(outs)
```

```python
import functools
import math

import jax
import jax.numpy as jnp
from jax import lax
from jax.experimental import pallas as pl
from jax.experimental.pallas import tpu as pltpu

F32 = jnp.float32
BF16 = jnp.bfloat16
MESH = pl.DeviceIdType.MESH
SDS = jax.ShapeDtypeStruct
ANY = pl.BlockSpec(memory_space=pl.ANY)

D_MODEL = 2048
D_CONV = 1024
N_HEADS = 8
QK_NOPE = 128
QK_ROPE = 64
QK_HEAD = QK_NOPE + QK_ROPE
V_HEAD = 128
D_ATTN = N_HEADS * V_HEAD
Q_LORA = 512
KV_LORA = 256
ROPE_BASE = 10000.0
EPS = 1e-6
LOG2E = math.log2(math.e)
LN2 = math.log(2.0)
ADAM_LR, ADAM_B1, ADAM_B2, ADAM_EPS, ADAM_WD, ADAM_STEP = 0.001, 0.9, 0.999, 1e-08, 0.01, 10
N_CHIPS = 4
N_DEV = 8

LANES = 128
V7X_VMEM_BYTES = 64 * 1024 * 1024
MIB = 1024 * 1024

HEAD_PAD = 256
Q_PAD = N_HEADS * HEAD_PAD
U_ZA = 4 * D_CONV
U_CQ = U_ZA + D_ATTN
U_CKV = U_CQ + Q_LORA
U_KR = U_CKV + KV_LORA
KR_PAD = 256
U_COLS = U_KR + KR_PAD
MLA_COLS = Q_LORA + KV_LORA + KR_PAD

ATT_T = 512
INPROJ_TM, INPROJ_TN = 1024, 512
ROW_T = 512
OUT_T = 256
DH_TM, DH_TN = 512, 512
TN_TM, TN_TN, TN_TK = 1024, 1024, 2048
ATT_UNROLL = 4


def _cp(sem=None, vmem_mib=None, **kw):
    if sem is not None:
        kw["dimension_semantics"] = sem
    if vmem_mib is not None:
        kw["vmem_limit_bytes"] = min(vmem_mib * MIB, V7X_VMEM_BYTES - 4 * MIB)
    return pltpu.CompilerParams(**kw)


def _sigmoid(z):
    return 1.0 / (1.0 + jnp.exp(-z))


def _silu_grad(z, sg):
    return sg * (1.0 + z * (1.0 - sg))


def _nt(a, b):
    return lax.dot_general(a, b, (((1,), (1,)), ((), ())), preferred_element_type=F32)


def _tn(a, b):
    return lax.dot_general(a, b, (((0,), (0,)), ((), ())), preferred_element_type=F32)


def _nn(a, b):
    return jnp.dot(a, b, preferred_element_type=F32)


def _place():
    return lax.axis_index("x"), lax.axis_index("y"), lax.axis_index("c")


def _gather8(v, name, with_sum):
    rows, cols = v.shape

    def body(v_ref, out_ref, *rest):
        if with_sum:
            sum_ref, send_sems, recv_sems = rest
        else:
            send_sems, recv_sems = rest
        mx, my, mc = _place()
        me = 4 * mx + 2 * my + mc
        out_ref[me] = v_ref[...]
        peers = []
        for d in range(1, N_DEV):
            px = 1 - mx if d & 4 else mx
            py = 1 - my if d & 2 else my
            pc = 1 - mc if d & 1 else mc
            peers.append((px, py, pc))

        def copy(d, slot, to):
            return pltpu.make_async_remote_copy(
                src_ref=v_ref, dst_ref=out_ref.at[slot], send_sem=send_sems.at[d], recv_sem=recv_sems.at[d],
                device_id=to, device_id_type=MESH)

        sends = [copy(d, me, p) for d, p in enumerate(peers)]
        for cp in sends:
            cp.start()
        for d, (px, py, pc) in enumerate(peers):
            copy(d, 4 * px + 2 * py + pc, (px, py, pc)).wait_recv()
        for cp in sends:
            cp.wait_send()
        if with_sum:
            acc = out_ref[0]
            for b in range(1, N_DEV):
                acc = acc + out_ref[b]
            sum_ref[...] = acc

    out_shape = [SDS((N_DEV, rows, cols), F32)]
    if with_sum:
        out_shape.append(SDS((rows, cols), F32))
    vm = pl.BlockSpec(memory_space=pltpu.VMEM)
    return pl.pallas_call(
        body, name=name, out_shape=out_shape, in_specs=[vm], out_specs=[vm] * len(out_shape),
        scratch_shapes=[pltpu.SemaphoreType.DMA((N_DEV - 1,)), pltpu.SemaphoreType.DMA((N_DEV - 1,))],
    )(v)


def _chips_of(mx, my):
    chips = [(mx, 1 - my), (1 - mx, my), (1 - mx, 1 - my)]
    return chips, [2 * px + py for px, py in chips]


def _allgather_shards(shards):
    na = len(shards)
    halves = [s.shape[0] // 2 for s in shards]

    def body(*refs):
        ins, outs = refs[:na], refs[na:2 * na]
        s1, r1, s2, r2 = refs[2 * na:]
        mx, my, mc = _place()
        k = 2 * mx + my
        sib = (mx, my, 1 - mc)
        chips, kks = _chips_of(mx, my)

        def half(a, slot, c):
            return outs[a].at[slot, pl.ds(c * halves[a], halves[a])]

        def mine(a):
            return ins[a].at[pl.ds(mc * halves[a], halves[a])]

        sends = []
        for a in range(na):
            for d, (px, py) in enumerate(chips):
                cp = pltpu.make_async_remote_copy(
                    src_ref=mine(a), dst_ref=half(a, k, mc), send_sem=s1.at[3 * a + d], recv_sem=r1.at[3 * a + d],
                    device_id=(px, py, mc), device_id_type=MESH)
                cp.start()
                sends.append(cp)
        for a in range(na):
            for d, (px, py) in enumerate(chips):
                pltpu.make_async_remote_copy(
                    src_ref=mine(a), dst_ref=half(a, kks[d], mc), send_sem=s1.at[3 * a + d], recv_sem=r1.at[3 * a + d],
                    device_id=(px, py, mc), device_id_type=MESH).wait_recv()
                cp = pltpu.make_async_remote_copy(
                    src_ref=half(a, kks[d], mc), dst_ref=half(a, kks[d], mc), send_sem=s2.at[3 * a + d],
                    recv_sem=r2.at[3 * a + d], device_id=sib, device_id_type=MESH)
                cp.start()
                sends.append(cp)
        for a in range(na):
            for d in range(3):
                pltpu.make_async_remote_copy(
                    src_ref=half(a, kks[d], 1 - mc), dst_ref=half(a, kks[d], 1 - mc), send_sem=s2.at[3 * a + d],
                    recv_sem=r2.at[3 * a + d], device_id=sib, device_id_type=MESH).wait_recv()
        for cp in sends:
            cp.wait_send()

    return pl.pallas_call(
        body, name="allgather_weights",
        out_shape=[SDS((N_CHIPS,) + s.shape, s.dtype) for s in shards],
        in_specs=[ANY] * na, out_specs=[ANY] * na,
        scratch_shapes=[pltpu.SemaphoreType.DMA((3 * na,))] * 4,
    )(*shards)


def _rs_core_swap(grads):
    na = len(grads)
    halves = [g.shape[1] // 2 for g in grads]

    def body(*refs):
        ins, outs = refs[:na], refs[na:2 * na]
        ssem, rsem = refs[2 * na:]
        mx, my, mc = _place()
        sib = (mx, my, 1 - mc)
        sends = []
        for a in range(na):
            cp = pltpu.make_async_remote_copy(
                src_ref=ins[a].at[:, pl.ds((1 - mc) * halves[a], halves[a])], dst_ref=outs[a],
                send_sem=ssem.at[a], recv_sem=rsem.at[a], device_id=sib, device_id_type=MESH)
            cp.start()
            sends.append(cp)
        for cp in sends:
            cp.wait_recv()
        for cp in sends:
            cp.wait_send()

    return pl.pallas_call(
        body, name="rs_core_swap", out_shape=[SDS((N_CHIPS, h) + g.shape[2:], g.dtype) for g, h in zip(grads, halves)],
        in_specs=[ANY] * na, out_specs=[ANY] * na,
        scratch_shapes=[pltpu.SemaphoreType.DMA((na,))] * 2,
    )(*grads)


def _chip_exchange_copies(ins, outs, ssem, rsem):
    mx, my, mc = _place()
    k = 2 * mx + my
    chips, kks = _chips_of(mx, my)
    sends, recvs = [], []
    for a in range(len(ins)):
        for d, (px, py) in enumerate(chips):
            def copy(dst_slot):
                return pltpu.make_async_remote_copy(
                    src_ref=ins[a].at[kks[d]], dst_ref=outs[a].at[dst_slot], send_sem=ssem.at[3 * a + d],
                    recv_sem=rsem.at[3 * a + d], device_id=(px, py, mc), device_id_type=MESH)
            sends.append(copy(k))
            recvs.append(copy(kks[d]))
    return sends, recvs


def _rs_core_join(halves):
    na = len(halves)

    def body(*refs):
        ins, outs = refs[:na], refs[na:2 * na]
        ssem, rsem = refs[2 * na:]
        mx, my, mc = _place()
        sib = (mx, my, 1 - mc)
        sends = []
        for a in range(na):
            cp = pltpu.make_async_remote_copy(
                src_ref=ins[a], dst_ref=outs[a].at[mc], send_sem=ssem.at[a], recv_sem=rsem.at[a],
                device_id=sib, device_id_type=MESH)
            cp.start()
            sends.append(cp)
        for a in range(na):
            pltpu.make_async_remote_copy(
                src_ref=ins[a], dst_ref=outs[a].at[1 - mc], send_sem=ssem.at[a], recv_sem=rsem.at[a],
                device_id=sib, device_id_type=MESH).wait_recv()
        for cp in sends:
            cp.wait_send()

    return pl.pallas_call(
        body, name="rs_core_join", out_shape=[SDS((2,) + h.shape, h.dtype) for h in halves],
        in_specs=[ANY] * na, out_specs=[ANY] * na,
        scratch_shapes=[pltpu.SemaphoreType.DMA((na,))] * 2,
    )(*halves)


def _row_tile(rows, limit, mult=16):
    if rows <= limit:
        return rows
    best = None
    for t in range(mult, limit + 1, mult):
        if rows % t == 0:
            best = t
    assert best is not None, rows
    return best


def _add_half_bf16(g, b, core, name):
    _, h, cols = b.shape
    tb = _row_tile(h, 512)
    nb = h // tb

    def body(core_ref, g_ref, b_ref, o_ref):
        o_ref[...] = (g_ref[...].astype(F32) + b_ref[...].astype(F32)).astype(BF16)

    spec = pl.BlockSpec((None, tb, cols), lambda kk, i, core_ref: (kk, i, 0))
    return pl.pallas_call(
        body, name=name,
        grid_spec=pltpu.PrefetchScalarGridSpec(
            num_scalar_prefetch=1, grid=(N_CHIPS, nb),
            in_specs=[pl.BlockSpec((None, tb, cols), lambda kk, i, core_ref: (kk, core_ref[0] * nb + i, 0)), spec],
            out_specs=spec),
        out_shape=SDS(b.shape, BF16), compiler_params=_cp(("arbitrary", "arbitrary")),
    )(core, g, b)


def _sum_chips(p, name):
    _, rows, cols = p.shape
    tb = _row_tile(rows, 256)

    def body(p_ref, o_ref):
        acc = p_ref[0].astype(F32)
        for j in range(1, N_CHIPS):
            acc = acc + p_ref[j].astype(F32)
        o_ref[...] = acc

    return pl.pallas_call(
        body, name=name, grid=(rows // tb,),
        in_specs=[pl.BlockSpec((N_CHIPS, tb, cols), lambda i: (0, i, 0))],
        out_specs=pl.BlockSpec((tb, cols), lambda i: (i, 0)), out_shape=SDS((rows, cols), F32),
        compiler_params=_cp(("arbitrary",)),
    )(p)


def _adamw_math(w, g, m, v):
    m2 = ADAM_B1 * m + (1.0 - ADAM_B1) * g
    v2 = ADAM_B2 * v + (1.0 - ADAM_B2) * (g * g)
    m_hat = m2 / (1.0 - ADAM_B1 ** ADAM_STEP)
    v_hat = v2 / (1.0 - ADAM_B2 ** ADAM_STEP)
    delta = -ADAM_LR * (m_hat / (jnp.sqrt(v_hat) + ADAM_EPS) + ADAM_WD * w)
    return delta, m2, v2


def _adamw(w, g, m, v, name, echo_g=False):
    rows, cols = w.shape
    tb = _row_tile(rows, 256, mult=8)
    nout = 4 if echo_g else 3

    def body(w_ref, g_ref, m_ref, v_ref, d_ref, m2_ref, v2_ref, *echo):
        gv = g_ref[...]
        d, m2, v2 = _adamw_math(w_ref[...], gv, m_ref[...], v_ref[...])
        d_ref[...] = d
        m2_ref[...] = m2
        v2_ref[...] = v2
        if echo_g:
            echo[0][...] = gv

    spec = pl.BlockSpec((tb, cols), lambda i: (i, 0))
    return pl.pallas_call(
        body, name=name, grid=(rows // tb,), in_specs=[spec] * 4, out_specs=[spec] * nout,
        out_shape=[SDS((rows, cols), F32)] * nout, compiler_params=_cp(("arbitrary",), 40),
    )(w, g, m, v)


def _ada_w_update(sc_all, dmod_k, w, m, v):
    rows, cols = w.shape
    tb = 256

    def body(s_ref, dm_ref, w_ref, m_ref, v_ref, g_ref, d_ref, m2_ref, v2_ref):
        g = _tn(s_ref[...].astype(BF16), dm_ref[...].astype(BF16))
        d, m2, v2 = _adamw_math(w_ref[...], g, m_ref[...], v_ref[...])
        g_ref[...] = g
        d_ref[...] = d
        m2_ref[...] = m2
        v2_ref[...] = v2

    spec = pl.BlockSpec((tb, cols), lambda i: (i, 0))
    return pl.pallas_call(
        body, name="ada_w_update", grid=(rows // tb,),
        in_specs=[pl.BlockSpec((N_DEV, tb), lambda i: (0, i)), pl.BlockSpec((N_DEV, cols), lambda i: (0, 0)), spec, spec, spec],
        out_specs=[spec] * 4, out_shape=[SDS((rows, cols), F32)] * 4, compiler_params=_cp(("arbitrary",), 40),
    )(sc_all, dmod_k, w, m, v)


def _ada_mod(c_all, w, b_k):
    rows, cols = w.shape
    tn = 512

    def body(c_ref, w_ref, b_ref, o_ref, s_ref):
        cv = c_ref[...]
        s = cv * _sigmoid(cv)
        s_ref[...] = s
        o_ref[...] = _nn(s.astype(BF16), w_ref[...].astype(BF16)) + b_ref[...]

    return pl.pallas_call(
        body, name="ada_mod", grid=(cols // tn,),
        in_specs=[pl.BlockSpec((N_DEV, rows), lambda j: (0, 0)), pl.BlockSpec((rows, tn), lambda j: (0, j)),
                  pl.BlockSpec((1, tn), lambda j: (0, j))],
        out_specs=[pl.BlockSpec((N_DEV, tn), lambda j: (0, j)), pl.BlockSpec((N_DEV, rows), lambda j: (0, 0))],
        out_shape=[SDS((N_DEV, cols), F32), SDS((N_DEV, rows), F32)], compiler_params=_cp(("arbitrary",)),
    )(c_all, w, b_k)


def _inproj(x, norm_g, scale, shift, w_t, shards):
    seq, dm = x.shape
    ncols = w_t.shape[0]
    tm, tn = min(INPROJ_TM, seq), INPROJ_TN
    ni, nj = seq // tm, ncols // tn
    na = len(shards)

    def body(x_ref, g_ref, sc_ref, sh_ref, w_ref, *rest):
        shard_refs, h_ref, u_ref, got_refs = rest[:na], rest[na], rest[na + 1], rest[na + 2:2 * na + 2]
        ssem, rsem = rest[2 * na + 2:]
        i, j = pl.program_id(0), pl.program_id(1)
        mx, my, mc = _place()
        k = 2 * mx + my
        chips, kks = _chips_of(mx, my)

        def copy(a, d, slot):
            return pltpu.make_async_remote_copy(
                src_ref=shard_refs[a], dst_ref=got_refs[a].at[slot], send_sem=ssem.at[3 * a + d],
                recv_sem=rsem.at[3 * a + d], device_id=(chips[d][0], chips[d][1], mc), device_id_type=MESH)

        @pl.when((i == 0) & (j == 0))
        def _():
            for a in range(na):
                for d in range(3):
                    copy(a, d, k).start()

        @pl.when(j == 0)
        def _():
            xv = x_ref[...]
            r = lax.rsqrt(jnp.mean(xv * xv, axis=-1, keepdims=True) + EPS)
            hv = (xv * r * g_ref[...]) * (1.0 + sc_ref[...]) + sh_ref[...]
            h_ref[...] = hv.astype(BF16)

        u_ref[...] = _nt(h_ref[...], w_ref[...]).astype(BF16)

        @pl.when((i == ni - 1) & (j == nj - 1))
        def _():
            for a in range(na):
                for d in range(3):
                    copy(a, d, kks[d]).wait_recv()
            for a in range(na):
                for d in range(3):
                    copy(a, d, k).wait_send()

    vec = pl.BlockSpec((1, dm), lambda i, j: (0, 0))
    outs = pl.pallas_call(
        body, name="inproj", grid=(ni, nj),
        in_specs=[pl.BlockSpec((tm, dm), lambda i, j: (i, 0)), vec, vec, vec, pl.BlockSpec((tn, dm), lambda i, j: (j, 0))]
                 + [ANY] * na,
        out_specs=[pl.BlockSpec((tm, dm), lambda i, j: (i, 0)), pl.BlockSpec((tm, tn), lambda i, j: (i, j))] + [ANY] * na,
        out_shape=[SDS((seq, dm), BF16), SDS((seq, ncols), BF16)] + [SDS((N_CHIPS,) + s.shape, s.dtype) for s in shards],
        scratch_shapes=[pltpu.SemaphoreType.DMA((3 * na,))] * 2,
        compiler_params=_cp(("arbitrary", "arbitrary"), 48),
    )(x, norm_g, scale, shift, w_t, *shards)
    return outs[0], outs[1], outs[2:]


HALO = 16


def _conv_taps(uc, halo, ext_ref, ts, causal):
    if causal:
        ext_ref[0:HALO, :] = halo
        ext_ref[HALO:HALO + ts, :] = uc
        return ext_ref[pl.ds(HALO - 1, ts), :], ext_ref[pl.ds(HALO - 2, ts), :]
    ext_ref[0:ts, :] = uc
    ext_ref[ts:ts + HALO, :] = halo
    return ext_ref[pl.ds(1, ts), :], ext_ref[pl.ds(2, ts), :]


def _conv_fwd(u, conv_w):
    seq = u.shape[0]
    ts = min(ROW_T, seq)
    hb = ts // HALO

    def body(xc_ref, bc_ref, cc_ref, zc_ref, xp_ref, cp_ref, w_ref, y_ref, ext_ref):
        i = pl.program_id(0)
        uc = cc_ref[...].astype(F32) * xc_ref[...].astype(F32)
        up = cp_ref[...].astype(F32) * xp_ref[...].astype(F32)
        up = jnp.where(i > 0, up, 0.0)
        u1, u2 = _conv_taps(uc, up, ext_ref, ts, True)
        conv = w_ref[0:1, :] * u2 + w_ref[1:2, :] * u1 + w_ref[2:3, :] * uc
        z = zc_ref[...].astype(F32)
        y_ref[...] = ((bc_ref[...].astype(F32) * conv) * (z * _sigmoid(z))).astype(BF16)

    def col(cb):
        return pl.BlockSpec((ts, D_CONV), lambda i: (i, cb))

    def prev(cb):
        return pl.BlockSpec((HALO, D_CONV), lambda i: (jnp.maximum(i * hb - 1, 0), cb))

    return pl.pallas_call(
        body, name="conv_fwd", grid=(seq // ts,),
        in_specs=[col(0), col(1), col(2), col(3), prev(0), prev(2), pl.BlockSpec((3, D_CONV), lambda i: (0, 0))],
        out_specs=pl.BlockSpec((ts, D_CONV), lambda i: (i, 0)), out_shape=SDS((seq, D_CONV), BF16),
        scratch_shapes=[pltpu.VMEM((ts + HALO, D_CONV), F32)],
        compiler_params=_cp(("arbitrary",), 40),
    )(u, u, u, u, u, u, conv_w)


def _rope_tables(pos_ref, freq_ref):
    ang = pos_ref[...].astype(F32) * freq_ref[...]
    lane = lax.broadcasted_iota(jnp.int32, ang.shape, 1)
    cs, sn = jnp.cos(ang), jnp.sin(ang)
    half = QK_ROPE // 2
    cos_t = jnp.where(lane < QK_ROPE, cs, 0.0)
    sin_lo = jnp.where(lane < half, sn, 0.0)
    sin_hi = jnp.where((lane >= half) & (lane < QK_ROPE), sn, 0.0)
    return cos_t, sin_lo, sin_hi


def _rope(blk, tables):
    cos_t, sin_lo, sin_hi = tables
    half = QK_ROPE // 2
    return blk * cos_t - pltpu.roll(blk, LANES - half, 1) * sin_lo + pltpu.roll(blk, half, 1) * sin_hi


def _rope_bwd(g, tables):
    cos_t, sin_lo, sin_hi = tables
    half = QK_ROPE // 2
    return g * cos_t + pltpu.roll(g, LANES - half, 1) * sin_lo - pltpu.roll(g, half, 1) * sin_hi


def _rms(v, n):
    r = lax.rsqrt(jnp.sum(v * v, axis=-1, keepdims=True) * (1.0 / n) + EPS)
    return v * r, r


def _mla_prep(u, pos, freq, q_a_g, wq, kv_a_g, wkn, wv, q_g, k_g):
    seq = u.shape[0]
    ts = min(ROW_T, seq)
    qscale = LOG2E / math.sqrt(QK_HEAD)

    def body(cq_ref, ckv_ref, kr_ref, pos_ref, freq_ref, qag_ref, wq_ref, kvag_ref, wkn_ref, wv_ref, qg_ref, kg_ref,
             q_ref, k_ref, v_ref):
        tables = _rope_tables(pos_ref, freq_ref)
        cqn, _ = _rms(cq_ref[...].astype(F32), Q_LORA)
        qp = _nn((cqn * qag_ref[...]).astype(BF16), wq_ref[...])
        qg = qg_ref[...]
        for h in range(N_HEADS):
            lo = h * HEAD_PAD
            qn, _ = _rms(qp[:, lo:lo + HEAD_PAD], QK_HEAD)
            qn = qn * qg
            q_ref[:, lo:lo + LANES] = (qn[:, :LANES] * qscale).astype(BF16)
            q_ref[:, lo + LANES:lo + HEAD_PAD] = (_rope(qn[:, LANES:], tables) * qscale).astype(BF16)
        ckvn, _ = _rms(ckv_ref[...].astype(F32), KV_LORA)
        ckvb = (ckvn * kvag_ref[...]).astype(BF16)
        kn = _nn(ckvb, wkn_ref[...])
        v_ref[...] = _nn(ckvb, wv_ref[...]).astype(BF16)
        kr = kr_ref[:, 0:LANES].astype(F32)
        ssr = jnp.sum(kr * kr, axis=-1, keepdims=True)
        kg = kg_ref[...]
        for h in range(N_HEADS):
            knh = kn[:, h * QK_NOPE:(h + 1) * QK_NOPE]
            r = lax.rsqrt((jnp.sum(knh * knh, axis=-1, keepdims=True) + ssr) * (1.0 / QK_HEAD) + EPS)
            lo = h * HEAD_PAD
            k_ref[:, lo:lo + LANES] = (knh * r * kg[:, :LANES]).astype(BF16)
            k_ref[:, lo + LANES:lo + HEAD_PAD] = _rope(kr * r * kg[:, LANES:], tables).astype(BF16)

    def full(a):
        return pl.BlockSpec(a.shape, lambda i: (0,) * a.ndim)

    return pl.pallas_call(
        body, name="mla_prep", grid=(seq // ts,),
        in_specs=[pl.BlockSpec((ts, Q_LORA), lambda i: (i, U_CQ // Q_LORA)),
                  pl.BlockSpec((ts, KV_LORA), lambda i: (i, U_CKV // KV_LORA)),
                  pl.BlockSpec((ts, KR_PAD), lambda i: (i, U_KR // KR_PAD)),
                  pl.BlockSpec((ts, 1), lambda i: (i, 0)), full(freq), full(q_a_g), full(wq), full(kv_a_g), full(wkn),
                  full(wv), full(q_g), full(k_g)],
        out_specs=[pl.BlockSpec((ts, Q_PAD), lambda i: (i, 0)), pl.BlockSpec((ts, Q_PAD), lambda i: (i, 0)),
                   pl.BlockSpec((ts, D_ATTN), lambda i: (i, 0))],
        out_shape=[SDS((seq, Q_PAD), BF16), SDS((seq, Q_PAD), BF16), SDS((seq, D_ATTN), BF16)],
        compiler_params=_cp(("arbitrary",), 48),
    )(u, u, u, pos, freq, q_a_g, wq, kv_a_g, wkn, wv, q_g, k_g)


def _causal_mask(t, tq, q0):
    return lax.broadcasted_iota(jnp.int32, (t, tq), 0) <= lax.broadcasted_iota(jnp.int32, (t, tq), 1) + q0


def _flash_fwd(q, k, v, u):
    seq = q.shape[0]
    t = min(ATT_T, seq)
    n = seq // t
    n_pairs = n * (n + 1) // 2
    za_blk = U_ZA // V_HEAD

    def body(q_ref, k_ref, v_ref, z_ref, o_ref, y_ref, lse_ref, s_a, s_b, top_a, top_b, m_all, l_all, acc_all):
        ones = jnp.ones((16, t), BF16)
        bufs = ((s_a, top_a), (s_b, top_b))

        def rows(i):
            return pl.ds(pl.multiple_of(i * t, t), t)

        def scores(i, j, buf):
            s_ref, top_ref = buf
            s = _nt(k_ref[rows(j), :], q_ref[rows(i), :])
            ahead = lax.broadcasted_iota(jnp.int32, (t, t), 0) - lax.broadcasted_iota(jnp.int32, (t, t), 1)
            s = jnp.where(ahead <= (i - j) * t, s, -jnp.inf)
            s_ref[...] = s
            top_ref[...] = jnp.max(s, axis=0, keepdims=True)

        def absorb(i, j, buf):
            s_ref, top_ref = buf
            first = j == 0
            m = jnp.where(first, -jnp.inf, m_all[i])
            l = jnp.where(first, 0.0, l_all[i])
            acc = jnp.where(first, 0.0, acc_all[i])
            m_new = jnp.maximum(m, top_ref[...])
            alpha = jnp.exp2(m - m_new)
            p = jnp.exp2((s_ref[...] - m_new).astype(BF16))
            m_all[i] = m_new
            l_all[i] = alpha * l + _nn(ones, p)[0:1, :]
            acc_all[i] = alpha * acc + _tn(v_ref[rows(j), :], p)

        def trip(width):
            def walk(_, pair):
                i, j = pair
                for w in range(width):
                    done = j == i
                    ni, nj = jnp.where(done, i + 1, i), jnp.where(done, 0, j + 1)
                    scores(jnp.minimum(ni, n - 1), nj, bufs[(w + 1) % 2])
                    absorb(i, j, bufs[w % 2])
                    i, j = ni, nj
                return i, j
            return walk

        scores(0, 0, bufs[0])
        pair = lax.fori_loop(0, n_pairs // ATT_UNROLL, trip(ATT_UNROLL), (jnp.int32(0), jnp.int32(0)))
        if n_pairs % ATT_UNROLL:
            trip(n_pairs % ATT_UNROLL)(0, pair)

        def finish(i, carry):
            l = l_all[i]
            o = (acc_all[i] * (1.0 / l)).T
            lse_ref[:, rows(i)] = m_all[i] + jnp.log2(l)
            o_ref[rows(i), :] = o.astype(BF16)
            z = z_ref[rows(i), :].astype(F32)
            y_ref[rows(i), :] = (o * (z * _sigmoid(z))).astype(BF16)
            return carry

        lax.fori_loop(0, n, finish, 0)

    def col(width, cb):
        return pl.BlockSpec((seq, width), lambda h: (0, cb + h))

    return pl.pallas_call(
        body, name="flash_fwd", grid=(N_HEADS,),
        in_specs=[col(HEAD_PAD, 0), col(HEAD_PAD, 0), col(V_HEAD, 0), col(V_HEAD, za_blk)],
        out_specs=[col(V_HEAD, 0), col(V_HEAD, 0), pl.BlockSpec((None, 1, seq), lambda h: (h, 0, 0))],
        out_shape=[SDS((seq, D_ATTN), BF16), SDS((seq, D_ATTN), BF16), SDS((N_HEADS, 1, seq), F32)],
        scratch_shapes=[pltpu.VMEM((t, t), F32)] * 2 + [pltpu.VMEM((1, t), F32)] * 2
                       + [pltpu.VMEM((n, 1, t), F32)] * 2 + [pltpu.VMEM((n, V_HEAD, t), F32)],
        compiler_params=_cp(("arbitrary",), 52),
    )(q, k, v, u)


def _outproj_loss(y_conv, y_attn, x, target, gate, w_out):
    seq, dm = x.shape
    ts = min(OUT_T, seq)
    n = seq // ts
    dmix = w_out.shape[0]

    def body(yc_ref, ya_ref, x_ref, t_ref, gate_ref, wo_hbm, dout_ref, dy_ref, dyc_ref, stats_ref, wo_ref, sem, acc_ref):
        i = pl.program_id(0)

        @pl.when(i == 0)
        def _():
            cp = pltpu.make_async_copy(wo_hbm, wo_ref, sem)
            cp.start()
            cp.wait()
            acc_ref[...] = jnp.zeros_like(acc_ref)

        y = _nn(yc_ref[...], wo_ref[0:D_CONV, :]) + _nn(ya_ref[...], wo_ref[D_CONV:dmix, :])
        gate_v = gate_ref[...]
        diff = (x_ref[...] + gate_v * y) - t_ref[...]
        dout = diff * (1.0 / dm)
        dout_ref[...] = dout
        acc_ref[0:8, :] += jnp.sum((dout * y).reshape(ts // 8, 8, dm), axis=0)
        acc_ref[8:16, :] += jnp.sum((diff * diff).reshape(ts // 8, 8, dm), axis=0)
        dy = (dout * gate_v).astype(BF16)
        dy_ref[...] = dy
        dyc_ref[...] = _nt(dy, wo_ref[...]).astype(BF16)

        @pl.when(i == n - 1)
        def _():
            stats_ref[...] = jnp.zeros_like(stats_ref)
            stats_ref[0:1, :] = jnp.sum(acc_ref[0:8, :], axis=0, keepdims=True)
            loss = jnp.sum(acc_ref[8:16, :]) * (0.5 / dm)
            stats_ref[1:2, :] = jnp.full((1, dm), loss, F32)

    row = pl.BlockSpec((ts, dm), lambda i: (i, 0))
    half = pl.BlockSpec((ts, D_CONV), lambda i: (i, 0))
    return pl.pallas_call(
        body, name="outproj_loss", grid=(n,),
        in_specs=[half, half, row, row, pl.BlockSpec((1, dm), lambda i: (0, 0)), ANY],
        out_specs=[row, row, pl.BlockSpec((ts, dmix), lambda i: (i, 0)), pl.BlockSpec((8, dm), lambda i: (0, 0))],
        out_shape=[SDS((seq, dm), F32), SDS((seq, dm), BF16), SDS((seq, dmix), BF16), SDS((8, dm), F32)],
        scratch_shapes=[pltpu.VMEM(w_out.shape, BF16), pltpu.SemaphoreType.DMA(()), pltpu.VMEM((16, dm), F32)],
        compiler_params=_cp(("arbitrary",), 52),
    )(y_conv, y_attn, x, target, gate, w_out)


def _matmul_tn(a, b, name):
    seq, m = a.shape
    n = b.shape[1]
    tm, tn, tk = min(TN_TM, m), min(TN_TN, n), min(TN_TK, seq)
    nk = seq // tk

    def body(a_ref, b_ref, o_ref, acc_ref):
        kk = pl.program_id(2)

        @pl.when(kk == 0)
        def _():
            acc_ref[...] = jnp.zeros_like(acc_ref)

        acc_ref[...] += _tn(a_ref[...], b_ref[...])

        @pl.when(kk == nk - 1)
        def _():
            o_ref[...] = acc_ref[...].astype(BF16)

    return pl.pallas_call(
        body, name=name, grid=(m // tm, n // tn, nk),
        in_specs=[pl.BlockSpec((tk, tm), lambda i, j, kk: (kk, i)), pl.BlockSpec((tk, tn), lambda i, j, kk: (kk, j))],
        out_specs=pl.BlockSpec((tm, tn), lambda i, j, kk: (i, j)), out_shape=SDS((m, n), BF16),
        scratch_shapes=[pltpu.VMEM((tm, tn), F32)],
        compiler_params=_cp(("arbitrary", "arbitrary", "arbitrary"), 40),
    )(a, b)


def _attn_gate_bwd(dycat, o, u):
    seq = o.shape[0]
    ts = min(ROW_T, seq)

    def body(dy_ref, o_ref, z_ref, dot_ref, dz_ref, dl_ref):
        dy = dy_ref[...].astype(F32)
        ov = o_ref[...].astype(F32)
        z = z_ref[...].astype(F32)
        sg = _sigmoid(z)
        do = dy * (z * sg)
        dz_ref[...] = (dy * ov * _silu_grad(z, sg)).astype(BF16)
        prod = do * ov
        ones = jnp.ones((8, V_HEAD), F32)
        for h in range(N_HEADS):
            cols = slice(h * V_HEAD, (h + 1) * V_HEAD)
            dot_ref[h] = do[:, cols].T.astype(BF16)
            rows = lax.dot_general(ones, prod[:, cols], (((1,), (1,)), ((), ())), precision=lax.Precision.HIGHEST,
                                   preferred_element_type=F32)
            dl_ref[h] = rows[0:1, :]

    blk = pl.BlockSpec((ts, D_ATTN), lambda i: (i, 0))
    return pl.pallas_call(
        body, name="attn_gate_bwd", grid=(seq // ts,),
        in_specs=[pl.BlockSpec((ts, D_ATTN), lambda i: (i, 1)), blk, pl.BlockSpec((ts, D_ATTN), lambda i: (i, U_ZA // D_ATTN))],
        out_specs=[pl.BlockSpec((N_HEADS, V_HEAD, ts), lambda i: (0, 0, i)), blk,
                   pl.BlockSpec((N_HEADS, 1, ts), lambda i: (0, 0, i))],
        out_shape=[SDS((N_HEADS, V_HEAD, seq), BF16), SDS((seq, D_ATTN), BF16), SDS((N_HEADS, 1, seq), F32)],
        compiler_params=_cp(("arbitrary",), 40),
    )(dycat, o, u)


def _flash_bwd(q, k, v, do_t, lse, delta):
    seq = q.shape[0]
    t = min(ATT_T, seq)
    n = seq // t
    n_pairs = n * (n + 1) // 2

    def body(k_ref, v_ref, q_ref, dot_ref, lse_ref, dl_ref, dq_ref, dk_ref, dv_ref, s_a, s_b, dp_a, dp_b, dq_acc, dk_acc,
             dvt_acc):
        bufs = ((s_a, dp_a), (s_b, dp_b))

        def rows(i):
            return pl.ds(pl.multiple_of(i * t, t), t)

        def products(j, i, buf):
            s_ref, dp_ref = buf
            s = _nt(k_ref[rows(j), :], q_ref[rows(i), :])
            ahead = lax.broadcasted_iota(jnp.int32, (t, t), 0) - lax.broadcasted_iota(jnp.int32, (t, t), 1)
            s_ref[...] = jnp.where(ahead <= (i - j) * t, s, -jnp.inf)
            dp_ref[...] = _nn(v_ref[rows(j), :], dot_ref[:, rows(i)])

        def absorb(j, i, buf):
            s_ref, dp_ref = buf
            first = i == j
            p = jnp.exp2((s_ref[...] - lse_ref[:, rows(i)]).astype(BF16))
            dvt = jnp.where(first, 0.0, dvt_acc[...]) + _nt(dot_ref[:, rows(i)], p)
            ds = p * (dp_ref[...] - dl_ref[:, rows(i)]).astype(BF16)
            dk = jnp.where(first, 0.0, dk_acc[...]) + _nn(ds, q_ref[rows(i), :])
            dq_acc[rows(i), :] += _tn(ds, k_ref[rows(j), :])
            dvt_acc[...] = dvt
            dk_acc[...] = dk
            dk_ref[rows(j), :] = dk.astype(BF16)
            dv_ref[rows(j), :] = dvt.T.astype(BF16)

        def trip(width):
            def walk(_, pair):
                j, i = pair
                for w in range(width):
                    done = i == n - 1
                    nj = jnp.where(done, j + 1, j)
                    ni = jnp.where(done, j + 1, i + 1)
                    products(jnp.minimum(nj, n - 1), jnp.minimum(ni, n - 1), bufs[(w + 1) % 2])
                    absorb(j, i, bufs[w % 2])
                    j, i = nj, ni
                return j, i
            return walk

        dq_acc[...] = jnp.zeros_like(dq_acc)
        products(0, 0, bufs[0])
        pair = lax.fori_loop(0, n_pairs // ATT_UNROLL, trip(ATT_UNROLL), (jnp.int32(0), jnp.int32(0)))
        if n_pairs % ATT_UNROLL:
            trip(n_pairs % ATT_UNROLL)(0, pair)

        def finish(i, carry):
            dq_ref[rows(i), :] = dq_acc[rows(i), :].astype(BF16)
            return carry

        lax.fori_loop(0, n, finish, 0)

    def col(width):
        return pl.BlockSpec((seq, width), lambda h: (0, h))

    row = pl.BlockSpec((None, 1, seq), lambda h: (h, 0, 0))
    return pl.pallas_call(
        body, name="flash_bwd", grid=(N_HEADS,),
        in_specs=[col(HEAD_PAD), col(V_HEAD), col(HEAD_PAD), pl.BlockSpec((None, V_HEAD, seq), lambda h: (h, 0, 0)), row, row],
        out_specs=[col(HEAD_PAD), col(HEAD_PAD), col(V_HEAD)],
        out_shape=[SDS((seq, Q_PAD), BF16), SDS((seq, Q_PAD), BF16), SDS((seq, D_ATTN), BF16)],
        scratch_shapes=[pltpu.VMEM((t, t), F32)] * 4
                       + [pltpu.VMEM((seq, HEAD_PAD), F32), pltpu.VMEM((t, HEAD_PAD), F32), pltpu.VMEM((V_HEAD, t), F32)],
        compiler_params=_cp(("arbitrary",), 60),
    )(k, v, q, do_t, lse, delta)


SG_QAG, SG_KVAG, SG_QG, SG_KG, SG_COLS = 0, Q_LORA, Q_LORA + KV_LORA, Q_LORA + KV_LORA + HEAD_PAD, D_MODEL


def _mla_bwd(dq, dk, dv, u, pos, freq, q_a_g, wq, kv_a_g, wkn, wv, q_g, k_g):
    seq = u.shape[0]
    ts = min(ROW_T, seq)
    n = seq // ts
    qscale = 1.0 / math.sqrt(QK_HEAD)

    def body(dq_ref, dk_ref, dv_ref, cq_ref, ckv_ref, kr_ref, pos_ref, freq_ref, qag_ref, wq_ref, kvag_ref, wkn_ref,
             wv_ref, qg_ref, kg_ref, du_ref, dwq_ref, dwkn_ref, dwv_ref, sg_ref, dqp_ref, dkn_ref):
        i = pl.program_id(0)

        @pl.when(i == 0)
        def _():
            dwq_ref[...] = jnp.zeros_like(dwq_ref)
            dwkn_ref[...] = jnp.zeros_like(dwkn_ref)
            dwv_ref[...] = jnp.zeros_like(dwv_ref)
            sg_ref[...] = jnp.zeros_like(sg_ref)

        tables = _rope_tables(pos_ref, freq_ref)

        cq = cq_ref[...].astype(F32)
        cqn, rq = _rms(cq, Q_LORA)
        qag = qag_ref[...]
        cqb = (cqn * qag).astype(BF16)
        qp = _nn(cqb, wq_ref[...])
        qg = qg_ref[...]
        dqg = jnp.zeros((1, HEAD_PAD), F32)
        for h in range(N_HEADS):
            lo = h * HEAD_PAD
            xn, r = _rms(qp[:, lo:lo + HEAD_PAD], QK_HEAD)
            g = jnp.concatenate([dq_ref[:, lo:lo + LANES].astype(F32),
                                 _rope_bwd(dq_ref[:, lo + LANES:lo + HEAD_PAD].astype(F32), tables)], axis=-1) * qscale
            dqg = dqg + jnp.sum(g * xn, axis=0, keepdims=True)
            gy = g * qg
            mean = jnp.sum(gy * xn, axis=-1, keepdims=True) * (1.0 / QK_HEAD)
            dqp_ref[:, lo:lo + HEAD_PAD] = (r * (gy - xn * mean)).astype(BF16)
        dqp = dqp_ref[...]
        dwq_ref[...] += _tn(cqb, dqp)
        dcqn = _nt(dqp, wq_ref[...])
        sg_ref[0:1, SG_QAG:SG_QAG + Q_LORA] += jnp.sum(dcqn * cqn, axis=0, keepdims=True)
        sg_ref[0:1, SG_QG:SG_QG + HEAD_PAD] += dqg
        gy = dcqn * qag
        mean = jnp.sum(gy * cqn, axis=-1, keepdims=True) * (1.0 / Q_LORA)
        du_ref[:, 0:Q_LORA] = (rq * (gy - cqn * mean)).astype(BF16)

        ckv = ckv_ref[...].astype(F32)
        ckvn, rkv = _rms(ckv, KV_LORA)
        kvag = kvag_ref[...]
        ckvb = (ckvn * kvag).astype(BF16)
        kn = _nn(ckvb, wkn_ref[...])
        kr = kr_ref[:, 0:LANES].astype(F32)
        ssr = jnp.sum(kr * kr, axis=-1, keepdims=True)
        kg = kg_ref[...]
        kg_n, kg_r = kg[:, :LANES] * LN2, kg[:, LANES:] * LN2
        dkg_n = jnp.zeros((1, LANES), F32)
        dkg_r = jnp.zeros((1, LANES), F32)
        dkr = jnp.zeros((ts, LANES), F32)
        for h in range(N_HEADS):
            knh = kn[:, h * QK_NOPE:(h + 1) * QK_NOPE]
            r = lax.rsqrt((jnp.sum(knh * knh, axis=-1, keepdims=True) + ssr) * (1.0 / QK_HEAD) + EPS)
            xn_n, xn_r = knh * r, kr * r
            lo = h * HEAD_PAD
            g_n = dk_ref[:, lo:lo + LANES].astype(F32)
            g_r = _rope_bwd(dk_ref[:, lo + LANES:lo + HEAD_PAD].astype(F32), tables)
            dkg_n = dkg_n + jnp.sum(g_n * xn_n, axis=0, keepdims=True)
            dkg_r = dkg_r + jnp.sum(g_r * xn_r, axis=0, keepdims=True)
            gy_n, gy_r = g_n * kg_n, g_r * kg_r
            mean = (jnp.sum(gy_n * xn_n, axis=-1, keepdims=True) + jnp.sum(gy_r * xn_r, axis=-1, keepdims=True)) * (1.0 / QK_HEAD)
            dkn_ref[:, h * QK_NOPE:(h + 1) * QK_NOPE] = (r * (gy_n - xn_n * mean)).astype(BF16)
            dkr = dkr + r * (gy_r - xn_r * mean)
        dkn = dkn_ref[...]
        dvv = dv_ref[...]
        dwkn_ref[...] += _tn(ckvb, dkn)
        dwv_ref[...] += _tn(ckvb, dvv)
        dckvn = _nt(dkn, wkn_ref[...]) + _nt(dvv, wv_ref[...])
        sg_ref[0:1, SG_KVAG:SG_KVAG + KV_LORA] += jnp.sum(dckvn * ckvn, axis=0, keepdims=True)
        sg_ref[0:1, SG_KG:SG_KG + LANES] += dkg_n * LN2
        sg_ref[0:1, SG_KG + LANES:SG_KG + HEAD_PAD] += dkg_r * LN2
        gy = dckvn * kvag
        mean = jnp.sum(gy * ckvn, axis=-1, keepdims=True) * (1.0 / KV_LORA)
        du_ref[:, Q_LORA:Q_LORA + KV_LORA] = (rkv * (gy - ckvn * mean)).astype(BF16)
        du_ref[:, Q_LORA + KV_LORA:Q_LORA + KV_LORA + LANES] = dkr.astype(BF16)
        du_ref[:, Q_LORA + KV_LORA + LANES:MLA_COLS] = jnp.zeros((ts, LANES), BF16)

    def full(a):
        return pl.BlockSpec(a.shape, lambda i: (0,) * a.ndim)

    wide = pl.BlockSpec((ts, Q_PAD), lambda i: (i, 0))
    return pl.pallas_call(
        body, name="mla_bwd", grid=(n,),
        in_specs=[wide, wide, pl.BlockSpec((ts, D_ATTN), lambda i: (i, 0)),
                  pl.BlockSpec((ts, Q_LORA), lambda i: (i, U_CQ // Q_LORA)),
                  pl.BlockSpec((ts, KV_LORA), lambda i: (i, U_CKV // KV_LORA)),
                  pl.BlockSpec((ts, KR_PAD), lambda i: (i, U_KR // KR_PAD)),
                  pl.BlockSpec((ts, 1), lambda i: (i, 0)), full(freq), full(q_a_g), full(wq), full(kv_a_g), full(wkn),
                  full(wv), full(q_g), full(k_g)],
        out_specs=[pl.BlockSpec((ts, MLA_COLS), lambda i: (i, 0)), pl.BlockSpec((Q_LORA, Q_PAD), lambda i: (0, 0)),
                   pl.BlockSpec((KV_LORA, D_ATTN), lambda i: (0, 0)), pl.BlockSpec((KV_LORA, D_ATTN), lambda i: (0, 0)),
                   pl.BlockSpec((8, SG_COLS), lambda i: (0, 0))],
        out_shape=[SDS((seq, MLA_COLS), BF16), SDS((Q_LORA, Q_PAD), F32), SDS((KV_LORA, D_ATTN), F32),
                   SDS((KV_LORA, D_ATTN), F32), SDS((8, SG_COLS), F32)],
        scratch_shapes=[pltpu.VMEM((ts, Q_PAD), BF16), pltpu.VMEM((ts, D_ATTN), BF16)],
        compiler_params=_cp(("arbitrary",), 56),
    )(dq, dk, dv, u, u, u, pos, freq, q_a_g, wq, kv_a_g, wkn, wv, q_g, k_g)


def _conv_bwd(dycat, u, conv_w):
    seq = u.shape[0]
    ts = min(ROW_T, seq)
    n = seq // ts
    hb = ts // HALO

    def body(dy_ref, xc_ref, bc_ref, cc_ref, zc_ref, xp_ref, cp_ref, dyn_ref, bn_ref, zn_ref, w_ref,
             du_ref, dw_ref, ext_ref):
        i = pl.program_id(0)

        @pl.when(i == 0)
        def _():
            dw_ref[...] = jnp.zeros_like(dw_ref)

        xc = xc_ref[...].astype(F32)
        cc = cc_ref[...].astype(F32)
        uc = cc * xc
        up = jnp.where(i > 0, cp_ref[...].astype(F32) * xp_ref[...].astype(F32), 0.0)
        u1, u2 = _conv_taps(uc, up, ext_ref, ts, True)
        w0, w1, w2 = w_ref[0:1, :], w_ref[1:2, :], w_ref[2:3, :]
        conv = w0 * u2 + w1 * u1 + w2 * uc
        z = zc_ref[...].astype(F32)
        sg = _sigmoid(z)
        sz = z * sg
        b = bc_ref[...].astype(F32)
        dy = dy_ref[...].astype(F32)
        du_ref[:, 3 * D_CONV:4 * D_CONV] = (dy * (b * conv) * _silu_grad(z, sg)).astype(BF16)
        du_ref[:, D_CONV:2 * D_CONV] = (dy * sz * conv).astype(BF16)
        dconv = dy * sz * b
        dw_ref[0:1, :] += jnp.sum(dconv * u2, axis=0, keepdims=True)
        dw_ref[1:2, :] += jnp.sum(dconv * u1, axis=0, keepdims=True)
        dw_ref[2:3, :] += jnp.sum(dconv * uc, axis=0, keepdims=True)
        zn = zn_ref[...].astype(F32)
        dnext = dyn_ref[...].astype(F32) * (zn * _sigmoid(zn)) * bn_ref[...].astype(F32)
        dnext = jnp.where(i < n - 1, dnext, 0.0)
        d1, d2 = _conv_taps(dconv, dnext, ext_ref, ts, False)
        du = w2 * dconv + w1 * d1 + w0 * d2
        du_ref[:, 2 * D_CONV:3 * D_CONV] = (du * xc).astype(BF16)
        du_ref[:, 0:D_CONV] = (du * cc).astype(BF16)

    def col(cb):
        return pl.BlockSpec((ts, D_CONV), lambda i: (i, cb))

    def prev(cb):
        return pl.BlockSpec((HALO, D_CONV), lambda i: (jnp.maximum(i * hb - 1, 0), cb))

    def nxt(cb):
        return pl.BlockSpec((HALO, D_CONV), lambda i: (jnp.minimum((i + 1) * hb, n * hb - 1), cb))

    return pl.pallas_call(
        body, name="conv_bwd", grid=(n,),
        in_specs=[col(0), col(0), col(1), col(2), col(3), prev(0), prev(2), nxt(0), nxt(1), nxt(3),
                  pl.BlockSpec((3, D_CONV), lambda i: (0, 0))],
        out_specs=[pl.BlockSpec((ts, 4 * D_CONV), lambda i: (i, 0)), pl.BlockSpec((8, D_CONV), lambda i: (0, 0))],
        out_shape=[SDS((seq, 4 * D_CONV), BF16), SDS((8, D_CONV), F32)],
        scratch_shapes=[pltpu.VMEM((ts + HALO, D_CONV), F32)],
        compiler_params=_cp(("arbitrary",), 48),
    )(dycat, u, u, u, u, u, u, dycat, u, u, conv_w)


def _inproj_bwd(du_conv, du_za, du_mla, w_t, parts):
    seq = du_conv.shape[0]
    dm = w_t.shape[1]
    tm, tn = min(DH_TM, seq), DH_TN
    ni, nj = seq // tm, dm // tn
    na = len(parts)

    def body(dc_ref, dz_ref, dm_ref, w_ref, *rest):
        part_refs, o_ref, recv_refs = rest[:na], rest[na], rest[na + 1:2 * na + 1]
        ssem, rsem = rest[2 * na + 1:]
        i, j = pl.program_id(0), pl.program_id(1)
        sends, recvs = _chip_exchange_copies(part_refs, recv_refs, ssem, rsem)

        @pl.when((i == 0) & (j == 0))
        def _():
            for cp in sends:
                cp.start()

        acc = _nn(dc_ref[...], w_ref[0:U_ZA, :])
        acc = acc + _nn(dz_ref[...], w_ref[U_ZA:U_CQ, :])
        acc = acc + _nn(dm_ref[...], w_ref[U_CQ:U_COLS, :])
        o_ref[...] = acc

        @pl.when((i == ni - 1) & (j == nj - 1))
        def _():
            for cp in recvs:
                cp.wait_recv()
            for cp in sends:
                cp.wait_send()

    outs = pl.pallas_call(
        body, name="inproj_bwd", grid=(ni, nj),
        in_specs=[pl.BlockSpec((tm, U_ZA), lambda i, j: (i, 0)), pl.BlockSpec((tm, D_ATTN), lambda i, j: (i, 0)),
                  pl.BlockSpec((tm, MLA_COLS), lambda i, j: (i, 0)), pl.BlockSpec((U_COLS, tn), lambda i, j: (0, j))]
                 + [ANY] * na,
        out_specs=[pl.BlockSpec((tm, tn), lambda i, j: (i, j))] + [ANY] * na,
        out_shape=[SDS((seq, dm), F32)] + [SDS(p.shape, p.dtype) for p in parts],
        scratch_shapes=[pltpu.SemaphoreType.DMA((3 * na,))] * 2,
        compiler_params=_cp(("arbitrary", "arbitrary"), 48),
    )(du_conv, du_za, du_mla, w_t, *parts)
    return outs[0], outs[1:]


def _prenorm_bwd(x, dh, dout, norm_g, scale):
    seq, dm = x.shape
    ts = min(ROW_T, seq)
    n = seq // ts

    def body(x_ref, dh_ref, dout_ref, g_ref, sc_ref, gx_ref, st_ref, acc_ref):
        i = pl.program_id(0)

        @pl.when(i == 0)
        def _():
            acc_ref[...] = jnp.zeros_like(acc_ref)

        xv = x_ref[...]
        xn, r = _rms(xv, dm)
        dh_v = dh_ref[...]
        gv = g_ref[...]
        one_sc = 1.0 + sc_ref[...]

        def fold(a):
            return jnp.sum(a.reshape(ts // 8, 8, dm), axis=0)

        acc_ref[0:8, :] += fold(dh_v)
        acc_ref[8:16, :] += fold(dh_v * (xn * gv))
        dxg = dh_v * one_sc
        acc_ref[16:24, :] += fold(dxg * xn)
        dxn = dxg * gv
        mean = jnp.sum(dxn * xn, axis=-1, keepdims=True) * (1.0 / dm)
        gx_ref[...] = dout_ref[...] + r * (dxn - xn * mean)

        @pl.when(i == n - 1)
        def _():
            st_ref[...] = jnp.zeros_like(st_ref)
            for k in range(3):
                st_ref[k:k + 1, :] = jnp.sum(acc_ref[8 * k:8 * k + 8, :], axis=0, keepdims=True)

    row = pl.BlockSpec((ts, dm), lambda i: (i, 0))
    vec = pl.BlockSpec((1, dm), lambda i: (0, 0))
    return pl.pallas_call(
        body, name="prenorm_bwd", grid=(n,), in_specs=[row, row, row, vec, vec],
        out_specs=[row, pl.BlockSpec((8, dm), lambda i: (0, 0))],
        out_shape=[SDS((seq, dm), F32), SDS((8, dm), F32)],
        scratch_shapes=[pltpu.VMEM((24, dm), F32)], input_output_aliases={2: 0},
        compiler_params=_cp(("arbitrary",), 52),
    )(x, dh, dout, norm_g, scale)


def _unshard_cols(g):
    return jnp.transpose(g, (1, 0, 2)).reshape(g.shape[1], -1)


def _shard_cols(w):
    r = w.shape[0]
    return jnp.transpose(w.reshape(r, N_CHIPS, -1), (1, 0, 2))


W_IN_COLS = 4 * D_CONV + Q_LORA + KV_LORA + QK_ROPE + D_ATTN
SHARD_ROWS = W_IN_COLS // N_CHIPS
SHARD_PAD = 1536


def _w_in_pieces():
    c4 = 4 * D_CONV
    groups = [(0, c4, 0), (c4, c4 + Q_LORA, U_CQ), (c4 + Q_LORA, c4 + Q_LORA + KV_LORA, U_CKV),
              (c4 + Q_LORA + KV_LORA, W_IN_COLS - D_ATTN, U_KR), (W_IN_COLS - D_ATTN, W_IN_COLS, U_ZA)]
    pieces = []
    for lo, hi, my in groups:
        for chip in range(N_CHIPS):
            a, b = max(lo, chip * SHARD_ROWS), min(hi, (chip + 1) * SHARD_ROWS)
            if a < b:
                pieces.append((chip, a - chip * SHARD_ROWS, b - a, my + a - lo))
    return pieces


def _w_t_to_my(g):
    w = jnp.zeros((U_COLS, g.shape[2]), g.dtype)
    for chip, row, n, my in _w_in_pieces():
        w = lax.dynamic_update_slice(w, g[chip, row:row + n], (my, 0))
    return w


def _w_t_from_my(g_conv, g_za, g_mla):
    w = jnp.zeros((N_CHIPS, SHARD_PAD, g_conv.shape[1]), g_conv.dtype)
    for chip, row, n, my in _w_in_pieces():
        src, base = (g_conv, 0) if my < U_ZA else (g_za, U_ZA) if my < U_CQ else (g_mla, U_CQ)
        w = lax.dynamic_update_slice(w, src[my - base:my - base + n][None], (chip, row, 0))
    return w


def _heads_pad(w):
    r = w.shape[0]
    w3 = w.reshape(r, N_HEADS, QK_HEAD)
    return jnp.pad(w3, ((0, 0), (0, 0), (0, HEAD_PAD - QK_HEAD))).reshape(r, Q_PAD)


def _heads_unpad(w):
    r = w.shape[0]
    return w.reshape(r, N_HEADS, HEAD_PAD)[:, :, :QK_HEAD].reshape(r, N_HEADS * QK_HEAD)


def kernel(x, c, positions, ada_w, ada_b, norm_g, w_in, conv_w, q_a_g, w_q_b, kv_a_g, w_kv_b, q_g, k_g, w_out, loss_target, m_ada_w, m_ada_b, m_norm_g, m_w_in, m_conv_w, m_q_a_g, m_w_q_b, m_kv_a_g, m_w_kv_b, m_q_g, m_k_g, m_w_out, v_ada_w, v_ada_b, v_norm_g, v_w_in, v_conv_w, v_q_a_g, v_w_q_b, v_kv_a_g, v_w_kv_b, v_q_g, v_k_g, v_w_out):
    mx, my, mc = _place()
    chip = 2 * mx + my
    me = 2 * chip + mc
    seq = x.shape[1]
    x2, t2 = x[0], loss_target[0]
    cw_cols = conv_w.shape[2]

    small = jnp.zeros((8, D_MODEL), F32)
    small = small.at[0].set(c[0])
    small = small.at[1:4, :cw_cols].set(conv_w[0])
    small_all = _gather8(small, "gather_c_conv", False)[0]
    c_all = small_all[:, 0, :]
    conv_full = jnp.transpose(small_all.reshape(N_CHIPS, 2, 8, D_MODEL)[:, 0, 1:4, :cw_cols], (1, 0, 2)).reshape(3, D_CONV)

    ada_cols = ada_w.shape[2]
    b_k = lax.dynamic_slice(ada_b, (0, chip * ada_cols), (1, ada_cols))
    mod_k, sc_all = _ada_mod(c_all, ada_w[0], b_k)
    mod_all = _gather8(mod_k, "gather_mod", False)[0]
    mod_row = lax.dynamic_slice(mod_all.reshape(N_CHIPS, 2, N_DEV, ada_cols), (0, mc, me, 0), (N_CHIPS, 1, 1, ada_cols))
    mod_row = mod_row.reshape(3, D_MODEL)
    shift, scale, gate = mod_row[0:1], mod_row[1:2], mod_row[2:3]

    def own_slot(g, s):
        return lax.dynamic_update_slice(g, s[None], (chip, 0, 0))

    w_in_t, m_w_in_t, v_w_in_t = [jnp.transpose(a[0]) for a in (w_in, m_w_in, v_w_in)]
    shard_in = jnp.pad(w_in_t.astype(BF16), ((0, SHARD_PAD - SHARD_ROWS), (0, 0)))
    g_in = own_slot(_allgather_shards([shard_in])[0], shard_in)
    w_t = _w_t_to_my(g_in)

    later = [w_q_b[0].astype(BF16), w_kv_b[0].astype(BF16), w_out[0].astype(BF16)]
    h, u, got = _inproj(x2, norm_g, scale, shift, w_t, later)
    g_q, g_kv, g_out = [own_slot(g, s) for g, s in zip(got, later)]
    wq = _heads_pad(_unshard_cols(g_q))
    wkv = _unshard_cols(g_kv).reshape(KV_LORA, N_HEADS, QK_NOPE + V_HEAD)
    wkn = wkv[:, :, :QK_NOPE].reshape(KV_LORA, N_HEADS * QK_NOPE)
    wv = wkv[:, :, QK_NOPE:].reshape(KV_LORA, D_ATTN)
    wo = g_out.reshape(N_CHIPS * g_out.shape[1], D_MODEL)
    y_conv = _conv_fwd(u, conv_full)
    pos = positions.reshape(seq, 1)
    inv_freq = ROPE_BASE ** (-jnp.arange(0, QK_ROPE, 2, dtype=F32) / QK_ROPE)
    freq = jnp.concatenate([inv_freq, inv_freq, jnp.zeros((LANES - QK_ROPE,), F32)]).reshape(1, LANES)
    q_g_pad = jnp.pad(q_g, ((0, 0), (0, HEAD_PAD - QK_HEAD)))
    k_g_pad = jnp.pad(k_g, ((0, 0), (0, HEAD_PAD - QK_HEAD)))
    q, k, v = _mla_prep(u, pos, freq, q_a_g, wq, kv_a_g, wkn, wv, q_g_pad, k_g_pad)
    o, y_attn, lse = _flash_fwd(q, k, v, u)
    dout, dy, dycat, st_out = _outproj_loss(y_conv, y_attn, x2, t2, gate, wo)

    dw_out = jnp.concatenate([_matmul_tn(y_conv, dy, "dw_out_conv"), _matmul_tn(y_attn, dy, "dw_out_attn")], axis=0)
    do_t, du_za, delta = _attn_gate_bwd(dycat, o, u)
    dq, dk, dv = _flash_bwd(q, k, v, do_t, lse, delta)
    du_mla, dwq, dwkn, dwv, sg_mla = _mla_bwd(dq, dk, dv, u, pos, freq, q_a_g, wq, kv_a_g, wkn, wv, q_g_pad, k_g_pad)
    du_conv, dconv_w = _conv_bwd(dycat, u, conv_full)
    dw_conv = _matmul_tn(du_conv, h, "dw_in_conv")
    dw_za = _matmul_tn(du_za, h, "dw_in_za")
    dw_mla = _matmul_tn(du_mla, h, "dw_in_mla")

    dw_q_nat = _heads_unpad(dwq).astype(BF16)
    dw_kv_nat = jnp.concatenate([dwkn.reshape(KV_LORA, N_HEADS, QK_NOPE), dwv.reshape(KV_LORA, N_HEADS, V_HEAD)],
                                axis=2).reshape(KV_LORA, N_HEADS * (QK_NOPE + V_HEAD)).astype(BF16)
    grads = [_w_t_from_my(dw_conv, dw_za, dw_mla), _shard_cols(dw_q_nat), _shard_cols(dw_kv_nat),
             dw_out.reshape(N_CHIPS, dw_out.shape[0] // N_CHIPS, D_MODEL)]
    theirs = _rs_core_swap(grads)
    names = ["w_in", "w_q_b", "w_kv_b", "w_out"]
    core = jnp.reshape(mc, (1,)).astype(jnp.int32)
    parts = [_add_half_bf16(g, b, core, "rs_add_" + nm) for g, b, nm in zip(grads, theirs, names)]
    dh, recv = _inproj_bwd(du_conv, du_za, du_mla, w_t, parts)
    recv = [lax.dynamic_update_slice(r, lax.dynamic_slice(p, (chip, 0, 0), (1,) + p.shape[1:]), (chip, 0, 0))
            for r, p in zip(recv, parts)]
    halves = [_sum_chips(p, "rs_sum_" + nm) for p, nm in zip(recv, names)]
    joined = _rs_core_join(halves)
    joined = [lax.dynamic_update_slice(j, hf[None], (mc, 0, 0)) for j, hf in zip(joined, halves)]
    g_big = [j.reshape(2 * j.shape[1], j.shape[2]) for j in joined]
    grad_x, st_in = _prenorm_bwd(x2, dh, dout, norm_g, scale)

    sgrad = jnp.zeros((8, D_MODEL), F32)
    sgrad = sgrad.at[0:2].set(st_in[0:2])
    sgrad = sgrad.at[2].set(st_out[0])
    sgrad = sgrad.at[3].set(st_in[2])
    sgrad = sgrad.at[4, :D_CONV].set(dconv_w[0]).at[4, D_CONV:].set(dconv_w[1])
    sgrad = sgrad.at[5, :D_CONV].set(dconv_w[2]).at[5, D_CONV:].set(sg_mla[0, :D_CONV])
    sgrad = sgrad.at[6, :HEAD_PAD].set(sg_mla[0, SG_KG:SG_KG + HEAD_PAD])
    sgrad = sgrad.at[7].set(st_out[1])
    sg_all, sg_sum = _gather8(sgrad, "gather_small_grads", True)
    loss = sg_sum[7, 0]
    g_ada_b = sg_sum[0:3].reshape(1, 3 * D_MODEL)
    g_norm_g = sg_sum[3:4]
    conv_sum = jnp.stack([sg_sum[4, :D_CONV], sg_sum[4, D_CONV:], sg_sum[5, :D_CONV]])
    g_conv_w = lax.dynamic_slice(conv_sum, (0, chip * cw_cols), (3, cw_cols))
    g_q_a_g = sg_sum[5:6, D_CONV + SG_QAG:D_CONV + SG_QAG + Q_LORA]
    g_kv_a_g = sg_sum[5:6, D_CONV + SG_KVAG:D_CONV + SG_KVAG + KV_LORA]
    g_q_g = sg_sum[5:6, D_CONV + SG_QG:D_CONV + SG_QG + QK_HEAD]
    g_k_g = sg_sum[6:7, :QK_HEAD]
    dmod_k = lax.dynamic_slice(sg_all[:, 0:3, :].reshape(N_DEV, 3 * D_MODEL), (0, chip * ada_cols), (N_DEV, ada_cols))

    g_ada_w, d_ada_w, nm_ada_w, nv_ada_w = _ada_w_update(sc_all, dmod_k, ada_w[0], m_ada_w[0], v_ada_w[0])
    upd = {}
    big = {"w_q_b": (w_q_b, m_w_q_b, v_w_q_b), "w_kv_b": (w_kv_b, m_w_kv_b, v_w_kv_b), "w_out": (w_out, m_w_out, v_w_out)}
    for nm, g in zip(names[1:], g_big[1:]):
        w_, m_, v_ = big[nm]
        upd[nm] = (g,) + tuple(_adamw(w_[0], g, m_[0], v_[0], "adamw_" + nm))
    d_t, nm_t, nv_t, g_t = _adamw(w_in_t, g_big[0], m_w_in_t, v_w_in_t, "adamw_w_in", echo_g=True)
    upd["w_in"] = tuple(jnp.transpose(a) for a in (g_t, d_t, nm_t, nv_t))
    small_w = {"ada_b": (ada_b, m_ada_b, v_ada_b, g_ada_b), "norm_g": (norm_g, m_norm_g, v_norm_g, g_norm_g),
               "conv_w": (conv_w[0], m_conv_w[0], v_conv_w[0], g_conv_w), "q_a_g": (q_a_g, m_q_a_g, v_q_a_g, g_q_a_g),
               "kv_a_g": (kv_a_g, m_kv_a_g, v_kv_a_g, g_kv_a_g), "q_g": (q_g, m_q_g, v_q_g, g_q_g),
               "k_g": (k_g, m_k_g, v_k_g, g_k_g)}
    for nm, (w_, m_, v_, g) in small_w.items():
        upd[nm] = (g,) + tuple(_adamw(w_, g, m_, v_, "adamw_" + nm))
    upd["ada_w"] = (g_ada_w, d_ada_w, nm_ada_w, nv_ada_w)

    order = ["ada_w", "ada_b", "norm_g", "w_in", "conv_w", "q_a_g", "w_q_b", "kv_a_g", "w_kv_b", "q_g", "k_g", "w_out"]
    lead1 = {"ada_w", "w_in", "conv_w", "w_q_b", "w_kv_b", "w_out"}

    def shaped(nm, a):
        return a[None] if nm in lead1 else a

    outs = [loss, grad_x[None]]
    for idx in range(4):
        outs += [shaped(nm, upd[nm][idx]) for nm in order]
    return tuple(outs)
```

```python
import functools
import math

import jax
import jax.numpy as jnp
from jax import lax
from jax.experimental import pallas as pl
from jax.experimental.pallas import tpu as pltpu

F32 = jnp.float32
BF16 = jnp.bfloat16
MESH = pl.DeviceIdType.MESH
SDS = jax.ShapeDtypeStruct
ANY = pl.BlockSpec(memory_space=pl.ANY)

D_MODEL = 2048
D_CONV = 1024
N_HEADS = 8
QK_NOPE = 128
QK_ROPE = 64
QK_HEAD = QK_NOPE + QK_ROPE
V_HEAD = 128
D_ATTN = N_HEADS * V_HEAD
Q_LORA = 512
KV_LORA = 256
ROPE_BASE = 10000.0
EPS = 1e-6
LOG2E = math.log2(math.e)
LN2 = math.log(2.0)
ADAM_LR, ADAM_B1, ADAM_B2, ADAM_EPS, ADAM_WD, ADAM_STEP = 0.001, 0.9, 0.999, 1e-08, 0.01, 10
N_CHIPS = 4
N_DEV = 8

LANES = 128
V7X_VMEM_BYTES = 64 * 1024 * 1024
MIB = 1024 * 1024

HEAD_PAD = 256
Q_PAD = N_HEADS * HEAD_PAD
U_ZA = 4 * D_CONV
U_CQ = U_ZA + D_ATTN
U_CKV = U_CQ + Q_LORA
U_KR = U_CKV + KV_LORA
KR_PAD = 256
U_COLS = U_KR + KR_PAD
MLA_COLS = Q_LORA + KV_LORA + KR_PAD

ATT_T = 512
INPROJ_TM, INPROJ_TN = 1024, 512
ROW_T = 512
OUT_T = 256
DH_TM, DH_TN = 512, 512
TN_TM, TN_TN, TN_TK = 1024, 1024, 2048
ATT_UNROLL = 8


def _cp(sem=None, vmem_mib=None, **kw):
    if sem is not None:
        kw["dimension_semantics"] = sem
    if vmem_mib is not None:
        kw["vmem_limit_bytes"] = min(vmem_mib * MIB, V7X_VMEM_BYTES - 4 * MIB)
    return pltpu.CompilerParams(**kw)


def _sigmoid(z):
    return 1.0 / (1.0 + jnp.exp(-z))


def _silu_grad(z, sg):
    return sg * (1.0 + z * (1.0 - sg))


def _nt(a, b):
    return lax.dot_general(a, b, (((1,), (1,)), ((), ())), preferred_element_type=F32)


def _tn(a, b):
    return lax.dot_general(a, b, (((0,), (0,)), ((), ())), preferred_element_type=F32)


def _nn(a, b):
    return jnp.dot(a, b, preferred_element_type=F32)


def _place():
    return lax.axis_index("x"), lax.axis_index("y"), lax.axis_index("c")


def _gather8(v, name, with_sum):
    rows, cols = v.shape

    def body(v_ref, out_ref, *rest):
        if with_sum:
            sum_ref, send_sems, recv_sems = rest
        else:
            send_sems, recv_sems = rest
        mx, my, mc = _place()
        me = 4 * mx + 2 * my + mc
        out_ref[me] = v_ref[...]
        peers = []
        for d in range(1, N_DEV):
            px = 1 - mx if d & 4 else mx
            py = 1 - my if d & 2 else my
            pc = 1 - mc if d & 1 else mc
            peers.append((px, py, pc))

        def copy(d, slot, to):
            return pltpu.make_async_remote_copy(
                src_ref=v_ref, dst_ref=out_ref.at[slot], send_sem=send_sems.at[d], recv_sem=recv_sems.at[d],
                device_id=to, device_id_type=MESH)

        sends = [copy(d, me, p) for d, p in enumerate(peers)]
        for cp in sends:
            cp.start()
        for d, (px, py, pc) in enumerate(peers):
            copy(d, 4 * px + 2 * py + pc, (px, py, pc)).wait_recv()
        for cp in sends:
            cp.wait_send()
        if with_sum:
            acc = out_ref[0]
            for b in range(1, N_DEV):
                acc = acc + out_ref[b]
            sum_ref[...] = acc

    out_shape = [SDS((N_DEV, rows, cols), F32)]
    if with_sum:
        out_shape.append(SDS((rows, cols), F32))
    vm = pl.BlockSpec(memory_space=pltpu.VMEM)
    return pl.pallas_call(
        body, name=name, out_shape=out_shape, in_specs=[vm], out_specs=[vm] * len(out_shape),
        scratch_shapes=[pltpu.SemaphoreType.DMA((N_DEV - 1,)), pltpu.SemaphoreType.DMA((N_DEV - 1,))],
    )(v)


def _chips_of(mx, my):
    chips = [(mx, 1 - my), (1 - mx, my), (1 - mx, 1 - my)]
    return chips, [2 * px + py for px, py in chips]


def _allgather_shards(shards):
    na = len(shards)
    halves = [s.shape[0] // 2 for s in shards]

    def body(*refs):
        ins, outs = refs[:na], refs[na:2 * na]
        s1, r1, s2, r2 = refs[2 * na:]
        mx, my, mc = _place()
        k = 2 * mx + my
        sib = (mx, my, 1 - mc)
        chips, kks = _chips_of(mx, my)

        def half(a, slot, c):
            return outs[a].at[slot, pl.ds(c * halves[a], halves[a])]

        def mine(a):
            return ins[a].at[pl.ds(mc * halves[a], halves[a])]

        sends = []
        for a in range(na):
            for d, (px, py) in enumerate(chips):
                cp = pltpu.make_async_remote_copy(
                    src_ref=mine(a), dst_ref=half(a, k, mc), send_sem=s1.at[3 * a + d], recv_sem=r1.at[3 * a + d],
                    device_id=(px, py, mc), device_id_type=MESH)
                cp.start()
                sends.append(cp)
        for a in range(na):
            for d, (px, py) in enumerate(chips):
                pltpu.make_async_remote_copy(
                    src_ref=mine(a), dst_ref=half(a, kks[d], mc), send_sem=s1.at[3 * a + d], recv_sem=r1.at[3 * a + d],
                    device_id=(px, py, mc), device_id_type=MESH).wait_recv()
                cp = pltpu.make_async_remote_copy(
                    src_ref=half(a, kks[d], mc), dst_ref=half(a, kks[d], mc), send_sem=s2.at[3 * a + d],
                    recv_sem=r2.at[3 * a + d], device_id=sib, device_id_type=MESH)
                cp.start()
                sends.append(cp)
        for a in range(na):
            for d in range(3):
                pltpu.make_async_remote_copy(
                    src_ref=half(a, kks[d], 1 - mc), dst_ref=half(a, kks[d], 1 - mc), send_sem=s2.at[3 * a + d],
                    recv_sem=r2.at[3 * a + d], device_id=sib, device_id_type=MESH).wait_recv()
        for cp in sends:
            cp.wait_send()

    return pl.pallas_call(
        body, name="allgather_weights",
        out_shape=[SDS((N_CHIPS,) + s.shape, s.dtype) for s in shards],
        in_specs=[ANY] * na, out_specs=[ANY] * na,
        scratch_shapes=[pltpu.SemaphoreType.DMA((3 * na,))] * 4,
    )(*shards)


def _rs_core_swap(grads):
    na = len(grads)
    halves = [g.shape[1] // 2 for g in grads]

    def body(*refs):
        ins, outs = refs[:na], refs[na:2 * na]
        ssem, rsem = refs[2 * na:]
        mx, my, mc = _place()
        sib = (mx, my, 1 - mc)
        sends = []
        for a in range(na):
            cp = pltpu.make_async_remote_copy(
                src_ref=ins[a].at[:, pl.ds((1 - mc) * halves[a], halves[a])], dst_ref=outs[a],
                send_sem=ssem.at[a], recv_sem=rsem.at[a], device_id=sib, device_id_type=MESH)
            cp.start()
            sends.append(cp)
        for cp in sends:
            cp.wait_recv()
        for cp in sends:
            cp.wait_send()

    return pl.pallas_call(
        body, name="rs_core_swap", out_shape=[SDS((N_CHIPS, h) + g.shape[2:], g.dtype) for g, h in zip(grads, halves)],
        in_specs=[ANY] * na, out_specs=[ANY] * na,
        scratch_shapes=[pltpu.SemaphoreType.DMA((na,))] * 2,
    )(*grads)


def _chip_exchange_copies(ins, outs, ssem, rsem):
    mx, my, mc = _place()
    k = 2 * mx + my
    chips, kks = _chips_of(mx, my)
    sends, recvs = [], []
    for a in range(len(ins)):
        for d, (px, py) in enumerate(chips):
            def copy(dst_slot):
                return pltpu.make_async_remote_copy(
                    src_ref=ins[a].at[kks[d]], dst_ref=outs[a].at[dst_slot], send_sem=ssem.at[3 * a + d],
                    recv_sem=rsem.at[3 * a + d], device_id=(px, py, mc), device_id_type=MESH)
            sends.append(copy(k))
            recvs.append(copy(kks[d]))
    return sends, recvs


def _rs_core_join(halves):
    na = len(halves)

    def body(*refs):
        ins, outs = refs[:na], refs[na:2 * na]
        ssem, rsem = refs[2 * na:]
        mx, my, mc = _place()
        sib = (mx, my, 1 - mc)
        sends = []
        for a in range(na):
            cp = pltpu.make_async_remote_copy(
                src_ref=ins[a], dst_ref=outs[a].at[mc], send_sem=ssem.at[a], recv_sem=rsem.at[a],
                device_id=sib, device_id_type=MESH)
            cp.start()
            sends.append(cp)
        for a in range(na):
            pltpu.make_async_remote_copy(
                src_ref=ins[a], dst_ref=outs[a].at[1 - mc], send_sem=ssem.at[a], recv_sem=rsem.at[a],
                device_id=sib, device_id_type=MESH).wait_recv()
        for cp in sends:
            cp.wait_send()

    return pl.pallas_call(
        body, name="rs_core_join", out_shape=[SDS((2,) + h.shape, h.dtype) for h in halves],
        in_specs=[ANY] * na, out_specs=[ANY] * na,
        scratch_shapes=[pltpu.SemaphoreType.DMA((na,))] * 2,
    )(*halves)


def _row_tile(rows, limit, mult=16):
    if rows <= limit:
        return rows
    best = None
    for t in range(mult, limit + 1, mult):
        if rows % t == 0:
            best = t
    assert best is not None, rows
    return best


def _add_half_bf16(g, b, core, name):
    _, h, cols = b.shape
    tb = _row_tile(h, 512)
    nb = h // tb

    def body(core_ref, g_ref, b_ref, o_ref):
        o_ref[...] = (g_ref[...].astype(F32) + b_ref[...].astype(F32)).astype(BF16)

    spec = pl.BlockSpec((None, tb, cols), lambda kk, i, core_ref: (kk, i, 0))
    return pl.pallas_call(
        body, name=name,
        grid_spec=pltpu.PrefetchScalarGridSpec(
            num_scalar_prefetch=1, grid=(N_CHIPS, nb),
            in_specs=[pl.BlockSpec((None, tb, cols), lambda kk, i, core_ref: (kk, core_ref[0] * nb + i, 0)), spec],
            out_specs=spec),
        out_shape=SDS(b.shape, BF16), compiler_params=_cp(("arbitrary", "arbitrary")),
    )(core, g, b)


def _sum_chips(p, name):
    _, rows, cols = p.shape
    tb = _row_tile(rows, 256)

    def body(p_ref, o_ref):
        acc = p_ref[0].astype(F32)
        for j in range(1, N_CHIPS):
            acc = acc + p_ref[j].astype(F32)
        o_ref[...] = acc

    return pl.pallas_call(
        body, name=name, grid=(rows // tb,),
        in_specs=[pl.BlockSpec((N_CHIPS, tb, cols), lambda i: (0, i, 0))],
        out_specs=pl.BlockSpec((tb, cols), lambda i: (i, 0)), out_shape=SDS((rows, cols), F32),
        compiler_params=_cp(("arbitrary",)),
    )(p)


def _adamw_math(w, g, m, v):
    m2 = ADAM_B1 * m + (1.0 - ADAM_B1) * g
    v2 = ADAM_B2 * v + (1.0 - ADAM_B2) * (g * g)
    m_hat = m2 / (1.0 - ADAM_B1 ** ADAM_STEP)
    v_hat = v2 / (1.0 - ADAM_B2 ** ADAM_STEP)
    delta = -ADAM_LR * (m_hat / (jnp.sqrt(v_hat) + ADAM_EPS) + ADAM_WD * w)
    return delta, m2, v2


def _adamw(w, g, m, v, name):
    rows, cols = w.shape
    tb = _row_tile(rows, 256, mult=8)

    def body(w_ref, g_ref, m_ref, v_ref, d_ref, m2_ref, v2_ref):
        d, m2, v2 = _adamw_math(w_ref[...], g_ref[...], m_ref[...], v_ref[...])
        d_ref[...] = d
        m2_ref[...] = m2
        v2_ref[...] = v2

    spec = pl.BlockSpec((tb, cols), lambda i: (i, 0))
    return pl.pallas_call(
        body, name=name, grid=(rows // tb,), in_specs=[spec] * 4, out_specs=[spec] * 3,
        out_shape=[SDS((rows, cols), F32)] * 3, compiler_params=_cp(("arbitrary",), 40),
    )(w, g, m, v)


def _ada_w_update(sc_all, dmod_k, w, m, v):
    rows, cols = w.shape
    tb = 256

    def body(s_ref, dm_ref, w_ref, m_ref, v_ref, g_ref, d_ref, m2_ref, v2_ref):
        g = _tn(s_ref[...].astype(BF16), dm_ref[...].astype(BF16))
        d, m2, v2 = _adamw_math(w_ref[...], g, m_ref[...], v_ref[...])
        g_ref[...] = g
        d_ref[...] = d
        m2_ref[...] = m2
        v2_ref[...] = v2

    spec = pl.BlockSpec((tb, cols), lambda i: (i, 0))
    return pl.pallas_call(
        body, name="ada_w_update", grid=(rows // tb,),
        in_specs=[pl.BlockSpec((N_DEV, tb), lambda i: (0, i)), pl.BlockSpec((N_DEV, cols), lambda i: (0, 0)), spec, spec, spec],
        out_specs=[spec] * 4, out_shape=[SDS((rows, cols), F32)] * 4, compiler_params=_cp(("arbitrary",), 40),
    )(sc_all, dmod_k, w, m, v)


def _ada_mod(c_all, w, b_k):
    rows, cols = w.shape
    tn = 512

    def body(c_ref, w_ref, b_ref, o_ref, s_ref):
        cv = c_ref[...]
        s = cv * _sigmoid(cv)
        s_ref[...] = s
        o_ref[...] = _nn(s.astype(BF16), w_ref[...].astype(BF16)) + b_ref[...]

    return pl.pallas_call(
        body, name="ada_mod", grid=(cols // tn,),
        in_specs=[pl.BlockSpec((N_DEV, rows), lambda j: (0, 0)), pl.BlockSpec((rows, tn), lambda j: (0, j)),
                  pl.BlockSpec((1, tn), lambda j: (0, j))],
        out_specs=[pl.BlockSpec((N_DEV, tn), lambda j: (0, j)), pl.BlockSpec((N_DEV, rows), lambda j: (0, 0))],
        out_shape=[SDS((N_DEV, cols), F32), SDS((N_DEV, rows), F32)], compiler_params=_cp(("arbitrary",)),
    )(c_all, w, b_k)


def _inproj(x, norm_g, scale, shift, w_t, shards):
    seq, dm = x.shape
    ncols = w_t.shape[0]
    tm, tn = min(INPROJ_TM, seq), INPROJ_TN
    ni, nj = seq // tm, ncols // tn
    na = len(shards)

    def body(x_ref, g_ref, sc_ref, sh_ref, w_ref, *rest):
        shard_refs, h_ref, u_ref, got_refs = rest[:na], rest[na], rest[na + 1], rest[na + 2:2 * na + 2]
        ssem, rsem = rest[2 * na + 2:]
        i, j = pl.program_id(0), pl.program_id(1)
        mx, my, mc = _place()
        k = 2 * mx + my
        chips, kks = _chips_of(mx, my)

        def copy(a, d, slot):
            return pltpu.make_async_remote_copy(
                src_ref=shard_refs[a], dst_ref=got_refs[a].at[slot], send_sem=ssem.at[3 * a + d],
                recv_sem=rsem.at[3 * a + d], device_id=(chips[d][0], chips[d][1], mc), device_id_type=MESH)

        @pl.when((i == 0) & (j == 0))
        def _():
            for a in range(na):
                for d in range(3):
                    copy(a, d, k).start()

        @pl.when(j == 0)
        def _():
            xv = x_ref[...]
            r = lax.rsqrt(jnp.mean(xv * xv, axis=-1, keepdims=True) + EPS)
            hv = (xv * r * g_ref[...]) * (1.0 + sc_ref[...]) + sh_ref[...]
            h_ref[...] = hv.astype(BF16)

        u_ref[...] = _nt(h_ref[...], w_ref[...]).astype(BF16)

        @pl.when((i == ni - 1) & (j == nj - 1))
        def _():
            for a in range(na):
                for d in range(3):
                    copy(a, d, kks[d]).wait_recv()
            for a in range(na):
                for d in range(3):
                    copy(a, d, k).wait_send()

    vec = pl.BlockSpec((1, dm), lambda i, j: (0, 0))
    outs = pl.pallas_call(
        body, name="inproj", grid=(ni, nj),
        in_specs=[pl.BlockSpec((tm, dm), lambda i, j: (i, 0)), vec, vec, vec, pl.BlockSpec((tn, dm), lambda i, j: (j, 0))]
                 + [ANY] * na,
        out_specs=[pl.BlockSpec((tm, dm), lambda i, j: (i, 0)), pl.BlockSpec((tm, tn), lambda i, j: (i, j))] + [ANY] * na,
        out_shape=[SDS((seq, dm), BF16), SDS((seq, ncols), BF16)] + [SDS((N_CHIPS,) + s.shape, s.dtype) for s in shards],
        scratch_shapes=[pltpu.SemaphoreType.DMA((3 * na,))] * 2,
        compiler_params=_cp(("arbitrary", "arbitrary"), 48),
    )(x, norm_g, scale, shift, w_t, *shards)
    return outs[0], outs[1], outs[2:]


HALO = 16


def _conv_taps(uc, halo, ext_ref, ts, causal):
    if causal:
        ext_ref[0:HALO, :] = halo
        ext_ref[HALO:HALO + ts, :] = uc
        return ext_ref[pl.ds(HALO - 1, ts), :], ext_ref[pl.ds(HALO - 2, ts), :]
    ext_ref[0:ts, :] = uc
    ext_ref[ts:ts + HALO, :] = halo
    return ext_ref[pl.ds(1, ts), :], ext_ref[pl.ds(2, ts), :]


def _conv_fwd(u, conv_w):
    seq = u.shape[0]
    ts = min(ROW_T, seq)
    hb = ts // HALO

    def body(xc_ref, bc_ref, cc_ref, zc_ref, xp_ref, cp_ref, w_ref, y_ref, ext_ref):
        i = pl.program_id(0)
        uc = cc_ref[...].astype(F32) * xc_ref[...].astype(F32)
        up = cp_ref[...].astype(F32) * xp_ref[...].astype(F32)
        up = jnp.where(i > 0, up, 0.0)
        u1, u2 = _conv_taps(uc, up, ext_ref, ts, True)
        conv = w_ref[0:1, :] * u2 + w_ref[1:2, :] * u1 + w_ref[2:3, :] * uc
        z = zc_ref[...].astype(F32)
        y_ref[...] = ((bc_ref[...].astype(F32) * conv) * (z * _sigmoid(z))).astype(BF16)

    def col(cb):
        return pl.BlockSpec((ts, D_CONV), lambda i: (i, cb))

    def prev(cb):
        return pl.BlockSpec((HALO, D_CONV), lambda i: (jnp.maximum(i * hb - 1, 0), cb))

    return pl.pallas_call(
        body, name="conv_fwd", grid=(seq // ts,),
        in_specs=[col(0), col(1), col(2), col(3), prev(0), prev(2), pl.BlockSpec((3, D_CONV), lambda i: (0, 0))],
        out_specs=pl.BlockSpec((ts, D_CONV), lambda i: (i, 0)), out_shape=SDS((seq, D_CONV), BF16),
        scratch_shapes=[pltpu.VMEM((ts + HALO, D_CONV), F32)],
        compiler_params=_cp(("arbitrary",), 40),
    )(u, u, u, u, u, u, conv_w)


def _rope_tables(pos_ref, freq_ref):
    ang = pos_ref[...].astype(F32) * freq_ref[...]
    lane = lax.broadcasted_iota(jnp.int32, ang.shape, 1)
    cs, sn = jnp.cos(ang), jnp.sin(ang)
    half = QK_ROPE // 2
    cos_t = jnp.where(lane < QK_ROPE, cs, 0.0)
    sin_lo = jnp.where(lane < half, sn, 0.0)
    sin_hi = jnp.where((lane >= half) & (lane < QK_ROPE), sn, 0.0)
    return cos_t, sin_lo, sin_hi


def _rope(blk, tables):
    cos_t, sin_lo, sin_hi = tables
    half = QK_ROPE // 2
    return blk * cos_t - pltpu.roll(blk, LANES - half, 1) * sin_lo + pltpu.roll(blk, half, 1) * sin_hi


def _rope_bwd(g, tables):
    cos_t, sin_lo, sin_hi = tables
    half = QK_ROPE // 2
    return g * cos_t + pltpu.roll(g, LANES - half, 1) * sin_lo - pltpu.roll(g, half, 1) * sin_hi


def _rms(v, n):
    r = lax.rsqrt(jnp.sum(v * v, axis=-1, keepdims=True) * (1.0 / n) + EPS)
    return v * r, r


def _mla_prep(u, pos, freq, q_a_g, wq, kv_a_g, wkn, wv, q_g, k_g):
    seq = u.shape[0]
    ts = min(ROW_T, seq)
    qscale = LOG2E / math.sqrt(QK_HEAD)

    def body(cq_ref, ckv_ref, kr_ref, pos_ref, freq_ref, qag_ref, wq_ref, kvag_ref, wkn_ref, wv_ref, qg_ref, kg_ref,
             q_ref, k_ref, v_ref):
        tables = _rope_tables(pos_ref, freq_ref)
        cqn, _ = _rms(cq_ref[...].astype(F32), Q_LORA)
        qp = _nn((cqn * qag_ref[...]).astype(BF16), wq_ref[...])
        qg = qg_ref[...]
        for h in range(N_HEADS):
            lo = h * HEAD_PAD
            qn, _ = _rms(qp[:, lo:lo + HEAD_PAD], QK_HEAD)
            qn = qn * qg
            q_ref[:, lo:lo + LANES] = (qn[:, :LANES] * qscale).astype(BF16)
            q_ref[:, lo + LANES:lo + HEAD_PAD] = (_rope(qn[:, LANES:], tables) * qscale).astype(BF16)
        ckvn, _ = _rms(ckv_ref[...].astype(F32), KV_LORA)
        ckvb = (ckvn * kvag_ref[...]).astype(BF16)
        kn = _nn(ckvb, wkn_ref[...])
        v_ref[...] = _nn(ckvb, wv_ref[...]).astype(BF16)
        kr = kr_ref[:, 0:LANES].astype(F32)
        ssr = jnp.sum(kr * kr, axis=-1, keepdims=True)
        kg = kg_ref[...]
        for h in range(N_HEADS):
            knh = kn[:, h * QK_NOPE:(h + 1) * QK_NOPE]
            r = lax.rsqrt((jnp.sum(knh * knh, axis=-1, keepdims=True) + ssr) * (1.0 / QK_HEAD) + EPS)
            lo = h * HEAD_PAD
            k_ref[:, lo:lo + LANES] = (knh * r * kg[:, :LANES]).astype(BF16)
            k_ref[:, lo + LANES:lo + HEAD_PAD] = _rope(kr * r * kg[:, LANES:], tables).astype(BF16)

    def full(a):
        return pl.BlockSpec(a.shape, lambda i: (0,) * a.ndim)

    return pl.pallas_call(
        body, name="mla_prep", grid=(seq // ts,),
        in_specs=[pl.BlockSpec((ts, Q_LORA), lambda i: (i, U_CQ // Q_LORA)),
                  pl.BlockSpec((ts, KV_LORA), lambda i: (i, U_CKV // KV_LORA)),
                  pl.BlockSpec((ts, KR_PAD), lambda i: (i, U_KR // KR_PAD)),
                  pl.BlockSpec((ts, 1), lambda i: (i, 0)), full(freq), full(q_a_g), full(wq), full(kv_a_g), full(wkn),
                  full(wv), full(q_g), full(k_g)],
        out_specs=[pl.BlockSpec((ts, Q_PAD), lambda i: (i, 0)), pl.BlockSpec((ts, Q_PAD), lambda i: (i, 0)),
                   pl.BlockSpec((ts, D_ATTN), lambda i: (i, 0))],
        out_shape=[SDS((seq, Q_PAD), BF16), SDS((seq, Q_PAD), BF16), SDS((seq, D_ATTN), BF16)],
        compiler_params=_cp(("arbitrary",), 48),
    )(u, u, u, pos, freq, q_a_g, wq, kv_a_g, wkn, wv, q_g, k_g)


def _causal_mask(t, tq, q0):
    return lax.broadcasted_iota(jnp.int32, (t, tq), 0) <= lax.broadcasted_iota(jnp.int32, (t, tq), 1) + q0


def _flash_fwd(q, k, v, u):
    seq = q.shape[0]
    t = min(ATT_T, seq)
    n = seq // t
    n_pairs = n * (n + 1) // 2
    za_blk = U_ZA // V_HEAD

    def body(q_ref, k_ref, v_ref, z_ref, o_ref, y_ref, lse_ref, s_a, s_b, top_a, top_b, m_all, l_all, acc_all):
        ones = jnp.ones((16, t), BF16)
        bufs = ((s_a, top_a), (s_b, top_b))

        def rows(i):
            return pl.ds(pl.multiple_of(i * t, t), t)

        def scores(i, j, buf):
            s_ref, top_ref = buf
            s = _nt(k_ref[rows(j), :], q_ref[rows(i), :])
            ahead = lax.broadcasted_iota(jnp.int32, (t, t), 0) - lax.broadcasted_iota(jnp.int32, (t, t), 1)
            s = jnp.where(ahead <= (i - j) * t, s, -jnp.inf)
            s_ref[...] = s
            top_ref[...] = jnp.max(s, axis=0, keepdims=True)

        def absorb(i, j, buf):
            s_ref, top_ref = buf
            first = j == 0
            m = jnp.where(first, -jnp.inf, m_all[i])
            l = jnp.where(first, 0.0, l_all[i])
            acc = jnp.where(first, 0.0, acc_all[i])
            m_new = jnp.maximum(m, top_ref[...])
            alpha = jnp.exp2(m - m_new)
            p = jnp.exp2((s_ref[...] - m_new).astype(BF16))
            m_all[i] = m_new
            l_all[i] = alpha * l + _nn(ones, p)[0:1, :]
            acc_all[i] = alpha * acc + _tn(v_ref[rows(j), :], p)

        def trip(width):
            def walk(_, pair):
                i, j = pair
                for w in range(width):
                    done = j == i
                    ni, nj = jnp.where(done, i + 1, i), jnp.where(done, 0, j + 1)
                    scores(jnp.minimum(ni, n - 1), nj, bufs[(w + 1) % 2])
                    absorb(i, j, bufs[w % 2])
                    i, j = ni, nj
                return i, j
            return walk

        scores(0, 0, bufs[0])
        pair = lax.fori_loop(0, n_pairs // ATT_UNROLL, trip(ATT_UNROLL), (jnp.int32(0), jnp.int32(0)))
        if n_pairs % ATT_UNROLL:
            trip(n_pairs % ATT_UNROLL)(0, pair)

        def finish(i, carry):
            l = l_all[i]
            o = (acc_all[i] * (1.0 / l)).T
            lse_ref[:, rows(i)] = m_all[i] + jnp.log2(l)
            o_ref[rows(i), :] = o.astype(BF16)
            z = z_ref[rows(i), :].astype(F32)
            y_ref[rows(i), :] = (o * (z * _sigmoid(z))).astype(BF16)
            return carry

        lax.fori_loop(0, n, finish, 0)

    def col(width, cb):
        return pl.BlockSpec((seq, width), lambda h: (0, cb + h))

    return pl.pallas_call(
        body, name="flash_fwd", grid=(N_HEADS,),
        in_specs=[col(HEAD_PAD, 0), col(HEAD_PAD, 0), col(V_HEAD, 0), col(V_HEAD, za_blk)],
        out_specs=[col(V_HEAD, 0), col(V_HEAD, 0), pl.BlockSpec((None, 1, seq), lambda h: (h, 0, 0))],
        out_shape=[SDS((seq, D_ATTN), BF16), SDS((seq, D_ATTN), BF16), SDS((N_HEADS, 1, seq), F32)],
        scratch_shapes=[pltpu.VMEM((t, t), F32)] * 2 + [pltpu.VMEM((1, t), F32)] * 2
                       + [pltpu.VMEM((n, 1, t), F32)] * 2 + [pltpu.VMEM((n, V_HEAD, t), F32)],
        compiler_params=_cp(("arbitrary",), 52),
    )(q, k, v, u)


def _outproj_loss(y_conv, y_attn, x, target, gate, w_out):
    seq, dm = x.shape
    ts = min(OUT_T, seq)
    n = seq // ts
    dmix = w_out.shape[0]

    def body(yc_ref, ya_ref, x_ref, t_ref, gate_ref, wo_hbm, dout_ref, dy_ref, dyc_ref, stats_ref, wo_ref, sem, acc_ref):
        i = pl.program_id(0)

        @pl.when(i == 0)
        def _():
            cp = pltpu.make_async_copy(wo_hbm, wo_ref, sem)
            cp.start()
            cp.wait()
            acc_ref[...] = jnp.zeros_like(acc_ref)

        y = _nn(yc_ref[...], wo_ref[0:D_CONV, :]) + _nn(ya_ref[...], wo_ref[D_CONV:dmix, :])
        gate_v = gate_ref[...]
        diff = (x_ref[...] + gate_v * y) - t_ref[...]
        dout = diff * (1.0 / dm)
        dout_ref[...] = dout
        acc_ref[0:8, :] += jnp.sum((dout * y).reshape(ts // 8, 8, dm), axis=0)
        acc_ref[8:16, :] += jnp.sum((diff * diff).reshape(ts // 8, 8, dm), axis=0)
        dy = (dout * gate_v).astype(BF16)
        dy_ref[...] = dy
        dyc_ref[...] = _nt(dy, wo_ref[...]).astype(BF16)

        @pl.when(i == n - 1)
        def _():
            stats_ref[...] = jnp.zeros_like(stats_ref)
            stats_ref[0:1, :] = jnp.sum(acc_ref[0:8, :], axis=0, keepdims=True)
            loss = jnp.sum(acc_ref[8:16, :]) * (0.5 / dm)
            stats_ref[1:2, :] = jnp.full((1, dm), loss, F32)

    row = pl.BlockSpec((ts, dm), lambda i: (i, 0))
    half = pl.BlockSpec((ts, D_CONV), lambda i: (i, 0))
    return pl.pallas_call(
        body, name="outproj_loss", grid=(n,),
        in_specs=[half, half, row, row, pl.BlockSpec((1, dm), lambda i: (0, 0)), ANY],
        out_specs=[row, row, pl.BlockSpec((ts, dmix), lambda i: (i, 0)), pl.BlockSpec((8, dm), lambda i: (0, 0))],
        out_shape=[SDS((seq, dm), F32), SDS((seq, dm), BF16), SDS((seq, dmix), BF16), SDS((8, dm), F32)],
        scratch_shapes=[pltpu.VMEM(w_out.shape, BF16), pltpu.SemaphoreType.DMA(()), pltpu.VMEM((16, dm), F32)],
        compiler_params=_cp(("arbitrary",), 52),
    )(y_conv, y_attn, x, target, gate, w_out)


def _matmul_tn(a, b, name):
    seq, m = a.shape
    n = b.shape[1]
    tm, tn, tk = min(TN_TM, m), min(TN_TN, n), min(TN_TK, seq)
    nk = seq // tk

    def body(a_ref, b_ref, o_ref, acc_ref):
        kk = pl.program_id(2)

        @pl.when(kk == 0)
        def _():
            acc_ref[...] = jnp.zeros_like(acc_ref)

        acc_ref[...] += _tn(a_ref[...], b_ref[...])

        @pl.when(kk == nk - 1)
        def _():
            o_ref[...] = acc_ref[...].astype(BF16)

    return pl.pallas_call(
        body, name=name, grid=(m // tm, n // tn, nk),
        in_specs=[pl.BlockSpec((tk, tm), lambda i, j, kk: (kk, i)), pl.BlockSpec((tk, tn), lambda i, j, kk: (kk, j))],
        out_specs=pl.BlockSpec((tm, tn), lambda i, j, kk: (i, j)), out_shape=SDS((m, n), BF16),
        scratch_shapes=[pltpu.VMEM((tm, tn), F32)],
        compiler_params=_cp(("arbitrary", "arbitrary", "arbitrary"), 40),
    )(a, b)


def _attn_gate_bwd(dycat, o, u):
    seq = o.shape[0]
    ts = min(ROW_T, seq)

    def body(dy_ref, o_ref, z_ref, dot_ref, dz_ref, dl_ref):
        dy = dy_ref[...].astype(F32)
        ov = o_ref[...].astype(F32)
        z = z_ref[...].astype(F32)
        sg = _sigmoid(z)
        do = dy * (z * sg)
        dz_ref[...] = (dy * ov * _silu_grad(z, sg)).astype(BF16)
        prod = do * ov
        ones = jnp.ones((8, V_HEAD), F32)
        for h in range(N_HEADS):
            cols = slice(h * V_HEAD, (h + 1) * V_HEAD)
            dot_ref[h] = do[:, cols].T.astype(BF16)
            rows = lax.dot_general(ones, prod[:, cols], (((1,), (1,)), ((), ())), precision=lax.Precision.HIGHEST,
                                   preferred_element_type=F32)
            dl_ref[h] = rows[0:1, :]

    blk = pl.BlockSpec((ts, D_ATTN), lambda i: (i, 0))
    return pl.pallas_call(
        body, name="attn_gate_bwd", grid=(seq // ts,),
        in_specs=[pl.BlockSpec((ts, D_ATTN), lambda i: (i, 1)), blk, pl.BlockSpec((ts, D_ATTN), lambda i: (i, U_ZA // D_ATTN))],
        out_specs=[pl.BlockSpec((N_HEADS, V_HEAD, ts), lambda i: (0, 0, i)), blk,
                   pl.BlockSpec((N_HEADS, 1, ts), lambda i: (0, 0, i))],
        out_shape=[SDS((N_HEADS, V_HEAD, seq), BF16), SDS((seq, D_ATTN), BF16), SDS((N_HEADS, 1, seq), F32)],
        compiler_params=_cp(("arbitrary",), 40),
    )(dycat, o, u)


def _flash_bwd(q, k, v, do_t, lse, delta):
    seq = q.shape[0]
    t = min(ATT_T, seq)
    n = seq // t
    n_pairs = n * (n + 1) // 2

    def body(k_ref, v_ref, q_ref, dot_ref, lse_ref, dl_ref, dq_ref, dk_ref, dv_ref, s_a, s_b, dp_a, dp_b, dq_acc, dk_acc,
             dvt_acc):
        bufs = ((s_a, dp_a), (s_b, dp_b))

        def rows(i):
            return pl.ds(pl.multiple_of(i * t, t), t)

        def products(j, i, buf):
            s_ref, dp_ref = buf
            s = _nt(k_ref[rows(j), :], q_ref[rows(i), :])
            ahead = lax.broadcasted_iota(jnp.int32, (t, t), 0) - lax.broadcasted_iota(jnp.int32, (t, t), 1)
            s_ref[...] = jnp.where(ahead <= (i - j) * t, s, -jnp.inf)
            dp_ref[...] = _nn(v_ref[rows(j), :], dot_ref[:, rows(i)])

        def absorb(j, i, buf):
            s_ref, dp_ref = buf
            first = i == j
            p = jnp.exp2((s_ref[...] - lse_ref[:, rows(i)]).astype(BF16))
            dvt = jnp.where(first, 0.0, dvt_acc[...]) + _nt(dot_ref[:, rows(i)], p)
            ds = p * (dp_ref[...] - dl_ref[:, rows(i)]).astype(BF16)
            dk = jnp.where(first, 0.0, dk_acc[...]) + _nn(ds, q_ref[rows(i), :])
            dq_acc[rows(i), :] += _tn(ds, k_ref[rows(j), :])
            dvt_acc[...] = dvt
            dk_acc[...] = dk
            dk_ref[rows(j), :] = dk.astype(BF16)
            dv_ref[rows(j), :] = dvt.T.astype(BF16)

        def trip(width):
            def walk(_, pair):
                j, i = pair
                for w in range(width):
                    done = i == n - 1
                    nj = jnp.where(done, j + 1, j)
                    ni = jnp.where(done, j + 1, i + 1)
                    absorb(j, i, bufs[w % 2])
                    products(jnp.minimum(nj, n - 1), jnp.minimum(ni, n - 1), bufs[(w + 1) % 2])
                    j, i = nj, ni
                return j, i
            return walk

        dq_acc[...] = jnp.zeros_like(dq_acc)
        products(0, 0, bufs[0])
        pair = lax.fori_loop(0, n_pairs // ATT_UNROLL, trip(ATT_UNROLL), (jnp.int32(0), jnp.int32(0)))
        if n_pairs % ATT_UNROLL:
            trip(n_pairs % ATT_UNROLL)(0, pair)

        def finish(i, carry):
            dq_ref[rows(i), :] = dq_acc[rows(i), :].astype(BF16)
            return carry

        lax.fori_loop(0, n, finish, 0)

    def col(width):
        return pl.BlockSpec((seq, width), lambda h: (0, h))

    row = pl.BlockSpec((None, 1, seq), lambda h: (h, 0, 0))
    return pl.pallas_call(
        body, name="flash_bwd", grid=(N_HEADS,),
        in_specs=[col(HEAD_PAD), col(V_HEAD), col(HEAD_PAD), pl.BlockSpec((None, V_HEAD, seq), lambda h: (h, 0, 0)), row, row],
        out_specs=[col(HEAD_PAD), col(HEAD_PAD), col(V_HEAD)],
        out_shape=[SDS((seq, Q_PAD), BF16), SDS((seq, Q_PAD), BF16), SDS((seq, D_ATTN), BF16)],
        scratch_shapes=[pltpu.VMEM((t, t), F32)] * 4
                       + [pltpu.VMEM((seq, HEAD_PAD), F32), pltpu.VMEM((t, HEAD_PAD), F32), pltpu.VMEM((V_HEAD, t), F32)],
        compiler_params=_cp(("arbitrary",), 60),
    )(k, v, q, do_t, lse, delta)


SG_QAG, SG_KVAG, SG_QG, SG_KG, SG_COLS = 0, Q_LORA, Q_LORA + KV_LORA, Q_LORA + KV_LORA + HEAD_PAD, D_MODEL


def _mla_bwd(dq, dk, dv, u, pos, freq, q_a_g, wq, kv_a_g, wkn, wv, q_g, k_g):
    seq = u.shape[0]
    ts = min(ROW_T, seq)
    n = seq // ts
    qscale = 1.0 / math.sqrt(QK_HEAD)

    def body(dq_ref, dk_ref, dv_ref, cq_ref, ckv_ref, kr_ref, pos_ref, freq_ref, qag_ref, wq_ref, kvag_ref, wkn_ref,
             wv_ref, qg_ref, kg_ref, du_ref, dwq_ref, dwkn_ref, dwv_ref, sg_ref, dqp_ref, dkn_ref):
        i = pl.program_id(0)

        @pl.when(i == 0)
        def _():
            dwq_ref[...] = jnp.zeros_like(dwq_ref)
            dwkn_ref[...] = jnp.zeros_like(dwkn_ref)
            dwv_ref[...] = jnp.zeros_like(dwv_ref)
            sg_ref[...] = jnp.zeros_like(sg_ref)

        tables = _rope_tables(pos_ref, freq_ref)

        cq = cq_ref[...].astype(F32)
        cqn, rq = _rms(cq, Q_LORA)
        qag = qag_ref[...]
        cqb = (cqn * qag).astype(BF16)
        qp = _nn(cqb, wq_ref[...])
        qg = qg_ref[...]
        dqg = jnp.zeros((1, HEAD_PAD), F32)
        for h in range(N_HEADS):
            lo = h * HEAD_PAD
            xn, r = _rms(qp[:, lo:lo + HEAD_PAD], QK_HEAD)
            g = jnp.concatenate([dq_ref[:, lo:lo + LANES].astype(F32),
                                 _rope_bwd(dq_ref[:, lo + LANES:lo + HEAD_PAD].astype(F32), tables)], axis=-1) * qscale
            dqg = dqg + jnp.sum(g * xn, axis=0, keepdims=True)
            gy = g * qg
            mean = jnp.sum(gy * xn, axis=-1, keepdims=True) * (1.0 / QK_HEAD)
            dqp_ref[:, lo:lo + HEAD_PAD] = (r * (gy - xn * mean)).astype(BF16)
        dqp = dqp_ref[...]
        dwq_ref[...] += _tn(cqb, dqp)
        dcqn = _nt(dqp, wq_ref[...])
        sg_ref[0:1, SG_QAG:SG_QAG + Q_LORA] += jnp.sum(dcqn * cqn, axis=0, keepdims=True)
        sg_ref[0:1, SG_QG:SG_QG + HEAD_PAD] += dqg
        gy = dcqn * qag
        mean = jnp.sum(gy * cqn, axis=-1, keepdims=True) * (1.0 / Q_LORA)
        du_ref[:, 0:Q_LORA] = (rq * (gy - cqn * mean)).astype(BF16)

        ckv = ckv_ref[...].astype(F32)
        ckvn, rkv = _rms(ckv, KV_LORA)
        kvag = kvag_ref[...]
        ckvb = (ckvn * kvag).astype(BF16)
        kn = _nn(ckvb, wkn_ref[...])
        kr = kr_ref[:, 0:LANES].astype(F32)
        ssr = jnp.sum(kr * kr, axis=-1, keepdims=True)
        kg = kg_ref[...]
        kg_n, kg_r = kg[:, :LANES] * LN2, kg[:, LANES:] * LN2
        dkg_n = jnp.zeros((1, LANES), F32)
        dkg_r = jnp.zeros((1, LANES), F32)
        dkr = jnp.zeros((ts, LANES), F32)
        for h in range(N_HEADS):
            knh = kn[:, h * QK_NOPE:(h + 1) * QK_NOPE]
            r = lax.rsqrt((jnp.sum(knh * knh, axis=-1, keepdims=True) + ssr) * (1.0 / QK_HEAD) + EPS)
            xn_n, xn_r = knh * r, kr * r
            lo = h * HEAD_PAD
            g_n = dk_ref[:, lo:lo + LANES].astype(F32)
            g_r = _rope_bwd(dk_ref[:, lo + LANES:lo + HEAD_PAD].astype(F32), tables)
            dkg_n = dkg_n + jnp.sum(g_n * xn_n, axis=0, keepdims=True)
            dkg_r = dkg_r + jnp.sum(g_r * xn_r, axis=0, keepdims=True)
            gy_n, gy_r = g_n * kg_n, g_r * kg_r
            mean = (jnp.sum(gy_n * xn_n, axis=-1, keepdims=True) + jnp.sum(gy_r * xn_r, axis=-1, keepdims=True)) * (1.0 / QK_HEAD)
            dkn_ref[:, h * QK_NOPE:(h + 1) * QK_NOPE] = (r * (gy_n - xn_n * mean)).astype(BF16)
            dkr = dkr + r * (gy_r - xn_r * mean)
        dkn = dkn_ref[...]
        dvv = dv_ref[...]
        dwkn_ref[...] += _tn(ckvb, dkn)
        dwv_ref[...] += _tn(ckvb, dvv)
        dckvn = _nt(dkn, wkn_ref[...]) + _nt(dvv, wv_ref[...])
        sg_ref[0:1, SG_KVAG:SG_KVAG + KV_LORA] += jnp.sum(dckvn * ckvn, axis=0, keepdims=True)
        sg_ref[0:1, SG_KG:SG_KG + LANES] += dkg_n * LN2
        sg_ref[0:1, SG_KG + LANES:SG_KG + HEAD_PAD] += dkg_r * LN2
        gy = dckvn * kvag
        mean = jnp.sum(gy * ckvn, axis=-1, keepdims=True) * (1.0 / KV_LORA)
        du_ref[:, Q_LORA:Q_LORA + KV_LORA] = (rkv * (gy - ckvn * mean)).astype(BF16)
        du_ref[:, Q_LORA + KV_LORA:Q_LORA + KV_LORA + LANES] = dkr.astype(BF16)
        du_ref[:, Q_LORA + KV_LORA + LANES:MLA_COLS] = jnp.zeros((ts, LANES), BF16)

    def full(a):
        return pl.BlockSpec(a.shape, lambda i: (0,) * a.ndim)

    wide = pl.BlockSpec((ts, Q_PAD), lambda i: (i, 0))
    return pl.pallas_call(
        body, name="mla_bwd", grid=(n,),
        in_specs=[wide, wide, pl.BlockSpec((ts, D_ATTN), lambda i: (i, 0)),
                  pl.BlockSpec((ts, Q_LORA), lambda i: (i, U_CQ // Q_LORA)),
                  pl.BlockSpec((ts, KV_LORA), lambda i: (i, U_CKV // KV_LORA)),
                  pl.BlockSpec((ts, KR_PAD), lambda i: (i, U_KR // KR_PAD)),
                  pl.BlockSpec((ts, 1), lambda i: (i, 0)), full(freq), full(q_a_g), full(wq), full(kv_a_g), full(wkn),
                  full(wv), full(q_g), full(k_g)],
        out_specs=[pl.BlockSpec((ts, MLA_COLS), lambda i: (i, 0)), pl.BlockSpec((Q_LORA, Q_PAD), lambda i: (0, 0)),
                   pl.BlockSpec((KV_LORA, D_ATTN), lambda i: (0, 0)), pl.BlockSpec((KV_LORA, D_ATTN), lambda i: (0, 0)),
                   pl.BlockSpec((8, SG_COLS), lambda i: (0, 0))],
        out_shape=[SDS((seq, MLA_COLS), BF16), SDS((Q_LORA, Q_PAD), F32), SDS((KV_LORA, D_ATTN), F32),
                   SDS((KV_LORA, D_ATTN), F32), SDS((8, SG_COLS), F32)],
        scratch_shapes=[pltpu.VMEM((ts, Q_PAD), BF16), pltpu.VMEM((ts, D_ATTN), BF16)],
        compiler_params=_cp(("arbitrary",), 56),
    )(dq, dk, dv, u, u, u, pos, freq, q_a_g, wq, kv_a_g, wkn, wv, q_g, k_g)


def _conv_bwd(dycat, u, conv_w):
    seq = u.shape[0]
    ts = min(ROW_T, seq)
    n = seq // ts
    hb = ts // HALO

    def body(dy_ref, xc_ref, bc_ref, cc_ref, zc_ref, xp_ref, cp_ref, dyn_ref, bn_ref, zn_ref, w_ref,
             du_ref, dw_ref, ext_ref):
        i = pl.program_id(0)

        @pl.when(i == 0)
        def _():
            dw_ref[...] = jnp.zeros_like(dw_ref)

        xc = xc_ref[...].astype(F32)
        cc = cc_ref[...].astype(F32)
        uc = cc * xc
        up = jnp.where(i > 0, cp_ref[...].astype(F32) * xp_ref[...].astype(F32), 0.0)
        u1, u2 = _conv_taps(uc, up, ext_ref, ts, True)
        w0, w1, w2 = w_ref[0:1, :], w_ref[1:2, :], w_ref[2:3, :]
        conv = w0 * u2 + w1 * u1 + w2 * uc
        z = zc_ref[...].astype(F32)
        sg = _sigmoid(z)
        sz = z * sg
        b = bc_ref[...].astype(F32)
        dy = dy_ref[...].astype(F32)
        du_ref[:, 3 * D_CONV:4 * D_CONV] = (dy * (b * conv) * _silu_grad(z, sg)).astype(BF16)
        du_ref[:, D_CONV:2 * D_CONV] = (dy * sz * conv).astype(BF16)
        dconv = dy * sz * b
        dw_ref[0:1, :] += jnp.sum(dconv * u2, axis=0, keepdims=True)
        dw_ref[1:2, :] += jnp.sum(dconv * u1, axis=0, keepdims=True)
        dw_ref[2:3, :] += jnp.sum(dconv * uc, axis=0, keepdims=True)
        zn = zn_ref[...].astype(F32)
        dnext = dyn_ref[...].astype(F32) * (zn * _sigmoid(zn)) * bn_ref[...].astype(F32)
        dnext = jnp.where(i < n - 1, dnext, 0.0)
        d1, d2 = _conv_taps(dconv, dnext, ext_ref, ts, False)
        du = w2 * dconv + w1 * d1 + w0 * d2
        du_ref[:, 2 * D_CONV:3 * D_CONV] = (du * xc).astype(BF16)
        du_ref[:, 0:D_CONV] = (du * cc).astype(BF16)

    def col(cb):
        return pl.BlockSpec((ts, D_CONV), lambda i: (i, cb))

    def prev(cb):
        return pl.BlockSpec((HALO, D_CONV), lambda i: (jnp.maximum(i * hb - 1, 0), cb))

    def nxt(cb):
        return pl.BlockSpec((HALO, D_CONV), lambda i: (jnp.minimum((i + 1) * hb, n * hb - 1), cb))

    return pl.pallas_call(
        body, name="conv_bwd", grid=(n,),
        in_specs=[col(0), col(0), col(1), col(2), col(3), prev(0), prev(2), nxt(0), nxt(1), nxt(3),
                  pl.BlockSpec((3, D_CONV), lambda i: (0, 0))],
        out_specs=[pl.BlockSpec((ts, 4 * D_CONV), lambda i: (i, 0)), pl.BlockSpec((8, D_CONV), lambda i: (0, 0))],
        out_shape=[SDS((seq, 4 * D_CONV), BF16), SDS((8, D_CONV), F32)],
        scratch_shapes=[pltpu.VMEM((ts + HALO, D_CONV), F32)],
        compiler_params=_cp(("arbitrary",), 48),
    )(dycat, u, u, u, u, u, u, dycat, u, u, conv_w)


def _inproj_bwd(du_conv, du_za, du_mla, w_t, parts):
    seq = du_conv.shape[0]
    dm = w_t.shape[1]
    tm, tn = min(DH_TM, seq), DH_TN
    ni, nj = seq // tm, dm // tn
    na = len(parts)

    def body(dc_ref, dz_ref, dm_ref, w_ref, *rest):
        part_refs, o_ref, recv_refs = rest[:na], rest[na], rest[na + 1:2 * na + 1]
        ssem, rsem = rest[2 * na + 1:]
        i, j = pl.program_id(0), pl.program_id(1)
        sends, recvs = _chip_exchange_copies(part_refs, recv_refs, ssem, rsem)

        @pl.when((i == 0) & (j == 0))
        def _():
            for cp in sends:
                cp.start()

        acc = _nn(dc_ref[...], w_ref[0:U_ZA, :])
        acc = acc + _nn(dz_ref[...], w_ref[U_ZA:U_CQ, :])
        acc = acc + _nn(dm_ref[...], w_ref[U_CQ:U_COLS, :])
        o_ref[...] = acc

        @pl.when((i == ni - 1) & (j == nj - 1))
        def _():
            for cp in recvs:
                cp.wait_recv()
            for cp in sends:
                cp.wait_send()

    outs = pl.pallas_call(
        body, name="inproj_bwd", grid=(ni, nj),
        in_specs=[pl.BlockSpec((tm, U_ZA), lambda i, j: (i, 0)), pl.BlockSpec((tm, D_ATTN), lambda i, j: (i, 0)),
                  pl.BlockSpec((tm, MLA_COLS), lambda i, j: (i, 0)), pl.BlockSpec((U_COLS, tn), lambda i, j: (0, j))]
                 + [ANY] * na,
        out_specs=[pl.BlockSpec((tm, tn), lambda i, j: (i, j))] + [ANY] * na,
        out_shape=[SDS((seq, dm), F32)] + [SDS(p.shape, p.dtype) for p in parts],
        scratch_shapes=[pltpu.SemaphoreType.DMA((3 * na,))] * 2,
        compiler_params=_cp(("arbitrary", "arbitrary"), 48),
    )(du_conv, du_za, du_mla, w_t, *parts)
    return outs[0], outs[1:]


def _prenorm_bwd(x, dh, dout, norm_g, scale):
    seq, dm = x.shape
    ts = min(ROW_T, seq)
    n = seq // ts

    def body(x_ref, dh_ref, dout_ref, g_ref, sc_ref, gx_ref, st_ref, acc_ref):
        i = pl.program_id(0)

        @pl.when(i == 0)
        def _():
            acc_ref[...] = jnp.zeros_like(acc_ref)

        xv = x_ref[...]
        xn, r = _rms(xv, dm)
        dh_v = dh_ref[...]
        gv = g_ref[...]
        one_sc = 1.0 + sc_ref[...]

        def fold(a):
            return jnp.sum(a.reshape(ts // 8, 8, dm), axis=0)

        acc_ref[0:8, :] += fold(dh_v)
        acc_ref[8:16, :] += fold(dh_v * (xn * gv))
        dxg = dh_v * one_sc
        acc_ref[16:24, :] += fold(dxg * xn)
        dxn = dxg * gv
        mean = jnp.sum(dxn * xn, axis=-1, keepdims=True) * (1.0 / dm)
        gx_ref[...] = dout_ref[...] + r * (dxn - xn * mean)

        @pl.when(i == n - 1)
        def _():
            st_ref[...] = jnp.zeros_like(st_ref)
            for k in range(3):
                st_ref[k:k + 1, :] = jnp.sum(acc_ref[8 * k:8 * k + 8, :], axis=0, keepdims=True)

    row = pl.BlockSpec((ts, dm), lambda i: (i, 0))
    vec = pl.BlockSpec((1, dm), lambda i: (0, 0))
    return pl.pallas_call(
        body, name="prenorm_bwd", grid=(n,), in_specs=[row, row, row, vec, vec],
        out_specs=[row, pl.BlockSpec((8, dm), lambda i: (0, 0))],
        out_shape=[SDS((seq, dm), F32), SDS((8, dm), F32)],
        scratch_shapes=[pltpu.VMEM((24, dm), F32)], input_output_aliases={2: 0},
        compiler_params=_cp(("arbitrary",), 52),
    )(x, dh, dout, norm_g, scale)


def _unshard_cols(g):
    return jnp.transpose(g, (1, 0, 2)).reshape(g.shape[1], -1)


def _shard_cols(w):
    r = w.shape[0]
    return jnp.transpose(w.reshape(r, N_CHIPS, -1), (1, 0, 2))


W_IN_COLS = 4 * D_CONV + Q_LORA + KV_LORA + QK_ROPE + D_ATTN
SHARD_ROWS = W_IN_COLS // N_CHIPS
SHARD_PAD = 1536


def _w_in_pieces():
    c4 = 4 * D_CONV
    groups = [(0, c4, 0), (c4, c4 + Q_LORA, U_CQ), (c4 + Q_LORA, c4 + Q_LORA + KV_LORA, U_CKV),
              (c4 + Q_LORA + KV_LORA, W_IN_COLS - D_ATTN, U_KR), (W_IN_COLS - D_ATTN, W_IN_COLS, U_ZA)]
    pieces = []
    for lo, hi, my in groups:
        for chip in range(N_CHIPS):
            a, b = max(lo, chip * SHARD_ROWS), min(hi, (chip + 1) * SHARD_ROWS)
            if a < b:
                pieces.append((chip, a - chip * SHARD_ROWS, b - a, my + a - lo))
    return pieces


def _w_t_to_my(g):
    w = jnp.zeros((U_COLS, g.shape[2]), g.dtype)
    for chip, row, n, my in _w_in_pieces():
        w = lax.dynamic_update_slice(w, g[chip, row:row + n], (my, 0))
    return w


def _w_t_from_my(g_conv, g_za, g_mla):
    w = jnp.zeros((N_CHIPS, SHARD_PAD, g_conv.shape[1]), g_conv.dtype)
    for chip, row, n, my in _w_in_pieces():
        src, base = (g_conv, 0) if my < U_ZA else (g_za, U_ZA) if my < U_CQ else (g_mla, U_CQ)
        w = lax.dynamic_update_slice(w, src[my - base:my - base + n][None], (chip, row, 0))
    return w


def _heads_pad(w):
    r = w.shape[0]
    w3 = w.reshape(r, N_HEADS, QK_HEAD)
    return jnp.pad(w3, ((0, 0), (0, 0), (0, HEAD_PAD - QK_HEAD))).reshape(r, Q_PAD)


def _heads_unpad(w):
    r = w.shape[0]
    return w.reshape(r, N_HEADS, HEAD_PAD)[:, :, :QK_HEAD].reshape(r, N_HEADS * QK_HEAD)


def kernel(x, c, positions, ada_w, ada_b, norm_g, w_in, conv_w, q_a_g, w_q_b, kv_a_g, w_kv_b, q_g, k_g, w_out, loss_target, m_ada_w, m_ada_b, m_norm_g, m_w_in, m_conv_w, m_q_a_g, m_w_q_b, m_kv_a_g, m_w_kv_b, m_q_g, m_k_g, m_w_out, v_ada_w, v_ada_b, v_norm_g, v_w_in, v_conv_w, v_q_a_g, v_w_q_b, v_kv_a_g, v_w_kv_b, v_q_g, v_k_g, v_w_out):
    mx, my, mc = _place()
    chip = 2 * mx + my
    me = 2 * chip + mc
    seq = x.shape[1]
    x2, t2 = x[0], loss_target[0]
    cw_cols = conv_w.shape[2]

    small = jnp.zeros((8, D_MODEL), F32)
    small = small.at[0].set(c[0])
    small = small.at[1:4, :cw_cols].set(conv_w[0])
    small_all = _gather8(small, "gather_c_conv", False)[0]
    c_all = small_all[:, 0, :]
    conv_full = jnp.transpose(small_all.reshape(N_CHIPS, 2, 8, D_MODEL)[:, 0, 1:4, :cw_cols], (1, 0, 2)).reshape(3, D_CONV)

    ada_cols = ada_w.shape[2]
    b_k = lax.dynamic_slice(ada_b, (0, chip * ada_cols), (1, ada_cols))
    mod_k, sc_all = _ada_mod(c_all, ada_w[0], b_k)
    mod_all = _gather8(mod_k, "gather_mod", False)[0]
    mod_row = lax.dynamic_slice(mod_all.reshape(N_CHIPS, 2, N_DEV, ada_cols), (0, mc, me, 0), (N_CHIPS, 1, 1, ada_cols))
    mod_row = mod_row.reshape(3, D_MODEL)
    shift, scale, gate = mod_row[0:1], mod_row[1:2], mod_row[2:3]

    def own_slot(g, s):
        return lax.dynamic_update_slice(g, s[None], (chip, 0, 0))

    shard_in = jnp.pad(jnp.transpose(w_in[0].astype(BF16)), ((0, SHARD_PAD - SHARD_ROWS), (0, 0)))
    g_in = own_slot(_allgather_shards([shard_in])[0], shard_in)
    w_t = _w_t_to_my(g_in)

    later = [w_q_b[0].astype(BF16), w_kv_b[0].astype(BF16), w_out[0].astype(BF16)]
    h, u, got = _inproj(x2, norm_g, scale, shift, w_t, later)
    g_q, g_kv, g_out = [own_slot(g, s) for g, s in zip(got, later)]
    wq = _heads_pad(_unshard_cols(g_q))
    wkv = _unshard_cols(g_kv).reshape(KV_LORA, N_HEADS, QK_NOPE + V_HEAD)
    wkn = wkv[:, :, :QK_NOPE].reshape(KV_LORA, N_HEADS * QK_NOPE)
    wv = wkv[:, :, QK_NOPE:].reshape(KV_LORA, D_ATTN)
    wo = g_out.reshape(N_CHIPS * g_out.shape[1], D_MODEL)
    y_conv = _conv_fwd(u, conv_full)
    pos = positions.reshape(seq, 1)
    inv_freq = ROPE_BASE ** (-jnp.arange(0, QK_ROPE, 2, dtype=F32) / QK_ROPE)
    freq = jnp.concatenate([inv_freq, inv_freq, jnp.zeros((LANES - QK_ROPE,), F32)]).reshape(1, LANES)
    q_g_pad = jnp.pad(q_g, ((0, 0), (0, HEAD_PAD - QK_HEAD)))
    k_g_pad = jnp.pad(k_g, ((0, 0), (0, HEAD_PAD - QK_HEAD)))
    q, k, v = _mla_prep(u, pos, freq, q_a_g, wq, kv_a_g, wkn, wv, q_g_pad, k_g_pad)
    o, y_attn, lse = _flash_fwd(q, k, v, u)
    dout, dy, dycat, st_out = _outproj_loss(y_conv, y_attn, x2, t2, gate, wo)

    dw_out = jnp.concatenate([_matmul_tn(y_conv, dy, "dw_out_conv"), _matmul_tn(y_attn, dy, "dw_out_attn")], axis=0)
    do_t, du_za, delta = _attn_gate_bwd(dycat, o, u)
    dq, dk, dv = _flash_bwd(q, k, v, do_t, lse, delta)
    du_mla, dwq, dwkn, dwv, sg_mla = _mla_bwd(dq, dk, dv, u, pos, freq, q_a_g, wq, kv_a_g, wkn, wv, q_g_pad, k_g_pad)
    du_conv, dconv_w = _conv_bwd(dycat, u, conv_full)
    dw_conv = _matmul_tn(du_conv, h, "dw_in_conv")
    dw_za = _matmul_tn(du_za, h, "dw_in_za")
    dw_mla = _matmul_tn(du_mla, h, "dw_in_mla")

    dw_q_nat = _heads_unpad(dwq).astype(BF16)
    dw_kv_nat = jnp.concatenate([dwkn.reshape(KV_LORA, N_HEADS, QK_NOPE), dwv.reshape(KV_LORA, N_HEADS, V_HEAD)],
                                axis=2).reshape(KV_LORA, N_HEADS * (QK_NOPE + V_HEAD)).astype(BF16)
    grads = [_w_t_from_my(dw_conv, dw_za, dw_mla), _shard_cols(dw_q_nat), _shard_cols(dw_kv_nat),
             dw_out.reshape(N_CHIPS, dw_out.shape[0] // N_CHIPS, D_MODEL)]
    theirs = _rs_core_swap(grads)
    names = ["w_in", "w_q_b", "w_kv_b", "w_out"]
    core = jnp.reshape(mc, (1,)).astype(jnp.int32)
    parts = [_add_half_bf16(g, b, core, "rs_add_" + nm) for g, b, nm in zip(grads, theirs, names)]
    dh, recv = _inproj_bwd(du_conv, du_za, du_mla, w_t, parts)
    recv = [lax.dynamic_update_slice(r, lax.dynamic_slice(p, (chip, 0, 0), (1,) + p.shape[1:]), (chip, 0, 0))
            for r, p in zip(recv, parts)]
    halves = [_sum_chips(p, "rs_sum_" + nm) for p, nm in zip(recv, names)]
    joined = _rs_core_join(halves)
    joined = [lax.dynamic_update_slice(j, hf[None], (mc, 0, 0)) for j, hf in zip(joined, halves)]
    g_big = [j.reshape(2 * j.shape[1], j.shape[2]) for j in joined]
    grad_x, st_in = _prenorm_bwd(x2, dh, dout, norm_g, scale)

    sgrad = jnp.zeros((8, D_MODEL), F32)
    sgrad = sgrad.at[0:2].set(st_in[0:2])
    sgrad = sgrad.at[2].set(st_out[0])
    sgrad = sgrad.at[3].set(st_in[2])
    sgrad = sgrad.at[4, :D_CONV].set(dconv_w[0]).at[4, D_CONV:].set(dconv_w[1])
    sgrad = sgrad.at[5, :D_CONV].set(dconv_w[2]).at[5, D_CONV:].set(sg_mla[0, :D_CONV])
    sgrad = sgrad.at[6, :HEAD_PAD].set(sg_mla[0, SG_KG:SG_KG + HEAD_PAD])
    sgrad = sgrad.at[7].set(st_out[1])
    sg_all, sg_sum = _gather8(sgrad, "gather_small_grads", True)
    loss = sg_sum[7, 0]
    g_ada_b = sg_sum[0:3].reshape(1, 3 * D_MODEL)
    g_norm_g = sg_sum[3:4]
    conv_sum = jnp.stack([sg_sum[4, :D_CONV], sg_sum[4, D_CONV:], sg_sum[5, :D_CONV]])
    g_conv_w = lax.dynamic_slice(conv_sum, (0, chip * cw_cols), (3, cw_cols))
    g_q_a_g = sg_sum[5:6, D_CONV + SG_QAG:D_CONV + SG_QAG + Q_LORA]
    g_kv_a_g = sg_sum[5:6, D_CONV + SG_KVAG:D_CONV + SG_KVAG + KV_LORA]
    g_q_g = sg_sum[5:6, D_CONV + SG_QG:D_CONV + SG_QG + QK_HEAD]
    g_k_g = sg_sum[6:7, :QK_HEAD]
    dmod_k = lax.dynamic_slice(sg_all[:, 0:3, :].reshape(N_DEV, 3 * D_MODEL), (0, chip * ada_cols), (N_DEV, ada_cols))

    g_ada_w, d_ada_w, nm_ada_w, nv_ada_w = _ada_w_update(sc_all, dmod_k, ada_w[0], m_ada_w[0], v_ada_w[0])
    upd = {}
    big = {"w_q_b": (w_q_b, m_w_q_b, v_w_q_b), "w_kv_b": (w_kv_b, m_w_kv_b, v_w_kv_b), "w_out": (w_out, m_w_out, v_w_out)}
    for nm, g in zip(names[1:], g_big[1:]):
        w_, m_, v_ = big[nm]
        upd[nm] = (g,) + tuple(_adamw(w_[0], g, m_[0], v_[0], "adamw_" + nm))
    g_w_in = jnp.transpose(g_big[0][:SHARD_ROWS])
    upd["w_in"] = (g_w_in,) + tuple(_adamw(w_in[0], g_w_in, m_w_in[0], v_w_in[0], "adamw_w_in"))
    small_w = {"ada_b": (ada_b, m_ada_b, v_ada_b, g_ada_b), "norm_g": (norm_g, m_norm_g, v_norm_g, g_norm_g),
               "conv_w": (conv_w[0], m_conv_w[0], v_conv_w[0], g_conv_w), "q_a_g": (q_a_g, m_q_a_g, v_q_a_g, g_q_a_g),
               "kv_a_g": (kv_a_g, m_kv_a_g, v_kv_a_g, g_kv_a_g), "q_g": (q_g, m_q_g, v_q_g, g_q_g),
               "k_g": (k_g, m_k_g, v_k_g, g_k_g)}
    for nm, (w_, m_, v_, g) in small_w.items():
        upd[nm] = (g,) + tuple(_adamw(w_, g, m_, v_, "adamw_" + nm))
    upd["ada_w"] = (g_ada_w, d_ada_w, nm_ada_w, nv_ada_w)

    order = ["ada_w", "ada_b", "norm_g", "w_in", "conv_w", "q_a_g", "w_q_b", "kv_a_g", "w_kv_b", "q_g", "k_g", "w_out"]
    lead1 = {"ada_w", "w_in", "conv_w", "w_q_b", "w_kv_b", "w_out"}

    def shaped(nm, a):
        return a[None] if nm in lead1 else a

    outs = [loss, grad_x[None]]
    for idx in range(4):
        outs += [shaped(nm, upd[nm][idx]) for nm in order]
    return tuple(outs)
```

```python
import functools
import math

import jax
import jax.numpy as jnp
from jax import lax
from jax.experimental import pallas as pl
from jax.experimental.pallas import tpu as pltpu

F32 = jnp.float32
BF16 = jnp.bfloat16
MESH = pl.DeviceIdType.MESH
SDS = jax.ShapeDtypeStruct
ANY = pl.BlockSpec(memory_space=pl.ANY)

D_MODEL = 2048
D_CONV = 1024
N_HEADS = 8
QK_NOPE = 128
QK_ROPE = 64
QK_HEAD = QK_NOPE + QK_ROPE
V_HEAD = 128
D_ATTN = N_HEADS * V_HEAD
Q_LORA = 512
KV_LORA = 256
ROPE_BASE = 10000.0
EPS = 1e-6
LOG2E = math.log2(math.e)
LN2 = math.log(2.0)
ADAM_LR, ADAM_B1, ADAM_B2, ADAM_EPS, ADAM_WD, ADAM_STEP = 0.001, 0.9, 0.999, 1e-08, 0.01, 10
N_CHIPS = 4
N_DEV = 8

LANES = 128
V7X_VMEM_BYTES = 64 * 1024 * 1024
MIB = 1024 * 1024

HEAD_PAD = 256
Q_PAD = N_HEADS * HEAD_PAD
U_ZA = 4 * D_CONV
U_CQ = U_ZA + D_ATTN
U_CKV = U_CQ + Q_LORA
U_KR = U_CKV + KV_LORA
KR_PAD = 256
U_COLS = U_KR + KR_PAD
MLA_COLS = Q_LORA + KV_LORA + KR_PAD

ATT_T = 512
INPROJ_TM, INPROJ_TN = 1024, 512
ROW_T = 512
OUT_T = 256
DH_TM, DH_TN = 512, 512
TN_TM, TN_TN, TN_TK = 1024, 1024, 2048
ATT_UNROLL = 8


def _cp(sem=None, vmem_mib=None, **kw):
    if sem is not None:
        kw["dimension_semantics"] = sem
    if vmem_mib is not None:
        kw["vmem_limit_bytes"] = min(vmem_mib * MIB, V7X_VMEM_BYTES - 4 * MIB)
    return pltpu.CompilerParams(**kw)


def _sigmoid(z):
    return 1.0 / (1.0 + jnp.exp(-z))


def _silu_grad(z, sg):
    return sg * (1.0 + z * (1.0 - sg))


def _nt(a, b):
    return lax.dot_general(a, b, (((1,), (1,)), ((), ())), preferred_element_type=F32)


def _tn(a, b):
    return lax.dot_general(a, b, (((0,), (0,)), ((), ())), preferred_element_type=F32)


def _nn(a, b):
    return jnp.dot(a, b, preferred_element_type=F32)


def _place():
    return lax.axis_index("x"), lax.axis_index("y"), lax.axis_index("c")


def _gather8(v, name, with_sum):
    rows, cols = v.shape

    def body(v_ref, out_ref, *rest):
        if with_sum:
            sum_ref, send_sems, recv_sems = rest
        else:
            send_sems, recv_sems = rest
        mx, my, mc = _place()
        me = 4 * mx + 2 * my + mc
        out_ref[me] = v_ref[...]
        peers = []
        for d in range(1, N_DEV):
            px = 1 - mx if d & 4 else mx
            py = 1 - my if d & 2 else my
            pc = 1 - mc if d & 1 else mc
            peers.append((px, py, pc))

        def copy(d, slot, to):
            return pltpu.make_async_remote_copy(
                src_ref=v_ref, dst_ref=out_ref.at[slot], send_sem=send_sems.at[d], recv_sem=recv_sems.at[d],
                device_id=to, device_id_type=MESH)

        sends = [copy(d, me, p) for d, p in enumerate(peers)]
        for cp in sends:
            cp.start()
        for d, (px, py, pc) in enumerate(peers):
            copy(d, 4 * px + 2 * py + pc, (px, py, pc)).wait_recv()
        for cp in sends:
            cp.wait_send()
        if with_sum:
            acc = out_ref[0]
            for b in range(1, N_DEV):
                acc = acc + out_ref[b]
            sum_ref[...] = acc

    out_shape = [SDS((N_DEV, rows, cols), F32)]
    if with_sum:
        out_shape.append(SDS((rows, cols), F32))
    vm = pl.BlockSpec(memory_space=pltpu.VMEM)
    return pl.pallas_call(
        body, name=name, out_shape=out_shape, in_specs=[vm], out_specs=[vm] * len(out_shape),
        scratch_shapes=[pltpu.SemaphoreType.DMA((N_DEV - 1,)), pltpu.SemaphoreType.DMA((N_DEV - 1,))],
    )(v)


def _chips_of(mx, my):
    chips = [(mx, 1 - my), (1 - mx, my), (1 - mx, 1 - my)]
    return chips, [2 * px + py for px, py in chips]


def _allgather_shards(shards):
    na = len(shards)
    halves = [s.shape[0] // 2 for s in shards]

    def body(*refs):
        ins, outs = refs[:na], refs[na:2 * na]
        s1, r1, s2, r2 = refs[2 * na:]
        mx, my, mc = _place()
        k = 2 * mx + my
        sib = (mx, my, 1 - mc)
        chips, kks = _chips_of(mx, my)

        def half(a, slot, c):
            return outs[a].at[slot, pl.ds(c * halves[a], halves[a])]

        def mine(a):
            return ins[a].at[pl.ds(mc * halves[a], halves[a])]

        sends = []
        for a in range(na):
            for d, (px, py) in enumerate(chips):
                cp = pltpu.make_async_remote_copy(
                    src_ref=mine(a), dst_ref=half(a, k, mc), send_sem=s1.at[3 * a + d], recv_sem=r1.at[3 * a + d],
                    device_id=(px, py, mc), device_id_type=MESH)
                cp.start()
                sends.append(cp)
        for a in range(na):
            for d, (px, py) in enumerate(chips):
                pltpu.make_async_remote_copy(
                    src_ref=mine(a), dst_ref=half(a, kks[d], mc), send_sem=s1.at[3 * a + d], recv_sem=r1.at[3 * a + d],
                    device_id=(px, py, mc), device_id_type=MESH).wait_recv()
                cp = pltpu.make_async_remote_copy(
                    src_ref=half(a, kks[d], mc), dst_ref=half(a, kks[d], mc), send_sem=s2.at[3 * a + d],
                    recv_sem=r2.at[3 * a + d], device_id=sib, device_id_type=MESH)
                cp.start()
                sends.append(cp)
        for a in range(na):
            for d in range(3):
                pltpu.make_async_remote_copy(
                    src_ref=half(a, kks[d], 1 - mc), dst_ref=half(a, kks[d], 1 - mc), send_sem=s2.at[3 * a + d],
                    recv_sem=r2.at[3 * a + d], device_id=sib, device_id_type=MESH).wait_recv()
        for cp in sends:
            cp.wait_send()

    return pl.pallas_call(
        body, name="allgather_weights",
        out_shape=[SDS((N_CHIPS,) + s.shape, s.dtype) for s in shards],
        in_specs=[ANY] * na, out_specs=[ANY] * na,
        scratch_shapes=[pltpu.SemaphoreType.DMA((3 * na,))] * 4,
    )(*shards)


def _rs_core_swap(grads):
    na = len(grads)
    halves = [g.shape[1] // 2 for g in grads]

    def body(*refs):
        ins, outs = refs[:na], refs[na:2 * na]
        ssem, rsem = refs[2 * na:]
        mx, my, mc = _place()
        sib = (mx, my, 1 - mc)
        sends = []
        for a in range(na):
            cp = pltpu.make_async_remote_copy(
                src_ref=ins[a].at[:, pl.ds((1 - mc) * halves[a], halves[a])], dst_ref=outs[a],
                send_sem=ssem.at[a], recv_sem=rsem.at[a], device_id=sib, device_id_type=MESH)
            cp.start()
            sends.append(cp)
        for cp in sends:
            cp.wait_recv()
        for cp in sends:
            cp.wait_send()

    return pl.pallas_call(
        body, name="rs_core_swap", out_shape=[SDS((N_CHIPS, h) + g.shape[2:], g.dtype) for g, h in zip(grads, halves)],
        in_specs=[ANY] * na, out_specs=[ANY] * na,
        scratch_shapes=[pltpu.SemaphoreType.DMA((na,))] * 2,
    )(*grads)


def _chip_exchange_copies(ins, outs, ssem, rsem):
    mx, my, mc = _place()
    k = 2 * mx + my
    chips, kks = _chips_of(mx, my)
    sends, recvs = [], []
    for a in range(len(ins)):
        for d, (px, py) in enumerate(chips):
            def copy(dst_slot):
                return pltpu.make_async_remote_copy(
                    src_ref=ins[a].at[kks[d]], dst_ref=outs[a].at[dst_slot], send_sem=ssem.at[3 * a + d],
                    recv_sem=rsem.at[3 * a + d], device_id=(px, py, mc), device_id_type=MESH)
            sends.append(copy(k))
            recvs.append(copy(kks[d]))
    return sends, recvs


def _rs_core_join(halves):
    na = len(halves)

    def body(*refs):
        ins, outs = refs[:na], refs[na:2 * na]
        ssem, rsem = refs[2 * na:]
        mx, my, mc = _place()
        sib = (mx, my, 1 - mc)
        sends = []
        for a in range(na):
            cp = pltpu.make_async_remote_copy(
                src_ref=ins[a], dst_ref=outs[a].at[mc], send_sem=ssem.at[a], recv_sem=rsem.at[a],
                device_id=sib, device_id_type=MESH)
            cp.start()
            sends.append(cp)
        for a in range(na):
            pltpu.make_async_remote_copy(
                src_ref=ins[a], dst_ref=outs[a].at[1 - mc], send_sem=ssem.at[a], recv_sem=rsem.at[a],
                device_id=sib, device_id_type=MESH).wait_recv()
        for cp in sends:
            cp.wait_send()

    return pl.pallas_call(
        body, name="rs_core_join", out_shape=[SDS((2,) + h.shape, h.dtype) for h in halves],
        in_specs=[ANY] * na, out_specs=[ANY] * na,
        scratch_shapes=[pltpu.SemaphoreType.DMA((na,))] * 2,
    )(*halves)


def _row_tile(rows, limit, mult=16):
    if rows <= limit:
        return rows
    best = None
    for t in range(mult, limit + 1, mult):
        if rows % t == 0:
            best = t
    assert best is not None, rows
    return best


def _add_half_bf16(g, b, core, name):
    _, h, cols = b.shape
    tb = _row_tile(h, 512)
    nb = h // tb

    def body(core_ref, g_ref, b_ref, o_ref):
        o_ref[...] = (g_ref[...].astype(F32) + b_ref[...].astype(F32)).astype(BF16)

    spec = pl.BlockSpec((None, tb, cols), lambda kk, i, core_ref: (kk, i, 0))
    return pl.pallas_call(
        body, name=name,
        grid_spec=pltpu.PrefetchScalarGridSpec(
            num_scalar_prefetch=1, grid=(N_CHIPS, nb),
            in_specs=[pl.BlockSpec((None, tb, cols), lambda kk, i, core_ref: (kk, core_ref[0] * nb + i, 0)), spec],
            out_specs=spec),
        out_shape=SDS(b.shape, BF16), compiler_params=_cp(("arbitrary", "arbitrary")),
    )(core, g, b)


def _sum_chips(p, name):
    _, rows, cols = p.shape
    tb = _row_tile(rows, 256)

    def body(p_ref, o_ref):
        acc = p_ref[0].astype(F32)
        for j in range(1, N_CHIPS):
            acc = acc + p_ref[j].astype(F32)
        o_ref[...] = acc

    return pl.pallas_call(
        body, name=name, grid=(rows // tb,),
        in_specs=[pl.BlockSpec((N_CHIPS, tb, cols), lambda i: (0, i, 0))],
        out_specs=pl.BlockSpec((tb, cols), lambda i: (i, 0)), out_shape=SDS((rows, cols), F32),
        compiler_params=_cp(("arbitrary",)),
    )(p)


def _adamw_math(w, g, m, v):
    m2 = ADAM_B1 * m + (1.0 - ADAM_B1) * g
    v2 = ADAM_B2 * v + (1.0 - ADAM_B2) * (g * g)
    m_hat = m2 / (1.0 - ADAM_B1 ** ADAM_STEP)
    v_hat = v2 / (1.0 - ADAM_B2 ** ADAM_STEP)
    delta = -ADAM_LR * (m_hat / (jnp.sqrt(v_hat) + ADAM_EPS) + ADAM_WD * w)
    return delta, m2, v2


def _adamw(w, g, m, v, name, echo_g=False):
    rows, cols = w.shape
    tb = _row_tile(rows, 256, mult=8)
    nout = 4 if echo_g else 3

    def body(w_ref, g_ref, m_ref, v_ref, d_ref, m2_ref, v2_ref, *echo):
        gv = g_ref[...]
        d, m2, v2 = _adamw_math(w_ref[...], gv, m_ref[...], v_ref[...])
        d_ref[...] = d
        m2_ref[...] = m2
        v2_ref[...] = v2
        if echo_g:
            echo[0][...] = gv

    spec = pl.BlockSpec((tb, cols), lambda i: (i, 0))
    return pl.pallas_call(
        body, name=name, grid=(rows // tb,), in_specs=[spec] * 4, out_specs=[spec] * nout,
        out_shape=[SDS((rows, cols), F32)] * nout, compiler_params=_cp(("arbitrary",), 40),
    )(w, g, m, v)


def _ada_w_update(sc_all, dmod_k, w, m, v):
    rows, cols = w.shape
    tb = 256

    def body(s_ref, dm_ref, w_ref, m_ref, v_ref, g_ref, d_ref, m2_ref, v2_ref):
        g = _tn(s_ref[...].astype(BF16), dm_ref[...].astype(BF16))
        d, m2, v2 = _adamw_math(w_ref[...], g, m_ref[...], v_ref[...])
        g_ref[...] = g
        d_ref[...] = d
        m2_ref[...] = m2
        v2_ref[...] = v2

    spec = pl.BlockSpec((tb, cols), lambda i: (i, 0))
    return pl.pallas_call(
        body, name="ada_w_update", grid=(rows // tb,),
        in_specs=[pl.BlockSpec((N_DEV, tb), lambda i: (0, i)), pl.BlockSpec((N_DEV, cols), lambda i: (0, 0)), spec, spec, spec],
        out_specs=[spec] * 4, out_shape=[SDS((rows, cols), F32)] * 4, compiler_params=_cp(("arbitrary",), 40),
    )(sc_all, dmod_k, w, m, v)


def _ada_mod(c_all, w, b_k):
    rows, cols = w.shape
    tn = 512

    def body(c_ref, w_ref, b_ref, o_ref, s_ref):
        cv = c_ref[...]
        s = cv * _sigmoid(cv)
        s_ref[...] = s
        o_ref[...] = _nn(s.astype(BF16), w_ref[...].astype(BF16)) + b_ref[...]

    return pl.pallas_call(
        body, name="ada_mod", grid=(cols // tn,),
        in_specs=[pl.BlockSpec((N_DEV, rows), lambda j: (0, 0)), pl.BlockSpec((rows, tn), lambda j: (0, j)),
                  pl.BlockSpec((1, tn), lambda j: (0, j))],
        out_specs=[pl.BlockSpec((N_DEV, tn), lambda j: (0, j)), pl.BlockSpec((N_DEV, rows), lambda j: (0, 0))],
        out_shape=[SDS((N_DEV, cols), F32), SDS((N_DEV, rows), F32)], compiler_params=_cp(("arbitrary",)),
    )(c_all, w, b_k)


def _inproj(x, norm_g, scale, shift, w_t, shards):
    seq, dm = x.shape
    ncols = w_t.shape[0]
    tm, tn = min(INPROJ_TM, seq), INPROJ_TN
    ni, nj = seq // tm, ncols // tn
    na = len(shards)

    def body(x_ref, g_ref, sc_ref, sh_ref, w_ref, *rest):
        shard_refs, h_ref, u_ref, got_refs = rest[:na], rest[na], rest[na + 1], rest[na + 2:2 * na + 2]
        ssem, rsem = rest[2 * na + 2:]
        i, j = pl.program_id(0), pl.program_id(1)
        mx, my, mc = _place()
        k = 2 * mx + my
        chips, kks = _chips_of(mx, my)

        def copy(a, d, slot):
            return pltpu.make_async_remote_copy(
                src_ref=shard_refs[a], dst_ref=got_refs[a].at[slot], send_sem=ssem.at[3 * a + d],
                recv_sem=rsem.at[3 * a + d], device_id=(chips[d][0], chips[d][1], mc), device_id_type=MESH)

        @pl.when((i == 0) & (j == 0))
        def _():
            for a in range(na):
                for d in range(3):
                    copy(a, d, k).start()

        @pl.when(j == 0)
        def _():
            xv = x_ref[...]
            r = lax.rsqrt(jnp.mean(xv * xv, axis=-1, keepdims=True) + EPS)
            hv = (xv * r * g_ref[...]) * (1.0 + sc_ref[...]) + sh_ref[...]
            h_ref[...] = hv.astype(BF16)

        u_ref[...] = _nt(h_ref[...], w_ref[...]).astype(BF16)

        @pl.when((i == ni - 1) & (j == nj - 1))
        def _():
            for a in range(na):
                for d in range(3):
                    copy(a, d, kks[d]).wait_recv()
            for a in range(na):
                for d in range(3):
                    copy(a, d, k).wait_send()

    vec = pl.BlockSpec((1, dm), lambda i, j: (0, 0))
    outs = pl.pallas_call(
        body, name="inproj", grid=(ni, nj),
        in_specs=[pl.BlockSpec((tm, dm), lambda i, j: (i, 0)), vec, vec, vec, pl.BlockSpec((tn, dm), lambda i, j: (j, 0))]
                 + [ANY] * na,
        out_specs=[pl.BlockSpec((tm, dm), lambda i, j: (i, 0)), pl.BlockSpec((tm, tn), lambda i, j: (i, j))] + [ANY] * na,
        out_shape=[SDS((seq, dm), BF16), SDS((seq, ncols), BF16)] + [SDS((N_CHIPS,) + s.shape, s.dtype) for s in shards],
        scratch_shapes=[pltpu.SemaphoreType.DMA((3 * na,))] * 2,
        compiler_params=_cp(("arbitrary", "arbitrary"), 48),
    )(x, norm_g, scale, shift, w_t, *shards)
    return outs[0], outs[1], outs[2:]


HALO = 16


def _conv_taps(uc, halo, ext_ref, ts, causal):
    if causal:
        ext_ref[0:HALO, :] = halo
        ext_ref[HALO:HALO + ts, :] = uc
        return ext_ref[pl.ds(HALO - 1, ts), :], ext_ref[pl.ds(HALO - 2, ts), :]
    ext_ref[0:ts, :] = uc
    ext_ref[ts:ts + HALO, :] = halo
    return ext_ref[pl.ds(1, ts), :], ext_ref[pl.ds(2, ts), :]


def _conv_fwd(u, conv_w):
    seq = u.shape[0]
    ts = min(ROW_T, seq)
    hb = ts // HALO

    def body(xc_ref, bc_ref, cc_ref, zc_ref, xp_ref, cp_ref, w_ref, y_ref, ext_ref):
        i = pl.program_id(0)
        uc = cc_ref[...].astype(F32) * xc_ref[...].astype(F32)
        up = cp_ref[...].astype(F32) * xp_ref[...].astype(F32)
        up = jnp.where(i > 0, up, 0.0)
        u1, u2 = _conv_taps(uc, up, ext_ref, ts, True)
        conv = w_ref[0:1, :] * u2 + w_ref[1:2, :] * u1 + w_ref[2:3, :] * uc
        z = zc_ref[...].astype(F32)
        y_ref[...] = ((bc_ref[...].astype(F32) * conv) * (z * _sigmoid(z))).astype(BF16)

    def col(cb):
        return pl.BlockSpec((ts, D_CONV), lambda i: (i, cb))

    def prev(cb):
        return pl.BlockSpec((HALO, D_CONV), lambda i: (jnp.maximum(i * hb - 1, 0), cb))

    return pl.pallas_call(
        body, name="conv_fwd", grid=(seq // ts,),
        in_specs=[col(0), col(1), col(2), col(3), prev(0), prev(2), pl.BlockSpec((3, D_CONV), lambda i: (0, 0))],
        out_specs=pl.BlockSpec((ts, D_CONV), lambda i: (i, 0)), out_shape=SDS((seq, D_CONV), BF16),
        scratch_shapes=[pltpu.VMEM((ts + HALO, D_CONV), F32)],
        compiler_params=_cp(("arbitrary",), 40),
    )(u, u, u, u, u, u, conv_w)


def _rope_tables(pos_ref, freq_ref):
    ang = pos_ref[...].astype(F32) * freq_ref[...]
    lane = lax.broadcasted_iota(jnp.int32, ang.shape, 1)
    cs, sn = jnp.cos(ang), jnp.sin(ang)
    half = QK_ROPE // 2
    cos_t = jnp.where(lane < QK_ROPE, cs, 0.0)
    sin_lo = jnp.where(lane < half, sn, 0.0)
    sin_hi = jnp.where((lane >= half) & (lane < QK_ROPE), sn, 0.0)
    return cos_t, sin_lo, sin_hi


def _rope(blk, tables):
    cos_t, sin_lo, sin_hi = tables
    half = QK_ROPE // 2
    return blk * cos_t - pltpu.roll(blk, LANES - half, 1) * sin_lo + pltpu.roll(blk, half, 1) * sin_hi


def _rope_bwd(g, tables):
    cos_t, sin_lo, sin_hi = tables
    half = QK_ROPE // 2
    return g * cos_t + pltpu.roll(g, LANES - half, 1) * sin_lo - pltpu.roll(g, half, 1) * sin_hi


def _rms(v, n):
    r = lax.rsqrt(jnp.sum(v * v, axis=-1, keepdims=True) * (1.0 / n) + EPS)
    return v * r, r


def _mla_prep(u, pos, freq, q_a_g, wq, kv_a_g, wkn, wv, q_g, k_g):
    seq = u.shape[0]
    ts = min(ROW_T, seq)
    qscale = LOG2E / math.sqrt(QK_HEAD)

    def body(cq_ref, ckv_ref, kr_ref, pos_ref, freq_ref, qag_ref, wq_ref, kvag_ref, wkn_ref, wv_ref, qg_ref, kg_ref,
             q_ref, k_ref, v_ref):
        tables = _rope_tables(pos_ref, freq_ref)
        cqn, _ = _rms(cq_ref[...].astype(F32), Q_LORA)
        qp = _nn((cqn * qag_ref[...]).astype(BF16), wq_ref[...])
        qg = qg_ref[...]
        for h in range(N_HEADS):
            lo = h * HEAD_PAD
            qn, _ = _rms(qp[:, lo:lo + HEAD_PAD], QK_HEAD)
            qn = qn * qg
            q_ref[:, lo:lo + LANES] = (qn[:, :LANES] * qscale).astype(BF16)
            q_ref[:, lo + LANES:lo + HEAD_PAD] = (_rope(qn[:, LANES:], tables) * qscale).astype(BF16)
        ckvn, _ = _rms(ckv_ref[...].astype(F32), KV_LORA)
        ckvb = (ckvn * kvag_ref[...]).astype(BF16)
        kn = _nn(ckvb, wkn_ref[...])
        v_ref[...] = _nn(ckvb, wv_ref[...]).astype(BF16)
        kr = kr_ref[:, 0:LANES].astype(F32)
        ssr = jnp.sum(kr * kr, axis=-1, keepdims=True)
        kg = kg_ref[...]
        for h in range(N_HEADS):
            knh = kn[:, h * QK_NOPE:(h + 1) * QK_NOPE]
            r = lax.rsqrt((jnp.sum(knh * knh, axis=-1, keepdims=True) + ssr) * (1.0 / QK_HEAD) + EPS)
            lo = h * HEAD_PAD
            k_ref[:, lo:lo + LANES] = (knh * r * kg[:, :LANES]).astype(BF16)
            k_ref[:, lo + LANES:lo + HEAD_PAD] = _rope(kr * r * kg[:, LANES:], tables).astype(BF16)

    def full(a):
        return pl.BlockSpec(a.shape, lambda i: (0,) * a.ndim)

    return pl.pallas_call(
        body, name="mla_prep", grid=(seq // ts,),
        in_specs=[pl.BlockSpec((ts, Q_LORA), lambda i: (i, U_CQ // Q_LORA)),
                  pl.BlockSpec((ts, KV_LORA), lambda i: (i, U_CKV // KV_LORA)),
                  pl.BlockSpec((ts, KR_PAD), lambda i: (i, U_KR // KR_PAD)),
                  pl.BlockSpec((ts, 1), lambda i: (i, 0)), full(freq), full(q_a_g), full(wq), full(kv_a_g), full(wkn),
                  full(wv), full(q_g), full(k_g)],
        out_specs=[pl.BlockSpec((ts, Q_PAD), lambda i: (i, 0)), pl.BlockSpec((ts, Q_PAD), lambda i: (i, 0)),
                   pl.BlockSpec((ts, D_ATTN), lambda i: (i, 0))],
        out_shape=[SDS((seq, Q_PAD), BF16), SDS((seq, Q_PAD), BF16), SDS((seq, D_ATTN), BF16)],
        compiler_params=_cp(("arbitrary",), 48),
    )(u, u, u, pos, freq, q_a_g, wq, kv_a_g, wkn, wv, q_g, k_g)


def _causal_bias(bias_ref, t):
    allowed = lax.broadcasted_iota(jnp.int32, (t, t), 0) <= lax.broadcasted_iota(jnp.int32, (t, t), 1)
    bias_ref[0] = jnp.zeros((t, t), F32)
    bias_ref[1] = jnp.where(allowed, 0.0, -jnp.inf)


def _flash_fwd(q, k, v, u):
    seq = q.shape[0]
    t = min(ATT_T, seq)
    n = seq // t
    n_pairs = n * (n + 1) // 2
    za_blk = U_ZA // V_HEAD

    def body(q_ref, k_ref, v_ref, z_ref, o_ref, y_ref, lse_ref, s_a, s_b, top_a, top_b, m_all, l_all, acc_all, bias_ref):
        ones = jnp.ones((16, t), BF16)
        bufs = ((s_a, top_a), (s_b, top_b))

        @pl.when(pl.program_id(0) == 0)
        def _():
            _causal_bias(bias_ref, t)

        def rows(i):
            return pl.ds(pl.multiple_of(i * t, t), t)

        def scores(i, j, buf):
            s_ref, top_ref = buf
            s = _nt(k_ref[rows(j), :], q_ref[rows(i), :]) + bias_ref[jnp.where(i == j, 1, 0)]
            s_ref[...] = s
            top_ref[...] = jnp.max(s, axis=0, keepdims=True)

        def absorb(i, j, buf):
            s_ref, top_ref = buf
            first = j == 0
            m = jnp.where(first, -jnp.inf, m_all[i])
            l = jnp.where(first, 0.0, l_all[i])
            acc = jnp.where(first, 0.0, acc_all[i])
            m_new = jnp.maximum(m, top_ref[...])
            alpha = jnp.exp2(m - m_new)
            p = jnp.exp2((s_ref[...] - m_new).astype(BF16))
            m_all[i] = m_new
            l_all[i] = alpha * l + _nn(ones, p)[0:1, :]
            acc_all[i] = alpha * acc + _tn(v_ref[rows(j), :], p)

        def trip(width):
            def walk(_, pair):
                i, j = pair
                for w in range(width):
                    done = j == i
                    ni, nj = jnp.where(done, i + 1, i), jnp.where(done, 0, j + 1)
                    scores(jnp.minimum(ni, n - 1), nj, bufs[(w + 1) % 2])
                    absorb(i, j, bufs[w % 2])
                    i, j = ni, nj
                return i, j
            return walk

        scores(0, 0, bufs[0])
        pair = lax.fori_loop(0, n_pairs // ATT_UNROLL, trip(ATT_UNROLL), (jnp.int32(0), jnp.int32(0)))
        if n_pairs % ATT_UNROLL:
            trip(n_pairs % ATT_UNROLL)(0, pair)

        def finish(i, carry):
            l = l_all[i]
            o = (acc_all[i] * (1.0 / l)).T
            lse_ref[:, rows(i)] = m_all[i] + jnp.log2(l)
            o_ref[rows(i), :] = o.astype(BF16)
            z = z_ref[rows(i), :].astype(F32)
            y_ref[rows(i), :] = (o * (z * _sigmoid(z))).astype(BF16)
            return carry

        lax.fori_loop(0, n, finish, 0)

    def col(width, cb):
        return pl.BlockSpec((seq, width), lambda h: (0, cb + h))

    return pl.pallas_call(
        body, name="flash_fwd", grid=(N_HEADS,),
        in_specs=[col(HEAD_PAD, 0), col(HEAD_PAD, 0), col(V_HEAD, 0), col(V_HEAD, za_blk)],
        out_specs=[col(V_HEAD, 0), col(V_HEAD, 0), pl.BlockSpec((None, 1, seq), lambda h: (h, 0, 0))],
        out_shape=[SDS((seq, D_ATTN), BF16), SDS((seq, D_ATTN), BF16), SDS((N_HEADS, 1, seq), F32)],
        scratch_shapes=[pltpu.VMEM((t, t), F32)] * 2 + [pltpu.VMEM((1, t), F32)] * 2
                       + [pltpu.VMEM((n, 1, t), F32)] * 2 + [pltpu.VMEM((n, V_HEAD, t), F32), pltpu.VMEM((2, t, t), F32)],
        compiler_params=_cp(("arbitrary",), 52),
    )(q, k, v, u)


def _outproj_loss(y_conv, y_attn, x, target, gate, w_out):
    seq, dm = x.shape
    ts = min(OUT_T, seq)
    n = seq // ts
    dmix = w_out.shape[0]

    def body(yc_ref, ya_ref, x_ref, t_ref, gate_ref, wo_hbm, dout_ref, dy_ref, dyc_ref, stats_ref, wo_ref, sem, acc_ref):
        i = pl.program_id(0)

        @pl.when(i == 0)
        def _():
            cp = pltpu.make_async_copy(wo_hbm, wo_ref, sem)
            cp.start()
            cp.wait()
            acc_ref[...] = jnp.zeros_like(acc_ref)

        y = _nn(yc_ref[...], wo_ref[0:D_CONV, :]) + _nn(ya_ref[...], wo_ref[D_CONV:dmix, :])
        gate_v = gate_ref[...]
        diff = (x_ref[...] + gate_v * y) - t_ref[...]
        dout = diff * (1.0 / dm)
        dout_ref[...] = dout
        acc_ref[0:8, :] += jnp.sum((dout * y).reshape(ts // 8, 8, dm), axis=0)
        acc_ref[8:16, :] += jnp.sum((diff * diff).reshape(ts // 8, 8, dm), axis=0)
        dy = (dout * gate_v).astype(BF16)
        dy_ref[...] = dy
        dyc_ref[...] = _nt(dy, wo_ref[...]).astype(BF16)

        @pl.when(i == n - 1)
        def _():
            stats_ref[...] = jnp.zeros_like(stats_ref)
            stats_ref[0:1, :] = jnp.sum(acc_ref[0:8, :], axis=0, keepdims=True)
            loss = jnp.sum(acc_ref[8:16, :]) * (0.5 / dm)
            stats_ref[1:2, :] = jnp.full((1, dm), loss, F32)

    row = pl.BlockSpec((ts, dm), lambda i: (i, 0))
    half = pl.BlockSpec((ts, D_CONV), lambda i: (i, 0))
    return pl.pallas_call(
        body, name="outproj_loss", grid=(n,),
        in_specs=[half, half, row, row, pl.BlockSpec((1, dm), lambda i: (0, 0)), ANY],
        out_specs=[row, row, pl.BlockSpec((ts, dmix), lambda i: (i, 0)), pl.BlockSpec((8, dm), lambda i: (0, 0))],
        out_shape=[SDS((seq, dm), F32), SDS((seq, dm), BF16), SDS((seq, dmix), BF16), SDS((8, dm), F32)],
        scratch_shapes=[pltpu.VMEM(w_out.shape, BF16), pltpu.SemaphoreType.DMA(()), pltpu.VMEM((16, dm), F32)],
        compiler_params=_cp(("arbitrary",), 52),
    )(y_conv, y_attn, x, target, gate, w_out)


def _matmul_tn(a, b, name):
    seq, m = a.shape
    n = b.shape[1]
    tm, tn, tk = min(TN_TM, m), min(TN_TN, n), min(TN_TK, seq)
    nk = seq // tk

    def body(a_ref, b_ref, o_ref, acc_ref):
        kk = pl.program_id(2)

        @pl.when(kk == 0)
        def _():
            acc_ref[...] = jnp.zeros_like(acc_ref)

        acc_ref[...] += _tn(a_ref[...], b_ref[...])

        @pl.when(kk == nk - 1)
        def _():
            o_ref[...] = acc_ref[...].astype(BF16)

    return pl.pallas_call(
        body, name=name, grid=(m // tm, n // tn, nk),
        in_specs=[pl.BlockSpec((tk, tm), lambda i, j, kk: (kk, i)), pl.BlockSpec((tk, tn), lambda i, j, kk: (kk, j))],
        out_specs=pl.BlockSpec((tm, tn), lambda i, j, kk: (i, j)), out_shape=SDS((m, n), BF16),
        scratch_shapes=[pltpu.VMEM((tm, tn), F32)],
        compiler_params=_cp(("arbitrary", "arbitrary", "arbitrary"), 40),
    )(a, b)


def _attn_gate_bwd(dycat, o, u):
    seq = o.shape[0]
    ts = min(ROW_T, seq)

    def body(dy_ref, o_ref, z_ref, dot_ref, dz_ref, dl_ref):
        dy = dy_ref[...].astype(F32)
        ov = o_ref[...].astype(F32)
        z = z_ref[...].astype(F32)
        sg = _sigmoid(z)
        do = dy * (z * sg)
        dz_ref[...] = (dy * ov * _silu_grad(z, sg)).astype(BF16)
        prod = do * ov
        ones = jnp.ones((8, V_HEAD), F32)
        for h in range(N_HEADS):
            cols = slice(h * V_HEAD, (h + 1) * V_HEAD)
            dot_ref[h] = do[:, cols].T.astype(BF16)
            rows = lax.dot_general(ones, prod[:, cols], (((1,), (1,)), ((), ())), precision=lax.Precision.HIGHEST,
                                   preferred_element_type=F32)
            dl_ref[h] = rows[0:1, :]

    blk = pl.BlockSpec((ts, D_ATTN), lambda i: (i, 0))
    return pl.pallas_call(
        body, name="attn_gate_bwd", grid=(seq // ts,),
        in_specs=[pl.BlockSpec((ts, D_ATTN), lambda i: (i, 1)), blk, pl.BlockSpec((ts, D_ATTN), lambda i: (i, U_ZA // D_ATTN))],
        out_specs=[pl.BlockSpec((N_HEADS, V_HEAD, ts), lambda i: (0, 0, i)), blk,
                   pl.BlockSpec((N_HEADS, 1, ts), lambda i: (0, 0, i))],
        out_shape=[SDS((N_HEADS, V_HEAD, seq), BF16), SDS((seq, D_ATTN), BF16), SDS((N_HEADS, 1, seq), F32)],
        compiler_params=_cp(("arbitrary",), 40),
    )(dycat, o, u)


def _flash_bwd(q, k, v, do_t, lse, delta):
    seq = q.shape[0]
    t = min(ATT_T, seq)
    n = seq // t
    n_pairs = n * (n + 1) // 2

    def body(k_ref, v_ref, q_ref, dot_ref, lse_ref, dl_ref, dq_ref, dk_ref, dv_ref, s_a, s_b, dp_a, dp_b, dq_acc, dk_acc,
             dvt_acc, bias_ref):
        bufs = ((s_a, dp_a), (s_b, dp_b))

        @pl.when(pl.program_id(0) == 0)
        def _():
            _causal_bias(bias_ref, t)

        def rows(i):
            return pl.ds(pl.multiple_of(i * t, t), t)

        def products(j, i, buf):
            s_ref, dp_ref = buf
            s_ref[...] = _nt(k_ref[rows(j), :], q_ref[rows(i), :]) + bias_ref[jnp.where(i == j, 1, 0)]
            dp_ref[...] = _nn(v_ref[rows(j), :], dot_ref[:, rows(i)])

        def absorb(j, i, buf):
            s_ref, dp_ref = buf
            first = i == j
            p = jnp.exp2((s_ref[...] - lse_ref[:, rows(i)]).astype(BF16))
            dvt = jnp.where(first, 0.0, dvt_acc[...]) + _nt(dot_ref[:, rows(i)], p)
            ds = p * (dp_ref[...] - dl_ref[:, rows(i)]).astype(BF16)
            dk = jnp.where(first, 0.0, dk_acc[...]) + _nn(ds, q_ref[rows(i), :])
            dq_acc[rows(i), :] += _tn(ds, k_ref[rows(j), :])
            dvt_acc[...] = dvt
            dk_acc[...] = dk
            dk_ref[rows(j), :] = dk.astype(BF16)
            dv_ref[rows(j), :] = dvt.T.astype(BF16)

        def trip(width):
            def walk(_, pair):
                j, i = pair
                for w in range(width):
                    done = i == n - 1
                    nj = jnp.where(done, j + 1, j)
                    ni = jnp.where(done, j + 1, i + 1)
                    absorb(j, i, bufs[w % 2])
                    products(jnp.minimum(nj, n - 1), jnp.minimum(ni, n - 1), bufs[(w + 1) % 2])
                    j, i = nj, ni
                return j, i
            return walk

        dq_acc[...] = jnp.zeros_like(dq_acc)
        products(0, 0, bufs[0])
        pair = lax.fori_loop(0, n_pairs // ATT_UNROLL, trip(ATT_UNROLL), (jnp.int32(0), jnp.int32(0)))
        if n_pairs % ATT_UNROLL:
            trip(n_pairs % ATT_UNROLL)(0, pair)

        def finish(i, carry):
            dq_ref[rows(i), :] = dq_acc[rows(i), :].astype(BF16)
            return carry

        lax.fori_loop(0, n, finish, 0)

    def col(width):
        return pl.BlockSpec((seq, width), lambda h: (0, h))

    row = pl.BlockSpec((None, 1, seq), lambda h: (h, 0, 0))
    return pl.pallas_call(
        body, name="flash_bwd", grid=(N_HEADS,),
        in_specs=[col(HEAD_PAD), col(V_HEAD), col(HEAD_PAD), pl.BlockSpec((None, V_HEAD, seq), lambda h: (h, 0, 0)), row, row],
        out_specs=[col(HEAD_PAD), col(HEAD_PAD), col(V_HEAD)],
        out_shape=[SDS((seq, Q_PAD), BF16), SDS((seq, Q_PAD), BF16), SDS((seq, D_ATTN), BF16)],
        scratch_shapes=[pltpu.VMEM((t, t), F32)] * 4
                       + [pltpu.VMEM((seq, HEAD_PAD), F32), pltpu.VMEM((t, HEAD_PAD), F32), pltpu.VMEM((V_HEAD, t), F32),
                          pltpu.VMEM((2, t, t), F32)],
        compiler_params=_cp(("arbitrary",), 60),
    )(k, v, q, do_t, lse, delta)


SG_QAG, SG_KVAG, SG_QG, SG_KG, SG_COLS = 0, Q_LORA, Q_LORA + KV_LORA, Q_LORA + KV_LORA + HEAD_PAD, D_MODEL


def _mla_bwd(dq, dk, dv, u, pos, freq, q_a_g, wq, kv_a_g, wkn, wv, q_g, k_g):
    seq = u.shape[0]
    ts = min(ROW_T, seq)
    n = seq // ts
    qscale = 1.0 / math.sqrt(QK_HEAD)

    def body(dq_ref, dk_ref, dv_ref, cq_ref, ckv_ref, kr_ref, pos_ref, freq_ref, qag_ref, wq_ref, kvag_ref, wkn_ref,
             wv_ref, qg_ref, kg_ref, du_ref, dwq_ref, dwkn_ref, dwv_ref, sg_ref, dqp_ref, dkn_ref):
        i = pl.program_id(0)

        @pl.when(i == 0)
        def _():
            dwq_ref[...] = jnp.zeros_like(dwq_ref)
            dwkn_ref[...] = jnp.zeros_like(dwkn_ref)
            dwv_ref[...] = jnp.zeros_like(dwv_ref)
            sg_ref[...] = jnp.zeros_like(sg_ref)

        tables = _rope_tables(pos_ref, freq_ref)

        cq = cq_ref[...].astype(F32)
        cqn, rq = _rms(cq, Q_LORA)
        qag = qag_ref[...]
        cqb = (cqn * qag).astype(BF16)
        qp = _nn(cqb, wq_ref[...])
        qg = qg_ref[...]
        dqg = jnp.zeros((1, HEAD_PAD), F32)
        for h in range(N_HEADS):
            lo = h * HEAD_PAD
            xn, r = _rms(qp[:, lo:lo + HEAD_PAD], QK_HEAD)
            g = jnp.concatenate([dq_ref[:, lo:lo + LANES].astype(F32),
                                 _rope_bwd(dq_ref[:, lo + LANES:lo + HEAD_PAD].astype(F32), tables)], axis=-1) * qscale
            dqg = dqg + jnp.sum(g * xn, axis=0, keepdims=True)
            gy = g * qg
            mean = jnp.sum(gy * xn, axis=-1, keepdims=True) * (1.0 / QK_HEAD)
            dqp_ref[:, lo:lo + HEAD_PAD] = (r * (gy - xn * mean)).astype(BF16)
        dqp = dqp_ref[...]
        dwq_ref[...] += _tn(cqb, dqp)
        dcqn = _nt(dqp, wq_ref[...])
        sg_ref[0:1, SG_QAG:SG_QAG + Q_LORA] += jnp.sum(dcqn * cqn, axis=0, keepdims=True)
        sg_ref[0:1, SG_QG:SG_QG + HEAD_PAD] += dqg
        gy = dcqn * qag
        mean = jnp.sum(gy * cqn, axis=-1, keepdims=True) * (1.0 / Q_LORA)
        du_ref[:, 0:Q_LORA] = (rq * (gy - cqn * mean)).astype(BF16)

        ckv = ckv_ref[...].astype(F32)
        ckvn, rkv = _rms(ckv, KV_LORA)
        kvag = kvag_ref[...]
        ckvb = (ckvn * kvag).astype(BF16)
        kn = _nn(ckvb, wkn_ref[...])
        kr = kr_ref[:, 0:LANES].astype(F32)
        ssr = jnp.sum(kr * kr, axis=-1, keepdims=True)
        kg = kg_ref[...]
        kg_n, kg_r = kg[:, :LANES] * LN2, kg[:, LANES:] * LN2
        dkg_n = jnp.zeros((1, LANES), F32)
        dkg_r = jnp.zeros((1, LANES), F32)
        dkr = jnp.zeros((ts, LANES), F32)
        for h in range(N_HEADS):
            knh = kn[:, h * QK_NOPE:(h + 1) * QK_NOPE]
            r = lax.rsqrt((jnp.sum(knh * knh, axis=-1, keepdims=True) + ssr) * (1.0 / QK_HEAD) + EPS)
            xn_n, xn_r = knh * r, kr * r
            lo = h * HEAD_PAD
            g_n = dk_ref[:, lo:lo + LANES].astype(F32)
            g_r = _rope_bwd(dk_ref[:, lo + LANES:lo + HEAD_PAD].astype(F32), tables)
            dkg_n = dkg_n + jnp.sum(g_n * xn_n, axis=0, keepdims=True)
            dkg_r = dkg_r + jnp.sum(g_r * xn_r, axis=0, keepdims=True)
            gy_n, gy_r = g_n * kg_n, g_r * kg_r
            mean = (jnp.sum(gy_n * xn_n, axis=-1, keepdims=True) + jnp.sum(gy_r * xn_r, axis=-1, keepdims=True)) * (1.0 / QK_HEAD)
            dkn_ref[:, h * QK_NOPE:(h + 1) * QK_NOPE] = (r * (gy_n - xn_n * mean)).astype(BF16)
            dkr = dkr + r * (gy_r - xn_r * mean)
        dkn = dkn_ref[...]
        dvv = dv_ref[...]
        dwkn_ref[...] += _tn(ckvb, dkn)
        dwv_ref[...] += _tn(ckvb, dvv)
        dckvn = _nt(dkn, wkn_ref[...]) + _nt(dvv, wv_ref[...])
        sg_ref[0:1, SG_KVAG:SG_KVAG + KV_LORA] += jnp.sum(dckvn * ckvn, axis=0, keepdims=True)
        sg_ref[0:1, SG_KG:SG_KG + LANES] += dkg_n * LN2
        sg_ref[0:1, SG_KG + LANES:SG_KG + HEAD_PAD] += dkg_r * LN2
        gy = dckvn * kvag
        mean = jnp.sum(gy * ckvn, axis=-1, keepdims=True) * (1.0 / KV_LORA)
        du_ref[:, Q_LORA:Q_LORA + KV_LORA] = (rkv * (gy - ckvn * mean)).astype(BF16)
        du_ref[:, Q_LORA + KV_LORA:Q_LORA + KV_LORA + LANES] = dkr.astype(BF16)
        du_ref[:, Q_LORA + KV_LORA + LANES:MLA_COLS] = jnp.zeros((ts, LANES), BF16)

    def full(a):
        return pl.BlockSpec(a.shape, lambda i: (0,) * a.ndim)

    wide = pl.BlockSpec((ts, Q_PAD), lambda i: (i, 0))
    return pl.pallas_call(
        body, name="mla_bwd", grid=(n,),
        in_specs=[wide, wide, pl.BlockSpec((ts, D_ATTN), lambda i: (i, 0)),
                  pl.BlockSpec((ts, Q_LORA), lambda i: (i, U_CQ // Q_LORA)),
                  pl.BlockSpec((ts, KV_LORA), lambda i: (i, U_CKV // KV_LORA)),
                  pl.BlockSpec((ts, KR_PAD), lambda i: (i, U_KR // KR_PAD)),
                  pl.BlockSpec((ts, 1), lambda i: (i, 0)), full(freq), full(q_a_g), full(wq), full(kv_a_g), full(wkn),
                  full(wv), full(q_g), full(k_g)],
        out_specs=[pl.BlockSpec((ts, MLA_COLS), lambda i: (i, 0)), pl.BlockSpec((Q_LORA, Q_PAD), lambda i: (0, 0)),
                   pl.BlockSpec((KV_LORA, D_ATTN), lambda i: (0, 0)), pl.BlockSpec((KV_LORA, D_ATTN), lambda i: (0, 0)),
                   pl.BlockSpec((8, SG_COLS), lambda i: (0, 0))],
        out_shape=[SDS((seq, MLA_COLS), BF16), SDS((Q_LORA, Q_PAD), F32), SDS((KV_LORA, D_ATTN), F32),
                   SDS((KV_LORA, D_ATTN), F32), SDS((8, SG_COLS), F32)],
        scratch_shapes=[pltpu.VMEM((ts, Q_PAD), BF16), pltpu.VMEM((ts, D_ATTN), BF16)],
        compiler_params=_cp(("arbitrary",), 56),
    )(dq, dk, dv, u, u, u, pos, freq, q_a_g, wq, kv_a_g, wkn, wv, q_g, k_g)


def _conv_bwd(dycat, u, conv_w):
    seq = u.shape[0]
    ts = min(ROW_T, seq)
    n = seq // ts
    hb = ts // HALO

    def body(dy_ref, xc_ref, bc_ref, cc_ref, zc_ref, xp_ref, cp_ref, dyn_ref, bn_ref, zn_ref, w_ref,
             du_ref, dw_ref, ext_ref):
        i = pl.program_id(0)

        @pl.when(i == 0)
        def _():
            dw_ref[...] = jnp.zeros_like(dw_ref)

        xc = xc_ref[...].astype(F32)
        cc = cc_ref[...].astype(F32)
        uc = cc * xc
        up = jnp.where(i > 0, cp_ref[...].astype(F32) * xp_ref[...].astype(F32), 0.0)
        u1, u2 = _conv_taps(uc, up, ext_ref, ts, True)
        w0, w1, w2 = w_ref[0:1, :], w_ref[1:2, :], w_ref[2:3, :]
        conv = w0 * u2 + w1 * u1 + w2 * uc
        z = zc_ref[...].astype(F32)
        sg = _sigmoid(z)
        sz = z * sg
        b = bc_ref[...].astype(F32)
        dy = dy_ref[...].astype(F32)
        du_ref[:, 3 * D_CONV:4 * D_CONV] = (dy * (b * conv) * _silu_grad(z, sg)).astype(BF16)
        du_ref[:, D_CONV:2 * D_CONV] = (dy * sz * conv).astype(BF16)
        dconv = dy * sz * b
        dw_ref[0:1, :] += jnp.sum(dconv * u2, axis=0, keepdims=True)
        dw_ref[1:2, :] += jnp.sum(dconv * u1, axis=0, keepdims=True)
        dw_ref[2:3, :] += jnp.sum(dconv * uc, axis=0, keepdims=True)
        zn = zn_ref[...].astype(F32)
        dnext = dyn_ref[...].astype(F32) * (zn * _sigmoid(zn)) * bn_ref[...].astype(F32)
        dnext = jnp.where(i < n - 1, dnext, 0.0)
        d1, d2 = _conv_taps(dconv, dnext, ext_ref, ts, False)
        du = w2 * dconv + w1 * d1 + w0 * d2
        du_ref[:, 2 * D_CONV:3 * D_CONV] = (du * xc).astype(BF16)
        du_ref[:, 0:D_CONV] = (du * cc).astype(BF16)

    def col(cb):
        return pl.BlockSpec((ts, D_CONV), lambda i: (i, cb))

    def prev(cb):
        return pl.BlockSpec((HALO, D_CONV), lambda i: (jnp.maximum(i * hb - 1, 0), cb))

    def nxt(cb):
        return pl.BlockSpec((HALO, D_CONV), lambda i: (jnp.minimum((i + 1) * hb, n * hb - 1), cb))

    return pl.pallas_call(
        body, name="conv_bwd", grid=(n,),
        in_specs=[col(0), col(0), col(1), col(2), col(3), prev(0), prev(2), nxt(0), nxt(1), nxt(3),
                  pl.BlockSpec((3, D_CONV), lambda i: (0, 0))],
        out_specs=[pl.BlockSpec((ts, 4 * D_CONV), lambda i: (i, 0)), pl.BlockSpec((8, D_CONV), lambda i: (0, 0))],
        out_shape=[SDS((seq, 4 * D_CONV), BF16), SDS((8, D_CONV), F32)],
        scratch_shapes=[pltpu.VMEM((ts + HALO, D_CONV), F32)],
        compiler_params=_cp(("arbitrary",), 48),
    )(dycat, u, u, u, u, u, u, dycat, u, u, conv_w)


def _inproj_bwd(du_conv, du_za, du_mla, w_t, parts):
    seq = du_conv.shape[0]
    dm = w_t.shape[1]
    tm, tn = min(DH_TM, seq), DH_TN
    ni, nj = seq // tm, dm // tn
    na = len(parts)

    def body(dc_ref, dz_ref, dm_ref, w_ref, *rest):
        part_refs, o_ref, recv_refs = rest[:na], rest[na], rest[na + 1:2 * na + 1]
        ssem, rsem = rest[2 * na + 1:]
        i, j = pl.program_id(0), pl.program_id(1)
        sends, recvs = _chip_exchange_copies(part_refs, recv_refs, ssem, rsem)

        @pl.when((i == 0) & (j == 0))
        def _():
            for cp in sends:
                cp.start()

        acc = _nn(dc_ref[...], w_ref[0:U_ZA, :])
        acc = acc + _nn(dz_ref[...], w_ref[U_ZA:U_CQ, :])
        acc = acc + _nn(dm_ref[...], w_ref[U_CQ:U_COLS, :])
        o_ref[...] = acc

        @pl.when((i == ni - 1) & (j == nj - 1))
        def _():
            for cp in recvs:
                cp.wait_recv()
            for cp in sends:
                cp.wait_send()

    outs = pl.pallas_call(
        body, name="inproj_bwd", grid=(ni, nj),
        in_specs=[pl.BlockSpec((tm, U_ZA), lambda i, j: (i, 0)), pl.BlockSpec((tm, D_ATTN), lambda i, j: (i, 0)),
                  pl.BlockSpec((tm, MLA_COLS), lambda i, j: (i, 0)), pl.BlockSpec((U_COLS, tn), lambda i, j: (0, j))]
                 + [ANY] * na,
        out_specs=[pl.BlockSpec((tm, tn), lambda i, j: (i, j))] + [ANY] * na,
        out_shape=[SDS((seq, dm), F32)] + [SDS(p.shape, p.dtype) for p in parts],
        scratch_shapes=[pltpu.SemaphoreType.DMA((3 * na,))] * 2,
        compiler_params=_cp(("arbitrary", "arbitrary"), 48),
    )(du_conv, du_za, du_mla, w_t, *parts)
    return outs[0], outs[1:]


def _prenorm_bwd(x, dh, dout, norm_g, scale):
    seq, dm = x.shape
    ts = min(ROW_T, seq)
    n = seq // ts

    def body(x_ref, dh_ref, dout_ref, g_ref, sc_ref, gx_ref, st_ref, acc_ref):
        i = pl.program_id(0)

        @pl.when(i == 0)
        def _():
            acc_ref[...] = jnp.zeros_like(acc_ref)

        xv = x_ref[...]
        xn, r = _rms(xv, dm)
        dh_v = dh_ref[...]
        gv = g_ref[...]
        one_sc = 1.0 + sc_ref[...]

        def fold(a):
            return jnp.sum(a.reshape(ts // 8, 8, dm), axis=0)

        acc_ref[0:8, :] += fold(dh_v)
        acc_ref[8:16, :] += fold(dh_v * (xn * gv))
        dxg = dh_v * one_sc
        acc_ref[16:24, :] += fold(dxg * xn)
        dxn = dxg * gv
        mean = jnp.sum(dxn * xn, axis=-1, keepdims=True) * (1.0 / dm)
        gx_ref[...] = dout_ref[...] + r * (dxn - xn * mean)

        @pl.when(i == n - 1)
        def _():
            st_ref[...] = jnp.zeros_like(st_ref)
            for k in range(3):
                st_ref[k:k + 1, :] = jnp.sum(acc_ref[8 * k:8 * k + 8, :], axis=0, keepdims=True)

    row = pl.BlockSpec((ts, dm), lambda i: (i, 0))
    vec = pl.BlockSpec((1, dm), lambda i: (0, 0))
    return pl.pallas_call(
        body, name="prenorm_bwd", grid=(n,), in_specs=[row, row, row, vec, vec],
        out_specs=[row, pl.BlockSpec((8, dm), lambda i: (0, 0))],
        out_shape=[SDS((seq, dm), F32), SDS((8, dm), F32)],
        scratch_shapes=[pltpu.VMEM((24, dm), F32)], input_output_aliases={2: 0},
        compiler_params=_cp(("arbitrary",), 52),
    )(x, dh, dout, norm_g, scale)


def _unshard_cols(g):
    return jnp.transpose(g, (1, 0, 2)).reshape(g.shape[1], -1)


def _shard_cols(w):
    r = w.shape[0]
    return jnp.transpose(w.reshape(r, N_CHIPS, -1), (1, 0, 2))


W_IN_COLS = 4 * D_CONV + Q_LORA + KV_LORA + QK_ROPE + D_ATTN
SHARD_ROWS = W_IN_COLS // N_CHIPS
SHARD_PAD = 1536


def _w_in_pieces():
    c4 = 4 * D_CONV
    groups = [(0, c4, 0), (c4, c4 + Q_LORA, U_CQ), (c4 + Q_LORA, c4 + Q_LORA + KV_LORA, U_CKV),
              (c4 + Q_LORA + KV_LORA, W_IN_COLS - D_ATTN, U_KR), (W_IN_COLS - D_ATTN, W_IN_COLS, U_ZA)]
    pieces = []
    for lo, hi, my in groups:
        for chip in range(N_CHIPS):
            a, b = max(lo, chip * SHARD_ROWS), min(hi, (chip + 1) * SHARD_ROWS)
            if a < b:
                pieces.append((chip, a - chip * SHARD_ROWS, b - a, my + a - lo))
    return pieces


def _w_t_to_my(g):
    w = jnp.zeros((U_COLS, g.shape[2]), g.dtype)
    for chip, row, n, my in _w_in_pieces():
        w = lax.dynamic_update_slice(w, g[chip, row:row + n], (my, 0))
    return w


def _w_t_from_my(g_conv, g_za, g_mla):
    w = jnp.zeros((N_CHIPS, SHARD_PAD, g_conv.shape[1]), g_conv.dtype)
    for chip, row, n, my in _w_in_pieces():
        src, base = (g_conv, 0) if my < U_ZA else (g_za, U_ZA) if my < U_CQ else (g_mla, U_CQ)
        w = lax.dynamic_update_slice(w, src[my - base:my - base + n][None], (chip, row, 0))
    return w


def _heads_pad(w):
    r = w.shape[0]
    w3 = w.reshape(r, N_HEADS, QK_HEAD)
    return jnp.pad(w3, ((0, 0), (0, 0), (0, HEAD_PAD - QK_HEAD))).reshape(r, Q_PAD)


def _heads_unpad(w):
    r = w.shape[0]
    return w.reshape(r, N_HEADS, HEAD_PAD)[:, :, :QK_HEAD].reshape(r, N_HEADS * QK_HEAD)


def kernel(x, c, positions, ada_w, ada_b, norm_g, w_in, conv_w, q_a_g, w_q_b, kv_a_g, w_kv_b, q_g, k_g, w_out, loss_target, m_ada_w, m_ada_b, m_norm_g, m_w_in, m_conv_w, m_q_a_g, m_w_q_b, m_kv_a_g, m_w_kv_b, m_q_g, m_k_g, m_w_out, v_ada_w, v_ada_b, v_norm_g, v_w_in, v_conv_w, v_q_a_g, v_w_q_b, v_kv_a_g, v_w_kv_b, v_q_g, v_k_g, v_w_out):
    mx, my, mc = _place()
    chip = 2 * mx + my
    me = 2 * chip + mc
    seq = x.shape[1]
    x2, t2 = x[0], loss_target[0]
    cw_cols = conv_w.shape[2]

    small = jnp.zeros((8, D_MODEL), F32)
    small = small.at[0].set(c[0])
    small = small.at[1:4, :cw_cols].set(conv_w[0])
    small_all = _gather8(small, "gather_c_conv", False)[0]
    c_all = small_all[:, 0, :]
    conv_full = jnp.transpose(small_all.reshape(N_CHIPS, 2, 8, D_MODEL)[:, 0, 1:4, :cw_cols], (1, 0, 2)).reshape(3, D_CONV)

    ada_cols = ada_w.shape[2]
    b_k = lax.dynamic_slice(ada_b, (0, chip * ada_cols), (1, ada_cols))
    mod_k, sc_all = _ada_mod(c_all, ada_w[0], b_k)
    mod_all = _gather8(mod_k, "gather_mod", False)[0]
    mod_row = lax.dynamic_slice(mod_all.reshape(N_CHIPS, 2, N_DEV, ada_cols), (0, mc, me, 0), (N_CHIPS, 1, 1, ada_cols))
    mod_row = mod_row.reshape(3, D_MODEL)
    shift, scale, gate = mod_row[0:1], mod_row[1:2], mod_row[2:3]

    def own_slot(g, s):
        return lax.dynamic_update_slice(g, s[None], (chip, 0, 0))

    w_in_t, m_w_in_t, v_w_in_t = [jnp.transpose(a[0]) for a in (w_in, m_w_in, v_w_in)]
    shard_in = jnp.pad(w_in_t.astype(BF16), ((0, SHARD_PAD - SHARD_ROWS), (0, 0)))
    g_in = own_slot(_allgather_shards([shard_in])[0], shard_in)
    w_t = _w_t_to_my(g_in)

    later = [w_q_b[0].astype(BF16), w_kv_b[0].astype(BF16), w_out[0].astype(BF16)]
    h, u, got = _inproj(x2, norm_g, scale, shift, w_t, later)
    g_q, g_kv, g_out = [own_slot(g, s) for g, s in zip(got, later)]
    wq = _heads_pad(_unshard_cols(g_q))
    wkv = _unshard_cols(g_kv).reshape(KV_LORA, N_HEADS, QK_NOPE + V_HEAD)
    wkn = wkv[:, :, :QK_NOPE].reshape(KV_LORA, N_HEADS * QK_NOPE)
    wv = wkv[:, :, QK_NOPE:].reshape(KV_LORA, D_ATTN)
    wo = g_out.reshape(N_CHIPS * g_out.shape[1], D_MODEL)
    y_conv = _conv_fwd(u, conv_full)
    pos = positions.reshape(seq, 1)
    inv_freq = ROPE_BASE ** (-jnp.arange(0, QK_ROPE, 2, dtype=F32) / QK_ROPE)
    freq = jnp.concatenate([inv_freq, inv_freq, jnp.zeros((LANES - QK_ROPE,), F32)]).reshape(1, LANES)
    q_g_pad = jnp.pad(q_g, ((0, 0), (0, HEAD_PAD - QK_HEAD)))
    k_g_pad = jnp.pad(k_g, ((0, 0), (0, HEAD_PAD - QK_HEAD)))
    q, k, v = _mla_prep(u, pos, freq, q_a_g, wq, kv_a_g, wkn, wv, q_g_pad, k_g_pad)
    o, y_attn, lse = _flash_fwd(q, k, v, u)
    dout, dy, dycat, st_out = _outproj_loss(y_conv, y_attn, x2, t2, gate, wo)

    dw_out = jnp.concatenate([_matmul_tn(y_conv, dy, "dw_out_conv"), _matmul_tn(y_attn, dy, "dw_out_attn")], axis=0)
    do_t, du_za, delta = _attn_gate_bwd(dycat, o, u)
    dq, dk, dv = _flash_bwd(q, k, v, do_t, lse, delta)
    du_mla, dwq, dwkn, dwv, sg_mla = _mla_bwd(dq, dk, dv, u, pos, freq, q_a_g, wq, kv_a_g, wkn, wv, q_g_pad, k_g_pad)
    du_conv, dconv_w = _conv_bwd(dycat, u, conv_full)
    dw_conv = _matmul_tn(du_conv, h, "dw_in_conv")
    dw_za = _matmul_tn(du_za, h, "dw_in_za")
    dw_mla = _matmul_tn(du_mla, h, "dw_in_mla")

    dw_q_nat = _heads_unpad(dwq).astype(BF16)
    dw_kv_nat = jnp.concatenate([dwkn.reshape(KV_LORA, N_HEADS, QK_NOPE), dwv.reshape(KV_LORA, N_HEADS, V_HEAD)],
                                axis=2).reshape(KV_LORA, N_HEADS * (QK_NOPE + V_HEAD)).astype(BF16)
    grads = [_w_t_from_my(dw_conv, dw_za, dw_mla), _shard_cols(dw_q_nat), _shard_cols(dw_kv_nat),
             dw_out.reshape(N_CHIPS, dw_out.shape[0] // N_CHIPS, D_MODEL)]
    theirs = _rs_core_swap(grads)
    names = ["w_in", "w_q_b", "w_kv_b", "w_out"]
    core = jnp.reshape(mc, (1,)).astype(jnp.int32)
    parts = [_add_half_bf16(g, b, core, "rs_add_" + nm) for g, b, nm in zip(grads, theirs, names)]
    dh, recv = _inproj_bwd(du_conv, du_za, du_mla, w_t, parts)
    recv = [lax.dynamic_update_slice(r, lax.dynamic_slice(p, (chip, 0, 0), (1,) + p.shape[1:]), (chip, 0, 0))
            for r, p in zip(recv, parts)]
    halves = [_sum_chips(p, "rs_sum_" + nm) for p, nm in zip(recv, names)]
    joined = _rs_core_join(halves)
    joined = [lax.dynamic_update_slice(j, hf[None], (mc, 0, 0)) for j, hf in zip(joined, halves)]
    g_big = [j.reshape(2 * j.shape[1], j.shape[2]) for j in joined]
    grad_x, st_in = _prenorm_bwd(x2, dh, dout, norm_g, scale)

    sgrad = jnp.zeros((8, D_MODEL), F32)
    sgrad = sgrad.at[0:2].set(st_in[0:2])
    sgrad = sgrad.at[2].set(st_out[0])
    sgrad = sgrad.at[3].set(st_in[2])
    sgrad = sgrad.at[4, :D_CONV].set(dconv_w[0]).at[4, D_CONV:].set(dconv_w[1])
    sgrad = sgrad.at[5, :D_CONV].set(dconv_w[2]).at[5, D_CONV:].set(sg_mla[0, :D_CONV])
    sgrad = sgrad.at[6, :HEAD_PAD].set(sg_mla[0, SG_KG:SG_KG + HEAD_PAD])
    sgrad = sgrad.at[7].set(st_out[1])
    sg_all, sg_sum = _gather8(sgrad, "gather_small_grads", True)
    loss = sg_sum[7, 0]
    g_ada_b = sg_sum[0:3].reshape(1, 3 * D_MODEL)
    g_norm_g = sg_sum[3:4]
    conv_sum = jnp.stack([sg_sum[4, :D_CONV], sg_sum[4, D_CONV:], sg_sum[5, :D_CONV]])
    g_conv_w = lax.dynamic_slice(conv_sum, (0, chip * cw_cols), (3, cw_cols))
    g_q_a_g = sg_sum[5:6, D_CONV + SG_QAG:D_CONV + SG_QAG + Q_LORA]
    g_kv_a_g = sg_sum[5:6, D_CONV + SG_KVAG:D_CONV + SG_KVAG + KV_LORA]
    g_q_g = sg_sum[5:6, D_CONV + SG_QG:D_CONV + SG_QG + QK_HEAD]
    g_k_g = sg_sum[6:7, :QK_HEAD]
    dmod_k = lax.dynamic_slice(sg_all[:, 0:3, :].reshape(N_DEV, 3 * D_MODEL), (0, chip * ada_cols), (N_DEV, ada_cols))

    g_ada_w, d_ada_w, nm_ada_w, nv_ada_w = _ada_w_update(sc_all, dmod_k, ada_w[0], m_ada_w[0], v_ada_w[0])
    upd = {}
    big = {"w_q_b": (w_q_b, m_w_q_b, v_w_q_b), "w_kv_b": (w_kv_b, m_w_kv_b, v_w_kv_b), "w_out": (w_out, m_w_out, v_w_out)}
    for nm, g in zip(names[1:], g_big[1:]):
        w_, m_, v_ = big[nm]
        upd[nm] = (g,) + tuple(_adamw(w_[0], g, m_[0], v_[0], "adamw_" + nm))
    d_t, nm_t, nv_t, g_t = _adamw(w_in_t, g_big[0], m_w_in_t, v_w_in_t, "adamw_w_in", echo_g=True)
    upd["w_in"] = tuple(jnp.transpose(a) for a in (g_t, d_t, nm_t, nv_t))
    small_w = {"ada_b": (ada_b, m_ada_b, v_ada_b, g_ada_b), "norm_g": (norm_g, m_norm_g, v_norm_g, g_norm_g),
               "conv_w": (conv_w[0], m_conv_w[0], v_conv_w[0], g_conv_w), "q_a_g": (q_a_g, m_q_a_g, v_q_a_g, g_q_a_g),
               "kv_a_g": (kv_a_g, m_kv_a_g, v_kv_a_g, g_kv_a_g), "q_g": (q_g, m_q_g, v_q_g, g_q_g),
               "k_g": (k_g, m_k_g, v_k_g, g_k_g)}
    for nm, (w_, m_, v_, g) in small_w.items():
        upd[nm] = (g,) + tuple(_adamw(w_, g, m_, v_, "adamw_" + nm))
    upd["ada_w"] = (g_ada_w, d_ada_w, nm_ada_w, nv_ada_w)

    order = ["ada_w", "ada_b", "norm_g", "w_in", "conv_w", "q_a_g", "w_q_b", "kv_a_g", "w_kv_b", "q_g", "k_g", "w_out"]
    lead1 = {"ada_w", "w_in", "conv_w", "w_q_b", "w_kv_b", "w_out"}

    def shaped(nm, a):
        return a[None] if nm in lead1 else a

    outs = [loss, grad_x[None]]
    for idx in range(4):
        outs += [shaped(nm, upd[nm][idx]) for nm in order]
    return tuple(outs)
```

```python
import functools
import math

import jax
import jax.numpy as jnp
from jax import lax
from jax.experimental import pallas as pl
from jax.experimental.pallas import tpu as pltpu

F32 = jnp.float32
BF16 = jnp.bfloat16
MESH = pl.DeviceIdType.MESH
SDS = jax.ShapeDtypeStruct
ANY = pl.BlockSpec(memory_space=pl.ANY)

D_MODEL = 2048
D_CONV = 1024
N_HEADS = 8
QK_NOPE = 128
QK_ROPE = 64
QK_HEAD = QK_NOPE + QK_ROPE
V_HEAD = 128
D_ATTN = N_HEADS * V_HEAD
Q_LORA = 512
KV_LORA = 256
ROPE_BASE = 10000.0
EPS = 1e-6
LOG2E = math.log2(math.e)
LN2 = math.log(2.0)
ADAM_LR, ADAM_B1, ADAM_B2, ADAM_EPS, ADAM_WD, ADAM_STEP = 0.001, 0.9, 0.999, 1e-08, 0.01, 10
N_CHIPS = 4
N_DEV = 8

LANES = 128
V7X_VMEM_BYTES = 64 * 1024 * 1024
MIB = 1024 * 1024

HEAD_PAD = 256
Q_PAD = N_HEADS * HEAD_PAD
U_ZA = 4 * D_CONV
U_CQ = U_ZA + D_ATTN
U_CKV = U_CQ + Q_LORA
U_KR = U_CKV + KV_LORA
KR_PAD = 256
U_COLS = U_KR + KR_PAD
MLA_COLS = Q_LORA + KV_LORA + KR_PAD

ATT_T = 512
INPROJ_TM, INPROJ_TN = 1024, 512
ROW_T = 512
OUT_T = 256
DH_TM, DH_TN = 512, 512
TN_TM, TN_TN, TN_TK = 1024, 1024, 2048
ATT_UNROLL = 8


def _cp(sem=None, vmem_mib=None, **kw):
    if sem is not None:
        kw["dimension_semantics"] = sem
    if vmem_mib is not None:
        kw["vmem_limit_bytes"] = min(vmem_mib * MIB, V7X_VMEM_BYTES - 4 * MIB)
    return pltpu.CompilerParams(**kw)


def _sigmoid(z):
    return 1.0 / (1.0 + jnp.exp(-z))


def _silu_grad(z, sg):
    return sg * (1.0 + z * (1.0 - sg))


def _nt(a, b):
    return lax.dot_general(a, b, (((1,), (1,)), ((), ())), preferred_element_type=F32)


def _tn(a, b):
    return lax.dot_general(a, b, (((0,), (0,)), ((), ())), preferred_element_type=F32)


def _nn(a, b):
    return jnp.dot(a, b, preferred_element_type=F32)


def _place():
    return lax.axis_index("x"), lax.axis_index("y"), lax.axis_index("c")


def _gather8(v, name, with_sum):
    rows, cols = v.shape

    def body(v_ref, out_ref, *rest):
        if with_sum:
            sum_ref, send_sems, recv_sems = rest
        else:
            send_sems, recv_sems = rest
        mx, my, mc = _place()
        me = 4 * mx + 2 * my + mc
        out_ref[me] = v_ref[...]
        peers = []
        for d in range(1, N_DEV):
            px = 1 - mx if d & 4 else mx
            py = 1 - my if d & 2 else my
            pc = 1 - mc if d & 1 else mc
            peers.append((px, py, pc))

        def copy(d, slot, to):
            return pltpu.make_async_remote_copy(
                src_ref=v_ref, dst_ref=out_ref.at[slot], send_sem=send_sems.at[d], recv_sem=recv_sems.at[d],
                device_id=to, device_id_type=MESH)

        sends = [copy(d, me, p) for d, p in enumerate(peers)]
        for cp in sends:
            cp.start()
        for d, (px, py, pc) in enumerate(peers):
            copy(d, 4 * px + 2 * py + pc, (px, py, pc)).wait_recv()
        for cp in sends:
            cp.wait_send()
        if with_sum:
            acc = out_ref[0]
            for b in range(1, N_DEV):
                acc = acc + out_ref[b]
            sum_ref[...] = acc

    out_shape = [SDS((N_DEV, rows, cols), F32)]
    if with_sum:
        out_shape.append(SDS((rows, cols), F32))
    vm = pl.BlockSpec(memory_space=pltpu.VMEM)
    return pl.pallas_call(
        body, name=name, out_shape=out_shape, in_specs=[vm], out_specs=[vm] * len(out_shape),
        scratch_shapes=[pltpu.SemaphoreType.DMA((N_DEV - 1,)), pltpu.SemaphoreType.DMA((N_DEV - 1,))],
    )(v)


def _chips_of(mx, my):
    chips = [(mx, 1 - my), (1 - mx, my), (1 - mx, 1 - my)]
    return chips, [2 * px + py for px, py in chips]


def _allgather_shards(shards):
    na = len(shards)
    halves = [s.shape[0] // 2 for s in shards]

    def body(*refs):
        ins, outs = refs[:na], refs[na:2 * na]
        s1, r1, s2, r2 = refs[2 * na:]
        mx, my, mc = _place()
        k = 2 * mx + my
        sib = (mx, my, 1 - mc)
        chips, kks = _chips_of(mx, my)

        def half(a, slot, c):
            return outs[a].at[slot, pl.ds(c * halves[a], halves[a])]

        def mine(a):
            return ins[a].at[pl.ds(mc * halves[a], halves[a])]

        sends = []
        for a in range(na):
            for d, (px, py) in enumerate(chips):
                cp = pltpu.make_async_remote_copy(
                    src_ref=mine(a), dst_ref=half(a, k, mc), send_sem=s1.at[3 * a + d], recv_sem=r1.at[3 * a + d],
                    device_id=(px, py, mc), device_id_type=MESH)
                cp.start()
                sends.append(cp)
        for a in range(na):
            for d, (px, py) in enumerate(chips):
                pltpu.make_async_remote_copy(
                    src_ref=mine(a), dst_ref=half(a, kks[d], mc), send_sem=s1.at[3 * a + d], recv_sem=r1.at[3 * a + d],
                    device_id=(px, py, mc), device_id_type=MESH).wait_recv()
                cp = pltpu.make_async_remote_copy(
                    src_ref=half(a, kks[d], mc), dst_ref=half(a, kks[d], mc), send_sem=s2.at[3 * a + d],
                    recv_sem=r2.at[3 * a + d], device_id=sib, device_id_type=MESH)
                cp.start()
                sends.append(cp)
        for a in range(na):
            for d in range(3):
                pltpu.make_async_remote_copy(
                    src_ref=half(a, kks[d], 1 - mc), dst_ref=half(a, kks[d], 1 - mc), send_sem=s2.at[3 * a + d],
                    recv_sem=r2.at[3 * a + d], device_id=sib, device_id_type=MESH).wait_recv()
        for cp in sends:
            cp.wait_send()

    return pl.pallas_call(
        body, name="allgather_weights",
        out_shape=[SDS((N_CHIPS,) + s.shape, s.dtype) for s in shards],
        in_specs=[ANY] * na, out_specs=[ANY] * na,
        scratch_shapes=[pltpu.SemaphoreType.DMA((3 * na,))] * 4,
    )(*shards)


def _rs_core_swap(grads):
    na = len(grads)
    halves = [g.shape[1] // 2 for g in grads]

    def body(*refs):
        ins, outs = refs[:na], refs[na:2 * na]
        ssem, rsem = refs[2 * na:]
        mx, my, mc = _place()
        sib = (mx, my, 1 - mc)
        sends = []
        for a in range(na):
            cp = pltpu.make_async_remote_copy(
                src_ref=ins[a].at[:, pl.ds((1 - mc) * halves[a], halves[a])], dst_ref=outs[a],
                send_sem=ssem.at[a], recv_sem=rsem.at[a], device_id=sib, device_id_type=MESH)
            cp.start()
            sends.append(cp)
        for cp in sends:
            cp.wait_recv()
        for cp in sends:
            cp.wait_send()

    return pl.pallas_call(
        body, name="rs_core_swap", out_shape=[SDS((N_CHIPS, h) + g.shape[2:], g.dtype) for g, h in zip(grads, halves)],
        in_specs=[ANY] * na, out_specs=[ANY] * na,
        scratch_shapes=[pltpu.SemaphoreType.DMA((na,))] * 2,
    )(*grads)


def _chip_exchange_copies(ins, outs, ssem, rsem):
    mx, my, mc = _place()
    k = 2 * mx + my
    chips, kks = _chips_of(mx, my)
    sends, recvs = [], []
    for a in range(len(ins)):
        for d, (px, py) in enumerate(chips):
            def copy(dst_slot):
                return pltpu.make_async_remote_copy(
                    src_ref=ins[a].at[kks[d]], dst_ref=outs[a].at[dst_slot], send_sem=ssem.at[3 * a + d],
                    recv_sem=rsem.at[3 * a + d], device_id=(px, py, mc), device_id_type=MESH)
            sends.append(copy(k))
            recvs.append(copy(kks[d]))
    return sends, recvs


def _rs_core_join(halves):
    na = len(halves)

    def body(*refs):
        ins, outs = refs[:na], refs[na:2 * na]
        ssem, rsem = refs[2 * na:]
        mx, my, mc = _place()
        sib = (mx, my, 1 - mc)
        sends = []
        for a in range(na):
            cp = pltpu.make_async_remote_copy(
                src_ref=ins[a], dst_ref=outs[a].at[mc], send_sem=ssem.at[a], recv_sem=rsem.at[a],
                device_id=sib, device_id_type=MESH)
            cp.start()
            sends.append(cp)
        for a in range(na):
            pltpu.make_async_remote_copy(
                src_ref=ins[a], dst_ref=outs[a].at[1 - mc], send_sem=ssem.at[a], recv_sem=rsem.at[a],
                device_id=sib, device_id_type=MESH).wait_recv()
        for cp in sends:
            cp.wait_send()

    return pl.pallas_call(
        body, name="rs_core_join", out_shape=[SDS((2,) + h.shape, h.dtype) for h in halves],
        in_specs=[ANY] * na, out_specs=[ANY] * na,
        scratch_shapes=[pltpu.SemaphoreType.DMA((na,))] * 2,
    )(*halves)


def _row_tile(rows, limit, mult=16):
    if rows <= limit:
        return rows
    best = None
    for t in range(mult, limit + 1, mult):
        if rows % t == 0:
            best = t
    assert best is not None, rows
    return best


def _add_half_bf16(g, b, core, name):
    _, h, cols = b.shape
    tb = _row_tile(h, 512)
    nb = h // tb

    def body(core_ref, g_ref, b_ref, o_ref):
        o_ref[...] = (g_ref[...].astype(F32) + b_ref[...].astype(F32)).astype(BF16)

    spec = pl.BlockSpec((None, tb, cols), lambda kk, i, core_ref: (kk, i, 0))
    return pl.pallas_call(
        body, name=name,
        grid_spec=pltpu.PrefetchScalarGridSpec(
            num_scalar_prefetch=1, grid=(N_CHIPS, nb),
            in_specs=[pl.BlockSpec((None, tb, cols), lambda kk, i, core_ref: (kk, core_ref[0] * nb + i, 0)), spec],
            out_specs=spec),
        out_shape=SDS(b.shape, BF16), compiler_params=_cp(("arbitrary", "arbitrary")),
    )(core, g, b)


def _sum_chips(p, name):
    _, rows, cols = p.shape
    tb = _row_tile(rows, 256)

    def body(p_ref, o_ref):
        acc = p_ref[0].astype(F32)
        for j in range(1, N_CHIPS):
            acc = acc + p_ref[j].astype(F32)
        o_ref[...] = acc

    return pl.pallas_call(
        body, name=name, grid=(rows // tb,),
        in_specs=[pl.BlockSpec((N_CHIPS, tb, cols), lambda i: (0, i, 0))],
        out_specs=pl.BlockSpec((tb, cols), lambda i: (i, 0)), out_shape=SDS((rows, cols), F32),
        compiler_params=_cp(("arbitrary",)),
    )(p)


def _adamw_math(w, g, m, v):
    m2 = ADAM_B1 * m + (1.0 - ADAM_B1) * g
    v2 = ADAM_B2 * v + (1.0 - ADAM_B2) * (g * g)
    m_hat = m2 / (1.0 - ADAM_B1 ** ADAM_STEP)
    v_hat = v2 / (1.0 - ADAM_B2 ** ADAM_STEP)
    delta = -ADAM_LR * (m_hat / (jnp.sqrt(v_hat) + ADAM_EPS) + ADAM_WD * w)
    return delta, m2, v2


def _adamw(w, g, m, v, name, echo_g=False):
    rows, cols = w.shape
    tb = _row_tile(rows, 256, mult=8)
    nout = 4 if echo_g else 3

    def body(w_ref, g_ref, m_ref, v_ref, d_ref, m2_ref, v2_ref, *echo):
        gv = g_ref[...]
        d, m2, v2 = _adamw_math(w_ref[...], gv, m_ref[...], v_ref[...])
        d_ref[...] = d
        m2_ref[...] = m2
        v2_ref[...] = v2
        if echo_g:
            echo[0][...] = gv

    spec = pl.BlockSpec((tb, cols), lambda i: (i, 0))
    return pl.pallas_call(
        body, name=name, grid=(rows // tb,), in_specs=[spec] * 4, out_specs=[spec] * nout,
        out_shape=[SDS((rows, cols), F32)] * nout, compiler_params=_cp(("arbitrary",), 40),
    )(w, g, m, v)


def _ada_w_update(sc_all, dmod_k, w, m, v):
    rows, cols = w.shape
    tb = 256

    def body(s_ref, dm_ref, w_ref, m_ref, v_ref, g_ref, d_ref, m2_ref, v2_ref):
        g = _tn(s_ref[...].astype(BF16), dm_ref[...].astype(BF16))
        d, m2, v2 = _adamw_math(w_ref[...], g, m_ref[...], v_ref[...])
        g_ref[...] = g
        d_ref[...] = d
        m2_ref[...] = m2
        v2_ref[...] = v2

    spec = pl.BlockSpec((tb, cols), lambda i: (i, 0))
    return pl.pallas_call(
        body, name="ada_w_update", grid=(rows // tb,),
        in_specs=[pl.BlockSpec((N_DEV, tb), lambda i: (0, i)), pl.BlockSpec((N_DEV, cols), lambda i: (0, 0)), spec, spec, spec],
        out_specs=[spec] * 4, out_shape=[SDS((rows, cols), F32)] * 4, compiler_params=_cp(("arbitrary",), 40),
    )(sc_all, dmod_k, w, m, v)


def _ada_mod(c_all, w, b_k):
    rows, cols = w.shape
    tn = 512

    def body(c_ref, w_ref, b_ref, o_ref, s_ref):
        cv = c_ref[...]
        s = cv * _sigmoid(cv)
        s_ref[...] = s
        o_ref[...] = _nn(s.astype(BF16), w_ref[...].astype(BF16)) + b_ref[...]

    return pl.pallas_call(
        body, name="ada_mod", grid=(cols // tn,),
        in_specs=[pl.BlockSpec((N_DEV, rows), lambda j: (0, 0)), pl.BlockSpec((rows, tn), lambda j: (0, j)),
                  pl.BlockSpec((1, tn), lambda j: (0, j))],
        out_specs=[pl.BlockSpec((N_DEV, tn), lambda j: (0, j)), pl.BlockSpec((N_DEV, rows), lambda j: (0, 0))],
        out_shape=[SDS((N_DEV, cols), F32), SDS((N_DEV, rows), F32)], compiler_params=_cp(("arbitrary",)),
    )(c_all, w, b_k)


def _inproj(x, norm_g, scale, shift, w_t, shards):
    seq, dm = x.shape
    ncols = w_t.shape[0]
    tm, tn = min(INPROJ_TM, seq), INPROJ_TN
    ni, nj = seq // tm, ncols // tn
    na = len(shards)

    def body(x_ref, g_ref, sc_ref, sh_ref, w_ref, *rest):
        shard_refs, h_ref, u_ref, got_refs = rest[:na], rest[na], rest[na + 1], rest[na + 2:2 * na + 2]
        ssem, rsem = rest[2 * na + 2:]
        i, j = pl.program_id(0), pl.program_id(1)
        mx, my, mc = _place()
        k = 2 * mx + my
        chips, kks = _chips_of(mx, my)

        def copy(a, d, slot):
            return pltpu.make_async_remote_copy(
                src_ref=shard_refs[a], dst_ref=got_refs[a].at[slot], send_sem=ssem.at[3 * a + d],
                recv_sem=rsem.at[3 * a + d], device_id=(chips[d][0], chips[d][1], mc), device_id_type=MESH)

        @pl.when((i == 0) & (j == 0))
        def _():
            for a in range(na):
                for d in range(3):
                    copy(a, d, k).start()

        @pl.when(j == 0)
        def _():
            xv = x_ref[...]
            r = lax.rsqrt(jnp.mean(xv * xv, axis=-1, keepdims=True) + EPS)
            hv = (xv * r * g_ref[...]) * (1.0 + sc_ref[...]) + sh_ref[...]
            h_ref[...] = hv.astype(BF16)

        u_ref[...] = _nt(h_ref[...], w_ref[...]).astype(BF16)

        @pl.when((i == ni - 1) & (j == nj - 1))
        def _():
            for a in range(na):
                for d in range(3):
                    copy(a, d, kks[d]).wait_recv()
            for a in range(na):
                for d in range(3):
                    copy(a, d, k).wait_send()

    vec = pl.BlockSpec((1, dm), lambda i, j: (0, 0))
    outs = pl.pallas_call(
        body, name="inproj", grid=(ni, nj),
        in_specs=[pl.BlockSpec((tm, dm), lambda i, j: (i, 0)), vec, vec, vec, pl.BlockSpec((tn, dm), lambda i, j: (j, 0))]
                 + [ANY] * na,
        out_specs=[pl.BlockSpec((tm, dm), lambda i, j: (i, 0)), pl.BlockSpec((tm, tn), lambda i, j: (i, j))] + [ANY] * na,
        out_shape=[SDS((seq, dm), BF16), SDS((seq, ncols), BF16)] + [SDS((N_CHIPS,) + s.shape, s.dtype) for s in shards],
        scratch_shapes=[pltpu.SemaphoreType.DMA((3 * na,))] * 2,
        compiler_params=_cp(("arbitrary", "arbitrary"), 48),
    )(x, norm_g, scale, shift, w_t, *shards)
    return outs[0], outs[1], outs[2:]


HALO = 16
CHUNK_ROWS, CHUNK_LANES = 16, 512


def _conv_fwd(u, conv_w):
    seq = u.shape[0]
    ts = min(ROW_T, seq)
    hb = ts // HALO

    def body(xc_ref, bc_ref, cc_ref, zc_ref, xp_ref, cp_ref, w_ref, y_ref, ext_ref):
        i = pl.program_id(0)
        up = cp_ref[...].astype(F32) * xp_ref[...].astype(F32)
        ext_ref[0:HALO, :] = jnp.where(i > 0, up, 0.0)
        ext_ref[HALO:HALO + ts, :] = cc_ref[...].astype(F32) * xc_ref[...].astype(F32)
        for r0 in range(0, ts, CHUNK_ROWS):
            rows = slice(r0, r0 + CHUNK_ROWS)
            for c0 in range(0, D_CONV, CHUNK_LANES):
                cols = slice(c0, c0 + CHUNK_LANES)
                uc = ext_ref[HALO + r0:HALO + r0 + CHUNK_ROWS, cols]
                u1 = ext_ref[HALO - 1 + r0:HALO - 1 + r0 + CHUNK_ROWS, cols]
                u2 = ext_ref[HALO - 2 + r0:HALO - 2 + r0 + CHUNK_ROWS, cols]
                conv = w_ref[0:1, cols] * u2 + w_ref[1:2, cols] * u1 + w_ref[2:3, cols] * uc
                z = zc_ref[rows, cols].astype(F32)
                y_ref[rows, cols] = ((bc_ref[rows, cols].astype(F32) * conv) * (z * _sigmoid(z))).astype(BF16)

    def col(cb):
        return pl.BlockSpec((ts, D_CONV), lambda i: (i, cb))

    def prev(cb):
        return pl.BlockSpec((HALO, D_CONV), lambda i: (jnp.maximum(i * hb - 1, 0), cb))

    return pl.pallas_call(
        body, name="conv_fwd", grid=(seq // ts,),
        in_specs=[col(0), col(1), col(2), col(3), prev(0), prev(2), pl.BlockSpec((3, D_CONV), lambda i: (0, 0))],
        out_specs=pl.BlockSpec((ts, D_CONV), lambda i: (i, 0)), out_shape=SDS((seq, D_CONV), BF16),
        scratch_shapes=[pltpu.VMEM((ts + HALO, D_CONV), F32)],
        compiler_params=_cp(("arbitrary",), 40),
    )(u, u, u, u, u, u, conv_w)


def _rope_tables(pos_ref, freq_ref):
    ang = pos_ref[...].astype(F32) * freq_ref[...]
    lane = lax.broadcasted_iota(jnp.int32, ang.shape, 1)
    cs, sn = jnp.cos(ang), jnp.sin(ang)
    half = QK_ROPE // 2
    cos_t = jnp.where(lane < QK_ROPE, cs, 0.0)
    sin_lo = jnp.where(lane < half, sn, 0.0)
    sin_hi = jnp.where((lane >= half) & (lane < QK_ROPE), sn, 0.0)
    return cos_t, sin_lo, sin_hi


def _rope(blk, tables):
    cos_t, sin_lo, sin_hi = tables
    half = QK_ROPE // 2
    return blk * cos_t - pltpu.roll(blk, LANES - half, 1) * sin_lo + pltpu.roll(blk, half, 1) * sin_hi


def _rope_bwd(g, tables):
    cos_t, sin_lo, sin_hi = tables
    half = QK_ROPE // 2
    return g * cos_t + pltpu.roll(g, LANES - half, 1) * sin_lo - pltpu.roll(g, half, 1) * sin_hi


def _rms(v, n):
    r = lax.rsqrt(jnp.sum(v * v, axis=-1, keepdims=True) * (1.0 / n) + EPS)
    return v * r, r


def _mla_prep(u, pos, freq, q_a_g, wq, kv_a_g, wkn, wv, q_g, k_g):
    seq = u.shape[0]
    ts = min(ROW_T, seq)
    qscale = LOG2E / math.sqrt(QK_HEAD)

    def body(cq_ref, ckv_ref, kr_ref, pos_ref, freq_ref, qag_ref, wq_ref, kvag_ref, wkn_ref, wv_ref, qg_ref, kg_ref,
             q_ref, k_ref, v_ref):
        tables = _rope_tables(pos_ref, freq_ref)
        cqn, _ = _rms(cq_ref[...].astype(F32), Q_LORA)
        qp = _nn((cqn * qag_ref[...]).astype(BF16), wq_ref[...])
        qg = qg_ref[...]
        for h in range(N_HEADS):
            lo = h * HEAD_PAD
            qn, _ = _rms(qp[:, lo:lo + HEAD_PAD], QK_HEAD)
            qn = qn * qg
            q_ref[:, lo:lo + LANES] = (qn[:, :LANES] * qscale).astype(BF16)
            q_ref[:, lo + LANES:lo + HEAD_PAD] = (_rope(qn[:, LANES:], tables) * qscale).astype(BF16)
        ckvn, _ = _rms(ckv_ref[...].astype(F32), KV_LORA)
        ckvb = (ckvn * kvag_ref[...]).astype(BF16)
        kn = _nn(ckvb, wkn_ref[...])
        v_ref[...] = _nn(ckvb, wv_ref[...]).astype(BF16)
        kr = kr_ref[:, 0:LANES].astype(F32)
        ssr = jnp.sum(kr * kr, axis=-1, keepdims=True)
        kg = kg_ref[...]
        for h in range(N_HEADS):
            knh = kn[:, h * QK_NOPE:(h + 1) * QK_NOPE]
            r = lax.rsqrt((jnp.sum(knh * knh, axis=-1, keepdims=True) + ssr) * (1.0 / QK_HEAD) + EPS)
            lo = h * HEAD_PAD
            k_ref[:, lo:lo + LANES] = (knh * r * kg[:, :LANES]).astype(BF16)
            k_ref[:, lo + LANES:lo + HEAD_PAD] = _rope(kr * r * kg[:, LANES:], tables).astype(BF16)

    def full(a):
        return pl.BlockSpec(a.shape, lambda i: (0,) * a.ndim)

    return pl.pallas_call(
        body, name="mla_prep", grid=(seq // ts,),
        in_specs=[pl.BlockSpec((ts, Q_LORA), lambda i: (i, U_CQ // Q_LORA)),
                  pl.BlockSpec((ts, KV_LORA), lambda i: (i, U_CKV // KV_LORA)),
                  pl.BlockSpec((ts, KR_PAD), lambda i: (i, U_KR // KR_PAD)),
                  pl.BlockSpec((ts, 1), lambda i: (i, 0)), full(freq), full(q_a_g), full(wq), full(kv_a_g), full(wkn),
                  full(wv), full(q_g), full(k_g)],
        out_specs=[pl.BlockSpec((ts, Q_PAD), lambda i: (i, 0)), pl.BlockSpec((ts, Q_PAD), lambda i: (i, 0)),
                   pl.BlockSpec((ts, D_ATTN), lambda i: (i, 0))],
        out_shape=[SDS((seq, Q_PAD), BF16), SDS((seq, Q_PAD), BF16), SDS((seq, D_ATTN), BF16)],
        compiler_params=_cp(("arbitrary",), 48),
    )(u, u, u, pos, freq, q_a_g, wq, kv_a_g, wkn, wv, q_g, k_g)


def _flash_fwd(q, k, v, u):
    seq = q.shape[0]
    t = min(ATT_T, seq)
    n = seq // t
    n_pairs = n * (n + 1) // 2
    za_blk = U_ZA // V_HEAD

    def body(q_ref, k_ref, v_ref, z_ref, o_ref, y_ref, lse_ref, s_a, s_b, top_a, top_b, m_all, l_all, acc_all):
        ones = jnp.ones((16, t), BF16)
        bufs = ((s_a, top_a), (s_b, top_b))

        def rows(i):
            return pl.ds(pl.multiple_of(i * t, t), t)

        def scores(i, j, buf):
            s_ref, top_ref = buf
            s = _nt(k_ref[rows(j), :], q_ref[rows(i), :])
            ahead = lax.broadcasted_iota(jnp.int32, (t, t), 0) - lax.broadcasted_iota(jnp.int32, (t, t), 1)
            s = jnp.where(ahead <= (i - j) * t, s, -jnp.inf)
            s_ref[...] = s
            top_ref[...] = jnp.max(s, axis=0, keepdims=True)

        def absorb(i, j, buf):
            s_ref, top_ref = buf
            first = j == 0
            m = jnp.where(first, -jnp.inf, m_all[i])
            l = jnp.where(first, 0.0, l_all[i])
            acc = jnp.where(first, 0.0, acc_all[i])
            m_new = jnp.maximum(m, top_ref[...])
            alpha = jnp.exp2(m - m_new)
            p = jnp.exp2((s_ref[...] - m_new).astype(BF16))
            m_all[i] = m_new
            l_all[i] = alpha * l + _nn(ones, p)[0:1, :]
            acc_all[i] = alpha * acc + _tn(v_ref[rows(j), :], p)

        def trip(width):
            def walk(_, pair):
                i, j = pair
                for w in range(width):
                    done = j == i
                    ni, nj = jnp.where(done, i + 1, i), jnp.where(done, 0, j + 1)
                    scores(jnp.minimum(ni, n - 1), nj, bufs[(w + 1) % 2])
                    absorb(i, j, bufs[w % 2])
                    i, j = ni, nj
                return i, j
            return walk

        scores(0, 0, bufs[0])
        pair = lax.fori_loop(0, n_pairs // ATT_UNROLL, trip(ATT_UNROLL), (jnp.int32(0), jnp.int32(0)))
        if n_pairs % ATT_UNROLL:
            trip(n_pairs % ATT_UNROLL)(0, pair)

        def finish(i, carry):
            l = l_all[i]
            o = (acc_all[i] * (1.0 / l)).T
            lse_ref[:, rows(i)] = m_all[i] + jnp.log2(l)
            o_ref[rows(i), :] = o.astype(BF16)
            z = z_ref[rows(i), :].astype(F32)
            y_ref[rows(i), :] = (o * (z * _sigmoid(z))).astype(BF16)
            return carry

        lax.fori_loop(0, n, finish, 0)

    def col(width, cb):
        return pl.BlockSpec((seq, width), lambda h: (0, cb + h))

    return pl.pallas_call(
        body, name="flash_fwd", grid=(N_HEADS,),
        in_specs=[col(HEAD_PAD, 0), col(HEAD_PAD, 0), col(V_HEAD, 0), col(V_HEAD, za_blk)],
        out_specs=[col(V_HEAD, 0), col(V_HEAD, 0), pl.BlockSpec((None, 1, seq), lambda h: (h, 0, 0))],
        out_shape=[SDS((seq, D_ATTN), BF16), SDS((seq, D_ATTN), BF16), SDS((N_HEADS, 1, seq), F32)],
        scratch_shapes=[pltpu.VMEM((t, t), F32)] * 2 + [pltpu.VMEM((1, t), F32)] * 2
                       + [pltpu.VMEM((n, 1, t), F32)] * 2 + [pltpu.VMEM((n, V_HEAD, t), F32)],
        compiler_params=_cp(("arbitrary",), 52),
    )(q, k, v, u)


def _outproj_loss(y_conv, y_attn, x, target, gate, w_out):
    seq, dm = x.shape
    ts = min(OUT_T, seq)
    n = seq // ts
    dmix = w_out.shape[0]

    def body(yc_ref, ya_ref, x_ref, t_ref, gate_ref, wo_hbm, dout_ref, dy_ref, dyc_ref, stats_ref, wo_ref, sem, acc_ref):
        i = pl.program_id(0)

        @pl.when(i == 0)
        def _():
            cp = pltpu.make_async_copy(wo_hbm, wo_ref, sem)
            cp.start()
            cp.wait()
            acc_ref[...] = jnp.zeros_like(acc_ref)

        y = _nn(yc_ref[...], wo_ref[0:D_CONV, :]) + _nn(ya_ref[...], wo_ref[D_CONV:dmix, :])
        gate_v = gate_ref[...]
        diff = (x_ref[...] + gate_v * y) - t_ref[...]
        dout = diff * (1.0 / dm)
        dout_ref[...] = dout
        acc_ref[0:8, :] += jnp.sum((dout * y).reshape(ts // 8, 8, dm), axis=0)
        acc_ref[8:16, :] += jnp.sum((diff * diff).reshape(ts // 8, 8, dm), axis=0)
        dy = (dout * gate_v).astype(BF16)
        dy_ref[...] = dy
        dyc_ref[...] = _nt(dy, wo_ref[...]).astype(BF16)

        @pl.when(i == n - 1)
        def _():
            stats_ref[...] = jnp.zeros_like(stats_ref)
            stats_ref[0:1, :] = jnp.sum(acc_ref[0:8, :], axis=0, keepdims=True)
            loss = jnp.sum(acc_ref[8:16, :]) * (0.5 / dm)
            stats_ref[1:2, :] = jnp.full((1, dm), loss, F32)

    row = pl.BlockSpec((ts, dm), lambda i: (i, 0))
    half = pl.BlockSpec((ts, D_CONV), lambda i: (i, 0))
    return pl.pallas_call(
        body, name="outproj_loss", grid=(n,),
        in_specs=[half, half, row, row, pl.BlockSpec((1, dm), lambda i: (0, 0)), ANY],
        out_specs=[row, row, pl.BlockSpec((ts, dmix), lambda i: (i, 0)), pl.BlockSpec((8, dm), lambda i: (0, 0))],
        out_shape=[SDS((seq, dm), F32), SDS((seq, dm), BF16), SDS((seq, dmix), BF16), SDS((8, dm), F32)],
        scratch_shapes=[pltpu.VMEM(w_out.shape, BF16), pltpu.SemaphoreType.DMA(()), pltpu.VMEM((16, dm), F32)],
        compiler_params=_cp(("arbitrary",), 52),
    )(y_conv, y_attn, x, target, gate, w_out)


def _matmul_tn(a, b, name):
    seq, m = a.shape
    n = b.shape[1]
    tm, tn, tk = min(TN_TM, m), min(TN_TN, n), min(TN_TK, seq)
    nk = seq // tk

    def body(a_ref, b_ref, o_ref, acc_ref):
        kk = pl.program_id(2)

        @pl.when(kk == 0)
        def _():
            acc_ref[...] = jnp.zeros_like(acc_ref)

        acc_ref[...] += _tn(a_ref[...], b_ref[...])

        @pl.when(kk == nk - 1)
        def _():
            o_ref[...] = acc_ref[...].astype(BF16)

    return pl.pallas_call(
        body, name=name, grid=(m // tm, n // tn, nk),
        in_specs=[pl.BlockSpec((tk, tm), lambda i, j, kk: (kk, i)), pl.BlockSpec((tk, tn), lambda i, j, kk: (kk, j))],
        out_specs=pl.BlockSpec((tm, tn), lambda i, j, kk: (i, j)), out_shape=SDS((m, n), BF16),
        scratch_shapes=[pltpu.VMEM((tm, tn), F32)],
        compiler_params=_cp(("arbitrary", "arbitrary", "arbitrary"), 40),
    )(a, b)


def _attn_gate_bwd(dycat, o, u):
    seq = o.shape[0]
    ts = min(ROW_T, seq)

    def body(dy_ref, o_ref, z_ref, dot_ref, dz_ref, dl_ref):
        dy = dy_ref[...].astype(F32)
        ov = o_ref[...].astype(F32)
        z = z_ref[...].astype(F32)
        sg = _sigmoid(z)
        do = dy * (z * sg)
        dz_ref[...] = (dy * ov * _silu_grad(z, sg)).astype(BF16)
        prod = do * ov
        ones = jnp.ones((8, V_HEAD), F32)
        for h in range(N_HEADS):
            cols = slice(h * V_HEAD, (h + 1) * V_HEAD)
            dot_ref[h] = do[:, cols].T.astype(BF16)
            rows = lax.dot_general(ones, prod[:, cols], (((1,), (1,)), ((), ())), precision=lax.Precision.HIGHEST,
                                   preferred_element_type=F32)
            dl_ref[h] = rows[0:1, :]

    blk = pl.BlockSpec((ts, D_ATTN), lambda i: (i, 0))
    return pl.pallas_call(
        body, name="attn_gate_bwd", grid=(seq // ts,),
        in_specs=[pl.BlockSpec((ts, D_ATTN), lambda i: (i, 1)), blk, pl.BlockSpec((ts, D_ATTN), lambda i: (i, U_ZA // D_ATTN))],
        out_specs=[pl.BlockSpec((N_HEADS, V_HEAD, ts), lambda i: (0, 0, i)), blk,
                   pl.BlockSpec((N_HEADS, 1, ts), lambda i: (0, 0, i))],
        out_shape=[SDS((N_HEADS, V_HEAD, seq), BF16), SDS((seq, D_ATTN), BF16), SDS((N_HEADS, 1, seq), F32)],
        compiler_params=_cp(("arbitrary",), 40),
    )(dycat, o, u)


def _flash_bwd(q, k, v, do_t, lse, delta):
    seq = q.shape[0]
    t = min(ATT_T, seq)
    n = seq // t
    n_pairs = n * (n + 1) // 2

    def body(k_ref, v_ref, q_ref, dot_ref, lse_ref, dl_ref, dq_ref, dk_ref, dv_ref, s_a, s_b, dp_a, dp_b, dq_acc, dk_acc,
             dvt_acc):
        bufs = ((s_a, dp_a), (s_b, dp_b))

        def rows(i):
            return pl.ds(pl.multiple_of(i * t, t), t)

        def products(j, i, buf):
            s_ref, dp_ref = buf
            s = _nt(k_ref[rows(j), :], q_ref[rows(i), :])
            ahead = lax.broadcasted_iota(jnp.int32, (t, t), 0) - lax.broadcasted_iota(jnp.int32, (t, t), 1)
            s_ref[...] = jnp.where(ahead <= (i - j) * t, s, -jnp.inf)
            dp_ref[...] = _nn(v_ref[rows(j), :], dot_ref[:, rows(i)])

        def absorb(j, i, buf):
            s_ref, dp_ref = buf
            first = i == j
            p = jnp.exp2((s_ref[...] - lse_ref[:, rows(i)]).astype(BF16))
            dvt = jnp.where(first, 0.0, dvt_acc[...]) + _nt(dot_ref[:, rows(i)], p)
            ds = p * (dp_ref[...] - dl_ref[:, rows(i)]).astype(BF16)
            dk = jnp.where(first, 0.0, dk_acc[...]) + _nn(ds, q_ref[rows(i), :])
            dq_acc[rows(i), :] += _tn(ds, k_ref[rows(j), :])
            dvt_acc[...] = dvt
            dk_acc[...] = dk
            dk_ref[rows(j), :] = dk.astype(BF16)
            dv_ref[rows(j), :] = dvt.T.astype(BF16)

        def trip(width):
            def walk(_, pair):
                j, i = pair
                for w in range(width):
                    done = i == n - 1
                    nj = jnp.where(done, j + 1, j)
                    ni = jnp.where(done, j + 1, i + 1)
                    absorb(j, i, bufs[w % 2])
                    products(jnp.minimum(nj, n - 1), jnp.minimum(ni, n - 1), bufs[(w + 1) % 2])
                    j, i = nj, ni
                return j, i
            return walk

        dq_acc[...] = jnp.zeros_like(dq_acc)
        products(0, 0, bufs[0])
        pair = lax.fori_loop(0, n_pairs // ATT_UNROLL, trip(ATT_UNROLL), (jnp.int32(0), jnp.int32(0)))
        if n_pairs % ATT_UNROLL:
            trip(n_pairs % ATT_UNROLL)(0, pair)

        def finish(i, carry):
            dq_ref[rows(i), :] = dq_acc[rows(i), :].astype(BF16)
            return carry

        lax.fori_loop(0, n, finish, 0)

    def col(width):
        return pl.BlockSpec((seq, width), lambda h: (0, h))

    row = pl.BlockSpec((None, 1, seq), lambda h: (h, 0, 0))
    return pl.pallas_call(
        body, name="flash_bwd", grid=(N_HEADS,),
        in_specs=[col(HEAD_PAD), col(V_HEAD), col(HEAD_PAD), pl.BlockSpec((None, V_HEAD, seq), lambda h: (h, 0, 0)), row, row],
        out_specs=[col(HEAD_PAD), col(HEAD_PAD), col(V_HEAD)],
        out_shape=[SDS((seq, Q_PAD), BF16), SDS((seq, Q_PAD), BF16), SDS((seq, D_ATTN), BF16)],
        scratch_shapes=[pltpu.VMEM((t, t), F32)] * 4
                       + [pltpu.VMEM((seq, HEAD_PAD), F32), pltpu.VMEM((t, HEAD_PAD), F32), pltpu.VMEM((V_HEAD, t), F32)],
        compiler_params=_cp(("arbitrary",), 60),
    )(k, v, q, do_t, lse, delta)


SG_QAG, SG_KVAG, SG_QG, SG_KG, SG_COLS = 0, Q_LORA, Q_LORA + KV_LORA, Q_LORA + KV_LORA + HEAD_PAD, D_MODEL


def _mla_bwd(dq, dk, dv, u, pos, freq, q_a_g, wq, kv_a_g, wkn, wv, q_g, k_g):
    seq = u.shape[0]
    ts = min(ROW_T, seq)
    n = seq // ts
    qscale = 1.0 / math.sqrt(QK_HEAD)

    def body(dq_ref, dk_ref, dv_ref, cq_ref, ckv_ref, kr_ref, pos_ref, freq_ref, qag_ref, wq_ref, kvag_ref, wkn_ref,
             wv_ref, qg_ref, kg_ref, du_ref, dwq_ref, dwkn_ref, dwv_ref, sg_ref, dqp_ref, dkn_ref):
        i = pl.program_id(0)

        @pl.when(i == 0)
        def _():
            dwq_ref[...] = jnp.zeros_like(dwq_ref)
            dwkn_ref[...] = jnp.zeros_like(dwkn_ref)
            dwv_ref[...] = jnp.zeros_like(dwv_ref)
            sg_ref[...] = jnp.zeros_like(sg_ref)

        tables = _rope_tables(pos_ref, freq_ref)

        cq = cq_ref[...].astype(F32)
        cqn, rq = _rms(cq, Q_LORA)
        qag = qag_ref[...]
        cqb = (cqn * qag).astype(BF16)
        qp = _nn(cqb, wq_ref[...])
        qg = qg_ref[...]
        dqg = jnp.zeros((1, HEAD_PAD), F32)
        for h in range(N_HEADS):
            lo = h * HEAD_PAD
            xn, r = _rms(qp[:, lo:lo + HEAD_PAD], QK_HEAD)
            g = jnp.concatenate([dq_ref[:, lo:lo + LANES].astype(F32),
                                 _rope_bwd(dq_ref[:, lo + LANES:lo + HEAD_PAD].astype(F32), tables)], axis=-1) * qscale
            dqg = dqg + jnp.sum(g * xn, axis=0, keepdims=True)
            gy = g * qg
            mean = jnp.sum(gy * xn, axis=-1, keepdims=True) * (1.0 / QK_HEAD)
            dqp_ref[:, lo:lo + HEAD_PAD] = (r * (gy - xn * mean)).astype(BF16)
        dqp = dqp_ref[...]
        dwq_ref[...] += _tn(cqb, dqp)
        dcqn = _nt(dqp, wq_ref[...])
        sg_ref[0:1, SG_QAG:SG_QAG + Q_LORA] += jnp.sum(dcqn * cqn, axis=0, keepdims=True)
        sg_ref[0:1, SG_QG:SG_QG + HEAD_PAD] += dqg
        gy = dcqn * qag
        mean = jnp.sum(gy * cqn, axis=-1, keepdims=True) * (1.0 / Q_LORA)
        du_ref[:, 0:Q_LORA] = (rq * (gy - cqn * mean)).astype(BF16)

        ckv = ckv_ref[...].astype(F32)
        ckvn, rkv = _rms(ckv, KV_LORA)
        kvag = kvag_ref[...]
        ckvb = (ckvn * kvag).astype(BF16)
        kn = _nn(ckvb, wkn_ref[...])
        kr = kr_ref[:, 0:LANES].astype(F32)
        ssr = jnp.sum(kr * kr, axis=-1, keepdims=True)
        kg = kg_ref[...]
        kg_n, kg_r = kg[:, :LANES] * LN2, kg[:, LANES:] * LN2
        dkg_n = jnp.zeros((1, LANES), F32)
        dkg_r = jnp.zeros((1, LANES), F32)
        dkr = jnp.zeros((ts, LANES), F32)
        for h in range(N_HEADS):
            knh = kn[:, h * QK_NOPE:(h + 1) * QK_NOPE]
            r = lax.rsqrt((jnp.sum(knh * knh, axis=-1, keepdims=True) + ssr) * (1.0 / QK_HEAD) + EPS)
            xn_n, xn_r = knh * r, kr * r
            lo = h * HEAD_PAD
            g_n = dk_ref[:, lo:lo + LANES].astype(F32)
            g_r = _rope_bwd(dk_ref[:, lo + LANES:lo + HEAD_PAD].astype(F32), tables)
            dkg_n = dkg_n + jnp.sum(g_n * xn_n, axis=0, keepdims=True)
            dkg_r = dkg_r + jnp.sum(g_r * xn_r, axis=0, keepdims=True)
            gy_n, gy_r = g_n * kg_n, g_r * kg_r
            mean = (jnp.sum(gy_n * xn_n, axis=-1, keepdims=True) + jnp.sum(gy_r * xn_r, axis=-1, keepdims=True)) * (1.0 / QK_HEAD)
            dkn_ref[:, h * QK_NOPE:(h + 1) * QK_NOPE] = (r * (gy_n - xn_n * mean)).astype(BF16)
            dkr = dkr + r * (gy_r - xn_r * mean)
        dkn = dkn_ref[...]
        dvv = dv_ref[...]
        dwkn_ref[...] += _tn(ckvb, dkn)
        dwv_ref[...] += _tn(ckvb, dvv)
        dckvn = _nt(dkn, wkn_ref[...]) + _nt(dvv, wv_ref[...])
        sg_ref[0:1, SG_KVAG:SG_KVAG + KV_LORA] += jnp.sum(dckvn * ckvn, axis=0, keepdims=True)
        sg_ref[0:1, SG_KG:SG_KG + LANES] += dkg_n * LN2
        sg_ref[0:1, SG_KG + LANES:SG_KG + HEAD_PAD] += dkg_r * LN2
        gy = dckvn * kvag
        mean = jnp.sum(gy * ckvn, axis=-1, keepdims=True) * (1.0 / KV_LORA)
        du_ref[:, Q_LORA:Q_LORA + KV_LORA] = (rkv * (gy - ckvn * mean)).astype(BF16)
        du_ref[:, Q_LORA + KV_LORA:Q_LORA + KV_LORA + LANES] = dkr.astype(BF16)
        du_ref[:, Q_LORA + KV_LORA + LANES:MLA_COLS] = jnp.zeros((ts, LANES), BF16)

    def full(a):
        return pl.BlockSpec(a.shape, lambda i: (0,) * a.ndim)

    wide = pl.BlockSpec((ts, Q_PAD), lambda i: (i, 0))
    return pl.pallas_call(
        body, name="mla_bwd", grid=(n,),
        in_specs=[wide, wide, pl.BlockSpec((ts, D_ATTN), lambda i: (i, 0)),
                  pl.BlockSpec((ts, Q_LORA), lambda i: (i, U_CQ // Q_LORA)),
                  pl.BlockSpec((ts, KV_LORA), lambda i: (i, U_CKV // KV_LORA)),
                  pl.BlockSpec((ts, KR_PAD), lambda i: (i, U_KR // KR_PAD)),
                  pl.BlockSpec((ts, 1), lambda i: (i, 0)), full(freq), full(q_a_g), full(wq), full(kv_a_g), full(wkn),
                  full(wv), full(q_g), full(k_g)],
        out_specs=[pl.BlockSpec((ts, MLA_COLS), lambda i: (i, 0)), pl.BlockSpec((Q_LORA, Q_PAD), lambda i: (0, 0)),
                   pl.BlockSpec((KV_LORA, D_ATTN), lambda i: (0, 0)), pl.BlockSpec((KV_LORA, D_ATTN), lambda i: (0, 0)),
                   pl.BlockSpec((8, SG_COLS), lambda i: (0, 0))],
        out_shape=[SDS((seq, MLA_COLS), BF16), SDS((Q_LORA, Q_PAD), F32), SDS((KV_LORA, D_ATTN), F32),
                   SDS((KV_LORA, D_ATTN), F32), SDS((8, SG_COLS), F32)],
        scratch_shapes=[pltpu.VMEM((ts, Q_PAD), BF16), pltpu.VMEM((ts, D_ATTN), BF16)],
        compiler_params=_cp(("arbitrary",), 56),
    )(dq, dk, dv, u, u, u, pos, freq, q_a_g, wq, kv_a_g, wkn, wv, q_g, k_g)


def _conv_bwd(dycat, u, conv_w):
    seq = u.shape[0]
    ts = min(ROW_T, seq)
    n = seq // ts
    hb = ts // HALO

    def body(dy_ref, xc_ref, bc_ref, cc_ref, zc_ref, xp_ref, cp_ref, dyn_ref, bn_ref, zn_ref, w_ref,
             du_ref, dw_ref, ext_ref, dext_ref, acc_ref):
        i = pl.program_id(0)

        @pl.when(i == 0)
        def _():
            dw_ref[...] = jnp.zeros_like(dw_ref)

        up = cp_ref[...].astype(F32) * xp_ref[...].astype(F32)
        ext_ref[0:HALO, :] = jnp.where(i > 0, up, 0.0)
        ext_ref[HALO:HALO + ts, :] = cc_ref[...].astype(F32) * xc_ref[...].astype(F32)
        zn = zn_ref[...].astype(F32)
        dnext = dyn_ref[...].astype(F32) * (zn * _sigmoid(zn)) * bn_ref[...].astype(F32)
        dext_ref[ts:ts + HALO, :] = jnp.where(i < n - 1, dnext, 0.0)
        acc_ref[...] = jnp.zeros_like(acc_ref)

        for r0 in range(0, ts, CHUNK_ROWS):
            rows = slice(r0, r0 + CHUNK_ROWS)
            for c0 in range(0, D_CONV, CHUNK_LANES):
                cols = slice(c0, c0 + CHUNK_LANES)
                uc = ext_ref[HALO + r0:HALO + r0 + CHUNK_ROWS, cols]
                u1 = ext_ref[HALO - 1 + r0:HALO - 1 + r0 + CHUNK_ROWS, cols]
                u2 = ext_ref[HALO - 2 + r0:HALO - 2 + r0 + CHUNK_ROWS, cols]
                conv = w_ref[0:1, cols] * u2 + w_ref[1:2, cols] * u1 + w_ref[2:3, cols] * uc
                z = zc_ref[rows, cols].astype(F32)
                sg = _sigmoid(z)
                sz = z * sg
                b = bc_ref[rows, cols].astype(F32)
                dy = dy_ref[rows, cols].astype(F32)
                du_ref[rows, 3 * D_CONV + c0:3 * D_CONV + c0 + CHUNK_LANES] = (dy * (b * conv) * _silu_grad(z, sg)).astype(BF16)
                du_ref[rows, D_CONV + c0:D_CONV + c0 + CHUNK_LANES] = (dy * sz * conv).astype(BF16)
                dconv = dy * sz * b
                dext_ref[rows, cols] = dconv
                acc_ref[0:CHUNK_ROWS, cols] += dconv * u2
                acc_ref[CHUNK_ROWS:2 * CHUNK_ROWS, cols] += dconv * u1
                acc_ref[2 * CHUNK_ROWS:3 * CHUNK_ROWS, cols] += dconv * uc
        for r0 in range(0, ts, CHUNK_ROWS):
            rows = slice(r0, r0 + CHUNK_ROWS)
            for c0 in range(0, D_CONV, CHUNK_LANES):
                cols = slice(c0, c0 + CHUNK_LANES)
                du = (w_ref[2:3, cols] * dext_ref[rows, cols] + w_ref[1:2, cols] * dext_ref[r0 + 1:r0 + 1 + CHUNK_ROWS, cols]
                      + w_ref[0:1, cols] * dext_ref[r0 + 2:r0 + 2 + CHUNK_ROWS, cols])
                du_ref[rows, 2 * D_CONV + c0:2 * D_CONV + c0 + CHUNK_LANES] = (du * xc_ref[rows, cols].astype(F32)).astype(BF16)
                du_ref[rows, c0:c0 + CHUNK_LANES] = (du * cc_ref[rows, cols].astype(F32)).astype(BF16)
        for k in range(3):
            dw_ref[k:k + 1, :] += jnp.sum(acc_ref[k * CHUNK_ROWS:(k + 1) * CHUNK_ROWS, :], axis=0, keepdims=True)

    def col(cb):
        return pl.BlockSpec((ts, D_CONV), lambda i: (i, cb))

    def prev(cb):
        return pl.BlockSpec((HALO, D_CONV), lambda i: (jnp.maximum(i * hb - 1, 0), cb))

    def nxt(cb):
        return pl.BlockSpec((HALO, D_CONV), lambda i: (jnp.minimum((i + 1) * hb, n * hb - 1), cb))

    return pl.pallas_call(
        body, name="conv_bwd", grid=(n,),
        in_specs=[col(0), col(0), col(1), col(2), col(3), prev(0), prev(2), nxt(0), nxt(1), nxt(3),
                  pl.BlockSpec((3, D_CONV), lambda i: (0, 0))],
        out_specs=[pl.BlockSpec((ts, 4 * D_CONV), lambda i: (i, 0)), pl.BlockSpec((8, D_CONV), lambda i: (0, 0))],
        out_shape=[SDS((seq, 4 * D_CONV), BF16), SDS((8, D_CONV), F32)],
        scratch_shapes=[pltpu.VMEM((ts + HALO, D_CONV), F32), pltpu.VMEM((ts + HALO, D_CONV), F32),
                        pltpu.VMEM((3 * CHUNK_ROWS, D_CONV), F32)],
        compiler_params=_cp(("arbitrary",), 48),
    )(dycat, u, u, u, u, u, u, dycat, u, u, conv_w)


def _inproj_bwd(du_conv, du_za, du_mla, w_t, parts):
    seq = du_conv.shape[0]
    dm = w_t.shape[1]
    tm, tn = min(DH_TM, seq), DH_TN
    ni, nj = seq // tm, dm // tn
    na = len(parts)

    def body(dc_ref, dz_ref, dm_ref, w_ref, *rest):
        part_refs, o_ref, recv_refs = rest[:na], rest[na], rest[na + 1:2 * na + 1]
        ssem, rsem = rest[2 * na + 1:]
        i, j = pl.program_id(0), pl.program_id(1)
        sends, recvs = _chip_exchange_copies(part_refs, recv_refs, ssem, rsem)

        @pl.when((i == 0) & (j == 0))
        def _():
            for cp in sends:
                cp.start()

        acc = _nn(dc_ref[...], w_ref[0:U_ZA, :])
        acc = acc + _nn(dz_ref[...], w_ref[U_ZA:U_CQ, :])
        acc = acc + _nn(dm_ref[...], w_ref[U_CQ:U_COLS, :])
        o_ref[...] = acc

        @pl.when((i == ni - 1) & (j == nj - 1))
        def _():
            for cp in recvs:
                cp.wait_recv()
            for cp in sends:
                cp.wait_send()

    outs = pl.pallas_call(
        body, name="inproj_bwd", grid=(ni, nj),
        in_specs=[pl.BlockSpec((tm, U_ZA), lambda i, j: (i, 0)), pl.BlockSpec((tm, D_ATTN), lambda i, j: (i, 0)),
                  pl.BlockSpec((tm, MLA_COLS), lambda i, j: (i, 0)), pl.BlockSpec((U_COLS, tn), lambda i, j: (0, j))]
                 + [ANY] * na,
        out_specs=[pl.BlockSpec((tm, tn), lambda i, j: (i, j))] + [ANY] * na,
        out_shape=[SDS((seq, dm), F32)] + [SDS(p.shape, p.dtype) for p in parts],
        scratch_shapes=[pltpu.SemaphoreType.DMA((3 * na,))] * 2,
        compiler_params=_cp(("arbitrary", "arbitrary"), 48),
    )(du_conv, du_za, du_mla, w_t, *parts)
    return outs[0], outs[1:]


def _prenorm_bwd(x, dh, dout, norm_g, scale):
    seq, dm = x.shape
    ts = min(ROW_T, seq)
    n = seq // ts

    def body(x_ref, dh_ref, dout_ref, g_ref, sc_ref, gx_ref, st_ref, acc_ref):
        i = pl.program_id(0)

        @pl.when(i == 0)
        def _():
            acc_ref[...] = jnp.zeros_like(acc_ref)

        xv = x_ref[...]
        xn, r = _rms(xv, dm)
        dh_v = dh_ref[...]
        gv = g_ref[...]
        one_sc = 1.0 + sc_ref[...]

        def fold(a):
            return jnp.sum(a.reshape(ts // 8, 8, dm), axis=0)

        acc_ref[0:8, :] += fold(dh_v)
        acc_ref[8:16, :] += fold(dh_v * (xn * gv))
        dxg = dh_v * one_sc
        acc_ref[16:24, :] += fold(dxg * xn)
        dxn = dxg * gv
        mean = jnp.sum(dxn * xn, axis=-1, keepdims=True) * (1.0 / dm)
        gx_ref[...] = dout_ref[...] + r * (dxn - xn * mean)

        @pl.when(i == n - 1)
        def _():
            st_ref[...] = jnp.zeros_like(st_ref)
            for k in range(3):
                st_ref[k:k + 1, :] = jnp.sum(acc_ref[8 * k:8 * k + 8, :], axis=0, keepdims=True)

    row = pl.BlockSpec((ts, dm), lambda i: (i, 0))
    vec = pl.BlockSpec((1, dm), lambda i: (0, 0))
    return pl.pallas_call(
        body, name="prenorm_bwd", grid=(n,), in_specs=[row, row, row, vec, vec],
        out_specs=[row, pl.BlockSpec((8, dm), lambda i: (0, 0))],
        out_shape=[SDS((seq, dm), F32), SDS((8, dm), F32)],
        scratch_shapes=[pltpu.VMEM((24, dm), F32)], input_output_aliases={2: 0},
        compiler_params=_cp(("arbitrary",), 52),
    )(x, dh, dout, norm_g, scale)


def _unshard_cols(g):
    return jnp.transpose(g, (1, 0, 2)).reshape(g.shape[1], -1)


def _shard_cols(w):
    r = w.shape[0]
    return jnp.transpose(w.reshape(r, N_CHIPS, -1), (1, 0, 2))


W_IN_COLS = 4 * D_CONV + Q_LORA + KV_LORA + QK_ROPE + D_ATTN
SHARD_ROWS = W_IN_COLS // N_CHIPS
SHARD_PAD = 1536


def _w_in_pieces():
    c4 = 4 * D_CONV
    groups = [(0, c4, 0), (c4, c4 + Q_LORA, U_CQ), (c4 + Q_LORA, c4 + Q_LORA + KV_LORA, U_CKV),
              (c4 + Q_LORA + KV_LORA, W_IN_COLS - D_ATTN, U_KR), (W_IN_COLS - D_ATTN, W_IN_COLS, U_ZA)]
    pieces = []
    for lo, hi, my in groups:
        for chip in range(N_CHIPS):
            a, b = max(lo, chip * SHARD_ROWS), min(hi, (chip + 1) * SHARD_ROWS)
            if a < b:
                pieces.append((chip, a - chip * SHARD_ROWS, b - a, my + a - lo))
    return pieces


def _w_t_to_my(g):
    w = jnp.zeros((U_COLS, g.shape[2]), g.dtype)
    for chip, row, n, my in _w_in_pieces():
        w = lax.dynamic_update_slice(w, g[chip, row:row + n], (my, 0))
    return w


def _w_t_from_my(g_conv, g_za, g_mla):
    w = jnp.zeros((N_CHIPS, SHARD_PAD, g_conv.shape[1]), g_conv.dtype)
    for chip, row, n, my in _w_in_pieces():
        src, base = (g_conv, 0) if my < U_ZA else (g_za, U_ZA) if my < U_CQ else (g_mla, U_CQ)
        w = lax.dynamic_update_slice(w, src[my - base:my - base + n][None], (chip, row, 0))
    return w


def _heads_pad(w):
    r = w.shape[0]
    w3 = w.reshape(r, N_HEADS, QK_HEAD)
    return jnp.pad(w3, ((0, 0), (0, 0), (0, HEAD_PAD - QK_HEAD))).reshape(r, Q_PAD)


def _heads_unpad(w):
    r = w.shape[0]
    return w.reshape(r, N_HEADS, HEAD_PAD)[:, :, :QK_HEAD].reshape(r, N_HEADS * QK_HEAD)


def kernel(x, c, positions, ada_w, ada_b, norm_g, w_in, conv_w, q_a_g, w_q_b, kv_a_g, w_kv_b, q_g, k_g, w_out, loss_target, m_ada_w, m_ada_b, m_norm_g, m_w_in, m_conv_w, m_q_a_g, m_w_q_b, m_kv_a_g, m_w_kv_b, m_q_g, m_k_g, m_w_out, v_ada_w, v_ada_b, v_norm_g, v_w_in, v_conv_w, v_q_a_g, v_w_q_b, v_kv_a_g, v_w_kv_b, v_q_g, v_k_g, v_w_out):
    mx, my, mc = _place()
    chip = 2 * mx + my
    me = 2 * chip + mc
    seq = x.shape[1]
    x2, t2 = x[0], loss_target[0]
    cw_cols = conv_w.shape[2]

    small = jnp.zeros((8, D_MODEL), F32)
    small = small.at[0].set(c[0])
    small = small.at[1:4, :cw_cols].set(conv_w[0])
    small_all = _gather8(small, "gather_c_conv", False)[0]
    c_all = small_all[:, 0, :]
    conv_full = jnp.transpose(small_all.reshape(N_CHIPS, 2, 8, D_MODEL)[:, 0, 1:4, :cw_cols], (1, 0, 2)).reshape(3, D_CONV)

    ada_cols = ada_w.shape[2]
    b_k = lax.dynamic_slice(ada_b, (0, chip * ada_cols), (1, ada_cols))
    mod_k, sc_all = _ada_mod(c_all, ada_w[0], b_k)
    mod_all = _gather8(mod_k, "gather_mod", False)[0]
    mod_row = lax.dynamic_slice(mod_all.reshape(N_CHIPS, 2, N_DEV, ada_cols), (0, mc, me, 0), (N_CHIPS, 1, 1, ada_cols))
    mod_row = mod_row.reshape(3, D_MODEL)
    shift, scale, gate = mod_row[0:1], mod_row[1:2], mod_row[2:3]

    def own_slot(g, s):
        return lax.dynamic_update_slice(g, s[None], (chip, 0, 0))

    w_in_t, m_w_in_t, v_w_in_t = [jnp.transpose(a[0]) for a in (w_in, m_w_in, v_w_in)]
    shard_in = jnp.pad(w_in_t.astype(BF16), ((0, SHARD_PAD - SHARD_ROWS), (0, 0)))
    g_in = own_slot(_allgather_shards([shard_in])[0], shard_in)
    w_t = _w_t_to_my(g_in)

    later = [w_q_b[0].astype(BF16), w_kv_b[0].astype(BF16), w_out[0].astype(BF16)]
    h, u, got = _inproj(x2, norm_g, scale, shift, w_t, later)
    g_q, g_kv, g_out = [own_slot(g, s) for g, s in zip(got, later)]
    wq = _heads_pad(_unshard_cols(g_q))
    wkv = _unshard_cols(g_kv).reshape(KV_LORA, N_HEADS, QK_NOPE + V_HEAD)
    wkn = wkv[:, :, :QK_NOPE].reshape(KV_LORA, N_HEADS * QK_NOPE)
    wv = wkv[:, :, QK_NOPE:].reshape(KV_LORA, D_ATTN)
    wo = g_out.reshape(N_CHIPS * g_out.shape[1], D_MODEL)
    y_conv = _conv_fwd(u, conv_full)
    pos = positions.reshape(seq, 1)
    inv_freq = ROPE_BASE ** (-jnp.arange(0, QK_ROPE, 2, dtype=F32) / QK_ROPE)
    freq = jnp.concatenate([inv_freq, inv_freq, jnp.zeros((LANES - QK_ROPE,), F32)]).reshape(1, LANES)
    q_g_pad = jnp.pad(q_g, ((0, 0), (0, HEAD_PAD - QK_HEAD)))
    k_g_pad = jnp.pad(k_g, ((0, 0), (0, HEAD_PAD - QK_HEAD)))
    q, k, v = _mla_prep(u, pos, freq, q_a_g, wq, kv_a_g, wkn, wv, q_g_pad, k_g_pad)
    o, y_attn, lse = _flash_fwd(q, k, v, u)
    dout, dy, dycat, st_out = _outproj_loss(y_conv, y_attn, x2, t2, gate, wo)

    dw_out = jnp.concatenate([_matmul_tn(y_conv, dy, "dw_out_conv"), _matmul_tn(y_attn, dy, "dw_out_attn")], axis=0)
    do_t, du_za, delta = _attn_gate_bwd(dycat, o, u)
    dq, dk, dv = _flash_bwd(q, k, v, do_t, lse, delta)
    du_mla, dwq, dwkn, dwv, sg_mla = _mla_bwd(dq, dk, dv, u, pos, freq, q_a_g, wq, kv_a_g, wkn, wv, q_g_pad, k_g_pad)
    du_conv, dconv_w = _conv_bwd(dycat, u, conv_full)
    dw_conv = _matmul_tn(du_conv, h, "dw_in_conv")
    dw_za = _matmul_tn(du_za, h, "dw_in_za")
    dw_mla = _matmul_tn(du_mla, h, "dw_in_mla")

    dw_q_nat = _heads_unpad(dwq).astype(BF16)
    dw_kv_nat = jnp.concatenate([dwkn.reshape(KV_LORA, N_HEADS, QK_NOPE), dwv.reshape(KV_LORA, N_HEADS, V_HEAD)],
                                axis=2).reshape(KV_LORA, N_HEADS * (QK_NOPE + V_HEAD)).astype(BF16)
    grads = [_w_t_from_my(dw_conv, dw_za, dw_mla), _shard_cols(dw_q_nat), _shard_cols(dw_kv_nat),
             dw_out.reshape(N_CHIPS, dw_out.shape[0] // N_CHIPS, D_MODEL)]
    theirs = _rs_core_swap(grads)
    names = ["w_in", "w_q_b", "w_kv_b", "w_out"]
    core = jnp.reshape(mc, (1,)).astype(jnp.int32)
    parts = [_add_half_bf16(g, b, core, "rs_add_" + nm) for g, b, nm in zip(grads, theirs, names)]
    dh, recv = _inproj_bwd(du_conv, du_za, du_mla, w_t, parts)
    recv = [lax.dynamic_update_slice(r, lax.dynamic_slice(p, (chip, 0, 0), (1,) + p.shape[1:]), (chip, 0, 0))
            for r, p in zip(recv, parts)]
    halves = [_sum_chips(p, "rs_sum_" + nm) for p, nm in zip(recv, names)]
    joined = _rs_core_join(halves)
    joined = [lax.dynamic_update_slice(j, hf[None], (mc, 0, 0)) for j, hf in zip(joined, halves)]
    g_big = [j.reshape(2 * j.shape[1], j.shape[2]) for j in joined]
    grad_x, st_in = _prenorm_bwd(x2, dh, dout, norm_g, scale)

    sgrad = jnp.zeros((8, D_MODEL), F32)
    sgrad = sgrad.at[0:2].set(st_in[0:2])
    sgrad = sgrad.at[2].set(st_out[0])
    sgrad = sgrad.at[3].set(st_in[2])
    sgrad = sgrad.at[4, :D_CONV].set(dconv_w[0]).at[4, D_CONV:].set(dconv_w[1])
    sgrad = sgrad.at[5, :D_CONV].set(dconv_w[2]).at[5, D_CONV:].set(sg_mla[0, :D_CONV])
    sgrad = sgrad.at[6, :HEAD_PAD].set(sg_mla[0, SG_KG:SG_KG + HEAD_PAD])
    sgrad = sgrad.at[7].set(st_out[1])
    sg_all, sg_sum = _gather8(sgrad, "gather_small_grads", True)
    loss = sg_sum[7, 0]
    g_ada_b = sg_sum[0:3].reshape(1, 3 * D_MODEL)
    g_norm_g = sg_sum[3:4]
    conv_sum = jnp.stack([sg_sum[4, :D_CONV], sg_sum[4, D_CONV:], sg_sum[5, :D_CONV]])
    g_conv_w = lax.dynamic_slice(conv_sum, (0, chip * cw_cols), (3, cw_cols))
    g_q_a_g = sg_sum[5:6, D_CONV + SG_QAG:D_CONV + SG_QAG + Q_LORA]
    g_kv_a_g = sg_sum[5:6, D_CONV + SG_KVAG:D_CONV + SG_KVAG + KV_LORA]
    g_q_g = sg_sum[5:6, D_CONV + SG_QG:D_CONV + SG_QG + QK_HEAD]
    g_k_g = sg_sum[6:7, :QK_HEAD]
    dmod_k = lax.dynamic_slice(sg_all[:, 0:3, :].reshape(N_DEV, 3 * D_MODEL), (0, chip * ada_cols), (N_DEV, ada_cols))

    g_ada_w, d_ada_w, nm_ada_w, nv_ada_w = _ada_w_update(sc_all, dmod_k, ada_w[0], m_ada_w[0], v_ada_w[0])
    upd = {}
    big = {"w_q_b": (w_q_b, m_w_q_b, v_w_q_b), "w_kv_b": (w_kv_b, m_w_kv_b, v_w_kv_b), "w_out": (w_out, m_w_out, v_w_out)}
    for nm, g in zip(names[1:], g_big[1:]):
        w_, m_, v_ = big[nm]
        upd[nm] = (g,) + tuple(_adamw(w_[0], g, m_[0], v_[0], "adamw_" + nm))
    d_t, nm_t, nv_t, g_t = _adamw(w_in_t, g_big[0], m_w_in_t, v_w_in_t, "adamw_w_in", echo_g=True)
    upd["w_in"] = tuple(jnp.transpose(a) for a in (g_t, d_t, nm_t, nv_t))
    small_w = {"ada_b": (ada_b, m_ada_b, v_ada_b, g_ada_b), "norm_g": (norm_g, m_norm_g, v_norm_g, g_norm_g),
               "conv_w": (conv_w[0], m_conv_w[0], v_conv_w[0], g_conv_w), "q_a_g": (q_a_g, m_q_a_g, v_q_a_g, g_q_a_g),
               "kv_a_g": (kv_a_g, m_kv_a_g, v_kv_a_g, g_kv_a_g), "q_g": (q_g, m_q_g, v_q_g, g_q_g),
               "k_g": (k_g, m_k_g, v_k_g, g_k_g)}
    for nm, (w_, m_, v_, g) in small_w.items():
        upd[nm] = (g,) + tuple(_adamw(w_, g, m_, v_, "adamw_" + nm))
    upd["ada_w"] = (g_ada_w, d_ada_w, nm_ada_w, nv_ada_w)

    order = ["ada_w", "ada_b", "norm_g", "w_in", "conv_w", "q_a_g", "w_q_b", "kv_a_g", "w_kv_b", "q_g", "k_g", "w_out"]
    lead1 = {"ada_w", "w_in", "conv_w", "w_q_b", "w_kv_b", "w_out"}

    def shaped(nm, a):
        return a[None] if nm in lead1 else a

    outs = [loss, grad_x[None]]
    for idx in range(4):
        outs += [shaped(nm, upd[nm][idx]) for nm in order]
    return tuple(outs)
```

```python
import functools
import math

import jax
import jax.numpy as jnp
from jax import lax
from jax.experimental import pallas as pl
from jax.experimental.pallas import tpu as pltpu

F32 = jnp.float32
BF16 = jnp.bfloat16
MESH = pl.DeviceIdType.MESH
SDS = jax.ShapeDtypeStruct
ANY = pl.BlockSpec(memory_space=pl.ANY)

D_MODEL = 2048
D_CONV = 1024
N_HEADS = 8
QK_NOPE = 128
QK_ROPE = 64
QK_HEAD = QK_NOPE + QK_ROPE
V_HEAD = 128
D_ATTN = N_HEADS * V_HEAD
Q_LORA = 512
KV_LORA = 256
ROPE_BASE = 10000.0
EPS = 1e-6
LOG2E = math.log2(math.e)
LN2 = math.log(2.0)
ADAM_LR, ADAM_B1, ADAM_B2, ADAM_EPS, ADAM_WD, ADAM_STEP = 0.001, 0.9, 0.999, 1e-08, 0.01, 10
N_CHIPS = 4
N_DEV = 8

LANES = 128
V7X_VMEM_BYTES = 64 * 1024 * 1024
MIB = 1024 * 1024

HEAD_PAD = 256
Q_PAD = N_HEADS * HEAD_PAD
U_ZA = 4 * D_CONV
U_CQ = U_ZA + D_ATTN
U_CKV = U_CQ + Q_LORA
U_KR = U_CKV + KV_LORA
KR_PAD = 256
U_COLS = U_KR + KR_PAD
MLA_COLS = Q_LORA + KV_LORA + KR_PAD

ATT_T = 512
INPROJ_TM, INPROJ_TN = 1024, 512
ROW_T = 512
OUT_T = 256
DH_TM, DH_TN = 512, 512
TN_TM, TN_TN, TN_TK = 1024, 1024, 2048
ATT_UNROLL = 8


def _cp(sem=None, vmem_mib=None, **kw):
    if sem is not None:
        kw["dimension_semantics"] = sem
    if vmem_mib is not None:
        kw["vmem_limit_bytes"] = min(vmem_mib * MIB, V7X_VMEM_BYTES - 4 * MIB)
    return pltpu.CompilerParams(**kw)


def _sigmoid(z):
    return 1.0 / (1.0 + jnp.exp(-z))


def _silu_grad(z, sg):
    return sg * (1.0 + z * (1.0 - sg))


def _nt(a, b):
    return lax.dot_general(a, b, (((1,), (1,)), ((), ())), preferred_element_type=F32)


def _tn(a, b):
    return lax.dot_general(a, b, (((0,), (0,)), ((), ())), preferred_element_type=F32)


def _nn(a, b):
    return jnp.dot(a, b, preferred_element_type=F32)


def _place():
    return lax.axis_index("x"), lax.axis_index("y"), lax.axis_index("c")


def _gather8(v, name, with_sum):
    rows, cols = v.shape

    def body(v_ref, out_ref, *rest):
        if with_sum:
            sum_ref, send_sems, recv_sems = rest
        else:
            send_sems, recv_sems = rest
        mx, my, mc = _place()
        me = 4 * mx + 2 * my + mc
        out_ref[me] = v_ref[...]
        peers = []
        for d in range(1, N_DEV):
            px = 1 - mx if d & 4 else mx
            py = 1 - my if d & 2 else my
            pc = 1 - mc if d & 1 else mc
            peers.append((px, py, pc))

        def copy(d, slot, to):
            return pltpu.make_async_remote_copy(
                src_ref=v_ref, dst_ref=out_ref.at[slot], send_sem=send_sems.at[d], recv_sem=recv_sems.at[d],
                device_id=to, device_id_type=MESH)

        sends = [copy(d, me, p) for d, p in enumerate(peers)]
        for cp in sends:
            cp.start()
        for d, (px, py, pc) in enumerate(peers):
            copy(d, 4 * px + 2 * py + pc, (px, py, pc)).wait_recv()
        for cp in sends:
            cp.wait_send()
        if with_sum:
            acc = out_ref[0]
            for b in range(1, N_DEV):
                acc = acc + out_ref[b]
            sum_ref[...] = acc

    out_shape = [SDS((N_DEV, rows, cols), F32)]
    if with_sum:
        out_shape.append(SDS((rows, cols), F32))
    vm = pl.BlockSpec(memory_space=pltpu.VMEM)
    return pl.pallas_call(
        body, name=name, out_shape=out_shape, in_specs=[vm], out_specs=[vm] * len(out_shape),
        scratch_shapes=[pltpu.SemaphoreType.DMA((N_DEV - 1,)), pltpu.SemaphoreType.DMA((N_DEV - 1,))],
    )(v)


def _chips_of(mx, my):
    chips = [(mx, 1 - my), (1 - mx, my), (1 - mx, 1 - my)]
    return chips, [2 * px + py for px, py in chips]


def _prenorm_gather(x, norm_g, scale, shift, shard):
    seq, dm = x.shape
    tm = min(INPROJ_TM, seq)
    ni = seq // tm
    half_rows = shard.shape[0] // 2

    def body(x_ref, g_ref, sc_ref, sh_ref, shard_ref, h_ref, got_ref, s1, r1, s2, r2):
        i = pl.program_id(0)
        mx, my, mc = _place()
        k = 2 * mx + my
        sib = (mx, my, 1 - mc)
        chips, kks = _chips_of(mx, my)

        def half(slot, c):
            return got_ref.at[slot, pl.ds(c * half_rows, half_rows)]

        def over_ici(d, slot):
            return pltpu.make_async_remote_copy(
                src_ref=shard_ref.at[pl.ds(mc * half_rows, half_rows)], dst_ref=half(slot, mc), send_sem=s1.at[d],
                recv_sem=r1.at[d], device_id=(chips[d][0], chips[d][1], mc), device_id_type=MESH)

        def to_sibling(d, c):
            return pltpu.make_async_remote_copy(
                src_ref=half(kks[d], c), dst_ref=half(kks[d], c), send_sem=s2.at[d], recv_sem=r2.at[d],
                device_id=sib, device_id_type=MESH)

        @pl.when(i == 0)
        def _():
            for d in range(3):
                over_ici(d, k).start()

        xv = x_ref[...]
        r = lax.rsqrt(jnp.mean(xv * xv, axis=-1, keepdims=True) + EPS)
        h_ref[...] = ((xv * r * g_ref[...]) * (1.0 + sc_ref[...]) + sh_ref[...]).astype(BF16)

        @pl.when(i == ni - 1)
        def _():
            for d in range(3):
                over_ici(d, kks[d]).wait_recv()
                to_sibling(d, mc).start()
            for d in range(3):
                to_sibling(d, 1 - mc).wait_recv()
            for d in range(3):
                over_ici(d, k).wait_send()
                to_sibling(d, mc).wait_send()

    vec = pl.BlockSpec((1, dm), lambda i: (0, 0))
    row = pl.BlockSpec((tm, dm), lambda i: (i, 0))
    return pl.pallas_call(
        body, name="prenorm_gather", grid=(ni,), in_specs=[row, vec, vec, vec, ANY], out_specs=[row, ANY],
        out_shape=[SDS((seq, dm), BF16), SDS((N_CHIPS,) + shard.shape, shard.dtype)],
        scratch_shapes=[pltpu.SemaphoreType.DMA((3,))] * 4,
        compiler_params=_cp(("arbitrary",), 48),
    )(x, norm_g, scale, shift, shard)


def _rs_core_swap(grads):
    na = len(grads)
    halves = [g.shape[1] // 2 for g in grads]

    def body(*refs):
        ins, outs = refs[:na], refs[na:2 * na]
        ssem, rsem = refs[2 * na:]
        mx, my, mc = _place()
        sib = (mx, my, 1 - mc)
        sends = []
        for a in range(na):
            cp = pltpu.make_async_remote_copy(
                src_ref=ins[a].at[:, pl.ds((1 - mc) * halves[a], halves[a])], dst_ref=outs[a],
                send_sem=ssem.at[a], recv_sem=rsem.at[a], device_id=sib, device_id_type=MESH)
            cp.start()
            sends.append(cp)
        for cp in sends:
            cp.wait_recv()
        for cp in sends:
            cp.wait_send()

    return pl.pallas_call(
        body, name="rs_core_swap", out_shape=[SDS((N_CHIPS, h) + g.shape[2:], g.dtype) for g, h in zip(grads, halves)],
        in_specs=[ANY] * na, out_specs=[ANY] * na,
        scratch_shapes=[pltpu.SemaphoreType.DMA((na,))] * 2,
    )(*grads)


def _chip_exchange_copies(ins, outs, ssem, rsem):
    mx, my, mc = _place()
    k = 2 * mx + my
    chips, kks = _chips_of(mx, my)
    sends, recvs = [], []
    for a in range(len(ins)):
        for d, (px, py) in enumerate(chips):
            def copy(dst_slot):
                return pltpu.make_async_remote_copy(
                    src_ref=ins[a].at[kks[d]], dst_ref=outs[a].at[dst_slot], send_sem=ssem.at[3 * a + d],
                    recv_sem=rsem.at[3 * a + d], device_id=(px, py, mc), device_id_type=MESH)
            sends.append(copy(k))
            recvs.append(copy(kks[d]))
    return sends, recvs


def _rs_core_join(halves):
    na = len(halves)

    def body(*refs):
        ins, outs = refs[:na], refs[na:2 * na]
        ssem, rsem = refs[2 * na:]
        mx, my, mc = _place()
        sib = (mx, my, 1 - mc)
        sends = []
        for a in range(na):
            cp = pltpu.make_async_remote_copy(
                src_ref=ins[a], dst_ref=outs[a].at[mc], send_sem=ssem.at[a], recv_sem=rsem.at[a],
                device_id=sib, device_id_type=MESH)
            cp.start()
            sends.append(cp)
        for a in range(na):
            pltpu.make_async_remote_copy(
                src_ref=ins[a], dst_ref=outs[a].at[1 - mc], send_sem=ssem.at[a], recv_sem=rsem.at[a],
                device_id=sib, device_id_type=MESH).wait_recv()
        for cp in sends:
            cp.wait_send()

    return pl.pallas_call(
        body, name="rs_core_join", out_shape=[SDS((2,) + h.shape, h.dtype) for h in halves],
        in_specs=[ANY] * na, out_specs=[ANY] * na,
        scratch_shapes=[pltpu.SemaphoreType.DMA((na,))] * 2,
    )(*halves)


def _row_tile(rows, limit, mult=16):
    if rows <= limit:
        return rows
    best = None
    for t in range(mult, limit + 1, mult):
        if rows % t == 0:
            best = t
    assert best is not None, rows
    return best


def _add_half_bf16(g, b, core, name):
    _, h, cols = b.shape
    tb = _row_tile(h, 512)
    nb = h // tb

    def body(core_ref, g_ref, b_ref, o_ref):
        o_ref[...] = (g_ref[...].astype(F32) + b_ref[...].astype(F32)).astype(BF16)

    spec = pl.BlockSpec((None, tb, cols), lambda kk, i, core_ref: (kk, i, 0))
    return pl.pallas_call(
        body, name=name,
        grid_spec=pltpu.PrefetchScalarGridSpec(
            num_scalar_prefetch=1, grid=(N_CHIPS, nb),
            in_specs=[pl.BlockSpec((None, tb, cols), lambda kk, i, core_ref: (kk, core_ref[0] * nb + i, 0)), spec],
            out_specs=spec),
        out_shape=SDS(b.shape, BF16), compiler_params=_cp(("arbitrary", "arbitrary")),
    )(core, g, b)


def _sum_chips(p, name):
    _, rows, cols = p.shape
    tb = _row_tile(rows, 256)

    def body(p_ref, o_ref):
        acc = p_ref[0].astype(F32)
        for j in range(1, N_CHIPS):
            acc = acc + p_ref[j].astype(F32)
        o_ref[...] = acc

    return pl.pallas_call(
        body, name=name, grid=(rows // tb,),
        in_specs=[pl.BlockSpec((N_CHIPS, tb, cols), lambda i: (0, i, 0))],
        out_specs=pl.BlockSpec((tb, cols), lambda i: (i, 0)), out_shape=SDS((rows, cols), F32),
        compiler_params=_cp(("arbitrary",)),
    )(p)


def _adamw_math(w, g, m, v):
    m2 = ADAM_B1 * m + (1.0 - ADAM_B1) * g
    v2 = ADAM_B2 * v + (1.0 - ADAM_B2) * (g * g)
    m_hat = m2 / (1.0 - ADAM_B1 ** ADAM_STEP)
    v_hat = v2 / (1.0 - ADAM_B2 ** ADAM_STEP)
    delta = -ADAM_LR * (m_hat / (jnp.sqrt(v_hat) + ADAM_EPS) + ADAM_WD * w)
    return delta, m2, v2


def _adamw(w, g, m, v, name, echo_g=False):
    rows, cols = w.shape
    tb = _row_tile(rows, 256, mult=8)
    nout = 4 if echo_g else 3

    def body(w_ref, g_ref, m_ref, v_ref, d_ref, m2_ref, v2_ref, *echo):
        gv = g_ref[...]
        d, m2, v2 = _adamw_math(w_ref[...], gv, m_ref[...], v_ref[...])
        d_ref[...] = d
        m2_ref[...] = m2
        v2_ref[...] = v2
        if echo_g:
            echo[0][...] = gv

    spec = pl.BlockSpec((tb, cols), lambda i: (i, 0))
    return pl.pallas_call(
        body, name=name, grid=(rows // tb,), in_specs=[spec] * 4, out_specs=[spec] * nout,
        out_shape=[SDS((rows, cols), F32)] * nout, compiler_params=_cp(("arbitrary",), 40),
    )(w, g, m, v)


def _ada_w_update(sc_all, dmod_k, w, m, v):
    rows, cols = w.shape
    tb = 256

    def body(s_ref, dm_ref, w_ref, m_ref, v_ref, g_ref, d_ref, m2_ref, v2_ref):
        g = _tn(s_ref[...].astype(BF16), dm_ref[...].astype(BF16))
        d, m2, v2 = _adamw_math(w_ref[...], g, m_ref[...], v_ref[...])
        g_ref[...] = g
        d_ref[...] = d
        m2_ref[...] = m2
        v2_ref[...] = v2

    spec = pl.BlockSpec((tb, cols), lambda i: (i, 0))
    return pl.pallas_call(
        body, name="ada_w_update", grid=(rows // tb,),
        in_specs=[pl.BlockSpec((N_DEV, tb), lambda i: (0, i)), pl.BlockSpec((N_DEV, cols), lambda i: (0, 0)), spec, spec, spec],
        out_specs=[spec] * 4, out_shape=[SDS((rows, cols), F32)] * 4, compiler_params=_cp(("arbitrary",), 40),
    )(sc_all, dmod_k, w, m, v)


def _ada_mod(c_all, w, b_k):
    rows, cols = w.shape
    tn = 512

    def body(c_ref, w_ref, b_ref, o_ref, s_ref):
        cv = c_ref[...]
        s = cv * _sigmoid(cv)
        s_ref[...] = s
        o_ref[...] = _nn(s.astype(BF16), w_ref[...].astype(BF16)) + b_ref[...]

    return pl.pallas_call(
        body, name="ada_mod", grid=(cols // tn,),
        in_specs=[pl.BlockSpec((N_DEV, rows), lambda j: (0, 0)), pl.BlockSpec((rows, tn), lambda j: (0, j)),
                  pl.BlockSpec((1, tn), lambda j: (0, j))],
        out_specs=[pl.BlockSpec((N_DEV, tn), lambda j: (0, j)), pl.BlockSpec((N_DEV, rows), lambda j: (0, 0))],
        out_shape=[SDS((N_DEV, cols), F32), SDS((N_DEV, rows), F32)], compiler_params=_cp(("arbitrary",)),
    )(c_all, w, b_k)


def _inproj(h, w_t, shards):
    seq, dm = h.shape
    ncols = w_t.shape[0]
    tm, tn = min(INPROJ_TM, seq), INPROJ_TN
    ni, nj = seq // tm, ncols // tn
    na = len(shards)

    def body(h_ref, w_ref, *rest):
        shard_refs, u_ref, got_refs = rest[:na], rest[na], rest[na + 1:2 * na + 1]
        ssem, rsem = rest[2 * na + 1:]
        i, j = pl.program_id(0), pl.program_id(1)
        mx, my, mc = _place()
        k = 2 * mx + my
        chips, kks = _chips_of(mx, my)

        def copy(a, d, slot):
            return pltpu.make_async_remote_copy(
                src_ref=shard_refs[a], dst_ref=got_refs[a].at[slot], send_sem=ssem.at[3 * a + d],
                recv_sem=rsem.at[3 * a + d], device_id=(chips[d][0], chips[d][1], mc), device_id_type=MESH)

        @pl.when((i == 0) & (j == 0))
        def _():
            for a in range(na):
                for d in range(3):
                    copy(a, d, k).start()

        u_ref[...] = _nt(h_ref[...], w_ref[...]).astype(BF16)

        @pl.when((i == ni - 1) & (j == nj - 1))
        def _():
            for a in range(na):
                for d in range(3):
                    copy(a, d, kks[d]).wait_recv()
            for a in range(na):
                for d in range(3):
                    copy(a, d, k).wait_send()

    outs = pl.pallas_call(
        body, name="inproj", grid=(ni, nj),
        in_specs=[pl.BlockSpec((tm, dm), lambda i, j: (i, 0)), pl.BlockSpec((tn, dm), lambda i, j: (j, 0))] + [ANY] * na,
        out_specs=[pl.BlockSpec((tm, tn), lambda i, j: (i, j))] + [ANY] * na,
        out_shape=[SDS((seq, ncols), BF16)] + [SDS((N_CHIPS,) + s.shape, s.dtype) for s in shards],
        scratch_shapes=[pltpu.SemaphoreType.DMA((3 * na,))] * 2,
        compiler_params=_cp(("arbitrary", "arbitrary"), 48),
    )(h, w_t, *shards)
    return outs[0], outs[1:]


HALO = 16
CHUNK_ROWS, CHUNK_LANES = 16, 512


def _conv_fwd(u, conv_w):
    seq = u.shape[0]
    ts = min(ROW_T, seq)
    hb = ts // HALO

    def body(xc_ref, bc_ref, cc_ref, zc_ref, xp_ref, cp_ref, w_ref, y_ref, ext_ref):
        i = pl.program_id(0)
        up = cp_ref[...].astype(F32) * xp_ref[...].astype(F32)
        ext_ref[0:HALO, :] = jnp.where(i > 0, up, 0.0)
        ext_ref[HALO:HALO + ts, :] = cc_ref[...].astype(F32) * xc_ref[...].astype(F32)
        for r0 in range(0, ts, CHUNK_ROWS):
            rows = slice(r0, r0 + CHUNK_ROWS)
            for c0 in range(0, D_CONV, CHUNK_LANES):
                cols = slice(c0, c0 + CHUNK_LANES)
                uc = ext_ref[HALO + r0:HALO + r0 + CHUNK_ROWS, cols]
                u1 = ext_ref[HALO - 1 + r0:HALO - 1 + r0 + CHUNK_ROWS, cols]
                u2 = ext_ref[HALO - 2 + r0:HALO - 2 + r0 + CHUNK_ROWS, cols]
                conv = w_ref[0:1, cols] * u2 + w_ref[1:2, cols] * u1 + w_ref[2:3, cols] * uc
                z = zc_ref[rows, cols].astype(F32)
                y_ref[rows, cols] = ((bc_ref[rows, cols].astype(F32) * conv) * (z * _sigmoid(z))).astype(BF16)

    def col(cb):
        return pl.BlockSpec((ts, D_CONV), lambda i: (i, cb))

    def prev(cb):
        return pl.BlockSpec((HALO, D_CONV), lambda i: (jnp.maximum(i * hb - 1, 0), cb))

    return pl.pallas_call(
        body, name="conv_fwd", grid=(seq // ts,),
        in_specs=[col(0), col(1), col(2), col(3), prev(0), prev(2), pl.BlockSpec((3, D_CONV), lambda i: (0, 0))],
        out_specs=pl.BlockSpec((ts, D_CONV), lambda i: (i, 0)), out_shape=SDS((seq, D_CONV), BF16),
        scratch_shapes=[pltpu.VMEM((ts + HALO, D_CONV), F32)],
        compiler_params=_cp(("arbitrary",), 40),
    )(u, u, u, u, u, u, conv_w)


def _rope_tables(pos_ref, freq_ref):
    ang = pos_ref[...].astype(F32) * freq_ref[...]
    lane = lax.broadcasted_iota(jnp.int32, ang.shape, 1)
    cs, sn = jnp.cos(ang), jnp.sin(ang)
    half = QK_ROPE // 2
    cos_t = jnp.where(lane < QK_ROPE, cs, 0.0)
    sin_lo = jnp.where(lane < half, sn, 0.0)
    sin_hi = jnp.where((lane >= half) & (lane < QK_ROPE), sn, 0.0)
    return cos_t, sin_lo, sin_hi


def _rope(blk, tables):
    cos_t, sin_lo, sin_hi = tables
    half = QK_ROPE // 2
    return blk * cos_t - pltpu.roll(blk, LANES - half, 1) * sin_lo + pltpu.roll(blk, half, 1) * sin_hi


def _rope_bwd(g, tables):
    cos_t, sin_lo, sin_hi = tables
    half = QK_ROPE // 2
    return g * cos_t + pltpu.roll(g, LANES - half, 1) * sin_lo - pltpu.roll(g, half, 1) * sin_hi


def _rms(v, n):
    r = lax.rsqrt(jnp.sum(v * v, axis=-1, keepdims=True) * (1.0 / n) + EPS)
    return v * r, r


def _mla_prep(u, pos, freq, q_a_g, wq, kv_a_g, wkn, wv, q_g, k_g):
    seq = u.shape[0]
    ts = min(ROW_T, seq)
    qscale = LOG2E / math.sqrt(QK_HEAD)

    def body(cq_ref, ckv_ref, kr_ref, pos_ref, freq_ref, qag_ref, wq_ref, kvag_ref, wkn_ref, wv_ref, qg_ref, kg_ref,
             q_ref, k_ref, v_ref):
        tables = _rope_tables(pos_ref, freq_ref)
        cqn, _ = _rms(cq_ref[...].astype(F32), Q_LORA)
        qp = _nn((cqn * qag_ref[...]).astype(BF16), wq_ref[...])
        qg = qg_ref[...]
        for h in range(N_HEADS):
            lo = h * HEAD_PAD
            qn, _ = _rms(qp[:, lo:lo + HEAD_PAD], QK_HEAD)
            qn = qn * qg
            q_ref[:, lo:lo + LANES] = (qn[:, :LANES] * qscale).astype(BF16)
            q_ref[:, lo + LANES:lo + HEAD_PAD] = (_rope(qn[:, LANES:], tables) * qscale).astype(BF16)
        ckvn, _ = _rms(ckv_ref[...].astype(F32), KV_LORA)
        ckvb = (ckvn * kvag_ref[...]).astype(BF16)
        kn = _nn(ckvb, wkn_ref[...])
        v_ref[...] = _nn(ckvb, wv_ref[...]).astype(BF16)
        kr = kr_ref[:, 0:LANES].astype(F32)
        ssr = jnp.sum(kr * kr, axis=-1, keepdims=True)
        kg = kg_ref[...]
        for h in range(N_HEADS):
            knh = kn[:, h * QK_NOPE:(h + 1) * QK_NOPE]
            r = lax.rsqrt((jnp.sum(knh * knh, axis=-1, keepdims=True) + ssr) * (1.0 / QK_HEAD) + EPS)
            lo = h * HEAD_PAD
            k_ref[:, lo:lo + LANES] = (knh * r * kg[:, :LANES]).astype(BF16)
            k_ref[:, lo + LANES:lo + HEAD_PAD] = _rope(kr * r * kg[:, LANES:], tables).astype(BF16)

    def full(a):
        return pl.BlockSpec(a.shape, lambda i: (0,) * a.ndim)

    return pl.pallas_call(
        body, name="mla_prep", grid=(seq // ts,),
        in_specs=[pl.BlockSpec((ts, Q_LORA), lambda i: (i, U_CQ // Q_LORA)),
                  pl.BlockSpec((ts, KV_LORA), lambda i: (i, U_CKV // KV_LORA)),
                  pl.BlockSpec((ts, KR_PAD), lambda i: (i, U_KR // KR_PAD)),
                  pl.BlockSpec((ts, 1), lambda i: (i, 0)), full(freq), full(q_a_g), full(wq), full(kv_a_g), full(wkn),
                  full(wv), full(q_g), full(k_g)],
        out_specs=[pl.BlockSpec((ts, Q_PAD), lambda i: (i, 0)), pl.BlockSpec((ts, Q_PAD), lambda i: (i, 0)),
                   pl.BlockSpec((ts, D_ATTN), lambda i: (i, 0))],
        out_shape=[SDS((seq, Q_PAD), BF16), SDS((seq, Q_PAD), BF16), SDS((seq, D_ATTN), BF16)],
        compiler_params=_cp(("arbitrary",), 48),
    )(u, u, u, pos, freq, q_a_g, wq, kv_a_g, wkn, wv, q_g, k_g)


def _flash_fwd(q, k, v, u):
    seq = q.shape[0]
    t = min(ATT_T, seq)
    n = seq // t
    n_pairs = n * (n + 1) // 2
    za_blk = U_ZA // V_HEAD

    def body(q_ref, k_ref, v_ref, z_ref, o_ref, y_ref, lse_ref, s_a, s_b, top_a, top_b, m_all, l_all, acc_all):
        ones = jnp.ones((16, t), BF16)
        bufs = ((s_a, top_a), (s_b, top_b))

        def rows(i):
            return pl.ds(pl.multiple_of(i * t, t), t)

        def scores(i, j, buf):
            s_ref, top_ref = buf
            s = _nt(k_ref[rows(j), :], q_ref[rows(i), :])
            ahead = lax.broadcasted_iota(jnp.int32, (t, t), 0) - lax.broadcasted_iota(jnp.int32, (t, t), 1)
            s = jnp.where(ahead <= (i - j) * t, s, -jnp.inf)
            s_ref[...] = s
            top_ref[...] = jnp.max(s, axis=0, keepdims=True)

        def absorb(i, j, buf):
            s_ref, top_ref = buf
            first = j == 0
            m = jnp.where(first, -jnp.inf, m_all[i])
            l = jnp.where(first, 0.0, l_all[i])
            acc = jnp.where(first, 0.0, acc_all[i])
            m_new = jnp.maximum(m, top_ref[...])
            alpha = jnp.exp2(m - m_new)
            p = jnp.exp2((s_ref[...] - m_new).astype(BF16))
            m_all[i] = m_new
            l_all[i] = alpha * l + _nn(ones, p)[0:1, :]
            acc_all[i] = alpha * acc + _tn(v_ref[rows(j), :], p)

        def trip(width):
            def walk(_, pair):
                i, j = pair
                for w in range(width):
                    done = j == i
                    ni, nj = jnp.where(done, i + 1, i), jnp.where(done, 0, j + 1)
                    scores(jnp.minimum(ni, n - 1), nj, bufs[(w + 1) % 2])
                    absorb(i, j, bufs[w % 2])
                    i, j = ni, nj
                return i, j
            return walk

        scores(0, 0, bufs[0])
        pair = lax.fori_loop(0, n_pairs // ATT_UNROLL, trip(ATT_UNROLL), (jnp.int32(0), jnp.int32(0)))
        if n_pairs % ATT_UNROLL:
            trip(n_pairs % ATT_UNROLL)(0, pair)

        def finish(i, carry):
            l = l_all[i]
            o = (acc_all[i] * (1.0 / l)).T
            lse_ref[:, rows(i)] = m_all[i] + jnp.log2(l)
            o_ref[rows(i), :] = o.astype(BF16)
            z = z_ref[rows(i), :].astype(F32)
            y_ref[rows(i), :] = (o * (z * _sigmoid(z))).astype(BF16)
            return carry

        lax.fori_loop(0, n, finish, 0)

    def col(width, cb):
        return pl.BlockSpec((seq, width), lambda h: (0, cb + h))

    return pl.pallas_call(
        body, name="flash_fwd", grid=(N_HEADS,),
        in_specs=[col(HEAD_PAD, 0), col(HEAD_PAD, 0), col(V_HEAD, 0), col(V_HEAD, za_blk)],
        out_specs=[col(V_HEAD, 0), col(V_HEAD, 0), pl.BlockSpec((None, 1, seq), lambda h: (h, 0, 0))],
        out_shape=[SDS((seq, D_ATTN), BF16), SDS((seq, D_ATTN), BF16), SDS((N_HEADS, 1, seq), F32)],
        scratch_shapes=[pltpu.VMEM((t, t), F32)] * 2 + [pltpu.VMEM((1, t), F32)] * 2
                       + [pltpu.VMEM((n, 1, t), F32)] * 2 + [pltpu.VMEM((n, V_HEAD, t), F32)],
        compiler_params=_cp(("arbitrary",), 52),
    )(q, k, v, u)


def _outproj_loss(y_conv, y_attn, x, target, gate, w_out):
    seq, dm = x.shape
    ts = min(OUT_T, seq)
    n = seq // ts
    dmix = w_out.shape[0]

    def body(yc_ref, ya_ref, x_ref, t_ref, gate_ref, wo_hbm, dout_ref, dy_ref, dyc_ref, stats_ref, wo_ref, sem, acc_ref):
        i = pl.program_id(0)

        @pl.when(i == 0)
        def _():
            cp = pltpu.make_async_copy(wo_hbm, wo_ref, sem)
            cp.start()
            cp.wait()
            acc_ref[...] = jnp.zeros_like(acc_ref)

        y = _nn(yc_ref[...], wo_ref[0:D_CONV, :]) + _nn(ya_ref[...], wo_ref[D_CONV:dmix, :])
        gate_v = gate_ref[...]
        diff = (x_ref[...] + gate_v * y) - t_ref[...]
        dout = diff * (1.0 / dm)
        dout_ref[...] = dout
        acc_ref[0:8, :] += jnp.sum((dout * y).reshape(ts // 8, 8, dm), axis=0)
        acc_ref[8:16, :] += jnp.sum((diff * diff).reshape(ts // 8, 8, dm), axis=0)
        dy = (dout * gate_v).astype(BF16)
        dy_ref[...] = dy
        dyc_ref[...] = _nt(dy, wo_ref[...]).astype(BF16)

        @pl.when(i == n - 1)
        def _():
            stats_ref[...] = jnp.zeros_like(stats_ref)
            stats_ref[0:1, :] = jnp.sum(acc_ref[0:8, :], axis=0, keepdims=True)
            loss = jnp.sum(acc_ref[8:16, :]) * (0.5 / dm)
            stats_ref[1:2, :] = jnp.full((1, dm), loss, F32)

    row = pl.BlockSpec((ts, dm), lambda i: (i, 0))
    half = pl.BlockSpec((ts, D_CONV), lambda i: (i, 0))
    return pl.pallas_call(
        body, name="outproj_loss", grid=(n,),
        in_specs=[half, half, row, row, pl.BlockSpec((1, dm), lambda i: (0, 0)), ANY],
        out_specs=[row, row, pl.BlockSpec((ts, dmix), lambda i: (i, 0)), pl.BlockSpec((8, dm), lambda i: (0, 0))],
        out_shape=[SDS((seq, dm), F32), SDS((seq, dm), BF16), SDS((seq, dmix), BF16), SDS((8, dm), F32)],
        scratch_shapes=[pltpu.VMEM(w_out.shape, BF16), pltpu.SemaphoreType.DMA(()), pltpu.VMEM((16, dm), F32)],
        compiler_params=_cp(("arbitrary",), 52),
    )(y_conv, y_attn, x, target, gate, w_out)


def _matmul_tn(a, b, name):
    seq, m = a.shape
    n = b.shape[1]
    tm, tn, tk = min(TN_TM, m), min(TN_TN, n), min(TN_TK, seq)
    nk = seq // tk

    def body(a_ref, b_ref, o_ref, acc_ref):
        kk = pl.program_id(2)

        @pl.when(kk == 0)
        def _():
            acc_ref[...] = jnp.zeros_like(acc_ref)

        acc_ref[...] += _tn(a_ref[...], b_ref[...])

        @pl.when(kk == nk - 1)
        def _():
            o_ref[...] = acc_ref[...].astype(BF16)

    return pl.pallas_call(
        body, name=name, grid=(m // tm, n // tn, nk),
        in_specs=[pl.BlockSpec((tk, tm), lambda i, j, kk: (kk, i)), pl.BlockSpec((tk, tn), lambda i, j, kk: (kk, j))],
        out_specs=pl.BlockSpec((tm, tn), lambda i, j, kk: (i, j)), out_shape=SDS((m, n), BF16),
        scratch_shapes=[pltpu.VMEM((tm, tn), F32)],
        compiler_params=_cp(("arbitrary", "arbitrary", "arbitrary"), 40),
    )(a, b)


def _attn_gate_bwd(dycat, o, u):
    seq = o.shape[0]
    ts = min(ROW_T, seq)

    def body(dy_ref, o_ref, z_ref, dot_ref, dz_ref, dl_ref):
        dy = dy_ref[...].astype(F32)
        ov = o_ref[...].astype(F32)
        z = z_ref[...].astype(F32)
        sg = _sigmoid(z)
        do = dy * (z * sg)
        dz_ref[...] = (dy * ov * _silu_grad(z, sg)).astype(BF16)
        prod = do * ov
        ones = jnp.ones((8, V_HEAD), F32)
        for h in range(N_HEADS):
            cols = slice(h * V_HEAD, (h + 1) * V_HEAD)
            dot_ref[h] = do[:, cols].T.astype(BF16)
            rows = lax.dot_general(ones, prod[:, cols], (((1,), (1,)), ((), ())), precision=lax.Precision.HIGHEST,
                                   preferred_element_type=F32)
            dl_ref[h] = rows[0:1, :]

    blk = pl.BlockSpec((ts, D_ATTN), lambda i: (i, 0))
    return pl.pallas_call(
        body, name="attn_gate_bwd", grid=(seq // ts,),
        in_specs=[pl.BlockSpec((ts, D_ATTN), lambda i: (i, 1)), blk, pl.BlockSpec((ts, D_ATTN), lambda i: (i, U_ZA // D_ATTN))],
        out_specs=[pl.BlockSpec((N_HEADS, V_HEAD, ts), lambda i: (0, 0, i)), blk,
                   pl.BlockSpec((N_HEADS, 1, ts), lambda i: (0, 0, i))],
        out_shape=[SDS((N_HEADS, V_HEAD, seq), BF16), SDS((seq, D_ATTN), BF16), SDS((N_HEADS, 1, seq), F32)],
        compiler_params=_cp(("arbitrary",), 40),
    )(dycat, o, u)


def _flash_bwd(q, k, v, do_t, lse, delta):
    seq = q.shape[0]
    t = min(ATT_T, seq)
    n = seq // t
    n_pairs = n * (n + 1) // 2

    def body(k_ref, v_ref, q_ref, dot_ref, lse_ref, dl_ref, dq_ref, dk_ref, dv_ref, s_a, s_b, dp_a, dp_b, dq_acc, dk_acc,
             dvt_acc):
        bufs = ((s_a, dp_a), (s_b, dp_b))

        def rows(i):
            return pl.ds(pl.multiple_of(i * t, t), t)

        def products(j, i, buf):
            s_ref, dp_ref = buf
            s = _nt(k_ref[rows(j), :], q_ref[rows(i), :])
            ahead = lax.broadcasted_iota(jnp.int32, (t, t), 0) - lax.broadcasted_iota(jnp.int32, (t, t), 1)
            s_ref[...] = jnp.where(ahead <= (i - j) * t, s, -jnp.inf)
            dp_ref[...] = _nn(v_ref[rows(j), :], dot_ref[:, rows(i)])

        def absorb(j, i, buf):
            s_ref, dp_ref = buf
            first = i == j
            p = jnp.exp2((s_ref[...] - lse_ref[:, rows(i)]).astype(BF16))
            dvt = jnp.where(first, 0.0, dvt_acc[...]) + _nt(dot_ref[:, rows(i)], p)
            ds = p * (dp_ref[...] - dl_ref[:, rows(i)]).astype(BF16)
            dk = jnp.where(first, 0.0, dk_acc[...]) + _nn(ds, q_ref[rows(i), :])
            dq_acc[rows(i), :] += _tn(ds, k_ref[rows(j), :])
            dvt_acc[...] = dvt
            dk_acc[...] = dk
            dk_ref[rows(j), :] = dk.astype(BF16)
            dv_ref[rows(j), :] = dvt.T.astype(BF16)

        def trip(width):
            def walk(_, pair):
                j, i = pair
                for w in range(width):
                    done = i == n - 1
                    nj = jnp.where(done, j + 1, j)
                    ni = jnp.where(done, j + 1, i + 1)
                    absorb(j, i, bufs[w % 2])
                    products(jnp.minimum(nj, n - 1), jnp.minimum(ni, n - 1), bufs[(w + 1) % 2])
                    j, i = nj, ni
                return j, i
            return walk

        dq_acc[...] = jnp.zeros_like(dq_acc)
        products(0, 0, bufs[0])
        pair = lax.fori_loop(0, n_pairs // ATT_UNROLL, trip(ATT_UNROLL), (jnp.int32(0), jnp.int32(0)))
        if n_pairs % ATT_UNROLL:
            trip(n_pairs % ATT_UNROLL)(0, pair)

        def finish(i, carry):
            dq_ref[rows(i), :] = dq_acc[rows(i), :].astype(BF16)
            return carry

        lax.fori_loop(0, n, finish, 0)

    def col(width):
        return pl.BlockSpec((seq, width), lambda h: (0, h))

    row = pl.BlockSpec((None, 1, seq), lambda h: (h, 0, 0))
    return pl.pallas_call(
        body, name="flash_bwd", grid=(N_HEADS,),
        in_specs=[col(HEAD_PAD), col(V_HEAD), col(HEAD_PAD), pl.BlockSpec((None, V_HEAD, seq), lambda h: (h, 0, 0)), row, row],
        out_specs=[col(HEAD_PAD), col(HEAD_PAD), col(V_HEAD)],
        out_shape=[SDS((seq, Q_PAD), BF16), SDS((seq, Q_PAD), BF16), SDS((seq, D_ATTN), BF16)],
        scratch_shapes=[pltpu.VMEM((t, t), F32)] * 4
                       + [pltpu.VMEM((seq, HEAD_PAD), F32), pltpu.VMEM((t, HEAD_PAD), F32), pltpu.VMEM((V_HEAD, t), F32)],
        compiler_params=_cp(("arbitrary",), 60),
    )(k, v, q, do_t, lse, delta)


SG_QAG, SG_KVAG, SG_QG, SG_KG, SG_COLS = 0, Q_LORA, Q_LORA + KV_LORA, Q_LORA + KV_LORA + HEAD_PAD, D_MODEL


def _mla_bwd(dq, dk, dv, u, pos, freq, q_a_g, wq, kv_a_g, wkn, wv, q_g, k_g):
    seq = u.shape[0]
    ts = min(ROW_T, seq)
    n = seq // ts
    qscale = 1.0 / math.sqrt(QK_HEAD)

    def body(dq_ref, dk_ref, dv_ref, cq_ref, ckv_ref, kr_ref, pos_ref, freq_ref, qag_ref, wq_ref, kvag_ref, wkn_ref,
             wv_ref, qg_ref, kg_ref, du_ref, dwq_ref, dwkn_ref, dwv_ref, sg_ref, dqp_ref, dkn_ref):
        i = pl.program_id(0)

        @pl.when(i == 0)
        def _():
            dwq_ref[...] = jnp.zeros_like(dwq_ref)
            dwkn_ref[...] = jnp.zeros_like(dwkn_ref)
            dwv_ref[...] = jnp.zeros_like(dwv_ref)
            sg_ref[...] = jnp.zeros_like(sg_ref)

        tables = _rope_tables(pos_ref, freq_ref)

        cq = cq_ref[...].astype(F32)
        cqn, rq = _rms(cq, Q_LORA)
        qag = qag_ref[...]
        cqb = (cqn * qag).astype(BF16)
        qp = _nn(cqb, wq_ref[...])
        qg = qg_ref[...]
        dqg = jnp.zeros((1, HEAD_PAD), F32)
        for h in range(N_HEADS):
            lo = h * HEAD_PAD
            xn, r = _rms(qp[:, lo:lo + HEAD_PAD], QK_HEAD)
            g = jnp.concatenate([dq_ref[:, lo:lo + LANES].astype(F32),
                                 _rope_bwd(dq_ref[:, lo + LANES:lo + HEAD_PAD].astype(F32), tables)], axis=-1) * qscale
            dqg = dqg + jnp.sum(g * xn, axis=0, keepdims=True)
            gy = g * qg
            mean = jnp.sum(gy * xn, axis=-1, keepdims=True) * (1.0 / QK_HEAD)
            dqp_ref[:, lo:lo + HEAD_PAD] = (r * (gy - xn * mean)).astype(BF16)
        dqp = dqp_ref[...]
        dwq_ref[...] += _tn(cqb, dqp)
        dcqn = _nt(dqp, wq_ref[...])
        sg_ref[0:1, SG_QAG:SG_QAG + Q_LORA] += jnp.sum(dcqn * cqn, axis=0, keepdims=True)
        sg_ref[0:1, SG_QG:SG_QG + HEAD_PAD] += dqg
        gy = dcqn * qag
        mean = jnp.sum(gy * cqn, axis=-1, keepdims=True) * (1.0 / Q_LORA)
        du_ref[:, 0:Q_LORA] = (rq * (gy - cqn * mean)).astype(BF16)

        ckv = ckv_ref[...].astype(F32)
        ckvn, rkv = _rms(ckv, KV_LORA)
        kvag = kvag_ref[...]
        ckvb = (ckvn * kvag).astype(BF16)
        kn = _nn(ckvb, wkn_ref[...])
        kr = kr_ref[:, 0:LANES].astype(F32)
        ssr = jnp.sum(kr * kr, axis=-1, keepdims=True)
        kg = kg_ref[...]
        kg_n, kg_r = kg[:, :LANES] * LN2, kg[:, LANES:] * LN2
        dkg_n = jnp.zeros((1, LANES), F32)
        dkg_r = jnp.zeros((1, LANES), F32)
        dkr = jnp.zeros((ts, LANES), F32)
        for h in range(N_HEADS):
            knh = kn[:, h * QK_NOPE:(h + 1) * QK_NOPE]
            r = lax.rsqrt((jnp.sum(knh * knh, axis=-1, keepdims=True) + ssr) * (1.0 / QK_HEAD) + EPS)
            xn_n, xn_r = knh * r, kr * r
            lo = h * HEAD_PAD
            g_n = dk_ref[:, lo:lo + LANES].astype(F32)
            g_r = _rope_bwd(dk_ref[:, lo + LANES:lo + HEAD_PAD].astype(F32), tables)
            dkg_n = dkg_n + jnp.sum(g_n * xn_n, axis=0, keepdims=True)
            dkg_r = dkg_r + jnp.sum(g_r * xn_r, axis=0, keepdims=True)
            gy_n, gy_r = g_n * kg_n, g_r * kg_r
            mean = (jnp.sum(gy_n * xn_n, axis=-1, keepdims=True) + jnp.sum(gy_r * xn_r, axis=-1, keepdims=True)) * (1.0 / QK_HEAD)
            dkn_ref[:, h * QK_NOPE:(h + 1) * QK_NOPE] = (r * (gy_n - xn_n * mean)).astype(BF16)
            dkr = dkr + r * (gy_r - xn_r * mean)
        dkn = dkn_ref[...]
        dvv = dv_ref[...]
        dwkn_ref[...] += _tn(ckvb, dkn)
        dwv_ref[...] += _tn(ckvb, dvv)
        dckvn = _nt(dkn, wkn_ref[...]) + _nt(dvv, wv_ref[...])
        sg_ref[0:1, SG_KVAG:SG_KVAG + KV_LORA] += jnp.sum(dckvn * ckvn, axis=0, keepdims=True)
        sg_ref[0:1, SG_KG:SG_KG + LANES] += dkg_n * LN2
        sg_ref[0:1, SG_KG + LANES:SG_KG + HEAD_PAD] += dkg_r * LN2
        gy = dckvn * kvag
        mean = jnp.sum(gy * ckvn, axis=-1, keepdims=True) * (1.0 / KV_LORA)
        du_ref[:, Q_LORA:Q_LORA + KV_LORA] = (rkv * (gy - ckvn * mean)).astype(BF16)
        du_ref[:, Q_LORA + KV_LORA:Q_LORA + KV_LORA + LANES] = dkr.astype(BF16)
        du_ref[:, Q_LORA + KV_LORA + LANES:MLA_COLS] = jnp.zeros((ts, LANES), BF16)

    def full(a):
        return pl.BlockSpec(a.shape, lambda i: (0,) * a.ndim)

    wide = pl.BlockSpec((ts, Q_PAD), lambda i: (i, 0))
    return pl.pallas_call(
        body, name="mla_bwd", grid=(n,),
        in_specs=[wide, wide, pl.BlockSpec((ts, D_ATTN), lambda i: (i, 0)),
                  pl.BlockSpec((ts, Q_LORA), lambda i: (i, U_CQ // Q_LORA)),
                  pl.BlockSpec((ts, KV_LORA), lambda i: (i, U_CKV // KV_LORA)),
                  pl.BlockSpec((ts, KR_PAD), lambda i: (i, U_KR // KR_PAD)),
                  pl.BlockSpec((ts, 1), lambda i: (i, 0)), full(freq), full(q_a_g), full(wq), full(kv_a_g), full(wkn),
                  full(wv), full(q_g), full(k_g)],
        out_specs=[pl.BlockSpec((ts, MLA_COLS), lambda i: (i, 0)), pl.BlockSpec((Q_LORA, Q_PAD), lambda i: (0, 0)),
                   pl.BlockSpec((KV_LORA, D_ATTN), lambda i: (0, 0)), pl.BlockSpec((KV_LORA, D_ATTN), lambda i: (0, 0)),
                   pl.BlockSpec((8, SG_COLS), lambda i: (0, 0))],
        out_shape=[SDS((seq, MLA_COLS), BF16), SDS((Q_LORA, Q_PAD), F32), SDS((KV_LORA, D_ATTN), F32),
                   SDS((KV_LORA, D_ATTN), F32), SDS((8, SG_COLS), F32)],
        scratch_shapes=[pltpu.VMEM((ts, Q_PAD), BF16), pltpu.VMEM((ts, D_ATTN), BF16)],
        compiler_params=_cp(("arbitrary",), 56),
    )(dq, dk, dv, u, u, u, pos, freq, q_a_g, wq, kv_a_g, wkn, wv, q_g, k_g)


def _conv_bwd(dycat, u, conv_w):
    seq = u.shape[0]
    ts = min(ROW_T, seq)
    n = seq // ts
    hb = ts // HALO

    def body(dy_ref, xc_ref, bc_ref, cc_ref, zc_ref, xp_ref, cp_ref, dyn_ref, bn_ref, zn_ref, w_ref,
             du_ref, dw_ref, ext_ref, dext_ref, acc_ref):
        i = pl.program_id(0)

        @pl.when(i == 0)
        def _():
            dw_ref[...] = jnp.zeros_like(dw_ref)

        up = cp_ref[...].astype(F32) * xp_ref[...].astype(F32)
        ext_ref[0:HALO, :] = jnp.where(i > 0, up, 0.0)
        ext_ref[HALO:HALO + ts, :] = cc_ref[...].astype(F32) * xc_ref[...].astype(F32)
        zn = zn_ref[...].astype(F32)
        dnext = dyn_ref[...].astype(F32) * (zn * _sigmoid(zn)) * bn_ref[...].astype(F32)
        dext_ref[ts:ts + HALO, :] = jnp.where(i < n - 1, dnext, 0.0)
        acc_ref[...] = jnp.zeros_like(acc_ref)

        for r0 in range(0, ts, CHUNK_ROWS):
            rows = slice(r0, r0 + CHUNK_ROWS)
            for c0 in range(0, D_CONV, CHUNK_LANES):
                cols = slice(c0, c0 + CHUNK_LANES)
                uc = ext_ref[HALO + r0:HALO + r0 + CHUNK_ROWS, cols]
                u1 = ext_ref[HALO - 1 + r0:HALO - 1 + r0 + CHUNK_ROWS, cols]
                u2 = ext_ref[HALO - 2 + r0:HALO - 2 + r0 + CHUNK_ROWS, cols]
                conv = w_ref[0:1, cols] * u2 + w_ref[1:2, cols] * u1 + w_ref[2:3, cols] * uc
                z = zc_ref[rows, cols].astype(F32)
                sg = _sigmoid(z)
                sz = z * sg
                b = bc_ref[rows, cols].astype(F32)
                dy = dy_ref[rows, cols].astype(F32)
                du_ref[rows, 3 * D_CONV + c0:3 * D_CONV + c0 + CHUNK_LANES] = (dy * (b * conv) * _silu_grad(z, sg)).astype(BF16)
                du_ref[rows, D_CONV + c0:D_CONV + c0 + CHUNK_LANES] = (dy * sz * conv).astype(BF16)
                dconv = dy * sz * b
                dext_ref[rows, cols] = dconv
                acc_ref[0:CHUNK_ROWS, cols] += dconv * u2
                acc_ref[CHUNK_ROWS:2 * CHUNK_ROWS, cols] += dconv * u1
                acc_ref[2 * CHUNK_ROWS:3 * CHUNK_ROWS, cols] += dconv * uc
        for r0 in range(0, ts, CHUNK_ROWS):
            rows = slice(r0, r0 + CHUNK_ROWS)
            for c0 in range(0, D_CONV, CHUNK_LANES):
                cols = slice(c0, c0 + CHUNK_LANES)
                du = (w_ref[2:3, cols] * dext_ref[rows, cols] + w_ref[1:2, cols] * dext_ref[r0 + 1:r0 + 1 + CHUNK_ROWS, cols]
                      + w_ref[0:1, cols] * dext_ref[r0 + 2:r0 + 2 + CHUNK_ROWS, cols])
                du_ref[rows, 2 * D_CONV + c0:2 * D_CONV + c0 + CHUNK_LANES] = (du * xc_ref[rows, cols].astype(F32)).astype(BF16)
                du_ref[rows, c0:c0 + CHUNK_LANES] = (du * cc_ref[rows, cols].astype(F32)).astype(BF16)
        for k in range(3):
            dw_ref[k:k + 1, :] += jnp.sum(acc_ref[k * CHUNK_ROWS:(k + 1) * CHUNK_ROWS, :], axis=0, keepdims=True)

    def col(cb):
        return pl.BlockSpec((ts, D_CONV), lambda i: (i, cb))

    def prev(cb):
        return pl.BlockSpec((HALO, D_CONV), lambda i: (jnp.maximum(i * hb - 1, 0), cb))

    def nxt(cb):
        return pl.BlockSpec((HALO, D_CONV), lambda i: (jnp.minimum((i + 1) * hb, n * hb - 1), cb))

    return pl.pallas_call(
        body, name="conv_bwd", grid=(n,),
        in_specs=[col(0), col(0), col(1), col(2), col(3), prev(0), prev(2), nxt(0), nxt(1), nxt(3),
                  pl.BlockSpec((3, D_CONV), lambda i: (0, 0))],
        out_specs=[pl.BlockSpec((ts, 4 * D_CONV), lambda i: (i, 0)), pl.BlockSpec((8, D_CONV), lambda i: (0, 0))],
        out_shape=[SDS((seq, 4 * D_CONV), BF16), SDS((8, D_CONV), F32)],
        scratch_shapes=[pltpu.VMEM((ts + HALO, D_CONV), F32), pltpu.VMEM((ts + HALO, D_CONV), F32),
                        pltpu.VMEM((3 * CHUNK_ROWS, D_CONV), F32)],
        compiler_params=_cp(("arbitrary",), 48),
    )(dycat, u, u, u, u, u, u, dycat, u, u, conv_w)


def _inproj_bwd(du_conv, du_za, du_mla, w_t, parts):
    seq = du_conv.shape[0]
    dm = w_t.shape[1]
    tm, tn = min(DH_TM, seq), DH_TN
    ni, nj = seq // tm, dm // tn
    na = len(parts)

    def body(dc_ref, dz_ref, dm_ref, w_ref, *rest):
        part_refs, o_ref, recv_refs = rest[:na], rest[na], rest[na + 1:2 * na + 1]
        ssem, rsem = rest[2 * na + 1:]
        i, j = pl.program_id(0), pl.program_id(1)
        sends, recvs = _chip_exchange_copies(part_refs, recv_refs, ssem, rsem)

        @pl.when((i == 0) & (j == 0))
        def _():
            for cp in sends:
                cp.start()

        acc = _nn(dc_ref[...], w_ref[0:U_ZA, :])
        acc = acc + _nn(dz_ref[...], w_ref[U_ZA:U_CQ, :])
        acc = acc + _nn(dm_ref[...], w_ref[U_CQ:U_COLS, :])
        o_ref[...] = acc

        @pl.when((i == ni - 1) & (j == nj - 1))
        def _():
            for cp in recvs:
                cp.wait_recv()
            for cp in sends:
                cp.wait_send()

    outs = pl.pallas_call(
        body, name="inproj_bwd", grid=(ni, nj),
        in_specs=[pl.BlockSpec((tm, U_ZA), lambda i, j: (i, 0)), pl.BlockSpec((tm, D_ATTN), lambda i, j: (i, 0)),
                  pl.BlockSpec((tm, MLA_COLS), lambda i, j: (i, 0)), pl.BlockSpec((U_COLS, tn), lambda i, j: (0, j))]
                 + [ANY] * na,
        out_specs=[pl.BlockSpec((tm, tn), lambda i, j: (i, j))] + [ANY] * na,
        out_shape=[SDS((seq, dm), F32)] + [SDS(p.shape, p.dtype) for p in parts],
        scratch_shapes=[pltpu.SemaphoreType.DMA((3 * na,))] * 2,
        compiler_params=_cp(("arbitrary", "arbitrary"), 48),
    )(du_conv, du_za, du_mla, w_t, *parts)
    return outs[0], outs[1:]


def _prenorm_bwd(x, dh, dout, norm_g, scale):
    seq, dm = x.shape
    ts = min(ROW_T, seq)
    n = seq // ts

    def body(x_ref, dh_ref, dout_ref, g_ref, sc_ref, gx_ref, st_ref, acc_ref):
        i = pl.program_id(0)

        @pl.when(i == 0)
        def _():
            acc_ref[...] = jnp.zeros_like(acc_ref)

        xv = x_ref[...]
        xn, r = _rms(xv, dm)
        dh_v = dh_ref[...]
        gv = g_ref[...]
        one_sc = 1.0 + sc_ref[...]

        def fold(a):
            return jnp.sum(a.reshape(ts // 8, 8, dm), axis=0)

        acc_ref[0:8, :] += fold(dh_v)
        acc_ref[8:16, :] += fold(dh_v * (xn * gv))
        dxg = dh_v * one_sc
        acc_ref[16:24, :] += fold(dxg * xn)
        dxn = dxg * gv
        mean = jnp.sum(dxn * xn, axis=-1, keepdims=True) * (1.0 / dm)
        gx_ref[...] = dout_ref[...] + r * (dxn - xn * mean)

        @pl.when(i == n - 1)
        def _():
            st_ref[...] = jnp.zeros_like(st_ref)
            for k in range(3):
                st_ref[k:k + 1, :] = jnp.sum(acc_ref[8 * k:8 * k + 8, :], axis=0, keepdims=True)

    row = pl.BlockSpec((ts, dm), lambda i: (i, 0))
    vec = pl.BlockSpec((1, dm), lambda i: (0, 0))
    return pl.pallas_call(
        body, name="prenorm_bwd", grid=(n,), in_specs=[row, row, row, vec, vec],
        out_specs=[row, pl.BlockSpec((8, dm), lambda i: (0, 0))],
        out_shape=[SDS((seq, dm), F32), SDS((8, dm), F32)],
        scratch_shapes=[pltpu.VMEM((24, dm), F32)], input_output_aliases={2: 0},
        compiler_params=_cp(("arbitrary",), 52),
    )(x, dh, dout, norm_g, scale)


def _unshard_cols(g):
    return jnp.transpose(g, (1, 0, 2)).reshape(g.shape[1], -1)


def _shard_cols(w):
    r = w.shape[0]
    return jnp.transpose(w.reshape(r, N_CHIPS, -1), (1, 0, 2))


W_IN_COLS = 4 * D_CONV + Q_LORA + KV_LORA + QK_ROPE + D_ATTN
SHARD_ROWS = W_IN_COLS // N_CHIPS
SHARD_PAD = 1536


def _w_in_pieces():
    c4 = 4 * D_CONV
    groups = [(0, c4, 0), (c4, c4 + Q_LORA, U_CQ), (c4 + Q_LORA, c4 + Q_LORA + KV_LORA, U_CKV),
              (c4 + Q_LORA + KV_LORA, W_IN_COLS - D_ATTN, U_KR), (W_IN_COLS - D_ATTN, W_IN_COLS, U_ZA)]
    pieces = []
    for lo, hi, my in groups:
        for chip in range(N_CHIPS):
            a, b = max(lo, chip * SHARD_ROWS), min(hi, (chip + 1) * SHARD_ROWS)
            if a < b:
                pieces.append((chip, a - chip * SHARD_ROWS, b - a, my + a - lo))
    return pieces


def _w_t_to_my(g):
    w = jnp.zeros((U_COLS, g.shape[2]), g.dtype)
    for chip, row, n, my in _w_in_pieces():
        w = lax.dynamic_update_slice(w, g[chip, row:row + n], (my, 0))
    return w


def _w_t_from_my(g_conv, g_za, g_mla):
    w = jnp.zeros((N_CHIPS, SHARD_PAD, g_conv.shape[1]), g_conv.dtype)
    for chip, row, n, my in _w_in_pieces():
        src, base = (g_conv, 0) if my < U_ZA else (g_za, U_ZA) if my < U_CQ else (g_mla, U_CQ)
        w = lax.dynamic_update_slice(w, src[my - base:my - base + n][None], (chip, row, 0))
    return w


def _heads_pad(w):
    r = w.shape[0]
    w3 = w.reshape(r, N_HEADS, QK_HEAD)
    return jnp.pad(w3, ((0, 0), (0, 0), (0, HEAD_PAD - QK_HEAD))).reshape(r, Q_PAD)


def _heads_unpad(w):
    r = w.shape[0]
    return w.reshape(r, N_HEADS, HEAD_PAD)[:, :, :QK_HEAD].reshape(r, N_HEADS * QK_HEAD)


def kernel(x, c, positions, ada_w, ada_b, norm_g, w_in, conv_w, q_a_g, w_q_b, kv_a_g, w_kv_b, q_g, k_g, w_out, loss_target, m_ada_w, m_ada_b, m_norm_g, m_w_in, m_conv_w, m_q_a_g, m_w_q_b, m_kv_a_g, m_w_kv_b, m_q_g, m_k_g, m_w_out, v_ada_w, v_ada_b, v_norm_g, v_w_in, v_conv_w, v_q_a_g, v_w_q_b, v_kv_a_g, v_w_kv_b, v_q_g, v_k_g, v_w_out):
    mx, my, mc = _place()
    chip = 2 * mx + my
    me = 2 * chip + mc
    seq = x.shape[1]
    x2, t2 = x[0], loss_target[0]
    cw_cols = conv_w.shape[2]

    small = jnp.zeros((8, D_MODEL), F32)
    small = small.at[0].set(c[0])
    small = small.at[1:4, :cw_cols].set(conv_w[0])
    small_all = _gather8(small, "gather_c_conv", False)[0]
    c_all = small_all[:, 0, :]
    conv_full = jnp.transpose(small_all.reshape(N_CHIPS, 2, 8, D_MODEL)[:, 0, 1:4, :cw_cols], (1, 0, 2)).reshape(3, D_CONV)

    ada_cols = ada_w.shape[2]
    b_k = lax.dynamic_slice(ada_b, (0, chip * ada_cols), (1, ada_cols))
    mod_k, sc_all = _ada_mod(c_all, ada_w[0], b_k)
    mod_all = _gather8(mod_k, "gather_mod", False)[0]
    mod_row = lax.dynamic_slice(mod_all.reshape(N_CHIPS, 2, N_DEV, ada_cols), (0, mc, me, 0), (N_CHIPS, 1, 1, ada_cols))
    mod_row = mod_row.reshape(3, D_MODEL)
    shift, scale, gate = mod_row[0:1], mod_row[1:2], mod_row[2:3]

    def own_slot(g, s):
        return lax.dynamic_update_slice(g, s[None], (chip, 0, 0))

    w_in_t, m_w_in_t, v_w_in_t = [jnp.transpose(a[0]) for a in (w_in, m_w_in, v_w_in)]
    shard_in = jnp.pad(w_in_t.astype(BF16), ((0, SHARD_PAD - SHARD_ROWS), (0, 0)))
    h, g_in = _prenorm_gather(x2, norm_g, scale, shift, shard_in)
    g_in = own_slot(g_in, shard_in)
    w_t = _w_t_to_my(g_in)

    later = [w_q_b[0].astype(BF16), w_kv_b[0].astype(BF16), w_out[0].astype(BF16)]
    u, got = _inproj(h, w_t, later)
    g_q, g_kv, g_out = [own_slot(g, s) for g, s in zip(got, later)]
    wq = _heads_pad(_unshard_cols(g_q))
    wkv = _unshard_cols(g_kv).reshape(KV_LORA, N_HEADS, QK_NOPE + V_HEAD)
    wkn = wkv[:, :, :QK_NOPE].reshape(KV_LORA, N_HEADS * QK_NOPE)
    wv = wkv[:, :, QK_NOPE:].reshape(KV_LORA, D_ATTN)
    wo = g_out.reshape(N_CHIPS * g_out.shape[1], D_MODEL)
    y_conv = _conv_fwd(u, conv_full)
    pos = positions.reshape(seq, 1)
    inv_freq = ROPE_BASE ** (-jnp.arange(0, QK_ROPE, 2, dtype=F32) / QK_ROPE)
    freq = jnp.concatenate([inv_freq, inv_freq, jnp.zeros((LANES - QK_ROPE,), F32)]).reshape(1, LANES)
    q_g_pad = jnp.pad(q_g, ((0, 0), (0, HEAD_PAD - QK_HEAD)))
    k_g_pad = jnp.pad(k_g, ((0, 0), (0, HEAD_PAD - QK_HEAD)))
    q, k, v = _mla_prep(u, pos, freq, q_a_g, wq, kv_a_g, wkn, wv, q_g_pad, k_g_pad)
    o, y_attn, lse = _flash_fwd(q, k, v, u)
    dout, dy, dycat, st_out = _outproj_loss(y_conv, y_attn, x2, t2, gate, wo)

    dw_out = jnp.concatenate([_matmul_tn(y_conv, dy, "dw_out_conv"), _matmul_tn(y_attn, dy, "dw_out_attn")], axis=0)
    do_t, du_za, delta = _attn_gate_bwd(dycat, o, u)
    dq, dk, dv = _flash_bwd(q, k, v, do_t, lse, delta)
    du_mla, dwq, dwkn, dwv, sg_mla = _mla_bwd(dq, dk, dv, u, pos, freq, q_a_g, wq, kv_a_g, wkn, wv, q_g_pad, k_g_pad)
    du_conv, dconv_w = _conv_bwd(dycat, u, conv_full)
    dw_conv = _matmul_tn(du_conv, h, "dw_in_conv")
    dw_za = _matmul_tn(du_za, h, "dw_in_za")
    dw_mla = _matmul_tn(du_mla, h, "dw_in_mla")

    dw_q_nat = _heads_unpad(dwq).astype(BF16)
    dw_kv_nat = jnp.concatenate([dwkn.reshape(KV_LORA, N_HEADS, QK_NOPE), dwv.reshape(KV_LORA, N_HEADS, V_HEAD)],
                                axis=2).reshape(KV_LORA, N_HEADS * (QK_NOPE + V_HEAD)).astype(BF16)
    grads = [_w_t_from_my(dw_conv, dw_za, dw_mla), _shard_cols(dw_q_nat), _shard_cols(dw_kv_nat),
             dw_out.reshape(N_CHIPS, dw_out.shape[0] // N_CHIPS, D_MODEL)]
    theirs = _rs_core_swap(grads)
    names = ["w_in", "w_q_b", "w_kv_b", "w_out"]
    core = jnp.reshape(mc, (1,)).astype(jnp.int32)
    parts = [_add_half_bf16(g, b, core, "rs_add_" + nm) for g, b, nm in zip(grads, theirs, names)]
    dh, recv = _inproj_bwd(du_conv, du_za, du_mla, w_t, parts)
    recv = [lax.dynamic_update_slice(r, lax.dynamic_slice(p, (chip, 0, 0), (1,) + p.shape[1:]), (chip, 0, 0))
            for r, p in zip(recv, parts)]
    halves = [_sum_chips(p, "rs_sum_" + nm) for p, nm in zip(recv, names)]
    joined = _rs_core_join(halves)
    joined = [lax.dynamic_update_slice(j, hf[None], (mc, 0, 0)) for j, hf in zip(joined, halves)]
    g_big = [j.reshape(2 * j.shape[1], j.shape[2]) for j in joined]
    grad_x, st_in = _prenorm_bwd(x2, dh, dout, norm_g, scale)

    sgrad = jnp.zeros((8, D_MODEL), F32)
    sgrad = sgrad.at[0:2].set(st_in[0:2])
    sgrad = sgrad.at[2].set(st_out[0])
    sgrad = sgrad.at[3].set(st_in[2])
    sgrad = sgrad.at[4, :D_CONV].set(dconv_w[0]).at[4, D_CONV:].set(dconv_w[1])
    sgrad = sgrad.at[5, :D_CONV].set(dconv_w[2]).at[5, D_CONV:].set(sg_mla[0, :D_CONV])
    sgrad = sgrad.at[6, :HEAD_PAD].set(sg_mla[0, SG_KG:SG_KG + HEAD_PAD])
    sgrad = sgrad.at[7].set(st_out[1])
    sg_all, sg_sum = _gather8(sgrad, "gather_small_grads", True)
    loss = sg_sum[7, 0]
    g_ada_b = sg_sum[0:3].reshape(1, 3 * D_MODEL)
    g_norm_g = sg_sum[3:4]
    conv_sum = jnp.stack([sg_sum[4, :D_CONV], sg_sum[4, D_CONV:], sg_sum[5, :D_CONV]])
    g_conv_w = lax.dynamic_slice(conv_sum, (0, chip * cw_cols), (3, cw_cols))
    g_q_a_g = sg_sum[5:6, D_CONV + SG_QAG:D_CONV + SG_QAG + Q_LORA]
    g_kv_a_g = sg_sum[5:6, D_CONV + SG_KVAG:D_CONV + SG_KVAG + KV_LORA]
    g_q_g = sg_sum[5:6, D_CONV + SG_QG:D_CONV + SG_QG + QK_HEAD]
    g_k_g = sg_sum[6:7, :QK_HEAD]
    dmod_k = lax.dynamic_slice(sg_all[:, 0:3, :].reshape(N_DEV, 3 * D_MODEL), (0, chip * ada_cols), (N_DEV, ada_cols))

    g_ada_w, d_ada_w, nm_ada_w, nv_ada_w = _ada_w_update(sc_all, dmod_k, ada_w[0], m_ada_w[0], v_ada_w[0])
    upd = {}
    big = {"w_q_b": (w_q_b, m_w_q_b, v_w_q_b), "w_kv_b": (w_kv_b, m_w_kv_b, v_w_kv_b), "w_out": (w_out, m_w_out, v_w_out)}
    for nm, g in zip(names[1:], g_big[1:]):
        w_, m_, v_ = big[nm]
        upd[nm] = (g,) + tuple(_adamw(w_[0], g, m_[0], v_[0], "adamw_" + nm))
    d_t, nm_t, nv_t, g_t = _adamw(w_in_t, g_big[0], m_w_in_t, v_w_in_t, "adamw_w_in", echo_g=True)
    upd["w_in"] = tuple(jnp.transpose(a) for a in (g_t, d_t, nm_t, nv_t))
    small_w = {"ada_b": (ada_b, m_ada_b, v_ada_b, g_ada_b), "norm_g": (norm_g, m_norm_g, v_norm_g, g_norm_g),
               "conv_w": (conv_w[0], m_conv_w[0], v_conv_w[0], g_conv_w), "q_a_g": (q_a_g, m_q_a_g, v_q_a_g, g_q_a_g),
               "kv_a_g": (kv_a_g, m_kv_a_g, v_kv_a_g, g_kv_a_g), "q_g": (q_g, m_q_g, v_q_g, g_q_g),
               "k_g": (k_g, m_k_g, v_k_g, g_k_g)}
    for nm, (w_, m_, v_, g) in small_w.items():
        upd[nm] = (g,) + tuple(_adamw(w_, g, m_, v_, "adamw_" + nm))
    upd["ada_w"] = (g_ada_w, d_ada_w, nm_ada_w, nv_ada_w)

    order = ["ada_w", "ada_b", "norm_g", "w_in", "conv_w", "q_a_g", "w_q_b", "kv_a_g", "w_kv_b", "q_g", "k_g", "w_out"]
    lead1 = {"ada_w", "w_in", "conv_w", "w_q_b", "w_kv_b", "w_out"}

    def shaped(nm, a):
        return a[None] if nm in lead1 else a

    outs = [loss, grad_x[None]]
    for idx in range(4):
        outs += [shaped(nm, upd[nm][idx]) for nm in order]
    return tuple(outs)
```

```python
import math

import jax
import jax.numpy as jnp
from jax import lax
from jax.experimental import pallas as pl
from jax.experimental.pallas import tpu as pltpu

F32 = jnp.float32
BF16 = jnp.bfloat16
MESH = pl.DeviceIdType.MESH
SDS = jax.ShapeDtypeStruct
ANY = pl.BlockSpec(memory_space=pl.ANY)

D_MODEL = 2048
D_CONV = 1024
N_HEADS = 8
QK_NOPE = 128
QK_ROPE = 64
QK_HEAD = QK_NOPE + QK_ROPE
V_HEAD = 128
D_ATTN = N_HEADS * V_HEAD
Q_LORA = 512
KV_LORA = 256
ROPE_BASE = 10000.0
EPS = 1e-6
LOG2E = math.log2(math.e)
LN2 = math.log(2.0)
ADAM_LR, ADAM_B1, ADAM_B2, ADAM_EPS, ADAM_WD, ADAM_STEP = 0.001, 0.9, 0.999, 1e-08, 0.01, 10
N_CHIPS = 4
N_DEV = 8

LANES = 128
V7X_VMEM_BYTES = 64 * 1024 * 1024
MIB = 1024 * 1024

HEAD_PAD = 256
Q_PAD = N_HEADS * HEAD_PAD
U_ZA = 4 * D_CONV
U_CQ = U_ZA + D_ATTN
U_CKV = U_CQ + Q_LORA
U_KR = U_CKV + KV_LORA
KR_PAD = 256
U_COLS = U_KR + KR_PAD
MLA_COLS = Q_LORA + KV_LORA + KR_PAD

ATT_T = 512
INPROJ_TM, INPROJ_TN = 1024, 1024
ROW_T = 512
OUT_T = 256
DH_TM, DH_TN = 512, 1024
TN_TM, TN_TN, TN_TK = 1024, 1024, 2048
ATT_UNROLL = 8


def _cp(sem=None, vmem_mib=None, **kw):
    if sem is not None:
        kw["dimension_semantics"] = sem
    if vmem_mib is not None:
        kw["vmem_limit_bytes"] = min(vmem_mib * MIB, V7X_VMEM_BYTES - 4 * MIB)
    return pltpu.CompilerParams(**kw)


def _sigmoid(z):
    return 1.0 / (1.0 + jnp.exp(-z))


def _silu_grad(z, sg):
    return sg * (1.0 + z * (1.0 - sg))


def _nt(a, b):
    return lax.dot_general(a, b, (((1,), (1,)), ((), ())), preferred_element_type=F32)


def _tn(a, b):
    return lax.dot_general(a, b, (((0,), (0,)), ((), ())), preferred_element_type=F32)


def _nn(a, b):
    return jnp.dot(a, b, preferred_element_type=F32)


def _place():
    return lax.axis_index("x"), lax.axis_index("y"), lax.axis_index("c")


def _gather8(v, name, with_sum):
    rows, cols = v.shape

    def body(v_ref, out_ref, *rest):
        if with_sum:
            sum_ref, send_sems, recv_sems = rest
        else:
            send_sems, recv_sems = rest
        mx, my, mc = _place()
        me = 4 * mx + 2 * my + mc
        out_ref[me] = v_ref[...]
        peers = []
        for d in range(1, N_DEV):
            px = 1 - mx if d & 4 else mx
            py = 1 - my if d & 2 else my
            pc = 1 - mc if d & 1 else mc
            peers.append((px, py, pc))

        def copy(d, slot, to):
            return pltpu.make_async_remote_copy(
                src_ref=v_ref, dst_ref=out_ref.at[slot], send_sem=send_sems.at[d], recv_sem=recv_sems.at[d],
                device_id=to, device_id_type=MESH)

        sends = [copy(d, me, p) for d, p in enumerate(peers)]
        for cp in sends:
            cp.start()
        for d, (px, py, pc) in enumerate(peers):
            copy(d, 4 * px + 2 * py + pc, (px, py, pc)).wait_recv()
        for cp in sends:
            cp.wait_send()
        if with_sum:
            acc = out_ref[0]
            for b in range(1, N_DEV):
                acc = acc + out_ref[b]
            sum_ref[...] = acc

    out_shape = [SDS((N_DEV, rows, cols), F32)]
    if with_sum:
        out_shape.append(SDS((rows, cols), F32))
    vm = pl.BlockSpec(memory_space=pltpu.VMEM)
    return pl.pallas_call(
        body, name=name, out_shape=out_shape, in_specs=[vm], out_specs=[vm] * len(out_shape),
        scratch_shapes=[pltpu.SemaphoreType.DMA((N_DEV - 1,)), pltpu.SemaphoreType.DMA((N_DEV - 1,))],
    )(v)


def _chips_of(mx, my):
    chips = [(mx, 1 - my), (1 - mx, my), (1 - mx, 1 - my)]
    return chips, [2 * px + py for px, py in chips]


def _prenorm_gather(x, norm_g, scale, shift, shard):
    seq, dm = x.shape
    tm = min(INPROJ_TM, seq)
    ni = seq // tm
    half_rows = shard.shape[0] // 2

    def body(x_ref, g_ref, sc_ref, sh_ref, shard_ref, h_ref, got_ref, s1, r1, s2, r2):
        i = pl.program_id(0)
        mx, my, mc = _place()
        k = 2 * mx + my
        sib = (mx, my, 1 - mc)
        chips, kks = _chips_of(mx, my)

        def half(slot, c):
            return got_ref.at[slot, pl.ds(c * half_rows, half_rows)]

        def over_ici(d, slot):
            return pltpu.make_async_remote_copy(
                src_ref=shard_ref.at[pl.ds(mc * half_rows, half_rows)], dst_ref=half(slot, mc), send_sem=s1.at[d],
                recv_sem=r1.at[d], device_id=(chips[d][0], chips[d][1], mc), device_id_type=MESH)

        def to_sibling(d, c):
            return pltpu.make_async_remote_copy(
                src_ref=half(kks[d], c), dst_ref=half(kks[d], c), send_sem=s2.at[d], recv_sem=r2.at[d],
                device_id=sib, device_id_type=MESH)

        @pl.when(i == 0)
        def _():
            for d in range(3):
                over_ici(d, k).start()

        xv = x_ref[...]
        r = lax.rsqrt(jnp.mean(xv * xv, axis=-1, keepdims=True) + EPS)
        h_ref[...] = ((xv * r * g_ref[...]) * (1.0 + sc_ref[...]) + sh_ref[...]).astype(BF16)

        @pl.when(i == ni - 1)
        def _():
            for d in range(3):
                over_ici(d, kks[d]).wait_recv()
                to_sibling(d, mc).start()
            for d in range(3):
                to_sibling(d, 1 - mc).wait_recv()
            for d in range(3):
                over_ici(d, k).wait_send()
                to_sibling(d, mc).wait_send()

    vec = pl.BlockSpec((1, dm), lambda i: (0, 0))
    row = pl.BlockSpec((tm, dm), lambda i: (i, 0))
    return pl.pallas_call(
        body, name="prenorm_gather", grid=(ni,), in_specs=[row, vec, vec, vec, ANY], out_specs=[row, ANY],
        out_shape=[SDS((seq, dm), BF16), SDS((N_CHIPS,) + shard.shape, shard.dtype)],
        scratch_shapes=[pltpu.SemaphoreType.DMA((3,))] * 4,
        compiler_params=_cp(("arbitrary",), 48),
    )(x, norm_g, scale, shift, shard)


def _rs_core_swap(grads):
    na = len(grads)
    halves = [g.shape[1] // 2 for g in grads]

    def body(*refs):
        ins, outs = refs[:na], refs[na:2 * na]
        ssem, rsem = refs[2 * na:]
        mx, my, mc = _place()
        sib = (mx, my, 1 - mc)
        sends = []
        for a in range(na):
            cp = pltpu.make_async_remote_copy(
                src_ref=ins[a].at[:, pl.ds((1 - mc) * halves[a], halves[a])], dst_ref=outs[a],
                send_sem=ssem.at[a], recv_sem=rsem.at[a], device_id=sib, device_id_type=MESH)
            cp.start()
            sends.append(cp)
        for cp in sends:
            cp.wait_recv()
        for cp in sends:
            cp.wait_send()

    return pl.pallas_call(
        body, name="rs_core_swap", out_shape=[SDS((N_CHIPS, h) + g.shape[2:], g.dtype) for g, h in zip(grads, halves)],
        in_specs=[ANY] * na, out_specs=[ANY] * na,
        scratch_shapes=[pltpu.SemaphoreType.DMA((na,))] * 2,
    )(*grads)


def _chip_exchange_copies(ins, outs, ssem, rsem):
    mx, my, mc = _place()
    k = 2 * mx + my
    chips, kks = _chips_of(mx, my)
    sends, recvs = [], []
    for a in range(len(ins)):
        for d, (px, py) in enumerate(chips):
            def copy(dst_slot):
                return pltpu.make_async_remote_copy(
                    src_ref=ins[a].at[kks[d]], dst_ref=outs[a].at[dst_slot], send_sem=ssem.at[3 * a + d],
                    recv_sem=rsem.at[3 * a + d], device_id=(px, py, mc), device_id_type=MESH)
            sends.append(copy(k))
            recvs.append(copy(kks[d]))
    return sends, recvs


def _rs_core_join(halves):
    na = len(halves)

    def body(*refs):
        ins, outs = refs[:na], refs[na:2 * na]
        ssem, rsem = refs[2 * na:]
        mx, my, mc = _place()
        sib = (mx, my, 1 - mc)
        sends = []
        for a in range(na):
            cp = pltpu.make_async_remote_copy(
                src_ref=ins[a], dst_ref=outs[a].at[mc], send_sem=ssem.at[a], recv_sem=rsem.at[a],
                device_id=sib, device_id_type=MESH)
            cp.start()
            sends.append(cp)
        for a in range(na):
            pltpu.make_async_remote_copy(
                src_ref=ins[a], dst_ref=outs[a].at[1 - mc], send_sem=ssem.at[a], recv_sem=rsem.at[a],
                device_id=sib, device_id_type=MESH).wait_recv()
        for cp in sends:
            cp.wait_send()

    return pl.pallas_call(
        body, name="rs_core_join", out_shape=[SDS((2,) + h.shape, h.dtype) for h in halves],
        in_specs=[ANY] * na, out_specs=[ANY] * na,
        scratch_shapes=[pltpu.SemaphoreType.DMA((na,))] * 2,
    )(*halves)


def _row_tile(rows, limit, mult=16):
    if rows <= limit:
        return rows
    best = None
    for t in range(mult, limit + 1, mult):
        if rows % t == 0:
            best = t
    assert best is not None, rows
    return best


def _add_half_bf16(g, b, core, name):
    _, h, cols = b.shape
    tb = _row_tile(h, 512)
    nb = h // tb

    def body(core_ref, g_ref, b_ref, o_ref):
        o_ref[...] = (g_ref[...].astype(F32) + b_ref[...].astype(F32)).astype(BF16)

    spec = pl.BlockSpec((None, tb, cols), lambda kk, i, core_ref: (kk, i, 0))
    return pl.pallas_call(
        body, name=name,
        grid_spec=pltpu.PrefetchScalarGridSpec(
            num_scalar_prefetch=1, grid=(N_CHIPS, nb),
            in_specs=[pl.BlockSpec((None, tb, cols), lambda kk, i, core_ref: (kk, core_ref[0] * nb + i, 0)), spec],
            out_specs=spec),
        out_shape=SDS(b.shape, BF16), compiler_params=_cp(("arbitrary", "arbitrary")),
    )(core, g, b)


def _sum_chips(p, name):
    _, rows, cols = p.shape
    tb = _row_tile(rows, 256)

    def body(p_ref, o_ref):
        acc = p_ref[0].astype(F32)
        for j in range(1, N_CHIPS):
            acc = acc + p_ref[j].astype(F32)
        o_ref[...] = acc

    return pl.pallas_call(
        body, name=name, grid=(rows // tb,),
        in_specs=[pl.BlockSpec((N_CHIPS, tb, cols), lambda i: (0, i, 0))],
        out_specs=pl.BlockSpec((tb, cols), lambda i: (i, 0)), out_shape=SDS((rows, cols), F32),
        compiler_params=_cp(("arbitrary",)),
    )(p)


def _adamw_math(w, g, m, v):
    m2 = ADAM_B1 * m + (1.0 - ADAM_B1) * g
    v2 = ADAM_B2 * v + (1.0 - ADAM_B2) * (g * g)
    m_hat = m2 / (1.0 - ADAM_B1 ** ADAM_STEP)
    v_hat = v2 / (1.0 - ADAM_B2 ** ADAM_STEP)
    delta = -ADAM_LR * (m_hat / (jnp.sqrt(v_hat) + ADAM_EPS) + ADAM_WD * w)
    return delta, m2, v2


def _adamw(w, g, m, v, name, echo_g=False):
    rows, cols = w.shape
    tb = _row_tile(rows, 256, mult=8)
    nout = 4 if echo_g else 3

    def body(w_ref, g_ref, m_ref, v_ref, d_ref, m2_ref, v2_ref, *echo):
        gv = g_ref[...]
        d, m2, v2 = _adamw_math(w_ref[...], gv, m_ref[...], v_ref[...])
        d_ref[...] = d
        m2_ref[...] = m2
        v2_ref[...] = v2
        if echo_g:
            echo[0][...] = gv

    spec = pl.BlockSpec((tb, cols), lambda i: (i, 0))
    return pl.pallas_call(
        body, name=name, grid=(rows // tb,), in_specs=[spec] * 4, out_specs=[spec] * nout,
        out_shape=[SDS((rows, cols), F32)] * nout, compiler_params=_cp(("arbitrary",), 40),
    )(w, g, m, v)


def _ada_w_update(sc_all, dmod_k, w, m, v):
    rows, cols = w.shape
    tb = 256

    def body(s_ref, dm_ref, w_ref, m_ref, v_ref, g_ref, d_ref, m2_ref, v2_ref):
        g = _tn(s_ref[...].astype(BF16), dm_ref[...].astype(BF16))
        d, m2, v2 = _adamw_math(w_ref[...], g, m_ref[...], v_ref[...])
        g_ref[...] = g
        d_ref[...] = d
        m2_ref[...] = m2
        v2_ref[...] = v2

    spec = pl.BlockSpec((tb, cols), lambda i: (i, 0))
    return pl.pallas_call(
        body, name="ada_w_update", grid=(rows // tb,),
        in_specs=[pl.BlockSpec((N_DEV, tb), lambda i: (0, i)), pl.BlockSpec((N_DEV, cols), lambda i: (0, 0)), spec, spec, spec],
        out_specs=[spec] * 4, out_shape=[SDS((rows, cols), F32)] * 4, compiler_params=_cp(("arbitrary",), 40),
    )(sc_all, dmod_k, w, m, v)


def _ada_mod(c_all, w, b_k):
    rows, cols = w.shape
    tn = 512

    def body(c_ref, w_ref, b_ref, o_ref, s_ref):
        cv = c_ref[...]
        s = cv * _sigmoid(cv)
        s_ref[...] = s
        o_ref[...] = _nn(s.astype(BF16), w_ref[...].astype(BF16)) + b_ref[...]

    return pl.pallas_call(
        body, name="ada_mod", grid=(cols // tn,),
        in_specs=[pl.BlockSpec((N_DEV, rows), lambda j: (0, 0)), pl.BlockSpec((rows, tn), lambda j: (0, j)),
                  pl.BlockSpec((1, tn), lambda j: (0, j))],
        out_specs=[pl.BlockSpec((N_DEV, tn), lambda j: (0, j)), pl.BlockSpec((N_DEV, rows), lambda j: (0, 0))],
        out_shape=[SDS((N_DEV, cols), F32), SDS((N_DEV, rows), F32)], compiler_params=_cp(("arbitrary",)),
    )(c_all, w, b_k)


def _inproj(h, w_t, shards):
    seq, dm = h.shape
    ncols = w_t.shape[0]
    tm, tn = min(INPROJ_TM, seq), INPROJ_TN
    ni, nj = seq // tm, ncols // tn
    na = len(shards)

    def body(h_ref, w_ref, *rest):
        shard_refs, u_ref, got_refs = rest[:na], rest[na], rest[na + 1:2 * na + 1]
        ssem, rsem = rest[2 * na + 1:]
        i, j = pl.program_id(0), pl.program_id(1)
        mx, my, mc = _place()
        k = 2 * mx + my
        chips, kks = _chips_of(mx, my)

        def copy(a, d, slot):
            return pltpu.make_async_remote_copy(
                src_ref=shard_refs[a], dst_ref=got_refs[a].at[slot], send_sem=ssem.at[3 * a + d],
                recv_sem=rsem.at[3 * a + d], device_id=(chips[d][0], chips[d][1], mc), device_id_type=MESH)

        @pl.when((i == 0) & (j == 0))
        def _():
            for a in range(na):
                for d in range(3):
                    copy(a, d, k).start()

        u_ref[...] = _nt(h_ref[...], w_ref[...]).astype(BF16)

        @pl.when((i == ni - 1) & (j == nj - 1))
        def _():
            for a in range(na):
                for d in range(3):
                    copy(a, d, kks[d]).wait_recv()
            for a in range(na):
                for d in range(3):
                    copy(a, d, k).wait_send()

    outs = pl.pallas_call(
        body, name="inproj", grid=(ni, nj),
        in_specs=[pl.BlockSpec((tm, dm), lambda i, j: (i, 0)), pl.BlockSpec((tn, dm), lambda i, j: (j, 0))] + [ANY] * na,
        out_specs=[pl.BlockSpec((tm, tn), lambda i, j: (i, j))] + [ANY] * na,
        out_shape=[SDS((seq, ncols), BF16)] + [SDS((N_CHIPS,) + s.shape, s.dtype) for s in shards],
        scratch_shapes=[pltpu.SemaphoreType.DMA((3 * na,))] * 2,
        compiler_params=_cp(("arbitrary", "arbitrary"), 48),
    )(h, w_t, *shards)
    return outs[0], outs[1:]


HALO = 16
CHUNK_ROWS, CHUNK_LANES = 16, 512


def _conv_fwd(u, conv_w):
    seq = u.shape[0]
    ts = min(ROW_T, seq)
    hb = ts // HALO

    def body(xc_ref, bc_ref, cc_ref, zc_ref, xp_ref, cp_ref, w_ref, y_ref, ext_ref):
        i = pl.program_id(0)
        up = cp_ref[...].astype(F32) * xp_ref[...].astype(F32)
        ext_ref[0:HALO, :] = jnp.where(i > 0, up, 0.0)
        ext_ref[HALO:HALO + ts, :] = cc_ref[...].astype(F32) * xc_ref[...].astype(F32)
        for r0 in range(0, ts, CHUNK_ROWS):
            rows = slice(r0, r0 + CHUNK_ROWS)
            for c0 in range(0, D_CONV, CHUNK_LANES):
                cols = slice(c0, c0 + CHUNK_LANES)
                uc = ext_ref[HALO + r0:HALO + r0 + CHUNK_ROWS, cols]
                u1 = ext_ref[HALO - 1 + r0:HALO - 1 + r0 + CHUNK_ROWS, cols]
                u2 = ext_ref[HALO - 2 + r0:HALO - 2 + r0 + CHUNK_ROWS, cols]
                conv = w_ref[0:1, cols] * u2 + w_ref[1:2, cols] * u1 + w_ref[2:3, cols] * uc
                z = zc_ref[rows, cols].astype(F32)
                y_ref[rows, cols] = ((bc_ref[rows, cols].astype(F32) * conv) * (z * _sigmoid(z))).astype(BF16)

    def col(cb):
        return pl.BlockSpec((ts, D_CONV), lambda i: (i, cb))

    def prev(cb):
        return pl.BlockSpec((HALO, D_CONV), lambda i: (jnp.maximum(i * hb - 1, 0), cb))

    return pl.pallas_call(
        body, name="conv_fwd", grid=(seq // ts,),
        in_specs=[col(0), col(1), col(2), col(3), prev(0), prev(2), pl.BlockSpec((3, D_CONV), lambda i: (0, 0))],
        out_specs=pl.BlockSpec((ts, D_CONV), lambda i: (i, 0)), out_shape=SDS((seq, D_CONV), BF16),
        scratch_shapes=[pltpu.VMEM((ts + HALO, D_CONV), F32)],
        compiler_params=_cp(("arbitrary",), 40),
    )(u, u, u, u, u, u, conv_w)


def _rope_tables(pos_ref, freq_ref):
    ang = pos_ref[...].astype(F32) * freq_ref[...]
    lane = lax.broadcasted_iota(jnp.int32, ang.shape, 1)
    cs, sn = jnp.cos(ang), jnp.sin(ang)
    half = QK_ROPE // 2
    cos_t = jnp.where(lane < QK_ROPE, cs, 0.0)
    sin_lo = jnp.where(lane < half, sn, 0.0)
    sin_hi = jnp.where((lane >= half) & (lane < QK_ROPE), sn, 0.0)
    return cos_t, sin_lo, sin_hi


def _rope(blk, tables):
    cos_t, sin_lo, sin_hi = tables
    half = QK_ROPE // 2
    return blk * cos_t - pltpu.roll(blk, LANES - half, 1) * sin_lo + pltpu.roll(blk, half, 1) * sin_hi


def _rope_bwd(g, tables):
    cos_t, sin_lo, sin_hi = tables
    half = QK_ROPE // 2
    return g * cos_t + pltpu.roll(g, LANES - half, 1) * sin_lo - pltpu.roll(g, half, 1) * sin_hi


def _rms(v, n):
    r = lax.rsqrt(jnp.sum(v * v, axis=-1, keepdims=True) * (1.0 / n) + EPS)
    return v * r, r


def _mla_prep(u, pos, freq, q_a_g, wq, kv_a_g, wkn, wv, q_g, k_g):
    seq = u.shape[0]
    ts = min(ROW_T, seq)
    qscale = LOG2E / math.sqrt(QK_HEAD)

    def body(cq_ref, ckv_ref, kr_ref, pos_ref, freq_ref, qag_ref, wq_ref, kvag_ref, wkn_ref, wv_ref, qg_ref, kg_ref,
             q_ref, k_ref, v_ref):
        tables = _rope_tables(pos_ref, freq_ref)
        cqn, _ = _rms(cq_ref[...].astype(F32), Q_LORA)
        qp = _nn((cqn * qag_ref[...]).astype(BF16), wq_ref[...])
        qg = qg_ref[...]
        for h in range(N_HEADS):
            lo = h * HEAD_PAD
            qn, _ = _rms(qp[:, lo:lo + HEAD_PAD], QK_HEAD)
            qn = qn * qg
            q_ref[:, lo:lo + LANES] = (qn[:, :LANES] * qscale).astype(BF16)
            q_ref[:, lo + LANES:lo + HEAD_PAD] = (_rope(qn[:, LANES:], tables) * qscale).astype(BF16)
        ckvn, _ = _rms(ckv_ref[...].astype(F32), KV_LORA)
        ckvb = (ckvn * kvag_ref[...]).astype(BF16)
        kn = _nn(ckvb, wkn_ref[...])
        v_ref[...] = _nn(ckvb, wv_ref[...]).astype(BF16)
        kr = kr_ref[:, 0:LANES].astype(F32)
        ssr = jnp.sum(kr * kr, axis=-1, keepdims=True)
        kg = kg_ref[...]
        for h in range(N_HEADS):
            knh = kn[:, h * QK_NOPE:(h + 1) * QK_NOPE]
            r = lax.rsqrt((jnp.sum(knh * knh, axis=-1, keepdims=True) + ssr) * (1.0 / QK_HEAD) + EPS)
            lo = h * HEAD_PAD
            k_ref[:, lo:lo + LANES] = (knh * r * kg[:, :LANES]).astype(BF16)
            k_ref[:, lo + LANES:lo + HEAD_PAD] = _rope(kr * r * kg[:, LANES:], tables).astype(BF16)

    def full(a):
        return pl.BlockSpec(a.shape, lambda i: (0,) * a.ndim)

    return pl.pallas_call(
        body, name="mla_prep", grid=(seq // ts,),
        in_specs=[pl.BlockSpec((ts, Q_LORA), lambda i: (i, U_CQ // Q_LORA)),
                  pl.BlockSpec((ts, KV_LORA), lambda i: (i, U_CKV // KV_LORA)),
                  pl.BlockSpec((ts, KR_PAD), lambda i: (i, U_KR // KR_PAD)),
                  pl.BlockSpec((ts, 1), lambda i: (i, 0)), full(freq), full(q_a_g), full(wq), full(kv_a_g), full(wkn),
                  full(wv), full(q_g), full(k_g)],
        out_specs=[pl.BlockSpec((ts, Q_PAD), lambda i: (i, 0)), pl.BlockSpec((ts, Q_PAD), lambda i: (i, 0)),
                   pl.BlockSpec((ts, D_ATTN), lambda i: (i, 0))],
        out_shape=[SDS((seq, Q_PAD), BF16), SDS((seq, Q_PAD), BF16), SDS((seq, D_ATTN), BF16)],
        compiler_params=_cp(("arbitrary",), 48),
    )(u, u, u, pos, freq, q_a_g, wq, kv_a_g, wkn, wv, q_g, k_g)


def _flash_fwd(q, k, v, u):
    seq = q.shape[0]
    t = min(ATT_T, seq)
    n = seq // t
    n_pairs = n * (n + 1) // 2
    za_blk = U_ZA // V_HEAD

    def body(q_ref, k_ref, v_ref, z_ref, o_ref, y_ref, lse_ref, s_a, s_b, top_a, top_b, m_all, l_all, acc_all):
        ones = jnp.ones((16, t), BF16)
        bufs = ((s_a, top_a), (s_b, top_b))

        def rows(i):
            return pl.ds(pl.multiple_of(i * t, t), t)

        def scores(i, j, buf):
            s_ref, top_ref = buf
            s = _nt(k_ref[rows(j), :], q_ref[rows(i), :])
            ahead = lax.broadcasted_iota(jnp.int32, (t, t), 0) - lax.broadcasted_iota(jnp.int32, (t, t), 1)
            s = jnp.where(ahead <= (i - j) * t, s, -jnp.inf)
            s_ref[...] = s
            top_ref[...] = jnp.max(s, axis=0, keepdims=True)

        def absorb(i, j, buf):
            s_ref, top_ref = buf
            first = j == 0
            m = jnp.where(first, -jnp.inf, m_all[i])
            l = jnp.where(first, 0.0, l_all[i])
            acc = jnp.where(first, 0.0, acc_all[i])
            m_new = jnp.maximum(m, top_ref[...])
            alpha = jnp.exp2(m - m_new)
            p = jnp.exp2((s_ref[...] - m_new).astype(BF16))
            m_all[i] = m_new
            l_all[i] = alpha * l + _nn(ones, p)[0:1, :]
            acc_all[i] = alpha * acc + _tn(v_ref[rows(j), :], p)

        def trip(width):
            def walk(_, pair):
                i, j = pair
                for w in range(width):
                    done = j == i
                    ni, nj = jnp.where(done, i + 1, i), jnp.where(done, 0, j + 1)
                    scores(jnp.minimum(ni, n - 1), nj, bufs[(w + 1) % 2])
                    absorb(i, j, bufs[w % 2])
                    i, j = ni, nj
                return i, j
            return walk

        scores(0, 0, bufs[0])
        pair = lax.fori_loop(0, n_pairs // ATT_UNROLL, trip(ATT_UNROLL), (jnp.int32(0), jnp.int32(0)))
        if n_pairs % ATT_UNROLL:
            trip(n_pairs % ATT_UNROLL)(0, pair)

        def finish(i, carry):
            l = l_all[i]
            o = (acc_all[i] * (1.0 / l)).T
            lse_ref[:, rows(i)] = m_all[i] + jnp.log2(l)
            o_ref[rows(i), :] = o.astype(BF16)
            z = z_ref[rows(i), :].astype(F32)
            y_ref[rows(i), :] = (o * (z * _sigmoid(z))).astype(BF16)
            return carry

        lax.fori_loop(0, n, finish, 0)

    def col(width, cb):
        return pl.BlockSpec((seq, width), lambda h: (0, cb + h))

    return pl.pallas_call(
        body, name="flash_fwd", grid=(N_HEADS,),
        in_specs=[col(HEAD_PAD, 0), col(HEAD_PAD, 0), col(V_HEAD, 0), col(V_HEAD, za_blk)],
        out_specs=[col(V_HEAD, 0), col(V_HEAD, 0), pl.BlockSpec((None, 1, seq), lambda h: (h, 0, 0))],
        out_shape=[SDS((seq, D_ATTN), BF16), SDS((seq, D_ATTN), BF16), SDS((N_HEADS, 1, seq), F32)],
        scratch_shapes=[pltpu.VMEM((t, t), F32)] * 2 + [pltpu.VMEM((1, t), F32)] * 2
                       + [pltpu.VMEM((n, 1, t), F32)] * 2 + [pltpu.VMEM((n, V_HEAD, t), F32)],
        compiler_params=_cp(("arbitrary",), 52),
    )(q, k, v, u)


def _outproj_loss(y_conv, y_attn, x, target, gate, w_out):
    seq, dm = x.shape
    ts = min(OUT_T, seq)
    n = seq // ts
    dmix = w_out.shape[0]

    def body(yc_ref, ya_ref, x_ref, t_ref, gate_ref, wo_hbm, dout_ref, dy_ref, dyc_ref, stats_ref, wo_ref, sem, acc_ref):
        i = pl.program_id(0)

        @pl.when(i == 0)
        def _():
            cp = pltpu.make_async_copy(wo_hbm, wo_ref, sem)
            cp.start()
            cp.wait()
            acc_ref[...] = jnp.zeros_like(acc_ref)

        y = _nn(yc_ref[...], wo_ref[0:D_CONV, :]) + _nn(ya_ref[...], wo_ref[D_CONV:dmix, :])
        gate_v = gate_ref[...]
        diff = (x_ref[...] + gate_v * y) - t_ref[...]
        dout = diff * (1.0 / dm)
        dout_ref[...] = dout
        acc_ref[0:8, :] += jnp.sum((dout * y).reshape(ts // 8, 8, dm), axis=0)
        acc_ref[8:16, :] += jnp.sum((diff * diff).reshape(ts // 8, 8, dm), axis=0)
        dy = (dout * gate_v).astype(BF16)
        dy_ref[...] = dy
        dyc_ref[...] = _nt(dy, wo_ref[...]).astype(BF16)

        @pl.when(i == n - 1)
        def _():
            stats_ref[...] = jnp.zeros_like(stats_ref)
            stats_ref[0:1, :] = jnp.sum(acc_ref[0:8, :], axis=0, keepdims=True)
            loss = jnp.sum(acc_ref[8:16, :]) * (0.5 / dm)
            stats_ref[1:2, :] = jnp.full((1, dm), loss, F32)

    row = pl.BlockSpec((ts, dm), lambda i: (i, 0))
    half = pl.BlockSpec((ts, D_CONV), lambda i: (i, 0))
    return pl.pallas_call(
        body, name="outproj_loss", grid=(n,),
        in_specs=[half, half, row, row, pl.BlockSpec((1, dm), lambda i: (0, 0)), ANY],
        out_specs=[row, row, pl.BlockSpec((ts, dmix), lambda i: (i, 0)), pl.BlockSpec((8, dm), lambda i: (0, 0))],
        out_shape=[SDS((seq, dm), F32), SDS((seq, dm), BF16), SDS((seq, dmix), BF16), SDS((8, dm), F32)],
        scratch_shapes=[pltpu.VMEM(w_out.shape, BF16), pltpu.SemaphoreType.DMA(()), pltpu.VMEM((16, dm), F32)],
        compiler_params=_cp(("arbitrary",), 52),
    )(y_conv, y_attn, x, target, gate, w_out)


def _matmul_tn(a, b, name):
    seq, m = a.shape
    n = b.shape[1]
    tm, tn, tk = min(TN_TM, m), min(TN_TN, n), min(TN_TK, seq)
    nk = seq // tk

    def body(a_ref, b_ref, o_ref, acc_ref):
        kk = pl.program_id(2)

        @pl.when(kk == 0)
        def _():
            acc_ref[...] = jnp.zeros_like(acc_ref)

        acc_ref[...] += _tn(a_ref[...], b_ref[...])

        @pl.when(kk == nk - 1)
        def _():
            o_ref[...] = acc_ref[...].astype(BF16)

    return pl.pallas_call(
        body, name=name, grid=(m // tm, n // tn, nk),
        in_specs=[pl.BlockSpec((tk, tm), lambda i, j, kk: (kk, i)), pl.BlockSpec((tk, tn), lambda i, j, kk: (kk, j))],
        out_specs=pl.BlockSpec((tm, tn), lambda i, j, kk: (i, j)), out_shape=SDS((m, n), BF16),
        scratch_shapes=[pltpu.VMEM((tm, tn), F32)],
        compiler_params=_cp(("arbitrary", "arbitrary", "arbitrary"), 40),
    )(a, b)


def _attn_gate_bwd(dycat, o, u):
    seq = o.shape[0]
    ts = min(ROW_T, seq)

    def body(dy_ref, o_ref, z_ref, dot_ref, dz_ref, dl_ref):
        dy = dy_ref[...].astype(F32)
        ov = o_ref[...].astype(F32)
        z = z_ref[...].astype(F32)
        sg = _sigmoid(z)
        do = dy * (z * sg)
        dz_ref[...] = (dy * ov * _silu_grad(z, sg)).astype(BF16)
        prod = do * ov
        ones = jnp.ones((8, V_HEAD), F32)
        for h in range(N_HEADS):
            cols = slice(h * V_HEAD, (h + 1) * V_HEAD)
            dot_ref[h] = do[:, cols].T.astype(BF16)
            rows = lax.dot_general(ones, prod[:, cols], (((1,), (1,)), ((), ())), precision=lax.Precision.HIGHEST,
                                   preferred_element_type=F32)
            dl_ref[h] = rows[0:1, :]

    blk = pl.BlockSpec((ts, D_ATTN), lambda i: (i, 0))
    return pl.pallas_call(
        body, name="attn_gate_bwd", grid=(seq // ts,),
        in_specs=[pl.BlockSpec((ts, D_ATTN), lambda i: (i, 1)), blk, pl.BlockSpec((ts, D_ATTN), lambda i: (i, U_ZA // D_ATTN))],
        out_specs=[pl.BlockSpec((N_HEADS, V_HEAD, ts), lambda i: (0, 0, i)), blk,
                   pl.BlockSpec((N_HEADS, 1, ts), lambda i: (0, 0, i))],
        out_shape=[SDS((N_HEADS, V_HEAD, seq), BF16), SDS((seq, D_ATTN), BF16), SDS((N_HEADS, 1, seq), F32)],
        compiler_params=_cp(("arbitrary",), 40),
    )(dycat, o, u)


def _flash_bwd(q, k, v, do_t, lse, delta):
    seq = q.shape[0]
    t = min(ATT_T, seq)
    n = seq // t
    n_pairs = n * (n + 1) // 2

    def body(k_ref, v_ref, q_ref, dot_ref, lse_ref, dl_ref, dq_ref, dk_ref, dv_ref, s_a, s_b, dp_a, dp_b, dq_acc, dk_acc,
             dvt_acc):
        bufs = ((s_a, dp_a), (s_b, dp_b))

        def rows(i):
            return pl.ds(pl.multiple_of(i * t, t), t)

        def products(j, i, buf):
            s_ref, dp_ref = buf
            s = _nt(k_ref[rows(j), :], q_ref[rows(i), :])
            ahead = lax.broadcasted_iota(jnp.int32, (t, t), 0) - lax.broadcasted_iota(jnp.int32, (t, t), 1)
            s_ref[...] = jnp.where(ahead <= (i - j) * t, s, -jnp.inf)
            dp_ref[...] = _nn(v_ref[rows(j), :], dot_ref[:, rows(i)])

        def absorb(j, i, buf):
            s_ref, dp_ref = buf
            first = i == j
            p = jnp.exp2((s_ref[...] - lse_ref[:, rows(i)]).astype(BF16))
            dvt = jnp.where(first, 0.0, dvt_acc[...]) + _nt(dot_ref[:, rows(i)], p)
            ds = p * (dp_ref[...] - dl_ref[:, rows(i)]).astype(BF16)
            dk = jnp.where(first, 0.0, dk_acc[...]) + _nn(ds, q_ref[rows(i), :])
            dq_acc[rows(i), :] += _tn(ds, k_ref[rows(j), :])
            dvt_acc[...] = dvt
            dk_acc[...] = dk
            dk_ref[rows(j), :] = dk.astype(BF16)
            dv_ref[rows(j), :] = dvt.T.astype(BF16)

        def trip(width):
            def walk(_, pair):
                j, i = pair
                for w in range(width):
                    done = i == n - 1
                    nj = jnp.where(done, j + 1, j)
                    ni = jnp.where(done, j + 1, i + 1)
                    absorb(j, i, bufs[w % 2])
                    products(jnp.minimum(nj, n - 1), jnp.minimum(ni, n - 1), bufs[(w + 1) % 2])
                    j, i = nj, ni
                return j, i
            return walk

        dq_acc[...] = jnp.zeros_like(dq_acc)
        products(0, 0, bufs[0])
        pair = lax.fori_loop(0, n_pairs // ATT_UNROLL, trip(ATT_UNROLL), (jnp.int32(0), jnp.int32(0)))
        if n_pairs % ATT_UNROLL:
            trip(n_pairs % ATT_UNROLL)(0, pair)

        def finish(i, carry):
            dq_ref[rows(i), :] = dq_acc[rows(i), :].astype(BF16)
            return carry

        lax.fori_loop(0, n, finish, 0)

    def col(width):
        return pl.BlockSpec((seq, width), lambda h: (0, h))

    row = pl.BlockSpec((None, 1, seq), lambda h: (h, 0, 0))
    return pl.pallas_call(
        body, name="flash_bwd", grid=(N_HEADS,),
        in_specs=[col(HEAD_PAD), col(V_HEAD), col(HEAD_PAD), pl.BlockSpec((None, V_HEAD, seq), lambda h: (h, 0, 0)), row, row],
        out_specs=[col(HEAD_PAD), col(HEAD_PAD), col(V_HEAD)],
        out_shape=[SDS((seq, Q_PAD), BF16), SDS((seq, Q_PAD), BF16), SDS((seq, D_ATTN), BF16)],
        scratch_shapes=[pltpu.VMEM((t, t), F32)] * 4
                       + [pltpu.VMEM((seq, HEAD_PAD), F32), pltpu.VMEM((t, HEAD_PAD), F32), pltpu.VMEM((V_HEAD, t), F32)],
        compiler_params=_cp(("arbitrary",), 60),
    )(k, v, q, do_t, lse, delta)


SG_QAG, SG_KVAG, SG_QG, SG_KG, SG_COLS = 0, Q_LORA, Q_LORA + KV_LORA, Q_LORA + KV_LORA + HEAD_PAD, D_MODEL


def _mla_bwd(dq, dk, dv, u, pos, freq, q_a_g, wq, kv_a_g, wkn, wv, q_g, k_g):
    seq = u.shape[0]
    ts = min(ROW_T, seq)
    n = seq // ts
    qscale = 1.0 / math.sqrt(QK_HEAD)

    def body(dq_ref, dk_ref, dv_ref, cq_ref, ckv_ref, kr_ref, pos_ref, freq_ref, qag_ref, wq_ref, kvag_ref, wkn_ref,
             wv_ref, qg_ref, kg_ref, du_ref, dwq_ref, dwkn_ref, dwv_ref, sg_ref, dqp_ref, dkn_ref):
        i = pl.program_id(0)

        @pl.when(i == 0)
        def _():
            dwq_ref[...] = jnp.zeros_like(dwq_ref)
            dwkn_ref[...] = jnp.zeros_like(dwkn_ref)
            dwv_ref[...] = jnp.zeros_like(dwv_ref)
            sg_ref[...] = jnp.zeros_like(sg_ref)

        tables = _rope_tables(pos_ref, freq_ref)

        cq = cq_ref[...].astype(F32)
        cqn, rq = _rms(cq, Q_LORA)
        qag = qag_ref[...]
        cqb = (cqn * qag).astype(BF16)
        qp = _nn(cqb, wq_ref[...])
        qg = qg_ref[...]
        dqg = jnp.zeros((1, HEAD_PAD), F32)
        for h in range(N_HEADS):
            lo = h * HEAD_PAD
            xn, r = _rms(qp[:, lo:lo + HEAD_PAD], QK_HEAD)
            g = jnp.concatenate([dq_ref[:, lo:lo + LANES].astype(F32),
                                 _rope_bwd(dq_ref[:, lo + LANES:lo + HEAD_PAD].astype(F32), tables)], axis=-1) * qscale
            dqg = dqg + jnp.sum(g * xn, axis=0, keepdims=True)
            gy = g * qg
            mean = jnp.sum(gy * xn, axis=-1, keepdims=True) * (1.0 / QK_HEAD)
            dqp_ref[:, lo:lo + HEAD_PAD] = (r * (gy - xn * mean)).astype(BF16)
        dqp = dqp_ref[...]
        dwq_ref[...] += _tn(cqb, dqp)
        dcqn = _nt(dqp, wq_ref[...])
        sg_ref[0:1, SG_QAG:SG_QAG + Q_LORA] += jnp.sum(dcqn * cqn, axis=0, keepdims=True)
        sg_ref[0:1, SG_QG:SG_QG + HEAD_PAD] += dqg
        gy = dcqn * qag
        mean = jnp.sum(gy * cqn, axis=-1, keepdims=True) * (1.0 / Q_LORA)
        du_ref[:, 0:Q_LORA] = (rq * (gy - cqn * mean)).astype(BF16)

        ckv = ckv_ref[...].astype(F32)
        ckvn, rkv = _rms(ckv, KV_LORA)
        kvag = kvag_ref[...]
        ckvb = (ckvn * kvag).astype(BF16)
        kn = _nn(ckvb, wkn_ref[...])
        kr = kr_ref[:, 0:LANES].astype(F32)
        ssr = jnp.sum(kr * kr, axis=-1, keepdims=True)
        kg = kg_ref[...]
        kg_n, kg_r = kg[:, :LANES] * LN2, kg[:, LANES:] * LN2
        dkg_n = jnp.zeros((1, LANES), F32)
        dkg_r = jnp.zeros((1, LANES), F32)
        dkr = jnp.zeros((ts, LANES), F32)
        for h in range(N_HEADS):
            knh = kn[:, h * QK_NOPE:(h + 1) * QK_NOPE]
            r = lax.rsqrt((jnp.sum(knh * knh, axis=-1, keepdims=True) + ssr) * (1.0 / QK_HEAD) + EPS)
            xn_n, xn_r = knh * r, kr * r
            lo = h * HEAD_PAD
            g_n = dk_ref[:, lo:lo + LANES].astype(F32)
            g_r = _rope_bwd(dk_ref[:, lo + LANES:lo + HEAD_PAD].astype(F32), tables)
            dkg_n = dkg_n + jnp.sum(g_n * xn_n, axis=0, keepdims=True)
            dkg_r = dkg_r + jnp.sum(g_r * xn_r, axis=0, keepdims=True)
            gy_n, gy_r = g_n * kg_n, g_r * kg_r
            mean = (jnp.sum(gy_n * xn_n, axis=-1, keepdims=True) + jnp.sum(gy_r * xn_r, axis=-1, keepdims=True)) * (1.0 / QK_HEAD)
            dkn_ref[:, h * QK_NOPE:(h + 1) * QK_NOPE] = (r * (gy_n - xn_n * mean)).astype(BF16)
            dkr = dkr + r * (gy_r - xn_r * mean)
        dkn = dkn_ref[...]
        dvv = dv_ref[...]
        dwkn_ref[...] += _tn(ckvb, dkn)
        dwv_ref[...] += _tn(ckvb, dvv)
        dckvn = _nt(dkn, wkn_ref[...]) + _nt(dvv, wv_ref[...])
        sg_ref[0:1, SG_KVAG:SG_KVAG + KV_LORA] += jnp.sum(dckvn * ckvn, axis=0, keepdims=True)
        sg_ref[0:1, SG_KG:SG_KG + LANES] += dkg_n * LN2
        sg_ref[0:1, SG_KG + LANES:SG_KG + HEAD_PAD] += dkg_r * LN2
        gy = dckvn * kvag
        mean = jnp.sum(gy * ckvn, axis=-1, keepdims=True) * (1.0 / KV_LORA)
        du_ref[:, Q_LORA:Q_LORA + KV_LORA] = (rkv * (gy - ckvn * mean)).astype(BF16)
        du_ref[:, Q_LORA + KV_LORA:Q_LORA + KV_LORA + LANES] = dkr.astype(BF16)
        du_ref[:, Q_LORA + KV_LORA + LANES:MLA_COLS] = jnp.zeros((ts, LANES), BF16)

    def full(a):
        return pl.BlockSpec(a.shape, lambda i: (0,) * a.ndim)

    wide = pl.BlockSpec((ts, Q_PAD), lambda i: (i, 0))
    return pl.pallas_call(
        body, name="mla_bwd", grid=(n,),
        in_specs=[wide, wide, pl.BlockSpec((ts, D_ATTN), lambda i: (i, 0)),
                  pl.BlockSpec((ts, Q_LORA), lambda i: (i, U_CQ // Q_LORA)),
                  pl.BlockSpec((ts, KV_LORA), lambda i: (i, U_CKV // KV_LORA)),
                  pl.BlockSpec((ts, KR_PAD), lambda i: (i, U_KR // KR_PAD)),
                  pl.BlockSpec((ts, 1), lambda i: (i, 0)), full(freq), full(q_a_g), full(wq), full(kv_a_g), full(wkn),
                  full(wv), full(q_g), full(k_g)],
        out_specs=[pl.BlockSpec((ts, MLA_COLS), lambda i: (i, 0)), pl.BlockSpec((Q_LORA, Q_PAD), lambda i: (0, 0)),
                   pl.BlockSpec((KV_LORA, D_ATTN), lambda i: (0, 0)), pl.BlockSpec((KV_LORA, D_ATTN), lambda i: (0, 0)),
                   pl.BlockSpec((8, SG_COLS), lambda i: (0, 0))],
        out_shape=[SDS((seq, MLA_COLS), BF16), SDS((Q_LORA, Q_PAD), F32), SDS((KV_LORA, D_ATTN), F32),
                   SDS((KV_LORA, D_ATTN), F32), SDS((8, SG_COLS), F32)],
        scratch_shapes=[pltpu.VMEM((ts, Q_PAD), BF16), pltpu.VMEM((ts, D_ATTN), BF16)],
        compiler_params=_cp(("arbitrary",), 56),
    )(dq, dk, dv, u, u, u, pos, freq, q_a_g, wq, kv_a_g, wkn, wv, q_g, k_g)


def _conv_bwd(dycat, u, conv_w):
    seq = u.shape[0]
    ts = min(ROW_T, seq)
    n = seq // ts
    hb = ts // HALO

    def body(dy_ref, xc_ref, bc_ref, cc_ref, zc_ref, xp_ref, cp_ref, dyn_ref, bn_ref, zn_ref, w_ref,
             du_ref, dw_ref, ext_ref, dext_ref, acc_ref):
        i = pl.program_id(0)

        @pl.when(i == 0)
        def _():
            dw_ref[...] = jnp.zeros_like(dw_ref)

        up = cp_ref[...].astype(F32) * xp_ref[...].astype(F32)
        ext_ref[0:HALO, :] = jnp.where(i > 0, up, 0.0)
        ext_ref[HALO:HALO + ts, :] = cc_ref[...].astype(F32) * xc_ref[...].astype(F32)
        zn = zn_ref[...].astype(F32)
        dnext = dyn_ref[...].astype(F32) * (zn * _sigmoid(zn)) * bn_ref[...].astype(F32)
        dext_ref[ts:ts + HALO, :] = jnp.where(i < n - 1, dnext, 0.0)
        acc_ref[...] = jnp.zeros_like(acc_ref)

        for r0 in range(0, ts, CHUNK_ROWS):
            rows = slice(r0, r0 + CHUNK_ROWS)
            for c0 in range(0, D_CONV, CHUNK_LANES):
                cols = slice(c0, c0 + CHUNK_LANES)
                uc = ext_ref[HALO + r0:HALO + r0 + CHUNK_ROWS, cols]
                u1 = ext_ref[HALO - 1 + r0:HALO - 1 + r0 + CHUNK_ROWS, cols]
                u2 = ext_ref[HALO - 2 + r0:HALO - 2 + r0 + CHUNK_ROWS, cols]
                conv = w_ref[0:1, cols] * u2 + w_ref[1:2, cols] * u1 + w_ref[2:3, cols] * uc
                z = zc_ref[rows, cols].astype(F32)
                sg = _sigmoid(z)
                sz = z * sg
                b = bc_ref[rows, cols].astype(F32)
                dy = dy_ref[rows, cols].astype(F32)
                du_ref[rows, 3 * D_CONV + c0:3 * D_CONV + c0 + CHUNK_LANES] = (dy * (b * conv) * _silu_grad(z, sg)).astype(BF16)
                du_ref[rows, D_CONV + c0:D_CONV + c0 + CHUNK_LANES] = (dy * sz * conv).astype(BF16)
                dconv = dy * sz * b
                dext_ref[rows, cols] = dconv
                acc_ref[0:CHUNK_ROWS, cols] += dconv * u2
                acc_ref[CHUNK_ROWS:2 * CHUNK_ROWS, cols] += dconv * u1
                acc_ref[2 * CHUNK_ROWS:3 * CHUNK_ROWS, cols] += dconv * uc
        for r0 in range(0, ts, CHUNK_ROWS):
            rows = slice(r0, r0 + CHUNK_ROWS)
            for c0 in range(0, D_CONV, CHUNK_LANES):
                cols = slice(c0, c0 + CHUNK_LANES)
                du = (w_ref[2:3, cols] * dext_ref[rows, cols] + w_ref[1:2, cols] * dext_ref[r0 + 1:r0 + 1 + CHUNK_ROWS, cols]
                      + w_ref[0:1, cols] * dext_ref[r0 + 2:r0 + 2 + CHUNK_ROWS, cols])
                du_ref[rows, 2 * D_CONV + c0:2 * D_CONV + c0 + CHUNK_LANES] = (du * xc_ref[rows, cols].astype(F32)).astype(BF16)
                du_ref[rows, c0:c0 + CHUNK_LANES] = (du * cc_ref[rows, cols].astype(F32)).astype(BF16)
        for k in range(3):
            dw_ref[k:k + 1, :] += jnp.sum(acc_ref[k * CHUNK_ROWS:(k + 1) * CHUNK_ROWS, :], axis=0, keepdims=True)

    def col(cb):
        return pl.BlockSpec((ts, D_CONV), lambda i: (i, cb))

    def prev(cb):
        return pl.BlockSpec((HALO, D_CONV), lambda i: (jnp.maximum(i * hb - 1, 0), cb))

    def nxt(cb):
        return pl.BlockSpec((HALO, D_CONV), lambda i: (jnp.minimum((i + 1) * hb, n * hb - 1), cb))

    return pl.pallas_call(
        body, name="conv_bwd", grid=(n,),
        in_specs=[col(0), col(0), col(1), col(2), col(3), prev(0), prev(2), nxt(0), nxt(1), nxt(3),
                  pl.BlockSpec((3, D_CONV), lambda i: (0, 0))],
        out_specs=[pl.BlockSpec((ts, 4 * D_CONV), lambda i: (i, 0)), pl.BlockSpec((8, D_CONV), lambda i: (0, 0))],
        out_shape=[SDS((seq, 4 * D_CONV), BF16), SDS((8, D_CONV), F32)],
        scratch_shapes=[pltpu.VMEM((ts + HALO, D_CONV), F32), pltpu.VMEM((ts + HALO, D_CONV), F32),
                        pltpu.VMEM((3 * CHUNK_ROWS, D_CONV), F32)],
        compiler_params=_cp(("arbitrary",), 48),
    )(dycat, u, u, u, u, u, u, dycat, u, u, conv_w)


def _inproj_bwd(du_conv, du_za, du_mla, w_t, parts):
    seq = du_conv.shape[0]
    dm = w_t.shape[1]
    tm, tn = min(DH_TM, seq), DH_TN
    ni, nj = seq // tm, dm // tn
    na = len(parts)

    def body(dc_ref, dz_ref, dm_ref, w_ref, *rest):
        part_refs, o_ref, recv_refs = rest[:na], rest[na], rest[na + 1:2 * na + 1]
        ssem, rsem = rest[2 * na + 1:]
        i, j = pl.program_id(0), pl.program_id(1)
        sends, recvs = _chip_exchange_copies(part_refs, recv_refs, ssem, rsem)

        @pl.when((i == 0) & (j == 0))
        def _():
            for cp in sends:
                cp.start()

        acc = _nn(dc_ref[...], w_ref[0:U_ZA, :])
        acc = acc + _nn(dz_ref[...], w_ref[U_ZA:U_CQ, :])
        acc = acc + _nn(dm_ref[...], w_ref[U_CQ:U_COLS, :])
        o_ref[...] = acc

        @pl.when((i == ni - 1) & (j == nj - 1))
        def _():
            for cp in recvs:
                cp.wait_recv()
            for cp in sends:
                cp.wait_send()

    outs = pl.pallas_call(
        body, name="inproj_bwd", grid=(ni, nj),
        in_specs=[pl.BlockSpec((tm, U_ZA), lambda i, j: (i, 0)), pl.BlockSpec((tm, D_ATTN), lambda i, j: (i, 0)),
                  pl.BlockSpec((tm, MLA_COLS), lambda i, j: (i, 0)), pl.BlockSpec((U_COLS, tn), lambda i, j: (0, j))]
                 + [ANY] * na,
        out_specs=[pl.BlockSpec((tm, tn), lambda i, j: (i, j))] + [ANY] * na,
        out_shape=[SDS((seq, dm), F32)] + [SDS(p.shape, p.dtype) for p in parts],
        scratch_shapes=[pltpu.SemaphoreType.DMA((3 * na,))] * 2,
        compiler_params=_cp(("arbitrary", "arbitrary"), 48),
    )(du_conv, du_za, du_mla, w_t, *parts)
    return outs[0], outs[1:]


def _prenorm_bwd(x, dh, dout, norm_g, scale):
    seq, dm = x.shape
    ts = min(ROW_T, seq)
    n = seq // ts

    def body(x_ref, dh_ref, dout_ref, g_ref, sc_ref, gx_ref, st_ref, acc_ref):
        i = pl.program_id(0)

        @pl.when(i == 0)
        def _():
            acc_ref[...] = jnp.zeros_like(acc_ref)

        xv = x_ref[...]
        xn, r = _rms(xv, dm)
        dh_v = dh_ref[...]
        gv = g_ref[...]
        one_sc = 1.0 + sc_ref[...]

        def fold(a):
            return jnp.sum(a.reshape(ts // 8, 8, dm), axis=0)

        acc_ref[0:8, :] += fold(dh_v)
        acc_ref[8:16, :] += fold(dh_v * (xn * gv))
        dxg = dh_v * one_sc
        acc_ref[16:24, :] += fold(dxg * xn)
        dxn = dxg * gv
        mean = jnp.sum(dxn * xn, axis=-1, keepdims=True) * (1.0 / dm)
        gx_ref[...] = dout_ref[...] + r * (dxn - xn * mean)

        @pl.when(i == n - 1)
        def _():
            st_ref[...] = jnp.zeros_like(st_ref)
            for k in range(3):
                st_ref[k:k + 1, :] = jnp.sum(acc_ref[8 * k:8 * k + 8, :], axis=0, keepdims=True)

    row = pl.BlockSpec((ts, dm), lambda i: (i, 0))
    vec = pl.BlockSpec((1, dm), lambda i: (0, 0))
    return pl.pallas_call(
        body, name="prenorm_bwd", grid=(n,), in_specs=[row, row, row, vec, vec],
        out_specs=[row, pl.BlockSpec((8, dm), lambda i: (0, 0))],
        out_shape=[SDS((seq, dm), F32), SDS((8, dm), F32)],
        scratch_shapes=[pltpu.VMEM((24, dm), F32)], input_output_aliases={2: 0},
        compiler_params=_cp(("arbitrary",), 52),
    )(x, dh, dout, norm_g, scale)


def _unshard_cols(g):
    return jnp.transpose(g, (1, 0, 2)).reshape(g.shape[1], -1)


def _shard_cols(w):
    r = w.shape[0]
    return jnp.transpose(w.reshape(r, N_CHIPS, -1), (1, 0, 2))


W_IN_COLS = 4 * D_CONV + Q_LORA + KV_LORA + QK_ROPE + D_ATTN
SHARD_ROWS = W_IN_COLS // N_CHIPS
SHARD_PAD = 1536


def _w_in_pieces():
    c4 = 4 * D_CONV
    groups = [(0, c4, 0), (c4, c4 + Q_LORA, U_CQ), (c4 + Q_LORA, c4 + Q_LORA + KV_LORA, U_CKV),
              (c4 + Q_LORA + KV_LORA, W_IN_COLS - D_ATTN, U_KR), (W_IN_COLS - D_ATTN, W_IN_COLS, U_ZA)]
    pieces = []
    for lo, hi, my in groups:
        for chip in range(N_CHIPS):
            a, b = max(lo, chip * SHARD_ROWS), min(hi, (chip + 1) * SHARD_ROWS)
            if a < b:
                pieces.append((chip, a - chip * SHARD_ROWS, b - a, my + a - lo))
    return pieces


def _w_t_to_my(g):
    w = jnp.zeros((U_COLS, g.shape[2]), g.dtype)
    for chip, row, n, my in _w_in_pieces():
        w = lax.dynamic_update_slice(w, g[chip, row:row + n], (my, 0))
    return w


def _w_t_from_my(g_conv, g_za, g_mla):
    w = jnp.zeros((N_CHIPS, SHARD_PAD, g_conv.shape[1]), g_conv.dtype)
    for chip, row, n, my in _w_in_pieces():
        src, base = (g_conv, 0) if my < U_ZA else (g_za, U_ZA) if my < U_CQ else (g_mla, U_CQ)
        w = lax.dynamic_update_slice(w, src[my - base:my - base + n][None], (chip, row, 0))
    return w


def _heads_pad(w):
    r = w.shape[0]
    w3 = w.reshape(r, N_HEADS, QK_HEAD)
    return jnp.pad(w3, ((0, 0), (0, 0), (0, HEAD_PAD - QK_HEAD))).reshape(r, Q_PAD)


def _heads_unpad(w):
    r = w.shape[0]
    return w.reshape(r, N_HEADS, HEAD_PAD)[:, :, :QK_HEAD].reshape(r, N_HEADS * QK_HEAD)


def kernel(x, c, positions, ada_w, ada_b, norm_g, w_in, conv_w, q_a_g, w_q_b, kv_a_g, w_kv_b, q_g, k_g, w_out, loss_target, m_ada_w, m_ada_b, m_norm_g, m_w_in, m_conv_w, m_q_a_g, m_w_q_b, m_kv_a_g, m_w_kv_b, m_q_g, m_k_g, m_w_out, v_ada_w, v_ada_b, v_norm_g, v_w_in, v_conv_w, v_q_a_g, v_w_q_b, v_kv_a_g, v_w_kv_b, v_q_g, v_k_g, v_w_out):
    mx, my, mc = _place()
    chip = 2 * mx + my
    me = 2 * chip + mc
    seq = x.shape[1]
    x2, t2 = x[0], loss_target[0]
    cw_cols = conv_w.shape[2]

    small = jnp.zeros((8, D_MODEL), F32)
    small = small.at[0].set(c[0])
    small = small.at[1:4, :cw_cols].set(conv_w[0])
    small_all = _gather8(small, "gather_c_conv", False)[0]
    c_all = small_all[:, 0, :]
    conv_full = jnp.transpose(small_all.reshape(N_CHIPS, 2, 8, D_MODEL)[:, 0, 1:4, :cw_cols], (1, 0, 2)).reshape(3, D_CONV)

    ada_cols = ada_w.shape[2]
    b_k = lax.dynamic_slice(ada_b, (0, chip * ada_cols), (1, ada_cols))
    mod_k, sc_all = _ada_mod(c_all, ada_w[0], b_k)
    mod_all = _gather8(mod_k, "gather_mod", False)[0]
    mod_row = lax.dynamic_slice(mod_all.reshape(N_CHIPS, 2, N_DEV, ada_cols), (0, mc, me, 0), (N_CHIPS, 1, 1, ada_cols))
    mod_row = mod_row.reshape(3, D_MODEL)
    shift, scale, gate = mod_row[0:1], mod_row[1:2], mod_row[2:3]

    def own_slot(g, s):
        return lax.dynamic_update_slice(g, s[None], (chip, 0, 0))

    w_in_t, m_w_in_t, v_w_in_t = [jnp.transpose(a[0]) for a in (w_in, m_w_in, v_w_in)]
    shard_in = jnp.pad(w_in_t.astype(BF16), ((0, SHARD_PAD - SHARD_ROWS), (0, 0)))
    h, g_in = _prenorm_gather(x2, norm_g, scale, shift, shard_in)
    g_in = own_slot(g_in, shard_in)
    w_t = _w_t_to_my(g_in)

    later = [w_q_b[0].astype(BF16), w_kv_b[0].astype(BF16), w_out[0].astype(BF16)]
    u, got = _inproj(h, w_t, later)
    g_q, g_kv, g_out = [own_slot(g, s) for g, s in zip(got, later)]
    wq = _heads_pad(_unshard_cols(g_q))
    wkv = _unshard_cols(g_kv).reshape(KV_LORA, N_HEADS, QK_NOPE + V_HEAD)
    wkn = wkv[:, :, :QK_NOPE].reshape(KV_LORA, N_HEADS * QK_NOPE)
    wv = wkv[:, :, QK_NOPE:].reshape(KV_LORA, D_ATTN)
    wo = g_out.reshape(N_CHIPS * g_out.shape[1], D_MODEL)
    y_conv = _conv_fwd(u, conv_full)
    pos = positions.reshape(seq, 1)
    inv_freq = ROPE_BASE ** (-jnp.arange(0, QK_ROPE, 2, dtype=F32) / QK_ROPE)
    freq = jnp.concatenate([inv_freq, inv_freq, jnp.zeros((LANES - QK_ROPE,), F32)]).reshape(1, LANES)
    q_g_pad = jnp.pad(q_g, ((0, 0), (0, HEAD_PAD - QK_HEAD)))
    k_g_pad = jnp.pad(k_g, ((0, 0), (0, HEAD_PAD - QK_HEAD)))
    q, k, v = _mla_prep(u, pos, freq, q_a_g, wq, kv_a_g, wkn, wv, q_g_pad, k_g_pad)
    o, y_attn, lse = _flash_fwd(q, k, v, u)
    dout, dy, dycat, st_out = _outproj_loss(y_conv, y_attn, x2, t2, gate, wo)

    dw_out = jnp.concatenate([_matmul_tn(y_conv, dy, "dw_out_conv"), _matmul_tn(y_attn, dy, "dw_out_attn")], axis=0)
    do_t, du_za, delta = _attn_gate_bwd(dycat, o, u)
    dq, dk, dv = _flash_bwd(q, k, v, do_t, lse, delta)
    du_mla, dwq, dwkn, dwv, sg_mla = _mla_bwd(dq, dk, dv, u, pos, freq, q_a_g, wq, kv_a_g, wkn, wv, q_g_pad, k_g_pad)
    du_conv, dconv_w = _conv_bwd(dycat, u, conv_full)
    dw_conv = _matmul_tn(du_conv, h, "dw_in_conv")
    dw_za = _matmul_tn(du_za, h, "dw_in_za")
    dw_mla = _matmul_tn(du_mla, h, "dw_in_mla")

    dw_q_nat = _heads_unpad(dwq).astype(BF16)
    dw_kv_nat = jnp.concatenate([dwkn.reshape(KV_LORA, N_HEADS, QK_NOPE), dwv.reshape(KV_LORA, N_HEADS, V_HEAD)],
                                axis=2).reshape(KV_LORA, N_HEADS * (QK_NOPE + V_HEAD)).astype(BF16)
    grads = [_w_t_from_my(dw_conv, dw_za, dw_mla), _shard_cols(dw_q_nat), _shard_cols(dw_kv_nat),
             dw_out.reshape(N_CHIPS, dw_out.shape[0] // N_CHIPS, D_MODEL)]
    theirs = _rs_core_swap(grads)
    names = ["w_in", "w_q_b", "w_kv_b", "w_out"]
    core = jnp.reshape(mc, (1,)).astype(jnp.int32)
    parts = [_add_half_bf16(g, b, core, "rs_add_" + nm) for g, b, nm in zip(grads, theirs, names)]
    dh, recv = _inproj_bwd(du_conv, du_za, du_mla, w_t, parts)
    recv = [lax.dynamic_update_slice(r, lax.dynamic_slice(p, (chip, 0, 0), (1,) + p.shape[1:]), (chip, 0, 0))
            for r, p in zip(recv, parts)]
    halves = [_sum_chips(p, "rs_sum_" + nm) for p, nm in zip(recv, names)]
    joined = _rs_core_join(halves)
    joined = [lax.dynamic_update_slice(j, hf[None], (mc, 0, 0)) for j, hf in zip(joined, halves)]
    g_big = [j.reshape(2 * j.shape[1], j.shape[2]) for j in joined]
    grad_x, st_in = _prenorm_bwd(x2, dh, dout, norm_g, scale)

    sgrad = jnp.zeros((8, D_MODEL), F32)
    sgrad = sgrad.at[0:2].set(st_in[0:2])
    sgrad = sgrad.at[2].set(st_out[0])
    sgrad = sgrad.at[3].set(st_in[2])
    sgrad = sgrad.at[4, :D_CONV].set(dconv_w[0]).at[4, D_CONV:].set(dconv_w[1])
    sgrad = sgrad.at[5, :D_CONV].set(dconv_w[2]).at[5, D_CONV:].set(sg_mla[0, :D_CONV])
    sgrad = sgrad.at[6, :HEAD_PAD].set(sg_mla[0, SG_KG:SG_KG + HEAD_PAD])
    sgrad = sgrad.at[7].set(st_out[1])
    sg_all, sg_sum = _gather8(sgrad, "gather_small_grads", True)
    loss = sg_sum[7, 0]
    g_ada_b = sg_sum[0:3].reshape(1, 3 * D_MODEL)
    g_norm_g = sg_sum[3:4]
    conv_sum = jnp.stack([sg_sum[4, :D_CONV], sg_sum[4, D_CONV:], sg_sum[5, :D_CONV]])
    g_conv_w = lax.dynamic_slice(conv_sum, (0, chip * cw_cols), (3, cw_cols))
    g_q_a_g = sg_sum[5:6, D_CONV + SG_QAG:D_CONV + SG_QAG + Q_LORA]
    g_kv_a_g = sg_sum[5:6, D_CONV + SG_KVAG:D_CONV + SG_KVAG + KV_LORA]
    g_q_g = sg_sum[5:6, D_CONV + SG_QG:D_CONV + SG_QG + QK_HEAD]
    g_k_g = sg_sum[6:7, :QK_HEAD]
    dmod_k = lax.dynamic_slice(sg_all[:, 0:3, :].reshape(N_DEV, 3 * D_MODEL), (0, chip * ada_cols), (N_DEV, ada_cols))

    g_ada_w, d_ada_w, nm_ada_w, nv_ada_w = _ada_w_update(sc_all, dmod_k, ada_w[0], m_ada_w[0], v_ada_w[0])
    upd = {}
    big = {"w_q_b": (w_q_b, m_w_q_b, v_w_q_b), "w_kv_b": (w_kv_b, m_w_kv_b, v_w_kv_b), "w_out": (w_out, m_w_out, v_w_out)}
    for nm, g in zip(names[1:], g_big[1:]):
        w_, m_, v_ = big[nm]
        upd[nm] = (g,) + tuple(_adamw(w_[0], g, m_[0], v_[0], "adamw_" + nm))
    d_t, nm_t, nv_t, g_t = _adamw(w_in_t, g_big[0], m_w_in_t, v_w_in_t, "adamw_w_in", echo_g=True)
    upd["w_in"] = tuple(jnp.transpose(a) for a in (g_t, d_t, nm_t, nv_t))
    small_w = {"ada_b": (ada_b, m_ada_b, v_ada_b, g_ada_b), "norm_g": (norm_g, m_norm_g, v_norm_g, g_norm_g),
               "conv_w": (conv_w[0], m_conv_w[0], v_conv_w[0], g_conv_w), "q_a_g": (q_a_g, m_q_a_g, v_q_a_g, g_q_a_g),
               "kv_a_g": (kv_a_g, m_kv_a_g, v_kv_a_g, g_kv_a_g), "q_g": (q_g, m_q_g, v_q_g, g_q_g),
               "k_g": (k_g, m_k_g, v_k_g, g_k_g)}
    for nm, (w_, m_, v_, g) in small_w.items():
        upd[nm] = (g,) + tuple(_adamw(w_, g, m_, v_, "adamw_" + nm))
    upd["ada_w"] = (g_ada_w, d_ada_w, nm_ada_w, nv_ada_w)

    order = ["ada_w", "ada_b", "norm_g", "w_in", "conv_w", "q_a_g", "w_q_b", "kv_a_g", "w_kv_b", "q_g", "k_g", "w_out"]
    lead1 = {"ada_w", "w_in", "conv_w", "w_q_b", "w_kv_b", "w_out"}

    def shaped(nm, a):
        return a[None] if nm in lead1 else a

    outs = [loss, grad_x[None]]
    for idx in range(4):
        outs += [shaped(nm, upd[nm][idx]) for nm in order]
    return tuple(outs)
```

```python
import math

import jax
import jax.numpy as jnp
from jax import lax
from jax.experimental import pallas as pl
from jax.experimental.pallas import tpu as pltpu

F32 = jnp.float32
BF16 = jnp.bfloat16
MESH = pl.DeviceIdType.MESH
SDS = jax.ShapeDtypeStruct
ANY = pl.BlockSpec(memory_space=pl.ANY)

D_MODEL = 2048
D_CONV = 1024
N_HEADS = 8
QK_NOPE = 128
QK_ROPE = 64
QK_HEAD = QK_NOPE + QK_ROPE
V_HEAD = 128
D_ATTN = N_HEADS * V_HEAD
Q_LORA = 512
KV_LORA = 256
ROPE_BASE = 10000.0
EPS = 1e-6
LOG2E = math.log2(math.e)
LN2 = math.log(2.0)
ADAM_LR, ADAM_B1, ADAM_B2, ADAM_EPS, ADAM_WD, ADAM_STEP = 0.001, 0.9, 0.999, 1e-08, 0.01, 10
N_CHIPS = 4
N_DEV = 8

LANES = 128
V7X_VMEM_BYTES = 64 * 1024 * 1024
MIB = 1024 * 1024

HEAD_PAD = 256
Q_PAD = N_HEADS * HEAD_PAD
U_ZA = 4 * D_CONV
U_CQ = U_ZA + D_ATTN
U_CKV = U_CQ + Q_LORA
U_KR = U_CKV + KV_LORA
KR_PAD = 256
U_COLS = U_KR + KR_PAD
MLA_COLS = Q_LORA + KV_LORA + KR_PAD

ATT_T = 512
INPROJ_TM, INPROJ_TN = 1024, 1024
ROW_T = 512
MLA_PREP_T = 1024
OUT_T = 256
DH_TM, DH_TN = 512, 1024
TN_TM, TN_TN, TN_TK = 1024, 1024, 2048
ATT_UNROLL = 8


def _cp(sem=None, vmem_mib=None, **kw):
    if sem is not None:
        kw["dimension_semantics"] = sem
    if vmem_mib is not None:
        kw["vmem_limit_bytes"] = min(vmem_mib * MIB, V7X_VMEM_BYTES - 4 * MIB)
    return pltpu.CompilerParams(**kw)


def _sigmoid(z):
    return 1.0 / (1.0 + jnp.exp(-z))


def _silu_grad(z, sg):
    return sg * (1.0 + z * (1.0 - sg))


def _nt(a, b):
    return lax.dot_general(a, b, (((1,), (1,)), ((), ())), preferred_element_type=F32)


def _tn(a, b):
    return lax.dot_general(a, b, (((0,), (0,)), ((), ())), preferred_element_type=F32)


def _nn(a, b):
    return jnp.dot(a, b, preferred_element_type=F32)


def _place():
    return lax.axis_index("x"), lax.axis_index("y"), lax.axis_index("c")


def _gather8(v, name, with_sum):
    rows, cols = v.shape

    def body(v_ref, out_ref, *rest):
        if with_sum:
            sum_ref, send_sems, recv_sems = rest
        else:
            send_sems, recv_sems = rest
        mx, my, mc = _place()
        me = 4 * mx + 2 * my + mc
        out_ref[me] = v_ref[...]
        peers = []
        for d in range(1, N_DEV):
            px = 1 - mx if d & 4 else mx
            py = 1 - my if d & 2 else my
            pc = 1 - mc if d & 1 else mc
            peers.append((px, py, pc))

        def copy(d, slot, to):
            return pltpu.make_async_remote_copy(
                src_ref=v_ref, dst_ref=out_ref.at[slot], send_sem=send_sems.at[d], recv_sem=recv_sems.at[d],
                device_id=to, device_id_type=MESH)

        sends = [copy(d, me, p) for d, p in enumerate(peers)]
        for cp in sends:
            cp.start()
        for d, (px, py, pc) in enumerate(peers):
            copy(d, 4 * px + 2 * py + pc, (px, py, pc)).wait_recv()
        for cp in sends:
            cp.wait_send()
        if with_sum:
            acc = out_ref[0]
            for b in range(1, N_DEV):
                acc = acc + out_ref[b]
            sum_ref[...] = acc

    out_shape = [SDS((N_DEV, rows, cols), F32)]
    if with_sum:
        out_shape.append(SDS((rows, cols), F32))
    vm = pl.BlockSpec(memory_space=pltpu.VMEM)
    return pl.pallas_call(
        body, name=name, out_shape=out_shape, in_specs=[vm], out_specs=[vm] * len(out_shape),
        scratch_shapes=[pltpu.SemaphoreType.DMA((N_DEV - 1,)), pltpu.SemaphoreType.DMA((N_DEV - 1,))],
    )(v)


def _chips_of(mx, my):
    chips = [(mx, 1 - my), (1 - mx, my), (1 - mx, 1 - my)]
    return chips, [2 * px + py for px, py in chips]


def _prenorm_gather(x, norm_g, scale, shift, shard):
    seq, dm = x.shape
    tm = min(INPROJ_TM, seq)
    ni = seq // tm
    half_rows = shard.shape[0] // 2

    def body(x_ref, g_ref, sc_ref, sh_ref, shard_ref, h_ref, got_ref, s1, r1, s2, r2):
        i = pl.program_id(0)
        mx, my, mc = _place()
        k = 2 * mx + my
        sib = (mx, my, 1 - mc)
        chips, kks = _chips_of(mx, my)

        def half(slot, c):
            return got_ref.at[slot, pl.ds(c * half_rows, half_rows)]

        def over_ici(d, slot):
            return pltpu.make_async_remote_copy(
                src_ref=shard_ref.at[pl.ds(mc * half_rows, half_rows)], dst_ref=half(slot, mc), send_sem=s1.at[d],
                recv_sem=r1.at[d], device_id=(chips[d][0], chips[d][1], mc), device_id_type=MESH)

        def to_sibling(d, c):
            return pltpu.make_async_remote_copy(
                src_ref=half(kks[d], c), dst_ref=half(kks[d], c), send_sem=s2.at[d], recv_sem=r2.at[d],
                device_id=sib, device_id_type=MESH)

        @pl.when(i == 0)
        def _():
            for d in range(3):
                over_ici(d, k).start()

        xv = x_ref[...]
        r = lax.rsqrt(jnp.mean(xv * xv, axis=-1, keepdims=True) + EPS)
        h_ref[...] = ((xv * r * g_ref[...]) * (1.0 + sc_ref[...]) + sh_ref[...]).astype(BF16)

        @pl.when(i == ni - 1)
        def _():
            for d in range(3):
                over_ici(d, kks[d]).wait_recv()
                to_sibling(d, mc).start()
            for d in range(3):
                to_sibling(d, 1 - mc).wait_recv()
            for d in range(3):
                over_ici(d, k).wait_send()
                to_sibling(d, mc).wait_send()

    vec = pl.BlockSpec((1, dm), lambda i: (0, 0))
    row = pl.BlockSpec((tm, dm), lambda i: (i, 0))
    return pl.pallas_call(
        body, name="prenorm_gather", grid=(ni,), in_specs=[row, vec, vec, vec, ANY], out_specs=[row, ANY],
        out_shape=[SDS((seq, dm), BF16), SDS((N_CHIPS,) + shard.shape, shard.dtype)],
        scratch_shapes=[pltpu.SemaphoreType.DMA((3,))] * 4,
        compiler_params=_cp(("arbitrary",), 48),
    )(x, norm_g, scale, shift, shard)


def _rs_core_swap(grads):
    na = len(grads)
    halves = [g.shape[1] // 2 for g in grads]

    def body(*refs):
        ins, outs = refs[:na], refs[na:2 * na]
        ssem, rsem = refs[2 * na:]
        mx, my, mc = _place()
        sib = (mx, my, 1 - mc)
        sends = []
        for a in range(na):
            cp = pltpu.make_async_remote_copy(
                src_ref=ins[a].at[:, pl.ds((1 - mc) * halves[a], halves[a])], dst_ref=outs[a],
                send_sem=ssem.at[a], recv_sem=rsem.at[a], device_id=sib, device_id_type=MESH)
            cp.start()
            sends.append(cp)
        for cp in sends:
            cp.wait_recv()
        for cp in sends:
            cp.wait_send()

    return pl.pallas_call(
        body, name="rs_core_swap", out_shape=[SDS((N_CHIPS, h) + g.shape[2:], g.dtype) for g, h in zip(grads, halves)],
        in_specs=[ANY] * na, out_specs=[ANY] * na,
        scratch_shapes=[pltpu.SemaphoreType.DMA((na,))] * 2,
    )(*grads)


def _chip_exchange_copies(ins, outs, ssem, rsem):
    mx, my, mc = _place()
    k = 2 * mx + my
    chips, kks = _chips_of(mx, my)
    sends, recvs = [], []
    for a in range(len(ins)):
        for d, (px, py) in enumerate(chips):
            def copy(dst_slot):
                return pltpu.make_async_remote_copy(
                    src_ref=ins[a].at[kks[d]], dst_ref=outs[a].at[dst_slot], send_sem=ssem.at[3 * a + d],
                    recv_sem=rsem.at[3 * a + d], device_id=(px, py, mc), device_id_type=MESH)
            sends.append(copy(k))
            recvs.append(copy(kks[d]))
    return sends, recvs


def _rs_core_join(halves):
    na = len(halves)

    def body(*refs):
        ins, outs = refs[:na], refs[na:2 * na]
        ssem, rsem = refs[2 * na:]
        mx, my, mc = _place()
        sib = (mx, my, 1 - mc)
        sends = []
        for a in range(na):
            cp = pltpu.make_async_remote_copy(
                src_ref=ins[a], dst_ref=outs[a].at[mc], send_sem=ssem.at[a], recv_sem=rsem.at[a],
                device_id=sib, device_id_type=MESH)
            cp.start()
            sends.append(cp)
        for a in range(na):
            pltpu.make_async_remote_copy(
                src_ref=ins[a], dst_ref=outs[a].at[1 - mc], send_sem=ssem.at[a], recv_sem=rsem.at[a],
                device_id=sib, device_id_type=MESH).wait_recv()
        for cp in sends:
            cp.wait_send()

    return pl.pallas_call(
        body, name="rs_core_join", out_shape=[SDS((2,) + h.shape, h.dtype) for h in halves],
        in_specs=[ANY] * na, out_specs=[ANY] * na,
        scratch_shapes=[pltpu.SemaphoreType.DMA((na,))] * 2,
    )(*halves)


def _row_tile(rows, limit, mult=16):
    if rows <= limit:
        return rows
    best = None
    for t in range(mult, limit + 1, mult):
        if rows % t == 0:
            best = t
    assert best is not None, rows
    return best


def _add_half_bf16(g, b, core, name):
    _, h, cols = b.shape
    tb = _row_tile(h, 512)
    nb = h // tb

    def body(core_ref, g_ref, b_ref, o_ref):
        o_ref[...] = (g_ref[...].astype(F32) + b_ref[...].astype(F32)).astype(BF16)

    spec = pl.BlockSpec((None, tb, cols), lambda kk, i, core_ref: (kk, i, 0))
    return pl.pallas_call(
        body, name=name,
        grid_spec=pltpu.PrefetchScalarGridSpec(
            num_scalar_prefetch=1, grid=(N_CHIPS, nb),
            in_specs=[pl.BlockSpec((None, tb, cols), lambda kk, i, core_ref: (kk, core_ref[0] * nb + i, 0)), spec],
            out_specs=spec),
        out_shape=SDS(b.shape, BF16), compiler_params=_cp(("arbitrary", "arbitrary")),
    )(core, g, b)


def _sum_chips(p, name):
    _, rows, cols = p.shape
    tb = _row_tile(rows, 256)

    def body(p_ref, o_ref):
        acc = p_ref[0].astype(F32)
        for j in range(1, N_CHIPS):
            acc = acc + p_ref[j].astype(F32)
        o_ref[...] = acc

    return pl.pallas_call(
        body, name=name, grid=(rows // tb,),
        in_specs=[pl.BlockSpec((N_CHIPS, tb, cols), lambda i: (0, i, 0))],
        out_specs=pl.BlockSpec((tb, cols), lambda i: (i, 0)), out_shape=SDS((rows, cols), F32),
        compiler_params=_cp(("arbitrary",)),
    )(p)


def _adamw_math(w, g, m, v):
    m2 = ADAM_B1 * m + (1.0 - ADAM_B1) * g
    v2 = ADAM_B2 * v + (1.0 - ADAM_B2) * (g * g)
    m_hat = m2 / (1.0 - ADAM_B1 ** ADAM_STEP)
    v_hat = v2 / (1.0 - ADAM_B2 ** ADAM_STEP)
    delta = -ADAM_LR * (m_hat / (jnp.sqrt(v_hat) + ADAM_EPS) + ADAM_WD * w)
    return delta, m2, v2


def _adamw(w, g, m, v, name, echo_g=False):
    rows, cols = w.shape
    tb = _row_tile(rows, 256, mult=8)
    nout = 4 if echo_g else 3

    def body(w_ref, g_ref, m_ref, v_ref, d_ref, m2_ref, v2_ref, *echo):
        gv = g_ref[...]
        d, m2, v2 = _adamw_math(w_ref[...], gv, m_ref[...], v_ref[...])
        d_ref[...] = d
        m2_ref[...] = m2
        v2_ref[...] = v2
        if echo_g:
            echo[0][...] = gv

    spec = pl.BlockSpec((tb, cols), lambda i: (i, 0))
    return pl.pallas_call(
        body, name=name, grid=(rows // tb,), in_specs=[spec] * 4, out_specs=[spec] * nout,
        out_shape=[SDS((rows, cols), F32)] * nout, compiler_params=_cp(("arbitrary",), 40),
    )(w, g, m, v)


def _ada_w_update(sc_all, dmod_k, w, m, v):
    rows, cols = w.shape
    tb = 256

    def body(s_ref, dm_ref, w_ref, m_ref, v_ref, g_ref, d_ref, m2_ref, v2_ref):
        g = _tn(s_ref[...].astype(BF16), dm_ref[...].astype(BF16))
        d, m2, v2 = _adamw_math(w_ref[...], g, m_ref[...], v_ref[...])
        g_ref[...] = g
        d_ref[...] = d
        m2_ref[...] = m2
        v2_ref[...] = v2

    spec = pl.BlockSpec((tb, cols), lambda i: (i, 0))
    return pl.pallas_call(
        body, name="ada_w_update", grid=(rows // tb,),
        in_specs=[pl.BlockSpec((N_DEV, tb), lambda i: (0, i)), pl.BlockSpec((N_DEV, cols), lambda i: (0, 0)), spec, spec, spec],
        out_specs=[spec] * 4, out_shape=[SDS((rows, cols), F32)] * 4, compiler_params=_cp(("arbitrary",), 40),
    )(sc_all, dmod_k, w, m, v)


def _ada_mod(c_all, w, b_k):
    rows, cols = w.shape
    tn = 512

    def body(c_ref, w_ref, b_ref, o_ref, s_ref):
        cv = c_ref[...]
        s = cv * _sigmoid(cv)
        s_ref[...] = s
        o_ref[...] = _nn(s.astype(BF16), w_ref[...].astype(BF16)) + b_ref[...]

    return pl.pallas_call(
        body, name="ada_mod", grid=(cols // tn,),
        in_specs=[pl.BlockSpec((N_DEV, rows), lambda j: (0, 0)), pl.BlockSpec((rows, tn), lambda j: (0, j)),
                  pl.BlockSpec((1, tn), lambda j: (0, j))],
        out_specs=[pl.BlockSpec((N_DEV, tn), lambda j: (0, j)), pl.BlockSpec((N_DEV, rows), lambda j: (0, 0))],
        out_shape=[SDS((N_DEV, cols), F32), SDS((N_DEV, rows), F32)], compiler_params=_cp(("arbitrary",)),
    )(c_all, w, b_k)


def _inproj(h, w_t, shards):
    seq, dm = h.shape
    ncols = w_t.shape[0]
    tm, tn = min(INPROJ_TM, seq), INPROJ_TN
    ni, nj = seq // tm, ncols // tn
    na = len(shards)

    def body(h_ref, w_ref, *rest):
        shard_refs, u_ref, got_refs = rest[:na], rest[na], rest[na + 1:2 * na + 1]
        ssem, rsem = rest[2 * na + 1:]
        i, j = pl.program_id(0), pl.program_id(1)
        mx, my, mc = _place()
        k = 2 * mx + my
        chips, kks = _chips_of(mx, my)

        def copy(a, d, slot):
            return pltpu.make_async_remote_copy(
                src_ref=shard_refs[a], dst_ref=got_refs[a].at[slot], send_sem=ssem.at[3 * a + d],
                recv_sem=rsem.at[3 * a + d], device_id=(chips[d][0], chips[d][1], mc), device_id_type=MESH)

        @pl.when((i == 0) & (j == 0))
        def _():
            for a in range(na):
                for d in range(3):
                    copy(a, d, k).start()

        u_ref[...] = _nt(h_ref[...], w_ref[...]).astype(BF16)

        @pl.when((i == ni - 1) & (j == nj - 1))
        def _():
            for a in range(na):
                for d in range(3):
                    copy(a, d, kks[d]).wait_recv()
            for a in range(na):
                for d in range(3):
                    copy(a, d, k).wait_send()

    outs = pl.pallas_call(
        body, name="inproj", grid=(ni, nj),
        in_specs=[pl.BlockSpec((tm, dm), lambda i, j: (i, 0)), pl.BlockSpec((tn, dm), lambda i, j: (j, 0))] + [ANY] * na,
        out_specs=[pl.BlockSpec((tm, tn), lambda i, j: (i, j))] + [ANY] * na,
        out_shape=[SDS((seq, ncols), BF16)] + [SDS((N_CHIPS,) + s.shape, s.dtype) for s in shards],
        scratch_shapes=[pltpu.SemaphoreType.DMA((3 * na,))] * 2,
        compiler_params=_cp(("arbitrary", "arbitrary"), 48),
    )(h, w_t, *shards)
    return outs[0], outs[1:]


HALO = 16
CHUNK_ROWS, CHUNK_LANES = 32, 256


def _conv_fwd(u, conv_w):
    seq = u.shape[0]
    ts = min(ROW_T, seq)
    hb = ts // HALO

    def body(xc_ref, bc_ref, cc_ref, zc_ref, xp_ref, cp_ref, w_ref, y_ref, ext_ref):
        i = pl.program_id(0)
        up = cp_ref[...].astype(F32) * xp_ref[...].astype(F32)
        ext_ref[0:HALO, :] = jnp.where(i > 0, up, 0.0)
        ext_ref[HALO:HALO + ts, :] = cc_ref[...].astype(F32) * xc_ref[...].astype(F32)
        for r0 in range(0, ts, CHUNK_ROWS):
            rows = slice(r0, r0 + CHUNK_ROWS)
            for c0 in range(0, D_CONV, CHUNK_LANES):
                cols = slice(c0, c0 + CHUNK_LANES)
                uc = ext_ref[HALO + r0:HALO + r0 + CHUNK_ROWS, cols]
                u1 = ext_ref[HALO - 1 + r0:HALO - 1 + r0 + CHUNK_ROWS, cols]
                u2 = ext_ref[HALO - 2 + r0:HALO - 2 + r0 + CHUNK_ROWS, cols]
                conv = w_ref[0:1, cols] * u2 + w_ref[1:2, cols] * u1 + w_ref[2:3, cols] * uc
                z = zc_ref[rows, cols].astype(F32)
                y_ref[rows, cols] = ((bc_ref[rows, cols].astype(F32) * conv) * (z * _sigmoid(z))).astype(BF16)

    def col(cb):
        return pl.BlockSpec((ts, D_CONV), lambda i: (i, cb))

    def prev(cb):
        return pl.BlockSpec((HALO, D_CONV), lambda i: (jnp.maximum(i * hb - 1, 0), cb))

    return pl.pallas_call(
        body, name="conv_fwd", grid=(seq // ts,),
        in_specs=[col(0), col(1), col(2), col(3), prev(0), prev(2), pl.BlockSpec((3, D_CONV), lambda i: (0, 0))],
        out_specs=pl.BlockSpec((ts, D_CONV), lambda i: (i, 0)), out_shape=SDS((seq, D_CONV), BF16),
        scratch_shapes=[pltpu.VMEM((ts + HALO, D_CONV), F32)],
        compiler_params=_cp(("arbitrary",), 40),
    )(u, u, u, u, u, u, conv_w)


def _rope_tables(pos_ref, freq_ref):
    ang = pos_ref[...].astype(F32) * freq_ref[...]
    lane = lax.broadcasted_iota(jnp.int32, ang.shape, 1)
    cs, sn = jnp.cos(ang), jnp.sin(ang)
    half = QK_ROPE // 2
    cos_t = jnp.where(lane < QK_ROPE, cs, 0.0)
    sin_lo = jnp.where(lane < half, sn, 0.0)
    sin_hi = jnp.where((lane >= half) & (lane < QK_ROPE), sn, 0.0)
    return cos_t, sin_lo, sin_hi


def _rope(blk, tables):
    cos_t, sin_lo, sin_hi = tables
    half = QK_ROPE // 2
    return blk * cos_t - pltpu.roll(blk, LANES - half, 1) * sin_lo + pltpu.roll(blk, half, 1) * sin_hi


def _rope_bwd(g, tables):
    cos_t, sin_lo, sin_hi = tables
    half = QK_ROPE // 2
    return g * cos_t + pltpu.roll(g, LANES - half, 1) * sin_lo - pltpu.roll(g, half, 1) * sin_hi


def _rms(v, n):
    r = lax.rsqrt(jnp.sum(v * v, axis=-1, keepdims=True) * (1.0 / n) + EPS)
    return v * r, r


def _mla_prep(u, pos, freq, q_a_g, wq, kv_a_g, wkn, wv, q_g, k_g):
    seq = u.shape[0]
    ts = min(MLA_PREP_T, seq)
    qscale = LOG2E / math.sqrt(QK_HEAD)

    def body(cq_ref, ckv_ref, kr_ref, pos_ref, freq_ref, qag_ref, wq_ref, kvag_ref, wkn_ref, wv_ref, qg_ref, kg_ref,
             q_ref, k_ref, v_ref):
        tables = _rope_tables(pos_ref, freq_ref)
        cqn, _ = _rms(cq_ref[...].astype(F32), Q_LORA)
        qp = _nn((cqn * qag_ref[...]).astype(BF16), wq_ref[...])
        qg = qg_ref[...]
        for h in range(N_HEADS):
            lo = h * HEAD_PAD
            qn, _ = _rms(qp[:, lo:lo + HEAD_PAD], QK_HEAD)
            qn = qn * qg
            q_ref[:, lo:lo + LANES] = (qn[:, :LANES] * qscale).astype(BF16)
            q_ref[:, lo + LANES:lo + HEAD_PAD] = (_rope(qn[:, LANES:], tables) * qscale).astype(BF16)
        ckvn, _ = _rms(ckv_ref[...].astype(F32), KV_LORA)
        ckvb = (ckvn * kvag_ref[...]).astype(BF16)
        kn = _nn(ckvb, wkn_ref[...])
        v_ref[...] = _nn(ckvb, wv_ref[...]).astype(BF16)
        kr = kr_ref[:, 0:LANES].astype(F32)
        ssr = jnp.sum(kr * kr, axis=-1, keepdims=True)
        kg = kg_ref[...]
        for h in range(N_HEADS):
            knh = kn[:, h * QK_NOPE:(h + 1) * QK_NOPE]
            r = lax.rsqrt((jnp.sum(knh * knh, axis=-1, keepdims=True) + ssr) * (1.0 / QK_HEAD) + EPS)
            lo = h * HEAD_PAD
            k_ref[:, lo:lo + LANES] = (knh * r * kg[:, :LANES]).astype(BF16)
            k_ref[:, lo + LANES:lo + HEAD_PAD] = _rope(kr * r * kg[:, LANES:], tables).astype(BF16)

    def full(a):
        return pl.BlockSpec(a.shape, lambda i: (0,) * a.ndim)

    return pl.pallas_call(
        body, name="mla_prep", grid=(seq // ts,),
        in_specs=[pl.BlockSpec((ts, Q_LORA), lambda i: (i, U_CQ // Q_LORA)),
                  pl.BlockSpec((ts, KV_LORA), lambda i: (i, U_CKV // KV_LORA)),
                  pl.BlockSpec((ts, KR_PAD), lambda i: (i, U_KR // KR_PAD)),
                  pl.BlockSpec((ts, 1), lambda i: (i, 0)), full(freq), full(q_a_g), full(wq), full(kv_a_g), full(wkn),
                  full(wv), full(q_g), full(k_g)],
        out_specs=[pl.BlockSpec((ts, Q_PAD), lambda i: (i, 0)), pl.BlockSpec((ts, Q_PAD), lambda i: (i, 0)),
                   pl.BlockSpec((ts, D_ATTN), lambda i: (i, 0))],
        out_shape=[SDS((seq, Q_PAD), BF16), SDS((seq, Q_PAD), BF16), SDS((seq, D_ATTN), BF16)],
        compiler_params=_cp(("arbitrary",), 48),
    )(u, u, u, pos, freq, q_a_g, wq, kv_a_g, wkn, wv, q_g, k_g)


def _flash_fwd(q, k, v, u):
    seq = q.shape[0]
    t = min(ATT_T, seq)
    n = seq // t
    n_pairs = n * (n + 1) // 2
    za_blk = U_ZA // V_HEAD

    def body(q_ref, k_ref, v_ref, z_ref, o_ref, y_ref, lse_ref, s_a, s_b, top_a, top_b, m_all, l_all, acc_all):
        ones = jnp.ones((16, t), BF16)
        bufs = ((s_a, top_a), (s_b, top_b))

        def rows(i):
            return pl.ds(pl.multiple_of(i * t, t), t)

        def scores(i, j, buf):
            s_ref, top_ref = buf
            s = _nt(k_ref[rows(j), :], q_ref[rows(i), :])
            ahead = lax.broadcasted_iota(jnp.int32, (t, t), 0) - lax.broadcasted_iota(jnp.int32, (t, t), 1)
            s = jnp.where(ahead <= (i - j) * t, s, -jnp.inf)
            s_ref[...] = s
            top_ref[...] = jnp.max(s, axis=0, keepdims=True)

        def absorb(i, j, buf):
            s_ref, top_ref = buf
            first = j == 0
            m = jnp.where(first, -jnp.inf, m_all[i])
            l = jnp.where(first, 0.0, l_all[i])
            acc = jnp.where(first, 0.0, acc_all[i])
            m_new = jnp.maximum(m, top_ref[...])
            alpha = jnp.exp2(m - m_new)
            p = jnp.exp2((s_ref[...] - m_new).astype(BF16))
            m_all[i] = m_new
            l_all[i] = alpha * l + _nn(ones, p)[0:1, :]
            acc_all[i] = alpha * acc + _tn(v_ref[rows(j), :], p)

        def trip(width):
            def walk(_, pair):
                i, j = pair
                for w in range(width):
                    done = j == i
                    ni, nj = jnp.where(done, i + 1, i), jnp.where(done, 0, j + 1)
                    scores(jnp.minimum(ni, n - 1), nj, bufs[(w + 1) % 2])
                    absorb(i, j, bufs[w % 2])
                    i, j = ni, nj
                return i, j
            return walk

        scores(0, 0, bufs[0])
        pair = lax.fori_loop(0, n_pairs // ATT_UNROLL, trip(ATT_UNROLL), (jnp.int32(0), jnp.int32(0)))
        if n_pairs % ATT_UNROLL:
            trip(n_pairs % ATT_UNROLL)(0, pair)

        def finish(i, carry):
            l = l_all[i]
            o = (acc_all[i] * (1.0 / l)).T
            lse_ref[:, rows(i)] = m_all[i] + jnp.log2(l)
            o_ref[rows(i), :] = o.astype(BF16)
            z = z_ref[rows(i), :].astype(F32)
            y_ref[rows(i), :] = (o * (z * _sigmoid(z))).astype(BF16)
            return carry

        lax.fori_loop(0, n, finish, 0)

    def col(width, cb):
        return pl.BlockSpec((seq, width), lambda h: (0, cb + h))

    return pl.pallas_call(
        body, name="flash_fwd", grid=(N_HEADS,),
        in_specs=[col(HEAD_PAD, 0), col(HEAD_PAD, 0), col(V_HEAD, 0), col(V_HEAD, za_blk)],
        out_specs=[col(V_HEAD, 0), col(V_HEAD, 0), pl.BlockSpec((None, 1, seq), lambda h: (h, 0, 0))],
        out_shape=[SDS((seq, D_ATTN), BF16), SDS((seq, D_ATTN), BF16), SDS((N_HEADS, 1, seq), F32)],
        scratch_shapes=[pltpu.VMEM((t, t), F32)] * 2 + [pltpu.VMEM((1, t), F32)] * 2
                       + [pltpu.VMEM((n, 1, t), F32)] * 2 + [pltpu.VMEM((n, V_HEAD, t), F32)],
        compiler_params=_cp(("arbitrary",), 52),
    )(q, k, v, u)


def _outproj_loss(y_conv, y_attn, x, target, gate, w_out):
    seq, dm = x.shape
    ts = min(OUT_T, seq)
    n = seq // ts
    dmix = w_out.shape[0]

    def body(yc_ref, ya_ref, x_ref, t_ref, gate_ref, wo_hbm, dout_ref, dy_ref, dyc_ref, stats_ref, wo_ref, sem, acc_ref):
        i = pl.program_id(0)

        @pl.when(i == 0)
        def _():
            cp = pltpu.make_async_copy(wo_hbm, wo_ref, sem)
            cp.start()
            cp.wait()
            acc_ref[...] = jnp.zeros_like(acc_ref)

        y = _nn(yc_ref[...], wo_ref[0:D_CONV, :]) + _nn(ya_ref[...], wo_ref[D_CONV:dmix, :])
        gate_v = gate_ref[...]
        diff = (x_ref[...] + gate_v * y) - t_ref[...]
        dout = diff * (1.0 / dm)
        dout_ref[...] = dout
        acc_ref[0:8, :] += jnp.sum((dout * y).reshape(ts // 8, 8, dm), axis=0)
        acc_ref[8:16, :] += jnp.sum((diff * diff).reshape(ts // 8, 8, dm), axis=0)
        dy = (dout * gate_v).astype(BF16)
        dy_ref[...] = dy
        dyc_ref[...] = _nt(dy, wo_ref[...]).astype(BF16)

        @pl.when(i == n - 1)
        def _():
            stats_ref[...] = jnp.zeros_like(stats_ref)
            stats_ref[0:1, :] = jnp.sum(acc_ref[0:8, :], axis=0, keepdims=True)
            loss = jnp.sum(acc_ref[8:16, :]) * (0.5 / dm)
            stats_ref[1:2, :] = jnp.full((1, dm), loss, F32)

    row = pl.BlockSpec((ts, dm), lambda i: (i, 0))
    half = pl.BlockSpec((ts, D_CONV), lambda i: (i, 0))
    return pl.pallas_call(
        body, name="outproj_loss", grid=(n,),
        in_specs=[half, half, row, row, pl.BlockSpec((1, dm), lambda i: (0, 0)), ANY],
        out_specs=[row, row, pl.BlockSpec((ts, dmix), lambda i: (i, 0)), pl.BlockSpec((8, dm), lambda i: (0, 0))],
        out_shape=[SDS((seq, dm), F32), SDS((seq, dm), BF16), SDS((seq, dmix), BF16), SDS((8, dm), F32)],
        scratch_shapes=[pltpu.VMEM(w_out.shape, BF16), pltpu.SemaphoreType.DMA(()), pltpu.VMEM((16, dm), F32)],
        compiler_params=_cp(("arbitrary",), 52),
    )(y_conv, y_attn, x, target, gate, w_out)


def _matmul_tn(a, b, name):
    seq, m = a.shape
    n = b.shape[1]
    tm, tn, tk = min(TN_TM, m), min(TN_TN, n), min(TN_TK, seq)
    nk = seq // tk

    def body(a_ref, b_ref, o_ref, acc_ref):
        kk = pl.program_id(2)

        @pl.when(kk == 0)
        def _():
            acc_ref[...] = jnp.zeros_like(acc_ref)

        acc_ref[...] += _tn(a_ref[...], b_ref[...])

        @pl.when(kk == nk - 1)
        def _():
            o_ref[...] = acc_ref[...].astype(BF16)

    return pl.pallas_call(
        body, name=name, grid=(m // tm, n // tn, nk),
        in_specs=[pl.BlockSpec((tk, tm), lambda i, j, kk: (kk, i)), pl.BlockSpec((tk, tn), lambda i, j, kk: (kk, j))],
        out_specs=pl.BlockSpec((tm, tn), lambda i, j, kk: (i, j)), out_shape=SDS((m, n), BF16),
        scratch_shapes=[pltpu.VMEM((tm, tn), F32)],
        compiler_params=_cp(("arbitrary", "arbitrary", "arbitrary"), 40),
    )(a, b)


def _attn_gate_bwd(dycat, o, u):
    seq = o.shape[0]
    ts = min(ROW_T, seq)

    def body(dy_ref, o_ref, z_ref, dot_ref, dz_ref, dl_ref):
        dy = dy_ref[...].astype(F32)
        ov = o_ref[...].astype(F32)
        z = z_ref[...].astype(F32)
        sg = _sigmoid(z)
        do = dy * (z * sg)
        dz_ref[...] = (dy * ov * _silu_grad(z, sg)).astype(BF16)
        prod = do * ov
        ones = jnp.ones((8, V_HEAD), F32)
        for h in range(N_HEADS):
            cols = slice(h * V_HEAD, (h + 1) * V_HEAD)
            dot_ref[h] = do[:, cols].T.astype(BF16)
            rows = lax.dot_general(ones, prod[:, cols], (((1,), (1,)), ((), ())), precision=lax.Precision.HIGHEST,
                                   preferred_element_type=F32)
            dl_ref[h] = rows[0:1, :]

    blk = pl.BlockSpec((ts, D_ATTN), lambda i: (i, 0))
    return pl.pallas_call(
        body, name="attn_gate_bwd", grid=(seq // ts,),
        in_specs=[pl.BlockSpec((ts, D_ATTN), lambda i: (i, 1)), blk, pl.BlockSpec((ts, D_ATTN), lambda i: (i, U_ZA // D_ATTN))],
        out_specs=[pl.BlockSpec((N_HEADS, V_HEAD, ts), lambda i: (0, 0, i)), blk,
                   pl.BlockSpec((N_HEADS, 1, ts), lambda i: (0, 0, i))],
        out_shape=[SDS((N_HEADS, V_HEAD, seq), BF16), SDS((seq, D_ATTN), BF16), SDS((N_HEADS, 1, seq), F32)],
        compiler_params=_cp(("arbitrary",), 40),
    )(dycat, o, u)


def _flash_bwd(q, k, v, do_t, lse, delta):
    seq = q.shape[0]
    t = min(ATT_T, seq)
    n = seq // t
    n_pairs = n * (n + 1) // 2

    def body(k_ref, v_ref, q_ref, dot_ref, lse_ref, dl_ref, dq_ref, dk_ref, dv_ref, s_a, s_b, dp_a, dp_b, dq_acc, dk_acc,
             dvt_acc):
        bufs = ((s_a, dp_a), (s_b, dp_b))

        def rows(i):
            return pl.ds(pl.multiple_of(i * t, t), t)

        def products(j, i, buf):
            s_ref, dp_ref = buf
            s = _nt(k_ref[rows(j), :], q_ref[rows(i), :])
            ahead = lax.broadcasted_iota(jnp.int32, (t, t), 0) - lax.broadcasted_iota(jnp.int32, (t, t), 1)
            s_ref[...] = jnp.where(ahead <= (i - j) * t, s, -jnp.inf)
            dp_ref[...] = _nn(v_ref[rows(j), :], dot_ref[:, rows(i)])

        def absorb(j, i, buf):
            s_ref, dp_ref = buf
            first = i == j
            p = jnp.exp2((s_ref[...] - lse_ref[:, rows(i)]).astype(BF16))
            dvt = jnp.where(first, 0.0, dvt_acc[...]) + _nt(dot_ref[:, rows(i)], p)
            ds = p * (dp_ref[...] - dl_ref[:, rows(i)]).astype(BF16)
            dk = jnp.where(first, 0.0, dk_acc[...]) + _nn(ds, q_ref[rows(i), :])
            dq_acc[rows(i), :] += _tn(ds, k_ref[rows(j), :])
            dvt_acc[...] = dvt
            dk_acc[...] = dk
            dk_ref[rows(j), :] = dk.astype(BF16)
            dv_ref[rows(j), :] = dvt.T.astype(BF16)

        def trip(width):
            def walk(_, pair):
                j, i = pair
                for w in range(width):
                    done = i == n - 1
                    nj = jnp.where(done, j + 1, j)
                    ni = jnp.where(done, j + 1, i + 1)
                    absorb(j, i, bufs[w % 2])
                    products(jnp.minimum(nj, n - 1), jnp.minimum(ni, n - 1), bufs[(w + 1) % 2])
                    j, i = nj, ni
                return j, i
            return walk

        dq_acc[...] = jnp.zeros_like(dq_acc)
        products(0, 0, bufs[0])
        pair = lax.fori_loop(0, n_pairs // ATT_UNROLL, trip(ATT_UNROLL), (jnp.int32(0), jnp.int32(0)))
        if n_pairs % ATT_UNROLL:
            trip(n_pairs % ATT_UNROLL)(0, pair)

        def finish(i, carry):
            dq_ref[rows(i), :] = dq_acc[rows(i), :].astype(BF16)
            return carry

        lax.fori_loop(0, n, finish, 0)

    def col(width):
        return pl.BlockSpec((seq, width), lambda h: (0, h))

    row = pl.BlockSpec((None, 1, seq), lambda h: (h, 0, 0))
    return pl.pallas_call(
        body, name="flash_bwd", grid=(N_HEADS,),
        in_specs=[col(HEAD_PAD), col(V_HEAD), col(HEAD_PAD), pl.BlockSpec((None, V_HEAD, seq), lambda h: (h, 0, 0)), row, row],
        out_specs=[col(HEAD_PAD), col(HEAD_PAD), col(V_HEAD)],
        out_shape=[SDS((seq, Q_PAD), BF16), SDS((seq, Q_PAD), BF16), SDS((seq, D_ATTN), BF16)],
        scratch_shapes=[pltpu.VMEM((t, t), F32)] * 4
                       + [pltpu.VMEM((seq, HEAD_PAD), F32), pltpu.VMEM((t, HEAD_PAD), F32), pltpu.VMEM((V_HEAD, t), F32)],
        compiler_params=_cp(("arbitrary",), 60),
    )(k, v, q, do_t, lse, delta)


SG_QAG, SG_KVAG, SG_QG, SG_KG, SG_COLS = 0, Q_LORA, Q_LORA + KV_LORA, Q_LORA + KV_LORA + HEAD_PAD, D_MODEL


def _mla_bwd(dq, dk, dv, u, pos, freq, q_a_g, wq, kv_a_g, wkn, wv, q_g, k_g):
    seq = u.shape[0]
    ts = min(ROW_T, seq)
    n = seq // ts
    qscale = 1.0 / math.sqrt(QK_HEAD)

    def body(dq_ref, dk_ref, dv_ref, cq_ref, ckv_ref, kr_ref, pos_ref, freq_ref, qag_ref, wq_ref, kvag_ref, wkn_ref,
             wv_ref, qg_ref, kg_ref, du_ref, dwq_ref, dwkn_ref, dwv_ref, sg_ref, dqp_ref, dkn_ref):
        i = pl.program_id(0)

        @pl.when(i == 0)
        def _():
            dwq_ref[...] = jnp.zeros_like(dwq_ref)
            dwkn_ref[...] = jnp.zeros_like(dwkn_ref)
            dwv_ref[...] = jnp.zeros_like(dwv_ref)
            sg_ref[...] = jnp.zeros_like(sg_ref)

        tables = _rope_tables(pos_ref, freq_ref)

        cq = cq_ref[...].astype(F32)
        cqn, rq = _rms(cq, Q_LORA)
        qag = qag_ref[...]
        cqb = (cqn * qag).astype(BF16)
        qp = _nn(cqb, wq_ref[...])
        qg = qg_ref[...]
        dqg = jnp.zeros((1, HEAD_PAD), F32)
        for h in range(N_HEADS):
            lo = h * HEAD_PAD
            xn, r = _rms(qp[:, lo:lo + HEAD_PAD], QK_HEAD)
            g = jnp.concatenate([dq_ref[:, lo:lo + LANES].astype(F32),
                                 _rope_bwd(dq_ref[:, lo + LANES:lo + HEAD_PAD].astype(F32), tables)], axis=-1) * qscale
            dqg = dqg + jnp.sum(g * xn, axis=0, keepdims=True)
            gy = g * qg
            mean = jnp.sum(gy * xn, axis=-1, keepdims=True) * (1.0 / QK_HEAD)
            dqp_ref[:, lo:lo + HEAD_PAD] = (r * (gy - xn * mean)).astype(BF16)
        dqp = dqp_ref[...]
        dwq_ref[...] += _tn(cqb, dqp)
        dcqn = _nt(dqp, wq_ref[...])
        sg_ref[0:1, SG_QAG:SG_QAG + Q_LORA] += jnp.sum(dcqn * cqn, axis=0, keepdims=True)
        sg_ref[0:1, SG_QG:SG_QG + HEAD_PAD] += dqg
        gy = dcqn * qag
        mean = jnp.sum(gy * cqn, axis=-1, keepdims=True) * (1.0 / Q_LORA)
        du_ref[:, 0:Q_LORA] = (rq * (gy - cqn * mean)).astype(BF16)

        ckv = ckv_ref[...].astype(F32)
        ckvn, rkv = _rms(ckv, KV_LORA)
        kvag = kvag_ref[...]
        ckvb = (ckvn * kvag).astype(BF16)
        kn = _nn(ckvb, wkn_ref[...])
        kr = kr_ref[:, 0:LANES].astype(F32)
        ssr = jnp.sum(kr * kr, axis=-1, keepdims=True)
        kg = kg_ref[...]
        kg_n, kg_r = kg[:, :LANES] * LN2, kg[:, LANES:] * LN2
        dkg_n = jnp.zeros((1, LANES), F32)
        dkg_r = jnp.zeros((1, LANES), F32)
        dkr = jnp.zeros((ts, LANES), F32)
        for h in range(N_HEADS):
            knh = kn[:, h * QK_NOPE:(h + 1) * QK_NOPE]
            r = lax.rsqrt((jnp.sum(knh * knh, axis=-1, keepdims=True) + ssr) * (1.0 / QK_HEAD) + EPS)
            xn_n, xn_r = knh * r, kr * r
            lo = h * HEAD_PAD
            g_n = dk_ref[:, lo:lo + LANES].astype(F32)
            g_r = _rope_bwd(dk_ref[:, lo + LANES:lo + HEAD_PAD].astype(F32), tables)
            dkg_n = dkg_n + jnp.sum(g_n * xn_n, axis=0, keepdims=True)
            dkg_r = dkg_r + jnp.sum(g_r * xn_r, axis=0, keepdims=True)
            gy_n, gy_r = g_n * kg_n, g_r * kg_r
            mean = (jnp.sum(gy_n * xn_n, axis=-1, keepdims=True) + jnp.sum(gy_r * xn_r, axis=-1, keepdims=True)) * (1.0 / QK_HEAD)
            dkn_ref[:, h * QK_NOPE:(h + 1) * QK_NOPE] = (r * (gy_n - xn_n * mean)).astype(BF16)
            dkr = dkr + r * (gy_r - xn_r * mean)
        dkn = dkn_ref[...]
        dvv = dv_ref[...]
        dwkn_ref[...] += _tn(ckvb, dkn)
        dwv_ref[...] += _tn(ckvb, dvv)
        dckvn = _nt(dkn, wkn_ref[...]) + _nt(dvv, wv_ref[...])
        sg_ref[0:1, SG_KVAG:SG_KVAG + KV_LORA] += jnp.sum(dckvn * ckvn, axis=0, keepdims=True)
        sg_ref[0:1, SG_KG:SG_KG + LANES] += dkg_n * LN2
        sg_ref[0:1, SG_KG + LANES:SG_KG + HEAD_PAD] += dkg_r * LN2
        gy = dckvn * kvag
        mean = jnp.sum(gy * ckvn, axis=-1, keepdims=True) * (1.0 / KV_LORA)
        du_ref[:, Q_LORA:Q_LORA + KV_LORA] = (rkv * (gy - ckvn * mean)).astype(BF16)
        du_ref[:, Q_LORA + KV_LORA:Q_LORA + KV_LORA + LANES] = dkr.astype(BF16)
        du_ref[:, Q_LORA + KV_LORA + LANES:MLA_COLS] = jnp.zeros((ts, LANES), BF16)

    def full(a):
        return pl.BlockSpec(a.shape, lambda i: (0,) * a.ndim)

    wide = pl.BlockSpec((ts, Q_PAD), lambda i: (i, 0))
    return pl.pallas_call(
        body, name="mla_bwd", grid=(n,),
        in_specs=[wide, wide, pl.BlockSpec((ts, D_ATTN), lambda i: (i, 0)),
                  pl.BlockSpec((ts, Q_LORA), lambda i: (i, U_CQ // Q_LORA)),
                  pl.BlockSpec((ts, KV_LORA), lambda i: (i, U_CKV // KV_LORA)),
                  pl.BlockSpec((ts, KR_PAD), lambda i: (i, U_KR // KR_PAD)),
                  pl.BlockSpec((ts, 1), lambda i: (i, 0)), full(freq), full(q_a_g), full(wq), full(kv_a_g), full(wkn),
                  full(wv), full(q_g), full(k_g)],
        out_specs=[pl.BlockSpec((ts, MLA_COLS), lambda i: (i, 0)), pl.BlockSpec((Q_LORA, Q_PAD), lambda i: (0, 0)),
                   pl.BlockSpec((KV_LORA, D_ATTN), lambda i: (0, 0)), pl.BlockSpec((KV_LORA, D_ATTN), lambda i: (0, 0)),
                   pl.BlockSpec((8, SG_COLS), lambda i: (0, 0))],
        out_shape=[SDS((seq, MLA_COLS), BF16), SDS((Q_LORA, Q_PAD), F32), SDS((KV_LORA, D_ATTN), F32),
                   SDS((KV_LORA, D_ATTN), F32), SDS((8, SG_COLS), F32)],
        scratch_shapes=[pltpu.VMEM((ts, Q_PAD), BF16), pltpu.VMEM((ts, D_ATTN), BF16)],
        compiler_params=_cp(("arbitrary",), 56),
    )(dq, dk, dv, u, u, u, pos, freq, q_a_g, wq, kv_a_g, wkn, wv, q_g, k_g)


def _conv_bwd(dycat, u, conv_w):
    seq = u.shape[0]
    ts = min(ROW_T, seq)
    n = seq // ts
    hb = ts // HALO

    def body(dy_ref, xc_ref, bc_ref, cc_ref, zc_ref, xp_ref, cp_ref, dyn_ref, bn_ref, zn_ref, w_ref,
             du_ref, dw_ref, ext_ref, dext_ref, acc_ref):
        i = pl.program_id(0)

        @pl.when(i == 0)
        def _():
            dw_ref[...] = jnp.zeros_like(dw_ref)

        up = cp_ref[...].astype(F32) * xp_ref[...].astype(F32)
        ext_ref[0:HALO, :] = jnp.where(i > 0, up, 0.0)
        ext_ref[HALO:HALO + ts, :] = cc_ref[...].astype(F32) * xc_ref[...].astype(F32)
        zn = zn_ref[...].astype(F32)
        dnext = dyn_ref[...].astype(F32) * (zn * _sigmoid(zn)) * bn_ref[...].astype(F32)
        dext_ref[ts:ts + HALO, :] = jnp.where(i < n - 1, dnext, 0.0)
        acc_ref[...] = jnp.zeros_like(acc_ref)

        for r0 in range(0, ts, CHUNK_ROWS):
            rows = slice(r0, r0 + CHUNK_ROWS)
            for c0 in range(0, D_CONV, CHUNK_LANES):
                cols = slice(c0, c0 + CHUNK_LANES)
                uc = ext_ref[HALO + r0:HALO + r0 + CHUNK_ROWS, cols]
                u1 = ext_ref[HALO - 1 + r0:HALO - 1 + r0 + CHUNK_ROWS, cols]
                u2 = ext_ref[HALO - 2 + r0:HALO - 2 + r0 + CHUNK_ROWS, cols]
                conv = w_ref[0:1, cols] * u2 + w_ref[1:2, cols] * u1 + w_ref[2:3, cols] * uc
                z = zc_ref[rows, cols].astype(F32)
                sg = _sigmoid(z)
                sz = z * sg
                b = bc_ref[rows, cols].astype(F32)
                dy = dy_ref[rows, cols].astype(F32)
                du_ref[rows, 3 * D_CONV + c0:3 * D_CONV + c0 + CHUNK_LANES] = (dy * (b * conv) * _silu_grad(z, sg)).astype(BF16)
                du_ref[rows, D_CONV + c0:D_CONV + c0 + CHUNK_LANES] = (dy * sz * conv).astype(BF16)
                dconv = dy * sz * b
                dext_ref[rows, cols] = dconv
                acc_ref[0:CHUNK_ROWS, cols] += dconv * u2
                acc_ref[CHUNK_ROWS:2 * CHUNK_ROWS, cols] += dconv * u1
                acc_ref[2 * CHUNK_ROWS:3 * CHUNK_ROWS, cols] += dconv * uc
        for r0 in range(0, ts, CHUNK_ROWS):
            rows = slice(r0, r0 + CHUNK_ROWS)
            for c0 in range(0, D_CONV, CHUNK_LANES):
                cols = slice(c0, c0 + CHUNK_LANES)
                du = (w_ref[2:3, cols] * dext_ref[rows, cols] + w_ref[1:2, cols] * dext_ref[r0 + 1:r0 + 1 + CHUNK_ROWS, cols]
                      + w_ref[0:1, cols] * dext_ref[r0 + 2:r0 + 2 + CHUNK_ROWS, cols])
                du_ref[rows, 2 * D_CONV + c0:2 * D_CONV + c0 + CHUNK_LANES] = (du * xc_ref[rows, cols].astype(F32)).astype(BF16)
                du_ref[rows, c0:c0 + CHUNK_LANES] = (du * cc_ref[rows, cols].astype(F32)).astype(BF16)
        for k in range(3):
            dw_ref[k:k + 1, :] += jnp.sum(acc_ref[k * CHUNK_ROWS:(k + 1) * CHUNK_ROWS, :], axis=0, keepdims=True)

    def col(cb):
        return pl.BlockSpec((ts, D_CONV), lambda i: (i, cb))

    def prev(cb):
        return pl.BlockSpec((HALO, D_CONV), lambda i: (jnp.maximum(i * hb - 1, 0), cb))

    def nxt(cb):
        return pl.BlockSpec((HALO, D_CONV), lambda i: (jnp.minimum((i + 1) * hb, n * hb - 1), cb))

    return pl.pallas_call(
        body, name="conv_bwd", grid=(n,),
        in_specs=[col(0), col(0), col(1), col(2), col(3), prev(0), prev(2), nxt(0), nxt(1), nxt(3),
                  pl.BlockSpec((3, D_CONV), lambda i: (0, 0))],
        out_specs=[pl.BlockSpec((ts, 4 * D_CONV), lambda i: (i, 0)), pl.BlockSpec((8, D_CONV), lambda i: (0, 0))],
        out_shape=[SDS((seq, 4 * D_CONV), BF16), SDS((8, D_CONV), F32)],
        scratch_shapes=[pltpu.VMEM((ts + HALO, D_CONV), F32), pltpu.VMEM((ts + HALO, D_CONV), F32),
                        pltpu.VMEM((3 * CHUNK_ROWS, D_CONV), F32)],
        compiler_params=_cp(("arbitrary",), 48),
    )(dycat, u, u, u, u, u, u, dycat, u, u, conv_w)


def _inproj_bwd(du_conv, du_za, du_mla, w_t, parts):
    seq = du_conv.shape[0]
    dm = w_t.shape[1]
    tm, tn = min(DH_TM, seq), DH_TN
    ni, nj = seq // tm, dm // tn
    na = len(parts)

    def body(dc_ref, dz_ref, dm_ref, w_ref, *rest):
        part_refs, o_ref, recv_refs = rest[:na], rest[na], rest[na + 1:2 * na + 1]
        ssem, rsem = rest[2 * na + 1:]
        i, j = pl.program_id(0), pl.program_id(1)
        sends, recvs = _chip_exchange_copies(part_refs, recv_refs, ssem, rsem)

        @pl.when((i == 0) & (j == 0))
        def _():
            for cp in sends:
                cp.start()

        acc = _nn(dc_ref[...], w_ref[0:U_ZA, :])
        acc = acc + _nn(dz_ref[...], w_ref[U_ZA:U_CQ, :])
        acc = acc + _nn(dm_ref[...], w_ref[U_CQ:U_COLS, :])
        o_ref[...] = acc

        @pl.when((i == ni - 1) & (j == nj - 1))
        def _():
            for cp in recvs:
                cp.wait_recv()
            for cp in sends:
                cp.wait_send()

    outs = pl.pallas_call(
        body, name="inproj_bwd", grid=(ni, nj),
        in_specs=[pl.BlockSpec((tm, U_ZA), lambda i, j: (i, 0)), pl.BlockSpec((tm, D_ATTN), lambda i, j: (i, 0)),
                  pl.BlockSpec((tm, MLA_COLS), lambda i, j: (i, 0)), pl.BlockSpec((U_COLS, tn), lambda i, j: (0, j))]
                 + [ANY] * na,
        out_specs=[pl.BlockSpec((tm, tn), lambda i, j: (i, j))] + [ANY] * na,
        out_shape=[SDS((seq, dm), F32)] + [SDS(p.shape, p.dtype) for p in parts],
        scratch_shapes=[pltpu.SemaphoreType.DMA((3 * na,))] * 2,
        compiler_params=_cp(("arbitrary", "arbitrary"), 48),
    )(du_conv, du_za, du_mla, w_t, *parts)
    return outs[0], outs[1:]


def _prenorm_bwd(x, dh, dout, norm_g, scale):
    seq, dm = x.shape
    ts = min(ROW_T, seq)
    n = seq // ts

    def body(x_ref, dh_ref, dout_ref, g_ref, sc_ref, gx_ref, st_ref, acc_ref):
        i = pl.program_id(0)

        @pl.when(i == 0)
        def _():
            acc_ref[...] = jnp.zeros_like(acc_ref)

        xv = x_ref[...]
        xn, r = _rms(xv, dm)
        dh_v = dh_ref[...]
        gv = g_ref[...]
        one_sc = 1.0 + sc_ref[...]

        def fold(a):
            return jnp.sum(a.reshape(ts // 8, 8, dm), axis=0)

        acc_ref[0:8, :] += fold(dh_v)
        acc_ref[8:16, :] += fold(dh_v * (xn * gv))
        dxg = dh_v * one_sc
        acc_ref[16:24, :] += fold(dxg * xn)
        dxn = dxg * gv
        mean = jnp.sum(dxn * xn, axis=-1, keepdims=True) * (1.0 / dm)
        gx_ref[...] = dout_ref[...] + r * (dxn - xn * mean)

        @pl.when(i == n - 1)
        def _():
            st_ref[...] = jnp.zeros_like(st_ref)
            for k in range(3):
                st_ref[k:k + 1, :] = jnp.sum(acc_ref[8 * k:8 * k + 8, :], axis=0, keepdims=True)

    row = pl.BlockSpec((ts, dm), lambda i: (i, 0))
    vec = pl.BlockSpec((1, dm), lambda i: (0, 0))
    return pl.pallas_call(
        body, name="prenorm_bwd", grid=(n,), in_specs=[row, row, row, vec, vec],
        out_specs=[row, pl.BlockSpec((8, dm), lambda i: (0, 0))],
        out_shape=[SDS((seq, dm), F32), SDS((8, dm), F32)],
        scratch_shapes=[pltpu.VMEM((24, dm), F32)], input_output_aliases={2: 0},
        compiler_params=_cp(("arbitrary",), 52),
    )(x, dh, dout, norm_g, scale)


def _unshard_cols(g):
    return jnp.transpose(g, (1, 0, 2)).reshape(g.shape[1], -1)


def _shard_cols(w):
    r = w.shape[0]
    return jnp.transpose(w.reshape(r, N_CHIPS, -1), (1, 0, 2))


W_IN_COLS = 4 * D_CONV + Q_LORA + KV_LORA + QK_ROPE + D_ATTN
SHARD_ROWS = W_IN_COLS // N_CHIPS
SHARD_PAD = 1536


def _w_in_pieces():
    c4 = 4 * D_CONV
    groups = [(0, c4, 0), (c4, c4 + Q_LORA, U_CQ), (c4 + Q_LORA, c4 + Q_LORA + KV_LORA, U_CKV),
              (c4 + Q_LORA + KV_LORA, W_IN_COLS - D_ATTN, U_KR), (W_IN_COLS - D_ATTN, W_IN_COLS, U_ZA)]
    pieces = []
    for lo, hi, my in groups:
        for chip in range(N_CHIPS):
            a, b = max(lo, chip * SHARD_ROWS), min(hi, (chip + 1) * SHARD_ROWS)
            if a < b:
                pieces.append((chip, a - chip * SHARD_ROWS, b - a, my + a - lo))
    return pieces


def _w_t_to_my(g):
    w = jnp.zeros((U_COLS, g.shape[2]), g.dtype)
    for chip, row, n, my in _w_in_pieces():
        w = lax.dynamic_update_slice(w, g[chip, row:row + n], (my, 0))
    return w


def _w_t_from_my(g_conv, g_za, g_mla):
    w = jnp.zeros((N_CHIPS, SHARD_PAD, g_conv.shape[1]), g_conv.dtype)
    for chip, row, n, my in _w_in_pieces():
        src, base = (g_conv, 0) if my < U_ZA else (g_za, U_ZA) if my < U_CQ else (g_mla, U_CQ)
        w = lax.dynamic_update_slice(w, src[my - base:my - base + n][None], (chip, row, 0))
    return w


def _heads_pad(w):
    r = w.shape[0]
    w3 = w.reshape(r, N_HEADS, QK_HEAD)
    return jnp.pad(w3, ((0, 0), (0, 0), (0, HEAD_PAD - QK_HEAD))).reshape(r, Q_PAD)


def _heads_unpad(w):
    r = w.shape[0]
    return w.reshape(r, N_HEADS, HEAD_PAD)[:, :, :QK_HEAD].reshape(r, N_HEADS * QK_HEAD)


def kernel(x, c, positions, ada_w, ada_b, norm_g, w_in, conv_w, q_a_g, w_q_b, kv_a_g, w_kv_b, q_g, k_g, w_out, loss_target, m_ada_w, m_ada_b, m_norm_g, m_w_in, m_conv_w, m_q_a_g, m_w_q_b, m_kv_a_g, m_w_kv_b, m_q_g, m_k_g, m_w_out, v_ada_w, v_ada_b, v_norm_g, v_w_in, v_conv_w, v_q_a_g, v_w_q_b, v_kv_a_g, v_w_kv_b, v_q_g, v_k_g, v_w_out):
    mx, my, mc = _place()
    chip = 2 * mx + my
    me = 2 * chip + mc
    seq = x.shape[1]
    x2, t2 = x[0], loss_target[0]
    cw_cols = conv_w.shape[2]

    small = jnp.zeros((8, D_MODEL), F32)
    small = small.at[0].set(c[0])
    small = small.at[1:4, :cw_cols].set(conv_w[0])
    small_all = _gather8(small, "gather_c_conv", False)[0]
    c_all = small_all[:, 0, :]
    conv_full = jnp.transpose(small_all.reshape(N_CHIPS, 2, 8, D_MODEL)[:, 0, 1:4, :cw_cols], (1, 0, 2)).reshape(3, D_CONV)

    ada_cols = ada_w.shape[2]
    b_k = lax.dynamic_slice(ada_b, (0, chip * ada_cols), (1, ada_cols))
    mod_k, sc_all = _ada_mod(c_all, ada_w[0], b_k)
    mod_all = _gather8(mod_k, "gather_mod", False)[0]
    mod_row = lax.dynamic_slice(mod_all.reshape(N_CHIPS, 2, N_DEV, ada_cols), (0, mc, me, 0), (N_CHIPS, 1, 1, ada_cols))
    mod_row = mod_row.reshape(3, D_MODEL)
    shift, scale, gate = mod_row[0:1], mod_row[1:2], mod_row[2:3]

    def own_slot(g, s):
        return lax.dynamic_update_slice(g, s[None], (chip, 0, 0))

    w_in_t, m_w_in_t, v_w_in_t = [jnp.transpose(a[0]) for a in (w_in, m_w_in, v_w_in)]
    shard_in = jnp.pad(w_in_t.astype(BF16), ((0, SHARD_PAD - SHARD_ROWS), (0, 0)))
    h, g_in = _prenorm_gather(x2, norm_g, scale, shift, shard_in)
    g_in = own_slot(g_in, shard_in)
    w_t = _w_t_to_my(g_in)

    later = [w_q_b[0].astype(BF16), w_kv_b[0].astype(BF16), w_out[0].astype(BF16)]
    u, got = _inproj(h, w_t, later)
    g_q, g_kv, g_out = [own_slot(g, s) for g, s in zip(got, later)]
    wq = _heads_pad(_unshard_cols(g_q))
    wkv = _unshard_cols(g_kv).reshape(KV_LORA, N_HEADS, QK_NOPE + V_HEAD)
    wkn = wkv[:, :, :QK_NOPE].reshape(KV_LORA, N_HEADS * QK_NOPE)
    wv = wkv[:, :, QK_NOPE:].reshape(KV_LORA, D_ATTN)
    wo = g_out.reshape(N_CHIPS * g_out.shape[1], D_MODEL)
    y_conv = _conv_fwd(u, conv_full)
    pos = positions.reshape(seq, 1)
    inv_freq = ROPE_BASE ** (-jnp.arange(0, QK_ROPE, 2, dtype=F32) / QK_ROPE)
    freq = jnp.concatenate([inv_freq, inv_freq, jnp.zeros((LANES - QK_ROPE,), F32)]).reshape(1, LANES)
    q_g_pad = jnp.pad(q_g, ((0, 0), (0, HEAD_PAD - QK_HEAD)))
    k_g_pad = jnp.pad(k_g, ((0, 0), (0, HEAD_PAD - QK_HEAD)))
    q, k, v = _mla_prep(u, pos, freq, q_a_g, wq, kv_a_g, wkn, wv, q_g_pad, k_g_pad)
    o, y_attn, lse = _flash_fwd(q, k, v, u)
    dout, dy, dycat, st_out = _outproj_loss(y_conv, y_attn, x2, t2, gate, wo)

    dw_out = jnp.concatenate([_matmul_tn(y_conv, dy, "dw_out_conv"), _matmul_tn(y_attn, dy, "dw_out_attn")], axis=0)
    do_t, du_za, delta = _attn_gate_bwd(dycat, o, u)
    dq, dk, dv = _flash_bwd(q, k, v, do_t, lse, delta)
    du_mla, dwq, dwkn, dwv, sg_mla = _mla_bwd(dq, dk, dv, u, pos, freq, q_a_g, wq, kv_a_g, wkn, wv, q_g_pad, k_g_pad)
    du_conv, dconv_w = _conv_bwd(dycat, u, conv_full)
    dw_conv = _matmul_tn(du_conv, h, "dw_in_conv")
    dw_za = _matmul_tn(du_za, h, "dw_in_za")
    dw_mla = _matmul_tn(du_mla, h, "dw_in_mla")

    dw_q_nat = _heads_unpad(dwq).astype(BF16)
    dw_kv_nat = jnp.concatenate([dwkn.reshape(KV_LORA, N_HEADS, QK_NOPE), dwv.reshape(KV_LORA, N_HEADS, V_HEAD)],
                                axis=2).reshape(KV_LORA, N_HEADS * (QK_NOPE + V_HEAD)).astype(BF16)
    grads = [_w_t_from_my(dw_conv, dw_za, dw_mla), _shard_cols(dw_q_nat), _shard_cols(dw_kv_nat),
             dw_out.reshape(N_CHIPS, dw_out.shape[0] // N_CHIPS, D_MODEL)]
    theirs = _rs_core_swap(grads)
    names = ["w_in", "w_q_b", "w_kv_b", "w_out"]
    core = jnp.reshape(mc, (1,)).astype(jnp.int32)
    parts = [_add_half_bf16(g, b, core, "rs_add_" + nm) for g, b, nm in zip(grads, theirs, names)]
    dh, recv = _inproj_bwd(du_conv, du_za, du_mla, w_t, parts)
    recv = [lax.dynamic_update_slice(r, lax.dynamic_slice(p, (chip, 0, 0), (1,) + p.shape[1:]), (chip, 0, 0))
            for r, p in zip(recv, parts)]
    halves = [_sum_chips(p, "rs_sum_" + nm) for p, nm in zip(recv, names)]
    joined = _rs_core_join(halves)
    joined = [lax.dynamic_update_slice(j, hf[None], (mc, 0, 0)) for j, hf in zip(joined, halves)]
    g_big = [j.reshape(2 * j.shape[1], j.shape[2]) for j in joined]
    grad_x, st_in = _prenorm_bwd(x2, dh, dout, norm_g, scale)

    sgrad = jnp.zeros((8, D_MODEL), F32)
    sgrad = sgrad.at[0:2].set(st_in[0:2])
    sgrad = sgrad.at[2].set(st_out[0])
    sgrad = sgrad.at[3].set(st_in[2])
    sgrad = sgrad.at[4, :D_CONV].set(dconv_w[0]).at[4, D_CONV:].set(dconv_w[1])
    sgrad = sgrad.at[5, :D_CONV].set(dconv_w[2]).at[5, D_CONV:].set(sg_mla[0, :D_CONV])
    sgrad = sgrad.at[6, :HEAD_PAD].set(sg_mla[0, SG_KG:SG_KG + HEAD_PAD])
    sgrad = sgrad.at[7].set(st_out[1])
    sg_all, sg_sum = _gather8(sgrad, "gather_small_grads", True)
    loss = sg_sum[7, 0]
    g_ada_b = sg_sum[0:3].reshape(1, 3 * D_MODEL)
    g_norm_g = sg_sum[3:4]
    conv_sum = jnp.stack([sg_sum[4, :D_CONV], sg_sum[4, D_CONV:], sg_sum[5, :D_CONV]])
    g_conv_w = lax.dynamic_slice(conv_sum, (0, chip * cw_cols), (3, cw_cols))
    g_q_a_g = sg_sum[5:6, D_CONV + SG_QAG:D_CONV + SG_QAG + Q_LORA]
    g_kv_a_g = sg_sum[5:6, D_CONV + SG_KVAG:D_CONV + SG_KVAG + KV_LORA]
    g_q_g = sg_sum[5:6, D_CONV + SG_QG:D_CONV + SG_QG + QK_HEAD]
    g_k_g = sg_sum[6:7, :QK_HEAD]
    dmod_k = lax.dynamic_slice(sg_all[:, 0:3, :].reshape(N_DEV, 3 * D_MODEL), (0, chip * ada_cols), (N_DEV, ada_cols))

    g_ada_w, d_ada_w, nm_ada_w, nv_ada_w = _ada_w_update(sc_all, dmod_k, ada_w[0], m_ada_w[0], v_ada_w[0])
    upd = {}
    big = {"w_q_b": (w_q_b, m_w_q_b, v_w_q_b), "w_kv_b": (w_kv_b, m_w_kv_b, v_w_kv_b), "w_out": (w_out, m_w_out, v_w_out)}
    for nm, g in zip(names[1:], g_big[1:]):
        w_, m_, v_ = big[nm]
        upd[nm] = (g,) + tuple(_adamw(w_[0], g, m_[0], v_[0], "adamw_" + nm))
    d_t, nm_t, nv_t, g_t = _adamw(w_in_t, g_big[0], m_w_in_t, v_w_in_t, "adamw_w_in", echo_g=True)
    upd["w_in"] = tuple(jnp.transpose(a) for a in (g_t, d_t, nm_t, nv_t))
    small_w = {"ada_b": (ada_b, m_ada_b, v_ada_b, g_ada_b), "norm_g": (norm_g, m_norm_g, v_norm_g, g_norm_g),
               "conv_w": (conv_w[0], m_conv_w[0], v_conv_w[0], g_conv_w), "q_a_g": (q_a_g, m_q_a_g, v_q_a_g, g_q_a_g),
               "kv_a_g": (kv_a_g, m_kv_a_g, v_kv_a_g, g_kv_a_g), "q_g": (q_g, m_q_g, v_q_g, g_q_g),
               "k_g": (k_g, m_k_g, v_k_g, g_k_g)}
    for nm, (w_, m_, v_, g) in small_w.items():
        upd[nm] = (g,) + tuple(_adamw(w_, g, m_, v_, "adamw_" + nm))
    upd["ada_w"] = (g_ada_w, d_ada_w, nm_ada_w, nv_ada_w)

    order = ["ada_w", "ada_b", "norm_g", "w_in", "conv_w", "q_a_g", "w_q_b", "kv_a_g", "w_kv_b", "q_g", "k_g", "w_out"]
    lead1 = {"ada_w", "w_in", "conv_w", "w_q_b", "w_kv_b", "w_out"}

    def shaped(nm, a):
        return a[None] if nm in lead1 else a

    outs = [loss, grad_x[None]]
    for idx in range(4):
        outs += [shaped(nm, upd[nm][idx]) for nm in order]
    return tuple(outs)
```

```python
import math

import jax
import jax.numpy as jnp
from jax import lax
from jax.experimental import pallas as pl
from jax.experimental.pallas import tpu as pltpu

F32 = jnp.float32
BF16 = jnp.bfloat16
MESH = pl.DeviceIdType.MESH
SDS = jax.ShapeDtypeStruct
ANY = pl.BlockSpec(memory_space=pl.ANY)

D_MODEL = 2048
D_CONV = 1024
N_HEADS = 8
QK_NOPE = 128
QK_ROPE = 64
QK_HEAD = QK_NOPE + QK_ROPE
V_HEAD = 128
D_ATTN = N_HEADS * V_HEAD
Q_LORA = 512
KV_LORA = 256
ROPE_BASE = 10000.0
EPS = 1e-6
LOG2E = math.log2(math.e)
LN2 = math.log(2.0)
ADAM_LR, ADAM_B1, ADAM_B2, ADAM_EPS, ADAM_WD, ADAM_STEP = 0.001, 0.9, 0.999, 1e-08, 0.01, 10
N_CHIPS = 4
N_DEV = 8

LANES = 128
V7X_VMEM_BYTES = 64 * 1024 * 1024
MIB = 1024 * 1024

HEAD_PAD = 256
Q_PAD = N_HEADS * HEAD_PAD
U_ZA = 4 * D_CONV
U_CQ = U_ZA + D_ATTN
U_CKV = U_CQ + Q_LORA
U_KR = U_CKV + KV_LORA
KR_PAD = 256
U_COLS = U_KR + KR_PAD
MLA_COLS = Q_LORA + KV_LORA + KR_PAD

ATT_T = 512
INPROJ_TM, INPROJ_TN = 1024, 1024
ROW_T = 512
MLA_PREP_T = 1024
OUT_T = 256
DH_TM, DH_TN = 512, 1024
TN_TM, TN_TN, TN_TK = 1024, 1024, 2048
ATT_UNROLL = 8


def _cp(sem=None, vmem_mib=None, **kw):
    if sem is not None:
        kw["dimension_semantics"] = sem
    if vmem_mib is not None:
        kw["vmem_limit_bytes"] = min(vmem_mib * MIB, V7X_VMEM_BYTES - 4 * MIB)
    return pltpu.CompilerParams(**kw)


def _sigmoid(z):
    return 1.0 / (1.0 + jnp.exp(-z))


def _silu_grad(z, sg):
    return sg * (1.0 + z * (1.0 - sg))


def _nt(a, b):
    return lax.dot_general(a, b, (((1,), (1,)), ((), ())), preferred_element_type=F32)


def _tn(a, b):
    return lax.dot_general(a, b, (((0,), (0,)), ((), ())), preferred_element_type=F32)


def _nn(a, b):
    return jnp.dot(a, b, preferred_element_type=F32)


def _place():
    return lax.axis_index("x"), lax.axis_index("y"), lax.axis_index("c")


def _gather8(v, name, with_sum):
    rows, cols = v.shape

    def body(v_ref, out_ref, *rest):
        if with_sum:
            sum_ref, send_sems, recv_sems = rest
        else:
            send_sems, recv_sems = rest
        mx, my, mc = _place()
        me = 4 * mx + 2 * my + mc
        out_ref[me] = v_ref[...]
        peers = []
        for d in range(1, N_DEV):
            px = 1 - mx if d & 4 else mx
            py = 1 - my if d & 2 else my
            pc = 1 - mc if d & 1 else mc
            peers.append((px, py, pc))

        def copy(d, slot, to):
            return pltpu.make_async_remote_copy(
                src_ref=v_ref, dst_ref=out_ref.at[slot], send_sem=send_sems.at[d], recv_sem=recv_sems.at[d],
                device_id=to, device_id_type=MESH)

        sends = [copy(d, me, p) for d, p in enumerate(peers)]
        for cp in sends:
            cp.start()
        for d, (px, py, pc) in enumerate(peers):
            copy(d, 4 * px + 2 * py + pc, (px, py, pc)).wait_recv()
        for cp in sends:
            cp.wait_send()
        if with_sum:
            acc = out_ref[0]
            for b in range(1, N_DEV):
                acc = acc + out_ref[b]
            sum_ref[...] = acc

    out_shape = [SDS((N_DEV, rows, cols), F32)]
    if with_sum:
        out_shape.append(SDS((rows, cols), F32))
    vm = pl.BlockSpec(memory_space=pltpu.VMEM)
    return pl.pallas_call(
        body, name=name, out_shape=out_shape, in_specs=[vm], out_specs=[vm] * len(out_shape),
        scratch_shapes=[pltpu.SemaphoreType.DMA((N_DEV - 1,)), pltpu.SemaphoreType.DMA((N_DEV - 1,))],
    )(v)


def _chips_of(mx, my):
    chips = [(mx, 1 - my), (1 - mx, my), (1 - mx, 1 - my)]
    return chips, [2 * px + py for px, py in chips]


def _prenorm_gather(x, norm_g, scale, shift, shard):
    seq, dm = x.shape
    tm = min(INPROJ_TM, seq)
    ni = seq // tm
    half_rows = shard.shape[0] // 2

    def body(x_ref, g_ref, sc_ref, sh_ref, shard_ref, h_ref, got_ref, s1, r1, s2, r2):
        i = pl.program_id(0)
        mx, my, mc = _place()
        k = 2 * mx + my
        sib = (mx, my, 1 - mc)
        chips, kks = _chips_of(mx, my)

        def half(slot, c):
            return got_ref.at[slot, pl.ds(c * half_rows, half_rows)]

        def over_ici(d, slot):
            return pltpu.make_async_remote_copy(
                src_ref=shard_ref.at[pl.ds(mc * half_rows, half_rows)], dst_ref=half(slot, mc), send_sem=s1.at[d],
                recv_sem=r1.at[d], device_id=(chips[d][0], chips[d][1], mc), device_id_type=MESH)

        def to_sibling(d, c):
            return pltpu.make_async_remote_copy(
                src_ref=half(kks[d], c), dst_ref=half(kks[d], c), send_sem=s2.at[d], recv_sem=r2.at[d],
                device_id=sib, device_id_type=MESH)

        @pl.when(i == 0)
        def _():
            for d in range(3):
                over_ici(d, k).start()

        xv = x_ref[...]
        r = lax.rsqrt(jnp.mean(xv * xv, axis=-1, keepdims=True) + EPS)
        h_ref[...] = ((xv * r * g_ref[...]) * (1.0 + sc_ref[...]) + sh_ref[...]).astype(BF16)

        @pl.when(i == ni - 1)
        def _():
            for d in range(3):
                over_ici(d, kks[d]).wait_recv()
                to_sibling(d, mc).start()
            for d in range(3):
                to_sibling(d, 1 - mc).wait_recv()
            for d in range(3):
                over_ici(d, k).wait_send()
                to_sibling(d, mc).wait_send()

    vec = pl.BlockSpec((1, dm), lambda i: (0, 0))
    row = pl.BlockSpec((tm, dm), lambda i: (i, 0))
    return pl.pallas_call(
        body, name="prenorm_gather", grid=(ni,), in_specs=[row, vec, vec, vec, ANY], out_specs=[row, ANY],
        out_shape=[SDS((seq, dm), BF16), SDS((N_CHIPS,) + shard.shape, shard.dtype)],
        scratch_shapes=[pltpu.SemaphoreType.DMA((3,))] * 4,
        compiler_params=_cp(("arbitrary",), 48),
    )(x, norm_g, scale, shift, shard)


def _rs_core_swap(grads):
    na = len(grads)
    halves = [g.shape[1] // 2 for g in grads]

    def body(*refs):
        ins, outs = refs[:na], refs[na:2 * na]
        ssem, rsem = refs[2 * na:]
        mx, my, mc = _place()
        sib = (mx, my, 1 - mc)
        sends = []
        for a in range(na):
            cp = pltpu.make_async_remote_copy(
                src_ref=ins[a].at[:, pl.ds((1 - mc) * halves[a], halves[a])], dst_ref=outs[a],
                send_sem=ssem.at[a], recv_sem=rsem.at[a], device_id=sib, device_id_type=MESH)
            cp.start()
            sends.append(cp)
        for cp in sends:
            cp.wait_recv()
        for cp in sends:
            cp.wait_send()

    return pl.pallas_call(
        body, name="rs_core_swap", out_shape=[SDS((N_CHIPS, h) + g.shape[2:], g.dtype) for g, h in zip(grads, halves)],
        in_specs=[ANY] * na, out_specs=[ANY] * na,
        scratch_shapes=[pltpu.SemaphoreType.DMA((na,))] * 2,
    )(*grads)


def _chip_exchange_copies(ins, outs, ssem, rsem):
    mx, my, mc = _place()
    k = 2 * mx + my
    chips, kks = _chips_of(mx, my)
    sends, recvs = [], []
    for a in range(len(ins)):
        for d, (px, py) in enumerate(chips):
            def copy(dst_slot):
                return pltpu.make_async_remote_copy(
                    src_ref=ins[a].at[kks[d]], dst_ref=outs[a].at[dst_slot], send_sem=ssem.at[3 * a + d],
                    recv_sem=rsem.at[3 * a + d], device_id=(px, py, mc), device_id_type=MESH)
            sends.append(copy(k))
            recvs.append(copy(kks[d]))
    return sends, recvs


def _rs_core_join(halves):
    na = len(halves)

    def body(*refs):
        ins, outs = refs[:na], refs[na:2 * na]
        ssem, rsem = refs[2 * na:]
        mx, my, mc = _place()
        sib = (mx, my, 1 - mc)
        sends = []
        for a in range(na):
            cp = pltpu.make_async_remote_copy(
                src_ref=ins[a], dst_ref=outs[a].at[mc], send_sem=ssem.at[a], recv_sem=rsem.at[a],
                device_id=sib, device_id_type=MESH)
            cp.start()
            sends.append(cp)
        for a in range(na):
            pltpu.make_async_remote_copy(
                src_ref=ins[a], dst_ref=outs[a].at[1 - mc], send_sem=ssem.at[a], recv_sem=rsem.at[a],
                device_id=sib, device_id_type=MESH).wait_recv()
        for cp in sends:
            cp.wait_send()

    return pl.pallas_call(
        body, name="rs_core_join", out_shape=[SDS((2,) + h.shape, h.dtype) for h in halves],
        in_specs=[ANY] * na, out_specs=[ANY] * na,
        scratch_shapes=[pltpu.SemaphoreType.DMA((na,))] * 2,
    )(*halves)


def _row_tile(rows, limit, mult=16):
    if rows <= limit:
        return rows
    best = None
    for t in range(mult, limit + 1, mult):
        if rows % t == 0:
            best = t
    assert best is not None, rows
    return best


def _add_half_bf16(g, b, core, name):
    _, h, cols = b.shape
    tb = _row_tile(h, 512)
    nb = h // tb

    def body(core_ref, g_ref, b_ref, o_ref):
        o_ref[...] = (g_ref[...].astype(F32) + b_ref[...].astype(F32)).astype(BF16)

    spec = pl.BlockSpec((None, tb, cols), lambda kk, i, core_ref: (kk, i, 0))
    return pl.pallas_call(
        body, name=name,
        grid_spec=pltpu.PrefetchScalarGridSpec(
            num_scalar_prefetch=1, grid=(N_CHIPS, nb),
            in_specs=[pl.BlockSpec((None, tb, cols), lambda kk, i, core_ref: (kk, core_ref[0] * nb + i, 0)), spec],
            out_specs=spec),
        out_shape=SDS(b.shape, BF16), compiler_params=_cp(("arbitrary", "arbitrary")),
    )(core, g, b)


def _sum_chips(p, name):
    _, rows, cols = p.shape
    tb = _row_tile(rows, 256)

    def body(p_ref, o_ref):
        acc = p_ref[0].astype(F32)
        for j in range(1, N_CHIPS):
            acc = acc + p_ref[j].astype(F32)
        o_ref[...] = acc

    return pl.pallas_call(
        body, name=name, grid=(rows // tb,),
        in_specs=[pl.BlockSpec((N_CHIPS, tb, cols), lambda i: (0, i, 0))],
        out_specs=pl.BlockSpec((tb, cols), lambda i: (i, 0)), out_shape=SDS((rows, cols), F32),
        compiler_params=_cp(("arbitrary",)),
    )(p)


def _adamw_math(w, g, m, v):
    m2 = ADAM_B1 * m + (1.0 - ADAM_B1) * g
    v2 = ADAM_B2 * v + (1.0 - ADAM_B2) * (g * g)
    m_hat = m2 / (1.0 - ADAM_B1 ** ADAM_STEP)
    v_hat = v2 / (1.0 - ADAM_B2 ** ADAM_STEP)
    delta = -ADAM_LR * (m_hat / (jnp.sqrt(v_hat) + ADAM_EPS) + ADAM_WD * w)
    return delta, m2, v2


def _adamw(w, g, m, v, name, echo_g=False):
    rows, cols = w.shape
    tb = _row_tile(rows, 256, mult=8)
    nout = 4 if echo_g else 3

    def body(w_ref, g_ref, m_ref, v_ref, d_ref, m2_ref, v2_ref, *echo):
        gv = g_ref[...]
        d, m2, v2 = _adamw_math(w_ref[...], gv, m_ref[...], v_ref[...])
        d_ref[...] = d
        m2_ref[...] = m2
        v2_ref[...] = v2
        if echo_g:
            echo[0][...] = gv

    spec = pl.BlockSpec((tb, cols), lambda i: (i, 0))
    return pl.pallas_call(
        body, name=name, grid=(rows // tb,), in_specs=[spec] * 4, out_specs=[spec] * nout,
        out_shape=[SDS((rows, cols), F32)] * nout, compiler_params=_cp(("arbitrary",), 40),
    )(w, g, m, v)


def _ada_w_update(sc_all, dmod_k, w, m, v):
    rows, cols = w.shape
    tb = 256

    def body(s_ref, dm_ref, w_ref, m_ref, v_ref, g_ref, d_ref, m2_ref, v2_ref):
        g = _tn(s_ref[...].astype(BF16), dm_ref[...].astype(BF16))
        d, m2, v2 = _adamw_math(w_ref[...], g, m_ref[...], v_ref[...])
        g_ref[...] = g
        d_ref[...] = d
        m2_ref[...] = m2
        v2_ref[...] = v2

    spec = pl.BlockSpec((tb, cols), lambda i: (i, 0))
    return pl.pallas_call(
        body, name="ada_w_update", grid=(rows // tb,),
        in_specs=[pl.BlockSpec((N_DEV, tb), lambda i: (0, i)), pl.BlockSpec((N_DEV, cols), lambda i: (0, 0)), spec, spec, spec],
        out_specs=[spec] * 4, out_shape=[SDS((rows, cols), F32)] * 4, compiler_params=_cp(("arbitrary",), 40),
    )(sc_all, dmod_k, w, m, v)


def _ada_mod(c_all, w, b_k):
    rows, cols = w.shape
    tn = 512

    def body(c_ref, w_ref, b_ref, o_ref, s_ref):
        cv = c_ref[...]
        s = cv * _sigmoid(cv)
        s_ref[...] = s
        o_ref[...] = _nn(s.astype(BF16), w_ref[...].astype(BF16)) + b_ref[...]

    return pl.pallas_call(
        body, name="ada_mod", grid=(cols // tn,),
        in_specs=[pl.BlockSpec((N_DEV, rows), lambda j: (0, 0)), pl.BlockSpec((rows, tn), lambda j: (0, j)),
                  pl.BlockSpec((1, tn), lambda j: (0, j))],
        out_specs=[pl.BlockSpec((N_DEV, tn), lambda j: (0, j)), pl.BlockSpec((N_DEV, rows), lambda j: (0, 0))],
        out_shape=[SDS((N_DEV, cols), F32), SDS((N_DEV, rows), F32)], compiler_params=_cp(("arbitrary",)),
    )(c_all, w, b_k)


def _inproj(h, w_t, shards):
    seq, dm = h.shape
    ncols = w_t.shape[0]
    tm, tn = min(INPROJ_TM, seq), INPROJ_TN
    ni, nj = seq // tm, ncols // tn
    na = len(shards)

    def body(h_ref, w_ref, *rest):
        shard_refs, u_ref, got_refs = rest[:na], rest[na], rest[na + 1:2 * na + 1]
        ssem, rsem = rest[2 * na + 1:]
        i, j = pl.program_id(0), pl.program_id(1)
        mx, my, mc = _place()
        k = 2 * mx + my
        chips, kks = _chips_of(mx, my)

        def copy(a, d, slot):
            return pltpu.make_async_remote_copy(
                src_ref=shard_refs[a], dst_ref=got_refs[a].at[slot], send_sem=ssem.at[3 * a + d],
                recv_sem=rsem.at[3 * a + d], device_id=(chips[d][0], chips[d][1], mc), device_id_type=MESH)

        @pl.when((i == 0) & (j == 0))
        def _():
            for a in range(na):
                for d in range(3):
                    copy(a, d, k).start()

        u_ref[...] = _nt(h_ref[...], w_ref[...]).astype(BF16)

        @pl.when((i == ni - 1) & (j == nj - 1))
        def _():
            for a in range(na):
                for d in range(3):
                    copy(a, d, kks[d]).wait_recv()
            for a in range(na):
                for d in range(3):
                    copy(a, d, k).wait_send()

    outs = pl.pallas_call(
        body, name="inproj", grid=(ni, nj),
        in_specs=[pl.BlockSpec((tm, dm), lambda i, j: (i, 0)), pl.BlockSpec((tn, dm), lambda i, j: (j, 0))] + [ANY] * na,
        out_specs=[pl.BlockSpec((tm, tn), lambda i, j: (i, j))] + [ANY] * na,
        out_shape=[SDS((seq, ncols), BF16)] + [SDS((N_CHIPS,) + s.shape, s.dtype) for s in shards],
        scratch_shapes=[pltpu.SemaphoreType.DMA((3 * na,))] * 2,
        compiler_params=_cp(("arbitrary", "arbitrary"), 48),
    )(h, w_t, *shards)
    return outs[0], outs[1:]


HALO = 16
CHUNK_ROWS, CHUNK_LANES = 32, 256


def _conv_fwd(u, conv_w):
    seq = u.shape[0]
    ts = min(ROW_T, seq)
    hb = ts // HALO

    def body(xc_ref, bc_ref, cc_ref, zc_ref, xp_ref, cp_ref, w_ref, y_ref, ext_ref):
        i = pl.program_id(0)
        up = cp_ref[...].astype(F32) * xp_ref[...].astype(F32)
        ext_ref[0:HALO, :] = jnp.where(i > 0, up, 0.0)
        ext_ref[HALO:HALO + ts, :] = cc_ref[...].astype(F32) * xc_ref[...].astype(F32)
        for r0 in range(0, ts, CHUNK_ROWS):
            rows = slice(r0, r0 + CHUNK_ROWS)
            for c0 in range(0, D_CONV, CHUNK_LANES):
                cols = slice(c0, c0 + CHUNK_LANES)
                uc = ext_ref[HALO + r0:HALO + r0 + CHUNK_ROWS, cols]
                u1 = ext_ref[HALO - 1 + r0:HALO - 1 + r0 + CHUNK_ROWS, cols]
                u2 = ext_ref[HALO - 2 + r0:HALO - 2 + r0 + CHUNK_ROWS, cols]
                conv = w_ref[0:1, cols] * u2 + w_ref[1:2, cols] * u1 + w_ref[2:3, cols] * uc
                z = zc_ref[rows, cols].astype(F32)
                y_ref[rows, cols] = ((bc_ref[rows, cols].astype(F32) * conv) * (z * _sigmoid(z))).astype(BF16)

    def col(cb):
        return pl.BlockSpec((ts, D_CONV), lambda i: (i, cb))

    def prev(cb):
        return pl.BlockSpec((HALO, D_CONV), lambda i: (jnp.maximum(i * hb - 1, 0), cb))

    return pl.pallas_call(
        body, name="conv_fwd", grid=(seq // ts,),
        in_specs=[col(0), col(1), col(2), col(3), prev(0), prev(2), pl.BlockSpec((3, D_CONV), lambda i: (0, 0))],
        out_specs=pl.BlockSpec((ts, D_CONV), lambda i: (i, 0)), out_shape=SDS((seq, D_CONV), BF16),
        scratch_shapes=[pltpu.VMEM((ts + HALO, D_CONV), F32)],
        compiler_params=_cp(("arbitrary",), 40),
    )(u, u, u, u, u, u, conv_w)


def _rope_tables(pos_ref, freq_ref):
    ang = pos_ref[...].astype(F32) * freq_ref[...]
    lane = lax.broadcasted_iota(jnp.int32, ang.shape, 1)
    cs, sn = jnp.cos(ang), jnp.sin(ang)
    half = QK_ROPE // 2
    cos_t = jnp.where(lane < QK_ROPE, cs, 0.0)
    sin_lo = jnp.where(lane < half, sn, 0.0)
    sin_hi = jnp.where((lane >= half) & (lane < QK_ROPE), sn, 0.0)
    return cos_t, sin_lo, sin_hi


def _rope(blk, tables):
    cos_t, sin_lo, sin_hi = tables
    half = QK_ROPE // 2
    return blk * cos_t - pltpu.roll(blk, LANES - half, 1) * sin_lo + pltpu.roll(blk, half, 1) * sin_hi


def _rope_bwd(g, tables):
    cos_t, sin_lo, sin_hi = tables
    half = QK_ROPE // 2
    return g * cos_t + pltpu.roll(g, LANES - half, 1) * sin_lo - pltpu.roll(g, half, 1) * sin_hi


def _rms(v, n):
    r = lax.rsqrt(jnp.sum(v * v, axis=-1, keepdims=True) * (1.0 / n) + EPS)
    return v * r, r


def _mla_prep(u, pos, freq, q_a_g, wq, kv_a_g, wkn, wv, q_g, k_g):
    seq = u.shape[0]
    ts = min(MLA_PREP_T, seq)
    qscale = LOG2E / math.sqrt(QK_HEAD)

    def body(cq_ref, ckv_ref, kr_ref, pos_ref, freq_ref, qag_ref, wq_ref, kvag_ref, wkn_ref, wv_ref, qg_ref, kg_ref,
             q_ref, k_ref, v_ref):
        tables = _rope_tables(pos_ref, freq_ref)
        cqn, _ = _rms(cq_ref[...].astype(F32), Q_LORA)
        qp = _nn((cqn * qag_ref[...]).astype(BF16), wq_ref[...])
        qg = qg_ref[...]
        for h in range(N_HEADS):
            lo = h * HEAD_PAD
            qn, _ = _rms(qp[:, lo:lo + HEAD_PAD], QK_HEAD)
            qn = qn * qg
            q_ref[:, lo:lo + LANES] = (qn[:, :LANES] * qscale).astype(BF16)
            q_ref[:, lo + LANES:lo + HEAD_PAD] = (_rope(qn[:, LANES:], tables) * qscale).astype(BF16)
        ckvn, _ = _rms(ckv_ref[...].astype(F32), KV_LORA)
        ckvb = (ckvn * kvag_ref[...]).astype(BF16)
        kn = _nn(ckvb, wkn_ref[...])
        v_ref[...] = _nn(ckvb, wv_ref[...]).astype(BF16)
        kr = kr_ref[:, 0:LANES].astype(F32)
        ssr = jnp.sum(kr * kr, axis=-1, keepdims=True)
        kg = kg_ref[...]
        for h in range(N_HEADS):
            knh = kn[:, h * QK_NOPE:(h + 1) * QK_NOPE]
            r = lax.rsqrt((jnp.sum(knh * knh, axis=-1, keepdims=True) + ssr) * (1.0 / QK_HEAD) + EPS)
            lo = h * HEAD_PAD
            k_ref[:, lo:lo + LANES] = (knh * r * kg[:, :LANES]).astype(BF16)
            k_ref[:, lo + LANES:lo + HEAD_PAD] = _rope(kr * r * kg[:, LANES:], tables).astype(BF16)

    def full(a):
        return pl.BlockSpec(a.shape, lambda i: (0,) * a.ndim)

    return pl.pallas_call(
        body, name="mla_prep", grid=(seq // ts,),
        in_specs=[pl.BlockSpec((ts, Q_LORA), lambda i: (i, U_CQ // Q_LORA)),
                  pl.BlockSpec((ts, KV_LORA), lambda i: (i, U_CKV // KV_LORA)),
                  pl.BlockSpec((ts, KR_PAD), lambda i: (i, U_KR // KR_PAD)),
                  pl.BlockSpec((ts, 1), lambda i: (i, 0)), full(freq), full(q_a_g), full(wq), full(kv_a_g), full(wkn),
                  full(wv), full(q_g), full(k_g)],
        out_specs=[pl.BlockSpec((ts, Q_PAD), lambda i: (i, 0)), pl.BlockSpec((ts, Q_PAD), lambda i: (i, 0)),
                   pl.BlockSpec((ts, D_ATTN), lambda i: (i, 0))],
        out_shape=[SDS((seq, Q_PAD), BF16), SDS((seq, Q_PAD), BF16), SDS((seq, D_ATTN), BF16)],
        compiler_params=_cp(("arbitrary",), 48),
    )(u, u, u, pos, freq, q_a_g, wq, kv_a_g, wkn, wv, q_g, k_g)


def _flash_fwd(q, k, v, u):
    seq = q.shape[0]
    t = min(ATT_T, seq)
    n = seq // t
    n_pairs = n * (n + 1) // 2
    za_blk = U_ZA // V_HEAD

    def body(q_ref, k_ref, v_ref, z_ref, o_ref, y_ref, lse_ref, s_a, s_b, top_a, top_b, m_all, l_all, acc_all):
        ones = jnp.ones((16, t), BF16)
        bufs = ((s_a, top_a), (s_b, top_b))

        def rows(i):
            return pl.ds(pl.multiple_of(i * t, t), t)

        def scores(i, j, buf):
            s_ref, top_ref = buf
            s = _nt(k_ref[rows(j), :], q_ref[rows(i), :])
            ahead = lax.broadcasted_iota(jnp.int32, (t, t), 0) - lax.broadcasted_iota(jnp.int32, (t, t), 1)
            s = jnp.where(ahead <= (i - j) * t, s, -jnp.inf)
            s_ref[...] = s
            top_ref[...] = jnp.max(s, axis=0, keepdims=True)

        def absorb(i, j, buf):
            s_ref, top_ref = buf
            first = j == 0
            m = jnp.where(first, -jnp.inf, m_all[i])
            l = jnp.where(first, 0.0, l_all[i])
            acc = jnp.where(first, 0.0, acc_all[i])
            m_new = jnp.maximum(m, top_ref[...])
            alpha = jnp.exp2(m - m_new)
            p = jnp.exp2((s_ref[...] - m_new).astype(BF16))
            m_all[i] = m_new
            l_all[i] = alpha * l + _nn(ones, p)[0:1, :]
            acc_all[i] = alpha * acc + _tn(v_ref[rows(j), :], p)

        def trip(width):
            def walk(_, pair):
                i, j = pair
                for w in range(width):
                    done = j == i
                    ni, nj = jnp.where(done, i + 1, i), jnp.where(done, 0, j + 1)
                    scores(jnp.minimum(ni, n - 1), nj, bufs[(w + 1) % 2])
                    absorb(i, j, bufs[w % 2])
                    i, j = ni, nj
                return i, j
            return walk

        scores(0, 0, bufs[0])
        pair = lax.fori_loop(0, n_pairs // ATT_UNROLL, trip(ATT_UNROLL), (jnp.int32(0), jnp.int32(0)))
        if n_pairs % ATT_UNROLL:
            trip(n_pairs % ATT_UNROLL)(0, pair)

        def finish(i, carry):
            l = l_all[i]
            o = (acc_all[i] * (1.0 / l)).T
            lse_ref[:, rows(i)] = m_all[i] + jnp.log2(l)
            o_ref[rows(i), :] = o.astype(BF16)
            z = z_ref[rows(i), :].astype(F32)
            y_ref[rows(i), :] = (o * (z * _sigmoid(z))).astype(BF16)
            return carry

        lax.fori_loop(0, n, finish, 0)

    def col(width, cb):
        return pl.BlockSpec((seq, width), lambda h: (0, cb + h))

    return pl.pallas_call(
        body, name="flash_fwd", grid=(N_HEADS,),
        in_specs=[col(HEAD_PAD, 0), col(HEAD_PAD, 0), col(V_HEAD, 0), col(V_HEAD, za_blk)],
        out_specs=[col(V_HEAD, 0), col(V_HEAD, 0), pl.BlockSpec((None, 1, seq), lambda h: (h, 0, 0))],
        out_shape=[SDS((seq, D_ATTN), BF16), SDS((seq, D_ATTN), BF16), SDS((N_HEADS, 1, seq), F32)],
        scratch_shapes=[pltpu.VMEM((t, t), F32)] * 2 + [pltpu.VMEM((1, t), F32)] * 2
                       + [pltpu.VMEM((n, 1, t), F32)] * 2 + [pltpu.VMEM((n, V_HEAD, t), F32)],
        compiler_params=_cp(("arbitrary",), 52),
    )(q, k, v, u)


def _outproj_loss(y_conv, y_attn, x, target, gate, w_out):
    seq, dm = x.shape
    ts = min(OUT_T, seq)
    n = seq // ts
    dmix = w_out.shape[0]

    def body(yc_ref, ya_ref, x_ref, t_ref, gate_ref, wo_hbm, dout_ref, dy_ref, dyc_ref, stats_ref, wo_ref, sem, acc_ref):
        i = pl.program_id(0)

        @pl.when(i == 0)
        def _():
            cp = pltpu.make_async_copy(wo_hbm, wo_ref, sem)
            cp.start()
            cp.wait()
            acc_ref[...] = jnp.zeros_like(acc_ref)

        y = _nn(yc_ref[...], wo_ref[0:D_CONV, :]) + _nn(ya_ref[...], wo_ref[D_CONV:dmix, :])
        gate_v = gate_ref[...]
        diff = (x_ref[...] + gate_v * y) - t_ref[...]
        dout = diff * (1.0 / dm)
        dout_ref[...] = dout
        acc_ref[0:8, :] += jnp.sum((dout * y).reshape(ts // 8, 8, dm), axis=0)
        acc_ref[8:16, :] += jnp.sum((diff * diff).reshape(ts // 8, 8, dm), axis=0)
        dy = (dout * gate_v).astype(BF16)
        dy_ref[...] = dy
        dyc_ref[...] = _nt(dy, wo_ref[...]).astype(BF16)

        @pl.when(i == n - 1)
        def _():
            stats_ref[...] = jnp.zeros_like(stats_ref)
            stats_ref[0:1, :] = jnp.sum(acc_ref[0:8, :], axis=0, keepdims=True)
            loss = jnp.sum(acc_ref[8:16, :]) * (0.5 / dm)
            stats_ref[1:2, :] = jnp.full((1, dm), loss, F32)

    row = pl.BlockSpec((ts, dm), lambda i: (i, 0))
    half = pl.BlockSpec((ts, D_CONV), lambda i: (i, 0))
    return pl.pallas_call(
        body, name="outproj_loss", grid=(n,),
        in_specs=[half, half, row, row, pl.BlockSpec((1, dm), lambda i: (0, 0)), ANY],
        out_specs=[row, row, pl.BlockSpec((ts, dmix), lambda i: (i, 0)), pl.BlockSpec((8, dm), lambda i: (0, 0))],
        out_shape=[SDS((seq, dm), F32), SDS((seq, dm), BF16), SDS((seq, dmix), BF16), SDS((8, dm), F32)],
        scratch_shapes=[pltpu.VMEM(w_out.shape, BF16), pltpu.SemaphoreType.DMA(()), pltpu.VMEM((16, dm), F32)],
        compiler_params=_cp(("arbitrary",), 52),
    )(y_conv, y_attn, x, target, gate, w_out)


def _matmul_tn(a, b, name):
    seq, m = a.shape
    n = b.shape[1]
    tm, tn, tk = min(TN_TM, m), min(TN_TN, n), min(TN_TK, seq)
    nk = seq // tk

    def body(a_ref, b_ref, o_ref, acc_ref):
        kk = pl.program_id(2)

        @pl.when(kk == 0)
        def _():
            acc_ref[...] = jnp.zeros_like(acc_ref)

        acc_ref[...] += _tn(a_ref[...], b_ref[...])

        @pl.when(kk == nk - 1)
        def _():
            o_ref[...] = acc_ref[...].astype(BF16)

    return pl.pallas_call(
        body, name=name, grid=(m // tm, n // tn, nk),
        in_specs=[pl.BlockSpec((tk, tm), lambda i, j, kk: (kk, i)), pl.BlockSpec((tk, tn), lambda i, j, kk: (kk, j))],
        out_specs=pl.BlockSpec((tm, tn), lambda i, j, kk: (i, j)), out_shape=SDS((m, n), BF16),
        scratch_shapes=[pltpu.VMEM((tm, tn), F32)],
        compiler_params=_cp(("arbitrary", "arbitrary", "arbitrary"), 40),
    )(a, b)


def _attn_gate_bwd(dycat, o, u):
    seq = o.shape[0]
    ts = min(ROW_T, seq)

    def body(dy_ref, o_ref, z_ref, dot_ref, dz_ref, dl_ref):
        dy = dy_ref[...].astype(F32)
        ov = o_ref[...].astype(F32)
        z = z_ref[...].astype(F32)
        sg = _sigmoid(z)
        do = dy * (z * sg)
        dz_ref[...] = (dy * ov * _silu_grad(z, sg)).astype(BF16)
        prod = do * ov
        ones = jnp.ones((8, V_HEAD), F32)
        for h in range(N_HEADS):
            cols = slice(h * V_HEAD, (h + 1) * V_HEAD)
            dot_ref[h] = do[:, cols].T.astype(BF16)
            rows = lax.dot_general(ones, prod[:, cols], (((1,), (1,)), ((), ())), precision=lax.Precision.HIGHEST,
                                   preferred_element_type=F32)
            dl_ref[h] = rows[0:1, :]

    blk = pl.BlockSpec((ts, D_ATTN), lambda i: (i, 0))
    return pl.pallas_call(
        body, name="attn_gate_bwd", grid=(seq // ts,),
        in_specs=[pl.BlockSpec((ts, D_ATTN), lambda i: (i, 1)), blk, pl.BlockSpec((ts, D_ATTN), lambda i: (i, U_ZA // D_ATTN))],
        out_specs=[pl.BlockSpec((N_HEADS, V_HEAD, ts), lambda i: (0, 0, i)), blk,
                   pl.BlockSpec((N_HEADS, 1, ts), lambda i: (0, 0, i))],
        out_shape=[SDS((N_HEADS, V_HEAD, seq), BF16), SDS((seq, D_ATTN), BF16), SDS((N_HEADS, 1, seq), F32)],
        compiler_params=_cp(("arbitrary",), 40),
    )(dycat, o, u)


def _flash_bwd(q, k, v, do_t, lse, delta):
    seq = q.shape[0]
    t = min(ATT_T, seq)
    n = seq // t
    n_pairs = n * (n + 1) // 2

    def body(k_ref, v_ref, q_ref, dot_ref, lse_ref, dl_ref, dq_ref, dk_ref, dv_ref, s_a, s_b, dp_a, dp_b, dq_acc, dk_acc,
             dvt_acc):
        bufs = ((s_a, dp_a), (s_b, dp_b))

        def rows(i):
            return pl.ds(pl.multiple_of(i * t, t), t)

        def products(j, i, buf):
            s_ref, dp_ref = buf
            s = _nt(k_ref[rows(j), :], q_ref[rows(i), :])
            ahead = lax.broadcasted_iota(jnp.int32, (t, t), 0) - lax.broadcasted_iota(jnp.int32, (t, t), 1)
            s_ref[...] = jnp.where(ahead <= (i - j) * t, s, -jnp.inf)
            dp_ref[...] = _nn(v_ref[rows(j), :], dot_ref[:, rows(i)])

        def absorb(j, i, buf):
            s_ref, dp_ref = buf
            first = i == j
            p = jnp.exp2((s_ref[...] - lse_ref[:, rows(i)]).astype(BF16))
            dvt = jnp.where(first, 0.0, dvt_acc[...]) + _nt(dot_ref[:, rows(i)], p)
            ds = p * (dp_ref[...] - dl_ref[:, rows(i)]).astype(BF16)
            dk = jnp.where(first, 0.0, dk_acc[...]) + _nn(ds, q_ref[rows(i), :])
            dq_acc[rows(i), :] += _tn(ds, k_ref[rows(j), :])
            dvt_acc[...] = dvt
            dk_acc[...] = dk
            dk_ref[rows(j), :] = dk.astype(BF16)
            dv_ref[rows(j), :] = dvt.T.astype(BF16)

        def trip(width):
            def walk(_, pair):
                j, i = pair
                for w in range(width):
                    done = i == n - 1
                    nj = jnp.where(done, j + 1, j)
                    ni = jnp.where(done, j + 1, i + 1)
                    absorb(j, i, bufs[w % 2])
                    products(jnp.minimum(nj, n - 1), jnp.minimum(ni, n - 1), bufs[(w + 1) % 2])
                    j, i = nj, ni
                return j, i
            return walk

        dq_acc[...] = jnp.zeros_like(dq_acc)
        products(0, 0, bufs[0])
        pair = lax.fori_loop(0, n_pairs // ATT_UNROLL, trip(ATT_UNROLL), (jnp.int32(0), jnp.int32(0)))
        if n_pairs % ATT_UNROLL:
            trip(n_pairs % ATT_UNROLL)(0, pair)

        def finish(i, carry):
            dq_ref[rows(i), :] = dq_acc[rows(i), :].astype(BF16)
            return carry

        lax.fori_loop(0, n, finish, 0)

    def col(width):
        return pl.BlockSpec((seq, width), lambda h: (0, h))

    row = pl.BlockSpec((None, 1, seq), lambda h: (h, 0, 0))
    return pl.pallas_call(
        body, name="flash_bwd", grid=(N_HEADS,),
        in_specs=[col(HEAD_PAD), col(V_HEAD), col(HEAD_PAD), pl.BlockSpec((None, V_HEAD, seq), lambda h: (h, 0, 0)), row, row],
        out_specs=[col(HEAD_PAD), col(HEAD_PAD), col(V_HEAD)],
        out_shape=[SDS((seq, Q_PAD), BF16), SDS((seq, Q_PAD), BF16), SDS((seq, D_ATTN), BF16)],
        scratch_shapes=[pltpu.VMEM((t, t), F32)] * 4
                       + [pltpu.VMEM((seq, HEAD_PAD), F32), pltpu.VMEM((t, HEAD_PAD), F32), pltpu.VMEM((V_HEAD, t), F32)],
        compiler_params=_cp(("arbitrary",), 60),
    )(k, v, q, do_t, lse, delta)


SG_QAG, SG_KVAG, SG_QG, SG_KG, SG_COLS = 0, Q_LORA, Q_LORA + KV_LORA, Q_LORA + KV_LORA + HEAD_PAD, D_MODEL


def _mla_bwd(dq, dk, dv, u, pos, freq, q_a_g, wq, kv_a_g, wkn, wv, q_g, k_g):
    seq = u.shape[0]
    ts = min(ROW_T, seq)
    n = seq // ts
    qscale = 1.0 / math.sqrt(QK_HEAD)

    def body(dq_ref, dk_ref, dv_ref, cq_ref, ckv_ref, kr_ref, pos_ref, freq_ref, qag_ref, wq_ref, kvag_ref, wkn_ref,
             wv_ref, qg_ref, kg_ref, du_ref, dwq_ref, dwkn_ref, dwv_ref, sg_ref, dqp_ref, dkn_ref):
        i = pl.program_id(0)

        @pl.when(i == 0)
        def _():
            dwq_ref[...] = jnp.zeros_like(dwq_ref)
            dwkn_ref[...] = jnp.zeros_like(dwkn_ref)
            dwv_ref[...] = jnp.zeros_like(dwv_ref)
            sg_ref[...] = jnp.zeros_like(sg_ref)

        tables = _rope_tables(pos_ref, freq_ref)

        cq = cq_ref[...].astype(F32)
        cqn, rq = _rms(cq, Q_LORA)
        qag = qag_ref[...]
        cqb = (cqn * qag).astype(BF16)
        qp = _nn(cqb, wq_ref[...])
        qg = qg_ref[...]
        dqg = jnp.zeros((1, HEAD_PAD), F32)
        for h in range(N_HEADS):
            lo = h * HEAD_PAD
            xn, r = _rms(qp[:, lo:lo + HEAD_PAD], QK_HEAD)
            g = jnp.concatenate([dq_ref[:, lo:lo + LANES].astype(F32),
                                 _rope_bwd(dq_ref[:, lo + LANES:lo + HEAD_PAD].astype(F32), tables)], axis=-1) * qscale
            dqg = dqg + jnp.sum(g * xn, axis=0, keepdims=True)
            gy = g * qg
            mean = jnp.sum(gy * xn, axis=-1, keepdims=True) * (1.0 / QK_HEAD)
            dqp_ref[:, lo:lo + HEAD_PAD] = (r * (gy - xn * mean)).astype(BF16)
        dqp = dqp_ref[...]
        dwq_ref[...] += _tn(cqb, dqp)
        dcqn = _nt(dqp, wq_ref[...])
        sg_ref[0:1, SG_QAG:SG_QAG + Q_LORA] += jnp.sum(dcqn * cqn, axis=0, keepdims=True)
        sg_ref[0:1, SG_QG:SG_QG + HEAD_PAD] += dqg
        gy = dcqn * qag
        mean = jnp.sum(gy * cqn, axis=-1, keepdims=True) * (1.0 / Q_LORA)
        du_ref[:, 0:Q_LORA] = (rq * (gy - cqn * mean)).astype(BF16)

        ckv = ckv_ref[...].astype(F32)
        ckvn, rkv = _rms(ckv, KV_LORA)
        kvag = kvag_ref[...]
        ckvb = (ckvn * kvag).astype(BF16)
        kn = _nn(ckvb, wkn_ref[...])
        kr = kr_ref[:, 0:LANES].astype(F32)
        ssr = jnp.sum(kr * kr, axis=-1, keepdims=True)
        kg = kg_ref[...]
        kg_n, kg_r = kg[:, :LANES] * LN2, kg[:, LANES:] * LN2
        dkg_n = jnp.zeros((1, LANES), F32)
        dkg_r = jnp.zeros((1, LANES), F32)
        dkr = jnp.zeros((ts, LANES), F32)
        for h in range(N_HEADS):
            knh = kn[:, h * QK_NOPE:(h + 1) * QK_NOPE]
            r = lax.rsqrt((jnp.sum(knh * knh, axis=-1, keepdims=True) + ssr) * (1.0 / QK_HEAD) + EPS)
            xn_n, xn_r = knh * r, kr * r
            lo = h * HEAD_PAD
            g_n = dk_ref[:, lo:lo + LANES].astype(F32)
            g_r = _rope_bwd(dk_ref[:, lo + LANES:lo + HEAD_PAD].astype(F32), tables)
            dkg_n = dkg_n + jnp.sum(g_n * xn_n, axis=0, keepdims=True)
            dkg_r = dkg_r + jnp.sum(g_r * xn_r, axis=0, keepdims=True)
            gy_n, gy_r = g_n * kg_n, g_r * kg_r
            mean = (jnp.sum(gy_n * xn_n, axis=-1, keepdims=True) + jnp.sum(gy_r * xn_r, axis=-1, keepdims=True)) * (1.0 / QK_HEAD)
            dkn_ref[:, h * QK_NOPE:(h + 1) * QK_NOPE] = (r * (gy_n - xn_n * mean)).astype(BF16)
            dkr = dkr + r * (gy_r - xn_r * mean)
        dkn = dkn_ref[...]
        dvv = dv_ref[...]
        dwkn_ref[...] += _tn(ckvb, dkn)
        dwv_ref[...] += _tn(ckvb, dvv)
        dckvn = _nt(dkn, wkn_ref[...]) + _nt(dvv, wv_ref[...])
        sg_ref[0:1, SG_KVAG:SG_KVAG + KV_LORA] += jnp.sum(dckvn * ckvn, axis=0, keepdims=True)
        sg_ref[0:1, SG_KG:SG_KG + LANES] += dkg_n * LN2
        sg_ref[0:1, SG_KG + LANES:SG_KG + HEAD_PAD] += dkg_r * LN2
        gy = dckvn * kvag
        mean = jnp.sum(gy * ckvn, axis=-1, keepdims=True) * (1.0 / KV_LORA)
        du_ref[:, Q_LORA:Q_LORA + KV_LORA] = (rkv * (gy - ckvn * mean)).astype(BF16)
        du_ref[:, Q_LORA + KV_LORA:Q_LORA + KV_LORA + LANES] = dkr.astype(BF16)
        du_ref[:, Q_LORA + KV_LORA + LANES:MLA_COLS] = jnp.zeros((ts, LANES), BF16)

    def full(a):
        return pl.BlockSpec(a.shape, lambda i: (0,) * a.ndim)

    wide = pl.BlockSpec((ts, Q_PAD), lambda i: (i, 0))
    return pl.pallas_call(
        body, name="mla_bwd", grid=(n,),
        in_specs=[wide, wide, pl.BlockSpec((ts, D_ATTN), lambda i: (i, 0)),
                  pl.BlockSpec((ts, Q_LORA), lambda i: (i, U_CQ // Q_LORA)),
                  pl.BlockSpec((ts, KV_LORA), lambda i: (i, U_CKV // KV_LORA)),
                  pl.BlockSpec((ts, KR_PAD), lambda i: (i, U_KR // KR_PAD)),
                  pl.BlockSpec((ts, 1), lambda i: (i, 0)), full(freq), full(q_a_g), full(wq), full(kv_a_g), full(wkn),
                  full(wv), full(q_g), full(k_g)],
        out_specs=[pl.BlockSpec((ts, MLA_COLS), lambda i: (i, 0)), pl.BlockSpec((Q_LORA, Q_PAD), lambda i: (0, 0)),
                   pl.BlockSpec((KV_LORA, D_ATTN), lambda i: (0, 0)), pl.BlockSpec((KV_LORA, D_ATTN), lambda i: (0, 0)),
                   pl.BlockSpec((8, SG_COLS), lambda i: (0, 0))],
        out_shape=[SDS((seq, MLA_COLS), BF16), SDS((Q_LORA, Q_PAD), F32), SDS((KV_LORA, D_ATTN), F32),
                   SDS((KV_LORA, D_ATTN), F32), SDS((8, SG_COLS), F32)],
        scratch_shapes=[pltpu.VMEM((ts, Q_PAD), BF16), pltpu.VMEM((ts, D_ATTN), BF16)],
        compiler_params=_cp(("arbitrary",), 56),
    )(dq, dk, dv, u, u, u, pos, freq, q_a_g, wq, kv_a_g, wkn, wv, q_g, k_g)


def _conv_bwd(dycat, u, conv_w):
    seq = u.shape[0]
    ts = min(ROW_T, seq)
    n = seq // ts
    hb = ts // HALO

    def body(dy_ref, xc_ref, bc_ref, cc_ref, zc_ref, xp_ref, cp_ref, dyn_ref, bn_ref, zn_ref, w_ref,
             du_ref, dw_ref, ext_ref, dext_ref, acc_ref):
        i = pl.program_id(0)

        @pl.when(i == 0)
        def _():
            dw_ref[...] = jnp.zeros_like(dw_ref)

        up = cp_ref[...].astype(F32) * xp_ref[...].astype(F32)
        ext_ref[0:HALO, :] = jnp.where(i > 0, up, 0.0)
        ext_ref[HALO:HALO + ts, :] = cc_ref[...].astype(F32) * xc_ref[...].astype(F32)
        zn = zn_ref[...].astype(F32)
        dnext = dyn_ref[...].astype(F32) * (zn * _sigmoid(zn)) * bn_ref[...].astype(F32)
        dext_ref[ts:ts + HALO, :] = jnp.where(i < n - 1, dnext, 0.0)
        acc_ref[...] = jnp.zeros_like(acc_ref)

        for r0 in range(0, ts, CHUNK_ROWS):
            rows = slice(r0, r0 + CHUNK_ROWS)
            for c0 in range(0, D_CONV, CHUNK_LANES):
                cols = slice(c0, c0 + CHUNK_LANES)
                uc = ext_ref[HALO + r0:HALO + r0 + CHUNK_ROWS, cols]
                u1 = ext_ref[HALO - 1 + r0:HALO - 1 + r0 + CHUNK_ROWS, cols]
                u2 = ext_ref[HALO - 2 + r0:HALO - 2 + r0 + CHUNK_ROWS, cols]
                conv = w_ref[0:1, cols] * u2 + w_ref[1:2, cols] * u1 + w_ref[2:3, cols] * uc
                z = zc_ref[rows, cols].astype(F32)
                sg = _sigmoid(z)
                sz = z * sg
                b = bc_ref[rows, cols].astype(F32)
                dy = dy_ref[rows, cols].astype(F32)
                du_ref[rows, 3 * D_CONV + c0:3 * D_CONV + c0 + CHUNK_LANES] = (dy * (b * conv) * _silu_grad(z, sg)).astype(BF16)
                du_ref[rows, D_CONV + c0:D_CONV + c0 + CHUNK_LANES] = (dy * sz * conv).astype(BF16)
                dconv = dy * sz * b
                dext_ref[rows, cols] = dconv
                acc_ref[0:CHUNK_ROWS, cols] += dconv * u2
                acc_ref[CHUNK_ROWS:2 * CHUNK_ROWS, cols] += dconv * u1
                acc_ref[2 * CHUNK_ROWS:3 * CHUNK_ROWS, cols] += dconv * uc
        for r0 in range(0, ts, CHUNK_ROWS):
            rows = slice(r0, r0 + CHUNK_ROWS)
            for c0 in range(0, D_CONV, CHUNK_LANES):
                cols = slice(c0, c0 + CHUNK_LANES)
                du = (w_ref[2:3, cols] * dext_ref[rows, cols] + w_ref[1:2, cols] * dext_ref[r0 + 1:r0 + 1 + CHUNK_ROWS, cols]
                      + w_ref[0:1, cols] * dext_ref[r0 + 2:r0 + 2 + CHUNK_ROWS, cols])
                du_ref[rows, 2 * D_CONV + c0:2 * D_CONV + c0 + CHUNK_LANES] = (du * xc_ref[rows, cols].astype(F32)).astype(BF16)
                du_ref[rows, c0:c0 + CHUNK_LANES] = (du * cc_ref[rows, cols].astype(F32)).astype(BF16)
        for k in range(3):
            dw_ref[k:k + 1, :] += jnp.sum(acc_ref[k * CHUNK_ROWS:(k + 1) * CHUNK_ROWS, :], axis=0, keepdims=True)

    def col(cb):
        return pl.BlockSpec((ts, D_CONV), lambda i: (i, cb))

    def prev(cb):
        return pl.BlockSpec((HALO, D_CONV), lambda i: (jnp.maximum(i * hb - 1, 0), cb))

    def nxt(cb):
        return pl.BlockSpec((HALO, D_CONV), lambda i: (jnp.minimum((i + 1) * hb, n * hb - 1), cb))

    return pl.pallas_call(
        body, name="conv_bwd", grid=(n,),
        in_specs=[col(0), col(0), col(1), col(2), col(3), prev(0), prev(2), nxt(0), nxt(1), nxt(3),
                  pl.BlockSpec((3, D_CONV), lambda i: (0, 0))],
        out_specs=[pl.BlockSpec((ts, 4 * D_CONV), lambda i: (i, 0)), pl.BlockSpec((8, D_CONV), lambda i: (0, 0))],
        out_shape=[SDS((seq, 4 * D_CONV), BF16), SDS((8, D_CONV), F32)],
        scratch_shapes=[pltpu.VMEM((ts + HALO, D_CONV), F32), pltpu.VMEM((ts + HALO, D_CONV), F32),
                        pltpu.VMEM((3 * CHUNK_ROWS, D_CONV), F32)],
        compiler_params=_cp(("arbitrary",), 48),
    )(dycat, u, u, u, u, u, u, dycat, u, u, conv_w)


def _inproj_bwd(du_conv, du_za, du_mla, w_t, parts):
    seq = du_conv.shape[0]
    dm = w_t.shape[1]
    tm, tn = min(DH_TM, seq), DH_TN
    ni, nj = seq // tm, dm // tn
    na = len(parts)

    def body(dc_ref, dz_ref, dm_ref, w_ref, *rest):
        part_refs, o_ref, recv_refs = rest[:na], rest[na], rest[na + 1:2 * na + 1]
        ssem, rsem = rest[2 * na + 1:]
        i, j = pl.program_id(0), pl.program_id(1)
        sends, recvs = _chip_exchange_copies(part_refs, recv_refs, ssem, rsem)

        @pl.when((i == 0) & (j == 0))
        def _():
            for cp in sends:
                cp.start()

        acc = _nn(dc_ref[...], w_ref[0:U_ZA, :])
        acc = acc + _nn(dz_ref[...], w_ref[U_ZA:U_CQ, :])
        acc = acc + _nn(dm_ref[...], w_ref[U_CQ:U_COLS, :])
        o_ref[...] = acc.astype(BF16)

        @pl.when((i == ni - 1) & (j == nj - 1))
        def _():
            for cp in recvs:
                cp.wait_recv()
            for cp in sends:
                cp.wait_send()

    outs = pl.pallas_call(
        body, name="inproj_bwd", grid=(ni, nj),
        in_specs=[pl.BlockSpec((tm, U_ZA), lambda i, j: (i, 0)), pl.BlockSpec((tm, D_ATTN), lambda i, j: (i, 0)),
                  pl.BlockSpec((tm, MLA_COLS), lambda i, j: (i, 0)), pl.BlockSpec((U_COLS, tn), lambda i, j: (0, j))]
                 + [ANY] * na,
        out_specs=[pl.BlockSpec((tm, tn), lambda i, j: (i, j))] + [ANY] * na,
        out_shape=[SDS((seq, dm), BF16)] + [SDS(p.shape, p.dtype) for p in parts],
        scratch_shapes=[pltpu.SemaphoreType.DMA((3 * na,))] * 2,
        compiler_params=_cp(("arbitrary", "arbitrary"), 48),
    )(du_conv, du_za, du_mla, w_t, *parts)
    return outs[0], outs[1:]


def _prenorm_bwd(x, dh, dout, norm_g, scale):
    seq, dm = x.shape
    ts = min(ROW_T, seq)
    n = seq // ts

    def body(x_ref, dh_ref, dout_ref, g_ref, sc_ref, gx_ref, st_ref, acc_ref):
        i = pl.program_id(0)

        @pl.when(i == 0)
        def _():
            acc_ref[...] = jnp.zeros_like(acc_ref)

        xv = x_ref[...]
        xn, r = _rms(xv, dm)
        dh_v = dh_ref[...].astype(F32)
        gv = g_ref[...]
        one_sc = 1.0 + sc_ref[...]

        def fold(a):
            return jnp.sum(a.reshape(ts // 8, 8, dm), axis=0)

        acc_ref[0:8, :] += fold(dh_v)
        acc_ref[8:16, :] += fold(dh_v * (xn * gv))
        dxg = dh_v * one_sc
        acc_ref[16:24, :] += fold(dxg * xn)
        dxn = dxg * gv
        mean = jnp.sum(dxn * xn, axis=-1, keepdims=True) * (1.0 / dm)
        gx_ref[...] = dout_ref[...] + r * (dxn - xn * mean)

        @pl.when(i == n - 1)
        def _():
            st_ref[...] = jnp.zeros_like(st_ref)
            for k in range(3):
                st_ref[k:k + 1, :] = jnp.sum(acc_ref[8 * k:8 * k + 8, :], axis=0, keepdims=True)

    row = pl.BlockSpec((ts, dm), lambda i: (i, 0))
    vec = pl.BlockSpec((1, dm), lambda i: (0, 0))
    return pl.pallas_call(
        body, name="prenorm_bwd", grid=(n,), in_specs=[row, row, row, vec, vec],
        out_specs=[row, pl.BlockSpec((8, dm), lambda i: (0, 0))],
        out_shape=[SDS((seq, dm), F32), SDS((8, dm), F32)],
        scratch_shapes=[pltpu.VMEM((24, dm), F32)], input_output_aliases={2: 0},
        compiler_params=_cp(("arbitrary",), 52),
    )(x, dh, dout, norm_g, scale)


def _unshard_cols(g):
    return jnp.transpose(g, (1, 0, 2)).reshape(g.shape[1], -1)


def _shard_cols(w):
    r = w.shape[0]
    return jnp.transpose(w.reshape(r, N_CHIPS, -1), (1, 0, 2))


W_IN_COLS = 4 * D_CONV + Q_LORA + KV_LORA + QK_ROPE + D_ATTN
SHARD_ROWS = W_IN_COLS // N_CHIPS
SHARD_PAD = 1536


def _w_in_pieces():
    c4 = 4 * D_CONV
    groups = [(0, c4, 0), (c4, c4 + Q_LORA, U_CQ), (c4 + Q_LORA, c4 + Q_LORA + KV_LORA, U_CKV),
              (c4 + Q_LORA + KV_LORA, W_IN_COLS - D_ATTN, U_KR), (W_IN_COLS - D_ATTN, W_IN_COLS, U_ZA)]
    pieces = []
    for lo, hi, my in groups:
        for chip in range(N_CHIPS):
            a, b = max(lo, chip * SHARD_ROWS), min(hi, (chip + 1) * SHARD_ROWS)
            if a < b:
                pieces.append((chip, a - chip * SHARD_ROWS, b - a, my + a - lo))
    return pieces


def _w_t_to_my(g):
    w = jnp.zeros((U_COLS, g.shape[2]), g.dtype)
    for chip, row, n, my in _w_in_pieces():
        w = lax.dynamic_update_slice(w, g[chip, row:row + n], (my, 0))
    return w


def _w_t_from_my(g_conv, g_za, g_mla):
    w = jnp.zeros((N_CHIPS, SHARD_PAD, g_conv.shape[1]), g_conv.dtype)
    for chip, row, n, my in _w_in_pieces():
        src, base = (g_conv, 0) if my < U_ZA else (g_za, U_ZA) if my < U_CQ else (g_mla, U_CQ)
        w = lax.dynamic_update_slice(w, src[my - base:my - base + n][None], (chip, row, 0))
    return w


def _heads_pad(w):
    r = w.shape[0]
    w3 = w.reshape(r, N_HEADS, QK_HEAD)
    return jnp.pad(w3, ((0, 0), (0, 0), (0, HEAD_PAD - QK_HEAD))).reshape(r, Q_PAD)


def _heads_unpad(w):
    r = w.shape[0]
    return w.reshape(r, N_HEADS, HEAD_PAD)[:, :, :QK_HEAD].reshape(r, N_HEADS * QK_HEAD)


def kernel(x, c, positions, ada_w, ada_b, norm_g, w_in, conv_w, q_a_g, w_q_b, kv_a_g, w_kv_b, q_g, k_g, w_out, loss_target, m_ada_w, m_ada_b, m_norm_g, m_w_in, m_conv_w, m_q_a_g, m_w_q_b, m_kv_a_g, m_w_kv_b, m_q_g, m_k_g, m_w_out, v_ada_w, v_ada_b, v_norm_g, v_w_in, v_conv_w, v_q_a_g, v_w_q_b, v_kv_a_g, v_w_kv_b, v_q_g, v_k_g, v_w_out):
    mx, my, mc = _place()
    chip = 2 * mx + my
    me = 2 * chip + mc
    seq = x.shape[1]
    x2, t2 = x[0], loss_target[0]
    cw_cols = conv_w.shape[2]

    small = jnp.zeros((8, D_MODEL), F32)
    small = small.at[0].set(c[0])
    small = small.at[1:4, :cw_cols].set(conv_w[0])
    small_all = _gather8(small, "gather_c_conv", False)[0]
    c_all = small_all[:, 0, :]
    conv_full = jnp.transpose(small_all.reshape(N_CHIPS, 2, 8, D_MODEL)[:, 0, 1:4, :cw_cols], (1, 0, 2)).reshape(3, D_CONV)

    ada_cols = ada_w.shape[2]
    b_k = lax.dynamic_slice(ada_b, (0, chip * ada_cols), (1, ada_cols))
    mod_k, sc_all = _ada_mod(c_all, ada_w[0], b_k)
    mod_all = _gather8(mod_k, "gather_mod", False)[0]
    mod_row = lax.dynamic_slice(mod_all.reshape(N_CHIPS, 2, N_DEV, ada_cols), (0, mc, me, 0), (N_CHIPS, 1, 1, ada_cols))
    mod_row = mod_row.reshape(3, D_MODEL)
    shift, scale, gate = mod_row[0:1], mod_row[1:2], mod_row[2:3]

    def own_slot(g, s):
        return lax.dynamic_update_slice(g, s[None], (chip, 0, 0))

    w_in_t, m_w_in_t, v_w_in_t = [jnp.transpose(a[0]) for a in (w_in, m_w_in, v_w_in)]
    shard_in = jnp.pad(w_in_t.astype(BF16), ((0, SHARD_PAD - SHARD_ROWS), (0, 0)))
    h, g_in = _prenorm_gather(x2, norm_g, scale, shift, shard_in)
    g_in = own_slot(g_in, shard_in)
    w_t = _w_t_to_my(g_in)

    later = [w_q_b[0].astype(BF16), w_kv_b[0].astype(BF16), w_out[0].astype(BF16)]
    u, got = _inproj(h, w_t, later)
    g_q, g_kv, g_out = [own_slot(g, s) for g, s in zip(got, later)]
    wq = _heads_pad(_unshard_cols(g_q))
    wkv = _unshard_cols(g_kv).reshape(KV_LORA, N_HEADS, QK_NOPE + V_HEAD)
    wkn = wkv[:, :, :QK_NOPE].reshape(KV_LORA, N_HEADS * QK_NOPE)
    wv = wkv[:, :, QK_NOPE:].reshape(KV_LORA, D_ATTN)
    wo = g_out.reshape(N_CHIPS * g_out.shape[1], D_MODEL)
    y_conv = _conv_fwd(u, conv_full)
    pos = positions.reshape(seq, 1)
    inv_freq = ROPE_BASE ** (-jnp.arange(0, QK_ROPE, 2, dtype=F32) / QK_ROPE)
    freq = jnp.concatenate([inv_freq, inv_freq, jnp.zeros((LANES - QK_ROPE,), F32)]).reshape(1, LANES)
    q_g_pad = jnp.pad(q_g, ((0, 0), (0, HEAD_PAD - QK_HEAD)))
    k_g_pad = jnp.pad(k_g, ((0, 0), (0, HEAD_PAD - QK_HEAD)))
    q, k, v = _mla_prep(u, pos, freq, q_a_g, wq, kv_a_g, wkn, wv, q_g_pad, k_g_pad)
    o, y_attn, lse = _flash_fwd(q, k, v, u)
    dout, dy, dycat, st_out = _outproj_loss(y_conv, y_attn, x2, t2, gate, wo)

    dw_out = jnp.concatenate([_matmul_tn(y_conv, dy, "dw_out_conv"), _matmul_tn(y_attn, dy, "dw_out_attn")], axis=0)
    do_t, du_za, delta = _attn_gate_bwd(dycat, o, u)
    dq, dk, dv = _flash_bwd(q, k, v, do_t, lse, delta)
    du_mla, dwq, dwkn, dwv, sg_mla = _mla_bwd(dq, dk, dv, u, pos, freq, q_a_g, wq, kv_a_g, wkn, wv, q_g_pad, k_g_pad)
    du_conv, dconv_w = _conv_bwd(dycat, u, conv_full)
    dw_conv = _matmul_tn(du_conv, h, "dw_in_conv")
    dw_za = _matmul_tn(du_za, h, "dw_in_za")
    dw_mla = _matmul_tn(du_mla, h, "dw_in_mla")

    dw_q_nat = _heads_unpad(dwq).astype(BF16)
    dw_kv_nat = jnp.concatenate([dwkn.reshape(KV_LORA, N_HEADS, QK_NOPE), dwv.reshape(KV_LORA, N_HEADS, V_HEAD)],
                                axis=2).reshape(KV_LORA, N_HEADS * (QK_NOPE + V_HEAD)).astype(BF16)
    grads = [_w_t_from_my(dw_conv, dw_za, dw_mla), _shard_cols(dw_q_nat), _shard_cols(dw_kv_nat),
             dw_out.reshape(N_CHIPS, dw_out.shape[0] // N_CHIPS, D_MODEL)]
    theirs = _rs_core_swap(grads)
    names = ["w_in", "w_q_b", "w_kv_b", "w_out"]
    core = jnp.reshape(mc, (1,)).astype(jnp.int32)
    parts = [_add_half_bf16(g, b, core, "rs_add_" + nm) for g, b, nm in zip(grads, theirs, names)]
    dh, recv = _inproj_bwd(du_conv, du_za, du_mla, w_t, parts)
    recv = [lax.dynamic_update_slice(r, lax.dynamic_slice(p, (chip, 0, 0), (1,) + p.shape[1:]), (chip, 0, 0))
            for r, p in zip(recv, parts)]
    halves = [_sum_chips(p, "rs_sum_" + nm) for p, nm in zip(recv, names)]
    joined = _rs_core_join(halves)
    joined = [lax.dynamic_update_slice(j, hf[None], (mc, 0, 0)) for j, hf in zip(joined, halves)]
    g_big = [j.reshape(2 * j.shape[1], j.shape[2]) for j in joined]
    grad_x, st_in = _prenorm_bwd(x2, dh, dout, norm_g, scale)

    sgrad = jnp.zeros((8, D_MODEL), F32)
    sgrad = sgrad.at[0:2].set(st_in[0:2])
    sgrad = sgrad.at[2].set(st_out[0])
    sgrad = sgrad.at[3].set(st_in[2])
    sgrad = sgrad.at[4, :D_CONV].set(dconv_w[0]).at[4, D_CONV:].set(dconv_w[1])
    sgrad = sgrad.at[5, :D_CONV].set(dconv_w[2]).at[5, D_CONV:].set(sg_mla[0, :D_CONV])
    sgrad = sgrad.at[6, :HEAD_PAD].set(sg_mla[0, SG_KG:SG_KG + HEAD_PAD])
    sgrad = sgrad.at[7].set(st_out[1])
    sg_all, sg_sum = _gather8(sgrad, "gather_small_grads", True)
    loss = sg_sum[7, 0]
    g_ada_b = sg_sum[0:3].reshape(1, 3 * D_MODEL)
    g_norm_g = sg_sum[3:4]
    conv_sum = jnp.stack([sg_sum[4, :D_CONV], sg_sum[4, D_CONV:], sg_sum[5, :D_CONV]])
    g_conv_w = lax.dynamic_slice(conv_sum, (0, chip * cw_cols), (3, cw_cols))
    g_q_a_g = sg_sum[5:6, D_CONV + SG_QAG:D_CONV + SG_QAG + Q_LORA]
    g_kv_a_g = sg_sum[5:6, D_CONV + SG_KVAG:D_CONV + SG_KVAG + KV_LORA]
    g_q_g = sg_sum[5:6, D_CONV + SG_QG:D_CONV + SG_QG + QK_HEAD]
    g_k_g = sg_sum[6:7, :QK_HEAD]
    dmod_k = lax.dynamic_slice(sg_all[:, 0:3, :].reshape(N_DEV, 3 * D_MODEL), (0, chip * ada_cols), (N_DEV, ada_cols))

    g_ada_w, d_ada_w, nm_ada_w, nv_ada_w = _ada_w_update(sc_all, dmod_k, ada_w[0], m_ada_w[0], v_ada_w[0])
    upd = {}
    big = {"w_q_b": (w_q_b, m_w_q_b, v_w_q_b), "w_kv_b": (w_kv_b, m_w_kv_b, v_w_kv_b), "w_out": (w_out, m_w_out, v_w_out)}
    for nm, g in zip(names[1:], g_big[1:]):
        w_, m_, v_ = big[nm]
        upd[nm] = (g,) + tuple(_adamw(w_[0], g, m_[0], v_[0], "adamw_" + nm))
    d_t, nm_t, nv_t, g_t = _adamw(w_in_t, g_big[0], m_w_in_t, v_w_in_t, "adamw_w_in", echo_g=True)
    upd["w_in"] = tuple(jnp.transpose(a) for a in (g_t, d_t, nm_t, nv_t))
    small_w = {"ada_b": (ada_b, m_ada_b, v_ada_b, g_ada_b), "norm_g": (norm_g, m_norm_g, v_norm_g, g_norm_g),
               "conv_w": (conv_w[0], m_conv_w[0], v_conv_w[0], g_conv_w), "q_a_g": (q_a_g, m_q_a_g, v_q_a_g, g_q_a_g),
               "kv_a_g": (kv_a_g, m_kv_a_g, v_kv_a_g, g_kv_a_g), "q_g": (q_g, m_q_g, v_q_g, g_q_g),
               "k_g": (k_g, m_k_g, v_k_g, g_k_g)}
    for nm, (w_, m_, v_, g) in small_w.items():
        upd[nm] = (g,) + tuple(_adamw(w_, g, m_, v_, "adamw_" + nm))
    upd["ada_w"] = (g_ada_w, d_ada_w, nm_ada_w, nv_ada_w)

    order = ["ada_w", "ada_b", "norm_g", "w_in", "conv_w", "q_a_g", "w_q_b", "kv_a_g", "w_kv_b", "q_g", "k_g", "w_out"]
    lead1 = {"ada_w", "w_in", "conv_w", "w_q_b", "w_kv_b", "w_out"}

    def shaped(nm, a):
        return a[None] if nm in lead1 else a

    outs = [loss, grad_x[None]]
    for idx in range(4):
        outs += [shaped(nm, upd[nm][idx]) for nm in order]
    return tuple(outs)
```

```python
import math

import jax
import jax.numpy as jnp
from jax import lax
from jax.experimental import pallas as pl
from jax.experimental.pallas import tpu as pltpu

F32 = jnp.float32
BF16 = jnp.bfloat16
MESH = pl.DeviceIdType.MESH
SDS = jax.ShapeDtypeStruct
ANY = pl.BlockSpec(memory_space=pl.ANY)

D_MODEL = 2048
D_CONV = 1024
N_HEADS = 8
QK_NOPE = 128
QK_ROPE = 64
QK_HEAD = QK_NOPE + QK_ROPE
V_HEAD = 128
D_ATTN = N_HEADS * V_HEAD
Q_LORA = 512
KV_LORA = 256
ROPE_BASE = 10000.0
EPS = 1e-6
LOG2E = math.log2(math.e)
LN2 = math.log(2.0)
ADAM_LR, ADAM_B1, ADAM_B2, ADAM_EPS, ADAM_WD, ADAM_STEP = 0.001, 0.9, 0.999, 1e-08, 0.01, 10
N_CHIPS = 4
N_DEV = 8

LANES = 128
V7X_VMEM_BYTES = 64 * 1024 * 1024
MIB = 1024 * 1024

HEAD_PAD = 256
Q_PAD = N_HEADS * HEAD_PAD
U_ZA = 4 * D_CONV
U_CQ = U_ZA + D_ATTN
U_CKV = U_CQ + Q_LORA
U_KR = U_CKV + KV_LORA
KR_PAD = 256
U_COLS = U_KR + KR_PAD
MLA_COLS = Q_LORA + KV_LORA + KR_PAD

ATT_T = 512
INPROJ_TM, INPROJ_TN = 1024, 1024
ROW_T = 512
MLA_PREP_T = 1024
OUT_T = 256
DH_TM, DH_TN = 512, 1024
TN_TM, TN_TN, TN_TK = 1024, 1024, 2048
ATT_UNROLL = 8


def _cp(sem=None, vmem_mib=None, **kw):
    if sem is not None:
        kw["dimension_semantics"] = sem
    if vmem_mib is not None:
        kw["vmem_limit_bytes"] = min(vmem_mib * MIB, V7X_VMEM_BYTES - 4 * MIB)
    return pltpu.CompilerParams(**kw)


def _sigmoid(z):
    return 1.0 / (1.0 + jnp.exp(-z))


def _silu_grad(z, sg):
    return sg * (1.0 + z * (1.0 - sg))


def _nt(a, b):
    return lax.dot_general(a, b, (((1,), (1,)), ((), ())), preferred_element_type=F32)


def _tn(a, b):
    return lax.dot_general(a, b, (((0,), (0,)), ((), ())), preferred_element_type=F32)


def _nn(a, b):
    return jnp.dot(a, b, preferred_element_type=F32)


def _place():
    return lax.axis_index("x"), lax.axis_index("y"), lax.axis_index("c")


def _gather8(v, name, with_sum):
    rows, cols = v.shape

    def body(v_ref, out_ref, *rest):
        if with_sum:
            sum_ref, send_sems, recv_sems = rest
        else:
            send_sems, recv_sems = rest
        mx, my, mc = _place()
        me = 4 * mx + 2 * my + mc
        out_ref[me] = v_ref[...]
        peers = []
        for d in range(1, N_DEV):
            px = 1 - mx if d & 4 else mx
            py = 1 - my if d & 2 else my
            pc = 1 - mc if d & 1 else mc
            peers.append((px, py, pc))

        def copy(d, slot, to):
            return pltpu.make_async_remote_copy(
                src_ref=v_ref, dst_ref=out_ref.at[slot], send_sem=send_sems.at[d], recv_sem=recv_sems.at[d],
                device_id=to, device_id_type=MESH)

        sends = [copy(d, me, p) for d, p in enumerate(peers)]
        for cp in sends:
            cp.start()
        for d, (px, py, pc) in enumerate(peers):
            copy(d, 4 * px + 2 * py + pc, (px, py, pc)).wait_recv()
        for cp in sends:
            cp.wait_send()
        if with_sum:
            acc = out_ref[0]
            for b in range(1, N_DEV):
                acc = acc + out_ref[b]
            sum_ref[...] = acc

    out_shape = [SDS((N_DEV, rows, cols), F32)]
    if with_sum:
        out_shape.append(SDS((rows, cols), F32))
    vm = pl.BlockSpec(memory_space=pltpu.VMEM)
    return pl.pallas_call(
        body, name=name, out_shape=out_shape, in_specs=[vm], out_specs=[vm] * len(out_shape),
        scratch_shapes=[pltpu.SemaphoreType.DMA((N_DEV - 1,)), pltpu.SemaphoreType.DMA((N_DEV - 1,))],
    )(v)


def _chips_of(mx, my):
    chips = [(mx, 1 - my), (1 - mx, my), (1 - mx, 1 - my)]
    return chips, [2 * px + py for px, py in chips]


def _prenorm_gather(x, norm_g, scale, shift, shard):
    seq, dm = x.shape
    tm = min(INPROJ_TM, seq)
    ni = seq // tm
    half_rows = shard.shape[0] // 2

    def body(x_ref, g_ref, sc_ref, sh_ref, shard_ref, h_ref, got_ref, s1, r1, s2, r2):
        i = pl.program_id(0)
        mx, my, mc = _place()
        k = 2 * mx + my
        sib = (mx, my, 1 - mc)
        chips, kks = _chips_of(mx, my)

        def half(slot, c):
            return got_ref.at[slot, pl.ds(c * half_rows, half_rows)]

        def over_ici(d, slot):
            return pltpu.make_async_remote_copy(
                src_ref=shard_ref.at[pl.ds(mc * half_rows, half_rows)], dst_ref=half(slot, mc), send_sem=s1.at[d],
                recv_sem=r1.at[d], device_id=(chips[d][0], chips[d][1], mc), device_id_type=MESH)

        def to_sibling(d, c):
            return pltpu.make_async_remote_copy(
                src_ref=half(kks[d], c), dst_ref=half(kks[d], c), send_sem=s2.at[d], recv_sem=r2.at[d],
                device_id=sib, device_id_type=MESH)

        @pl.when(i == 0)
        def _():
            for d in range(3):
                over_ici(d, k).start()

        xv = x_ref[...]
        r = lax.rsqrt(jnp.mean(xv * xv, axis=-1, keepdims=True) + EPS)
        h_ref[...] = ((xv * r * g_ref[...]) * (1.0 + sc_ref[...]) + sh_ref[...]).astype(BF16)

        @pl.when(i == ni - 1)
        def _():
            for d in range(3):
                over_ici(d, kks[d]).wait_recv()
                to_sibling(d, mc).start()
            for d in range(3):
                to_sibling(d, 1 - mc).wait_recv()
            for d in range(3):
                over_ici(d, k).wait_send()
                to_sibling(d, mc).wait_send()

    vec = pl.BlockSpec((1, dm), lambda i: (0, 0))
    row = pl.BlockSpec((tm, dm), lambda i: (i, 0))
    return pl.pallas_call(
        body, name="prenorm_gather", grid=(ni,), in_specs=[row, vec, vec, vec, ANY], out_specs=[row, ANY],
        out_shape=[SDS((seq, dm), BF16), SDS((N_CHIPS,) + shard.shape, shard.dtype)],
        scratch_shapes=[pltpu.SemaphoreType.DMA((3,))] * 4,
        compiler_params=_cp(("arbitrary",), 48),
    )(x, norm_g, scale, shift, shard)


def _rs_core_swap(grads):
    na = len(grads)
    halves = [g.shape[1] // 2 for g in grads]

    def body(*refs):
        ins, outs = refs[:na], refs[na:2 * na]
        ssem, rsem = refs[2 * na:]
        mx, my, mc = _place()
        sib = (mx, my, 1 - mc)
        sends = []
        for a in range(na):
            cp = pltpu.make_async_remote_copy(
                src_ref=ins[a].at[:, pl.ds((1 - mc) * halves[a], halves[a])], dst_ref=outs[a],
                send_sem=ssem.at[a], recv_sem=rsem.at[a], device_id=sib, device_id_type=MESH)
            cp.start()
            sends.append(cp)
        for cp in sends:
            cp.wait_recv()
        for cp in sends:
            cp.wait_send()

    return pl.pallas_call(
        body, name="rs_core_swap", out_shape=[SDS((N_CHIPS, h) + g.shape[2:], g.dtype) for g, h in zip(grads, halves)],
        in_specs=[ANY] * na, out_specs=[ANY] * na,
        scratch_shapes=[pltpu.SemaphoreType.DMA((na,))] * 2,
    )(*grads)


def _chip_exchange_copies(ins, outs, ssem, rsem):
    mx, my, mc = _place()
    k = 2 * mx + my
    chips, kks = _chips_of(mx, my)
    sends, recvs = [], []
    for a in range(len(ins)):
        for d, (px, py) in enumerate(chips):
            def copy(dst_slot):
                return pltpu.make_async_remote_copy(
                    src_ref=ins[a].at[kks[d]], dst_ref=outs[a].at[dst_slot], send_sem=ssem.at[3 * a + d],
                    recv_sem=rsem.at[3 * a + d], device_id=(px, py, mc), device_id_type=MESH)
            sends.append(copy(k))
            recvs.append(copy(kks[d]))
    return sends, recvs


def _rs_core_join(halves):
    na = len(halves)

    def body(*refs):
        ins, outs = refs[:na], refs[na:2 * na]
        ssem, rsem = refs[2 * na:]
        mx, my, mc = _place()
        sib = (mx, my, 1 - mc)
        sends = []
        for a in range(na):
            cp = pltpu.make_async_remote_copy(
                src_ref=ins[a], dst_ref=outs[a].at[mc], send_sem=ssem.at[a], recv_sem=rsem.at[a],
                device_id=sib, device_id_type=MESH)
            cp.start()
            sends.append(cp)
        for a in range(na):
            pltpu.make_async_remote_copy(
                src_ref=ins[a], dst_ref=outs[a].at[1 - mc], send_sem=ssem.at[a], recv_sem=rsem.at[a],
                device_id=sib, device_id_type=MESH).wait_recv()
        for cp in sends:
            cp.wait_send()

    return pl.pallas_call(
        body, name="rs_core_join", out_shape=[SDS((2,) + h.shape, h.dtype) for h in halves],
        in_specs=[ANY] * na, out_specs=[ANY] * na,
        scratch_shapes=[pltpu.SemaphoreType.DMA((na,))] * 2,
    )(*halves)


def _row_tile(rows, limit, mult=16):
    if rows <= limit:
        return rows
    best = None
    for t in range(mult, limit + 1, mult):
        if rows % t == 0:
            best = t
    assert best is not None, rows
    return best


def _add_half_bf16(g, b, core, name):
    _, h, cols = b.shape
    tb = _row_tile(h, 512)
    nb = h // tb

    def body(core_ref, g_ref, b_ref, o_ref):
        o_ref[...] = (g_ref[...].astype(F32) + b_ref[...].astype(F32)).astype(BF16)

    spec = pl.BlockSpec((None, tb, cols), lambda kk, i, core_ref: (kk, i, 0))
    return pl.pallas_call(
        body, name=name,
        grid_spec=pltpu.PrefetchScalarGridSpec(
            num_scalar_prefetch=1, grid=(N_CHIPS, nb),
            in_specs=[pl.BlockSpec((None, tb, cols), lambda kk, i, core_ref: (kk, core_ref[0] * nb + i, 0)), spec],
            out_specs=spec),
        out_shape=SDS(b.shape, BF16), compiler_params=_cp(("arbitrary", "arbitrary")),
    )(core, g, b)


def _sum_chips(p, name):
    _, rows, cols = p.shape
    tb = _row_tile(rows, 256)

    def body(p_ref, o_ref):
        acc = p_ref[0].astype(F32)
        for j in range(1, N_CHIPS):
            acc = acc + p_ref[j].astype(F32)
        o_ref[...] = acc

    return pl.pallas_call(
        body, name=name, grid=(rows // tb,),
        in_specs=[pl.BlockSpec((N_CHIPS, tb, cols), lambda i: (0, i, 0))],
        out_specs=pl.BlockSpec((tb, cols), lambda i: (i, 0)), out_shape=SDS((rows, cols), F32),
        compiler_params=_cp(("arbitrary",)),
    )(p)


def _adamw_math(w, g, m, v):
    m2 = ADAM_B1 * m + (1.0 - ADAM_B1) * g
    v2 = ADAM_B2 * v + (1.0 - ADAM_B2) * (g * g)
    m_hat = m2 / (1.0 - ADAM_B1 ** ADAM_STEP)
    v_hat = v2 / (1.0 - ADAM_B2 ** ADAM_STEP)
    delta = -ADAM_LR * (m_hat / (jnp.sqrt(v_hat) + ADAM_EPS) + ADAM_WD * w)
    return delta, m2, v2


def _adamw(w, g, m, v, name, echo_g=False):
    rows, cols = w.shape
    tb = _row_tile(rows, 256, mult=8)
    nout = 4 if echo_g else 3

    def body(w_ref, g_ref, m_ref, v_ref, d_ref, m2_ref, v2_ref, *echo):
        gv = g_ref[...]
        d, m2, v2 = _adamw_math(w_ref[...], gv, m_ref[...], v_ref[...])
        d_ref[...] = d
        m2_ref[...] = m2
        v2_ref[...] = v2
        if echo_g:
            echo[0][...] = gv

    spec = pl.BlockSpec((tb, cols), lambda i: (i, 0))
    return pl.pallas_call(
        body, name=name, grid=(rows // tb,), in_specs=[spec] * 4, out_specs=[spec] * nout,
        out_shape=[SDS((rows, cols), F32)] * nout, compiler_params=_cp(("arbitrary",), 40),
    )(w, g, m, v)


def _ada_w_update(sc_all, dmod_k, w, m, v):
    rows, cols = w.shape
    tb = 256

    def body(s_ref, dm_ref, w_ref, m_ref, v_ref, g_ref, d_ref, m2_ref, v2_ref):
        g = _tn(s_ref[...].astype(BF16), dm_ref[...].astype(BF16))
        d, m2, v2 = _adamw_math(w_ref[...], g, m_ref[...], v_ref[...])
        g_ref[...] = g
        d_ref[...] = d
        m2_ref[...] = m2
        v2_ref[...] = v2

    spec = pl.BlockSpec((tb, cols), lambda i: (i, 0))
    return pl.pallas_call(
        body, name="ada_w_update", grid=(rows // tb,),
        in_specs=[pl.BlockSpec((N_DEV, tb), lambda i: (0, i)), pl.BlockSpec((N_DEV, cols), lambda i: (0, 0)), spec, spec, spec],
        out_specs=[spec] * 4, out_shape=[SDS((rows, cols), F32)] * 4, compiler_params=_cp(("arbitrary",), 40),
    )(sc_all, dmod_k, w, m, v)


def _ada_mod(c_all, w, b_k):
    rows, cols = w.shape
    tn = 512

    def body(c_ref, w_ref, b_ref, o_ref, s_ref):
        cv = c_ref[...]
        s = cv * _sigmoid(cv)
        s_ref[...] = s
        o_ref[...] = _nn(s.astype(BF16), w_ref[...].astype(BF16)) + b_ref[...]

    return pl.pallas_call(
        body, name="ada_mod", grid=(cols // tn,),
        in_specs=[pl.BlockSpec((N_DEV, rows), lambda j: (0, 0)), pl.BlockSpec((rows, tn), lambda j: (0, j)),
                  pl.BlockSpec((1, tn), lambda j: (0, j))],
        out_specs=[pl.BlockSpec((N_DEV, tn), lambda j: (0, j)), pl.BlockSpec((N_DEV, rows), lambda j: (0, 0))],
        out_shape=[SDS((N_DEV, cols), F32), SDS((N_DEV, rows), F32)], compiler_params=_cp(("arbitrary",)),
    )(c_all, w, b_k)


def _inproj(h, w_t, shards):
    seq, dm = h.shape
    ncols = w_t.shape[0]
    tm, tn = min(INPROJ_TM, seq), INPROJ_TN
    ni, nj = seq // tm, ncols // tn
    na = len(shards)

    def body(h_ref, w_ref, *rest):
        shard_refs, u_ref, got_refs = rest[:na], rest[na], rest[na + 1:2 * na + 1]
        ssem, rsem = rest[2 * na + 1:]
        i, j = pl.program_id(0), pl.program_id(1)
        mx, my, mc = _place()
        k = 2 * mx + my
        chips, kks = _chips_of(mx, my)

        def copy(a, d, slot):
            return pltpu.make_async_remote_copy(
                src_ref=shard_refs[a], dst_ref=got_refs[a].at[slot], send_sem=ssem.at[3 * a + d],
                recv_sem=rsem.at[3 * a + d], device_id=(chips[d][0], chips[d][1], mc), device_id_type=MESH)

        @pl.when((i == 0) & (j == 0))
        def _():
            for a in range(na):
                for d in range(3):
                    copy(a, d, k).start()

        u_ref[...] = _nt(h_ref[...], w_ref[...]).astype(BF16)

        @pl.when((i == ni - 1) & (j == nj - 1))
        def _():
            for a in range(na):
                for d in range(3):
                    copy(a, d, kks[d]).wait_recv()
            for a in range(na):
                for d in range(3):
                    copy(a, d, k).wait_send()

    outs = pl.pallas_call(
        body, name="inproj", grid=(ni, nj),
        in_specs=[pl.BlockSpec((tm, dm), lambda i, j: (i, 0)), pl.BlockSpec((tn, dm), lambda i, j: (j, 0))] + [ANY] * na,
        out_specs=[pl.BlockSpec((tm, tn), lambda i, j: (i, j))] + [ANY] * na,
        out_shape=[SDS((seq, ncols), BF16)] + [SDS((N_CHIPS,) + s.shape, s.dtype) for s in shards],
        scratch_shapes=[pltpu.SemaphoreType.DMA((3 * na,))] * 2,
        compiler_params=_cp(("arbitrary", "arbitrary"), 48),
    )(h, w_t, *shards)
    return outs[0], outs[1:]


HALO = 16
CHUNK_ROWS, CHUNK_LANES = 32, 256


def _conv_fwd(u, conv_w):
    seq = u.shape[0]
    ts = min(ROW_T, seq)
    hb = ts // HALO

    def body(xc_ref, bc_ref, cc_ref, zc_ref, xp_ref, cp_ref, w_ref, y_ref, ext_ref):
        i = pl.program_id(0)
        up = cp_ref[...].astype(F32) * xp_ref[...].astype(F32)
        ext_ref[0:HALO, :] = jnp.where(i > 0, up, 0.0)
        ext_ref[HALO:HALO + ts, :] = cc_ref[...].astype(F32) * xc_ref[...].astype(F32)
        for r0 in range(0, ts, CHUNK_ROWS):
            rows = slice(r0, r0 + CHUNK_ROWS)
            for c0 in range(0, D_CONV, CHUNK_LANES):
                cols = slice(c0, c0 + CHUNK_LANES)
                uc = ext_ref[HALO + r0:HALO + r0 + CHUNK_ROWS, cols]
                u1 = ext_ref[HALO - 1 + r0:HALO - 1 + r0 + CHUNK_ROWS, cols]
                u2 = ext_ref[HALO - 2 + r0:HALO - 2 + r0 + CHUNK_ROWS, cols]
                conv = w_ref[0:1, cols] * u2 + w_ref[1:2, cols] * u1 + w_ref[2:3, cols] * uc
                z = zc_ref[rows, cols].astype(F32)
                y_ref[rows, cols] = ((bc_ref[rows, cols].astype(F32) * conv) * (z * _sigmoid(z))).astype(BF16)

    def col(cb):
        return pl.BlockSpec((ts, D_CONV), lambda i: (i, cb))

    def prev(cb):
        return pl.BlockSpec((HALO, D_CONV), lambda i: (jnp.maximum(i * hb - 1, 0), cb))

    return pl.pallas_call(
        body, name="conv_fwd", grid=(seq // ts,),
        in_specs=[col(0), col(1), col(2), col(3), prev(0), prev(2), pl.BlockSpec((3, D_CONV), lambda i: (0, 0))],
        out_specs=pl.BlockSpec((ts, D_CONV), lambda i: (i, 0)), out_shape=SDS((seq, D_CONV), BF16),
        scratch_shapes=[pltpu.VMEM((ts + HALO, D_CONV), F32)],
        compiler_params=_cp(("arbitrary",), 40),
    )(u, u, u, u, u, u, conv_w)


def _rope_tables(pos_ref, freq_ref):
    ang = pos_ref[...].astype(F32) * freq_ref[...]
    lane = lax.broadcasted_iota(jnp.int32, ang.shape, 1)
    cs, sn = jnp.cos(ang), jnp.sin(ang)
    half = QK_ROPE // 2
    cos_t = jnp.where(lane < QK_ROPE, cs, 0.0)
    sin_lo = jnp.where(lane < half, sn, 0.0)
    sin_hi = jnp.where((lane >= half) & (lane < QK_ROPE), sn, 0.0)
    return cos_t, sin_lo, sin_hi


def _rope(blk, tables):
    cos_t, sin_lo, sin_hi = tables
    half = QK_ROPE // 2
    return blk * cos_t - pltpu.roll(blk, LANES - half, 1) * sin_lo + pltpu.roll(blk, half, 1) * sin_hi


def _rope_bwd(g, tables):
    cos_t, sin_lo, sin_hi = tables
    half = QK_ROPE // 2
    return g * cos_t + pltpu.roll(g, LANES - half, 1) * sin_lo - pltpu.roll(g, half, 1) * sin_hi


def _rms(v, n):
    r = lax.rsqrt(jnp.sum(v * v, axis=-1, keepdims=True) * (1.0 / n) + EPS)
    return v * r, r


def _mla_prep(u, pos, freq, q_a_g, wq, kv_a_g, wkn, wv, q_g, k_g):
    seq = u.shape[0]
    ts = min(MLA_PREP_T, seq)
    qscale = LOG2E / math.sqrt(QK_HEAD)

    def body(cq_ref, ckv_ref, kr_ref, pos_ref, freq_ref, qag_ref, wq_ref, kvag_ref, wkn_ref, wv_ref, qg_ref, kg_ref,
             q_ref, k_ref, v_ref):
        tables = _rope_tables(pos_ref, freq_ref)
        cqn, _ = _rms(cq_ref[...].astype(F32), Q_LORA)
        qp = _nn((cqn * qag_ref[...]).astype(BF16), wq_ref[...])
        qg = qg_ref[...]
        for h in range(N_HEADS):
            lo = h * HEAD_PAD
            qn, _ = _rms(qp[:, lo:lo + HEAD_PAD], QK_HEAD)
            qn = qn * qg
            q_ref[:, lo:lo + LANES] = (qn[:, :LANES] * qscale).astype(BF16)
            q_ref[:, lo + LANES:lo + HEAD_PAD] = (_rope(qn[:, LANES:], tables) * qscale).astype(BF16)
        ckvn, _ = _rms(ckv_ref[...].astype(F32), KV_LORA)
        ckvb = (ckvn * kvag_ref[...]).astype(BF16)
        kn = _nn(ckvb, wkn_ref[...])
        v_ref[...] = _nn(ckvb, wv_ref[...]).astype(BF16)
        kr = kr_ref[:, 0:LANES].astype(F32)
        ssr = jnp.sum(kr * kr, axis=-1, keepdims=True)
        kg = kg_ref[...]
        for h in range(N_HEADS):
            knh = kn[:, h * QK_NOPE:(h + 1) * QK_NOPE]
            r = lax.rsqrt((jnp.sum(knh * knh, axis=-1, keepdims=True) + ssr) * (1.0 / QK_HEAD) + EPS)
            lo = h * HEAD_PAD
            k_ref[:, lo:lo + LANES] = (knh * r * kg[:, :LANES]).astype(BF16)
            k_ref[:, lo + LANES:lo + HEAD_PAD] = _rope(kr * r * kg[:, LANES:], tables).astype(BF16)

    def full(a):
        return pl.BlockSpec(a.shape, lambda i: (0,) * a.ndim)

    return pl.pallas_call(
        body, name="mla_prep", grid=(seq // ts,),
        in_specs=[pl.BlockSpec((ts, Q_LORA), lambda i: (i, U_CQ // Q_LORA)),
                  pl.BlockSpec((ts, KV_LORA), lambda i: (i, U_CKV // KV_LORA)),
                  pl.BlockSpec((ts, KR_PAD), lambda i: (i, U_KR // KR_PAD)),
                  pl.BlockSpec((ts, 1), lambda i: (i, 0)), full(freq), full(q_a_g), full(wq), full(kv_a_g), full(wkn),
                  full(wv), full(q_g), full(k_g)],
        out_specs=[pl.BlockSpec((ts, Q_PAD), lambda i: (i, 0)), pl.BlockSpec((ts, Q_PAD), lambda i: (i, 0)),
                   pl.BlockSpec((ts, D_ATTN), lambda i: (i, 0))],
        out_shape=[SDS((seq, Q_PAD), BF16), SDS((seq, Q_PAD), BF16), SDS((seq, D_ATTN), BF16)],
        compiler_params=_cp(("arbitrary",), 48),
    )(u, u, u, pos, freq, q_a_g, wq, kv_a_g, wkn, wv, q_g, k_g)


def _flash_fwd(q, k, v, u):
    seq = q.shape[0]
    t = min(ATT_T, seq)
    n = seq // t
    n_pairs = n * (n + 1) // 2
    za_blk = U_ZA // V_HEAD

    def body(q_ref, k_ref, v_ref, z_ref, o_ref, y_ref, lse_ref, s_a, s_b, top_a, top_b, m_all, l_all, acc_all):
        ones = jnp.ones((16, t), BF16)
        bufs = ((s_a, top_a), (s_b, top_b))

        def rows(i):
            return pl.ds(pl.multiple_of(i * t, t), t)

        def scores(i, j, buf):
            s_ref, top_ref = buf
            s = _nt(k_ref[rows(j), :], q_ref[rows(i), :])
            s_ref[...] = s
            top_ref[...] = jnp.max(s, axis=0, keepdims=True)

        def update(i, first, s, top, vj):
            m = jnp.where(first, -jnp.inf, m_all[i])
            l = jnp.where(first, 0.0, l_all[i])
            acc = jnp.where(first, 0.0, acc_all[i])
            m_new = jnp.maximum(m, top)
            alpha = jnp.exp2(m - m_new)
            p = jnp.exp2((s - m_new).astype(BF16))
            return m_new, alpha * l + _nn(ones, p)[0:1, :], alpha * acc + _tn(vj, p)

        def absorb(i, j, buf):
            s_ref, top_ref = buf
            m_all[i], l_all[i], acc_all[i] = update(i, j == 0, s_ref[...], top_ref[...], v_ref[rows(j), :])

        def trip(width):
            def walk(_, pair):
                i, j = pair
                for w in range(width):
                    done = j == i - 1
                    ni, nj = jnp.where(done, i + 1, i), jnp.where(done, 0, j + 1)
                    scores(jnp.minimum(ni, n - 1), nj, bufs[(w + 1) % 2])
                    absorb(i, j, bufs[w % 2])
                    i, j = ni, nj
                return i, j
            return walk

        below = n_pairs - n
        if below:
            scores(1, 0, bufs[0])
            pair = lax.fori_loop(0, below // ATT_UNROLL, trip(ATT_UNROLL), (jnp.int32(1), jnp.int32(0)))
            if below % ATT_UNROLL:
                trip(below % ATT_UNROLL)(0, pair)

        def finish(i, carry):
            s = _nt(k_ref[rows(i), :], q_ref[rows(i), :])
            allowed = lax.broadcasted_iota(jnp.int32, (t, t), 0) <= lax.broadcasted_iota(jnp.int32, (t, t), 1)
            s = jnp.where(allowed, s, -jnp.inf)
            m, l, acc = update(i, i == 0, s, jnp.max(s, axis=0, keepdims=True), v_ref[rows(i), :])
            o = (acc * (1.0 / l)).T
            lse_ref[:, rows(i)] = m + jnp.log2(l)
            o_ref[rows(i), :] = o.astype(BF16)
            z = z_ref[rows(i), :].astype(F32)
            y_ref[rows(i), :] = (o * (z * _sigmoid(z))).astype(BF16)
            return carry

        lax.fori_loop(0, n, finish, 0)

    def col(width, cb):
        return pl.BlockSpec((seq, width), lambda h: (0, cb + h))

    return pl.pallas_call(
        body, name="flash_fwd", grid=(N_HEADS,),
        in_specs=[col(HEAD_PAD, 0), col(HEAD_PAD, 0), col(V_HEAD, 0), col(V_HEAD, za_blk)],
        out_specs=[col(V_HEAD, 0), col(V_HEAD, 0), pl.BlockSpec((None, 1, seq), lambda h: (h, 0, 0))],
        out_shape=[SDS((seq, D_ATTN), BF16), SDS((seq, D_ATTN), BF16), SDS((N_HEADS, 1, seq), F32)],
        scratch_shapes=[pltpu.VMEM((t, t), F32)] * 2 + [pltpu.VMEM((1, t), F32)] * 2
                       + [pltpu.VMEM((n, 1, t), F32)] * 2 + [pltpu.VMEM((n, V_HEAD, t), F32)],
        compiler_params=_cp(("arbitrary",), 52),
    )(q, k, v, u)


def _outproj_loss(y_conv, y_attn, x, target, gate, w_out):
    seq, dm = x.shape
    ts = min(OUT_T, seq)
    n = seq // ts
    dmix = w_out.shape[0]

    def body(yc_ref, ya_ref, x_ref, t_ref, gate_ref, wo_hbm, dout_ref, dy_ref, dyc_ref, stats_ref, wo_ref, sem, acc_ref):
        i = pl.program_id(0)

        @pl.when(i == 0)
        def _():
            cp = pltpu.make_async_copy(wo_hbm, wo_ref, sem)
            cp.start()
            cp.wait()
            acc_ref[...] = jnp.zeros_like(acc_ref)

        y = _nn(yc_ref[...], wo_ref[0:D_CONV, :]) + _nn(ya_ref[...], wo_ref[D_CONV:dmix, :])
        gate_v = gate_ref[...]
        diff = (x_ref[...] + gate_v * y) - t_ref[...]
        dout = diff * (1.0 / dm)
        dout_ref[...] = dout
        acc_ref[0:8, :] += jnp.sum((dout * y).reshape(ts // 8, 8, dm), axis=0)
        acc_ref[8:16, :] += jnp.sum((diff * diff).reshape(ts // 8, 8, dm), axis=0)
        dy = (dout * gate_v).astype(BF16)
        dy_ref[...] = dy
        dyc_ref[...] = _nt(dy, wo_ref[...]).astype(BF16)

        @pl.when(i == n - 1)
        def _():
            stats_ref[...] = jnp.zeros_like(stats_ref)
            stats_ref[0:1, :] = jnp.sum(acc_ref[0:8, :], axis=0, keepdims=True)
            loss = jnp.sum(acc_ref[8:16, :]) * (0.5 / dm)
            stats_ref[1:2, :] = jnp.full((1, dm), loss, F32)

    row = pl.BlockSpec((ts, dm), lambda i: (i, 0))
    half = pl.BlockSpec((ts, D_CONV), lambda i: (i, 0))
    return pl.pallas_call(
        body, name="outproj_loss", grid=(n,),
        in_specs=[half, half, row, row, pl.BlockSpec((1, dm), lambda i: (0, 0)), ANY],
        out_specs=[row, row, pl.BlockSpec((ts, dmix), lambda i: (i, 0)), pl.BlockSpec((8, dm), lambda i: (0, 0))],
        out_shape=[SDS((seq, dm), F32), SDS((seq, dm), BF16), SDS((seq, dmix), BF16), SDS((8, dm), F32)],
        scratch_shapes=[pltpu.VMEM(w_out.shape, BF16), pltpu.SemaphoreType.DMA(()), pltpu.VMEM((16, dm), F32)],
        compiler_params=_cp(("arbitrary",), 52),
    )(y_conv, y_attn, x, target, gate, w_out)


def _matmul_tn(a, b, name):
    seq, m = a.shape
    n = b.shape[1]
    tm, tn, tk = min(TN_TM, m), min(TN_TN, n), min(TN_TK, seq)
    nk = seq // tk

    def body(a_ref, b_ref, o_ref, acc_ref):
        kk = pl.program_id(2)

        @pl.when(kk == 0)
        def _():
            acc_ref[...] = jnp.zeros_like(acc_ref)

        acc_ref[...] += _tn(a_ref[...], b_ref[...])

        @pl.when(kk == nk - 1)
        def _():
            o_ref[...] = acc_ref[...].astype(BF16)

    return pl.pallas_call(
        body, name=name, grid=(m // tm, n // tn, nk),
        in_specs=[pl.BlockSpec((tk, tm), lambda i, j, kk: (kk, i)), pl.BlockSpec((tk, tn), lambda i, j, kk: (kk, j))],
        out_specs=pl.BlockSpec((tm, tn), lambda i, j, kk: (i, j)), out_shape=SDS((m, n), BF16),
        scratch_shapes=[pltpu.VMEM((tm, tn), F32)],
        compiler_params=_cp(("arbitrary", "arbitrary", "arbitrary"), 40),
    )(a, b)


def _attn_gate_bwd(dycat, o, u):
    seq = o.shape[0]
    ts = min(ROW_T, seq)

    def body(dy_ref, o_ref, z_ref, dot_ref, dz_ref, dl_ref):
        dy = dy_ref[...].astype(F32)
        ov = o_ref[...].astype(F32)
        z = z_ref[...].astype(F32)
        sg = _sigmoid(z)
        do = dy * (z * sg)
        dz_ref[...] = (dy * ov * _silu_grad(z, sg)).astype(BF16)
        prod = do * ov
        ones = jnp.ones((8, V_HEAD), F32)
        for h in range(N_HEADS):
            cols = slice(h * V_HEAD, (h + 1) * V_HEAD)
            dot_ref[h] = do[:, cols].T.astype(BF16)
            rows = lax.dot_general(ones, prod[:, cols], (((1,), (1,)), ((), ())), precision=lax.Precision.HIGHEST,
                                   preferred_element_type=F32)
            dl_ref[h] = rows[0:1, :]

    blk = pl.BlockSpec((ts, D_ATTN), lambda i: (i, 0))
    return pl.pallas_call(
        body, name="attn_gate_bwd", grid=(seq // ts,),
        in_specs=[pl.BlockSpec((ts, D_ATTN), lambda i: (i, 1)), blk, pl.BlockSpec((ts, D_ATTN), lambda i: (i, U_ZA // D_ATTN))],
        out_specs=[pl.BlockSpec((N_HEADS, V_HEAD, ts), lambda i: (0, 0, i)), blk,
                   pl.BlockSpec((N_HEADS, 1, ts), lambda i: (0, 0, i))],
        out_shape=[SDS((N_HEADS, V_HEAD, seq), BF16), SDS((seq, D_ATTN), BF16), SDS((N_HEADS, 1, seq), F32)],
        compiler_params=_cp(("arbitrary",), 40),
    )(dycat, o, u)


def _flash_bwd(q, k, v, do_t, lse, delta):
    seq = q.shape[0]
    t = min(ATT_T, seq)
    n = seq // t
    n_pairs = n * (n + 1) // 2

    def body(k_ref, v_ref, q_ref, dot_ref, lse_ref, dl_ref, dq_ref, dk_ref, dv_ref, s_a, s_b, dp_a, dp_b, dq_acc, dk_acc,
             dvt_acc):
        bufs = ((s_a, dp_a), (s_b, dp_b))

        def rows(i):
            return pl.ds(pl.multiple_of(i * t, t), t)

        def products(j, i, buf):
            s_ref, dp_ref = buf
            s = _nt(k_ref[rows(j), :], q_ref[rows(i), :])
            ahead = lax.broadcasted_iota(jnp.int32, (t, t), 0) - lax.broadcasted_iota(jnp.int32, (t, t), 1)
            s_ref[...] = jnp.where(ahead <= (i - j) * t, s, -jnp.inf)
            dp_ref[...] = _nn(v_ref[rows(j), :], dot_ref[:, rows(i)])

        def absorb(j, i, buf):
            s_ref, dp_ref = buf
            first = i == j
            p = jnp.exp2((s_ref[...] - lse_ref[:, rows(i)]).astype(BF16))
            dvt = jnp.where(first, 0.0, dvt_acc[...]) + _nt(dot_ref[:, rows(i)], p)
            ds = p * (dp_ref[...] - dl_ref[:, rows(i)]).astype(BF16)
            dk = jnp.where(first, 0.0, dk_acc[...]) + _nn(ds, q_ref[rows(i), :])
            dq_acc[rows(i), :] += _tn(ds, k_ref[rows(j), :])
            dvt_acc[...] = dvt
            dk_acc[...] = dk
            dk_ref[rows(j), :] = dk.astype(BF16)
            dv_ref[rows(j), :] = dvt.T.astype(BF16)

        def trip(width):
            def walk(_, pair):
                j, i = pair
                for w in range(width):
                    done = i == n - 1
                    nj = jnp.where(done, j + 1, j)
                    ni = jnp.where(done, j + 1, i + 1)
                    absorb(j, i, bufs[w % 2])
                    products(jnp.minimum(nj, n - 1), jnp.minimum(ni, n - 1), bufs[(w + 1) % 2])
                    j, i = nj, ni
                return j, i
            return walk

        dq_acc[...] = jnp.zeros_like(dq_acc)
        products(0, 0, bufs[0])
        pair = lax.fori_loop(0, n_pairs // ATT_UNROLL, trip(ATT_UNROLL), (jnp.int32(0), jnp.int32(0)))
        if n_pairs % ATT_UNROLL:
            trip(n_pairs % ATT_UNROLL)(0, pair)

        def finish(i, carry):
            dq_ref[rows(i), :] = dq_acc[rows(i), :].astype(BF16)
            return carry

        lax.fori_loop(0, n, finish, 0)

    def col(width):
        return pl.BlockSpec((seq, width), lambda h: (0, h))

    row = pl.BlockSpec((None, 1, seq), lambda h: (h, 0, 0))
    return pl.pallas_call(
        body, name="flash_bwd", grid=(N_HEADS,),
        in_specs=[col(HEAD_PAD), col(V_HEAD), col(HEAD_PAD), pl.BlockSpec((None, V_HEAD, seq), lambda h: (h, 0, 0)), row, row],
        out_specs=[col(HEAD_PAD), col(HEAD_PAD), col(V_HEAD)],
        out_shape=[SDS((seq, Q_PAD), BF16), SDS((seq, Q_PAD), BF16), SDS((seq, D_ATTN), BF16)],
        scratch_shapes=[pltpu.VMEM((t, t), F32)] * 4
                       + [pltpu.VMEM((seq, HEAD_PAD), F32), pltpu.VMEM((t, HEAD_PAD), F32), pltpu.VMEM((V_HEAD, t), F32)],
        compiler_params=_cp(("arbitrary",), 60),
    )(k, v, q, do_t, lse, delta)


SG_QAG, SG_KVAG, SG_QG, SG_KG, SG_COLS = 0, Q_LORA, Q_LORA + KV_LORA, Q_LORA + KV_LORA + HEAD_PAD, D_MODEL


def _mla_bwd(dq, dk, dv, u, pos, freq, q_a_g, wq, kv_a_g, wkn, wv, q_g, k_g):
    seq = u.shape[0]
    ts = min(ROW_T, seq)
    n = seq // ts
    qscale = 1.0 / math.sqrt(QK_HEAD)

    def body(dq_ref, dk_ref, dv_ref, cq_ref, ckv_ref, kr_ref, pos_ref, freq_ref, qag_ref, wq_ref, kvag_ref, wkn_ref,
             wv_ref, qg_ref, kg_ref, du_ref, dwq_ref, dwkn_ref, dwv_ref, sg_ref, dqp_ref, dkn_ref):
        i = pl.program_id(0)

        @pl.when(i == 0)
        def _():
            dwq_ref[...] = jnp.zeros_like(dwq_ref)
            dwkn_ref[...] = jnp.zeros_like(dwkn_ref)
            dwv_ref[...] = jnp.zeros_like(dwv_ref)
            sg_ref[...] = jnp.zeros_like(sg_ref)

        tables = _rope_tables(pos_ref, freq_ref)

        cq = cq_ref[...].astype(F32)
        cqn, rq = _rms(cq, Q_LORA)
        qag = qag_ref[...]
        cqb = (cqn * qag).astype(BF16)
        qp = _nn(cqb, wq_ref[...])
        qg = qg_ref[...]
        dqg = jnp.zeros((1, HEAD_PAD), F32)
        for h in range(N_HEADS):
            lo = h * HEAD_PAD
            xn, r = _rms(qp[:, lo:lo + HEAD_PAD], QK_HEAD)
            g = jnp.concatenate([dq_ref[:, lo:lo + LANES].astype(F32),
                                 _rope_bwd(dq_ref[:, lo + LANES:lo + HEAD_PAD].astype(F32), tables)], axis=-1) * qscale
            dqg = dqg + jnp.sum(g * xn, axis=0, keepdims=True)
            gy = g * qg
            mean = jnp.sum(gy * xn, axis=-1, keepdims=True) * (1.0 / QK_HEAD)
            dqp_ref[:, lo:lo + HEAD_PAD] = (r * (gy - xn * mean)).astype(BF16)
        dqp = dqp_ref[...]
        dwq_ref[...] += _tn(cqb, dqp)
        dcqn = _nt(dqp, wq_ref[...])
        sg_ref[0:1, SG_QAG:SG_QAG + Q_LORA] += jnp.sum(dcqn * cqn, axis=0, keepdims=True)
        sg_ref[0:1, SG_QG:SG_QG + HEAD_PAD] += dqg
        gy = dcqn * qag
        mean = jnp.sum(gy * cqn, axis=-1, keepdims=True) * (1.0 / Q_LORA)
        du_ref[:, 0:Q_LORA] = (rq * (gy - cqn * mean)).astype(BF16)

        ckv = ckv_ref[...].astype(F32)
        ckvn, rkv = _rms(ckv, KV_LORA)
        kvag = kvag_ref[...]
        ckvb = (ckvn * kvag).astype(BF16)
        kn = _nn(ckvb, wkn_ref[...])
        kr = kr_ref[:, 0:LANES].astype(F32)
        ssr = jnp.sum(kr * kr, axis=-1, keepdims=True)
        kg = kg_ref[...]
        kg_n, kg_r = kg[:, :LANES] * LN2, kg[:, LANES:] * LN2
        dkg_n = jnp.zeros((1, LANES), F32)
        dkg_r = jnp.zeros((1, LANES), F32)
        dkr = jnp.zeros((ts, LANES), F32)
        for h in range(N_HEADS):
            knh = kn[:, h * QK_NOPE:(h + 1) * QK_NOPE]
            r = lax.rsqrt((jnp.sum(knh * knh, axis=-1, keepdims=True) + ssr) * (1.0 / QK_HEAD) + EPS)
            xn_n, xn_r = knh * r, kr * r
            lo = h * HEAD_PAD
            g_n = dk_ref[:, lo:lo + LANES].astype(F32)
            g_r = _rope_bwd(dk_ref[:, lo + LANES:lo + HEAD_PAD].astype(F32), tables)
            dkg_n = dkg_n + jnp.sum(g_n * xn_n, axis=0, keepdims=True)
            dkg_r = dkg_r + jnp.sum(g_r * xn_r, axis=0, keepdims=True)
            gy_n, gy_r = g_n * kg_n, g_r * kg_r
            mean = (jnp.sum(gy_n * xn_n, axis=-1, keepdims=True) + jnp.sum(gy_r * xn_r, axis=-1, keepdims=True)) * (1.0 / QK_HEAD)
            dkn_ref[:, h * QK_NOPE:(h + 1) * QK_NOPE] = (r * (gy_n - xn_n * mean)).astype(BF16)
            dkr = dkr + r * (gy_r - xn_r * mean)
        dkn = dkn_ref[...]
        dvv = dv_ref[...]
        dwkn_ref[...] += _tn(ckvb, dkn)
        dwv_ref[...] += _tn(ckvb, dvv)
        dckvn = _nt(dkn, wkn_ref[...]) + _nt(dvv, wv_ref[...])
        sg_ref[0:1, SG_KVAG:SG_KVAG + KV_LORA] += jnp.sum(dckvn * ckvn, axis=0, keepdims=True)
        sg_ref[0:1, SG_KG:SG_KG + LANES] += dkg_n * LN2
        sg_ref[0:1, SG_KG + LANES:SG_KG + HEAD_PAD] += dkg_r * LN2
        gy = dckvn * kvag
        mean = jnp.sum(gy * ckvn, axis=-1, keepdims=True) * (1.0 / KV_LORA)
        du_ref[:, Q_LORA:Q_LORA + KV_LORA] = (rkv * (gy - ckvn * mean)).astype(BF16)
        du_ref[:, Q_LORA + KV_LORA:Q_LORA + KV_LORA + LANES] = dkr.astype(BF16)
        du_ref[:, Q_LORA + KV_LORA + LANES:MLA_COLS] = jnp.zeros((ts, LANES), BF16)

    def full(a):
        return pl.BlockSpec(a.shape, lambda i: (0,) * a.ndim)

    wide = pl.BlockSpec((ts, Q_PAD), lambda i: (i, 0))
    return pl.pallas_call(
        body, name="mla_bwd", grid=(n,),
        in_specs=[wide, wide, pl.BlockSpec((ts, D_ATTN), lambda i: (i, 0)),
                  pl.BlockSpec((ts, Q_LORA), lambda i: (i, U_CQ // Q_LORA)),
                  pl.BlockSpec((ts, KV_LORA), lambda i: (i, U_CKV // KV_LORA)),
                  pl.BlockSpec((ts, KR_PAD), lambda i: (i, U_KR // KR_PAD)),
                  pl.BlockSpec((ts, 1), lambda i: (i, 0)), full(freq), full(q_a_g), full(wq), full(kv_a_g), full(wkn),
                  full(wv), full(q_g), full(k_g)],
        out_specs=[pl.BlockSpec((ts, MLA_COLS), lambda i: (i, 0)), pl.BlockSpec((Q_LORA, Q_PAD), lambda i: (0, 0)),
                   pl.BlockSpec((KV_LORA, D_ATTN), lambda i: (0, 0)), pl.BlockSpec((KV_LORA, D_ATTN), lambda i: (0, 0)),
                   pl.BlockSpec((8, SG_COLS), lambda i: (0, 0))],
        out_shape=[SDS((seq, MLA_COLS), BF16), SDS((Q_LORA, Q_PAD), F32), SDS((KV_LORA, D_ATTN), F32),
                   SDS((KV_LORA, D_ATTN), F32), SDS((8, SG_COLS), F32)],
        scratch_shapes=[pltpu.VMEM((ts, Q_PAD), BF16), pltpu.VMEM((ts, D_ATTN), BF16)],
        compiler_params=_cp(("arbitrary",), 56),
    )(dq, dk, dv, u, u, u, pos, freq, q_a_g, wq, kv_a_g, wkn, wv, q_g, k_g)


def _conv_bwd(dycat, u, conv_w):
    seq = u.shape[0]
    ts = min(ROW_T, seq)
    n = seq // ts
    hb = ts // HALO

    def body(dy_ref, xc_ref, bc_ref, cc_ref, zc_ref, xp_ref, cp_ref, dyn_ref, bn_ref, zn_ref, w_ref,
             du_ref, dw_ref, ext_ref, dext_ref, acc_ref):
        i = pl.program_id(0)

        @pl.when(i == 0)
        def _():
            dw_ref[...] = jnp.zeros_like(dw_ref)

        up = cp_ref[...].astype(F32) * xp_ref[...].astype(F32)
        ext_ref[0:HALO, :] = jnp.where(i > 0, up, 0.0)
        ext_ref[HALO:HALO + ts, :] = cc_ref[...].astype(F32) * xc_ref[...].astype(F32)
        zn = zn_ref[...].astype(F32)
        dnext = dyn_ref[...].astype(F32) * (zn * _sigmoid(zn)) * bn_ref[...].astype(F32)
        dext_ref[ts:ts + HALO, :] = jnp.where(i < n - 1, dnext, 0.0)
        acc_ref[...] = jnp.zeros_like(acc_ref)

        for r0 in range(0, ts, CHUNK_ROWS):
            rows = slice(r0, r0 + CHUNK_ROWS)
            for c0 in range(0, D_CONV, CHUNK_LANES):
                cols = slice(c0, c0 + CHUNK_LANES)
                uc = ext_ref[HALO + r0:HALO + r0 + CHUNK_ROWS, cols]
                u1 = ext_ref[HALO - 1 + r0:HALO - 1 + r0 + CHUNK_ROWS, cols]
                u2 = ext_ref[HALO - 2 + r0:HALO - 2 + r0 + CHUNK_ROWS, cols]
                conv = w_ref[0:1, cols] * u2 + w_ref[1:2, cols] * u1 + w_ref[2:3, cols] * uc
                z = zc_ref[rows, cols].astype(F32)
                sg = _sigmoid(z)
                sz = z * sg
                b = bc_ref[rows, cols].astype(F32)
                dy = dy_ref[rows, cols].astype(F32)
                du_ref[rows, 3 * D_CONV + c0:3 * D_CONV + c0 + CHUNK_LANES] = (dy * (b * conv) * _silu_grad(z, sg)).astype(BF16)
                du_ref[rows, D_CONV + c0:D_CONV + c0 + CHUNK_LANES] = (dy * sz * conv).astype(BF16)
                dconv = dy * sz * b
                dext_ref[rows, cols] = dconv
                acc_ref[0:CHUNK_ROWS, cols] += dconv * u2
                acc_ref[CHUNK_ROWS:2 * CHUNK_ROWS, cols] += dconv * u1
                acc_ref[2 * CHUNK_ROWS:3 * CHUNK_ROWS, cols] += dconv * uc
        for r0 in range(0, ts, CHUNK_ROWS):
            rows = slice(r0, r0 + CHUNK_ROWS)
            for c0 in range(0, D_CONV, CHUNK_LANES):
                cols = slice(c0, c0 + CHUNK_LANES)
                du = (w_ref[2:3, cols] * dext_ref[rows, cols] + w_ref[1:2, cols] * dext_ref[r0 + 1:r0 + 1 + CHUNK_ROWS, cols]
                      + w_ref[0:1, cols] * dext_ref[r0 + 2:r0 + 2 + CHUNK_ROWS, cols])
                du_ref[rows, 2 * D_CONV + c0:2 * D_CONV + c0 + CHUNK_LANES] = (du * xc_ref[rows, cols].astype(F32)).astype(BF16)
                du_ref[rows, c0:c0 + CHUNK_LANES] = (du * cc_ref[rows, cols].astype(F32)).astype(BF16)
        for k in range(3):
            dw_ref[k:k + 1, :] += jnp.sum(acc_ref[k * CHUNK_ROWS:(k + 1) * CHUNK_ROWS, :], axis=0, keepdims=True)

    def col(cb):
        return pl.BlockSpec((ts, D_CONV), lambda i: (i, cb))

    def prev(cb):
        return pl.BlockSpec((HALO, D_CONV), lambda i: (jnp.maximum(i * hb - 1, 0), cb))

    def nxt(cb):
        return pl.BlockSpec((HALO, D_CONV), lambda i: (jnp.minimum((i + 1) * hb, n * hb - 1), cb))

    return pl.pallas_call(
        body, name="conv_bwd", grid=(n,),
        in_specs=[col(0), col(0), col(1), col(2), col(3), prev(0), prev(2), nxt(0), nxt(1), nxt(3),
                  pl.BlockSpec((3, D_CONV), lambda i: (0, 0))],
        out_specs=[pl.BlockSpec((ts, 4 * D_CONV), lambda i: (i, 0)), pl.BlockSpec((8, D_CONV), lambda i: (0, 0))],
        out_shape=[SDS((seq, 4 * D_CONV), BF16), SDS((8, D_CONV), F32)],
        scratch_shapes=[pltpu.VMEM((ts + HALO, D_CONV), F32), pltpu.VMEM((ts + HALO, D_CONV), F32),
                        pltpu.VMEM((3 * CHUNK_ROWS, D_CONV), F32)],
        compiler_params=_cp(("arbitrary",), 48),
    )(dycat, u, u, u, u, u, u, dycat, u, u, conv_w)


def _inproj_bwd(du_conv, du_za, du_mla, w_t, parts):
    seq = du_conv.shape[0]
    dm = w_t.shape[1]
    tm, tn = min(DH_TM, seq), DH_TN
    ni, nj = seq // tm, dm // tn
    na = len(parts)

    def body(dc_ref, dz_ref, dm_ref, w_ref, *rest):
        part_refs, o_ref, recv_refs = rest[:na], rest[na], rest[na + 1:2 * na + 1]
        ssem, rsem = rest[2 * na + 1:]
        i, j = pl.program_id(0), pl.program_id(1)
        sends, recvs = _chip_exchange_copies(part_refs, recv_refs, ssem, rsem)

        @pl.when((i == 0) & (j == 0))
        def _():
            for cp in sends:
                cp.start()

        acc = _nn(dc_ref[...], w_ref[0:U_ZA, :])
        acc = acc + _nn(dz_ref[...], w_ref[U_ZA:U_CQ, :])
        acc = acc + _nn(dm_ref[...], w_ref[U_CQ:U_COLS, :])
        o_ref[...] = acc.astype(BF16)

        @pl.when((i == ni - 1) & (j == nj - 1))
        def _():
            for cp in recvs:
                cp.wait_recv()
            for cp in sends:
                cp.wait_send()

    outs = pl.pallas_call(
        body, name="inproj_bwd", grid=(ni, nj),
        in_specs=[pl.BlockSpec((tm, U_ZA), lambda i, j: (i, 0)), pl.BlockSpec((tm, D_ATTN), lambda i, j: (i, 0)),
                  pl.BlockSpec((tm, MLA_COLS), lambda i, j: (i, 0)), pl.BlockSpec((U_COLS, tn), lambda i, j: (0, j))]
                 + [ANY] * na,
        out_specs=[pl.BlockSpec((tm, tn), lambda i, j: (i, j))] + [ANY] * na,
        out_shape=[SDS((seq, dm), BF16)] + [SDS(p.shape, p.dtype) for p in parts],
        scratch_shapes=[pltpu.SemaphoreType.DMA((3 * na,))] * 2,
        compiler_params=_cp(("arbitrary", "arbitrary"), 48),
    )(du_conv, du_za, du_mla, w_t, *parts)
    return outs[0], outs[1:]


def _prenorm_bwd(x, dh, dout, norm_g, scale):
    seq, dm = x.shape
    ts = min(ROW_T, seq)
    n = seq // ts

    def body(x_ref, dh_ref, dout_ref, g_ref, sc_ref, gx_ref, st_ref, acc_ref):
        i = pl.program_id(0)

        @pl.when(i == 0)
        def _():
            acc_ref[...] = jnp.zeros_like(acc_ref)

        xv = x_ref[...]
        xn, r = _rms(xv, dm)
        dh_v = dh_ref[...].astype(F32)
        gv = g_ref[...]
        one_sc = 1.0 + sc_ref[...]

        def fold(a):
            return jnp.sum(a.reshape(ts // 8, 8, dm), axis=0)

        acc_ref[0:8, :] += fold(dh_v)
        acc_ref[8:16, :] += fold(dh_v * (xn * gv))
        dxg = dh_v * one_sc
        acc_ref[16:24, :] += fold(dxg * xn)
        dxn = dxg * gv
        mean = jnp.sum(dxn * xn, axis=-1, keepdims=True) * (1.0 / dm)
        gx_ref[...] = dout_ref[...] + r * (dxn - xn * mean)

        @pl.when(i == n - 1)
        def _():
            st_ref[...] = jnp.zeros_like(st_ref)
            for k in range(3):
                st_ref[k:k + 1, :] = jnp.sum(acc_ref[8 * k:8 * k + 8, :], axis=0, keepdims=True)

    row = pl.BlockSpec((ts, dm), lambda i: (i, 0))
    vec = pl.BlockSpec((1, dm), lambda i: (0, 0))
    return pl.pallas_call(
        body, name="prenorm_bwd", grid=(n,), in_specs=[row, row, row, vec, vec],
        out_specs=[row, pl.BlockSpec((8, dm), lambda i: (0, 0))],
        out_shape=[SDS((seq, dm), F32), SDS((8, dm), F32)],
        scratch_shapes=[pltpu.VMEM((24, dm), F32)], input_output_aliases={2: 0},
        compiler_params=_cp(("arbitrary",), 52),
    )(x, dh, dout, norm_g, scale)


def _unshard_cols(g):
    return jnp.transpose(g, (1, 0, 2)).reshape(g.shape[1], -1)


def _shard_cols(w):
    r = w.shape[0]
    return jnp.transpose(w.reshape(r, N_CHIPS, -1), (1, 0, 2))


W_IN_COLS = 4 * D_CONV + Q_LORA + KV_LORA + QK_ROPE + D_ATTN
SHARD_ROWS = W_IN_COLS // N_CHIPS
SHARD_PAD = 1536


def _w_in_pieces():
    c4 = 4 * D_CONV
    groups = [(0, c4, 0), (c4, c4 + Q_LORA, U_CQ), (c4 + Q_LORA, c4 + Q_LORA + KV_LORA, U_CKV),
              (c4 + Q_LORA + KV_LORA, W_IN_COLS - D_ATTN, U_KR), (W_IN_COLS - D_ATTN, W_IN_COLS, U_ZA)]
    pieces = []
    for lo, hi, my in groups:
        for chip in range(N_CHIPS):
            a, b = max(lo, chip * SHARD_ROWS), min(hi, (chip + 1) * SHARD_ROWS)
            if a < b:
                pieces.append((chip, a - chip * SHARD_ROWS, b - a, my + a - lo))
    return pieces


def _w_t_to_my(g):
    w = jnp.zeros((U_COLS, g.shape[2]), g.dtype)
    for chip, row, n, my in _w_in_pieces():
        w = lax.dynamic_update_slice(w, g[chip, row:row + n], (my, 0))
    return w


def _w_t_from_my(g_conv, g_za, g_mla):
    w = jnp.zeros((N_CHIPS, SHARD_PAD, g_conv.shape[1]), g_conv.dtype)
    for chip, row, n, my in _w_in_pieces():
        src, base = (g_conv, 0) if my < U_ZA else (g_za, U_ZA) if my < U_CQ else (g_mla, U_CQ)
        w = lax.dynamic_update_slice(w, src[my - base:my - base + n][None], (chip, row, 0))
    return w


def _heads_pad(w):
    r = w.shape[0]
    w3 = w.reshape(r, N_HEADS, QK_HEAD)
    return jnp.pad(w3, ((0, 0), (0, 0), (0, HEAD_PAD - QK_HEAD))).reshape(r, Q_PAD)


def _heads_unpad(w):
    r = w.shape[0]
    return w.reshape(r, N_HEADS, HEAD_PAD)[:, :, :QK_HEAD].reshape(r, N_HEADS * QK_HEAD)


def kernel(x, c, positions, ada_w, ada_b, norm_g, w_in, conv_w, q_a_g, w_q_b, kv_a_g, w_kv_b, q_g, k_g, w_out, loss_target, m_ada_w, m_ada_b, m_norm_g, m_w_in, m_conv_w, m_q_a_g, m_w_q_b, m_kv_a_g, m_w_kv_b, m_q_g, m_k_g, m_w_out, v_ada_w, v_ada_b, v_norm_g, v_w_in, v_conv_w, v_q_a_g, v_w_q_b, v_kv_a_g, v_w_kv_b, v_q_g, v_k_g, v_w_out):
    mx, my, mc = _place()
    chip = 2 * mx + my
    me = 2 * chip + mc
    seq = x.shape[1]
    x2, t2 = x[0], loss_target[0]
    cw_cols = conv_w.shape[2]

    small = jnp.zeros((8, D_MODEL), F32)
    small = small.at[0].set(c[0])
    small = small.at[1:4, :cw_cols].set(conv_w[0])
    small_all = _gather8(small, "gather_c_conv", False)[0]
    c_all = small_all[:, 0, :]
    conv_full = jnp.transpose(small_all.reshape(N_CHIPS, 2, 8, D_MODEL)[:, 0, 1:4, :cw_cols], (1, 0, 2)).reshape(3, D_CONV)

    ada_cols = ada_w.shape[2]
    b_k = lax.dynamic_slice(ada_b, (0, chip * ada_cols), (1, ada_cols))
    mod_k, sc_all = _ada_mod(c_all, ada_w[0], b_k)
    mod_all = _gather8(mod_k, "gather_mod", False)[0]
    mod_row = lax.dynamic_slice(mod_all.reshape(N_CHIPS, 2, N_DEV, ada_cols), (0, mc, me, 0), (N_CHIPS, 1, 1, ada_cols))
    mod_row = mod_row.reshape(3, D_MODEL)
    shift, scale, gate = mod_row[0:1], mod_row[1:2], mod_row[2:3]

    def own_slot(g, s):
        return lax.dynamic_update_slice(g, s[None], (chip, 0, 0))

    w_in_t, m_w_in_t, v_w_in_t = [jnp.transpose(a[0]) for a in (w_in, m_w_in, v_w_in)]
    shard_in = jnp.pad(w_in_t.astype(BF16), ((0, SHARD_PAD - SHARD_ROWS), (0, 0)))
    h, g_in = _prenorm_gather(x2, norm_g, scale, shift, shard_in)
    g_in = own_slot(g_in, shard_in)
    w_t = _w_t_to_my(g_in)

    later = [w_q_b[0].astype(BF16), w_kv_b[0].astype(BF16), w_out[0].astype(BF16)]
    u, got = _inproj(h, w_t, later)
    g_q, g_kv, g_out = [own_slot(g, s) for g, s in zip(got, later)]
    wq = _heads_pad(_unshard_cols(g_q))
    wkv = _unshard_cols(g_kv).reshape(KV_LORA, N_HEADS, QK_NOPE + V_HEAD)
    wkn = wkv[:, :, :QK_NOPE].reshape(KV_LORA, N_HEADS * QK_NOPE)
    wv = wkv[:, :, QK_NOPE:].reshape(KV_LORA, D_ATTN)
    wo = g_out.reshape(N_CHIPS * g_out.shape[1], D_MODEL)
    y_conv = _conv_fwd(u, conv_full)
    pos = positions.reshape(seq, 1)
    inv_freq = ROPE_BASE ** (-jnp.arange(0, QK_ROPE, 2, dtype=F32) / QK_ROPE)
    freq = jnp.concatenate([inv_freq, inv_freq, jnp.zeros((LANES - QK_ROPE,), F32)]).reshape(1, LANES)
    q_g_pad = jnp.pad(q_g, ((0, 0), (0, HEAD_PAD - QK_HEAD)))
    k_g_pad = jnp.pad(k_g, ((0, 0), (0, HEAD_PAD - QK_HEAD)))
    q, k, v = _mla_prep(u, pos, freq, q_a_g, wq, kv_a_g, wkn, wv, q_g_pad, k_g_pad)
    o, y_attn, lse = _flash_fwd(q, k, v, u)
    dout, dy, dycat, st_out = _outproj_loss(y_conv, y_attn, x2, t2, gate, wo)

    dw_out = jnp.concatenate([_matmul_tn(y_conv, dy, "dw_out_conv"), _matmul_tn(y_attn, dy, "dw_out_attn")], axis=0)
    do_t, du_za, delta = _attn_gate_bwd(dycat, o, u)
    dq, dk, dv = _flash_bwd(q, k, v, do_t, lse, delta)
    du_mla, dwq, dwkn, dwv, sg_mla = _mla_bwd(dq, dk, dv, u, pos, freq, q_a_g, wq, kv_a_g, wkn, wv, q_g_pad, k_g_pad)
    du_conv, dconv_w = _conv_bwd(dycat, u, conv_full)
    dw_conv = _matmul_tn(du_conv, h, "dw_in_conv")
    dw_za = _matmul_tn(du_za, h, "dw_in_za")
    dw_mla = _matmul_tn(du_mla, h, "dw_in_mla")

    dw_q_nat = _heads_unpad(dwq).astype(BF16)
    dw_kv_nat = jnp.concatenate([dwkn.reshape(KV_LORA, N_HEADS, QK_NOPE), dwv.reshape(KV_LORA, N_HEADS, V_HEAD)],
                                axis=2).reshape(KV_LORA, N_HEADS * (QK_NOPE + V_HEAD)).astype(BF16)
    grads = [_w_t_from_my(dw_conv, dw_za, dw_mla), _shard_cols(dw_q_nat), _shard_cols(dw_kv_nat),
             dw_out.reshape(N_CHIPS, dw_out.shape[0] // N_CHIPS, D_MODEL)]
    theirs = _rs_core_swap(grads)
    names = ["w_in", "w_q_b", "w_kv_b", "w_out"]
    core = jnp.reshape(mc, (1,)).astype(jnp.int32)
    parts = [_add_half_bf16(g, b, core, "rs_add_" + nm) for g, b, nm in zip(grads, theirs, names)]
    dh, recv = _inproj_bwd(du_conv, du_za, du_mla, w_t, parts)
    recv = [lax.dynamic_update_slice(r, lax.dynamic_slice(p, (chip, 0, 0), (1,) + p.shape[1:]), (chip, 0, 0))
            for r, p in zip(recv, parts)]
    halves = [_sum_chips(p, "rs_sum_" + nm) for p, nm in zip(recv, names)]
    joined = _rs_core_join(halves)
    joined = [lax.dynamic_update_slice(j, hf[None], (mc, 0, 0)) for j, hf in zip(joined, halves)]
    g_big = [j.reshape(2 * j.shape[1], j.shape[2]) for j in joined]
    grad_x, st_in = _prenorm_bwd(x2, dh, dout, norm_g, scale)

    sgrad = jnp.zeros((8, D_MODEL), F32)
    sgrad = sgrad.at[0:2].set(st_in[0:2])
    sgrad = sgrad.at[2].set(st_out[0])
    sgrad = sgrad.at[3].set(st_in[2])
    sgrad = sgrad.at[4, :D_CONV].set(dconv_w[0]).at[4, D_CONV:].set(dconv_w[1])
    sgrad = sgrad.at[5, :D_CONV].set(dconv_w[2]).at[5, D_CONV:].set(sg_mla[0, :D_CONV])
    sgrad = sgrad.at[6, :HEAD_PAD].set(sg_mla[0, SG_KG:SG_KG + HEAD_PAD])
    sgrad = sgrad.at[7].set(st_out[1])
    sg_all, sg_sum = _gather8(sgrad, "gather_small_grads", True)
    loss = sg_sum[7, 0]
    g_ada_b = sg_sum[0:3].reshape(1, 3 * D_MODEL)
    g_norm_g = sg_sum[3:4]
    conv_sum = jnp.stack([sg_sum[4, :D_CONV], sg_sum[4, D_CONV:], sg_sum[5, :D_CONV]])
    g_conv_w = lax.dynamic_slice(conv_sum, (0, chip * cw_cols), (3, cw_cols))
    g_q_a_g = sg_sum[5:6, D_CONV + SG_QAG:D_CONV + SG_QAG + Q_LORA]
    g_kv_a_g = sg_sum[5:6, D_CONV + SG_KVAG:D_CONV + SG_KVAG + KV_LORA]
    g_q_g = sg_sum[5:6, D_CONV + SG_QG:D_CONV + SG_QG + QK_HEAD]
    g_k_g = sg_sum[6:7, :QK_HEAD]
    dmod_k = lax.dynamic_slice(sg_all[:, 0:3, :].reshape(N_DEV, 3 * D_MODEL), (0, chip * ada_cols), (N_DEV, ada_cols))

    g_ada_w, d_ada_w, nm_ada_w, nv_ada_w = _ada_w_update(sc_all, dmod_k, ada_w[0], m_ada_w[0], v_ada_w[0])
    upd = {}
    big = {"w_q_b": (w_q_b, m_w_q_b, v_w_q_b), "w_kv_b": (w_kv_b, m_w_kv_b, v_w_kv_b), "w_out": (w_out, m_w_out, v_w_out)}
    for nm, g in zip(names[1:], g_big[1:]):
        w_, m_, v_ = big[nm]
        upd[nm] = (g,) + tuple(_adamw(w_[0], g, m_[0], v_[0], "adamw_" + nm))
    d_t, nm_t, nv_t, g_t = _adamw(w_in_t, g_big[0], m_w_in_t, v_w_in_t, "adamw_w_in", echo_g=True)
    upd["w_in"] = tuple(jnp.transpose(a) for a in (g_t, d_t, nm_t, nv_t))
    small_w = {"ada_b": (ada_b, m_ada_b, v_ada_b, g_ada_b), "norm_g": (norm_g, m_norm_g, v_norm_g, g_norm_g),
               "conv_w": (conv_w[0], m_conv_w[0], v_conv_w[0], g_conv_w), "q_a_g": (q_a_g, m_q_a_g, v_q_a_g, g_q_a_g),
               "kv_a_g": (kv_a_g, m_kv_a_g, v_kv_a_g, g_kv_a_g), "q_g": (q_g, m_q_g, v_q_g, g_q_g),
               "k_g": (k_g, m_k_g, v_k_g, g_k_g)}
    for nm, (w_, m_, v_, g) in small_w.items():
        upd[nm] = (g,) + tuple(_adamw(w_, g, m_, v_, "adamw_" + nm))
    upd["ada_w"] = (g_ada_w, d_ada_w, nm_ada_w, nv_ada_w)

    order = ["ada_w", "ada_b", "norm_g", "w_in", "conv_w", "q_a_g", "w_q_b", "kv_a_g", "w_kv_b", "q_g", "k_g", "w_out"]
    lead1 = {"ada_w", "w_in", "conv_w", "w_q_b", "w_kv_b", "w_out"}

    def shaped(nm, a):
        return a[None] if nm in lead1 else a

    outs = [loss, grad_x[None]]
    for idx in range(4):
        outs += [shaped(nm, upd[nm][idx]) for nm in order]
    return tuple(outs)
```

```python
import math

import jax
import jax.numpy as jnp
from jax import lax
from jax.experimental import pallas as pl
from jax.experimental.pallas import tpu as pltpu

F32 = jnp.float32
BF16 = jnp.bfloat16
MESH = pl.DeviceIdType.MESH
SDS = jax.ShapeDtypeStruct
ANY = pl.BlockSpec(memory_space=pl.ANY)

D_MODEL = 2048
D_CONV = 1024
N_HEADS = 8
QK_NOPE = 128
QK_ROPE = 64
QK_HEAD = QK_NOPE + QK_ROPE
V_HEAD = 128
D_ATTN = N_HEADS * V_HEAD
Q_LORA = 512
KV_LORA = 256
ROPE_BASE = 10000.0
EPS = 1e-6
LOG2E = math.log2(math.e)
LN2 = math.log(2.0)
ADAM_LR, ADAM_B1, ADAM_B2, ADAM_EPS, ADAM_WD, ADAM_STEP = 0.001, 0.9, 0.999, 1e-08, 0.01, 10
N_CHIPS = 4
N_DEV = 8

LANES = 128
V7X_VMEM_BYTES = 64 * 1024 * 1024
MIB = 1024 * 1024

HEAD_PAD = 256
Q_PAD = N_HEADS * HEAD_PAD
U_ZA = 4 * D_CONV
U_CQ = U_ZA + D_ATTN
U_CKV = U_CQ + Q_LORA
U_KR = U_CKV + KV_LORA
KR_PAD = 256
U_COLS = U_KR + KR_PAD
MLA_COLS = Q_LORA + KV_LORA + KR_PAD

ATT_T = 512
INPROJ_TM, INPROJ_TN = 1024, 1024
ROW_T = 512
MLA_PREP_T = 1024
OUT_T = 256
DH_TM, DH_TN = 512, 1024
TN_TM, TN_TN, TN_TK = 1024, 1024, 4096
ATT_UNROLL = 8


def _cp(sem=None, vmem_mib=None, **kw):
    if sem is not None:
        kw["dimension_semantics"] = sem
    if vmem_mib is not None:
        kw["vmem_limit_bytes"] = min(vmem_mib * MIB, V7X_VMEM_BYTES - 4 * MIB)
    return pltpu.CompilerParams(**kw)


def _sigmoid(z):
    return 1.0 / (1.0 + jnp.exp(-z))


def _silu_grad(z, sg):
    return sg * (1.0 + z * (1.0 - sg))


def _nt(a, b):
    return lax.dot_general(a, b, (((1,), (1,)), ((), ())), preferred_element_type=F32)


def _tn(a, b):
    return lax.dot_general(a, b, (((0,), (0,)), ((), ())), preferred_element_type=F32)


def _nn(a, b):
    return jnp.dot(a, b, preferred_element_type=F32)


def _place():
    return lax.axis_index("x"), lax.axis_index("y"), lax.axis_index("c")


def _gather8(v, name, with_sum):
    rows, cols = v.shape

    def body(v_ref, out_ref, *rest):
        if with_sum:
            sum_ref, send_sems, recv_sems = rest
        else:
            send_sems, recv_sems = rest
        mx, my, mc = _place()
        me = 4 * mx + 2 * my + mc
        out_ref[me] = v_ref[...]
        peers = []
        for d in range(1, N_DEV):
            px = 1 - mx if d & 4 else mx
            py = 1 - my if d & 2 else my
            pc = 1 - mc if d & 1 else mc
            peers.append((px, py, pc))

        def copy(d, slot, to):
            return pltpu.make_async_remote_copy(
                src_ref=v_ref, dst_ref=out_ref.at[slot], send_sem=send_sems.at[d], recv_sem=recv_sems.at[d],
                device_id=to, device_id_type=MESH)

        sends = [copy(d, me, p) for d, p in enumerate(peers)]
        for cp in sends:
            cp.start()
        for d, (px, py, pc) in enumerate(peers):
            copy(d, 4 * px + 2 * py + pc, (px, py, pc)).wait_recv()
        for cp in sends:
            cp.wait_send()
        if with_sum:
            acc = out_ref[0]
            for b in range(1, N_DEV):
                acc = acc + out_ref[b]
            sum_ref[...] = acc

    out_shape = [SDS((N_DEV, rows, cols), F32)]
    if with_sum:
        out_shape.append(SDS((rows, cols), F32))
    vm = pl.BlockSpec(memory_space=pltpu.VMEM)
    return pl.pallas_call(
        body, name=name, out_shape=out_shape, in_specs=[vm], out_specs=[vm] * len(out_shape),
        scratch_shapes=[pltpu.SemaphoreType.DMA((N_DEV - 1,)), pltpu.SemaphoreType.DMA((N_DEV - 1,))],
    )(v)


def _chips_of(mx, my):
    chips = [(mx, 1 - my), (1 - mx, my), (1 - mx, 1 - my)]
    return chips, [2 * px + py for px, py in chips]


def _prenorm_gather(x, norm_g, scale, shift, shard):
    seq, dm = x.shape
    tm = min(INPROJ_TM, seq)
    ni = seq // tm
    half_rows = shard.shape[0] // 2

    def body(x_ref, g_ref, sc_ref, sh_ref, shard_ref, h_ref, got_ref, s1, r1, s2, r2):
        i = pl.program_id(0)
        mx, my, mc = _place()
        k = 2 * mx + my
        sib = (mx, my, 1 - mc)
        chips, kks = _chips_of(mx, my)

        def half(slot, c):
            return got_ref.at[slot, pl.ds(c * half_rows, half_rows)]

        def over_ici(d, slot):
            return pltpu.make_async_remote_copy(
                src_ref=shard_ref.at[pl.ds(mc * half_rows, half_rows)], dst_ref=half(slot, mc), send_sem=s1.at[d],
                recv_sem=r1.at[d], device_id=(chips[d][0], chips[d][1], mc), device_id_type=MESH)

        def to_sibling(d, c):
            return pltpu.make_async_remote_copy(
                src_ref=half(kks[d], c), dst_ref=half(kks[d], c), send_sem=s2.at[d], recv_sem=r2.at[d],
                device_id=sib, device_id_type=MESH)

        @pl.when(i == 0)
        def _():
            for d in range(3):
                over_ici(d, k).start()

        xv = x_ref[...]
        r = lax.rsqrt(jnp.mean(xv * xv, axis=-1, keepdims=True) + EPS)
        h_ref[...] = ((xv * r * g_ref[...]) * (1.0 + sc_ref[...]) + sh_ref[...]).astype(BF16)

        @pl.when(i == ni - 1)
        def _():
            for d in range(3):
                over_ici(d, kks[d]).wait_recv()
                to_sibling(d, mc).start()
            for d in range(3):
                to_sibling(d, 1 - mc).wait_recv()
            for d in range(3):
                over_ici(d, k).wait_send()
                to_sibling(d, mc).wait_send()

    vec = pl.BlockSpec((1, dm), lambda i: (0, 0))
    row = pl.BlockSpec((tm, dm), lambda i: (i, 0))
    return pl.pallas_call(
        body, name="prenorm_gather", grid=(ni,), in_specs=[row, vec, vec, vec, ANY], out_specs=[row, ANY],
        out_shape=[SDS((seq, dm), BF16), SDS((N_CHIPS,) + shard.shape, shard.dtype)],
        scratch_shapes=[pltpu.SemaphoreType.DMA((3,))] * 4,
        compiler_params=_cp(("arbitrary",), 48),
    )(x, norm_g, scale, shift, shard)


def _rs_core_swap(grads):
    na = len(grads)
    halves = [g.shape[1] // 2 for g in grads]

    def body(*refs):
        ins, outs = refs[:na], refs[na:2 * na]
        ssem, rsem = refs[2 * na:]
        mx, my, mc = _place()
        sib = (mx, my, 1 - mc)
        sends = []
        for a in range(na):
            cp = pltpu.make_async_remote_copy(
                src_ref=ins[a].at[:, pl.ds((1 - mc) * halves[a], halves[a])], dst_ref=outs[a],
                send_sem=ssem.at[a], recv_sem=rsem.at[a], device_id=sib, device_id_type=MESH)
            cp.start()
            sends.append(cp)
        for cp in sends:
            cp.wait_recv()
        for cp in sends:
            cp.wait_send()

    return pl.pallas_call(
        body, name="rs_core_swap", out_shape=[SDS((N_CHIPS, h) + g.shape[2:], g.dtype) for g, h in zip(grads, halves)],
        in_specs=[ANY] * na, out_specs=[ANY] * na,
        scratch_shapes=[pltpu.SemaphoreType.DMA((na,))] * 2,
    )(*grads)


def _chip_exchange_copies(ins, outs, ssem, rsem):
    mx, my, mc = _place()
    k = 2 * mx + my
    chips, kks = _chips_of(mx, my)
    sends, recvs = [], []
    for a in range(len(ins)):
        for d, (px, py) in enumerate(chips):
            def copy(dst_slot):
                return pltpu.make_async_remote_copy(
                    src_ref=ins[a].at[kks[d]], dst_ref=outs[a].at[dst_slot], send_sem=ssem.at[3 * a + d],
                    recv_sem=rsem.at[3 * a + d], device_id=(px, py, mc), device_id_type=MESH)
            sends.append(copy(k))
            recvs.append(copy(kks[d]))
    return sends, recvs


def _rs_core_join(halves):
    na = len(halves)

    def body(*refs):
        ins, outs = refs[:na], refs[na:2 * na]
        ssem, rsem = refs[2 * na:]
        mx, my, mc = _place()
        sib = (mx, my, 1 - mc)
        sends = []
        for a in range(na):
            cp = pltpu.make_async_remote_copy(
                src_ref=ins[a], dst_ref=outs[a].at[mc], send_sem=ssem.at[a], recv_sem=rsem.at[a],
                device_id=sib, device_id_type=MESH)
            cp.start()
            sends.append(cp)
        for a in range(na):
            pltpu.make_async_remote_copy(
                src_ref=ins[a], dst_ref=outs[a].at[1 - mc], send_sem=ssem.at[a], recv_sem=rsem.at[a],
                device_id=sib, device_id_type=MESH).wait_recv()
        for cp in sends:
            cp.wait_send()

    return pl.pallas_call(
        body, name="rs_core_join", out_shape=[SDS((2,) + h.shape, h.dtype) for h in halves],
        in_specs=[ANY] * na, out_specs=[ANY] * na,
        scratch_shapes=[pltpu.SemaphoreType.DMA((na,))] * 2,
    )(*halves)


def _row_tile(rows, limit, mult=16):
    if rows <= limit:
        return rows
    best = None
    for t in range(mult, limit + 1, mult):
        if rows % t == 0:
            best = t
    assert best is not None, rows
    return best


def _add_half_bf16(g, b, core, name):
    _, h, cols = b.shape
    tb = _row_tile(h, 512)
    nb = h // tb

    def body(core_ref, g_ref, b_ref, o_ref):
        o_ref[...] = (g_ref[...].astype(F32) + b_ref[...].astype(F32)).astype(BF16)

    spec = pl.BlockSpec((None, tb, cols), lambda kk, i, core_ref: (kk, i, 0))
    return pl.pallas_call(
        body, name=name,
        grid_spec=pltpu.PrefetchScalarGridSpec(
            num_scalar_prefetch=1, grid=(N_CHIPS, nb),
            in_specs=[pl.BlockSpec((None, tb, cols), lambda kk, i, core_ref: (kk, core_ref[0] * nb + i, 0)), spec],
            out_specs=spec),
        out_shape=SDS(b.shape, BF16), compiler_params=_cp(("arbitrary", "arbitrary")),
    )(core, g, b)


def _sum_chips(p, name):
    _, rows, cols = p.shape
    tb = _row_tile(rows, 256)

    def body(p_ref, o_ref):
        acc = p_ref[0].astype(F32)
        for j in range(1, N_CHIPS):
            acc = acc + p_ref[j].astype(F32)
        o_ref[...] = acc

    return pl.pallas_call(
        body, name=name, grid=(rows // tb,),
        in_specs=[pl.BlockSpec((N_CHIPS, tb, cols), lambda i: (0, i, 0))],
        out_specs=pl.BlockSpec((tb, cols), lambda i: (i, 0)), out_shape=SDS((rows, cols), F32),
        compiler_params=_cp(("arbitrary",)),
    )(p)


def _adamw_math(w, g, m, v):
    m2 = ADAM_B1 * m + (1.0 - ADAM_B1) * g
    v2 = ADAM_B2 * v + (1.0 - ADAM_B2) * (g * g)
    m_hat = m2 / (1.0 - ADAM_B1 ** ADAM_STEP)
    v_hat = v2 / (1.0 - ADAM_B2 ** ADAM_STEP)
    delta = -ADAM_LR * (m_hat / (jnp.sqrt(v_hat) + ADAM_EPS) + ADAM_WD * w)
    return delta, m2, v2


def _adamw(w, g, m, v, name, echo_g=False):
    rows, cols = w.shape
    tb = _row_tile(rows, 256, mult=8)
    nout = 4 if echo_g else 3

    def body(w_ref, g_ref, m_ref, v_ref, d_ref, m2_ref, v2_ref, *echo):
        gv = g_ref[...]
        d, m2, v2 = _adamw_math(w_ref[...], gv, m_ref[...], v_ref[...])
        d_ref[...] = d
        m2_ref[...] = m2
        v2_ref[...] = v2
        if echo_g:
            echo[0][...] = gv

    spec = pl.BlockSpec((tb, cols), lambda i: (i, 0))
    return pl.pallas_call(
        body, name=name, grid=(rows // tb,), in_specs=[spec] * 4, out_specs=[spec] * nout,
        out_shape=[SDS((rows, cols), F32)] * nout, compiler_params=_cp(("arbitrary",), 40),
    )(w, g, m, v)


def _ada_w_update(sc_all, dmod_k, w, m, v):
    rows, cols = w.shape
    tb = 256

    def body(s_ref, dm_ref, w_ref, m_ref, v_ref, g_ref, d_ref, m2_ref, v2_ref):
        g = _tn(s_ref[...].astype(BF16), dm_ref[...].astype(BF16))
        d, m2, v2 = _adamw_math(w_ref[...], g, m_ref[...], v_ref[...])
        g_ref[...] = g
        d_ref[...] = d
        m2_ref[...] = m2
        v2_ref[...] = v2

    spec = pl.BlockSpec((tb, cols), lambda i: (i, 0))
    return pl.pallas_call(
        body, name="ada_w_update", grid=(rows // tb,),
        in_specs=[pl.BlockSpec((N_DEV, tb), lambda i: (0, i)), pl.BlockSpec((N_DEV, cols), lambda i: (0, 0)), spec, spec, spec],
        out_specs=[spec] * 4, out_shape=[SDS((rows, cols), F32)] * 4, compiler_params=_cp(("arbitrary",), 40),
    )(sc_all, dmod_k, w, m, v)


def _ada_mod(c_all, w, b_k):
    rows, cols = w.shape
    tn = 512

    def body(c_ref, w_ref, b_ref, o_ref, s_ref):
        cv = c_ref[...]
        s = cv * _sigmoid(cv)
        s_ref[...] = s
        o_ref[...] = _nn(s.astype(BF16), w_ref[...].astype(BF16)) + b_ref[...]

    return pl.pallas_call(
        body, name="ada_mod", grid=(cols // tn,),
        in_specs=[pl.BlockSpec((N_DEV, rows), lambda j: (0, 0)), pl.BlockSpec((rows, tn), lambda j: (0, j)),
                  pl.BlockSpec((1, tn), lambda j: (0, j))],
        out_specs=[pl.BlockSpec((N_DEV, tn), lambda j: (0, j)), pl.BlockSpec((N_DEV, rows), lambda j: (0, 0))],
        out_shape=[SDS((N_DEV, cols), F32), SDS((N_DEV, rows), F32)], compiler_params=_cp(("arbitrary",)),
    )(c_all, w, b_k)


def _inproj(h, w_t, shards):
    seq, dm = h.shape
    ncols = w_t.shape[0]
    tm, tn = min(INPROJ_TM, seq), INPROJ_TN
    ni, nj = seq // tm, ncols // tn
    na = len(shards)

    def body(h_ref, w_ref, *rest):
        shard_refs, u_ref, got_refs = rest[:na], rest[na], rest[na + 1:2 * na + 1]
        ssem, rsem = rest[2 * na + 1:]
        i, j = pl.program_id(0), pl.program_id(1)
        mx, my, mc = _place()
        k = 2 * mx + my
        chips, kks = _chips_of(mx, my)

        def copy(a, d, slot):
            return pltpu.make_async_remote_copy(
                src_ref=shard_refs[a], dst_ref=got_refs[a].at[slot], send_sem=ssem.at[3 * a + d],
                recv_sem=rsem.at[3 * a + d], device_id=(chips[d][0], chips[d][1], mc), device_id_type=MESH)

        @pl.when((i == 0) & (j == 0))
        def _():
            for a in range(na):
                for d in range(3):
                    copy(a, d, k).start()

        u_ref[...] = _nt(h_ref[...], w_ref[...]).astype(BF16)

        @pl.when((i == ni - 1) & (j == nj - 1))
        def _():
            for a in range(na):
                for d in range(3):
                    copy(a, d, kks[d]).wait_recv()
            for a in range(na):
                for d in range(3):
                    copy(a, d, k).wait_send()

    outs = pl.pallas_call(
        body, name="inproj", grid=(ni, nj),
        in_specs=[pl.BlockSpec((tm, dm), lambda i, j: (i, 0)), pl.BlockSpec((tn, dm), lambda i, j: (j, 0))] + [ANY] * na,
        out_specs=[pl.BlockSpec((tm, tn), lambda i, j: (i, j))] + [ANY] * na,
        out_shape=[SDS((seq, ncols), BF16)] + [SDS((N_CHIPS,) + s.shape, s.dtype) for s in shards],
        scratch_shapes=[pltpu.SemaphoreType.DMA((3 * na,))] * 2,
        compiler_params=_cp(("arbitrary", "arbitrary"), 48),
    )(h, w_t, *shards)
    return outs[0], outs[1:]


HALO = 16
CHUNK_ROWS, CHUNK_LANES = 32, 256


def _conv_fwd(u, conv_w):
    seq = u.shape[0]
    ts = min(ROW_T, seq)
    hb = ts // HALO

    def body(xc_ref, bc_ref, cc_ref, zc_ref, xp_ref, cp_ref, w_ref, y_ref, ext_ref):
        i = pl.program_id(0)
        up = cp_ref[...].astype(F32) * xp_ref[...].astype(F32)
        ext_ref[0:HALO, :] = jnp.where(i > 0, up, 0.0)
        ext_ref[HALO:HALO + ts, :] = cc_ref[...].astype(F32) * xc_ref[...].astype(F32)
        for r0 in range(0, ts, CHUNK_ROWS):
            rows = slice(r0, r0 + CHUNK_ROWS)
            for c0 in range(0, D_CONV, CHUNK_LANES):
                cols = slice(c0, c0 + CHUNK_LANES)
                uc = ext_ref[HALO + r0:HALO + r0 + CHUNK_ROWS, cols]
                u1 = ext_ref[HALO - 1 + r0:HALO - 1 + r0 + CHUNK_ROWS, cols]
                u2 = ext_ref[HALO - 2 + r0:HALO - 2 + r0 + CHUNK_ROWS, cols]
                conv = w_ref[0:1, cols] * u2 + w_ref[1:2, cols] * u1 + w_ref[2:3, cols] * uc
                z = zc_ref[rows, cols].astype(F32)
                y_ref[rows, cols] = ((bc_ref[rows, cols].astype(F32) * conv) * (z * _sigmoid(z))).astype(BF16)

    def col(cb):
        return pl.BlockSpec((ts, D_CONV), lambda i: (i, cb))

    def prev(cb):
        return pl.BlockSpec((HALO, D_CONV), lambda i: (jnp.maximum(i * hb - 1, 0), cb))

    return pl.pallas_call(
        body, name="conv_fwd", grid=(seq // ts,),
        in_specs=[col(0), col(1), col(2), col(3), prev(0), prev(2), pl.BlockSpec((3, D_CONV), lambda i: (0, 0))],
        out_specs=pl.BlockSpec((ts, D_CONV), lambda i: (i, 0)), out_shape=SDS((seq, D_CONV), BF16),
        scratch_shapes=[pltpu.VMEM((ts + HALO, D_CONV), F32)],
        compiler_params=_cp(("arbitrary",), 40),
    )(u, u, u, u, u, u, conv_w)


def _rope_tables(pos_ref, freq_ref):
    ang = pos_ref[...].astype(F32) * freq_ref[...]
    lane = lax.broadcasted_iota(jnp.int32, ang.shape, 1)
    cs, sn = jnp.cos(ang), jnp.sin(ang)
    half = QK_ROPE // 2
    cos_t = jnp.where(lane < QK_ROPE, cs, 0.0)
    sin_lo = jnp.where(lane < half, sn, 0.0)
    sin_hi = jnp.where((lane >= half) & (lane < QK_ROPE), sn, 0.0)
    return cos_t, sin_lo, sin_hi


def _rope(blk, tables):
    cos_t, sin_lo, sin_hi = tables
    half = QK_ROPE // 2
    return blk * cos_t - pltpu.roll(blk, LANES - half, 1) * sin_lo + pltpu.roll(blk, half, 1) * sin_hi


def _rope_bwd(g, tables):
    cos_t, sin_lo, sin_hi = tables
    half = QK_ROPE // 2
    return g * cos_t + pltpu.roll(g, LANES - half, 1) * sin_lo - pltpu.roll(g, half, 1) * sin_hi


def _rms(v, n):
    r = lax.rsqrt(jnp.sum(v * v, axis=-1, keepdims=True) * (1.0 / n) + EPS)
    return v * r, r


def _mla_prep(u, pos, freq, q_a_g, wq, kv_a_g, wkn, wv, q_g, k_g):
    seq = u.shape[0]
    ts = min(MLA_PREP_T, seq)
    qscale = LOG2E / math.sqrt(QK_HEAD)

    def body(cq_ref, ckv_ref, kr_ref, pos_ref, freq_ref, qag_ref, wq_ref, kvag_ref, wkn_ref, wv_ref, qg_ref, kg_ref,
             q_ref, k_ref, v_ref):
        tables = _rope_tables(pos_ref, freq_ref)
        cqn, _ = _rms(cq_ref[...].astype(F32), Q_LORA)
        qp = _nn((cqn * qag_ref[...]).astype(BF16), wq_ref[...])
        qg = qg_ref[...]
        for h in range(N_HEADS):
            lo = h * HEAD_PAD
            qn, _ = _rms(qp[:, lo:lo + HEAD_PAD], QK_HEAD)
            qn = qn * qg
            q_ref[:, lo:lo + LANES] = (qn[:, :LANES] * qscale).astype(BF16)
            q_ref[:, lo + LANES:lo + HEAD_PAD] = (_rope(qn[:, LANES:], tables) * qscale).astype(BF16)
        ckvn, _ = _rms(ckv_ref[...].astype(F32), KV_LORA)
        ckvb = (ckvn * kvag_ref[...]).astype(BF16)
        kn = _nn(ckvb, wkn_ref[...])
        v_ref[...] = _nn(ckvb, wv_ref[...]).astype(BF16)
        kr = kr_ref[:, 0:LANES].astype(F32)
        ssr = jnp.sum(kr * kr, axis=-1, keepdims=True)
        kg = kg_ref[...]
        for h in range(N_HEADS):
            knh = kn[:, h * QK_NOPE:(h + 1) * QK_NOPE]
            r = lax.rsqrt((jnp.sum(knh * knh, axis=-1, keepdims=True) + ssr) * (1.0 / QK_HEAD) + EPS)
            lo = h * HEAD_PAD
            k_ref[:, lo:lo + LANES] = (knh * r * kg[:, :LANES]).astype(BF16)
            k_ref[:, lo + LANES:lo + HEAD_PAD] = _rope(kr * r * kg[:, LANES:], tables).astype(BF16)

    def full(a):
        return pl.BlockSpec(a.shape, lambda i: (0,) * a.ndim)

    return pl.pallas_call(
        body, name="mla_prep", grid=(seq // ts,),
        in_specs=[pl.BlockSpec((ts, Q_LORA), lambda i: (i, U_CQ // Q_LORA)),
                  pl.BlockSpec((ts, KV_LORA), lambda i: (i, U_CKV // KV_LORA)),
                  pl.BlockSpec((ts, KR_PAD), lambda i: (i, U_KR // KR_PAD)),
                  pl.BlockSpec((ts, 1), lambda i: (i, 0)), full(freq), full(q_a_g), full(wq), full(kv_a_g), full(wkn),
                  full(wv), full(q_g), full(k_g)],
        out_specs=[pl.BlockSpec((ts, Q_PAD), lambda i: (i, 0)), pl.BlockSpec((ts, Q_PAD), lambda i: (i, 0)),
                   pl.BlockSpec((ts, D_ATTN), lambda i: (i, 0))],
        out_shape=[SDS((seq, Q_PAD), BF16), SDS((seq, Q_PAD), BF16), SDS((seq, D_ATTN), BF16)],
        compiler_params=_cp(("arbitrary",), 48),
    )(u, u, u, pos, freq, q_a_g, wq, kv_a_g, wkn, wv, q_g, k_g)


def _flash_fwd(q, k, v, u):
    seq = q.shape[0]
    t = min(ATT_T, seq)
    n = seq // t
    n_pairs = n * (n + 1) // 2
    za_blk = U_ZA // V_HEAD

    def body(q_ref, k_ref, v_ref, z_ref, o_ref, y_ref, lse_ref, s_a, s_b, top_a, top_b, m_all, l_all, acc_all):
        ones = jnp.ones((16, t), BF16)
        bufs = ((s_a, top_a), (s_b, top_b))

        def rows(i):
            return pl.ds(pl.multiple_of(i * t, t), t)

        def scores(i, j, buf):
            s_ref, top_ref = buf
            s = _nt(k_ref[rows(j), :], q_ref[rows(i), :])
            ahead = lax.broadcasted_iota(jnp.int32, (t, t), 0) - lax.broadcasted_iota(jnp.int32, (t, t), 1)
            s = jnp.where(ahead <= (i - j) * t, s, -jnp.inf)
            s_ref[...] = s
            top_ref[...] = jnp.max(s, axis=0, keepdims=True)

        def absorb(i, j, buf):
            s_ref, top_ref = buf
            first = j == 0
            m = jnp.where(first, -jnp.inf, m_all[i])
            l = jnp.where(first, 0.0, l_all[i])
            acc = jnp.where(first, 0.0, acc_all[i])
            m_new = jnp.maximum(m, top_ref[...])
            alpha = jnp.exp2(m - m_new)
            p = jnp.exp2((s_ref[...] - m_new).astype(BF16))
            m_all[i] = m_new
            l_all[i] = alpha * l + _nn(ones, p)[0:1, :]
            acc_all[i] = alpha * acc + _tn(v_ref[rows(j), :], p)

        def trip(width):
            def walk(_, pair):
                i, j = pair
                for w in range(width):
                    done = j == i
                    ni, nj = jnp.where(done, i + 1, i), jnp.where(done, 0, j + 1)
                    scores(jnp.minimum(ni, n - 1), nj, bufs[(w + 1) % 2])
                    absorb(i, j, bufs[w % 2])
                    i, j = ni, nj
                return i, j
            return walk

        scores(0, 0, bufs[0])
        pair = lax.fori_loop(0, n_pairs // ATT_UNROLL, trip(ATT_UNROLL), (jnp.int32(0), jnp.int32(0)))
        if n_pairs % ATT_UNROLL:
            trip(n_pairs % ATT_UNROLL)(0, pair)

        def finish(i, carry):
            l = l_all[i]
            o = (acc_all[i] * (1.0 / l)).T
            lse_ref[:, rows(i)] = m_all[i] + jnp.log2(l)
            o_ref[rows(i), :] = o.astype(BF16)
            z = z_ref[rows(i), :].astype(F32)
            y_ref[rows(i), :] = (o * (z * _sigmoid(z))).astype(BF16)
            return carry

        lax.fori_loop(0, n, finish, 0)

    def col(width, cb):
        return pl.BlockSpec((seq, width), lambda h: (0, cb + h))

    return pl.pallas_call(
        body, name="flash_fwd", grid=(N_HEADS,),
        in_specs=[col(HEAD_PAD, 0), col(HEAD_PAD, 0), col(V_HEAD, 0), col(V_HEAD, za_blk)],
        out_specs=[col(V_HEAD, 0), col(V_HEAD, 0), pl.BlockSpec((None, 1, seq), lambda h: (h, 0, 0))],
        out_shape=[SDS((seq, D_ATTN), BF16), SDS((seq, D_ATTN), BF16), SDS((N_HEADS, 1, seq), F32)],
        scratch_shapes=[pltpu.VMEM((t, t), F32)] * 2 + [pltpu.VMEM((1, t), F32)] * 2
                       + [pltpu.VMEM((n, 1, t), F32)] * 2 + [pltpu.VMEM((n, V_HEAD, t), F32)],
        compiler_params=_cp(("arbitrary",), 52),
    )(q, k, v, u)


def _outproj_loss(y_conv, y_attn, x, target, gate, w_out):
    seq, dm = x.shape
    ts = min(OUT_T, seq)
    n = seq // ts
    dmix = w_out.shape[0]

    def body(yc_ref, ya_ref, x_ref, t_ref, gate_ref, wo_hbm, dout_ref, dy_ref, dyc_ref, stats_ref, wo_ref, sem, acc_ref):
        i = pl.program_id(0)

        @pl.when(i == 0)
        def _():
            cp = pltpu.make_async_copy(wo_hbm, wo_ref, sem)
            cp.start()
            cp.wait()
            acc_ref[...] = jnp.zeros_like(acc_ref)

        y = _nn(yc_ref[...], wo_ref[0:D_CONV, :]) + _nn(ya_ref[...], wo_ref[D_CONV:dmix, :])
        gate_v = gate_ref[...]
        diff = (x_ref[...] + gate_v * y) - t_ref[...]
        dout = diff * (1.0 / dm)
        dout_ref[...] = dout
        acc_ref[0:8, :] += jnp.sum((dout * y).reshape(ts // 8, 8, dm), axis=0)
        acc_ref[8:16, :] += jnp.sum((diff * diff).reshape(ts // 8, 8, dm), axis=0)
        dy = (dout * gate_v).astype(BF16)
        dy_ref[...] = dy
        dyc_ref[...] = _nt(dy, wo_ref[...]).astype(BF16)

        @pl.when(i == n - 1)
        def _():
            stats_ref[...] = jnp.zeros_like(stats_ref)
            stats_ref[0:1, :] = jnp.sum(acc_ref[0:8, :], axis=0, keepdims=True)
            loss = jnp.sum(acc_ref[8:16, :]) * (0.5 / dm)
            stats_ref[1:2, :] = jnp.full((1, dm), loss, F32)

    row = pl.BlockSpec((ts, dm), lambda i: (i, 0))
    half = pl.BlockSpec((ts, D_CONV), lambda i: (i, 0))
    return pl.pallas_call(
        body, name="outproj_loss", grid=(n,),
        in_specs=[half, half, row, row, pl.BlockSpec((1, dm), lambda i: (0, 0)), ANY],
        out_specs=[row, row, pl.BlockSpec((ts, dmix), lambda i: (i, 0)), pl.BlockSpec((8, dm), lambda i: (0, 0))],
        out_shape=[SDS((seq, dm), F32), SDS((seq, dm), BF16), SDS((seq, dmix), BF16), SDS((8, dm), F32)],
        scratch_shapes=[pltpu.VMEM(w_out.shape, BF16), pltpu.SemaphoreType.DMA(()), pltpu.VMEM((16, dm), F32)],
        compiler_params=_cp(("arbitrary",), 52),
    )(y_conv, y_attn, x, target, gate, w_out)


def _matmul_tn(a, b, name):
    seq, m = a.shape
    n = b.shape[1]
    tm, tn, tk = min(TN_TM, m), min(TN_TN, n), min(TN_TK, seq)
    nk = seq // tk

    def body(a_ref, b_ref, o_ref, acc_ref):
        kk = pl.program_id(2)

        @pl.when(kk == 0)
        def _():
            acc_ref[...] = jnp.zeros_like(acc_ref)

        acc_ref[...] += _tn(a_ref[...], b_ref[...])

        @pl.when(kk == nk - 1)
        def _():
            o_ref[...] = acc_ref[...].astype(BF16)

    return pl.pallas_call(
        body, name=name, grid=(m // tm, n // tn, nk),
        in_specs=[pl.BlockSpec((tk, tm), lambda i, j, kk: (kk, i)), pl.BlockSpec((tk, tn), lambda i, j, kk: (kk, j))],
        out_specs=pl.BlockSpec((tm, tn), lambda i, j, kk: (i, j)), out_shape=SDS((m, n), BF16),
        scratch_shapes=[pltpu.VMEM((tm, tn), F32)],
        compiler_params=_cp(("arbitrary", "arbitrary", "arbitrary"), 52),
    )(a, b)


def _attn_gate_bwd(dycat, o, u):
    seq = o.shape[0]
    ts = min(ROW_T, seq)

    def body(dy_ref, o_ref, z_ref, dot_ref, dz_ref, dl_ref):
        dy = dy_ref[...].astype(F32)
        ov = o_ref[...].astype(F32)
        z = z_ref[...].astype(F32)
        sg = _sigmoid(z)
        do = dy * (z * sg)
        dz_ref[...] = (dy * ov * _silu_grad(z, sg)).astype(BF16)
        prod = do * ov
        ones = jnp.ones((8, V_HEAD), F32)
        for h in range(N_HEADS):
            cols = slice(h * V_HEAD, (h + 1) * V_HEAD)
            dot_ref[h] = do[:, cols].T.astype(BF16)
            rows = lax.dot_general(ones, prod[:, cols], (((1,), (1,)), ((), ())), precision=lax.Precision.HIGHEST,
                                   preferred_element_type=F32)
            dl_ref[h] = rows[0:1, :]

    blk = pl.BlockSpec((ts, D_ATTN), lambda i: (i, 0))
    return pl.pallas_call(
        body, name="attn_gate_bwd", grid=(seq // ts,),
        in_specs=[pl.BlockSpec((ts, D_ATTN), lambda i: (i, 1)), blk, pl.BlockSpec((ts, D_ATTN), lambda i: (i, U_ZA // D_ATTN))],
        out_specs=[pl.BlockSpec((N_HEADS, V_HEAD, ts), lambda i: (0, 0, i)), blk,
                   pl.BlockSpec((N_HEADS, 1, ts), lambda i: (0, 0, i))],
        out_shape=[SDS((N_HEADS, V_HEAD, seq), BF16), SDS((seq, D_ATTN), BF16), SDS((N_HEADS, 1, seq), F32)],
        compiler_params=_cp(("arbitrary",), 40),
    )(dycat, o, u)


def _flash_bwd(q, k, v, do_t, lse, delta):
    seq = q.shape[0]
    t = min(ATT_T, seq)
    n = seq // t
    n_pairs = n * (n + 1) // 2

    def body(k_ref, v_ref, q_ref, dot_ref, lse_ref, dl_ref, dq_ref, dk_ref, dv_ref, s_a, s_b, dp_a, dp_b, dq_acc, dk_acc,
             dvt_acc):
        bufs = ((s_a, dp_a), (s_b, dp_b))

        def rows(i):
            return pl.ds(pl.multiple_of(i * t, t), t)

        def products(j, i, buf):
            s_ref, dp_ref = buf
            s = _nt(k_ref[rows(j), :], q_ref[rows(i), :])
            ahead = lax.broadcasted_iota(jnp.int32, (t, t), 0) - lax.broadcasted_iota(jnp.int32, (t, t), 1)
            s_ref[...] = jnp.where(ahead <= (i - j) * t, s, -jnp.inf)
            dp_ref[...] = _nn(v_ref[rows(j), :], dot_ref[:, rows(i)])

        def absorb(j, i, buf):
            s_ref, dp_ref = buf
            first = i == j
            p = jnp.exp2((s_ref[...] - lse_ref[:, rows(i)]).astype(BF16))
            dvt = jnp.where(first, 0.0, dvt_acc[...]) + _nt(dot_ref[:, rows(i)], p)
            ds = p * (dp_ref[...] - dl_ref[:, rows(i)]).astype(BF16)
            dk = jnp.where(first, 0.0, dk_acc[...]) + _nn(ds, q_ref[rows(i), :])
            dq_acc[rows(i), :] += _tn(ds, k_ref[rows(j), :])
            dvt_acc[...] = dvt
            dk_acc[...] = dk
            dk_ref[rows(j), :] = dk.astype(BF16)
            dv_ref[rows(j), :] = dvt.T.astype(BF16)

        def trip(width):
            def walk(_, pair):
                j, i = pair
                for w in range(width):
                    done = i == n - 1
                    nj = jnp.where(done, j + 1, j)
                    ni = jnp.where(done, j + 1, i + 1)
                    absorb(j, i, bufs[w % 2])
                    products(jnp.minimum(nj, n - 1), jnp.minimum(ni, n - 1), bufs[(w + 1) % 2])
                    j, i = nj, ni
                return j, i
            return walk

        dq_acc[...] = jnp.zeros_like(dq_acc)
        products(0, 0, bufs[0])
        pair = lax.fori_loop(0, n_pairs // ATT_UNROLL, trip(ATT_UNROLL), (jnp.int32(0), jnp.int32(0)))
        if n_pairs % ATT_UNROLL:
            trip(n_pairs % ATT_UNROLL)(0, pair)

        def finish(i, carry):
            dq_ref[rows(i), :] = dq_acc[rows(i), :].astype(BF16)
            return carry

        lax.fori_loop(0, n, finish, 0)

    def col(width):
        return pl.BlockSpec((seq, width), lambda h: (0, h))

    row = pl.BlockSpec((None, 1, seq), lambda h: (h, 0, 0))
    return pl.pallas_call(
        body, name="flash_bwd", grid=(N_HEADS,),
        in_specs=[col(HEAD_PAD), col(V_HEAD), col(HEAD_PAD), pl.BlockSpec((None, V_HEAD, seq), lambda h: (h, 0, 0)), row, row],
        out_specs=[col(HEAD_PAD), col(HEAD_PAD), col(V_HEAD)],
        out_shape=[SDS((seq, Q_PAD), BF16), SDS((seq, Q_PAD), BF16), SDS((seq, D_ATTN), BF16)],
        scratch_shapes=[pltpu.VMEM((t, t), F32)] * 4
                       + [pltpu.VMEM((seq, HEAD_PAD), F32), pltpu.VMEM((t, HEAD_PAD), F32), pltpu.VMEM((V_HEAD, t), F32)],
        compiler_params=_cp(("arbitrary",), 60),
    )(k, v, q, do_t, lse, delta)


SG_QAG, SG_KVAG, SG_QG, SG_KG, SG_COLS = 0, Q_LORA, Q_LORA + KV_LORA, Q_LORA + KV_LORA + HEAD_PAD, D_MODEL


def _mla_bwd(dq, dk, dv, u, pos, freq, q_a_g, wq, kv_a_g, wkn, wv, q_g, k_g):
    seq = u.shape[0]
    ts = min(ROW_T, seq)
    n = seq // ts
    qscale = 1.0 / math.sqrt(QK_HEAD)

    def body(dq_ref, dk_ref, dv_ref, cq_ref, ckv_ref, kr_ref, pos_ref, freq_ref, qag_ref, wq_ref, kvag_ref, wkn_ref,
             wv_ref, qg_ref, kg_ref, du_ref, dwq_ref, dwkn_ref, dwv_ref, sg_ref, dqp_ref, dkn_ref):
        i = pl.program_id(0)

        @pl.when(i == 0)
        def _():
            dwq_ref[...] = jnp.zeros_like(dwq_ref)
            dwkn_ref[...] = jnp.zeros_like(dwkn_ref)
            dwv_ref[...] = jnp.zeros_like(dwv_ref)
            sg_ref[...] = jnp.zeros_like(sg_ref)

        tables = _rope_tables(pos_ref, freq_ref)

        cq = cq_ref[...].astype(F32)
        cqn, rq = _rms(cq, Q_LORA)
        qag = qag_ref[...]
        cqb = (cqn * qag).astype(BF16)
        qp = _nn(cqb, wq_ref[...])
        qg = qg_ref[...]
        dqg = jnp.zeros((1, HEAD_PAD), F32)
        for h in range(N_HEADS):
            lo = h * HEAD_PAD
            xn, r = _rms(qp[:, lo:lo + HEAD_PAD], QK_HEAD)
            g = jnp.concatenate([dq_ref[:, lo:lo + LANES].astype(F32),
                                 _rope_bwd(dq_ref[:, lo + LANES:lo + HEAD_PAD].astype(F32), tables)], axis=-1) * qscale
            dqg = dqg + jnp.sum(g * xn, axis=0, keepdims=True)
            gy = g * qg
            mean = jnp.sum(gy * xn, axis=-1, keepdims=True) * (1.0 / QK_HEAD)
            dqp_ref[:, lo:lo + HEAD_PAD] = (r * (gy - xn * mean)).astype(BF16)
        dqp = dqp_ref[...]
        dwq_ref[...] += _tn(cqb, dqp)
        dcqn = _nt(dqp, wq_ref[...])
        sg_ref[0:1, SG_QAG:SG_QAG + Q_LORA] += jnp.sum(dcqn * cqn, axis=0, keepdims=True)
        sg_ref[0:1, SG_QG:SG_QG + HEAD_PAD] += dqg
        gy = dcqn * qag
        mean = jnp.sum(gy * cqn, axis=-1, keepdims=True) * (1.0 / Q_LORA)
        du_ref[:, 0:Q_LORA] = (rq * (gy - cqn * mean)).astype(BF16)

        ckv = ckv_ref[...].astype(F32)
        ckvn, rkv = _rms(ckv, KV_LORA)
        kvag = kvag_ref[...]
        ckvb = (ckvn * kvag).astype(BF16)
        kn = _nn(ckvb, wkn_ref[...])
        kr = kr_ref[:, 0:LANES].astype(F32)
        ssr = jnp.sum(kr * kr, axis=-1, keepdims=True)
        kg = kg_ref[...]
        kg_n, kg_r = kg[:, :LANES] * LN2, kg[:, LANES:] * LN2
        dkg_n = jnp.zeros((1, LANES), F32)
        dkg_r = jnp.zeros((1, LANES), F32)
        dkr = jnp.zeros((ts, LANES), F32)
        for h in range(N_HEADS):
            knh = kn[:, h * QK_NOPE:(h + 1) * QK_NOPE]
            r = lax.rsqrt((jnp.sum(knh * knh, axis=-1, keepdims=True) + ssr) * (1.0 / QK_HEAD) + EPS)
            xn_n, xn_r = knh * r, kr * r
            lo = h * HEAD_PAD
            g_n = dk_ref[:, lo:lo + LANES].astype(F32)
            g_r = _rope_bwd(dk_ref[:, lo + LANES:lo + HEAD_PAD].astype(F32), tables)
            dkg_n = dkg_n + jnp.sum(g_n * xn_n, axis=0, keepdims=True)
            dkg_r = dkg_r + jnp.sum(g_r * xn_r, axis=0, keepdims=True)
            gy_n, gy_r = g_n * kg_n, g_r * kg_r
            mean = (jnp.sum(gy_n * xn_n, axis=-1, keepdims=True) + jnp.sum(gy_r * xn_r, axis=-1, keepdims=True)) * (1.0 / QK_HEAD)
            dkn_ref[:, h * QK_NOPE:(h + 1) * QK_NOPE] = (r * (gy_n - xn_n * mean)).astype(BF16)
            dkr = dkr + r * (gy_r - xn_r * mean)
        dkn = dkn_ref[...]
        dvv = dv_ref[...]
        dwkn_ref[...] += _tn(ckvb, dkn)
        dwv_ref[...] += _tn(ckvb, dvv)
        dckvn = _nt(dkn, wkn_ref[...]) + _nt(dvv, wv_ref[...])
        sg_ref[0:1, SG_KVAG:SG_KVAG + KV_LORA] += jnp.sum(dckvn * ckvn, axis=0, keepdims=True)
        sg_ref[0:1, SG_KG:SG_KG + LANES] += dkg_n * LN2
        sg_ref[0:1, SG_KG + LANES:SG_KG + HEAD_PAD] += dkg_r * LN2
        gy = dckvn * kvag
        mean = jnp.sum(gy * ckvn, axis=-1, keepdims=True) * (1.0 / KV_LORA)
        du_ref[:, Q_LORA:Q_LORA + KV_LORA] = (rkv * (gy - ckvn * mean)).astype(BF16)
        du_ref[:, Q_LORA + KV_LORA:Q_LORA + KV_LORA + LANES] = dkr.astype(BF16)
        du_ref[:, Q_LORA + KV_LORA + LANES:MLA_COLS] = jnp.zeros((ts, LANES), BF16)

    def full(a):
        return pl.BlockSpec(a.shape, lambda i: (0,) * a.ndim)

    wide = pl.BlockSpec((ts, Q_PAD), lambda i: (i, 0))
    return pl.pallas_call(
        body, name="mla_bwd", grid=(n,),
        in_specs=[wide, wide, pl.BlockSpec((ts, D_ATTN), lambda i: (i, 0)),
                  pl.BlockSpec((ts, Q_LORA), lambda i: (i, U_CQ // Q_LORA)),
                  pl.BlockSpec((ts, KV_LORA), lambda i: (i, U_CKV // KV_LORA)),
                  pl.BlockSpec((ts, KR_PAD), lambda i: (i, U_KR // KR_PAD)),
                  pl.BlockSpec((ts, 1), lambda i: (i, 0)), full(freq), full(q_a_g), full(wq), full(kv_a_g), full(wkn),
                  full(wv), full(q_g), full(k_g)],
        out_specs=[pl.BlockSpec((ts, MLA_COLS), lambda i: (i, 0)), pl.BlockSpec((Q_LORA, Q_PAD), lambda i: (0, 0)),
                   pl.BlockSpec((KV_LORA, D_ATTN), lambda i: (0, 0)), pl.BlockSpec((KV_LORA, D_ATTN), lambda i: (0, 0)),
                   pl.BlockSpec((8, SG_COLS), lambda i: (0, 0))],
        out_shape=[SDS((seq, MLA_COLS), BF16), SDS((Q_LORA, Q_PAD), F32), SDS((KV_LORA, D_ATTN), F32),
                   SDS((KV_LORA, D_ATTN), F32), SDS((8, SG_COLS), F32)],
        scratch_shapes=[pltpu.VMEM((ts, Q_PAD), BF16), pltpu.VMEM((ts, D_ATTN), BF16)],
        compiler_params=_cp(("arbitrary",), 56),
    )(dq, dk, dv, u, u, u, pos, freq, q_a_g, wq, kv_a_g, wkn, wv, q_g, k_g)


def _conv_bwd(dycat, u, conv_w):
    seq = u.shape[0]
    ts = min(ROW_T, seq)
    n = seq // ts
    hb = ts // HALO

    def body(dy_ref, xc_ref, bc_ref, cc_ref, zc_ref, xp_ref, cp_ref, dyn_ref, bn_ref, zn_ref, w_ref,
             du_ref, dw_ref, ext_ref, dext_ref, acc_ref):
        i = pl.program_id(0)

        @pl.when(i == 0)
        def _():
            dw_ref[...] = jnp.zeros_like(dw_ref)

        up = cp_ref[...].astype(F32) * xp_ref[...].astype(F32)
        ext_ref[0:HALO, :] = jnp.where(i > 0, up, 0.0)
        ext_ref[HALO:HALO + ts, :] = cc_ref[...].astype(F32) * xc_ref[...].astype(F32)
        zn = zn_ref[...].astype(F32)
        dnext = dyn_ref[...].astype(F32) * (zn * _sigmoid(zn)) * bn_ref[...].astype(F32)
        dext_ref[ts:ts + HALO, :] = jnp.where(i < n - 1, dnext, 0.0)
        acc_ref[...] = jnp.zeros_like(acc_ref)

        for r0 in range(0, ts, CHUNK_ROWS):
            rows = slice(r0, r0 + CHUNK_ROWS)
            for c0 in range(0, D_CONV, CHUNK_LANES):
                cols = slice(c0, c0 + CHUNK_LANES)
                uc = ext_ref[HALO + r0:HALO + r0 + CHUNK_ROWS, cols]
                u1 = ext_ref[HALO - 1 + r0:HALO - 1 + r0 + CHUNK_ROWS, cols]
                u2 = ext_ref[HALO - 2 + r0:HALO - 2 + r0 + CHUNK_ROWS, cols]
                conv = w_ref[0:1, cols] * u2 + w_ref[1:2, cols] * u1 + w_ref[2:3, cols] * uc
                z = zc_ref[rows, cols].astype(F32)
                sg = _sigmoid(z)
                sz = z * sg
                b = bc_ref[rows, cols].astype(F32)
                dy = dy_ref[rows, cols].astype(F32)
                du_ref[rows, 3 * D_CONV + c0:3 * D_CONV + c0 + CHUNK_LANES] = (dy * (b * conv) * _silu_grad(z, sg)).astype(BF16)
                du_ref[rows, D_CONV + c0:D_CONV + c0 + CHUNK_LANES] = (dy * sz * conv).astype(BF16)
                dconv = dy * sz * b
                dext_ref[rows, cols] = dconv
                acc_ref[0:CHUNK_ROWS, cols] += dconv * u2
                acc_ref[CHUNK_ROWS:2 * CHUNK_ROWS, cols] += dconv * u1
                acc_ref[2 * CHUNK_ROWS:3 * CHUNK_ROWS, cols] += dconv * uc
        for r0 in range(0, ts, CHUNK_ROWS):
            rows = slice(r0, r0 + CHUNK_ROWS)
            for c0 in range(0, D_CONV, CHUNK_LANES):
                cols = slice(c0, c0 + CHUNK_LANES)
                du = (w_ref[2:3, cols] * dext_ref[rows, cols] + w_ref[1:2, cols] * dext_ref[r0 + 1:r0 + 1 + CHUNK_ROWS, cols]
                      + w_ref[0:1, cols] * dext_ref[r0 + 2:r0 + 2 + CHUNK_ROWS, cols])
                du_ref[rows, 2 * D_CONV + c0:2 * D_CONV + c0 + CHUNK_LANES] = (du * xc_ref[rows, cols].astype(F32)).astype(BF16)
                du_ref[rows, c0:c0 + CHUNK_LANES] = (du * cc_ref[rows, cols].astype(F32)).astype(BF16)
        for k in range(3):
            dw_ref[k:k + 1, :] += jnp.sum(acc_ref[k * CHUNK_ROWS:(k + 1) * CHUNK_ROWS, :], axis=0, keepdims=True)

    def col(cb):
        return pl.BlockSpec((ts, D_CONV), lambda i: (i, cb))

    def prev(cb):
        return pl.BlockSpec((HALO, D_CONV), lambda i: (jnp.maximum(i * hb - 1, 0), cb))

    def nxt(cb):
        return pl.BlockSpec((HALO, D_CONV), lambda i: (jnp.minimum((i + 1) * hb, n * hb - 1), cb))

    return pl.pallas_call(
        body, name="conv_bwd", grid=(n,),
        in_specs=[col(0), col(0), col(1), col(2), col(3), prev(0), prev(2), nxt(0), nxt(1), nxt(3),
                  pl.BlockSpec((3, D_CONV), lambda i: (0, 0))],
        out_specs=[pl.BlockSpec((ts, 4 * D_CONV), lambda i: (i, 0)), pl.BlockSpec((8, D_CONV), lambda i: (0, 0))],
        out_shape=[SDS((seq, 4 * D_CONV), BF16), SDS((8, D_CONV), F32)],
        scratch_shapes=[pltpu.VMEM((ts + HALO, D_CONV), F32), pltpu.VMEM((ts + HALO, D_CONV), F32),
                        pltpu.VMEM((3 * CHUNK_ROWS, D_CONV), F32)],
        compiler_params=_cp(("arbitrary",), 48),
    )(dycat, u, u, u, u, u, u, dycat, u, u, conv_w)


def _inproj_bwd(du_conv, du_za, du_mla, w_t, parts):
    seq = du_conv.shape[0]
    dm = w_t.shape[1]
    tm, tn = min(DH_TM, seq), DH_TN
    ni, nj = seq // tm, dm // tn
    na = len(parts)

    def body(dc_ref, dz_ref, dm_ref, w_ref, *rest):
        part_refs, o_ref, recv_refs = rest[:na], rest[na], rest[na + 1:2 * na + 1]
        ssem, rsem = rest[2 * na + 1:]
        i, j = pl.program_id(0), pl.program_id(1)
        sends, recvs = _chip_exchange_copies(part_refs, recv_refs, ssem, rsem)

        @pl.when((i == 0) & (j == 0))
        def _():
            for cp in sends:
                cp.start()

        acc = _nn(dc_ref[...], w_ref[0:U_ZA, :])
        acc = acc + _nn(dz_ref[...], w_ref[U_ZA:U_CQ, :])
        acc = acc + _nn(dm_ref[...], w_ref[U_CQ:U_COLS, :])
        o_ref[...] = acc.astype(BF16)

        @pl.when((i == ni - 1) & (j == nj - 1))
        def _():
            for cp in recvs:
                cp.wait_recv()
            for cp in sends:
                cp.wait_send()

    outs = pl.pallas_call(
        body, name="inproj_bwd", grid=(ni, nj),
        in_specs=[pl.BlockSpec((tm, U_ZA), lambda i, j: (i, 0)), pl.BlockSpec((tm, D_ATTN), lambda i, j: (i, 0)),
                  pl.BlockSpec((tm, MLA_COLS), lambda i, j: (i, 0)), pl.BlockSpec((U_COLS, tn), lambda i, j: (0, j))]
                 + [ANY] * na,
        out_specs=[pl.BlockSpec((tm, tn), lambda i, j: (i, j))] + [ANY] * na,
        out_shape=[SDS((seq, dm), BF16)] + [SDS(p.shape, p.dtype) for p in parts],
        scratch_shapes=[pltpu.SemaphoreType.DMA((3 * na,))] * 2,
        compiler_params=_cp(("arbitrary", "arbitrary"), 48),
    )(du_conv, du_za, du_mla, w_t, *parts)
    return outs[0], outs[1:]


def _prenorm_bwd(x, dh, dout, norm_g, scale):
    seq, dm = x.shape
    ts = min(ROW_T, seq)
    n = seq // ts

    def body(x_ref, dh_ref, dout_ref, g_ref, sc_ref, gx_ref, st_ref, acc_ref):
        i = pl.program_id(0)

        @pl.when(i == 0)
        def _():
            acc_ref[...] = jnp.zeros_like(acc_ref)

        xv = x_ref[...]
        xn, r = _rms(xv, dm)
        dh_v = dh_ref[...].astype(F32)
        gv = g_ref[...]
        one_sc = 1.0 + sc_ref[...]

        def fold(a):
            return jnp.sum(a.reshape(ts // 8, 8, dm), axis=0)

        acc_ref[0:8, :] += fold(dh_v)
        acc_ref[8:16, :] += fold(dh_v * (xn * gv))
        dxg = dh_v * one_sc
        acc_ref[16:24, :] += fold(dxg * xn)
        dxn = dxg * gv
        mean = jnp.sum(dxn * xn, axis=-1, keepdims=True) * (1.0 / dm)
        gx_ref[...] = dout_ref[...] + r * (dxn - xn * mean)

        @pl.when(i == n - 1)
        def _():
            st_ref[...] = jnp.zeros_like(st_ref)
            for k in range(3):
                st_ref[k:k + 1, :] = jnp.sum(acc_ref[8 * k:8 * k + 8, :], axis=0, keepdims=True)

    row = pl.BlockSpec((ts, dm), lambda i: (i, 0))
    vec = pl.BlockSpec((1, dm), lambda i: (0, 0))
    return pl.pallas_call(
        body, name="prenorm_bwd", grid=(n,), in_specs=[row, row, row, vec, vec],
        out_specs=[row, pl.BlockSpec((8, dm), lambda i: (0, 0))],
        out_shape=[SDS((seq, dm), F32), SDS((8, dm), F32)],
        scratch_shapes=[pltpu.VMEM((24, dm), F32)], input_output_aliases={2: 0},
        compiler_params=_cp(("arbitrary",), 52),
    )(x, dh, dout, norm_g, scale)


def _unshard_cols(g):
    return jnp.transpose(g, (1, 0, 2)).reshape(g.shape[1], -1)


def _shard_cols(w):
    r = w.shape[0]
    return jnp.transpose(w.reshape(r, N_CHIPS, -1), (1, 0, 2))


W_IN_COLS = 4 * D_CONV + Q_LORA + KV_LORA + QK_ROPE + D_ATTN
SHARD_ROWS = W_IN_COLS // N_CHIPS
SHARD_PAD = 1536


def _w_in_pieces():
    c4 = 4 * D_CONV
    groups = [(0, c4, 0), (c4, c4 + Q_LORA, U_CQ), (c4 + Q_LORA, c4 + Q_LORA + KV_LORA, U_CKV),
              (c4 + Q_LORA + KV_LORA, W_IN_COLS - D_ATTN, U_KR), (W_IN_COLS - D_ATTN, W_IN_COLS, U_ZA)]
    pieces = []
    for lo, hi, my in groups:
        for chip in range(N_CHIPS):
            a, b = max(lo, chip * SHARD_ROWS), min(hi, (chip + 1) * SHARD_ROWS)
            if a < b:
                pieces.append((chip, a - chip * SHARD_ROWS, b - a, my + a - lo))
    return pieces


def _w_t_to_my(g):
    w = jnp.zeros((U_COLS, g.shape[2]), g.dtype)
    for chip, row, n, my in _w_in_pieces():
        w = lax.dynamic_update_slice(w, g[chip, row:row + n], (my, 0))
    return w


def _w_t_from_my(g_conv, g_za, g_mla):
    w = jnp.zeros((N_CHIPS, SHARD_PAD, g_conv.shape[1]), g_conv.dtype)
    for chip, row, n, my in _w_in_pieces():
        src, base = (g_conv, 0) if my < U_ZA else (g_za, U_ZA) if my < U_CQ else (g_mla, U_CQ)
        w = lax.dynamic_update_slice(w, src[my - base:my - base + n][None], (chip, row, 0))
    return w


def _heads_pad(w):
    r = w.shape[0]
    w3 = w.reshape(r, N_HEADS, QK_HEAD)
    return jnp.pad(w3, ((0, 0), (0, 0), (0, HEAD_PAD - QK_HEAD))).reshape(r, Q_PAD)


def _heads_unpad(w):
    r = w.shape[0]
    return w.reshape(r, N_HEADS, HEAD_PAD)[:, :, :QK_HEAD].reshape(r, N_HEADS * QK_HEAD)


def kernel(x, c, positions, ada_w, ada_b, norm_g, w_in, conv_w, q_a_g, w_q_b, kv_a_g, w_kv_b, q_g, k_g, w_out, loss_target, m_ada_w, m_ada_b, m_norm_g, m_w_in, m_conv_w, m_q_a_g, m_w_q_b, m_kv_a_g, m_w_kv_b, m_q_g, m_k_g, m_w_out, v_ada_w, v_ada_b, v_norm_g, v_w_in, v_conv_w, v_q_a_g, v_w_q_b, v_kv_a_g, v_w_kv_b, v_q_g, v_k_g, v_w_out):
    mx, my, mc = _place()
    chip = 2 * mx + my
    me = 2 * chip + mc
    seq = x.shape[1]
    x2, t2 = x[0], loss_target[0]
    cw_cols = conv_w.shape[2]

    small = jnp.zeros((8, D_MODEL), F32)
    small = small.at[0].set(c[0])
    small = small.at[1:4, :cw_cols].set(conv_w[0])
    small_all = _gather8(small, "gather_c_conv", False)[0]
    c_all = small_all[:, 0, :]
    conv_full = jnp.transpose(small_all.reshape(N_CHIPS, 2, 8, D_MODEL)[:, 0, 1:4, :cw_cols], (1, 0, 2)).reshape(3, D_CONV)

    ada_cols = ada_w.shape[2]
    b_k = lax.dynamic_slice(ada_b, (0, chip * ada_cols), (1, ada_cols))
    mod_k, sc_all = _ada_mod(c_all, ada_w[0], b_k)
    mod_all = _gather8(mod_k, "gather_mod", False)[0]
    mod_row = lax.dynamic_slice(mod_all.reshape(N_CHIPS, 2, N_DEV, ada_cols), (0, mc, me, 0), (N_CHIPS, 1, 1, ada_cols))
    mod_row = mod_row.reshape(3, D_MODEL)
    shift, scale, gate = mod_row[0:1], mod_row[1:2], mod_row[2:3]

    def own_slot(g, s):
        return lax.dynamic_update_slice(g, s[None], (chip, 0, 0))

    w_in_t, m_w_in_t, v_w_in_t = [jnp.transpose(a[0]) for a in (w_in, m_w_in, v_w_in)]
    shard_in = jnp.pad(w_in_t.astype(BF16), ((0, SHARD_PAD - SHARD_ROWS), (0, 0)))
    h, g_in = _prenorm_gather(x2, norm_g, scale, shift, shard_in)
    g_in = own_slot(g_in, shard_in)
    w_t = _w_t_to_my(g_in)

    later = [w_q_b[0].astype(BF16), w_kv_b[0].astype(BF16), w_out[0].astype(BF16)]
    u, got = _inproj(h, w_t, later)
    g_q, g_kv, g_out = [own_slot(g, s) for g, s in zip(got, later)]
    wq = _heads_pad(_unshard_cols(g_q))
    wkv = _unshard_cols(g_kv).reshape(KV_LORA, N_HEADS, QK_NOPE + V_HEAD)
    wkn = wkv[:, :, :QK_NOPE].reshape(KV_LORA, N_HEADS * QK_NOPE)
    wv = wkv[:, :, QK_NOPE:].reshape(KV_LORA, D_ATTN)
    wo = g_out.reshape(N_CHIPS * g_out.shape[1], D_MODEL)
    y_conv = _conv_fwd(u, conv_full)
    pos = positions.reshape(seq, 1)
    inv_freq = ROPE_BASE ** (-jnp.arange(0, QK_ROPE, 2, dtype=F32) / QK_ROPE)
    freq = jnp.concatenate([inv_freq, inv_freq, jnp.zeros((LANES - QK_ROPE,), F32)]).reshape(1, LANES)
    q_g_pad = jnp.pad(q_g, ((0, 0), (0, HEAD_PAD - QK_HEAD)))
    k_g_pad = jnp.pad(k_g, ((0, 0), (0, HEAD_PAD - QK_HEAD)))
    q, k, v = _mla_prep(u, pos, freq, q_a_g, wq, kv_a_g, wkn, wv, q_g_pad, k_g_pad)
    o, y_attn, lse = _flash_fwd(q, k, v, u)
    dout, dy, dycat, st_out = _outproj_loss(y_conv, y_attn, x2, t2, gate, wo)

    dw_out = jnp.concatenate([_matmul_tn(y_conv, dy, "dw_out_conv"), _matmul_tn(y_attn, dy, "dw_out_attn")], axis=0)
    do_t, du_za, delta = _attn_gate_bwd(dycat, o, u)
    dq, dk, dv = _flash_bwd(q, k, v, do_t, lse, delta)
    du_mla, dwq, dwkn, dwv, sg_mla = _mla_bwd(dq, dk, dv, u, pos, freq, q_a_g, wq, kv_a_g, wkn, wv, q_g_pad, k_g_pad)
    du_conv, dconv_w = _conv_bwd(dycat, u, conv_full)
    dw_conv = _matmul_tn(du_conv, h, "dw_in_conv")
    dw_za = _matmul_tn(du_za, h, "dw_in_za")
    dw_mla = _matmul_tn(du_mla, h, "dw_in_mla")

    dw_q_nat = _heads_unpad(dwq).astype(BF16)
    dw_kv_nat = jnp.concatenate([dwkn.reshape(KV_LORA, N_HEADS, QK_NOPE), dwv.reshape(KV_LORA, N_HEADS, V_HEAD)],
                                axis=2).reshape(KV_LORA, N_HEADS * (QK_NOPE + V_HEAD)).astype(BF16)
    grads = [_w_t_from_my(dw_conv, dw_za, dw_mla), _shard_cols(dw_q_nat), _shard_cols(dw_kv_nat),
             dw_out.reshape(N_CHIPS, dw_out.shape[0] // N_CHIPS, D_MODEL)]
    theirs = _rs_core_swap(grads)
    names = ["w_in", "w_q_b", "w_kv_b", "w_out"]
    core = jnp.reshape(mc, (1,)).astype(jnp.int32)
    parts = [_add_half_bf16(g, b, core, "rs_add_" + nm) for g, b, nm in zip(grads, theirs, names)]
    dh, recv = _inproj_bwd(du_conv, du_za, du_mla, w_t, parts)
    recv = [lax.dynamic_update_slice(r, lax.dynamic_slice(p, (chip, 0, 0), (1,) + p.shape[1:]), (chip, 0, 0))
            for r, p in zip(recv, parts)]
    halves = [_sum_chips(p, "rs_sum_" + nm) for p, nm in zip(recv, names)]
    joined = _rs_core_join(halves)
    joined = [lax.dynamic_update_slice(j, hf[None], (mc, 0, 0)) for j, hf in zip(joined, halves)]
    g_big = [j.reshape(2 * j.shape[1], j.shape[2]) for j in joined]
    grad_x, st_in = _prenorm_bwd(x2, dh, dout, norm_g, scale)

    sgrad = jnp.zeros((8, D_MODEL), F32)
    sgrad = sgrad.at[0:2].set(st_in[0:2])
    sgrad = sgrad.at[2].set(st_out[0])
    sgrad = sgrad.at[3].set(st_in[2])
    sgrad = sgrad.at[4, :D_CONV].set(dconv_w[0]).at[4, D_CONV:].set(dconv_w[1])
    sgrad = sgrad.at[5, :D_CONV].set(dconv_w[2]).at[5, D_CONV:].set(sg_mla[0, :D_CONV])
    sgrad = sgrad.at[6, :HEAD_PAD].set(sg_mla[0, SG_KG:SG_KG + HEAD_PAD])
    sgrad = sgrad.at[7].set(st_out[1])
    sg_all, sg_sum = _gather8(sgrad, "gather_small_grads", True)
    loss = sg_sum[7, 0]
    g_ada_b = sg_sum[0:3].reshape(1, 3 * D_MODEL)
    g_norm_g = sg_sum[3:4]
    conv_sum = jnp.stack([sg_sum[4, :D_CONV], sg_sum[4, D_CONV:], sg_sum[5, :D_CONV]])
    g_conv_w = lax.dynamic_slice(conv_sum, (0, chip * cw_cols), (3, cw_cols))
    g_q_a_g = sg_sum[5:6, D_CONV + SG_QAG:D_CONV + SG_QAG + Q_LORA]
    g_kv_a_g = sg_sum[5:6, D_CONV + SG_KVAG:D_CONV + SG_KVAG + KV_LORA]
    g_q_g = sg_sum[5:6, D_CONV + SG_QG:D_CONV + SG_QG + QK_HEAD]
    g_k_g = sg_sum[6:7, :QK_HEAD]
    dmod_k = lax.dynamic_slice(sg_all[:, 0:3, :].reshape(N_DEV, 3 * D_MODEL), (0, chip * ada_cols), (N_DEV, ada_cols))

    g_ada_w, d_ada_w, nm_ada_w, nv_ada_w = _ada_w_update(sc_all, dmod_k, ada_w[0], m_ada_w[0], v_ada_w[0])
    upd = {}
    big = {"w_q_b": (w_q_b, m_w_q_b, v_w_q_b), "w_kv_b": (w_kv_b, m_w_kv_b, v_w_kv_b), "w_out": (w_out, m_w_out, v_w_out)}
    for nm, g in zip(names[1:], g_big[1:]):
        w_, m_, v_ = big[nm]
        upd[nm] = (g,) + tuple(_adamw(w_[0], g, m_[0], v_[0], "adamw_" + nm))
    d_t, nm_t, nv_t, g_t = _adamw(w_in_t, g_big[0], m_w_in_t, v_w_in_t, "adamw_w_in", echo_g=True)
    upd["w_in"] = tuple(jnp.transpose(a) for a in (g_t, d_t, nm_t, nv_t))
    small_w = {"ada_b": (ada_b, m_ada_b, v_ada_b, g_ada_b), "norm_g": (norm_g, m_norm_g, v_norm_g, g_norm_g),
               "conv_w": (conv_w[0], m_conv_w[0], v_conv_w[0], g_conv_w), "q_a_g": (q_a_g, m_q_a_g, v_q_a_g, g_q_a_g),
               "kv_a_g": (kv_a_g, m_kv_a_g, v_kv_a_g, g_kv_a_g), "q_g": (q_g, m_q_g, v_q_g, g_q_g),
               "k_g": (k_g, m_k_g, v_k_g, g_k_g)}
    for nm, (w_, m_, v_, g) in small_w.items():
        upd[nm] = (g,) + tuple(_adamw(w_, g, m_, v_, "adamw_" + nm))
    upd["ada_w"] = (g_ada_w, d_ada_w, nm_ada_w, nv_ada_w)

    order = ["ada_w", "ada_b", "norm_g", "w_in", "conv_w", "q_a_g", "w_q_b", "kv_a_g", "w_kv_b", "q_g", "k_g", "w_out"]
    lead1 = {"ada_w", "w_in", "conv_w", "w_q_b", "w_kv_b", "w_out"}

    def shaped(nm, a):
        return a[None] if nm in lead1 else a

    outs = [loss, grad_x[None]]
    for idx in range(4):
        outs += [shaped(nm, upd[nm][idx]) for nm in order]
    return tuple(outs)
```
